```python
import math
import jax
import jax.numpy as jnp
from jax import lax
import numpy as np

D_MODEL = 1024
BATCH = 2
SEQ = 8192
DEPTH = 2
DEC_BATCH = 128
DEC_SEQ = 4
PAST_LEN = 16384
PAGE_SIZE = 128

N_EVEN = (DEPTH + 1) // 2
N_ODD = DEPTH // 2
MIX_W = D_MODEL
EPS = 1e-6

RWKV_HD = 64
RWKV_W = MIX_W // 2
RWKV_HEADS = RWKV_W // RWKV_HD
RWKV_LR_W = 64
RWKV_LR_A = 64
RWKV_LR_G = 128
RWKV_GN_EPS = 64e-5
SHIFT_W = 3 * RWKV_W + RWKV_LR_W + RWKV_LR_A + RWKV_LR_G

GLA_HEADS = 4
GLA_DV_TOT = MIX_W - RWKV_W
GLA_DK_TOT = GLA_DV_TOT // 2
GLA_DK = GLA_DK_TOT // GLA_HEADS
GLA_DV = GLA_DV_TOT // GLA_HEADS
GLA_LR = 16
GLA_GATE_NORM = 16.0
GLA_CHUNK = 16
GLA_W = 2 * GLA_DK_TOT + 2 * GLA_DV_TOT + GLA_LR
EVEN_PROJ = SHIFT_W + GLA_W

ATT_HD = 64
N_Q_HEADS = (MIX_W // 2) // ATT_HD
N_KV_HEADS = 2
Q_PER_KV = N_Q_HEADS // N_KV_HEADS
WINDOW = 128
N_BUCKETS = 32
BUCKET_MAX_DIST = 128

S5_W = MIX_W - N_Q_HEADS * ATT_HD
S5_P = 16
S5_GROUPS = S5_W // S5_P
S5_N = 64
ODD_PROJ = N_Q_HEADS * ATT_HD + 2 * N_KV_HEADS * ATT_HD + S5_W

D_FF = 2816
N_EXPERTS = 8
TOP_K = 2
D_FF_EXPERT = 3584

kernel_name = 'hybrid_rwkv7_gla_swa_s5_decode_step'


def rms_norm(x, g):
    xf = x.astype(jnp.float32)
    y = xf * lax.rsqrt(jnp.mean(xf * xf, -1, keepdims=True) + EPS) * g.astype(jnp.float32)
    return y.astype(x.dtype)


def swiglu(h, w1, w3, w2):
    return (jax.nn.silu(h @ w1) * (h @ w3)) @ w2


def moe_swiglu(h, router_w, router_b, w1, w3, w2):
    logits = (h @ router_w).astype(jnp.float32) + router_b.astype(jnp.float32)
    top_v, top_i = lax.top_k(logits, TOP_K)
    gates = jax.nn.softmax(top_v, axis=-1)
    out = jnp.zeros_like(h)
    for e in range(N_EXPERTS):
        g_e = jnp.sum(jnp.where(top_i == e, gates, 0.0), -1, keepdims=True).astype(h.dtype)
        out = out + g_e * swiglu(h, w1[e], w3[e], w2[e])
    return out


def rwkv7_mix(p, prev_row, s0, mu, w0, w_up, a0, a_up, g_up, k_k, k_a, r_k, lnx_w, lnx_b):
    bsz, t, _ = p.shape
    p_prev = jnp.concatenate([prev_row[:, None, :].astype(p.dtype), p[:, :-1]], axis=1)
    xs = p + (p_prev - p) * mu
    c1 = RWKV_W
    c2 = 2 * RWKV_W
    c3 = 3 * RWKV_W
    c4 = c3 + RWKV_LR_W
    c5 = c4 + RWKV_LR_A
    r, k, v, wd, ad, gd = jnp.split(xs, [c1, c2, c3, c4, c5], axis=-1)
    w_pre = (w0 + jnp.tanh(wd) @ w_up).astype(jnp.float32)
    log_neg_log_decay = -jax.nn.softplus(-w_pre) - 0.5
    decay = jnp.exp(-jnp.exp(log_neg_log_decay)).astype(p.dtype)
    a = jax.nn.sigmoid(a0 + ad @ a_up)
    g = jax.nn.sigmoid(gd) @ g_up
    heads = lambda z: z.reshape(bsz, t, RWKV_HEADS, RWKV_HD)
    kk = heads(k * k_k).astype(jnp.float32)
    kk = (kk / jnp.maximum(jnp.sqrt(jnp.sum(kk * kk, -1, keepdims=True)), 1e-12)).astype(p.dtype)
    k = k * (1 + (a - 1) * k_a)
    r_h, w_h, k_h, v_h, a_h = heads(r), heads(decay), heads(k), heads(v), heads(a)
    tm = lambda z: jnp.moveaxis(z, 1, 0).astype(s0.dtype)

    def step(s, inp):
        r_t, w_t, k_t, v_t, kk_t, a_t = inp
        sa = jnp.einsum('bhvk,bhk->bhv', s, -kk_t)
        s = (s * w_t[:, :, None, :] + sa[..., None] * (kk_t * a_t)[:, :, None, :]
             + v_t[..., None] * k_t[:, :, None, :]).astype(s0.dtype)
        return s, jnp.einsum('bhvk,bhk->bhv', s, r_t)

    s_fin, y = lax.scan(step, s0, (tm(r_h), tm(w_h), tm(k_h), tm(v_h), tm(kk), tm(a_h)))
    y = jnp.moveaxis(y, 0, 1).astype(jnp.float32)
    mean = jnp.mean(y, -1, keepdims=True)
    var = jnp.mean(jnp.square(y - mean), -1, keepdims=True)
    y = ((y - mean) * lax.rsqrt(var + RWKV_GN_EPS)).reshape(bsz, t, RWKV_W) * lnx_w + lnx_b
    bonus = jnp.sum(r_h * k_h * r_k, -1, keepdims=True) * v_h
    y = (y.astype(p.dtype) + bonus.reshape(bsz, t, RWKV_W)) * g
    return y, p[:, -1], s_fin


def gla_chunked(q, k, v, gk, s0):
    bsz, t, h, _ = q.shape
    dv = v.shape[-1]
    L = math.gcd(t, GLA_CHUNK)
    nc = t // L
    dt = s0.dtype
    blk = lambda z: z.reshape(bsz, nc, L, h, z.shape[-1]).transpose(1, 0, 3, 2, 4)
    qc, kc, vc = blk(q).astype(dt), blk(k).astype(dt), blk(v).astype(dt)
    bc = jnp.cumsum(blk(gk), axis=3)
    causal = jnp.tril(jnp.ones((L, L), dtype=bool))[:, :, None]
    rel = jnp.where(causal, bc[:, :, :, :, None, :] - bc[:, :, :, None, :, :], -jnp.inf)
    attn = jnp.einsum('cbhid,cbhjd,cbhijd->cbhij', qc, kc, jnp.exp(rel).astype(dt))
    o_intra = jnp.einsum('cbhij,cbhjv->cbhiv', attn, vc)
    b_last = bc[:, :, :, -1:, :]
    q_dec = qc * jnp.exp(bc).astype(dt)
    k_dec = kc * jnp.exp(b_last - bc).astype(dt)
    g_last = jnp.exp(b_last[:, :, :, 0, :]).astype(dt)

    def step(s, inp):
        qd, kd, vv, gl = inp
        o = jnp.einsum('bhid,bhdv->bhiv', qd, s)
        s = (s * gl[..., None] + jnp.einsum('bhjd,bhjv->bhdv', kd, vv)).astype(dt)
        return s, o

    s_fin, o_inter = lax.scan(step, s0, (q_dec, k_dec, vc, g_last))
    o = (o_inter + o_intra).transpose(1, 0, 3, 2, 4).reshape(bsz, t, h, dv)
    return o, s_fin


def gla_mix(p, s0, a_up, a_b, norm_w):
    bsz, t, _ = p.shape
    i1 = GLA_DK_TOT
    i2 = 2 * GLA_DK_TOT
    i3 = i2 + GLA_DV_TOT
    i4 = i3 + GLA_DV_TOT
    q, k, v, g, ad = jnp.split(p, [i1, i2, i3, i4], axis=-1)
    gk = jax.nn.log_sigmoid((ad @ a_up + a_b).astype(jnp.float32)) / GLA_GATE_NORM
    hk = lambda z: z.reshape(bsz, t, GLA_HEADS, GLA_DK)
    o, s_fin = gla_chunked(hk(q) * (GLA_DK ** -0.5), hk(k), v.reshape(bsz, t, GLA_HEADS, GLA_DV), hk(gk), s0)
    of = o.astype(jnp.float32)
    of = of * lax.rsqrt(jnp.mean(of * of, -1, keepdims=True) + EPS) * norm_w.astype(jnp.float32)
    return of.reshape(bsz, t, GLA_DV_TOT).astype(p.dtype) * jax.nn.silu(g), s_fin


def t5_bucket(dist):
    n = jnp.maximum(dist, 0)
    max_exact = N_BUCKETS // 2
    nf = jnp.maximum(n, 1).astype(jnp.float32)
    large = max_exact + (jnp.log(nf / max_exact) / math.log(BUCKET_MAX_DIST / max_exact)
                         * (N_BUCKETS - max_exact)).astype(jnp.int32)
    large = jnp.minimum(large, N_BUCKETS - 1)
    return jnp.where(n < max_exact, n, large)


def sink_attend(q, k, v, dist, valid, rel_table, sinks):
    s = jnp.einsum('bnqhgd,bnshd->bnhgqs', q, k).astype(jnp.float32) * (ATT_HD ** -0.5)
    nd, nq, ns = dist.shape
    bias = rel_table[t5_bucket(dist)].astype(jnp.float32)
    bias = bias.reshape(nd, nq, ns, N_KV_HEADS, Q_PER_KV).transpose(0, 3, 4, 1, 2)
    logits = jnp.where(valid[None, :, None, None], s + bias[None], -1e30)
    sink = sinks.astype(jnp.float32).reshape(1, 1, N_KV_HEADS, Q_PER_KV, 1, 1)
    m = jnp.maximum(jnp.max(logits, -1, keepdims=True), sink)
    pr = jnp.exp(logits - m)
    probs = pr / (jnp.sum(pr, -1, keepdims=True) + jnp.exp(sink - m))
    return jnp.einsum('bnhgqs,bnshd->bnqhgd', probs.astype(v.dtype), v)


def swa_prompt(q, k, v, rel_table, sinks):
    bsz, t = q.shape[:2]
    nb = t // WINDOW
    pad = jnp.zeros((bsz, WINDOW) + k.shape[2:], k.dtype)

    def band(z):
        zp = jnp.concatenate([pad, z], 1).reshape(bsz, nb + 1, WINDOW, N_KV_HEADS, ATT_HD)
        return jnp.concatenate([zp[:, :-1], zp[:, 1:]], axis=2)

    qi = jnp.arange(WINDOW)[:, None]
    kj = jnp.arange(2 * WINDOW)[None, :]
    dist = (qi + WINDOW - kj)[None]
    kpos = jnp.arange(nb)[:, None, None] * WINDOW - WINDOW + kj[None]
    valid = (dist >= 0) & (dist < WINDOW) & (kpos >= 0)
    o = sink_attend(q.reshape(bsz, nb, WINDOW, N_KV_HEADS, Q_PER_KV, ATT_HD), band(k), band(v),
                    dist, valid, rel_table, sinks)
    keep = min(WINDOW, t)
    return o.reshape(bsz, t, N_Q_HEADS * ATT_HD), k[:, t - keep:], v[:, t - keep:]


def swa_decode(q, k, v, buf_k, buf_v, rel_table, sinks):
    bsz, t = q.shape[:2]
    wb = buf_k.shape[1]
    k_all = jnp.concatenate([buf_k.astype(k.dtype), k], 1)
    v_all = jnp.concatenate([buf_v.astype(v.dtype), v], 1)
    kpos = jnp.concatenate([jnp.arange(wb) - wb, jnp.arange(t)])
    dist = jnp.arange(t)[:, None] - kpos[None, :]
    valid = (dist >= 0) & (dist < WINDOW)
    o = sink_attend(q.reshape(bsz, 1, t, N_KV_HEADS, Q_PER_KV, ATT_HD), k_all[:, None], v_all[:, None],
                    dist[None], valid[None], rel_table, sinks)
    return o.reshape(bsz, t, N_Q_HEADS * ATT_HD), k_all[:, t:], v_all[:, t:]


def _complex_affine_combine(e1, e2):
    a1r, a1i, b1r, b1i = e1
    a2r, a2i, b2r, b2i = e2
    return (a2r * a1r - a2i * a1i, a2r * a1i + a2i * a1r,
            a2r * b1r - a2i * b1i + b2r, a2r * b1i + a2i * b1r + b2i)


def s5_mix(u, h0_re, h0_im, a_re, a_im, log_dt, b_re, b_im, c_re, c_im, d_skip, glu_w, glu_b):
    f32 = jnp.float32
    bsz, t, _ = u.shape
    step = jnp.exp(log_dt.astype(f32))[:, None]
    ar, ai = a_re.astype(f32), a_im.astype(f32)
    mag = jnp.exp(step * ar)
    abar_re, abar_im = mag * jnp.cos(step * ai), mag * jnp.sin(step * ai)
    den = ar * ar + ai * ai
    f_re = ((abar_re - 1.0) * ar + abar_im * ai) / den
    f_im = (abar_im * ar - (abar_re - 1.0) * ai) / den
    br, bi = b_re.astype(f32), b_im.astype(f32)
    bbar_re = f_re[..., None] * br - f_im[..., None] * bi
    bbar_im = f_re[..., None] * bi + f_im[..., None] * br
    ug = u.astype(f32).reshape(bsz, t, S5_GROUPS, S5_P)
    bu_re = jnp.einsum('gnp,btgp->btgn', bbar_re, ug)
    bu_im = jnp.einsum('gnp,btgp->btgn', bbar_im, ug)
    h0r, h0i = h0_re.astype(f32), h0_im.astype(f32)
    bu_re = bu_re.at[:, 0].add(abar_re * h0r - abar_im * h0i)
    bu_im = bu_im.at[:, 0].add(abar_re * h0i + abar_im * h0r)
    elems = (jnp.broadcast_to(abar_re, bu_re.shape), jnp.broadcast_to(abar_im, bu_im.shape), bu_re, bu_im)
    _, _, hr, hi = lax.associative_scan(_complex_affine_combine, elems, axis=1)
    y = (jnp.einsum('gpn,btgn->btgp', c_re.astype(f32), hr)
         - jnp.einsum('gpn,btgn->btgp', c_im.astype(f32), hi))
    y = y.reshape(bsz, t, S5_W) + d_skip.astype(f32) * u.astype(f32)
    z = jax.nn.gelu(y)
    out = z * jax.nn.sigmoid(z @ glu_w.astype(f32) + glu_b.astype(f32))
    return out.astype(u.dtype), hr[:, -1].astype(h0_re.dtype), hi[:, -1].astype(h0_im.dtype)


def setup_inputs(seed: int = 0) -> dict:
    key = jax.random.key(seed)
    ks = iter(jax.random.split(key, 96))
    nrm = lambda shape, scale: jax.random.normal(next(ks), shape, jnp.float32) * scale
    unif = lambda shape, lo, hi: jax.random.uniform(next(ks), shape, jnp.float32, lo, hi)
    gain = lambda shape: 1.0 + nrm(shape, 0.05)
    win_buf = min(WINDOW, PAST_LEN)
    ne, no = N_EVEN, N_ODD
    a_im_base = math.pi * jnp.arange(S5_N, dtype=jnp.float32)[None, None, :]
    return {
        'x_prompt': nrm((BATCH, SEQ, D_MODEL), 1.0),
        'x_sample': nrm((DEC_BATCH, DEC_SEQ, D_MODEL), 1.0),
        'state_rwkv': nrm((ne, DEC_BATCH, RWKV_HEADS, RWKV_HD, RWKV_HD), 0.3),
        'state_shift': nrm((ne, DEC_BATCH, SHIFT_W), 1.0),
        'state_gla': nrm((ne, DEC_BATCH, GLA_HEADS, GLA_DK, GLA_DV), 0.3),
        'cache_win_k': nrm((no, DEC_BATCH, win_buf, N_KV_HEADS, ATT_HD), 1.0),
        'cache_win_v': nrm((no, DEC_BATCH, win_buf, N_KV_HEADS, ATT_HD), 1.0),
        'state_s5_re': nrm((no, DEC_BATCH, S5_GROUPS, S5_N), 0.3),
        'state_s5_im': nrm((no, DEC_BATCH, S5_GROUPS, S5_N), 0.3),
        'rel_table': nrm((N_BUCKETS, N_Q_HEADS), 0.5),
        'e_norm1': gain((ne, D_MODEL)),
        'e_w_in': nrm((ne, D_MODEL, EVEN_PROJ), D_MODEL ** -0.5),
        'e_mu': unif((ne, SHIFT_W), 0.0, 1.0),
        'e_w0': nrm((ne, RWKV_W), 1.0),
        'e_w_up': nrm((ne, RWKV_LR_W, RWKV_W), 0.5 * RWKV_LR_W ** -0.5),
        'e_a0': nrm((ne, RWKV_W), 0.5),
        'e_a_up': nrm((ne, RWKV_LR_A, RWKV_W), RWKV_LR_A ** -0.5),
        'e_g_up': nrm((ne, RWKV_LR_G, RWKV_W), RWKV_LR_G ** -0.5),
        'e_k_k': 0.85 + nrm((ne, RWKV_W), 0.05),
        'e_k_a': gain((ne, RWKV_W)),
        'e_r_k': nrm((ne, RWKV_HEADS, RWKV_HD), 0.1),
        'e_lnx_w': gain((ne, RWKV_W)),
        'e_lnx_b': nrm((ne, RWKV_W), 0.02),
        'e_gla_a_up': nrm((ne, GLA_LR, GLA_DK_TOT), GLA_LR ** -0.5),
        'e_gla_a_b': nrm((ne, GLA_DK_TOT), 1.0),
        'e_gla_norm': gain((ne, GLA_DV)),
        'e_w_out': nrm((ne, MIX_W, D_MODEL), MIX_W ** -0.5),
        'e_norm2': gain((ne, D_MODEL)),
        'e_ff_w1': nrm((ne, D_MODEL, D_FF), D_MODEL ** -0.5),
        'e_ff_w3': nrm((ne, D_MODEL, D_FF), D_MODEL ** -0.5),
        'e_ff_w2': nrm((ne, D_FF, D_MODEL), D_FF ** -0.5),
        'o_norm1': gain((no, D_MODEL)),
        'o_w_in': nrm((no, D_MODEL, ODD_PROJ), D_MODEL ** -0.5),
        'o_q_norm': gain((no, ATT_HD)),
        'o_k_norm': gain((no, ATT_HD)),
        'o_sinks': nrm((no, N_Q_HEADS), 1.0),
        'o_a_re': -0.5 + nrm((no, S5_GROUPS, S5_N), 0.01),
        'o_a_im': a_im_base + nrm((no, S5_GROUPS, S5_N), 0.001),
        'o_log_dt': unif((no, S5_GROUPS), float(np.log(0.001)), float(np.log(0.1))),
        'o_b_re': nrm((no, S5_GROUPS, S5_N, S5_P), S5_P ** -0.5),
        'o_b_im': nrm((no, S5_GROUPS, S5_N, S5_P), S5_P ** -0.5),
        'o_c_re': nrm((no, S5_GROUPS, S5_P, S5_N), S5_N ** -0.5),
        'o_c_im': nrm((no, S5_GROUPS, S5_P, S5_N), S5_N ** -0.5),
        'o_d': nrm((no, S5_W), 1.0),
        'o_glu_w': nrm((no, S5_W, S5_W), S5_W ** -0.5),
        'o_glu_b': nrm((no, S5_W), 0.02),
        'o_w_out': nrm((no, MIX_W, D_MODEL), MIX_W ** -0.5),
        'o_norm2': gain((no, D_MODEL)),
        'o_router_w': nrm((no, D_MODEL, N_EXPERTS), D_MODEL ** -0.5),
        'o_router_b': nrm((no, N_EXPERTS), 0.01),
        'o_moe_w1': nrm((no, N_EXPERTS, D_MODEL, D_FF_EXPERT), D_MODEL ** -0.5),
        'o_moe_w3': nrm((no, N_EXPERTS, D_MODEL, D_FF_EXPERT), D_MODEL ** -0.5),
        'o_moe_w2': nrm((no, N_EXPERTS, D_FF_EXPERT, D_MODEL), D_FF_EXPERT ** -0.5),
    }


def reference(x_prompt, x_sample, state_rwkv, state_shift, state_gla, cache_win_k, cache_win_v,
              state_s5_re, state_s5_im, rel_table,
              e_norm1, e_w_in, e_mu, e_w0, e_w_up, e_a0, e_a_up, e_g_up, e_k_k, e_k_a, e_r_k,
              e_lnx_w, e_lnx_b, e_gla_a_up, e_gla_a_b, e_gla_norm, e_w_out, e_norm2,
              e_ff_w1, e_ff_w3, e_ff_w2,
              o_norm1, o_w_in, o_q_norm, o_k_norm, o_sinks, o_a_re, o_a_im, o_log_dt,
              o_b_re, o_b_im, o_c_re, o_c_im, o_d, o_glu_w, o_glu_b, o_w_out, o_norm2,
              o_router_w, o_router_b, o_moe_w1, o_moe_w3, o_moe_w2):
    qw = N_Q_HEADS * ATT_HD
    kw = N_KV_HEADS * ATT_HD

    def trunk(x, s_rwkv, s_shift, s_gla, w_k, w_v, s5r, s5i):
        new_rwkv, new_shift, new_gla, new_wk, new_wv, new_s5r, new_s5i = [], [], [], [], [], [], []
        for layer in range(DEPTH):
            i = layer // 2
            if layer % 2 == 0:
                p = rms_norm(x, e_norm1[i]) @ e_w_in[i]
                ya, sh, sr = rwkv7_mix(p[..., :SHIFT_W], s_shift[i], s_rwkv[i], e_mu[i], e_w0[i], e_w_up[i],
                                       e_a0[i], e_a_up[i], e_g_up[i], e_k_k[i], e_k_a[i], e_r_k[i],
                                       e_lnx_w[i], e_lnx_b[i])
                yb, sg = gla_mix(p[..., SHIFT_W:], s_gla[i], e_gla_a_up[i], e_gla_a_b[i], e_gla_norm[i])
                x = x + jnp.concatenate([ya, yb], -1) @ e_w_out[i]
                x = x + swiglu(rms_norm(x, e_norm2[i]), e_ff_w1[i], e_ff_w3[i], e_ff_w2[i])
                new_rwkv.append(sr)
                new_shift.append(sh)
                new_gla.append(sg)
            else:
                p = rms_norm(x, o_norm1[i]) @ o_w_in[i]
                bsz, t, _ = p.shape
                q = rms_norm(p[..., :qw].reshape(bsz, t, N_Q_HEADS, ATT_HD), o_q_norm[i])
                k = rms_norm(p[..., qw:qw + kw].reshape(bsz, t, N_KV_HEADS, ATT_HD), o_k_norm[i])
                v = p[..., qw + kw:qw + 2 * kw].reshape(bsz, t, N_KV_HEADS, ATT_HD)
                u = p[..., qw + 2 * kw:]
                if w_k is None:
                    yc, nk, nv = swa_prompt(q, k, v, rel_table, o_sinks[i])
                else:
                    yc, nk, nv = swa_decode(q, k, v, w_k[i], w_v[i], rel_table, o_sinks[i])
                yd, hr, hi = s5_mix(u, s5r[i], s5i[i], o_a_re[i], o_a_im[i], o_log_dt[i], o_b_re[i], o_b_im[i],
                                    o_c_re[i], o_c_im[i], o_d[i], o_glu_w[i], o_glu_b[i])
                x = x + jnp.concatenate([yc, yd], -1) @ o_w_out[i]
                x = x + moe_swiglu(rms_norm(x, o_norm2[i]), o_router_w[i], o_router_b[i],
                                   o_moe_w1[i], o_moe_w3[i], o_moe_w2[i])
                new_wk.append(nk)
                new_wv.append(nv)
                new_s5r.append(hr)
                new_s5i.append(hi)
        return (x, jnp.stack(new_rwkv), jnp.stack(new_shift), jnp.stack(new_gla), jnp.stack(new_wk),
                jnp.stack(new_wv), jnp.stack(new_s5r), jnp.stack(new_s5i))

    bp = x_prompt.shape[0]
    dt = x_prompt.dtype
    y_prompt, rw_p, sh_p, gl_p, wk_p, wv_p, s5r_p, s5i_p = trunk(
        x_prompt,
        jnp.zeros((N_EVEN, bp, RWKV_HEADS, RWKV_HD, RWKV_HD), dt),
        jnp.zeros((N_EVEN, bp, SHIFT_W), dt),
        jnp.zeros((N_EVEN, bp, GLA_HEADS, GLA_DK, GLA_DV), dt),
        None, None,
        jnp.zeros((N_ODD, bp, S5_GROUPS, S5_N), dt),
        jnp.zeros((N_ODD, bp, S5_GROUPS, S5_N), dt))
    y_sample, rw_s, sh_s, gl_s, wk_s, wv_s, s5r_s, s5i_s = trunk(
        x_sample, state_rwkv, state_shift, state_gla, cache_win_k, cache_win_v, state_s5_re, state_s5_im)
    return (y_prompt, y_sample, rw_p, rw_s, sh_p, sh_s, gl_p, gl_s, wk_p, wk_s, wv_p, wv_s,
            s5r_p, s5r_s, s5i_p, s5i_s)
```

```python
import functools
import math

import jax
import jax.numpy as jnp
import numpy as np
from jax import lax
from jax.experimental import pallas as pl
from jax.experimental.pallas import tpu as pltpu

F32 = jnp.float32
BF16 = jnp.bfloat16

LANES = 128
SUBLANES = 8
VMEM_LIMIT_BYTES = 56 * 1024 * 1024

EPS = 1e-6
RWKV_HD = 64
RWKV_GN_EPS = 64e-5
GLA_GATE_NORM = 16.0
ATT_HD = 64
WINDOW = 128
N_BUCKETS = 32
BUCKET_MAX_DIST = 128
S5_P = 16
S5_N = 64
S5_CHUNK = 16
TOP_K = 2


def _cparams(sem):
    return pltpu.CompilerParams(dimension_semantics=sem, vmem_limit_bytes=VMEM_LIMIT_BYTES)


_NN = (((1,), (0,)), ((), ()))
_NT = (((1,), (1,)), ((), ()))
_TN = (((0,), (0,)), ((), ()))


def _dg(a, b, dims):
    return lax.dot_general(a, b, dims, preferred_element_type=F32)


def _bdot(a, b, dims=_NN):
    return _dg(a.astype(BF16), b.astype(BF16), dims)


def _split(a, n):
    terms = []
    r = a
    for _ in range(n):
        t = r.astype(BF16)
        terms.append(t)
        r = r - t.astype(F32)
    return terms


def _hdot(a, b, dims=_NN):
    a0, a1 = _split(a, 2)
    b0, b1 = _split(b, 2)
    return _dg(a0, b0, dims) + (_dg(a0, b1, dims) + _dg(a1, b0, dims))


def _xdot_l(a, e, dims=_NN):
    e = e.astype(BF16)
    a0, a1, a2 = _split(a, 3)
    return _dg(a0, e, dims) + (_dg(a1, e, dims) + _dg(a2, e, dims))


def _xdot_r(e, b, dims=_NN):
    e = e.astype(BF16)
    b0, b1, b2 = _split(b, 3)
    return _dg(e, b0, dims) + (_dg(e, b1, dims) + _dg(e, b2, dims))


def _iota(shape, axis):
    return lax.broadcasted_iota(jnp.int32, shape, axis)


def _seg_ones(n, seg):
    r = _iota((n, n), 0) // seg
    c = _iota((n, n), 1) // seg
    return jnp.where(r == c, 1.0, 0.0).astype(BF16)


def _seg_sum(x, seg):
    n = x.shape[-1]
    return _xdot_l(x, _seg_ones(n, seg))


def _sigmoid(x):
    return 1.0 / (1.0 + jnp.exp(-x))


def _silu(x):
    return x * _sigmoid(x)


def _softplus(x):
    return jnp.maximum(x, 0.0) + jnp.log(1.0 + jnp.exp(-jnp.abs(x)))


def _tri_incl(n):
    r = _iota((n, n), 0)
    c = _iota((n, n), 1)
    return jnp.where(c <= r, 1.0, 0.0).astype(BF16)


def _rms(x, g):
    return x * lax.rsqrt(jnp.mean(x * x, axis=-1, keepdims=True) + EPS) * g


def _norm_proj_kernel(n_w, x_ref, g_ref, *refs):
    xn = _rms(x_ref[...], g_ref[...]).astype(BF16)
    for w_ref, o_ref in zip(refs[:n_w], refs[n_w:]):
        o_ref[...] = _dg(xn, w_ref[...], _NN)


def norm_proj(x, g, ws_bf16, tm):
    m, d = x.shape
    return pl.pallas_call(
        functools.partial(_norm_proj_kernel, len(ws_bf16)),
        grid=(m // tm,),
        in_specs=[pl.BlockSpec((tm, d), lambda i: (i, 0)),
                  pl.BlockSpec((1, d), lambda i: (0, 0))]
                 + [pl.BlockSpec(w.shape, lambda i: (0, 0)) for w in ws_bf16],
        out_specs=[pl.BlockSpec((tm, w.shape[1]), lambda i: (i, 0)) for w in ws_bf16],
        out_shape=[jax.ShapeDtypeStruct((m, w.shape[1]), F32) for w in ws_bf16],
        compiler_params=_cparams(("parallel",)),
    )(x, g.reshape(1, d), *ws_bf16)


def _out_proj_kernel(x_ref, ya_ref, yb_ref, wa_ref, wb_ref, o_ref):
    o_ref[...] = (x_ref[...] + _dg(ya_ref[...].astype(BF16), wa_ref[...], _NN)
                  + _dg(yb_ref[...].astype(BF16), wb_ref[...], _NN))


def out_proj(x, ya, yb, wa, wb, tm):
    m, d = x.shape
    return pl.pallas_call(
        _out_proj_kernel,
        grid=(m // tm,),
        in_specs=[pl.BlockSpec((tm, d), lambda i: (i, 0)),
                  pl.BlockSpec((tm, ya.shape[1]), lambda i: (i, 0)),
                  pl.BlockSpec((tm, yb.shape[1]), lambda i: (i, 0)),
                  pl.BlockSpec(wa.shape, lambda i: (0, 0)),
                  pl.BlockSpec(wb.shape, lambda i: (0, 0))],
        out_specs=pl.BlockSpec((tm, d), lambda i: (i, 0)),
        out_shape=jax.ShapeDtypeStruct((m, d), F32),
        compiler_params=_cparams(("parallel",)),
    )(x, ya, yb, wa, wb)


def _ffn_kernel(x_ref, g_ref, w1_ref, w3_ref, w2_ref, o_ref, xn_scr):
    j = pl.program_id(1)

    @pl.when(j == 0)
    def _():
        x = x_ref[...]
        xn_scr[...] = _rms(x, g_ref[...]).astype(BF16)
        o_ref[...] = x

    xn = xn_scr[...]
    h = _silu(_dg(xn, w1_ref[...], _NN)) * _dg(xn, w3_ref[...], _NN)
    o_ref[...] += _dg(h.astype(BF16), w2_ref[...], _NN)


def ffn(x, g, w1, w3, w2, tm, fc):
    m, d = x.shape
    dff = w1.shape[1]
    return pl.pallas_call(
        _ffn_kernel,
        grid=(m // tm, dff // fc),
        in_specs=[pl.BlockSpec((tm, d), lambda i, j: (i, 0)),
                  pl.BlockSpec((1, d), lambda i, j: (0, 0)),
                  pl.BlockSpec((d, fc), lambda i, j: (0, j)),
                  pl.BlockSpec((d, fc), lambda i, j: (0, j)),
                  pl.BlockSpec((fc, d), lambda i, j: (j, 0))],
        out_specs=pl.BlockSpec((tm, d), lambda i, j: (i, 0)),
        out_shape=jax.ShapeDtypeStruct((m, d), F32),
        scratch_shapes=[pltpu.VMEM((tm, d), BF16)],
        compiler_params=_cparams(("parallel", "arbitrary")),
    )(x, g.reshape(1, d), w1, w3, w2)


def _router_kernel(n_exp, x_ref, g_ref, rw_ref, rb_ref, hn_ref, gate_ref):
    xn = _rms(x_ref[...], g_ref[...])
    hn_ref[...] = xn.astype(BF16)
    logits = _hdot(xn, rw_ref[...]) + rb_ref[...]
    lane = _iota(logits.shape, 1)
    logits = jnp.where(lane < n_exp, logits, -jnp.inf)
    m1 = jnp.max(logits, axis=-1, keepdims=True)
    i1 = jnp.min(jnp.where(logits == m1, lane, LANES), axis=-1, keepdims=True)
    rest = jnp.where(lane == i1, -jnp.inf, logits)
    m2 = jnp.max(rest, axis=-1, keepdims=True)
    i2 = jnp.min(jnp.where(rest == m2, lane, LANES), axis=-1, keepdims=True)
    e2 = jnp.exp(m2 - m1)
    g1 = 1.0 / (1.0 + e2)
    g2 = e2 / (1.0 + e2)
    gate_ref[...] = jnp.where(lane == i1, g1, 0.0) + jnp.where(lane == i2, g2, 0.0)


def router(x, g, rw, rb, tm):
    m, d = x.shape
    n_exp = rw.shape[1]
    rw_pad = jnp.pad(rw, ((0, 0), (0, LANES - n_exp)))
    rb_pad = jnp.pad(rb, (0, LANES - n_exp)).reshape(1, LANES)
    return pl.pallas_call(
        functools.partial(_router_kernel, n_exp),
        grid=(m // tm,),
        in_specs=[pl.BlockSpec((tm, d), lambda i: (i, 0)),
                  pl.BlockSpec((1, d), lambda i: (0, 0)),
                  pl.BlockSpec((d, LANES), lambda i: (0, 0)),
                  pl.BlockSpec((1, LANES), lambda i: (0, 0))],
        out_specs=[pl.BlockSpec((tm, d), lambda i: (i, 0)),
                   pl.BlockSpec((tm, LANES), lambda i: (i, 0))],
        out_shape=[jax.ShapeDtypeStruct((m, d), BF16), jax.ShapeDtypeStruct((m, LANES), F32)],
        compiler_params=_cparams(("parallel",)),
    )(x, g.reshape(1, d), rw_pad, rb_pad)


def _moe_kernel(x_ref, hn_ref, gate_ref, w1_ref, w3_ref, w2_ref, o_ref):
    e = pl.program_id(1)
    j = pl.program_id(2)

    @pl.when(jnp.logical_and(e == 0, j == 0))
    def _():
        o_ref[...] = x_ref[...]

    gt = gate_ref[...]
    gcol = jnp.sum(jnp.where(_iota(gt.shape, 1) == e, gt, 0.0), axis=-1, keepdims=True)
    hn = hn_ref[...]
    h = _silu(_dg(hn, w1_ref[0], _NN)) * _dg(hn, w3_ref[0], _NN)
    o_ref[...] += gcol * _dg(h.astype(BF16), w2_ref[0], _NN)


def moe(x, hn, gates, w1, w3, w2, tm, fc):
    m, d = x.shape
    n_exp, _, dff = w1.shape
    return pl.pallas_call(
        _moe_kernel,
        grid=(m // tm, n_exp, dff // fc),
        in_specs=[pl.BlockSpec((tm, d), lambda i, e, j: (i, 0)),
                  pl.BlockSpec((tm, d), lambda i, e, j: (i, 0)),
                  pl.BlockSpec((tm, LANES), lambda i, e, j: (i, 0)),
                  pl.BlockSpec((1, d, fc), lambda i, e, j: (e, 0, j)),
                  pl.BlockSpec((1, d, fc), lambda i, e, j: (e, 0, j)),
                  pl.BlockSpec((1, fc, d), lambda i, e, j: (e, j, 0))],
        out_specs=pl.BlockSpec((tm, d), lambda i, e, j: (i, 0)),
        out_shape=jax.ShapeDtypeStruct((m, d), F32),
        compiler_params=_cparams(("parallel", "arbitrary", "arbitrary")),
    )(x, hn, gates, w1, w3, w2)


RW = 512
RSHIFT = 1792


def _rwkv_pre_kernel(t_per_batch, tm, p_ref, pb_ref, prev_ref, mu_ref, w0_ref, wup_ref, a0_ref, aup_ref,
                     gup_ref, kk_ref, ka_ref, rk_ref,
                     r_o, lw_o, k_o, v_o, al_o, be_o, g_o, bo_o):
    i = pl.program_id(0)
    p = p_ref[...]
    rolled = pltpu.roll(p, 1, 0)
    row = _iota((tm, 1), 0)
    rolled = jnp.where(row == 0, pb_ref[SUBLANES - 1:SUBLANES, :], rolled)
    if t_per_batch >= tm:
        first = (i * tm) % t_per_batch == 0
        is_start = jnp.logical_and(row == 0, first)
        prev = jnp.where(is_start, prev_ref[0], rolled)
    else:
        is_start = (row % t_per_batch) == 0
        prev = jnp.where(is_start, prev_ref[...], rolled)
    xs = p + (prev - p) * mu_ref[...]
    r = xs[:, 0:RW]
    k = xs[:, RW:2 * RW]
    v = xs[:, 2 * RW:3 * RW]
    lr = xs[:, 3 * RW:3 * RW + LANES]
    gd = xs[:, 3 * RW + LANES:3 * RW + 2 * LANES]
    w_pre = w0_ref[...] + _bdot(jnp.tanh(lr), wup_ref[...])
    logw = -jnp.exp(-_softplus(-w_pre) - 0.5)
    a = _sigmoid(a0_ref[...] + _bdot(lr, aup_ref[...]))
    g = _bdot(_sigmoid(gd), gup_ref[...])
    kkr = k * kk_ref[...]
    kk = kkr / jnp.maximum(jnp.sqrt(_seg_sum(kkr * kkr, RWKV_HD)), 1e-12)
    k2 = k * (1.0 + (a - 1.0) * ka_ref[...])
    bonus = _seg_sum(r * k2 * rk_ref[...], RWKV_HD) * v
    r_o[...] = r
    lw_o[...] = logw
    k_o[...] = k2
    v_o[...] = v
    al_o[...] = -kk
    be_o[...] = kk * a
    g_o[...] = g
    bo_o[...] = bonus


def rwkv_pre(p_full, prev_rows, t_per_batch, tm, prm):
    m = p_full.shape[0]
    nb8 = tm // SUBLANES
    row_spec = pl.BlockSpec((1, RW), lambda i: (0, 0))
    if t_per_batch >= tm:
        prev_spec = pl.BlockSpec((1, 1, RSHIFT), lambda i: ((i * tm) // t_per_batch, 0, 0))
    else:
        prev_spec = pl.BlockSpec((tm, RSHIFT), lambda i: (i, 0))
    out_sds = jax.ShapeDtypeStruct((m, RW), F32)
    out_spec = pl.BlockSpec((tm, RW), lambda i: (i, 0))
    return pl.pallas_call(
        functools.partial(_rwkv_pre_kernel, t_per_batch, tm),
        grid=(m // tm,),
        in_specs=[pl.BlockSpec((tm, RSHIFT), lambda i: (i, 0)),
                  pl.BlockSpec((SUBLANES, RSHIFT), lambda i: (jnp.maximum(i * nb8 - 1, 0), 0)),
                  prev_spec,
                  pl.BlockSpec((1, RSHIFT), lambda i: (0, 0)),
                  row_spec,
                  pl.BlockSpec((LANES, RW), lambda i: (0, 0)),
                  row_spec,
                  pl.BlockSpec((LANES, RW), lambda i: (0, 0)),
                  pl.BlockSpec((LANES, RW), lambda i: (0, 0)),
                  row_spec, row_spec, row_spec],
        out_specs=[out_spec] * 8,
        out_shape=[out_sds] * 8,
        compiler_params=_cparams(("parallel",)),
    )(p_full, p_full, prev_rows, prm["mu"], prm["w0"], prm["wup"], prm["a0"], prm["aup"], prm["gup"],
      prm["k_k"], prm["k_a"], prm["r_k"])


def _rwkv_scan_kernel(L, n_pairs, r_ref, lw_ref, k_ref, v_ref, al_ref, be_ref, s0_ref,
                      y_ref, sf_ref, s_scr):
    c = pl.program_id(1)
    nc = pl.num_programs(1)
    L2 = 2 * L
    lane = _iota((1, LANES), 1)
    m0 = jnp.where(lane < RWKV_HD, 1.0, 0.0)
    m1 = 1.0 - m0
    rr = _iota((L2, L2), 0)
    cc = _iota((L2, L2), 1)
    same = (rr // L) == (cc // L)
    strict = jnp.logical_and(same, (cc % L) < (rr % L))
    incl = jnp.logical_and(same, (cc % L) <= (rr % L))
    eye2 = jnp.where(rr == cc, 1.0, 0.0)
    r128 = _iota((LANES, LANES), 0)
    c128 = _iota((LANES, LANES), 1)
    blk128 = (r128 // RWKV_HD) == (c128 // RWKV_HD)
    diag128 = r128 == c128
    fmat = jnp.where(_iota((LANES, RWKV_HD), 0) % RWKV_HD == _iota((LANES, RWKV_HD), 1), 1.0, 0.0)
    tri = _tri_incl(L)

    def bd(x):
        return jnp.concatenate([x * m0, x * m1], axis=0)

    @pl.when(c == 0)
    def _():
        for j in range(n_pairs):
            s0 = s0_ref[0, j]
            st = _xdot_r(fmat, s0, _NT)
            s_scr[j] = jnp.where(blk128, st, 0.0)

    for j in range(n_pairs):
        sl = slice(j * LANES, (j + 1) * LANES)
        R = r_ref[0, :, sl]
        LW = lw_ref[0, :, sl]
        K = k_ref[0, :, sl]
        V = v_ref[0, :, sl]
        AL = al_ref[0, :, sl]
        BE = be_ref[0, :, sl]
        b = _xdot_r(tri, LW)
        bl = b[L - 1:L, :]
        e_b = jnp.exp(b)
        e_nb = jnp.exp(-b)
        e_lb = jnp.exp(bl - b)
        at = bd(AL * jnp.exp(b - LW))
        rt = bd(R * e_b)
        bh = bd(BE * e_nb)
        kh = bd(K * e_nb)
        bt = bd(BE * e_lb)
        kt = bd(K * e_lb)
        vb = bd(V)
        pl_row = jnp.exp(bl)
        a_ab = jnp.where(strict, _hdot(at, bh, _NT), 0.0)
        a_ak = jnp.where(strict, _hdot(at, kh, _NT), 0.0)
        a_rb = jnp.where(incl, _hdot(rt, bh, _NT), 0.0)
        a_rk = jnp.where(incl, _hdot(rt, kh, _NT), 0.0)
        x = a_ab
        tm_ = eye2 + a_ab
        span = 2
        while span < L:
            x = _hdot(x, x)
            tm_ = tm_ + _hdot(tm_, x)
            span *= 2
        w = _hdot(tm_, at)
        uv = _hdot(tm_, _hdot(a_ak, vb))
        mm = jnp.where(diag128, pl_row, 0.0) + _hdot(bt, w, _TN)
        nn = _hdot(bt, uv, _TN) + _hdot(kt, vb, _TN)
        q = rt + _hdot(a_rb, w)
        yv = _hdot(a_rb, uv) + _hdot(a_rk, vb)
        s = s_scr[j]
        ybd = _hdot(q, s) + yv
        y_ref[0, :, sl] = ybd[0:L] + ybd[L:L2]
        s_new = _hdot(mm, s) + nn
        s_scr[j] = s_new

        @pl.when(c == nc - 1)
        def _():
            sf_ref[0, j] = _xdot_l(s_new, fmat, _TN)


def rwkv_scan(ops, s0, L):
    bsz, t, _ = ops[0].shape
    n_pairs = RW // LANES
    blk = pl.BlockSpec((1, L, RW), lambda b, c: (b, c, 0))
    st_spec = pl.BlockSpec((1, n_pairs, LANES, RWKV_HD), lambda b, c: (b, 0, 0, 0))
    return pl.pallas_call(
        functools.partial(_rwkv_scan_kernel, L, n_pairs),
        grid=(bsz, t // L),
        in_specs=[blk] * 6 + [st_spec],
        out_specs=[blk, st_spec],
        out_shape=[jax.ShapeDtypeStruct((bsz, t, RW), F32),
                   jax.ShapeDtypeStruct((bsz, n_pairs, LANES, RWKV_HD), F32)],
        scratch_shapes=[pltpu.VMEM((n_pairs, LANES, LANES), F32)],
        compiler_params=_cparams(("parallel", "arbitrary")),
    )(*ops, s0)


def _rwkv_post_kernel(y_ref, bo_ref, g_ref, lw_ref, lb_ref, o_ref):
    y = y_ref[...]
    mean = _seg_sum(y, RWKV_HD) * (1.0 / RWKV_HD)
    d = y - mean
    var = _seg_sum(d * d, RWKV_HD) * (1.0 / RWKV_HD)
    yn = d * lax.rsqrt(var + RWKV_GN_EPS) * lw_ref[...] + lb_ref[...]
    o_ref[...] = (yn + bo_ref[...]) * g_ref[...]


def rwkv_post(y, bonus, g, lnx_w, lnx_b, tm):
    m = y.shape[0]
    spec = pl.BlockSpec((tm, RW), lambda i: (i, 0))
    row = pl.BlockSpec((1, RW), lambda i: (0, 0))
    return pl.pallas_call(
        _rwkv_post_kernel,
        grid=(m // tm,),
        in_specs=[spec, spec, spec, row, row],
        out_specs=spec,
        out_shape=jax.ShapeDtypeStruct((m, RW), F32),
        compiler_params=_cparams(("parallel",)),
    )(y, bonus, g, lnx_w.reshape(1, RW), lnx_b.reshape(1, RW))


def rwkv_params(w, i):
    z64 = jnp.zeros((RWKV_HD, RW), F32)
    return {
        "mu": w["e_mu"][i].reshape(1, RSHIFT),
        "w0": w["e_w0"][i].reshape(1, RW),
        "wup": jnp.concatenate([w["e_w_up"][i], z64], 0).astype(BF16),
        "a0": w["e_a0"][i].reshape(1, RW),
        "aup": jnp.concatenate([z64, w["e_a_up"][i]], 0).astype(BF16),
        "gup": w["e_g_up"][i].astype(BF16),
        "k_k": w["e_k_k"][i].reshape(1, RW),
        "k_a": w["e_k_a"][i].reshape(1, RW),
        "r_k": w["e_r_k"][i].reshape(1, RW),
    }


def rwkv_mix(p_flat, prev, s0, bsz, t, L, tm, prm, lnx_w, lnx_b):
    m = bsz * t
    if t >= tm:
        prev_rows = prev.reshape(bsz, 1, RSHIFT)
    else:
        prev_rows = jnp.repeat(prev, t, axis=0)
    r, lw, k, v, al, be, g, bonus = rwkv_pre(p_flat, prev_rows, t, tm, prm)
    tp = -(-t // L) * L
    ops = [z.reshape(bsz, t, RW) for z in (r, lw, k, v, al, be)]
    if tp != t:
        ops = [jnp.pad(z, ((0, 0), (0, tp - t), (0, 0))) for z in ops]
    y, s_fin = rwkv_scan(ops, s0.reshape(bsz, RW // LANES, LANES, RWKV_HD), L)
    y = y[:, :t].reshape(m, RW)
    ya = rwkv_post(y, bonus, g, lnx_w, lnx_b, tm)
    return ya, s_fin.reshape(bsz, RW // RWKV_HD, RWKV_HD, RWKV_HD)


N_Q_HEADS = 8
N_KV_HEADS = 2
Q_PER_KV = N_Q_HEADS // N_KV_HEADS
QW = N_Q_HEADS * ATT_HD
KW = N_KV_HEADS * ATT_HD
NEG = -1e30


def _t5_bucket_np(dist):
    n = np.maximum(dist, 0)
    max_exact = N_BUCKETS // 2
    nf = np.maximum(n, 1).astype(np.float32)
    large = max_exact + (np.log(nf / np.float32(max_exact)) / np.float32(math.log(BUCKET_MAX_DIST / max_exact))
                         * np.float32(N_BUCKETS - max_exact)).astype(np.int32)
    large = np.minimum(large, N_BUCKETS - 1)
    return np.where(n < max_exact, n, large)


def _bias_kernel(rt_ref, oh_ref, o_ref):
    o_ref[...] = _xdot_l(rt_ref[...], oh_ref[...])


def rel_bias(rel_table, dist):
    bucket = _t5_bucket_np(dist).reshape(-1)
    n = bucket.shape[0]
    onehot = jnp.asarray((np.arange(N_BUCKETS)[:, None] == bucket[None, :]).astype(np.float32), BF16)
    out = pl.pallas_call(
        _bias_kernel,
        out_shape=jax.ShapeDtypeStruct((N_Q_HEADS, n), F32),
    )(rel_table.T, onehot)
    return out.reshape((N_Q_HEADS,) + dist.shape)


def _head_norm(x, w_row):
    return x * lax.rsqrt(_seg_sum(x * x, ATT_HD) * (1.0 / ATT_HD) + EPS) * w_row


def _swa_prompt_kernel(q_ref, kvc_ref, kvp_ref, qw_ref, kw_ref, bias_ref, sink_ref, o_ref, ko_ref, vo_ref):
    i = pl.program_id(1)
    qn = _head_norm(q_ref[0], qw_ref[...])
    kvc = kvc_ref[0]
    kvp = kvp_ref[0]
    kcn = _head_norm(kvc[:, 0:KW], kw_ref[...])
    kpn = _head_norm(kvp[:, 0:KW], kw_ref[...])
    vc = kvc[:, KW:2 * KW]
    kcat = jnp.concatenate([kpn, kcn], axis=0).astype(BF16)
    vcat = jnp.concatenate([kvp[:, KW:2 * KW], vc], axis=0).astype(BF16)
    qi = _iota((WINDOW, 2 * WINDOW), 0)
    kj = _iota((WINDOW, 2 * WINDOW), 1)
    dist = qi + WINDOW - kj
    valid = jnp.logical_and(jnp.logical_and(dist >= 0, dist < WINDOW), jnp.logical_or(kj >= WINDOW, i > 0))
    lane = _iota((1, LANES), 1)
    masks = (jnp.where(lane < ATT_HD, 1.0, 0.0), jnp.where(lane < ATT_HD, 0.0, 1.0))
    tiles = [None] * (QW // LANES)
    for h in range(N_Q_HEADS):
        kv, hh, jq = h // Q_PER_KV, h % 2, h // 2
        qt = qn[:, jq * LANES:(jq + 1) * LANES]
        if hh != kv:
            qt = pltpu.roll(qt, ATT_HD, 1)
        qm = (qt * masks[kv]).astype(BF16)
        s = _dg(qm, kcat, _NT) * (ATT_HD ** -0.5) + bias_ref[h]
        logits = jnp.where(valid, s, NEG)
        sink = sink_ref[h:h + 1, 0:1]
        mx = jnp.maximum(jnp.max(logits, axis=-1, keepdims=True), sink)
        pr = jnp.exp(logits - mx)
        den = jnp.sum(pr, axis=-1, keepdims=True) + jnp.exp(sink - mx)
        o = _dg((pr / den).astype(BF16), vcat, _NN)
        if hh != kv:
            o = pltpu.roll(o, ATT_HD, 1)
        o = o * masks[hh]
        tiles[jq] = o if tiles[jq] is None else tiles[jq] + o
    for jq in range(QW // LANES):
        o_ref[0, :, jq * LANES:(jq + 1) * LANES] = tiles[jq]
    ko_ref[0] = kcn
    vo_ref[0] = vc


def swa_prompt(q, kv, bsz, t, q_norm, k_norm, bias, sinks):
    nb = t // WINDOW
    q3 = q.reshape(bsz, t, QW)
    kv3 = kv.reshape(bsz, t, 2 * KW)
    o, ko, vo = pl.pallas_call(
        _swa_prompt_kernel,
        grid=(bsz, nb),
        in_specs=[pl.BlockSpec((1, WINDOW, QW), lambda b, i: (b, i, 0)),
                  pl.BlockSpec((1, WINDOW, 2 * KW), lambda b, i: (b, i, 0)),
                  pl.BlockSpec((1, WINDOW, 2 * KW), lambda b, i: (b, jnp.maximum(i - 1, 0), 0)),
                  pl.BlockSpec((1, QW), lambda b, i: (0, 0)),
                  pl.BlockSpec((1, KW), lambda b, i: (0, 0)),
                  pl.BlockSpec((N_Q_HEADS, WINDOW, 2 * WINDOW), lambda b, i: (0, 0, 0)),
                  pl.BlockSpec((N_Q_HEADS, LANES), lambda b, i: (0, 0))],
        out_specs=[pl.BlockSpec((1, WINDOW, QW), lambda b, i: (b, i, 0)),
                   pl.BlockSpec((1, WINDOW, KW), lambda b, i: (b, 0, 0)),
                   pl.BlockSpec((1, WINDOW, KW), lambda b, i: (b, 0, 0))],
        out_shape=[jax.ShapeDtypeStruct((bsz, t, QW), F32),
                   jax.ShapeDtypeStruct((bsz, WINDOW, KW), F32),
                   jax.ShapeDtypeStruct((bsz, WINDOW, KW), F32)],
        compiler_params=_cparams(("parallel", "arbitrary")),
    )(q3, kv3, kv3, jnp.tile(q_norm, N_Q_HEADS).reshape(1, QW), jnp.tile(k_norm, N_KV_HEADS).reshape(1, KW),
      bias, jnp.broadcast_to(sinks[:, None], (N_Q_HEADS, LANES)))
    return o.reshape(bsz * t, QW), ko, vo


DEC_TP = 8


def _swa_decode_kernel(nbt, t_real, q_ref, kv_ref, ck_ref, cv_ref, qw_ref, kw_ref, bc_ref, bn_ref, sink_ref,
                       o_ref, ko_ref, vo_ref):
    rows = Q_PER_KV * DEC_TP
    tq = _iota((rows, WINDOW), 0) % DEC_TP
    valid_c = _iota((rows, WINDOW), 1) > tq
    jn = _iota((rows, DEC_TP), 1)
    valid_n = jnp.logical_and(jn <= _iota((rows, DEC_TP), 0) % DEC_TP, jn < t_real)
    lane = _iota((1, LANES), 1)
    masks = (jnp.where(lane < ATT_HD, 1.0, 0.0), jnp.where(lane < ATT_HD, 0.0, 1.0))
    row8 = _iota((DEC_TP, 1), 0)
    for b in range(nbt):
        qn = _head_norm(q_ref[b], qw_ref[...])
        kvn = kv_ref[b]
        knew = _head_norm(kvn[:, 0:KW], kw_ref[...])
        vnew = kvn[:, KW:2 * KW]
        kc = ck_ref[b]
        vc = cv_ref[b]
        tiles = [None] * (QW // LANES)
        for kv in range(N_KV_HEADS):
            pieces = []
            for g in range(Q_PER_KV):
                h = kv * Q_PER_KV + g
                qt = qn[:, (h // 2) * LANES:(h // 2 + 1) * LANES]
                if h % 2 != kv:
                    qt = pltpu.roll(qt, ATT_HD, 1)
                pieces.append(qt * masks[kv])
            qg = jnp.concatenate(pieces, axis=0)
            l_c = jnp.where(valid_c, _bdot(qg, kc, _NT) * (ATT_HD ** -0.5) + bc_ref[kv], NEG)
            l_n = jnp.where(valid_n, _bdot(qg, knew, _NT) * (ATT_HD ** -0.5) + bn_ref[kv][:, 0:DEC_TP], NEG)
            sink = sink_ref[kv][:, 0:1]
            mx = jnp.maximum(jnp.maximum(jnp.max(l_c, axis=-1, keepdims=True),
                                         jnp.max(l_n, axis=-1, keepdims=True)), sink)
            p_c = jnp.exp(l_c - mx)
            p_n = jnp.exp(l_n - mx)
            den = (jnp.sum(p_c, axis=-1, keepdims=True) + jnp.sum(p_n, axis=-1, keepdims=True)
                   + jnp.exp(sink - mx))
            o = _bdot(p_c / den, vc) + _bdot(p_n / den, vnew)
            for g in range(Q_PER_KV):
                h = kv * Q_PER_KV + g
                piece = o[g * DEC_TP:(g + 1) * DEC_TP]
                if h % 2 != kv:
                    piece = pltpu.roll(piece, ATT_HD, 1)
                piece = piece * masks[h % 2]
                tiles[h // 2] = piece if tiles[h // 2] is None else tiles[h // 2] + piece
        for jq in range(QW // LANES):
            o_ref[b, :, jq * LANES:(jq + 1) * LANES] = tiles[jq]
        for cache, new, out in ((kc, knew, ko_ref), (vc, vnew, vo_ref)):
            shifted = pltpu.roll(cache, WINDOW - t_real, 0)
            new_r = pltpu.roll(new, DEC_TP - t_real, 0)
            out[b, 0:WINDOW - DEC_TP] = shifted[0:WINDOW - DEC_TP]
            out[b, WINDOW - DEC_TP:WINDOW] = jnp.where(row8 >= DEC_TP - t_real, new_r,
                                                       shifted[WINDOW - DEC_TP:WINDOW])


def swa_decode(q, kv, cache_k, cache_v, bsz, t, q_norm, k_norm, rel_table, sinks, nbt):
    pad = ((0, 0), (0, DEC_TP - t), (0, 0))
    q3 = jnp.pad(q.reshape(bsz, t, QW), pad)
    kv3 = jnp.pad(kv.reshape(bsz, t, 2 * KW), pad)
    kpos = np.concatenate([np.arange(WINDOW) - WINDOW, np.arange(DEC_TP)])
    dist = np.arange(DEC_TP)[:, None] - kpos[None, :]
    bias = rel_bias(rel_table, dist)
    rows = Q_PER_KV * DEC_TP
    bias = bias.reshape(N_KV_HEADS, rows, WINDOW + DEC_TP)
    bias_c = bias[:, :, :WINDOW]
    bias_n = jnp.pad(bias[:, :, WINDOW:], ((0, 0), (0, 0), (0, LANES - DEC_TP)))
    sink_rows = jnp.broadcast_to(sinks.reshape(N_KV_HEADS, Q_PER_KV, 1, 1),
                                 (N_KV_HEADS, Q_PER_KV, DEC_TP, LANES)).reshape(N_KV_HEADS, rows, LANES)
    full3 = lambda shape: pl.BlockSpec(shape, lambda i: (0, 0, 0))
    o, ko, vo = pl.pallas_call(
        functools.partial(_swa_decode_kernel, nbt, t),
        grid=(bsz // nbt,),
        in_specs=[pl.BlockSpec((nbt, DEC_TP, QW), lambda i: (i, 0, 0)),
                  pl.BlockSpec((nbt, DEC_TP, 2 * KW), lambda i: (i, 0, 0)),
                  pl.BlockSpec((nbt, WINDOW, KW), lambda i: (i, 0, 0)),
                  pl.BlockSpec((nbt, WINDOW, KW), lambda i: (i, 0, 0)),
                  pl.BlockSpec((1, QW), lambda i: (0, 0)),
                  pl.BlockSpec((1, KW), lambda i: (0, 0)),
                  full3((N_KV_HEADS, rows, WINDOW)),
                  full3((N_KV_HEADS, rows, LANES)),
                  full3((N_KV_HEADS, rows, LANES))],
        out_specs=[pl.BlockSpec((nbt, DEC_TP, QW), lambda i: (i, 0, 0)),
                   pl.BlockSpec((nbt, WINDOW, KW), lambda i: (i, 0, 0)),
                   pl.BlockSpec((nbt, WINDOW, KW), lambda i: (i, 0, 0))],
        out_shape=[jax.ShapeDtypeStruct((bsz, DEC_TP, QW), F32),
                   jax.ShapeDtypeStruct((bsz, WINDOW, KW), F32),
                   jax.ShapeDtypeStruct((bsz, WINDOW, KW), F32)],
        compiler_params=_cparams(("parallel",)),
    )(q3, kv3, cache_k, cache_v, jnp.tile(q_norm, N_Q_HEADS).reshape(1, QW),
      jnp.tile(k_norm, N_KV_HEADS).reshape(1, KW), bias_c, bias_n, sink_rows)
    return o[:, :t].reshape(bsz * t, QW), ko, vo


S5_G = 32
S5_W = S5_G * S5_P
S5_CP = S5_CHUNK * S5_P
S5_PK = 2 * S5_N
S5_HW = S5_G * S5_PK


def _s5_prep_kernel(t_eff, a_ref, ldt_ref, b1_ref, b2_ref, c1_ref, c2_ref, kmat_ref, kb_ref, kct_ref, al_ref):
    L = S5_CHUNK
    ar2 = a_ref[0, 0:1, :]
    ai2 = a_ref[0, 1:2, :]
    step = jnp.exp(ldt_ref[0])
    mi = _iota((3 * SUBLANES, S5_PK), 0).astype(F32)
    mag = jnp.exp(mi * (step * ar2))
    ang = mi * (step * ai2)
    pwa = mag * jnp.cos(ang)
    pwb = mag * jnp.sin(ang)
    abr = pwa[1:2]
    abi = pwb[1:2]
    den = ar2 * ar2 + ai2 * ai2
    fa = ((abr - 1.0) * ar2 + abi * ai2) / den
    fb = (abi * ar2 - (abr - 1.0) * ai2) / den
    b1 = b1_ref[0]
    b2 = b2_ref[0]
    bp1 = b1 * fa + b2 * fb
    bp2 = b2 * fa - b1 * fb
    c1 = c1_ref[0]
    c2 = c2_ref[0]
    cpow = [c1 * pwa[m:m + 1] + c2 * pwb[m:m + 1] for m in range(L + 1)]
    kern_t = _hdot(bp1, jnp.concatenate(cpow[0:L], axis=0), _NT)
    lane = _iota((S5_P, S5_CP), 1)
    blocks = [kern_t]
    for i in range(1, L):
        blocks.append(jnp.where(lane >= S5_P * i, pltpu.roll(kern_t, S5_P * i, 1), 0.0))
    kmat_ref[0] = jnp.concatenate(blocks, axis=0).astype(BF16)
    kb = []
    for i in range(L):
        e = max(t_eff - 1 - i, 0)
        kb.append(bp1 * pwa[e:e + 1] + bp2 * pwb[e:e + 1])
    kb_ref[0] = jnp.concatenate(kb, axis=0).astype(BF16)
    kct_ref[0] = jnp.concatenate(cpow[1:L + 1], axis=0).astype(BF16)
    sgn = jnp.where(_iota((1, S5_PK), 1) < S5_N, -1.0, 1.0)
    al_ref[0] = jnp.concatenate([pwa[t_eff:t_eff + 1], sgn * pwb[t_eff:t_eff + 1]], axis=0)


def s5_prep(w, i, t_eff):
    dup = lambda z: jnp.concatenate([z, z], axis=-1)
    a = jnp.stack([dup(w["o_a_re"][i]), dup(w["o_a_im"][i])], axis=1)
    ldt = jnp.broadcast_to(w["o_log_dt"][i][:, None, None], (S5_G, 1, S5_PK))
    bt_re = jnp.swapaxes(w["o_b_re"][i], 1, 2)
    bt_im = jnp.swapaxes(w["o_b_im"][i], 1, 2)
    b1 = jnp.concatenate([bt_re, bt_im], -1)
    b2 = jnp.concatenate([-bt_im, bt_re], -1)
    c_re, c_im = w["o_c_re"][i], w["o_c_im"][i]
    c1 = jnp.concatenate([c_re, -c_im], -1)
    c2 = jnp.concatenate([-c_im, -c_re], -1)
    g3 = lambda r, c: pl.BlockSpec((1, r, c), lambda g: (g, 0, 0))
    return pl.pallas_call(
        functools.partial(_s5_prep_kernel, t_eff),
        grid=(S5_G,),
        in_specs=[g3(2, S5_PK), g3(1, S5_PK), g3(S5_P, S5_PK), g3(S5_P, S5_PK), g3(S5_P, S5_PK), g3(S5_P, S5_PK)],
        out_specs=[g3(S5_CP, S5_CP), g3(S5_CP, S5_PK), g3(S5_CP, S5_PK), g3(2, S5_PK)],
        out_shape=[jax.ShapeDtypeStruct((S5_G, S5_CP, S5_CP), BF16),
                   jax.ShapeDtypeStruct((S5_G, S5_CP, S5_PK), BF16),
                   jax.ShapeDtypeStruct((S5_G, S5_CP, S5_PK), BF16),
                   jax.ShapeDtypeStruct((S5_G, 2, S5_PK), F32)],
        compiler_params=_cparams(("parallel",)),
    )(a, ldt, b1, b2, c1, c2)


def _s5_e_kernel(u_ref, kb_ref, e_ref):
    e_ref[...] = _bdot(u_ref[0], kb_ref[0])


def _s5_swap(h):
    n = h.shape[-1]
    lane = _iota(h.shape, 1)
    return jnp.where(lane % S5_PK < S5_N, pltpu.roll(h, n - S5_N, 1), pltpu.roll(h, S5_N, 1))


def _s5_scan_kernel(cg, e_ref, h0_ref, ala_ref, alb_ref, hp_ref, hf_ref, h_scr):
    @pl.when(pl.program_id(0) == 0)
    def _():
        h_scr[...] = h0_ref[...]

    ala = ala_ref[...]
    alb = alb_ref[...]

    def body(c, h):
        hp_ref[c] = h
        return ala * h + alb * _s5_swap(h) + e_ref[c]

    h = lax.fori_loop(0, cg, body, h_scr[...])
    h_scr[...] = h
    hf_ref[...] = h


def _s5_y_kernel(u_ref, hp_ref, kmat_ref, kct_ref, y_ref):
    y_ref[0] = _bdot(u_ref[0], kmat_ref[0]) + _bdot(hp_ref[...], kct_ref[0], _NT)


def s5_core(u, h0, bsz, t, prep):
    kmat, kb, kct, al = prep
    L = S5_CHUNK
    tp = -(-t // L) * L
    u3 = u.reshape(bsz, t, S5_W)
    if tp != t:
        u3 = jnp.pad(u3, ((0, 0), (0, tp - t), (0, 0)))
    nc = tp // L
    rows = nc * bsz
    ug = u3.reshape(bsz, nc, L, S5_G, S5_P).transpose(3, 1, 0, 2, 4).reshape(S5_G, rows, S5_CP)
    e = pl.pallas_call(
        _s5_e_kernel,
        grid=(S5_G,),
        in_specs=[pl.BlockSpec((1, rows, S5_CP), lambda g: (g, 0, 0)),
                  pl.BlockSpec((1, S5_CP, S5_PK), lambda g: (g, 0, 0))],
        out_specs=pl.BlockSpec((rows, S5_PK), lambda g: (0, g)),
        out_shape=jax.ShapeDtypeStruct((rows, S5_HW), F32),
        compiler_params=_cparams(("parallel",)),
    )(ug, kb)
    cg = math.gcd(nc, 64)
    ala = al[:, 0, :].reshape(1, S5_HW)
    alb = al[:, 1, :].reshape(1, S5_HW)
    hp, hf = pl.pallas_call(
        functools.partial(_s5_scan_kernel, cg),
        grid=(nc // cg,),
        in_specs=[pl.BlockSpec((cg, bsz, S5_HW), lambda i: (i, 0, 0)),
                  pl.BlockSpec((bsz, S5_HW), lambda i: (0, 0)),
                  pl.BlockSpec((1, S5_HW), lambda i: (0, 0)),
                  pl.BlockSpec((1, S5_HW), lambda i: (0, 0))],
        out_specs=[pl.BlockSpec((cg, bsz, S5_HW), lambda i: (i, 0, 0)),
                   pl.BlockSpec((bsz, S5_HW), lambda i: (0, 0))],
        out_shape=[jax.ShapeDtypeStruct((nc, bsz, S5_HW), F32),
                   jax.ShapeDtypeStruct((bsz, S5_HW), F32)],
        scratch_shapes=[pltpu.VMEM((bsz, S5_HW), F32)],
        compiler_params=_cparams(("arbitrary",)),
    )(e.reshape(nc, bsz, S5_HW), h0, ala, alb)
    yg = pl.pallas_call(
        _s5_y_kernel,
        grid=(S5_G,),
        in_specs=[pl.BlockSpec((1, rows, S5_CP), lambda g: (g, 0, 0)),
                  pl.BlockSpec((rows, S5_PK), lambda g: (0, g)),
                  pl.BlockSpec((1, S5_CP, S5_CP), lambda g: (g, 0, 0)),
                  pl.BlockSpec((1, S5_CP, S5_PK), lambda g: (g, 0, 0))],
        out_specs=pl.BlockSpec((1, rows, S5_CP), lambda g: (g, 0, 0)),
        out_shape=jax.ShapeDtypeStruct((S5_G, rows, S5_CP), F32),
        compiler_params=_cparams(("parallel",)),
    )(ug, hp.reshape(rows, S5_HW), kmat, kct)
    y = yg.reshape(S5_G, nc, bsz, L, S5_P).transpose(2, 1, 3, 0, 4).reshape(bsz, tp, S5_W)
    return y[:, :t].reshape(bsz * t, S5_W), hf


def _s5_post_kernel(y_ref, u_ref, d_ref, gw_ref, gb_ref, o_ref):
    x = y_ref[...] + d_ref[...] * u_ref[...]
    z = 0.5 * x * (1.0 + jnp.tanh(math.sqrt(2.0 / math.pi) * (x + 0.044715 * (x * x * x))))
    o_ref[...] = z * _sigmoid(_dg(z.astype(BF16), gw_ref[...], _NN) + gb_ref[...])


def s5_post(y, u, d, glu_w_bf16, glu_b, tm):
    m = y.shape[0]
    spec = pl.BlockSpec((tm, S5_W), lambda i: (i, 0))
    row = pl.BlockSpec((1, S5_W), lambda i: (0, 0))
    return pl.pallas_call(
        _s5_post_kernel,
        grid=(m // tm,),
        in_specs=[spec, spec, row, pl.BlockSpec((S5_W, S5_W), lambda i: (0, 0)), row],
        out_specs=spec,
        out_shape=jax.ShapeDtypeStruct((m, S5_W), F32),
        compiler_params=_cparams(("parallel",)),
    )(y, u, d.reshape(1, S5_W), glu_w_bf16, glu_b.reshape(1, S5_W))


def s5_mix(u, h_re, h_im, bsz, t, prep, d, glu_w_bf16, glu_b, tm):
    h0 = jnp.concatenate([h_re, h_im], axis=-1).reshape(bsz, S5_HW)
    y, hf = s5_core(u, h0, bsz, t, prep)
    out = s5_post(y, u, d, glu_w_bf16, glu_b, tm)
    hf = hf.reshape(bsz, S5_G, 2, S5_N)
    return out, hf[:, :, 0], hf[:, :, 1]


GLA_DK = 64
GLA_DV = 128
GLA_HEADS = 4
GLA_KW = GLA_HEADS * GLA_DK
GLA_VW = GLA_HEADS * GLA_DV
GLA_PW = 2 * GLA_KW + 2 * GLA_VW + LANES


def _gla_kernel(L, t_real, p_ref, aup_ref, ab_ref, nw_ref, s0_ref, y_ref, sf_ref, s_scr):
    c = pl.program_id(1)
    nc = pl.num_programs(1)
    n_pairs = GLA_KW // LANES

    @pl.when(c == 0)
    def _():
        s_scr[...] = s0_ref[0]

    p = p_ref[0]
    q = p[:, 0:GLA_KW]
    k = p[:, GLA_KW:2 * GLA_KW]
    v = p[:, 2 * GLA_KW:2 * GLA_KW + GLA_VW]
    g = p[:, 2 * GLA_KW + GLA_VW:2 * GLA_KW + 2 * GLA_VW]
    ad = p[:, 2 * GLA_KW + 2 * GLA_VW:]
    z = _bdot(ad, aup_ref[...]) + ab_ref[...]
    gk = -_softplus(-z) * (1.0 / GLA_GATE_NORM)
    if t_real % L != 0:
        tok = c * L + _iota((L, 1), 0)
        gk = jnp.where(tok < t_real, gk, 0.0)
    b = _xdot_r(_tri_incl(L), gk)
    bl = b[L - 1:L, :]
    qd = q * (GLA_DK ** -0.5) * jnp.exp(b)
    kh = k * jnp.exp(-b)
    kt = k * jnp.exp(bl - b)
    pl_row = jnp.exp(bl)
    lane = _iota((1, LANES), 1)
    masks = (jnp.where(lane < GLA_DK, 1.0, 0.0), jnp.where(lane < GLA_DK, 0.0, 1.0))
    incl = _iota((L, L), 1) <= _iota((L, L), 0)
    diag128 = _iota((LANES, LANES), 0) == _iota((LANES, LANES), 1)
    ones128 = jnp.ones((LANES, LANES), BF16)
    for j in range(n_pairs):
        sl = slice(j * LANES, (j + 1) * LANES)
        s = s_scr[j]
        kv_rows = []
        for hh in range(2):
            h = 2 * j + hh
            hs = slice(h * GLA_DV, (h + 1) * GLA_DV)
            qm = qd[:, sl] * masks[hh]
            attn = jnp.where(incl, _bdot(qm, kh[:, sl], _NT), 0.0)
            vh = v[:, hs]
            o = _bdot(attn, vh) + _bdot(qm, s)
            of = o * lax.rsqrt(jnp.mean(o * o, axis=-1, keepdims=True) + EPS) * nw_ref[...]
            y_ref[0, :, hs] = of * _silu(g[:, hs])
            kv = _bdot(kt[:, sl], vh, _TN)
            kv_rows.append(kv[hh * GLA_DK:(hh + 1) * GLA_DK])
        pcol = _xdot_l(jnp.where(diag128, pl_row[:, sl], 0.0), ones128)
        s_new = pcol * s + jnp.concatenate(kv_rows, axis=0)
        s_scr[j] = s_new

        @pl.when(c == nc - 1)
        def _():
            sf_ref[0, j] = s_new


def gla_mix(p_gla, s0, bsz, t, L, aup_pad, a_b, norm_w):
    tp = -(-t // L) * L
    p3 = p_gla.reshape(bsz, t, GLA_PW)
    if tp != t:
        p3 = jnp.pad(p3, ((0, 0), (0, tp - t), (0, 0)))
    n_pairs = GLA_KW // LANES
    st_spec = pl.BlockSpec((1, n_pairs, LANES, LANES), lambda b, c: (b, 0, 0, 0))
    y, s_fin = pl.pallas_call(
        functools.partial(_gla_kernel, L, t),
        grid=(bsz, tp // L),
        in_specs=[pl.BlockSpec((1, L, GLA_PW), lambda b, c: (b, c, 0)),
                  pl.BlockSpec((LANES, GLA_KW), lambda b, c: (0, 0)),
                  pl.BlockSpec((1, GLA_KW), lambda b, c: (0, 0)),
                  pl.BlockSpec((1, GLA_DV), lambda b, c: (0, 0)),
                  st_spec],
        out_specs=[pl.BlockSpec((1, L, GLA_VW), lambda b, c: (b, c, 0)), st_spec],
        out_shape=[jax.ShapeDtypeStruct((bsz, tp, GLA_VW), F32),
                   jax.ShapeDtypeStruct((bsz, n_pairs, LANES, LANES), F32)],
        scratch_shapes=[pltpu.VMEM((n_pairs, LANES, LANES), F32)],
        compiler_params=_cparams(("parallel", "arbitrary")),
    )(p3, aup_pad, a_b.reshape(1, GLA_KW), norm_w.reshape(1, GLA_DV),
      s0.reshape(bsz, n_pairs, LANES, LANES))
    return y[:, :t].reshape(bsz * t, GLA_VW), s_fin.reshape(bsz, GLA_HEADS, GLA_DK, GLA_DV)


GLA_LR = 16
D_FF_CHUNK = 1408
D_FF_EXPERT_CHUNK = 896


def _prepare_weights(w):
    win = w["e_w_in"][0]
    w_gla = jnp.pad(win[:, RSHIFT:], ((0, 0), (0, LANES - GLA_LR)))
    wo = w["o_w_in"][0]
    return {
        "e_w_rwkv": win[:, :RSHIFT].astype(BF16),
        "e_w_gla": w_gla.astype(BF16),
        "rwkv": rwkv_params(w, 0),
        "gla_aup": jnp.pad(w["e_gla_a_up"][0], ((0, LANES - GLA_LR), (0, 0))).astype(BF16),
        "e_wo_a": w["e_w_out"][0][:RW].astype(BF16),
        "e_wo_b": w["e_w_out"][0][RW:].astype(BF16),
        "ff_w1": w["e_ff_w1"][0].astype(BF16),
        "ff_w3": w["e_ff_w3"][0].astype(BF16),
        "ff_w2": w["e_ff_w2"][0].astype(BF16),
        "o_w_q": wo[:, :QW].astype(BF16),
        "o_w_kv": wo[:, QW:QW + 2 * KW].astype(BF16),
        "o_w_u": wo[:, QW + 2 * KW:].astype(BF16),
        "glu_w": w["o_glu_w"][0].astype(BF16),
        "o_wo_a": w["o_w_out"][0][:QW].astype(BF16),
        "o_wo_b": w["o_w_out"][0][QW:].astype(BF16),
        "moe_w1": w["o_moe_w1"][0].astype(BF16),
        "moe_w3": w["o_moe_w3"][0].astype(BF16),
        "moe_w2": w["o_moe_w2"][0].astype(BF16),
    }


def _trunk(x3, st, w, pw, tm, chunk, s5_prep_t, prompt_bias):
    bsz, t, d = x3.shape
    m = bsz * t
    x = x3.reshape(m, d)
    p_r, p_g = norm_proj(x, w["e_norm1"][0], [pw["e_w_rwkv"], pw["e_w_gla"]], tm)
    ya, s_rwkv = rwkv_mix(p_r, st["shift"], st["rwkv"], bsz, t, chunk, tm, pw["rwkv"],
                          w["e_lnx_w"][0], w["e_lnx_b"][0])
    s_shift = p_r.reshape(bsz, t, RSHIFT)[:, -1]
    yb, s_gla = gla_mix(p_g, st["gla"], bsz, t, chunk, pw["gla_aup"], w["e_gla_a_b"][0], w["e_gla_norm"][0])
    x = out_proj(x, ya, yb, pw["e_wo_a"], pw["e_wo_b"], tm)
    x = ffn(x, w["e_norm2"][0], pw["ff_w1"], pw["ff_w3"], pw["ff_w2"], tm, D_FF_CHUNK)
    q, kv, u = norm_proj(x, w["o_norm1"][0], [pw["o_w_q"], pw["o_w_kv"], pw["o_w_u"]], tm)
    if st["win_k"] is None:
        yc, nk, nv = swa_prompt(q, kv, bsz, t, w["o_q_norm"][0], w["o_k_norm"][0], prompt_bias, w["o_sinks"][0])
    else:
        yc, nk, nv = swa_decode(q, kv, st["win_k"].reshape(bsz, WINDOW, KW), st["win_v"].reshape(bsz, WINDOW, KW),
                                bsz, t, w["o_q_norm"][0], w["o_k_norm"][0], w["rel_table"], w["o_sinks"][0], 8)
    yd, s5r, s5i = s5_mix(u, st["s5_re"], st["s5_im"], bsz, t, s5_prep_t, w["o_d"][0], pw["glu_w"],
                          w["o_glu_b"][0], tm)
    x = out_proj(x, yc, yd, pw["o_wo_a"], pw["o_wo_b"], tm)
    hn, gates = router(x, w["o_norm2"][0], w["o_router_w"][0], w["o_router_b"][0], tm)
    x = moe(x, hn, gates, pw["moe_w1"], pw["moe_w3"], pw["moe_w2"], tm, D_FF_EXPERT_CHUNK)
    kv_shape = (bsz, WINDOW, N_KV_HEADS, ATT_HD)
    return (x.reshape(bsz, t, d), s_rwkv[None], s_shift[None], s_gla[None], nk.reshape(kv_shape)[None],
            nv.reshape(kv_shape)[None], s5r[None], s5i[None])


def kernel(x_prompt, x_sample, state_rwkv, state_shift, state_gla, cache_win_k, cache_win_v, state_s5_re,
           state_s5_im, rel_table, e_norm1, e_w_in, e_mu, e_w0, e_w_up, e_a0, e_a_up, e_g_up, e_k_k, e_k_a, e_r_k,
           e_lnx_w, e_lnx_b, e_gla_a_up, e_gla_a_b, e_gla_norm, e_w_out, e_norm2, e_ff_w1, e_ff_w3, e_ff_w2,
           o_norm1, o_w_in, o_q_norm, o_k_norm, o_sinks, o_a_re, o_a_im, o_log_dt, o_b_re, o_b_im, o_c_re, o_c_im,
           o_d, o_glu_w, o_glu_b, o_w_out, o_norm2, o_router_w, o_router_b, o_moe_w1, o_moe_w3, o_moe_w2):
    w = dict(rel_table=rel_table, e_norm1=e_norm1, e_w_in=e_w_in, e_mu=e_mu, e_w0=e_w0, e_w_up=e_w_up, e_a0=e_a0,
             e_a_up=e_a_up, e_g_up=e_g_up, e_k_k=e_k_k, e_k_a=e_k_a, e_r_k=e_r_k, e_lnx_w=e_lnx_w, e_lnx_b=e_lnx_b,
             e_gla_a_up=e_gla_a_up, e_gla_a_b=e_gla_a_b, e_gla_norm=e_gla_norm, e_w_out=e_w_out, e_norm2=e_norm2,
             e_ff_w1=e_ff_w1, e_ff_w3=e_ff_w3, e_ff_w2=e_ff_w2, o_norm1=o_norm1, o_w_in=o_w_in, o_q_norm=o_q_norm,
             o_k_norm=o_k_norm, o_sinks=o_sinks, o_a_re=o_a_re, o_a_im=o_a_im, o_log_dt=o_log_dt, o_b_re=o_b_re,
             o_b_im=o_b_im, o_c_re=o_c_re, o_c_im=o_c_im, o_d=o_d, o_glu_w=o_glu_w, o_glu_b=o_glu_b,
             o_w_out=o_w_out, o_norm2=o_norm2, o_router_w=o_router_w, o_router_b=o_router_b, o_moe_w1=o_moe_w1,
             o_moe_w3=o_moe_w3, o_moe_w2=o_moe_w2)
    pw = _prepare_weights(w)
    bp, tp, _ = x_prompt.shape
    bs, ts, _ = x_sample.shape
    qi = np.arange(WINDOW)[:, None]
    kj = np.arange(2 * WINDOW)[None, :]
    prompt_bias = rel_bias(rel_table, qi + WINDOW - kj)
    zeros = lambda *shape: jnp.zeros(shape, F32)
    st_p = {"rwkv": zeros(bp, RW // RWKV_HD, RWKV_HD, RWKV_HD), "shift": zeros(bp, RSHIFT),
            "gla": zeros(bp, GLA_HEADS, GLA_DK, GLA_DV), "win_k": None, "win_v": None,
            "s5_re": zeros(bp, S5_G, S5_N), "s5_im": zeros(bp, S5_G, S5_N)}
    st_s = {"rwkv": state_rwkv[0], "shift": state_shift[0], "gla": state_gla[0], "win_k": cache_win_k[0],
            "win_v": cache_win_v[0], "s5_re": state_s5_re[0], "s5_im": state_s5_im[0]}
    out_p = _trunk(x_prompt, st_p, w, pw, 512, 64, s5_prep(w, 0, S5_CHUNK), prompt_bias)
    out_s = _trunk(x_sample, st_s, w, pw, bs * ts, 8, s5_prep(w, 0, ts), None)
    res = [out_p[0], out_s[0]]
    for a, b in zip(out_p[1:], out_s[1:]):
        res += [a, b]
    return tuple(res)
```

```python
import functools
import math

import jax
import jax.numpy as jnp
import numpy as np
from jax import lax
from jax.experimental import pallas as pl
from jax.experimental.pallas import tpu as pltpu

F32 = jnp.float32
BF16 = jnp.bfloat16

LANES = 128
SUBLANES = 8
VMEM_LIMIT_BYTES = 56 * 1024 * 1024

EPS = 1e-6
RWKV_HD = 64
RWKV_GN_EPS = 64e-5
GLA_GATE_NORM = 16.0
ATT_HD = 64
WINDOW = 128
N_BUCKETS = 32
BUCKET_MAX_DIST = 128
S5_P = 16
S5_N = 64
S5_CHUNK = 16
TOP_K = 2


def _cparams(sem):
    return pltpu.CompilerParams(dimension_semantics=sem, vmem_limit_bytes=VMEM_LIMIT_BYTES)


_NN = (((1,), (0,)), ((), ()))
_NT = (((1,), (1,)), ((), ()))
_TN = (((0,), (0,)), ((), ()))


def _dg(a, b, dims):
    return lax.dot_general(a, b, dims, preferred_element_type=F32)


def _bdot(a, b, dims=_NN):
    return _dg(a.astype(BF16), b.astype(BF16), dims)


def _split(a, n):
    terms = []
    r = a
    for _ in range(n):
        t = r.astype(BF16)
        terms.append(t)
        r = r - t.astype(F32)
    return terms


def _hdot(a, b, dims=_NN):
    a0, a1 = _split(a, 2)
    b0, b1 = _split(b, 2)
    return _dg(a0, b0, dims) + (_dg(a0, b1, dims) + _dg(a1, b0, dims))


def _xdot_l(a, e, dims=_NN):
    e = e.astype(BF16)
    a0, a1, a2 = _split(a, 3)
    return _dg(a0, e, dims) + (_dg(a1, e, dims) + _dg(a2, e, dims))


def _xdot_r(e, b, dims=_NN):
    e = e.astype(BF16)
    b0, b1, b2 = _split(b, 3)
    return _dg(e, b0, dims) + (_dg(e, b1, dims) + _dg(e, b2, dims))


def _iota(shape, axis):
    return lax.broadcasted_iota(jnp.int32, shape, axis)


def _seg_ones(n, seg):
    r = _iota((n, n), 0) // seg
    c = _iota((n, n), 1) // seg
    return jnp.where(r == c, 1.0, 0.0).astype(BF16)


def _seg_sum(x, seg):
    n = x.shape[-1]
    return _xdot_l(x, _seg_ones(n, seg))


def _sigmoid(x):
    return 1.0 / (1.0 + jnp.exp(-x))


def _silu(x):
    return x * _sigmoid(x)


def _softplus(x):
    return jnp.maximum(x, 0.0) + jnp.log(1.0 + jnp.exp(-jnp.abs(x)))


def _tri_incl(n):
    r = _iota((n, n), 0)
    c = _iota((n, n), 1)
    return jnp.where(c <= r, 1.0, 0.0).astype(BF16)


def _rms(x, g):
    return x * lax.rsqrt(jnp.mean(x * x, axis=-1, keepdims=True) + EPS) * g


def _norm_proj_kernel(n_w, x_ref, g_ref, *refs):
    xn = _rms(x_ref[...], g_ref[...]).astype(BF16)
    for w_ref, o_ref in zip(refs[:n_w], refs[n_w:]):
        o_ref[...] = _dg(xn, w_ref[...], _NN)


def norm_proj(x, g, ws_bf16, tm):
    m, d = x.shape
    return pl.pallas_call(
        functools.partial(_norm_proj_kernel, len(ws_bf16)),
        grid=(m // tm,),
        in_specs=[pl.BlockSpec((tm, d), lambda i: (i, 0)),
                  pl.BlockSpec((1, d), lambda i: (0, 0))]
                 + [pl.BlockSpec(w.shape, lambda i: (0, 0)) for w in ws_bf16],
        out_specs=[pl.BlockSpec((tm, w.shape[1]), lambda i: (i, 0)) for w in ws_bf16],
        out_shape=[jax.ShapeDtypeStruct((m, w.shape[1]), F32) for w in ws_bf16],
        compiler_params=_cparams(("parallel",)), name="norm_proj",
    )(x, g.reshape(1, d), *ws_bf16)


def _out_proj_kernel(x_ref, ya_ref, yb_ref, wa_ref, wb_ref, o_ref):
    o_ref[...] = (x_ref[...] + _dg(ya_ref[...].astype(BF16), wa_ref[...], _NN)
                  + _dg(yb_ref[...].astype(BF16), wb_ref[...], _NN))


def out_proj(x, ya, yb, wa, wb, tm):
    m, d = x.shape
    return pl.pallas_call(
        _out_proj_kernel,
        grid=(m // tm,),
        in_specs=[pl.BlockSpec((tm, d), lambda i: (i, 0)),
                  pl.BlockSpec((tm, ya.shape[1]), lambda i: (i, 0)),
                  pl.BlockSpec((tm, yb.shape[1]), lambda i: (i, 0)),
                  pl.BlockSpec(wa.shape, lambda i: (0, 0)),
                  pl.BlockSpec(wb.shape, lambda i: (0, 0))],
        out_specs=pl.BlockSpec((tm, d), lambda i: (i, 0)),
        out_shape=jax.ShapeDtypeStruct((m, d), F32),
        compiler_params=_cparams(("parallel",)), name="out_proj",
    )(x, ya, yb, wa, wb)


def _ffn_kernel(x_ref, g_ref, w1_ref, w3_ref, w2_ref, o_ref, xn_scr):
    j = pl.program_id(1)

    @pl.when(j == 0)
    def _():
        x = x_ref[...]
        xn_scr[...] = _rms(x, g_ref[...]).astype(BF16)
        o_ref[...] = x

    xn = xn_scr[...]
    h = _silu(_dg(xn, w1_ref[...], _NN)) * _dg(xn, w3_ref[...], _NN)
    o_ref[...] += _dg(h.astype(BF16), w2_ref[...], _NN)


def ffn(x, g, w1, w3, w2, tm, fc):
    m, d = x.shape
    dff = w1.shape[1]
    return pl.pallas_call(
        _ffn_kernel,
        grid=(m // tm, dff // fc),
        in_specs=[pl.BlockSpec((tm, d), lambda i, j: (i, 0)),
                  pl.BlockSpec((1, d), lambda i, j: (0, 0)),
                  pl.BlockSpec((d, fc), lambda i, j: (0, j)),
                  pl.BlockSpec((d, fc), lambda i, j: (0, j)),
                  pl.BlockSpec((fc, d), lambda i, j: (j, 0))],
        out_specs=pl.BlockSpec((tm, d), lambda i, j: (i, 0)),
        out_shape=jax.ShapeDtypeStruct((m, d), F32),
        scratch_shapes=[pltpu.VMEM((tm, d), BF16)],
        compiler_params=_cparams(("parallel", "arbitrary")), name="ffn",
    )(x, g.reshape(1, d), w1, w3, w2)


def _router_kernel(n_exp, x_ref, g_ref, rw_ref, rb_ref, hn_ref, gate_ref, pos_ref, cnt_ref):
    xn = _rms(x_ref[...], g_ref[...])
    hn_ref[...] = xn.astype(BF16)
    logits = _hdot(xn, rw_ref[...]) + rb_ref[...]
    lane = _iota(logits.shape, 1)
    logits = jnp.where(lane < n_exp, logits, -jnp.inf)
    m1 = jnp.max(logits, axis=-1, keepdims=True)
    i1 = jnp.min(jnp.where(logits == m1, lane, LANES), axis=-1, keepdims=True)
    rest = jnp.where(lane == i1, -jnp.inf, logits)
    m2 = jnp.max(rest, axis=-1, keepdims=True)
    i2 = jnp.min(jnp.where(rest == m2, lane, LANES), axis=-1, keepdims=True)
    e2 = jnp.exp(m2 - m1)
    g1 = 1.0 / (1.0 + e2)
    g2 = e2 / (1.0 + e2)
    pick1 = lane == i1
    pick2 = lane == i2
    gate_ref[...] = jnp.where(pick1, g1, 0.0) + jnp.where(pick2, g2, 0.0)
    tm = logits.shape[0]
    sel = jnp.where(jnp.logical_or(pick1, pick2), 1.0, 0.0).astype(BF16)
    tr = _iota((tm, tm), 0)
    tc = _iota((tm, tm), 1)
    upper = jnp.where(tr <= tc, 1.0, 0.0).astype(BF16)
    eye = jnp.where(tr == tc, 1.0, 0.0).astype(BF16)
    rank_t = _dg(sel, upper, _TN)
    sel_t = _dg(sel, eye, _TN)
    pos_t = jnp.where(sel_t > 0.5, rank_t - 1.0, -1.0)
    pos_ref[...] = pos_t[0:SUBLANES, :]
    cnt_ref[0] = jnp.sum(sel.astype(F32), axis=0, keepdims=True)


def router(x, g, rw, rb, tm):
    m, d = x.shape
    n_exp = rw.shape[1]
    assert n_exp <= SUBLANES
    rw_pad = jnp.pad(rw, ((0, 0), (0, LANES - n_exp)))
    rb_pad = jnp.pad(rb, (0, LANES - n_exp)).reshape(1, LANES)
    return pl.pallas_call(
        functools.partial(_router_kernel, n_exp),
        grid=(m // tm,),
        in_specs=[pl.BlockSpec((tm, d), lambda i: (i, 0)),
                  pl.BlockSpec((1, d), lambda i: (0, 0)),
                  pl.BlockSpec((d, LANES), lambda i: (0, 0)),
                  pl.BlockSpec((1, LANES), lambda i: (0, 0))],
        out_specs=[pl.BlockSpec((tm, d), lambda i: (i, 0)),
                   pl.BlockSpec((tm, LANES), lambda i: (i, 0)),
                   pl.BlockSpec((SUBLANES, tm), lambda i: (0, i)),
                   pl.BlockSpec((1, 1, LANES), lambda i: (i, 0, 0))],
        out_shape=[jax.ShapeDtypeStruct((m, d), BF16), jax.ShapeDtypeStruct((m, LANES), F32),
                   jax.ShapeDtypeStruct((SUBLANES, m), F32),
                   jax.ShapeDtypeStruct((m // tm, 1, LANES), F32)],
        compiler_params=_cparams(("parallel",)), name="router",
    )(x, g.reshape(1, d), rw_pad, rb_pad)


MOE_ROWS = 128


def _moe_kernel(n_exp, cnt_ref, x_ref, hn_ref, gate_ref, pos_ref, w1_ref, w3_ref, w2_ref, o_ref, xs_scr, y_scr):
    i = pl.program_id(0)
    e = pl.program_id(1)
    j = pl.program_id(2)
    nj = pl.num_programs(2)
    nblk = (cnt_ref[i * n_exp + e] + (MOE_ROWS - 1)) // MOE_ROWS
    tm = hn_ref.shape[0]

    @pl.when(jnp.logical_and(e == 0, j == 0))
    def _():
        o_ref[...] = x_ref[...]

    def select(blk):
        pos = pos_ref[pl.ds(e, 1), :]
        want = (blk * MOE_ROWS + _iota((MOE_ROWS, tm), 0)).astype(F32)
        return jnp.where(pos == want, 1.0, 0.0).astype(BF16)

    @pl.when(j == 0)
    def _():
        def gather(blk, carry):
            rows = pl.ds(pl.multiple_of(blk * MOE_ROWS, MOE_ROWS), MOE_ROWS)
            xs_scr[rows, :] = _dg(select(blk), hn_ref[...], _NN).astype(BF16)
            y_scr[rows, :] = jnp.zeros((MOE_ROWS, y_scr.shape[1]), F32)
            return carry
        lax.fori_loop(0, nblk, gather, 0)

    def expert(blk, carry):
        rows = pl.ds(pl.multiple_of(blk * MOE_ROWS, MOE_ROWS), MOE_ROWS)
        xs = xs_scr[rows, :]
        h = _silu(_dg(xs, w1_ref[0], _NN)) * _dg(xs, w3_ref[0], _NN)
        y_scr[rows, :] += _dg(h.astype(BF16), w2_ref[0], _NN)
        return carry
    lax.fori_loop(0, nblk, expert, 0)

    @pl.when(j == nj - 1)
    def _():
        gt = gate_ref[...]
        gcol = jnp.sum(jnp.where(_iota(gt.shape, 1) == e, gt, 0.0), axis=-1, keepdims=True)

        def scatter(blk, carry):
            rows = pl.ds(pl.multiple_of(blk * MOE_ROWS, MOE_ROWS), MOE_ROWS)
            o_ref[...] += gcol * _dg(select(blk), y_scr[rows, :].astype(BF16), _TN)
            return carry
        lax.fori_loop(0, nblk, scatter, 0)


def moe(x, hn, gates, pos, counts, w1, w3, w2, tm, fc):
    m, d = x.shape
    n_exp, _, dff = w1.shape
    cnt = counts[:, 0, :n_exp].astype(jnp.int32).reshape(-1)
    grid_spec = pltpu.PrefetchScalarGridSpec(
        num_scalar_prefetch=1,
        grid=(m // tm, n_exp, dff // fc),
        in_specs=[pl.BlockSpec((tm, d), lambda i, e, j, c: (i, 0)),
                  pl.BlockSpec((tm, d), lambda i, e, j, c: (i, 0)),
                  pl.BlockSpec((tm, LANES), lambda i, e, j, c: (i, 0)),
                  pl.BlockSpec((SUBLANES, tm), lambda i, e, j, c: (0, i)),
                  pl.BlockSpec((1, d, fc), lambda i, e, j, c: (e, 0, j)),
                  pl.BlockSpec((1, d, fc), lambda i, e, j, c: (e, 0, j)),
                  pl.BlockSpec((1, fc, d), lambda i, e, j, c: (e, j, 0))],
        out_specs=pl.BlockSpec((tm, d), lambda i, e, j, c: (i, 0)),
        scratch_shapes=[pltpu.VMEM((tm, d), BF16), pltpu.VMEM((tm, d), F32)])
    return pl.pallas_call(
        functools.partial(_moe_kernel, n_exp),
        grid_spec=grid_spec,
        out_shape=jax.ShapeDtypeStruct((m, d), F32),
        compiler_params=_cparams(("parallel", "arbitrary", "arbitrary")), name="moe",
    )(cnt, x, hn, gates, pos, w1, w3, w2)


RW = 512
RSHIFT = 1792


def _rwkv_pre_kernel(t_per_batch, tm, p_ref, pb_ref, prev_ref, mu_ref, w0_ref, wup_ref, a0_ref, aup_ref,
                     gup_ref, kk_ref, ka_ref, rk_ref,
                     r_o, lw_o, k_o, v_o, al_o, be_o, g_o, bo_o):
    i = pl.program_id(0)
    p = p_ref[...]
    rolled = pltpu.roll(p, 1, 0)
    row = _iota((tm, 1), 0)
    rolled = jnp.where(row == 0, pb_ref[SUBLANES - 1:SUBLANES, :], rolled)
    if t_per_batch >= tm:
        first = (i * tm) % t_per_batch == 0
        is_start = jnp.logical_and(row == 0, first)
        prev = jnp.where(is_start, prev_ref[0], rolled)
    else:
        is_start = (row % t_per_batch) == 0
        prev = jnp.where(is_start, prev_ref[...], rolled)
    xs = p + (prev - p) * mu_ref[...]
    r = xs[:, 0:RW]
    k = xs[:, RW:2 * RW]
    v = xs[:, 2 * RW:3 * RW]
    lr = xs[:, 3 * RW:3 * RW + LANES]
    gd = xs[:, 3 * RW + LANES:3 * RW + 2 * LANES]
    w_pre = w0_ref[...] + _bdot(jnp.tanh(lr), wup_ref[...])
    logw = -jnp.exp(-_softplus(-w_pre) - 0.5)
    a = _sigmoid(a0_ref[...] + _bdot(lr, aup_ref[...]))
    g = _bdot(_sigmoid(gd), gup_ref[...])
    kkr = k * kk_ref[...]
    kk = kkr / jnp.maximum(jnp.sqrt(_seg_sum(kkr * kkr, RWKV_HD)), 1e-12)
    k2 = k * (1.0 + (a - 1.0) * ka_ref[...])
    bonus = _seg_sum(r * k2 * rk_ref[...], RWKV_HD) * v
    r_o[...] = r
    lw_o[...] = logw
    k_o[...] = k2
    v_o[...] = v
    al_o[...] = -kk
    be_o[...] = kk * a
    g_o[...] = g
    bo_o[...] = bonus


def rwkv_pre(p_full, prev_rows, t_per_batch, tm, prm):
    m = p_full.shape[0]
    nb8 = tm // SUBLANES
    row_spec = pl.BlockSpec((1, RW), lambda i: (0, 0))
    if t_per_batch >= tm:
        prev_spec = pl.BlockSpec((1, 1, RSHIFT), lambda i: ((i * tm) // t_per_batch, 0, 0))
    else:
        prev_spec = pl.BlockSpec((tm, RSHIFT), lambda i: (i, 0))
    out_sds = jax.ShapeDtypeStruct((m, RW), F32)
    out_spec = pl.BlockSpec((tm, RW), lambda i: (i, 0))
    return pl.pallas_call(
        functools.partial(_rwkv_pre_kernel, t_per_batch, tm),
        grid=(m // tm,),
        in_specs=[pl.BlockSpec((tm, RSHIFT), lambda i: (i, 0)),
                  pl.BlockSpec((SUBLANES, RSHIFT), lambda i: (jnp.maximum(i * nb8 - 1, 0), 0)),
                  prev_spec,
                  pl.BlockSpec((1, RSHIFT), lambda i: (0, 0)),
                  row_spec,
                  pl.BlockSpec((LANES, RW), lambda i: (0, 0)),
                  row_spec,
                  pl.BlockSpec((LANES, RW), lambda i: (0, 0)),
                  pl.BlockSpec((LANES, RW), lambda i: (0, 0)),
                  row_spec, row_spec, row_spec],
        out_specs=[out_spec] * 8,
        out_shape=[out_sds] * 8,
        compiler_params=_cparams(("parallel",)), name="rwkv_pre",
    )(p_full, p_full, prev_rows, prm["mu"], prm["w0"], prm["wup"], prm["a0"], prm["aup"], prm["gup"],
      prm["k_k"], prm["k_a"], prm["r_k"])


_RW_DOT_A = _bdot
_RW_DOT_T = _bdot
_RW_DOT_W = _bdot
_RW_DOT_S = _hdot


def _rwkv_scan_kernel(L, n_pairs, r_ref, lw_ref, k_ref, v_ref, al_ref, be_ref, s0_ref,
                      y_ref, sf_ref, s_scr):
    c = pl.program_id(1)
    nc = pl.num_programs(1)
    L2 = 2 * L
    lane = _iota((1, LANES), 1)
    m0 = jnp.where(lane < RWKV_HD, 1.0, 0.0)
    m1 = 1.0 - m0
    rr = _iota((L2, L2), 0)
    cc = _iota((L2, L2), 1)
    same = (rr // L) == (cc // L)
    strict = jnp.logical_and(same, (cc % L) < (rr % L))
    incl = jnp.logical_and(same, (cc % L) <= (rr % L))
    eye2 = jnp.where(rr == cc, 1.0, 0.0)
    r128 = _iota((LANES, LANES), 0)
    c128 = _iota((LANES, LANES), 1)
    blk128 = (r128 // RWKV_HD) == (c128 // RWKV_HD)
    diag128 = r128 == c128
    fmat = jnp.where(_iota((LANES, RWKV_HD), 0) % RWKV_HD == _iota((LANES, RWKV_HD), 1), 1.0, 0.0)
    tri = _tri_incl(L)

    def bd(x):
        return jnp.concatenate([x * m0, x * m1], axis=0)

    @pl.when(c == 0)
    def _():
        for j in range(n_pairs):
            s0 = s0_ref[0, j]
            st = _xdot_r(fmat, s0, _NT)
            s_scr[j] = jnp.where(blk128, st, 0.0)

    pairs = range(n_pairs)
    sls = [slice(j * LANES, (j + 1) * LANES) for j in pairs]
    lw = [lw_ref[0, :, sl] for sl in sls]
    b = [_xdot_r(tri, x) for x in lw]
    bl = [x[L - 1:L, :] for x in b]
    e_b = [jnp.exp(x) for x in b]
    e_nb = [jnp.exp(-x) for x in b]
    e_lb = [jnp.exp(bl[j] - b[j]) for j in pairs]
    at = [bd(al_ref[0, :, sls[j]] * jnp.exp(b[j] - lw[j])) for j in pairs]
    rt = [bd(r_ref[0, :, sls[j]] * e_b[j]) for j in pairs]
    bh = [bd(be_ref[0, :, sls[j]] * e_nb[j]) for j in pairs]
    kh = [bd(k_ref[0, :, sls[j]] * e_nb[j]) for j in pairs]
    bt = [bd(be_ref[0, :, sls[j]] * e_lb[j]) for j in pairs]
    kt = [bd(k_ref[0, :, sls[j]] * e_lb[j]) for j in pairs]
    vb = [bd(v_ref[0, :, sl]) for sl in sls]
    a_ab = [jnp.where(strict, _RW_DOT_A(at[j], bh[j], _NT), 0.0) for j in pairs]
    a_ak = [jnp.where(strict, _RW_DOT_A(at[j], kh[j], _NT), 0.0) for j in pairs]
    a_rb = [jnp.where(incl, _RW_DOT_A(rt[j], bh[j], _NT), 0.0) for j in pairs]
    a_rk = [jnp.where(incl, _RW_DOT_A(rt[j], kh[j], _NT), 0.0) for j in pairs]
    x = a_ab
    tinv = [eye2 + a for a in a_ab]
    span = 2
    while span < L:
        x = [_RW_DOT_T(xx, xx) for xx in x]
        tinv = [tinv[j] + _RW_DOT_T(tinv[j], x[j]) for j in pairs]
        span *= 2
    akv = [_RW_DOT_W(a_ak[j], vb[j]) for j in pairs]
    w = [_RW_DOT_W(tinv[j], at[j]) for j in pairs]
    uv = [_RW_DOT_W(tinv[j], akv[j]) for j in pairs]
    mm = [jnp.where(diag128, jnp.exp(bl[j]), 0.0) + _RW_DOT_W(bt[j], w[j], _TN) for j in pairs]
    nn = [_RW_DOT_W(bt[j], uv[j], _TN) + _RW_DOT_W(kt[j], vb[j], _TN) for j in pairs]
    q = [rt[j] + _RW_DOT_W(a_rb[j], w[j]) for j in pairs]
    yv = [_RW_DOT_W(a_rb[j], uv[j]) + _RW_DOT_W(a_rk[j], vb[j]) for j in pairs]
    s = [s_scr[j] for j in pairs]
    for j in pairs:
        ybd = _RW_DOT_S(q[j], s[j]) + yv[j]
        y_ref[0, :, sls[j]] = ybd[0:L] + ybd[L:L2]
    for j in pairs:
        s_scr[j] = _RW_DOT_S(mm[j], s[j]) + nn[j]

    @pl.when(c == nc - 1)
    def _():
        for j in range(n_pairs):
            sf_ref[0, j] = _xdot_l(s_scr[j], fmat, _TN)


def rwkv_scan(ops, s0, L):
    bsz, t, _ = ops[0].shape
    n_pairs = RW // LANES
    blk = pl.BlockSpec((1, L, RW), lambda b, c: (b, c, 0))
    st_spec = pl.BlockSpec((1, n_pairs, LANES, RWKV_HD), lambda b, c: (b, 0, 0, 0))
    return pl.pallas_call(
        functools.partial(_rwkv_scan_kernel, L, n_pairs),
        grid=(bsz, t // L),
        in_specs=[blk] * 6 + [st_spec],
        out_specs=[blk, st_spec],
        out_shape=[jax.ShapeDtypeStruct((bsz, t, RW), F32),
                   jax.ShapeDtypeStruct((bsz, n_pairs, LANES, RWKV_HD), F32)],
        scratch_shapes=[pltpu.VMEM((n_pairs, LANES, LANES), F32)],
        compiler_params=_cparams(("parallel", "arbitrary")), name="rwkv_scan",
    )(*ops, s0)


def _rwkv_post_kernel(y_ref, bo_ref, g_ref, lw_ref, lb_ref, o_ref):
    y = y_ref[...]
    mean = _seg_sum(y, RWKV_HD) * (1.0 / RWKV_HD)
    d = y - mean
    var = _seg_sum(d * d, RWKV_HD) * (1.0 / RWKV_HD)
    yn = d * lax.rsqrt(var + RWKV_GN_EPS) * lw_ref[...] + lb_ref[...]
    o_ref[...] = (yn + bo_ref[...]) * g_ref[...]


def rwkv_post(y, bonus, g, lnx_w, lnx_b, tm):
    m = y.shape[0]
    spec = pl.BlockSpec((tm, RW), lambda i: (i, 0))
    row = pl.BlockSpec((1, RW), lambda i: (0, 0))
    return pl.pallas_call(
        _rwkv_post_kernel,
        grid=(m // tm,),
        in_specs=[spec, spec, spec, row, row],
        out_specs=spec,
        out_shape=jax.ShapeDtypeStruct((m, RW), F32),
        compiler_params=_cparams(("parallel",)), name="rwkv_post",
    )(y, bonus, g, lnx_w.reshape(1, RW), lnx_b.reshape(1, RW))


def rwkv_params(w, i):
    z64 = jnp.zeros((RWKV_HD, RW), F32)
    return {
        "mu": w["e_mu"][i].reshape(1, RSHIFT),
        "w0": w["e_w0"][i].reshape(1, RW),
        "wup": jnp.concatenate([w["e_w_up"][i], z64], 0).astype(BF16),
        "a0": w["e_a0"][i].reshape(1, RW),
        "aup": jnp.concatenate([z64, w["e_a_up"][i]], 0).astype(BF16),
        "gup": w["e_g_up"][i].astype(BF16),
        "k_k": w["e_k_k"][i].reshape(1, RW),
        "k_a": w["e_k_a"][i].reshape(1, RW),
        "r_k": w["e_r_k"][i].reshape(1, RW),
    }


def rwkv_mix(p_flat, prev, s0, bsz, t, L, tm, prm, lnx_w, lnx_b):
    m = bsz * t
    if t >= tm:
        prev_rows = prev.reshape(bsz, 1, RSHIFT)
    else:
        prev_rows = jnp.repeat(prev, t, axis=0)
    r, lw, k, v, al, be, g, bonus = rwkv_pre(p_flat, prev_rows, t, tm, prm)
    tp = -(-t // L) * L
    ops = [z.reshape(bsz, t, RW) for z in (r, lw, k, v, al, be)]
    if tp != t:
        ops = [jnp.pad(z, ((0, 0), (0, tp - t), (0, 0))) for z in ops]
    y, s_fin = rwkv_scan(ops, s0.reshape(bsz, RW // LANES, LANES, RWKV_HD), L)
    y = y[:, :t].reshape(m, RW)
    ya = rwkv_post(y, bonus, g, lnx_w, lnx_b, tm)
    return ya, s_fin.reshape(bsz, RW // RWKV_HD, RWKV_HD, RWKV_HD)


N_Q_HEADS = 8
N_KV_HEADS = 2
Q_PER_KV = N_Q_HEADS // N_KV_HEADS
QW = N_Q_HEADS * ATT_HD
KW = N_KV_HEADS * ATT_HD
NEG = -1e30


def _t5_bucket_np(dist):
    n = np.maximum(dist, 0)
    max_exact = N_BUCKETS // 2
    nf = np.maximum(n, 1).astype(np.float32)
    large = max_exact + (np.log(nf / np.float32(max_exact)) / np.float32(math.log(BUCKET_MAX_DIST / max_exact))
                         * np.float32(N_BUCKETS - max_exact)).astype(np.int32)
    large = np.minimum(large, N_BUCKETS - 1)
    return np.where(n < max_exact, n, large)


def _bias_kernel(rt_ref, oh_ref, o_ref):
    o_ref[...] = _xdot_l(rt_ref[...], oh_ref[...])


def rel_bias(rel_table, dist):
    bucket = _t5_bucket_np(dist).reshape(-1)
    n = bucket.shape[0]
    onehot = jnp.asarray((np.arange(N_BUCKETS)[:, None] == bucket[None, :]).astype(np.float32), BF16)
    out = pl.pallas_call(
        _bias_kernel,
        out_shape=jax.ShapeDtypeStruct((N_Q_HEADS, n), F32),
    )(rel_table.T, onehot)
    return out.reshape((N_Q_HEADS,) + dist.shape)


def _head_norm(x, w_row):
    return x * lax.rsqrt(_seg_sum(x * x, ATT_HD) * (1.0 / ATT_HD) + EPS) * w_row


def _swa_prompt_kernel(q_ref, kvc_ref, kvp_ref, qw_ref, kw_ref, bias_ref, sink_ref, o_ref, ko_ref, vo_ref):
    i = pl.program_id(1)
    qn = _head_norm(q_ref[0], qw_ref[...])
    kvc = kvc_ref[0]
    kvp = kvp_ref[0]
    kcn = _head_norm(kvc[:, 0:KW], kw_ref[...])
    kpn = _head_norm(kvp[:, 0:KW], kw_ref[...])
    vc = kvc[:, KW:2 * KW]
    kcat = jnp.concatenate([kpn, kcn], axis=0).astype(BF16)
    vcat = jnp.concatenate([kvp[:, KW:2 * KW], vc], axis=0).astype(BF16)
    qi = _iota((WINDOW, 2 * WINDOW), 0)
    kj = _iota((WINDOW, 2 * WINDOW), 1)
    dist = qi + WINDOW - kj
    valid = jnp.logical_and(jnp.logical_and(dist >= 0, dist < WINDOW), jnp.logical_or(kj >= WINDOW, i > 0))
    lane = _iota((1, LANES), 1)
    masks = (jnp.where(lane < ATT_HD, 1.0, 0.0), jnp.where(lane < ATT_HD, 0.0, 1.0))
    tiles = [None] * (QW // LANES)
    for h in range(N_Q_HEADS):
        kv, hh, jq = h // Q_PER_KV, h % 2, h // 2
        qt = qn[:, jq * LANES:(jq + 1) * LANES]
        if hh != kv:
            qt = pltpu.roll(qt, ATT_HD, 1)
        qm = (qt * masks[kv]).astype(BF16)
        s = _dg(qm, kcat, _NT) * (ATT_HD ** -0.5) + bias_ref[h]
        logits = jnp.where(valid, s, NEG)
        sink = sink_ref[h:h + 1, 0:1]
        mx = jnp.maximum(jnp.max(logits, axis=-1, keepdims=True), sink)
        pr = jnp.exp(logits - mx)
        den = jnp.sum(pr, axis=-1, keepdims=True) + jnp.exp(sink - mx)
        o = _dg((pr / den).astype(BF16), vcat, _NN)
        if hh != kv:
            o = pltpu.roll(o, ATT_HD, 1)
        o = o * masks[hh]
        tiles[jq] = o if tiles[jq] is None else tiles[jq] + o
    for jq in range(QW // LANES):
        o_ref[0, :, jq * LANES:(jq + 1) * LANES] = tiles[jq]
    ko_ref[0] = kcn
    vo_ref[0] = vc


def swa_prompt(q, kv, bsz, t, q_norm, k_norm, bias, sinks):
    nb = t // WINDOW
    q3 = q.reshape(bsz, t, QW)
    kv3 = kv.reshape(bsz, t, 2 * KW)
    o, ko, vo = pl.pallas_call(
        _swa_prompt_kernel,
        grid=(bsz, nb),
        in_specs=[pl.BlockSpec((1, WINDOW, QW), lambda b, i: (b, i, 0)),
                  pl.BlockSpec((1, WINDOW, 2 * KW), lambda b, i: (b, i, 0)),
                  pl.BlockSpec((1, WINDOW, 2 * KW), lambda b, i: (b, jnp.maximum(i - 1, 0), 0)),
                  pl.BlockSpec((1, QW), lambda b, i: (0, 0)),
                  pl.BlockSpec((1, KW), lambda b, i: (0, 0)),
                  pl.BlockSpec((N_Q_HEADS, WINDOW, 2 * WINDOW), lambda b, i: (0, 0, 0)),
                  pl.BlockSpec((N_Q_HEADS, LANES), lambda b, i: (0, 0))],
        out_specs=[pl.BlockSpec((1, WINDOW, QW), lambda b, i: (b, i, 0)),
                   pl.BlockSpec((1, WINDOW, KW), lambda b, i: (b, 0, 0)),
                   pl.BlockSpec((1, WINDOW, KW), lambda b, i: (b, 0, 0))],
        out_shape=[jax.ShapeDtypeStruct((bsz, t, QW), F32),
                   jax.ShapeDtypeStruct((bsz, WINDOW, KW), F32),
                   jax.ShapeDtypeStruct((bsz, WINDOW, KW), F32)],
        compiler_params=_cparams(("parallel", "arbitrary")), name="swa_prompt",
    )(q3, kv3, kv3, jnp.tile(q_norm, N_Q_HEADS).reshape(1, QW), jnp.tile(k_norm, N_KV_HEADS).reshape(1, KW),
      bias, jnp.broadcast_to(sinks[:, None], (N_Q_HEADS, LANES)))
    return o.reshape(bsz * t, QW), ko, vo


DEC_TP = 8


def _swa_decode_kernel(nbt, t_real, q_ref, kv_ref, ck_ref, cv_ref, qw_ref, kw_ref, bc_ref, bn_ref, sink_ref,
                       o_ref, ko_ref, vo_ref):
    rows = Q_PER_KV * DEC_TP
    tq = _iota((rows, WINDOW), 0) % DEC_TP
    valid_c = _iota((rows, WINDOW), 1) > tq
    jn = _iota((rows, DEC_TP), 1)
    valid_n = jnp.logical_and(jn <= _iota((rows, DEC_TP), 0) % DEC_TP, jn < t_real)
    lane = _iota((1, LANES), 1)
    masks = (jnp.where(lane < ATT_HD, 1.0, 0.0), jnp.where(lane < ATT_HD, 0.0, 1.0))
    row8 = _iota((DEC_TP, 1), 0)
    for b in range(nbt):
        qn = _head_norm(q_ref[b], qw_ref[...])
        kvn = kv_ref[b]
        knew = _head_norm(kvn[:, 0:KW], kw_ref[...])
        vnew = kvn[:, KW:2 * KW]
        kc = ck_ref[b]
        vc = cv_ref[b]
        tiles = [None] * (QW // LANES)
        for kv in range(N_KV_HEADS):
            pieces = []
            for g in range(Q_PER_KV):
                h = kv * Q_PER_KV + g
                qt = qn[:, (h // 2) * LANES:(h // 2 + 1) * LANES]
                if h % 2 != kv:
                    qt = pltpu.roll(qt, ATT_HD, 1)
                pieces.append(qt * masks[kv])
            qg = jnp.concatenate(pieces, axis=0)
            l_c = jnp.where(valid_c, _bdot(qg, kc, _NT) * (ATT_HD ** -0.5) + bc_ref[kv], NEG)
            l_n = jnp.where(valid_n, _bdot(qg, knew, _NT) * (ATT_HD ** -0.5) + bn_ref[kv][:, 0:DEC_TP], NEG)
            sink = sink_ref[kv][:, 0:1]
            mx = jnp.maximum(jnp.maximum(jnp.max(l_c, axis=-1, keepdims=True),
                                         jnp.max(l_n, axis=-1, keepdims=True)), sink)
            p_c = jnp.exp(l_c - mx)
            p_n = jnp.exp(l_n - mx)
            den = (jnp.sum(p_c, axis=-1, keepdims=True) + jnp.sum(p_n, axis=-1, keepdims=True)
                   + jnp.exp(sink - mx))
            o = _bdot(p_c / den, vc) + _bdot(p_n / den, vnew)
            for g in range(Q_PER_KV):
                h = kv * Q_PER_KV + g
                piece = o[g * DEC_TP:(g + 1) * DEC_TP]
                if h % 2 != kv:
                    piece = pltpu.roll(piece, ATT_HD, 1)
                piece = piece * masks[h % 2]
                tiles[h // 2] = piece if tiles[h // 2] is None else tiles[h // 2] + piece
        for jq in range(QW // LANES):
            o_ref[b, :, jq * LANES:(jq + 1) * LANES] = tiles[jq]
        for cache, new, out in ((kc, knew, ko_ref), (vc, vnew, vo_ref)):
            shifted = pltpu.roll(cache, WINDOW - t_real, 0)
            new_r = pltpu.roll(new, DEC_TP - t_real, 0)
            out[b, 0:WINDOW - DEC_TP] = shifted[0:WINDOW - DEC_TP]
            out[b, WINDOW - DEC_TP:WINDOW] = jnp.where(row8 >= DEC_TP - t_real, new_r,
                                                       shifted[WINDOW - DEC_TP:WINDOW])


def swa_decode(q, kv, cache_k, cache_v, bsz, t, q_norm, k_norm, rel_table, sinks, nbt):
    pad = ((0, 0), (0, DEC_TP - t), (0, 0))
    q3 = jnp.pad(q.reshape(bsz, t, QW), pad)
    kv3 = jnp.pad(kv.reshape(bsz, t, 2 * KW), pad)
    kpos = np.concatenate([np.arange(WINDOW) - WINDOW, np.arange(DEC_TP)])
    dist = np.arange(DEC_TP)[:, None] - kpos[None, :]
    bias = rel_bias(rel_table, dist)
    rows = Q_PER_KV * DEC_TP
    bias = bias.reshape(N_KV_HEADS, rows, WINDOW + DEC_TP)
    bias_c = bias[:, :, :WINDOW]
    bias_n = jnp.pad(bias[:, :, WINDOW:], ((0, 0), (0, 0), (0, LANES - DEC_TP)))
    sink_rows = jnp.broadcast_to(sinks.reshape(N_KV_HEADS, Q_PER_KV, 1, 1),
                                 (N_KV_HEADS, Q_PER_KV, DEC_TP, LANES)).reshape(N_KV_HEADS, rows, LANES)
    full3 = lambda shape: pl.BlockSpec(shape, lambda i: (0, 0, 0))
    o, ko, vo = pl.pallas_call(
        functools.partial(_swa_decode_kernel, nbt, t),
        grid=(bsz // nbt,),
        in_specs=[pl.BlockSpec((nbt, DEC_TP, QW), lambda i: (i, 0, 0)),
                  pl.BlockSpec((nbt, DEC_TP, 2 * KW), lambda i: (i, 0, 0)),
                  pl.BlockSpec((nbt, WINDOW, KW), lambda i: (i, 0, 0)),
                  pl.BlockSpec((nbt, WINDOW, KW), lambda i: (i, 0, 0)),
                  pl.BlockSpec((1, QW), lambda i: (0, 0)),
                  pl.BlockSpec((1, KW), lambda i: (0, 0)),
                  full3((N_KV_HEADS, rows, WINDOW)),
                  full3((N_KV_HEADS, rows, LANES)),
                  full3((N_KV_HEADS, rows, LANES))],
        out_specs=[pl.BlockSpec((nbt, DEC_TP, QW), lambda i: (i, 0, 0)),
                   pl.BlockSpec((nbt, WINDOW, KW), lambda i: (i, 0, 0)),
                   pl.BlockSpec((nbt, WINDOW, KW), lambda i: (i, 0, 0))],
        out_shape=[jax.ShapeDtypeStruct((bsz, DEC_TP, QW), F32),
                   jax.ShapeDtypeStruct((bsz, WINDOW, KW), F32),
                   jax.ShapeDtypeStruct((bsz, WINDOW, KW), F32)],
        compiler_params=_cparams(("parallel",)), name="swa_decode",
    )(q3, kv3, cache_k, cache_v, jnp.tile(q_norm, N_Q_HEADS).reshape(1, QW),
      jnp.tile(k_norm, N_KV_HEADS).reshape(1, KW), bias_c, bias_n, sink_rows)
    return o[:, :t].reshape(bsz * t, QW), ko, vo


S5_G = 32
S5_W = S5_G * S5_P
S5_CP = S5_CHUNK * S5_P
S5_PK = 2 * S5_N
S5_HW = S5_G * S5_PK


def _s5_prep_kernel(t_eff, a_ref, ldt_ref, b1_ref, b2_ref, c1_ref, c2_ref, kmat_ref, kb_ref, kct_ref, al_ref):
    L = S5_CHUNK
    ar2 = a_ref[0, 0:1, :]
    ai2 = a_ref[0, 1:2, :]
    step = jnp.exp(ldt_ref[0])
    mi = _iota((3 * SUBLANES, S5_PK), 0).astype(F32)
    mag = jnp.exp(mi * (step * ar2))
    ang = mi * (step * ai2)
    pwa = mag * jnp.cos(ang)
    pwb = mag * jnp.sin(ang)
    abr = pwa[1:2]
    abi = pwb[1:2]
    den = ar2 * ar2 + ai2 * ai2
    fa = ((abr - 1.0) * ar2 + abi * ai2) / den
    fb = (abi * ar2 - (abr - 1.0) * ai2) / den
    b1 = b1_ref[0]
    b2 = b2_ref[0]
    bp1 = b1 * fa + b2 * fb
    bp2 = b2 * fa - b1 * fb
    c1 = c1_ref[0]
    c2 = c2_ref[0]
    cpow = [c1 * pwa[m:m + 1] + c2 * pwb[m:m + 1] for m in range(L + 1)]
    kern_t = _hdot(bp1, jnp.concatenate(cpow[0:L], axis=0), _NT)
    lane = _iota((S5_P, S5_CP), 1)
    blocks = [kern_t]
    for i in range(1, L):
        blocks.append(jnp.where(lane >= S5_P * i, pltpu.roll(kern_t, S5_P * i, 1), 0.0))
    kmat_ref[0] = jnp.concatenate(blocks, axis=0).astype(BF16)
    kb = []
    for i in range(L):
        e = max(t_eff - 1 - i, 0)
        kb.append(bp1 * pwa[e:e + 1] + bp2 * pwb[e:e + 1])
    kb_ref[0] = jnp.concatenate(kb, axis=0).astype(BF16)
    kct_ref[0] = jnp.concatenate(cpow[1:L + 1], axis=0).astype(BF16)
    sgn = jnp.where(_iota((1, S5_PK), 1) < S5_N, -1.0, 1.0)
    al_ref[0] = jnp.concatenate([pwa[t_eff:t_eff + 1], sgn * pwb[t_eff:t_eff + 1]], axis=0)


def s5_prep(w, i, t_eff):
    dup = lambda z: jnp.concatenate([z, z], axis=-1)
    a = jnp.stack([dup(w["o_a_re"][i]), dup(w["o_a_im"][i])], axis=1)
    ldt = jnp.broadcast_to(w["o_log_dt"][i][:, None, None], (S5_G, 1, S5_PK))
    bt_re = jnp.swapaxes(w["o_b_re"][i], 1, 2)
    bt_im = jnp.swapaxes(w["o_b_im"][i], 1, 2)
    b1 = jnp.concatenate([bt_re, bt_im], -1)
    b2 = jnp.concatenate([-bt_im, bt_re], -1)
    c_re, c_im = w["o_c_re"][i], w["o_c_im"][i]
    c1 = jnp.concatenate([c_re, -c_im], -1)
    c2 = jnp.concatenate([-c_im, -c_re], -1)
    g3 = lambda r, c: pl.BlockSpec((1, r, c), lambda g: (g, 0, 0))
    return pl.pallas_call(
        functools.partial(_s5_prep_kernel, t_eff),
        grid=(S5_G,),
        in_specs=[g3(2, S5_PK), g3(1, S5_PK), g3(S5_P, S5_PK), g3(S5_P, S5_PK), g3(S5_P, S5_PK), g3(S5_P, S5_PK)],
        out_specs=[g3(S5_CP, S5_CP), g3(S5_CP, S5_PK), g3(S5_CP, S5_PK), g3(2, S5_PK)],
        out_shape=[jax.ShapeDtypeStruct((S5_G, S5_CP, S5_CP), BF16),
                   jax.ShapeDtypeStruct((S5_G, S5_CP, S5_PK), BF16),
                   jax.ShapeDtypeStruct((S5_G, S5_CP, S5_PK), BF16),
                   jax.ShapeDtypeStruct((S5_G, 2, S5_PK), F32)],
        compiler_params=_cparams(("parallel",)), name="s5_prep",
    )(a, ldt, b1, b2, c1, c2)


def _s5_e_kernel(u_ref, kb_ref, e_ref):
    e_ref[...] = _bdot(u_ref[0], kb_ref[0])


def _s5_swap(h):
    n = h.shape[-1]
    lane = _iota(h.shape, 1)
    return jnp.where(lane % S5_PK < S5_N, pltpu.roll(h, n - S5_N, 1), pltpu.roll(h, S5_N, 1))


def _s5_scan_kernel(cg, e_ref, h0_ref, ala_ref, alb_ref, hp_ref, hf_ref, h_scr):
    @pl.when(pl.program_id(0) == 0)
    def _():
        h_scr[...] = h0_ref[...]

    ala = ala_ref[...]
    alb = alb_ref[...]

    def body(c, h):
        hp_ref[c] = h
        return ala * h + alb * _s5_swap(h) + e_ref[c]

    h = lax.fori_loop(0, cg, body, h_scr[...])
    h_scr[...] = h
    hf_ref[...] = h


def _s5_y_kernel(u_ref, hp_ref, kmat_ref, kct_ref, y_ref):
    y_ref[0] = _bdot(u_ref[0], kmat_ref[0]) + _bdot(hp_ref[...], kct_ref[0], _NT)


def s5_core(u, h0, bsz, t, prep):
    kmat, kb, kct, al = prep
    L = S5_CHUNK
    tp = -(-t // L) * L
    u3 = u.reshape(bsz, t, S5_W)
    if tp != t:
        u3 = jnp.pad(u3, ((0, 0), (0, tp - t), (0, 0)))
    nc = tp // L
    rows = nc * bsz
    ug = u3.reshape(bsz, nc, L, S5_G, S5_P).transpose(3, 1, 0, 2, 4).reshape(S5_G, rows, S5_CP)
    e = pl.pallas_call(
        _s5_e_kernel,
        grid=(S5_G,),
        in_specs=[pl.BlockSpec((1, rows, S5_CP), lambda g: (g, 0, 0)),
                  pl.BlockSpec((1, S5_CP, S5_PK), lambda g: (g, 0, 0))],
        out_specs=pl.BlockSpec((rows, S5_PK), lambda g: (0, g)),
        out_shape=jax.ShapeDtypeStruct((rows, S5_HW), F32),
        compiler_params=_cparams(("parallel",)), name="s5_e",
    )(ug, kb)
    cg = math.gcd(nc, 64)
    ala = al[:, 0, :].reshape(1, S5_HW)
    alb = al[:, 1, :].reshape(1, S5_HW)
    hp, hf = pl.pallas_call(
        functools.partial(_s5_scan_kernel, cg),
        grid=(nc // cg,),
        in_specs=[pl.BlockSpec((cg, bsz, S5_HW), lambda i: (i, 0, 0)),
                  pl.BlockSpec((bsz, S5_HW), lambda i: (0, 0)),
                  pl.BlockSpec((1, S5_HW), lambda i: (0, 0)),
                  pl.BlockSpec((1, S5_HW), lambda i: (0, 0))],
        out_specs=[pl.BlockSpec((cg, bsz, S5_HW), lambda i: (i, 0, 0)),
                   pl.BlockSpec((bsz, S5_HW), lambda i: (0, 0))],
        out_shape=[jax.ShapeDtypeStruct((nc, bsz, S5_HW), F32),
                   jax.ShapeDtypeStruct((bsz, S5_HW), F32)],
        scratch_shapes=[pltpu.VMEM((bsz, S5_HW), F32)],
        compiler_params=_cparams(("arbitrary",)), name="s5_scan",
    )(e.reshape(nc, bsz, S5_HW), h0, ala, alb)
    yg = pl.pallas_call(
        _s5_y_kernel,
        grid=(S5_G,),
        in_specs=[pl.BlockSpec((1, rows, S5_CP), lambda g: (g, 0, 0)),
                  pl.BlockSpec((rows, S5_PK), lambda g: (0, g)),
                  pl.BlockSpec((1, S5_CP, S5_CP), lambda g: (g, 0, 0)),
                  pl.BlockSpec((1, S5_CP, S5_PK), lambda g: (g, 0, 0))],
        out_specs=pl.BlockSpec((1, rows, S5_CP), lambda g: (g, 0, 0)),
        out_shape=jax.ShapeDtypeStruct((S5_G, rows, S5_CP), F32),
        compiler_params=_cparams(("parallel",)), name="s5_y",
    )(ug, hp.reshape(rows, S5_HW), kmat, kct)
    y = yg.reshape(S5_G, nc, bsz, L, S5_P).transpose(2, 1, 3, 0, 4).reshape(bsz, tp, S5_W)
    return y[:, :t].reshape(bsz * t, S5_W), hf


def _s5_post_kernel(y_ref, u_ref, d_ref, gw_ref, gb_ref, o_ref):
    x = y_ref[...] + d_ref[...] * u_ref[...]
    z = 0.5 * x * (1.0 + jnp.tanh(math.sqrt(2.0 / math.pi) * (x + 0.044715 * (x * x * x))))
    o_ref[...] = z * _sigmoid(_dg(z.astype(BF16), gw_ref[...], _NN) + gb_ref[...])


def s5_post(y, u, d, glu_w_bf16, glu_b, tm):
    m = y.shape[0]
    spec = pl.BlockSpec((tm, S5_W), lambda i: (i, 0))
    row = pl.BlockSpec((1, S5_W), lambda i: (0, 0))
    return pl.pallas_call(
        _s5_post_kernel,
        grid=(m // tm,),
        in_specs=[spec, spec, row, pl.BlockSpec((S5_W, S5_W), lambda i: (0, 0)), row],
        out_specs=spec,
        out_shape=jax.ShapeDtypeStruct((m, S5_W), F32),
        compiler_params=_cparams(("parallel",)), name="s5_post",
    )(y, u, d.reshape(1, S5_W), glu_w_bf16, glu_b.reshape(1, S5_W))


def s5_mix(u, h_re, h_im, bsz, t, prep, d, glu_w_bf16, glu_b, tm):
    h0 = jnp.concatenate([h_re, h_im], axis=-1).reshape(bsz, S5_HW)
    y, hf = s5_core(u, h0, bsz, t, prep)
    out = s5_post(y, u, d, glu_w_bf16, glu_b, tm)
    hf = hf.reshape(bsz, S5_G, 2, S5_N)
    return out, hf[:, :, 0], hf[:, :, 1]


GLA_DK = 64
GLA_DV = 128
GLA_HEADS = 4
GLA_KW = GLA_HEADS * GLA_DK
GLA_VW = GLA_HEADS * GLA_DV
GLA_PW = 2 * GLA_KW + 2 * GLA_VW + LANES


def _gla_kernel(L, t_real, p_ref, aup_ref, ab_ref, nw_ref, s0_ref, y_ref, sf_ref, s_scr):
    c = pl.program_id(1)
    nc = pl.num_programs(1)
    n_pairs = GLA_KW // LANES

    @pl.when(c == 0)
    def _():
        s_scr[...] = s0_ref[0]

    p = p_ref[0]
    q = p[:, 0:GLA_KW]
    k = p[:, GLA_KW:2 * GLA_KW]
    v = p[:, 2 * GLA_KW:2 * GLA_KW + GLA_VW]
    g = p[:, 2 * GLA_KW + GLA_VW:2 * GLA_KW + 2 * GLA_VW]
    ad = p[:, 2 * GLA_KW + 2 * GLA_VW:]
    z = _bdot(ad, aup_ref[...]) + ab_ref[...]
    gk = -_softplus(-z) * (1.0 / GLA_GATE_NORM)
    if t_real % L != 0:
        tok = c * L + _iota((L, 1), 0)
        gk = jnp.where(tok < t_real, gk, 0.0)
    b = _xdot_r(_tri_incl(L), gk)
    bl = b[L - 1:L, :]
    qd = q * (GLA_DK ** -0.5) * jnp.exp(b)
    kh = k * jnp.exp(-b)
    kt = k * jnp.exp(bl - b)
    pl_row = jnp.exp(bl)
    lane = _iota((1, LANES), 1)
    masks = (jnp.where(lane < GLA_DK, 1.0, 0.0), jnp.where(lane < GLA_DK, 0.0, 1.0))
    incl = _iota((L, L), 1) <= _iota((L, L), 0)
    diag128 = _iota((LANES, LANES), 0) == _iota((LANES, LANES), 1)
    ones128 = jnp.ones((LANES, LANES), BF16)
    for j in range(n_pairs):
        sl = slice(j * LANES, (j + 1) * LANES)
        s = s_scr[j]
        kv_rows = []
        for hh in range(2):
            h = 2 * j + hh
            hs = slice(h * GLA_DV, (h + 1) * GLA_DV)
            qm = qd[:, sl] * masks[hh]
            attn = jnp.where(incl, _bdot(qm, kh[:, sl], _NT), 0.0)
            vh = v[:, hs]
            o = _bdot(attn, vh) + _bdot(qm, s)
            of = o * lax.rsqrt(jnp.mean(o * o, axis=-1, keepdims=True) + EPS) * nw_ref[...]
            y_ref[0, :, hs] = of * _silu(g[:, hs])
            kv = _bdot(kt[:, sl], vh, _TN)
            kv_rows.append(kv[hh * GLA_DK:(hh + 1) * GLA_DK])
        pcol = _xdot_l(jnp.where(diag128, pl_row[:, sl], 0.0), ones128)
        s_scr[j] = pcol * s + jnp.concatenate(kv_rows, axis=0)

    @pl.when(c == nc - 1)
    def _():
        sf_ref[0] = s_scr[...]


def gla_mix(p_gla, s0, bsz, t, L, aup_pad, a_b, norm_w):
    tp = -(-t // L) * L
    p3 = p_gla.reshape(bsz, t, GLA_PW)
    if tp != t:
        p3 = jnp.pad(p3, ((0, 0), (0, tp - t), (0, 0)))
    n_pairs = GLA_KW // LANES
    st_spec = pl.BlockSpec((1, n_pairs, LANES, LANES), lambda b, c: (b, 0, 0, 0))
    y, s_fin = pl.pallas_call(
        functools.partial(_gla_kernel, L, t),
        grid=(bsz, tp // L),
        in_specs=[pl.BlockSpec((1, L, GLA_PW), lambda b, c: (b, c, 0)),
                  pl.BlockSpec((LANES, GLA_KW), lambda b, c: (0, 0)),
                  pl.BlockSpec((1, GLA_KW), lambda b, c: (0, 0)),
                  pl.BlockSpec((1, GLA_DV), lambda b, c: (0, 0)),
                  st_spec],
        out_specs=[pl.BlockSpec((1, L, GLA_VW), lambda b, c: (b, c, 0)), st_spec],
        out_shape=[jax.ShapeDtypeStruct((bsz, tp, GLA_VW), F32),
                   jax.ShapeDtypeStruct((bsz, n_pairs, LANES, LANES), F32)],
        scratch_shapes=[pltpu.VMEM((n_pairs, LANES, LANES), F32)],
        compiler_params=_cparams(("parallel", "arbitrary")), name="gla",
    )(p3, aup_pad, a_b.reshape(1, GLA_KW), norm_w.reshape(1, GLA_DV),
      s0.reshape(bsz, n_pairs, LANES, LANES))
    return y[:, :t].reshape(bsz * t, GLA_VW), s_fin.reshape(bsz, GLA_HEADS, GLA_DK, GLA_DV)


GLA_LR = 16
D_FF_CHUNK = 1408
D_FF_EXPERT_CHUNK = 896
MOE_TOKENS = 1024


def _prepare_weights(w):
    win = w["e_w_in"][0]
    w_gla = jnp.pad(win[:, RSHIFT:], ((0, 0), (0, LANES - GLA_LR)))
    wo = w["o_w_in"][0]
    return {
        "e_w_rwkv": win[:, :RSHIFT].astype(BF16),
        "e_w_gla": w_gla.astype(BF16),
        "rwkv": rwkv_params(w, 0),
        "gla_aup": jnp.pad(w["e_gla_a_up"][0], ((0, LANES - GLA_LR), (0, 0))).astype(BF16),
        "e_wo_a": w["e_w_out"][0][:RW].astype(BF16),
        "e_wo_b": w["e_w_out"][0][RW:].astype(BF16),
        "ff_w1": w["e_ff_w1"][0].astype(BF16),
        "ff_w3": w["e_ff_w3"][0].astype(BF16),
        "ff_w2": w["e_ff_w2"][0].astype(BF16),
        "o_w_q": wo[:, :QW].astype(BF16),
        "o_w_kv": wo[:, QW:QW + 2 * KW].astype(BF16),
        "o_w_u": wo[:, QW + 2 * KW:].astype(BF16),
        "glu_w": w["o_glu_w"][0].astype(BF16),
        "o_wo_a": w["o_w_out"][0][:QW].astype(BF16),
        "o_wo_b": w["o_w_out"][0][QW:].astype(BF16),
        "moe_w1": w["o_moe_w1"][0].astype(BF16),
        "moe_w3": w["o_moe_w3"][0].astype(BF16),
        "moe_w2": w["o_moe_w2"][0].astype(BF16),
    }


def _trunk(x3, st, w, pw, tm, chunk, s5_prep_t, prompt_bias):
    bsz, t, d = x3.shape
    m = bsz * t
    x = x3.reshape(m, d)
    p_r, p_g = norm_proj(x, w["e_norm1"][0], [pw["e_w_rwkv"], pw["e_w_gla"]], tm)
    ya, s_rwkv = rwkv_mix(p_r, st["shift"], st["rwkv"], bsz, t, chunk, tm, pw["rwkv"],
                          w["e_lnx_w"][0], w["e_lnx_b"][0])
    s_shift = p_r.reshape(bsz, t, RSHIFT)[:, -1]
    yb, s_gla = gla_mix(p_g, st["gla"], bsz, t, chunk, pw["gla_aup"], w["e_gla_a_b"][0], w["e_gla_norm"][0])
    x = out_proj(x, ya, yb, pw["e_wo_a"], pw["e_wo_b"], tm)
    x = ffn(x, w["e_norm2"][0], pw["ff_w1"], pw["ff_w3"], pw["ff_w2"], tm, D_FF_CHUNK)
    q, kv, u = norm_proj(x, w["o_norm1"][0], [pw["o_w_q"], pw["o_w_kv"], pw["o_w_u"]], tm)
    if st["win_k"] is None:
        yc, nk, nv = swa_prompt(q, kv, bsz, t, w["o_q_norm"][0], w["o_k_norm"][0], prompt_bias, w["o_sinks"][0])
    else:
        yc, nk, nv = swa_decode(q, kv, st["win_k"].reshape(bsz, WINDOW, KW), st["win_v"].reshape(bsz, WINDOW, KW),
                                bsz, t, w["o_q_norm"][0], w["o_k_norm"][0], w["rel_table"], w["o_sinks"][0], 8)
    yd, s5r, s5i = s5_mix(u, st["s5_re"], st["s5_im"], bsz, t, s5_prep_t, w["o_d"][0], pw["glu_w"],
                          w["o_glu_b"][0], tm)
    x = out_proj(x, yc, yd, pw["o_wo_a"], pw["o_wo_b"], tm)
    tmoe = min(m, MOE_TOKENS)
    hn, gates, pos, counts = router(x, w["o_norm2"][0], w["o_router_w"][0], w["o_router_b"][0], tmoe)
    x = moe(x, hn, gates, pos, counts, pw["moe_w1"], pw["moe_w3"], pw["moe_w2"], tmoe, D_FF_EXPERT_CHUNK)
    kv_shape = (bsz, WINDOW, N_KV_HEADS, ATT_HD)
    return (x.reshape(bsz, t, d), s_rwkv[None], s_shift[None], s_gla[None], nk.reshape(kv_shape)[None],
            nv.reshape(kv_shape)[None], s5r[None], s5i[None])


def kernel(x_prompt, x_sample, state_rwkv, state_shift, state_gla, cache_win_k, cache_win_v, state_s5_re,
           state_s5_im, rel_table, e_norm1, e_w_in, e_mu, e_w0, e_w_up, e_a0, e_a_up, e_g_up, e_k_k, e_k_a, e_r_k,
           e_lnx_w, e_lnx_b, e_gla_a_up, e_gla_a_b, e_gla_norm, e_w_out, e_norm2, e_ff_w1, e_ff_w3, e_ff_w2,
           o_norm1, o_w_in, o_q_norm, o_k_norm, o_sinks, o_a_re, o_a_im, o_log_dt, o_b_re, o_b_im, o_c_re, o_c_im,
           o_d, o_glu_w, o_glu_b, o_w_out, o_norm2, o_router_w, o_router_b, o_moe_w1, o_moe_w3, o_moe_w2):
    w = dict(rel_table=rel_table, e_norm1=e_norm1, e_w_in=e_w_in, e_mu=e_mu, e_w0=e_w0, e_w_up=e_w_up, e_a0=e_a0,
             e_a_up=e_a_up, e_g_up=e_g_up, e_k_k=e_k_k, e_k_a=e_k_a, e_r_k=e_r_k, e_lnx_w=e_lnx_w, e_lnx_b=e_lnx_b,
             e_gla_a_up=e_gla_a_up, e_gla_a_b=e_gla_a_b, e_gla_norm=e_gla_norm, e_w_out=e_w_out, e_norm2=e_norm2,
             e_ff_w1=e_ff_w1, e_ff_w3=e_ff_w3, e_ff_w2=e_ff_w2, o_norm1=o_norm1, o_w_in=o_w_in, o_q_norm=o_q_norm,
             o_k_norm=o_k_norm, o_sinks=o_sinks, o_a_re=o_a_re, o_a_im=o_a_im, o_log_dt=o_log_dt, o_b_re=o_b_re,
             o_b_im=o_b_im, o_c_re=o_c_re, o_c_im=o_c_im, o_d=o_d, o_glu_w=o_glu_w, o_glu_b=o_glu_b,
             o_w_out=o_w_out, o_norm2=o_norm2, o_router_w=o_router_w, o_router_b=o_router_b, o_moe_w1=o_moe_w1,
             o_moe_w3=o_moe_w3, o_moe_w2=o_moe_w2)
    pw = _prepare_weights(w)
    bp, tp, _ = x_prompt.shape
    bs, ts, _ = x_sample.shape
    qi = np.arange(WINDOW)[:, None]
    kj = np.arange(2 * WINDOW)[None, :]
    prompt_bias = rel_bias(rel_table, qi + WINDOW - kj)
    zeros = lambda *shape: jnp.zeros(shape, F32)
    st_p = {"rwkv": zeros(bp, RW // RWKV_HD, RWKV_HD, RWKV_HD), "shift": zeros(bp, RSHIFT),
            "gla": zeros(bp, GLA_HEADS, GLA_DK, GLA_DV), "win_k": None, "win_v": None,
            "s5_re": zeros(bp, S5_G, S5_N), "s5_im": zeros(bp, S5_G, S5_N)}
    st_s = {"rwkv": state_rwkv[0], "shift": state_shift[0], "gla": state_gla[0], "win_k": cache_win_k[0],
            "win_v": cache_win_v[0], "s5_re": state_s5_re[0], "s5_im": state_s5_im[0]}
    out_p = _trunk(x_prompt, st_p, w, pw, 512, 64, s5_prep(w, 0, S5_CHUNK), prompt_bias)
    out_s = _trunk(x_sample, st_s, w, pw, bs * ts, 8, s5_prep(w, 0, ts), None)
    res = [out_p[0], out_s[0]]
    for a, b in zip(out_p[1:], out_s[1:]):
        res += [a, b]
    return tuple(res)
```

```python
import functools
import math

import jax
import jax.numpy as jnp
import numpy as np
from jax import lax
from jax.experimental import pallas as pl
from jax.experimental.pallas import tpu as pltpu

F32 = jnp.float32
BF16 = jnp.bfloat16

LANES = 128
SUBLANES = 8
VMEM_LIMIT_BYTES = 56 * 1024 * 1024

EPS = 1e-6
RWKV_HD = 64
RWKV_GN_EPS = 64e-5
GLA_GATE_NORM = 16.0
ATT_HD = 64
WINDOW = 128
N_BUCKETS = 32
BUCKET_MAX_DIST = 128
S5_P = 16
S5_N = 64
S5_CHUNK = 16
TOP_K = 2


def _cparams(sem):
    return pltpu.CompilerParams(dimension_semantics=sem, vmem_limit_bytes=VMEM_LIMIT_BYTES)


_NN = (((1,), (0,)), ((), ()))
_NT = (((1,), (1,)), ((), ()))
_TN = (((0,), (0,)), ((), ()))


def _dg(a, b, dims):
    return lax.dot_general(a, b, dims, preferred_element_type=F32)


def _bdot(a, b, dims=_NN):
    return _dg(a.astype(BF16), b.astype(BF16), dims)


def _split(a, n):
    terms = []
    r = a
    for _ in range(n):
        t = r.astype(BF16)
        terms.append(t)
        r = r - t.astype(F32)
    return terms


def _hdot(a, b, dims=_NN):
    a0, a1 = _split(a, 2)
    b0, b1 = _split(b, 2)
    return _dg(a0, b0, dims) + (_dg(a0, b1, dims) + _dg(a1, b0, dims))


def _xdot_l(a, e, dims=_NN):
    e = e.astype(BF16)
    a0, a1, a2 = _split(a, 3)
    return _dg(a0, e, dims) + (_dg(a1, e, dims) + _dg(a2, e, dims))


def _xdot_r(e, b, dims=_NN):
    e = e.astype(BF16)
    b0, b1, b2 = _split(b, 3)
    return _dg(e, b0, dims) + (_dg(e, b1, dims) + _dg(e, b2, dims))


def _iota(shape, axis):
    return lax.broadcasted_iota(jnp.int32, shape, axis)


def _seg_ones(n, seg):
    r = _iota((n, n), 0) // seg
    c = _iota((n, n), 1) // seg
    return jnp.where(r == c, 1.0, 0.0).astype(BF16)


def _seg_sum(x, seg):
    n = x.shape[-1]
    return _xdot_l(x, _seg_ones(n, seg))


def _sigmoid(x):
    return 1.0 / (1.0 + jnp.exp(-x))


def _silu(x):
    return x * _sigmoid(x)


def _softplus(x):
    return jnp.maximum(x, 0.0) + jnp.log(1.0 + jnp.exp(-jnp.abs(x)))


def _tri_incl(n):
    r = _iota((n, n), 0)
    c = _iota((n, n), 1)
    return jnp.where(c <= r, 1.0, 0.0).astype(BF16)


def _rms(x, g):
    return x * lax.rsqrt(jnp.mean(x * x, axis=-1, keepdims=True) + EPS) * g


def _norm_proj_kernel(n_w, x_ref, g_ref, *refs):
    xn = _rms(x_ref[...], g_ref[...]).astype(BF16)
    for w_ref, o_ref in zip(refs[:n_w], refs[n_w:]):
        o_ref[...] = _dg(xn, w_ref[...], _NN)


def norm_proj(x, g, ws_bf16, tm):
    m, d = x.shape
    return pl.pallas_call(
        functools.partial(_norm_proj_kernel, len(ws_bf16)),
        grid=(m // tm,),
        in_specs=[pl.BlockSpec((tm, d), lambda i: (i, 0)),
                  pl.BlockSpec((1, d), lambda i: (0, 0))]
                 + [pl.BlockSpec(w.shape, lambda i: (0, 0)) for w in ws_bf16],
        out_specs=[pl.BlockSpec((tm, w.shape[1]), lambda i: (i, 0)) for w in ws_bf16],
        out_shape=[jax.ShapeDtypeStruct((m, w.shape[1]), F32) for w in ws_bf16],
        compiler_params=_cparams(("parallel",)), name="norm_proj",
    )(x, g.reshape(1, d), *ws_bf16)


def _out_proj_kernel(x_ref, ya_ref, yb_ref, wa_ref, wb_ref, o_ref):
    o_ref[...] = (x_ref[...] + _dg(ya_ref[...].astype(BF16), wa_ref[...], _NN)
                  + _dg(yb_ref[...].astype(BF16), wb_ref[...], _NN))


def out_proj(x, ya, yb, wa, wb, tm):
    m, d = x.shape
    return pl.pallas_call(
        _out_proj_kernel,
        grid=(m // tm,),
        in_specs=[pl.BlockSpec((tm, d), lambda i: (i, 0)),
                  pl.BlockSpec((tm, ya.shape[1]), lambda i: (i, 0)),
                  pl.BlockSpec((tm, yb.shape[1]), lambda i: (i, 0)),
                  pl.BlockSpec(wa.shape, lambda i: (0, 0)),
                  pl.BlockSpec(wb.shape, lambda i: (0, 0))],
        out_specs=pl.BlockSpec((tm, d), lambda i: (i, 0)),
        out_shape=jax.ShapeDtypeStruct((m, d), F32),
        compiler_params=_cparams(("parallel",)), name="out_proj",
    )(x, ya, yb, wa, wb)


def _ffn_kernel(x_ref, g_ref, w1_ref, w3_ref, w2_ref, o_ref, xn_scr):
    j = pl.program_id(1)

    @pl.when(j == 0)
    def _():
        x = x_ref[...]
        xn_scr[...] = _rms(x, g_ref[...]).astype(BF16)
        o_ref[...] = x

    xn = xn_scr[...]
    h = _silu(_dg(xn, w1_ref[...], _NN)) * _dg(xn, w3_ref[...], _NN)
    o_ref[...] += _dg(h.astype(BF16), w2_ref[...], _NN)


def ffn(x, g, w1, w3, w2, tm, fc):
    m, d = x.shape
    dff = w1.shape[1]
    return pl.pallas_call(
        _ffn_kernel,
        grid=(m // tm, dff // fc),
        in_specs=[pl.BlockSpec((tm, d), lambda i, j: (i, 0)),
                  pl.BlockSpec((1, d), lambda i, j: (0, 0)),
                  pl.BlockSpec((d, fc), lambda i, j: (0, j)),
                  pl.BlockSpec((d, fc), lambda i, j: (0, j)),
                  pl.BlockSpec((fc, d), lambda i, j: (j, 0))],
        out_specs=pl.BlockSpec((tm, d), lambda i, j: (i, 0)),
        out_shape=jax.ShapeDtypeStruct((m, d), F32),
        scratch_shapes=[pltpu.VMEM((tm, d), BF16)],
        compiler_params=_cparams(("parallel", "arbitrary")), name="ffn",
    )(x, g.reshape(1, d), w1, w3, w2)


def _router_kernel(n_exp, x_ref, g_ref, rw_ref, rb_ref, hn_ref, gate_ref, pos_ref, cnt_ref):
    xn = _rms(x_ref[...], g_ref[...])
    hn_ref[...] = xn.astype(BF16)
    logits = _hdot(xn, rw_ref[...]) + rb_ref[...]
    lane = _iota(logits.shape, 1)
    logits = jnp.where(lane < n_exp, logits, -jnp.inf)
    m1 = jnp.max(logits, axis=-1, keepdims=True)
    i1 = jnp.min(jnp.where(logits == m1, lane, LANES), axis=-1, keepdims=True)
    rest = jnp.where(lane == i1, -jnp.inf, logits)
    m2 = jnp.max(rest, axis=-1, keepdims=True)
    i2 = jnp.min(jnp.where(rest == m2, lane, LANES), axis=-1, keepdims=True)
    e2 = jnp.exp(m2 - m1)
    g1 = 1.0 / (1.0 + e2)
    g2 = e2 / (1.0 + e2)
    pick1 = lane == i1
    pick2 = lane == i2
    gate_ref[...] = jnp.where(pick1, g1, 0.0) + jnp.where(pick2, g2, 0.0)
    tm = logits.shape[0]
    sel = jnp.where(jnp.logical_or(pick1, pick2), 1.0, 0.0).astype(BF16)
    tr = _iota((tm, tm), 0)
    tc = _iota((tm, tm), 1)
    upper = jnp.where(tr <= tc, 1.0, 0.0).astype(BF16)
    eye = jnp.where(tr == tc, 1.0, 0.0).astype(BF16)
    rank_t = _dg(sel, upper, _TN)
    sel_t = _dg(sel, eye, _TN)
    pos_t = jnp.where(sel_t > 0.5, rank_t - 1.0, -1.0)
    pos_ref[...] = pos_t[0:SUBLANES, :]
    cnt_ref[0] = jnp.sum(sel.astype(F32), axis=0, keepdims=True)


def router(x, g, rw, rb, tm):
    m, d = x.shape
    n_exp = rw.shape[1]
    assert n_exp <= SUBLANES
    rw_pad = jnp.pad(rw, ((0, 0), (0, LANES - n_exp)))
    rb_pad = jnp.pad(rb, (0, LANES - n_exp)).reshape(1, LANES)
    return pl.pallas_call(
        functools.partial(_router_kernel, n_exp),
        grid=(m // tm,),
        in_specs=[pl.BlockSpec((tm, d), lambda i: (i, 0)),
                  pl.BlockSpec((1, d), lambda i: (0, 0)),
                  pl.BlockSpec((d, LANES), lambda i: (0, 0)),
                  pl.BlockSpec((1, LANES), lambda i: (0, 0))],
        out_specs=[pl.BlockSpec((tm, d), lambda i: (i, 0)),
                   pl.BlockSpec((tm, LANES), lambda i: (i, 0)),
                   pl.BlockSpec((SUBLANES, tm), lambda i: (0, i)),
                   pl.BlockSpec((1, 1, LANES), lambda i: (i, 0, 0))],
        out_shape=[jax.ShapeDtypeStruct((m, d), BF16), jax.ShapeDtypeStruct((m, LANES), F32),
                   jax.ShapeDtypeStruct((SUBLANES, m), F32),
                   jax.ShapeDtypeStruct((m // tm, 1, LANES), F32)],
        compiler_params=_cparams(("parallel",)), name="router",
    )(x, g.reshape(1, d), rw_pad, rb_pad)


MOE_ROWS = 128


def _moe_kernel(n_exp, cnt_ref, x_hbm, hn_ref, gate_ref, pos_ref, w1_ref, w3_ref, w2_ref, o_ref, xs_scr, y_scr):
    i = pl.program_id(0)
    e = pl.program_id(1)
    j = pl.program_id(2)
    nj = pl.num_programs(2)
    tm = hn_ref.shape[0]
    n_small = (cnt_ref[i * n_exp + e] + (MOE_ROWS - 1)) // MOE_ROWS
    n_big = n_small // 2
    tail = n_small % 2

    @pl.when(jnp.logical_and(e == 0, j == 0))
    def _():
        pltpu.sync_copy(x_hbm.at[pl.ds(pl.multiple_of(i * tm, tm), tm), :], o_ref)

    def select(start, nrows):
        pos = pos_ref[pl.ds(e, 1), :]
        want = (start + _iota((nrows, tm), 0)).astype(F32)
        return jnp.where(pos == want, 1.0, 0.0).astype(BF16)

    def blocks(body):
        def big(blk, carry):
            body(pl.multiple_of(blk * 2 * MOE_ROWS, 2 * MOE_ROWS), 2 * MOE_ROWS)
            return carry
        lax.fori_loop(0, n_big, big, 0)

        @pl.when(tail == 1)
        def _():
            body(pl.multiple_of(n_big * 2 * MOE_ROWS, 2 * MOE_ROWS), MOE_ROWS)

    @pl.when(j == 0)
    def _():
        def gather(start, nrows):
            rows = pl.ds(start, nrows)
            xs_scr[rows, :] = _dg(select(start, nrows), hn_ref[...], _NN).astype(BF16)
            y_scr[rows, :] = jnp.zeros((nrows, y_scr.shape[1]), F32)
        blocks(gather)

    def expert(start, nrows):
        rows = pl.ds(start, nrows)
        xs = xs_scr[rows, :]
        h = _silu(_dg(xs, w1_ref[0], _NN)) * _dg(xs, w3_ref[0], _NN)
        y_scr[rows, :] += _dg(h.astype(BF16), w2_ref[0], _NN)
    blocks(expert)

    @pl.when(j == nj - 1)
    def _():
        gt = gate_ref[...]
        gcol = jnp.sum(jnp.where(_iota(gt.shape, 1) == e, gt, 0.0), axis=-1, keepdims=True)

        def scatter(start, nrows):
            o_ref[...] += gcol * _dg(select(start, nrows), y_scr[pl.ds(start, nrows), :].astype(BF16), _TN)
        blocks(scatter)


def moe(x, hn, gates, pos, counts, w1, w3, w2, tm_router, tm, fc):
    m, d = x.shape
    n_exp, _, dff = w1.shape
    ratio = tm // tm_router
    cnt = counts[:, 0, :n_exp].astype(jnp.int32).reshape(m // tm, ratio, n_exp)
    before = (jnp.cumsum(cnt, axis=1) - cnt).astype(F32)
    shift = jnp.repeat(before.reshape(m // tm_router, n_exp).T, tm_router, axis=1)
    shift = jnp.pad(shift, ((0, SUBLANES - n_exp), (0, 0)))
    pos = jnp.where(pos >= 0, pos + shift, pos)
    cnt = cnt.sum(axis=1).reshape(-1)
    grid_spec = pltpu.PrefetchScalarGridSpec(
        num_scalar_prefetch=1,
        grid=(m // tm, n_exp, dff // fc),
        in_specs=[pl.BlockSpec(memory_space=pl.ANY),
                  pl.BlockSpec((tm, d), lambda i, e, j, c: (i, 0)),
                  pl.BlockSpec((tm, LANES), lambda i, e, j, c: (i, 0)),
                  pl.BlockSpec((SUBLANES, tm), lambda i, e, j, c: (0, i)),
                  pl.BlockSpec((1, d, fc), lambda i, e, j, c: (e, 0, j)),
                  pl.BlockSpec((1, d, fc), lambda i, e, j, c: (e, 0, j)),
                  pl.BlockSpec((1, fc, d), lambda i, e, j, c: (e, j, 0))],
        out_specs=pl.BlockSpec((tm, d), lambda i, e, j, c: (i, 0)),
        scratch_shapes=[pltpu.VMEM((tm, d), BF16), pltpu.VMEM((tm, d), F32)])
    return pl.pallas_call(
        functools.partial(_moe_kernel, n_exp),
        grid_spec=grid_spec,
        out_shape=jax.ShapeDtypeStruct((m, d), F32),
        compiler_params=_cparams(("parallel", "arbitrary", "arbitrary")), name="moe",
    )(cnt, x, hn, gates, pos, w1, w3, w2)


RW = 512
RSHIFT = 1792


def _rwkv_pre_kernel(t_per_batch, tm, p_ref, pb_ref, prev_ref, mu_ref, w0_ref, wup_ref, a0_ref, aup_ref,
                     gup_ref, kk_ref, ka_ref, rk_ref,
                     r_o, lw_o, k_o, v_o, al_o, be_o, g_o, bo_o):
    i = pl.program_id(0)
    p = p_ref[...]
    rolled = pltpu.roll(p, 1, 0)
    row = _iota((tm, 1), 0)
    rolled = jnp.where(row == 0, pb_ref[SUBLANES - 1:SUBLANES, :], rolled)
    if t_per_batch >= tm:
        first = (i * tm) % t_per_batch == 0
        is_start = jnp.logical_and(row == 0, first)
        prev = jnp.where(is_start, prev_ref[0], rolled)
    else:
        is_start = (row % t_per_batch) == 0
        prev = jnp.where(is_start, prev_ref[...], rolled)
    xs = p + (prev - p) * mu_ref[...]
    r = xs[:, 0:RW]
    k = xs[:, RW:2 * RW]
    v = xs[:, 2 * RW:3 * RW]
    lr = xs[:, 3 * RW:3 * RW + LANES]
    gd = xs[:, 3 * RW + LANES:3 * RW + 2 * LANES]
    w_pre = w0_ref[...] + _bdot(jnp.tanh(lr), wup_ref[...])
    logw = -jnp.exp(-_softplus(-w_pre) - 0.5)
    a = _sigmoid(a0_ref[...] + _bdot(lr, aup_ref[...]))
    g = _bdot(_sigmoid(gd), gup_ref[...])
    kkr = k * kk_ref[...]
    kk = kkr / jnp.maximum(jnp.sqrt(_seg_sum(kkr * kkr, RWKV_HD)), 1e-12)
    k2 = k * (1.0 + (a - 1.0) * ka_ref[...])
    bonus = _seg_sum(r * k2 * rk_ref[...], RWKV_HD) * v
    r_o[...] = r
    lw_o[...] = logw
    k_o[...] = k2
    v_o[...] = v
    al_o[...] = -kk
    be_o[...] = kk * a
    g_o[...] = g
    bo_o[...] = bonus


def rwkv_pre(p_full, prev_rows, t_per_batch, tm, prm):
    m = p_full.shape[0]
    nb8 = tm // SUBLANES
    row_spec = pl.BlockSpec((1, RW), lambda i: (0, 0))
    if t_per_batch >= tm:
        prev_spec = pl.BlockSpec((1, 1, RSHIFT), lambda i: ((i * tm) // t_per_batch, 0, 0))
    else:
        prev_spec = pl.BlockSpec((tm, RSHIFT), lambda i: (i, 0))
    out_sds = jax.ShapeDtypeStruct((m, RW), F32)
    out_spec = pl.BlockSpec((tm, RW), lambda i: (i, 0))
    return pl.pallas_call(
        functools.partial(_rwkv_pre_kernel, t_per_batch, tm),
        grid=(m // tm,),
        in_specs=[pl.BlockSpec((tm, RSHIFT), lambda i: (i, 0)),
                  pl.BlockSpec((SUBLANES, RSHIFT), lambda i: (jnp.maximum(i * nb8 - 1, 0), 0)),
                  prev_spec,
                  pl.BlockSpec((1, RSHIFT), lambda i: (0, 0)),
                  row_spec,
                  pl.BlockSpec((LANES, RW), lambda i: (0, 0)),
                  row_spec,
                  pl.BlockSpec((LANES, RW), lambda i: (0, 0)),
                  pl.BlockSpec((LANES, RW), lambda i: (0, 0)),
                  row_spec, row_spec, row_spec],
        out_specs=[out_spec] * 8,
        out_shape=[out_sds] * 8,
        compiler_params=_cparams(("parallel",)), name="rwkv_pre",
    )(p_full, p_full, prev_rows, prm["mu"], prm["w0"], prm["wup"], prm["a0"], prm["aup"], prm["gup"],
      prm["k_k"], prm["k_a"], prm["r_k"])


_RW_DOT_A = _bdot
_RW_DOT_T = _bdot
_RW_DOT_W = _bdot
_RW_DOT_S = _hdot


def _rwkv_scan_kernel(L, nb, n_pairs, r_ref, lw_ref, k_ref, v_ref, al_ref, be_ref, s0_ref,
                      y_ref, sf_ref, s_scr):
    c = pl.program_id(1)
    nc = pl.num_programs(1)
    L2 = 2 * L
    lane = _iota((1, LANES), 1)
    m0 = jnp.where(lane < RWKV_HD, 1.0, 0.0)
    m1 = 1.0 - m0
    rr = _iota((L2, L2), 0)
    cc = _iota((L2, L2), 1)
    same = (rr // L) == (cc // L)
    strict = jnp.logical_and(same, (cc % L) < (rr % L))
    incl = jnp.logical_and(same, (cc % L) <= (rr % L))
    eye2 = jnp.where(rr == cc, 1.0, 0.0)
    r128 = _iota((LANES, LANES), 0)
    c128 = _iota((LANES, LANES), 1)
    blk128 = (r128 // RWKV_HD) == (c128 // RWKV_HD)
    diag128 = r128 == c128
    fmat = jnp.where(_iota((LANES, RWKV_HD), 0) % RWKV_HD == _iota((LANES, RWKV_HD), 1), 1.0, 0.0)
    tri = _tri_incl(L)

    def bd(x):
        return jnp.concatenate([x * m0, x * m1], axis=0)

    @pl.when(c == 0)
    def _():
        for bb in range(nb):
            for j in range(n_pairs):
                s0 = s0_ref[bb, j]
                st = _xdot_r(fmat, s0, _NT)
                s_scr[bb * n_pairs + j] = jnp.where(blk128, st, 0.0)

    pairs = range(nb * n_pairs)
    bbs = [i // n_pairs for i in pairs]
    sls = [slice((i % n_pairs) * LANES, (i % n_pairs + 1) * LANES) for i in pairs]
    lw = [lw_ref[bbs[j], :, sls[j]] for j in pairs]
    b = [_xdot_r(tri, x) for x in lw]
    bl = [x[L - 1:L, :] for x in b]
    e_b = [jnp.exp(x) for x in b]
    e_nb = [jnp.exp(-x) for x in b]
    e_lb = [jnp.exp(bl[j] - b[j]) for j in pairs]
    at = [bd(al_ref[bbs[j], :, sls[j]] * jnp.exp(b[j] - lw[j])) for j in pairs]
    rt = [bd(r_ref[bbs[j], :, sls[j]] * e_b[j]) for j in pairs]
    bh = [bd(be_ref[bbs[j], :, sls[j]] * e_nb[j]) for j in pairs]
    kh = [bd(k_ref[bbs[j], :, sls[j]] * e_nb[j]) for j in pairs]
    bt = [bd(be_ref[bbs[j], :, sls[j]] * e_lb[j]) for j in pairs]
    kt = [bd(k_ref[bbs[j], :, sls[j]] * e_lb[j]) for j in pairs]
    vb = [bd(v_ref[bbs[j], :, sls[j]]) for j in pairs]
    a_ab = [jnp.where(strict, _RW_DOT_A(at[j], bh[j], _NT), 0.0) for j in pairs]
    a_ak = [jnp.where(strict, _RW_DOT_A(at[j], kh[j], _NT), 0.0) for j in pairs]
    a_rb = [jnp.where(incl, _RW_DOT_A(rt[j], bh[j], _NT), 0.0) for j in pairs]
    a_rk = [jnp.where(incl, _RW_DOT_A(rt[j], kh[j], _NT), 0.0) for j in pairs]
    x = a_ab
    tinv = [eye2 + a for a in a_ab]
    span = 2
    while span < L:
        x = [_RW_DOT_T(xx, xx) for xx in x]
        tinv = [tinv[j] + _RW_DOT_T(tinv[j], x[j]) for j in pairs]
        span *= 2
    akv = [_RW_DOT_W(a_ak[j], vb[j]) for j in pairs]
    w = [_RW_DOT_W(tinv[j], at[j]) for j in pairs]
    uv = [_RW_DOT_W(tinv[j], akv[j]) for j in pairs]
    mm = [jnp.where(diag128, jnp.exp(bl[j]), 0.0) + _RW_DOT_W(bt[j], w[j], _TN) for j in pairs]
    nn = [_RW_DOT_W(bt[j], uv[j], _TN) + _RW_DOT_W(kt[j], vb[j], _TN) for j in pairs]
    q = [rt[j] + _RW_DOT_W(a_rb[j], w[j]) for j in pairs]
    yv = [_RW_DOT_W(a_rb[j], uv[j]) + _RW_DOT_W(a_rk[j], vb[j]) for j in pairs]
    s = [s_scr[j] for j in pairs]
    for j in pairs:
        ybd = _RW_DOT_S(q[j], s[j]) + yv[j]
        y_ref[bbs[j], :, sls[j]] = ybd[0:L] + ybd[L:L2]
    for j in pairs:
        s_scr[j] = _RW_DOT_S(mm[j], s[j]) + nn[j]

    @pl.when(c == nc - 1)
    def _():
        for i in pairs:
            sf_ref[bbs[i], i % n_pairs] = _xdot_l(s_scr[i], fmat, _TN)


def rwkv_scan(ops, s0, L, nb):
    bsz, t, _ = ops[0].shape
    n_pairs = RW // LANES
    blk = pl.BlockSpec((nb, L, RW), lambda b, c: (b, c, 0))
    st_spec = pl.BlockSpec((nb, n_pairs, LANES, RWKV_HD), lambda b, c: (b, 0, 0, 0))
    return pl.pallas_call(
        functools.partial(_rwkv_scan_kernel, L, nb, n_pairs),
        grid=(bsz // nb, t // L),
        in_specs=[blk] * 6 + [st_spec],
        out_specs=[blk, st_spec],
        out_shape=[jax.ShapeDtypeStruct((bsz, t, RW), F32),
                   jax.ShapeDtypeStruct((bsz, n_pairs, LANES, RWKV_HD), F32)],
        scratch_shapes=[pltpu.VMEM((nb * n_pairs, LANES, LANES), F32)],
        compiler_params=_cparams(("parallel", "arbitrary")), name="rwkv_scan",
    )(*ops, s0)


def _rwkv_post_kernel(y_ref, bo_ref, g_ref, lw_ref, lb_ref, o_ref):
    y = y_ref[...]
    mean = _seg_sum(y, RWKV_HD) * (1.0 / RWKV_HD)
    d = y - mean
    var = _seg_sum(d * d, RWKV_HD) * (1.0 / RWKV_HD)
    yn = d * lax.rsqrt(var + RWKV_GN_EPS) * lw_ref[...] + lb_ref[...]
    o_ref[...] = (yn + bo_ref[...]) * g_ref[...]


def rwkv_post(y, bonus, g, lnx_w, lnx_b, tm):
    m = y.shape[0]
    spec = pl.BlockSpec((tm, RW), lambda i: (i, 0))
    row = pl.BlockSpec((1, RW), lambda i: (0, 0))
    return pl.pallas_call(
        _rwkv_post_kernel,
        grid=(m // tm,),
        in_specs=[spec, spec, spec, row, row],
        out_specs=spec,
        out_shape=jax.ShapeDtypeStruct((m, RW), F32),
        compiler_params=_cparams(("parallel",)), name="rwkv_post",
    )(y, bonus, g, lnx_w.reshape(1, RW), lnx_b.reshape(1, RW))


def rwkv_params(w, i):
    z64 = jnp.zeros((RWKV_HD, RW), F32)
    return {
        "mu": w["e_mu"][i].reshape(1, RSHIFT),
        "w0": w["e_w0"][i].reshape(1, RW),
        "wup": jnp.concatenate([w["e_w_up"][i], z64], 0).astype(BF16),
        "a0": w["e_a0"][i].reshape(1, RW),
        "aup": jnp.concatenate([z64, w["e_a_up"][i]], 0).astype(BF16),
        "gup": w["e_g_up"][i].astype(BF16),
        "k_k": w["e_k_k"][i].reshape(1, RW),
        "k_a": w["e_k_a"][i].reshape(1, RW),
        "r_k": w["e_r_k"][i].reshape(1, RW),
    }


def rwkv_mix(p_flat, prev, s0, bsz, t, L, tm, prm, lnx_w, lnx_b, nb=1):
    m = bsz * t
    if t >= tm:
        prev_rows = prev.reshape(bsz, 1, RSHIFT)
    else:
        prev_rows = jnp.repeat(prev, t, axis=0)
    r, lw, k, v, al, be, g, bonus = rwkv_pre(p_flat, prev_rows, t, tm, prm)
    tp = -(-t // L) * L
    ops = [z.reshape(bsz, t, RW) for z in (r, lw, k, v, al, be)]
    if tp != t:
        ops = [jnp.pad(z, ((0, 0), (0, tp - t), (0, 0))) for z in ops]
    y, s_fin = rwkv_scan(ops, s0.reshape(bsz, RW // LANES, LANES, RWKV_HD), L, nb)
    y = y[:, :t].reshape(m, RW)
    ya = rwkv_post(y, bonus, g, lnx_w, lnx_b, tm)
    return ya, s_fin.reshape(bsz, RW // RWKV_HD, RWKV_HD, RWKV_HD)


N_Q_HEADS = 8
N_KV_HEADS = 2
Q_PER_KV = N_Q_HEADS // N_KV_HEADS
QW = N_Q_HEADS * ATT_HD
KW = N_KV_HEADS * ATT_HD
NEG = -1e30


def _t5_bucket_np(dist):
    n = np.maximum(dist, 0)
    max_exact = N_BUCKETS // 2
    nf = np.maximum(n, 1).astype(np.float32)
    large = max_exact + (np.log(nf / np.float32(max_exact)) / np.float32(math.log(BUCKET_MAX_DIST / max_exact))
                         * np.float32(N_BUCKETS - max_exact)).astype(np.int32)
    large = np.minimum(large, N_BUCKETS - 1)
    return np.where(n < max_exact, n, large)


def _bias_kernel(rt_ref, oh_ref, o_ref):
    o_ref[...] = _xdot_l(rt_ref[...], oh_ref[...])


def rel_bias(rel_table, dist):
    bucket = _t5_bucket_np(dist).reshape(-1)
    n = bucket.shape[0]
    onehot = jnp.asarray((np.arange(N_BUCKETS)[:, None] == bucket[None, :]).astype(np.float32), BF16)
    out = pl.pallas_call(
        _bias_kernel,
        out_shape=jax.ShapeDtypeStruct((N_Q_HEADS, n), F32),
    )(rel_table.T, onehot)
    return out.reshape((N_Q_HEADS,) + dist.shape)


def _head_norm(x, w_row):
    return x * lax.rsqrt(_seg_sum(x * x, ATT_HD) * (1.0 / ATT_HD) + EPS) * w_row


def _swa_prompt_kernel(q_ref, kvc_ref, kvp_ref, qw_ref, kw_ref, bias_ref, sink_ref, o_ref, ko_ref, vo_ref):
    i = pl.program_id(1)
    qn = _head_norm(q_ref[0], qw_ref[...])
    kvc = kvc_ref[0]
    kvp = kvp_ref[0]
    kcn = _head_norm(kvc[:, 0:KW], kw_ref[...])
    kpn = _head_norm(kvp[:, 0:KW], kw_ref[...])
    vc = kvc[:, KW:2 * KW]
    kcat = jnp.concatenate([kpn, kcn], axis=0).astype(BF16)
    vcat = jnp.concatenate([kvp[:, KW:2 * KW], vc], axis=0).astype(BF16)
    qi = _iota((WINDOW, 2 * WINDOW), 0)
    kj = _iota((WINDOW, 2 * WINDOW), 1)
    dist = qi + WINDOW - kj
    valid = jnp.logical_and(jnp.logical_and(dist >= 0, dist < WINDOW), jnp.logical_or(kj >= WINDOW, i > 0))
    lane = _iota((1, LANES), 1)
    masks = (jnp.where(lane < ATT_HD, 1.0, 0.0), jnp.where(lane < ATT_HD, 0.0, 1.0))
    heads = range(N_Q_HEADS)
    kvs = [h // Q_PER_KV for h in heads]
    qts = [qn[:, (h // 2) * LANES:(h // 2 + 1) * LANES] for h in heads]
    qts = [pltpu.roll(qts[h], ATT_HD, 1) if h % 2 != kvs[h] else qts[h] for h in heads]
    qms = [(qts[h] * masks[kvs[h]]).astype(BF16) for h in heads]
    s_all = _dg(jnp.concatenate(qms, axis=0), kcat, _NT) * (ATT_HD ** -0.5)
    logits = [jnp.where(valid, s_all[h * WINDOW:(h + 1) * WINDOW] + bias_ref[h], NEG) for h in heads]
    sinks = [sink_ref[h:h + 1, 0:1] for h in heads]
    mx = [jnp.maximum(jnp.max(logits[h], axis=-1, keepdims=True), sinks[h]) for h in heads]
    pr = [jnp.exp(logits[h] - mx[h]) for h in heads]
    den = [jnp.sum(pr[h], axis=-1, keepdims=True) + jnp.exp(sinks[h] - mx[h]) for h in heads]
    probs = jnp.concatenate([(pr[h] * (1.0 / den[h])).astype(BF16) for h in heads], axis=0)
    o_all = _dg(probs, vcat, _NN)
    os_ = [o_all[h * WINDOW:(h + 1) * WINDOW] for h in heads]
    os_ = [pltpu.roll(os_[h], ATT_HD, 1) if h % 2 != kvs[h] else os_[h] for h in heads]
    for jq in range(QW // LANES):
        o_ref[0, :, jq * LANES:(jq + 1) * LANES] = os_[2 * jq] * masks[0] + os_[2 * jq + 1] * masks[1]
    ko_ref[0] = kcn
    vo_ref[0] = vc


def swa_prompt(q, kv, bsz, t, q_norm, k_norm, bias, sinks):
    nb = t // WINDOW
    q3 = q.reshape(bsz, t, QW)
    kv3 = kv.reshape(bsz, t, 2 * KW)
    o, ko, vo = pl.pallas_call(
        _swa_prompt_kernel,
        grid=(bsz, nb),
        in_specs=[pl.BlockSpec((1, WINDOW, QW), lambda b, i: (b, i, 0)),
                  pl.BlockSpec((1, WINDOW, 2 * KW), lambda b, i: (b, i, 0)),
                  pl.BlockSpec((1, WINDOW, 2 * KW), lambda b, i: (b, jnp.maximum(i - 1, 0), 0)),
                  pl.BlockSpec((1, QW), lambda b, i: (0, 0)),
                  pl.BlockSpec((1, KW), lambda b, i: (0, 0)),
                  pl.BlockSpec((N_Q_HEADS, WINDOW, 2 * WINDOW), lambda b, i: (0, 0, 0)),
                  pl.BlockSpec((N_Q_HEADS, LANES), lambda b, i: (0, 0))],
        out_specs=[pl.BlockSpec((1, WINDOW, QW), lambda b, i: (b, i, 0)),
                   pl.BlockSpec((1, WINDOW, KW), lambda b, i: (b, 0, 0)),
                   pl.BlockSpec((1, WINDOW, KW), lambda b, i: (b, 0, 0))],
        out_shape=[jax.ShapeDtypeStruct((bsz, t, QW), F32),
                   jax.ShapeDtypeStruct((bsz, WINDOW, KW), F32),
                   jax.ShapeDtypeStruct((bsz, WINDOW, KW), F32)],
        compiler_params=_cparams(("parallel", "arbitrary")), name="swa_prompt",
    )(q3, kv3, kv3, jnp.tile(q_norm, N_Q_HEADS).reshape(1, QW), jnp.tile(k_norm, N_KV_HEADS).reshape(1, KW),
      bias, jnp.broadcast_to(sinks[:, None], (N_Q_HEADS, LANES)))
    return o.reshape(bsz * t, QW), ko, vo


DEC_TP = 8


def _swa_decode_kernel(nbt, t_real, q_ref, kv_ref, ck_ref, cv_ref, qw_ref, kw_ref, bc_ref, bn_ref, sink_ref,
                       o_ref, ko_ref, vo_ref):
    rows = Q_PER_KV * DEC_TP
    tq = _iota((rows, WINDOW), 0) % DEC_TP
    valid_c = _iota((rows, WINDOW), 1) > tq
    jn = _iota((rows, DEC_TP), 1)
    valid_n = jnp.logical_and(jn <= _iota((rows, DEC_TP), 0) % DEC_TP, jn < t_real)
    lane = _iota((1, LANES), 1)
    masks = (jnp.where(lane < ATT_HD, 1.0, 0.0), jnp.where(lane < ATT_HD, 0.0, 1.0))
    row8 = _iota((DEC_TP, 1), 0)
    for b in range(nbt):
        qn = _head_norm(q_ref[b], qw_ref[...])
        kvn = kv_ref[b]
        knew = _head_norm(kvn[:, 0:KW], kw_ref[...])
        vnew = kvn[:, KW:2 * KW]
        kc = ck_ref[b]
        vc = cv_ref[b]
        tiles = [None] * (QW // LANES)
        for kv in range(N_KV_HEADS):
            pieces = []
            for g in range(Q_PER_KV):
                h = kv * Q_PER_KV + g
                qt = qn[:, (h // 2) * LANES:(h // 2 + 1) * LANES]
                if h % 2 != kv:
                    qt = pltpu.roll(qt, ATT_HD, 1)
                pieces.append(qt * masks[kv])
            qg = jnp.concatenate(pieces, axis=0)
            l_c = jnp.where(valid_c, _bdot(qg, kc, _NT) * (ATT_HD ** -0.5) + bc_ref[kv], NEG)
            l_n = jnp.where(valid_n, _bdot(qg, knew, _NT) * (ATT_HD ** -0.5) + bn_ref[kv][:, 0:DEC_TP], NEG)
            sink = sink_ref[kv][:, 0:1]
            mx = jnp.maximum(jnp.maximum(jnp.max(l_c, axis=-1, keepdims=True),
                                         jnp.max(l_n, axis=-1, keepdims=True)), sink)
            p_c = jnp.exp(l_c - mx)
            p_n = jnp.exp(l_n - mx)
            den = (jnp.sum(p_c, axis=-1, keepdims=True) + jnp.sum(p_n, axis=-1, keepdims=True)
                   + jnp.exp(sink - mx))
            o = _bdot(p_c / den, vc) + _bdot(p_n / den, vnew)
            for g in range(Q_PER_KV):
                h = kv * Q_PER_KV + g
                piece = o[g * DEC_TP:(g + 1) * DEC_TP]
                if h % 2 != kv:
                    piece = pltpu.roll(piece, ATT_HD, 1)
                piece = piece * masks[h % 2]
                tiles[h // 2] = piece if tiles[h // 2] is None else tiles[h // 2] + piece
        for jq in range(QW // LANES):
            o_ref[b, :, jq * LANES:(jq + 1) * LANES] = tiles[jq]
        for cache, new, out in ((kc, knew, ko_ref), (vc, vnew, vo_ref)):
            shifted = pltpu.roll(cache, WINDOW - t_real, 0)
            new_r = pltpu.roll(new, DEC_TP - t_real, 0)
            out[b, 0:WINDOW - DEC_TP] = shifted[0:WINDOW - DEC_TP]
            out[b, WINDOW - DEC_TP:WINDOW] = jnp.where(row8 >= DEC_TP - t_real, new_r,
                                                       shifted[WINDOW - DEC_TP:WINDOW])


def swa_decode(q, kv, cache_k, cache_v, bsz, t, q_norm, k_norm, rel_table, sinks, nbt):
    pad = ((0, 0), (0, DEC_TP - t), (0, 0))
    q3 = jnp.pad(q.reshape(bsz, t, QW), pad)
    kv3 = jnp.pad(kv.reshape(bsz, t, 2 * KW), pad)
    kpos = np.concatenate([np.arange(WINDOW) - WINDOW, np.arange(DEC_TP)])
    dist = np.arange(DEC_TP)[:, None] - kpos[None, :]
    bias = rel_bias(rel_table, dist)
    rows = Q_PER_KV * DEC_TP
    bias = bias.reshape(N_KV_HEADS, rows, WINDOW + DEC_TP)
    bias_c = bias[:, :, :WINDOW]
    bias_n = jnp.pad(bias[:, :, WINDOW:], ((0, 0), (0, 0), (0, LANES - DEC_TP)))
    sink_rows = jnp.broadcast_to(sinks.reshape(N_KV_HEADS, Q_PER_KV, 1, 1),
                                 (N_KV_HEADS, Q_PER_KV, DEC_TP, LANES)).reshape(N_KV_HEADS, rows, LANES)
    full3 = lambda shape: pl.BlockSpec(shape, lambda i: (0, 0, 0))
    o, ko, vo = pl.pallas_call(
        functools.partial(_swa_decode_kernel, nbt, t),
        grid=(bsz // nbt,),
        in_specs=[pl.BlockSpec((nbt, DEC_TP, QW), lambda i: (i, 0, 0)),
                  pl.BlockSpec((nbt, DEC_TP, 2 * KW), lambda i: (i, 0, 0)),
                  pl.BlockSpec((nbt, WINDOW, KW), lambda i: (i, 0, 0)),
                  pl.BlockSpec((nbt, WINDOW, KW), lambda i: (i, 0, 0)),
                  pl.BlockSpec((1, QW), lambda i: (0, 0)),
                  pl.BlockSpec((1, KW), lambda i: (0, 0)),
                  full3((N_KV_HEADS, rows, WINDOW)),
                  full3((N_KV_HEADS, rows, LANES)),
                  full3((N_KV_HEADS, rows, LANES))],
        out_specs=[pl.BlockSpec((nbt, DEC_TP, QW), lambda i: (i, 0, 0)),
                   pl.BlockSpec((nbt, WINDOW, KW), lambda i: (i, 0, 0)),
                   pl.BlockSpec((nbt, WINDOW, KW), lambda i: (i, 0, 0))],
        out_shape=[jax.ShapeDtypeStruct((bsz, DEC_TP, QW), F32),
                   jax.ShapeDtypeStruct((bsz, WINDOW, KW), F32),
                   jax.ShapeDtypeStruct((bsz, WINDOW, KW), F32)],
        compiler_params=_cparams(("parallel",)), name="swa_decode",
    )(q3, kv3, cache_k, cache_v, jnp.tile(q_norm, N_Q_HEADS).reshape(1, QW),
      jnp.tile(k_norm, N_KV_HEADS).reshape(1, KW), bias_c, bias_n, sink_rows)
    return o[:, :t].reshape(bsz * t, QW), ko, vo


S5_G = 32
S5_W = S5_G * S5_P
S5_CP = S5_CHUNK * S5_P
S5_PK = 2 * S5_N
S5_HW = S5_G * S5_PK


def _s5_prep_kernel(t_eff, a_ref, ldt_ref, b1_ref, b2_ref, c1_ref, c2_ref, kmat_ref, kb_ref, kct_ref, al_ref):
    L = S5_CHUNK
    ar2 = a_ref[0, 0:1, :]
    ai2 = a_ref[0, 1:2, :]
    step = jnp.exp(ldt_ref[0])
    mi = _iota((3 * SUBLANES, S5_PK), 0).astype(F32)
    mag = jnp.exp(mi * (step * ar2))
    ang = mi * (step * ai2)
    pwa = mag * jnp.cos(ang)
    pwb = mag * jnp.sin(ang)
    abr = pwa[1:2]
    abi = pwb[1:2]
    den = ar2 * ar2 + ai2 * ai2
    fa = ((abr - 1.0) * ar2 + abi * ai2) / den
    fb = (abi * ar2 - (abr - 1.0) * ai2) / den
    b1 = b1_ref[0]
    b2 = b2_ref[0]
    bp1 = b1 * fa + b2 * fb
    bp2 = b2 * fa - b1 * fb
    c1 = c1_ref[0]
    c2 = c2_ref[0]
    cpow = [c1 * pwa[m:m + 1] + c2 * pwb[m:m + 1] for m in range(L + 1)]
    kern_t = _hdot(bp1, jnp.concatenate(cpow[0:L], axis=0), _NT)
    lane = _iota((S5_P, S5_CP), 1)
    blocks = [kern_t]
    for i in range(1, L):
        blocks.append(jnp.where(lane >= S5_P * i, pltpu.roll(kern_t, S5_P * i, 1), 0.0))
    kmat_ref[0] = jnp.concatenate(blocks, axis=0).astype(BF16)
    kb = []
    for i in range(L):
        e = max(t_eff - 1 - i, 0)
        kb.append(bp1 * pwa[e:e + 1] + bp2 * pwb[e:e + 1])
    kb_ref[0] = jnp.concatenate(kb, axis=0).astype(BF16)
    kct_ref[0] = jnp.concatenate(cpow[1:L + 1], axis=0).astype(BF16)
    sgn = jnp.where(_iota((1, S5_PK), 1) < S5_N, -1.0, 1.0)
    al_ref[0] = jnp.concatenate([pwa[t_eff:t_eff + 1], sgn * pwb[t_eff:t_eff + 1]], axis=0)


def s5_prep(w, i, t_eff):
    dup = lambda z: jnp.concatenate([z, z], axis=-1)
    a = jnp.stack([dup(w["o_a_re"][i]), dup(w["o_a_im"][i])], axis=1)
    ldt = jnp.broadcast_to(w["o_log_dt"][i][:, None, None], (S5_G, 1, S5_PK))
    bt_re = jnp.swapaxes(w["o_b_re"][i], 1, 2)
    bt_im = jnp.swapaxes(w["o_b_im"][i], 1, 2)
    b1 = jnp.concatenate([bt_re, bt_im], -1)
    b2 = jnp.concatenate([-bt_im, bt_re], -1)
    c_re, c_im = w["o_c_re"][i], w["o_c_im"][i]
    c1 = jnp.concatenate([c_re, -c_im], -1)
    c2 = jnp.concatenate([-c_im, -c_re], -1)
    g3 = lambda r, c: pl.BlockSpec((1, r, c), lambda g: (g, 0, 0))
    return pl.pallas_call(
        functools.partial(_s5_prep_kernel, t_eff),
        grid=(S5_G,),
        in_specs=[g3(2, S5_PK), g3(1, S5_PK), g3(S5_P, S5_PK), g3(S5_P, S5_PK), g3(S5_P, S5_PK), g3(S5_P, S5_PK)],
        out_specs=[g3(S5_CP, S5_CP), g3(S5_CP, S5_PK), g3(S5_CP, S5_PK), g3(2, S5_PK)],
        out_shape=[jax.ShapeDtypeStruct((S5_G, S5_CP, S5_CP), BF16),
                   jax.ShapeDtypeStruct((S5_G, S5_CP, S5_PK), BF16),
                   jax.ShapeDtypeStruct((S5_G, S5_CP, S5_PK), BF16),
                   jax.ShapeDtypeStruct((S5_G, 2, S5_PK), F32)],
        compiler_params=_cparams(("parallel",)), name="s5_prep",
    )(a, ldt, b1, b2, c1, c2)


def _s5_e_kernel(u_ref, kb_ref, e_ref):
    e_ref[...] = _bdot(u_ref[0], kb_ref[0])


def _s5_swap(h):
    n = h.shape[-1]
    lane = _iota(h.shape, 1)
    return jnp.where(lane % S5_PK < S5_N, pltpu.roll(h, n - S5_N, 1), pltpu.roll(h, S5_N, 1))


def _s5_scan_kernel(cg, e_ref, h0_ref, ala_ref, alb_ref, hp_ref, hf_ref, h_scr):
    @pl.when(pl.program_id(0) == 0)
    def _():
        h_scr[...] = h0_ref[...]

    ala = ala_ref[...]
    alb = alb_ref[...]

    def body(c, h):
        hp_ref[c] = h
        return ala * h + alb * _s5_swap(h) + e_ref[c]

    h = lax.fori_loop(0, cg, body, h_scr[...])
    h_scr[...] = h
    hf_ref[...] = h


def _s5_y_kernel(u_ref, hp_ref, kmat_ref, kct_ref, y_ref):
    y_ref[0] = _bdot(u_ref[0], kmat_ref[0]) + _bdot(hp_ref[...], kct_ref[0], _NT)


def s5_core(u, h0, bsz, t, prep):
    kmat, kb, kct, al = prep
    L = S5_CHUNK
    tp = -(-t // L) * L
    u3 = u.reshape(bsz, t, S5_W)
    if tp != t:
        u3 = jnp.pad(u3, ((0, 0), (0, tp - t), (0, 0)))
    nc = tp // L
    rows = nc * bsz
    ug = u3.reshape(bsz, nc, L, S5_G, S5_P).transpose(3, 1, 0, 2, 4).reshape(S5_G, rows, S5_CP)
    e = pl.pallas_call(
        _s5_e_kernel,
        grid=(S5_G,),
        in_specs=[pl.BlockSpec((1, rows, S5_CP), lambda g: (g, 0, 0)),
                  pl.BlockSpec((1, S5_CP, S5_PK), lambda g: (g, 0, 0))],
        out_specs=pl.BlockSpec((rows, S5_PK), lambda g: (0, g)),
        out_shape=jax.ShapeDtypeStruct((rows, S5_HW), F32),
        compiler_params=_cparams(("parallel",)), name="s5_e",
    )(ug, kb)
    cg = math.gcd(nc, 64)
    ala = al[:, 0, :].reshape(1, S5_HW)
    alb = al[:, 1, :].reshape(1, S5_HW)
    hp, hf = pl.pallas_call(
        functools.partial(_s5_scan_kernel, cg),
        grid=(nc // cg,),
        in_specs=[pl.BlockSpec((cg, bsz, S5_HW), lambda i: (i, 0, 0)),
                  pl.BlockSpec((bsz, S5_HW), lambda i: (0, 0)),
                  pl.BlockSpec((1, S5_HW), lambda i: (0, 0)),
                  pl.BlockSpec((1, S5_HW), lambda i: (0, 0))],
        out_specs=[pl.BlockSpec((cg, bsz, S5_HW), lambda i: (i, 0, 0)),
                   pl.BlockSpec((bsz, S5_HW), lambda i: (0, 0))],
        out_shape=[jax.ShapeDtypeStruct((nc, bsz, S5_HW), F32),
                   jax.ShapeDtypeStruct((bsz, S5_HW), F32)],
        scratch_shapes=[pltpu.VMEM((bsz, S5_HW), F32)],
        compiler_params=_cparams(("arbitrary",)), name="s5_scan",
    )(e.reshape(nc, bsz, S5_HW), h0, ala, alb)
    yg = pl.pallas_call(
        _s5_y_kernel,
        grid=(S5_G,),
        in_specs=[pl.BlockSpec((1, rows, S5_CP), lambda g: (g, 0, 0)),
                  pl.BlockSpec((rows, S5_PK), lambda g: (0, g)),
                  pl.BlockSpec((1, S5_CP, S5_CP), lambda g: (g, 0, 0)),
                  pl.BlockSpec((1, S5_CP, S5_PK), lambda g: (g, 0, 0))],
        out_specs=pl.BlockSpec((1, rows, S5_CP), lambda g: (g, 0, 0)),
        out_shape=jax.ShapeDtypeStruct((S5_G, rows, S5_CP), F32),
        compiler_params=_cparams(("parallel",)), name="s5_y",
    )(ug, hp.reshape(rows, S5_HW), kmat, kct)
    y = yg.reshape(S5_G, nc, bsz, L, S5_P).transpose(2, 1, 3, 0, 4).reshape(bsz, tp, S5_W)
    return y[:, :t].reshape(bsz * t, S5_W), hf


def _s5_post_kernel(y_ref, u_ref, d_ref, gw_ref, gb_ref, o_ref):
    x = y_ref[...] + d_ref[...] * u_ref[...]
    z = 0.5 * x * (1.0 + jnp.tanh(math.sqrt(2.0 / math.pi) * (x + 0.044715 * (x * x * x))))
    o_ref[...] = z * _sigmoid(_dg(z.astype(BF16), gw_ref[...], _NN) + gb_ref[...])


def s5_post(y, u, d, glu_w_bf16, glu_b, tm):
    m = y.shape[0]
    spec = pl.BlockSpec((tm, S5_W), lambda i: (i, 0))
    row = pl.BlockSpec((1, S5_W), lambda i: (0, 0))
    return pl.pallas_call(
        _s5_post_kernel,
        grid=(m // tm,),
        in_specs=[spec, spec, row, pl.BlockSpec((S5_W, S5_W), lambda i: (0, 0)), row],
        out_specs=spec,
        out_shape=jax.ShapeDtypeStruct((m, S5_W), F32),
        compiler_params=_cparams(("parallel",)), name="s5_post",
    )(y, u, d.reshape(1, S5_W), glu_w_bf16, glu_b.reshape(1, S5_W))


def s5_mix(u, h_re, h_im, bsz, t, prep, d, glu_w_bf16, glu_b, tm):
    h0 = jnp.concatenate([h_re, h_im], axis=-1).reshape(bsz, S5_HW)
    y, hf = s5_core(u, h0, bsz, t, prep)
    out = s5_post(y, u, d, glu_w_bf16, glu_b, tm)
    hf = hf.reshape(bsz, S5_G, 2, S5_N)
    return out, hf[:, :, 0], hf[:, :, 1]


GLA_DK = 64
GLA_DV = 128
GLA_HEADS = 4
GLA_KW = GLA_HEADS * GLA_DK
GLA_VW = GLA_HEADS * GLA_DV
GLA_PW = 2 * GLA_KW + 2 * GLA_VW + LANES


def _gla_kernel(L, t_real, nb, p_ref, aup_ref, ab_ref, nw_ref, s0_ref, y_ref, sf_ref, s_scr):
    c = pl.program_id(1)
    nc = pl.num_programs(1)
    n_pairs = GLA_KW // LANES

    @pl.when(c == 0)
    def _():
        for bb in range(nb):
            for j in range(n_pairs):
                s_scr[bb * n_pairs + j] = s0_ref[bb, j]

    lane = _iota((1, LANES), 1)
    masks = (jnp.where(lane < GLA_DK, 1.0, 0.0), jnp.where(lane < GLA_DK, 0.0, 1.0))
    incl = _iota((L, L), 1) <= _iota((L, L), 0)
    diag128 = _iota((LANES, LANES), 0) == _iota((LANES, LANES), 1)
    ones128 = jnp.ones((LANES, LANES), BF16)
    tri = _tri_incl(L)
    rows = range(nb)
    ps = [p_ref[bb] for bb in rows]
    zs = [_bdot(p[:, 2 * GLA_KW + 2 * GLA_VW:], aup_ref[...]) + ab_ref[...] for p in ps]
    gk = [-_softplus(-z) * (1.0 / GLA_GATE_NORM) for z in zs]
    if t_real % L != 0:
        tok = c * L + _iota((L, 1), 0)
        gk = [jnp.where(tok < t_real, x, 0.0) for x in gk]
    b = [_xdot_r(tri, x) for x in gk]
    bl = [x[L - 1:L, :] for x in b]
    qd = [ps[i][:, 0:GLA_KW] * (GLA_DK ** -0.5) * jnp.exp(b[i]) for i in rows]
    kh = [ps[i][:, GLA_KW:2 * GLA_KW] * jnp.exp(-b[i]) for i in rows]
    kt = [ps[i][:, GLA_KW:2 * GLA_KW] * jnp.exp(bl[i] - b[i]) for i in rows]
    heads = [(bb, h) for bb in rows for h in range(GLA_HEADS)]
    sl = lambda h: slice((h // 2) * LANES, (h // 2 + 1) * LANES)
    hs = lambda h: slice(2 * GLA_KW + h * GLA_DV, 2 * GLA_KW + (h + 1) * GLA_DV)
    gs = lambda h: slice(2 * GLA_KW + GLA_VW + h * GLA_DV, 2 * GLA_KW + GLA_VW + (h + 1) * GLA_DV)
    st = [s_scr[bb * n_pairs + j] for bb in rows for j in range(n_pairs)]
    qm = [qd[bb][:, sl(h)] * masks[h % 2] for bb, h in heads]
    vh = [ps[bb][:, hs(h)] for bb, h in heads]
    attn = [jnp.where(incl, _bdot(qm[i], kh[bb][:, sl(h)], _NT), 0.0) for i, (bb, h) in enumerate(heads)]
    o = [_bdot(attn[i], vh[i]) + _bdot(qm[i], st[bb * n_pairs + h // 2]) for i, (bb, h) in enumerate(heads)]
    kv = [_bdot(kt[bb][:, sl(h)], vh[i], _TN) for i, (bb, h) in enumerate(heads)]
    for i, (bb, h) in enumerate(heads):
        of = o[i] * lax.rsqrt(jnp.mean(o[i] * o[i], axis=-1, keepdims=True) + EPS) * nw_ref[...]
        y_ref[bb, :, h * GLA_DV:(h + 1) * GLA_DV] = of * _silu(ps[bb][:, gs(h)])
    for bb in rows:
        for j in range(n_pairs):
            i0 = bb * GLA_HEADS + 2 * j
            pcol = _xdot_l(jnp.where(diag128, jnp.exp(bl[bb][:, j * LANES:(j + 1) * LANES]), 0.0), ones128)
            s_scr[bb * n_pairs + j] = pcol * st[bb * n_pairs + j] + jnp.concatenate(
                [kv[i0][0:GLA_DK], kv[i0 + 1][GLA_DK:2 * GLA_DK]], axis=0)

    @pl.when(c == nc - 1)
    def _():
        for bb in range(nb):
            for j in range(n_pairs):
                sf_ref[bb, j] = s_scr[bb * n_pairs + j]


def gla_mix(p_gla, s0, bsz, t, L, aup_pad, a_b, norm_w, nb=1):
    tp = -(-t // L) * L
    p3 = p_gla.reshape(bsz, t, GLA_PW)
    if tp != t:
        p3 = jnp.pad(p3, ((0, 0), (0, tp - t), (0, 0)))
    n_pairs = GLA_KW // LANES
    st_spec = pl.BlockSpec((nb, n_pairs, LANES, LANES), lambda b, c: (b, 0, 0, 0))
    y, s_fin = pl.pallas_call(
        functools.partial(_gla_kernel, L, t, nb),
        grid=(bsz // nb, tp // L),
        in_specs=[pl.BlockSpec((nb, L, GLA_PW), lambda b, c: (b, c, 0)),
                  pl.BlockSpec((LANES, GLA_KW), lambda b, c: (0, 0)),
                  pl.BlockSpec((1, GLA_KW), lambda b, c: (0, 0)),
                  pl.BlockSpec((1, GLA_DV), lambda b, c: (0, 0)),
                  st_spec],
        out_specs=[pl.BlockSpec((nb, L, GLA_VW), lambda b, c: (b, c, 0)), st_spec],
        out_shape=[jax.ShapeDtypeStruct((bsz, tp, GLA_VW), F32),
                   jax.ShapeDtypeStruct((bsz, n_pairs, LANES, LANES), F32)],
        scratch_shapes=[pltpu.VMEM((nb * n_pairs, LANES, LANES), F32)],
        compiler_params=_cparams(("parallel", "arbitrary")), name="gla",
    )(p3, aup_pad, a_b.reshape(1, GLA_KW), norm_w.reshape(1, GLA_DV),
      s0.reshape(bsz, n_pairs, LANES, LANES))
    return y[:, :t].reshape(bsz * t, GLA_VW), s_fin.reshape(bsz, GLA_HEADS, GLA_DK, GLA_DV)


GLA_LR = 16
D_FF_CHUNK = 1408
D_FF_EXPERT_CHUNK = 896
ROUTER_TOKENS = 1024
MOE_TOKENS = 2048


def _prepare_weights(w):
    win = w["e_w_in"][0]
    w_gla = jnp.pad(win[:, RSHIFT:], ((0, 0), (0, LANES - GLA_LR)))
    wo = w["o_w_in"][0]
    return {
        "e_w_rwkv": win[:, :RSHIFT].astype(BF16),
        "e_w_gla": w_gla.astype(BF16),
        "rwkv": rwkv_params(w, 0),
        "gla_aup": jnp.pad(w["e_gla_a_up"][0], ((0, LANES - GLA_LR), (0, 0))).astype(BF16),
        "e_wo_a": w["e_w_out"][0][:RW].astype(BF16),
        "e_wo_b": w["e_w_out"][0][RW:].astype(BF16),
        "ff_w1": w["e_ff_w1"][0].astype(BF16),
        "ff_w3": w["e_ff_w3"][0].astype(BF16),
        "ff_w2": w["e_ff_w2"][0].astype(BF16),
        "o_w_q": wo[:, :QW].astype(BF16),
        "o_w_kv": wo[:, QW:QW + 2 * KW].astype(BF16),
        "o_w_u": wo[:, QW + 2 * KW:].astype(BF16),
        "glu_w": w["o_glu_w"][0].astype(BF16),
        "o_wo_a": w["o_w_out"][0][:QW].astype(BF16),
        "o_wo_b": w["o_w_out"][0][QW:].astype(BF16),
        "moe_w1": w["o_moe_w1"][0].astype(BF16),
        "moe_w3": w["o_moe_w3"][0].astype(BF16),
        "moe_w2": w["o_moe_w2"][0].astype(BF16),
    }


def _trunk(x3, st, w, pw, tm, chunk, nb, s5_prep_t, prompt_bias):
    bsz, t, d = x3.shape
    m = bsz * t
    x = x3.reshape(m, d)
    p_r, p_g = norm_proj(x, w["e_norm1"][0], [pw["e_w_rwkv"], pw["e_w_gla"]], tm)
    ya, s_rwkv = rwkv_mix(p_r, st["shift"], st["rwkv"], bsz, t, chunk, tm, pw["rwkv"],
                          w["e_lnx_w"][0], w["e_lnx_b"][0], nb)
    s_shift = p_r.reshape(bsz, t, RSHIFT)[:, -1]
    yb, s_gla = gla_mix(p_g, st["gla"], bsz, t, chunk, pw["gla_aup"], w["e_gla_a_b"][0], w["e_gla_norm"][0], nb)
    x = out_proj(x, ya, yb, pw["e_wo_a"], pw["e_wo_b"], tm)
    x = ffn(x, w["e_norm2"][0], pw["ff_w1"], pw["ff_w3"], pw["ff_w2"], tm, D_FF_CHUNK)
    q, kv, u = norm_proj(x, w["o_norm1"][0], [pw["o_w_q"], pw["o_w_kv"], pw["o_w_u"]], tm)
    if st["win_k"] is None:
        yc, nk, nv = swa_prompt(q, kv, bsz, t, w["o_q_norm"][0], w["o_k_norm"][0], prompt_bias, w["o_sinks"][0])
    else:
        yc, nk, nv = swa_decode(q, kv, st["win_k"].reshape(bsz, WINDOW, KW), st["win_v"].reshape(bsz, WINDOW, KW),
                                bsz, t, w["o_q_norm"][0], w["o_k_norm"][0], w["rel_table"], w["o_sinks"][0], 8)
    yd, s5r, s5i = s5_mix(u, st["s5_re"], st["s5_im"], bsz, t, s5_prep_t, w["o_d"][0], pw["glu_w"],
                          w["o_glu_b"][0], tm)
    x = out_proj(x, yc, yd, pw["o_wo_a"], pw["o_wo_b"], tm)
    t_router = min(m, ROUTER_TOKENS)
    hn, gates, pos, counts = router(x, w["o_norm2"][0], w["o_router_w"][0], w["o_router_b"][0], t_router)
    x = moe(x, hn, gates, pos, counts, pw["moe_w1"], pw["moe_w3"], pw["moe_w2"], t_router, min(m, MOE_TOKENS),
            D_FF_EXPERT_CHUNK)
    kv_shape = (bsz, WINDOW, N_KV_HEADS, ATT_HD)
    return (x.reshape(bsz, t, d), s_rwkv[None], s_shift[None], s_gla[None], nk.reshape(kv_shape)[None],
            nv.reshape(kv_shape)[None], s5r[None], s5i[None])


def kernel(x_prompt, x_sample, state_rwkv, state_shift, state_gla, cache_win_k, cache_win_v, state_s5_re,
           state_s5_im, rel_table, e_norm1, e_w_in, e_mu, e_w0, e_w_up, e_a0, e_a_up, e_g_up, e_k_k, e_k_a, e_r_k,
           e_lnx_w, e_lnx_b, e_gla_a_up, e_gla_a_b, e_gla_norm, e_w_out, e_norm2, e_ff_w1, e_ff_w3, e_ff_w2,
           o_norm1, o_w_in, o_q_norm, o_k_norm, o_sinks, o_a_re, o_a_im, o_log_dt, o_b_re, o_b_im, o_c_re, o_c_im,
           o_d, o_glu_w, o_glu_b, o_w_out, o_norm2, o_router_w, o_router_b, o_moe_w1, o_moe_w3, o_moe_w2):
    w = dict(rel_table=rel_table, e_norm1=e_norm1, e_w_in=e_w_in, e_mu=e_mu, e_w0=e_w0, e_w_up=e_w_up, e_a0=e_a0,
             e_a_up=e_a_up, e_g_up=e_g_up, e_k_k=e_k_k, e_k_a=e_k_a, e_r_k=e_r_k, e_lnx_w=e_lnx_w, e_lnx_b=e_lnx_b,
             e_gla_a_up=e_gla_a_up, e_gla_a_b=e_gla_a_b, e_gla_norm=e_gla_norm, e_w_out=e_w_out, e_norm2=e_norm2,
             e_ff_w1=e_ff_w1, e_ff_w3=e_ff_w3, e_ff_w2=e_ff_w2, o_norm1=o_norm1, o_w_in=o_w_in, o_q_norm=o_q_norm,
             o_k_norm=o_k_norm, o_sinks=o_sinks, o_a_re=o_a_re, o_a_im=o_a_im, o_log_dt=o_log_dt, o_b_re=o_b_re,
             o_b_im=o_b_im, o_c_re=o_c_re, o_c_im=o_c_im, o_d=o_d, o_glu_w=o_glu_w, o_glu_b=o_glu_b,
             o_w_out=o_w_out, o_norm2=o_norm2, o_router_w=o_router_w, o_router_b=o_router_b, o_moe_w1=o_moe_w1,
             o_moe_w3=o_moe_w3, o_moe_w2=o_moe_w2)
    pw = _prepare_weights(w)
    bp, tp, _ = x_prompt.shape
    bs, ts, _ = x_sample.shape
    qi = np.arange(WINDOW)[:, None]
    kj = np.arange(2 * WINDOW)[None, :]
    prompt_bias = rel_bias(rel_table, qi + WINDOW - kj)
    zeros = lambda *shape: jnp.zeros(shape, F32)
    st_p = {"rwkv": zeros(bp, RW // RWKV_HD, RWKV_HD, RWKV_HD), "shift": zeros(bp, RSHIFT),
            "gla": zeros(bp, GLA_HEADS, GLA_DK, GLA_DV), "win_k": None, "win_v": None,
            "s5_re": zeros(bp, S5_G, S5_N), "s5_im": zeros(bp, S5_G, S5_N)}
    st_s = {"rwkv": state_rwkv[0], "shift": state_shift[0], "gla": state_gla[0], "win_k": cache_win_k[0],
            "win_v": cache_win_v[0], "s5_re": state_s5_re[0], "s5_im": state_s5_im[0]}
    out_p = _trunk(x_prompt, st_p, w, pw, 512, 64, math.gcd(bp, 2), s5_prep(w, 0, S5_CHUNK), prompt_bias)
    out_s = _trunk(x_sample, st_s, w, pw, bs * ts, 8, math.gcd(bs, 8), s5_prep(w, 0, ts), None)
    res = [out_p[0], out_s[0]]
    for a, b in zip(out_p[1:], out_s[1:]):
        res += [a, b]
    return tuple(res)
```

```python
import functools
import math

import jax
import jax.numpy as jnp
import numpy as np
from jax import lax
from jax.experimental import pallas as pl
from jax.experimental.pallas import tpu as pltpu

F32 = jnp.float32
BF16 = jnp.bfloat16

LANES = 128
SUBLANES = 8
VMEM_LIMIT_BYTES = 56 * 1024 * 1024

EPS = 1e-6
RWKV_HD = 64
RWKV_GN_EPS = 64e-5
GLA_GATE_NORM = 16.0
ATT_HD = 64
WINDOW = 128
N_BUCKETS = 32
BUCKET_MAX_DIST = 128
S5_P = 16
S5_N = 64
S5_CHUNK = 16
TOP_K = 2


def _cparams(sem):
    return pltpu.CompilerParams(dimension_semantics=sem, vmem_limit_bytes=VMEM_LIMIT_BYTES)


_NN = (((1,), (0,)), ((), ()))
_NT = (((1,), (1,)), ((), ()))
_TN = (((0,), (0,)), ((), ()))


def _dg(a, b, dims):
    return lax.dot_general(a, b, dims, preferred_element_type=F32)


def _bdot(a, b, dims=_NN):
    return _dg(a.astype(BF16), b.astype(BF16), dims)


def _split(a, n):
    terms = []
    r = a
    for _ in range(n):
        t = r.astype(BF16)
        terms.append(t)
        r = r - t.astype(F32)
    return terms


def _hdot(a, b, dims=_NN):
    a0, a1 = _split(a, 2)
    b0, b1 = _split(b, 2)
    return _dg(a0, b0, dims) + (_dg(a0, b1, dims) + _dg(a1, b0, dims))


def _xdot_l(a, e, dims=_NN):
    e = e.astype(BF16)
    a0, a1, a2 = _split(a, 3)
    return _dg(a0, e, dims) + (_dg(a1, e, dims) + _dg(a2, e, dims))


def _xdot_r(e, b, dims=_NN):
    e = e.astype(BF16)
    b0, b1, b2 = _split(b, 3)
    return _dg(e, b0, dims) + (_dg(e, b1, dims) + _dg(e, b2, dims))


def _iota(shape, axis):
    return lax.broadcasted_iota(jnp.int32, shape, axis)


def _seg_ones(n, seg):
    r = _iota((n, n), 0) // seg
    c = _iota((n, n), 1) // seg
    return jnp.where(r == c, 1.0, 0.0).astype(BF16)


def _seg_sum(x, seg):
    n = x.shape[-1]
    return _xdot_l(x, _seg_ones(n, seg))


def _sigmoid(x):
    return 1.0 / (1.0 + jnp.exp(-x))


def _silu(x):
    return x * _sigmoid(x)


def _softplus(x):
    return jnp.maximum(x, 0.0) + jnp.log(1.0 + jnp.exp(-jnp.abs(x)))


def _tri_incl(n):
    r = _iota((n, n), 0)
    c = _iota((n, n), 1)
    return jnp.where(c <= r, 1.0, 0.0).astype(BF16)


def _rms(x, g):
    return x * lax.rsqrt(jnp.mean(x * x, axis=-1, keepdims=True) + EPS) * g


def _norm_proj_kernel(n_w, x_ref, g_ref, *refs):
    xn = _rms(x_ref[...], g_ref[...]).astype(BF16)
    for w_ref, o_ref in zip(refs[:n_w], refs[n_w:]):
        o_ref[...] = _dg(xn, w_ref[...], _NN)


def norm_proj(x, g, ws_bf16, tm):
    m, d = x.shape
    return pl.pallas_call(
        functools.partial(_norm_proj_kernel, len(ws_bf16)),
        grid=(m // tm,),
        in_specs=[pl.BlockSpec((tm, d), lambda i: (i, 0)),
                  pl.BlockSpec((1, d), lambda i: (0, 0))]
                 + [pl.BlockSpec(w.shape, lambda i: (0, 0)) for w in ws_bf16],
        out_specs=[pl.BlockSpec((tm, w.shape[1]), lambda i: (i, 0)) for w in ws_bf16],
        out_shape=[jax.ShapeDtypeStruct((m, w.shape[1]), F32) for w in ws_bf16],
        compiler_params=_cparams(("parallel",)), name="norm_proj",
    )(x, g.reshape(1, d), *ws_bf16)


def _norm_proj_tiles_kernel(n_w, x_ref, g_ref, *refs):
    xn = _rms(x_ref[...], g_ref[...]).astype(BF16)
    for w_ref, o_ref in zip(refs[:n_w - 1], refs[n_w:]):
        o_ref[...] = _dg(xn, w_ref[...], _NN)
    u = _dg(xn, refs[n_w - 1][...], _NN)
    u_ref = refs[-1]
    for q in range(u_ref.shape[0]):
        u_ref[q] = u[:, q * LANES:(q + 1) * LANES]


def norm_proj_tiles(x, g, ws_bf16, tm):
    m, d = x.shape
    nt = ws_bf16[-1].shape[1] // LANES
    return pl.pallas_call(
        functools.partial(_norm_proj_tiles_kernel, len(ws_bf16)),
        grid=(m // tm,),
        in_specs=[pl.BlockSpec((tm, d), lambda i: (i, 0)),
                  pl.BlockSpec((1, d), lambda i: (0, 0))]
                 + [pl.BlockSpec(w.shape, lambda i: (0, 0)) for w in ws_bf16],
        out_specs=[pl.BlockSpec((tm, w.shape[1]), lambda i: (i, 0)) for w in ws_bf16[:-1]]
                  + [pl.BlockSpec((nt, tm, LANES), lambda i: (0, i, 0))],
        out_shape=[jax.ShapeDtypeStruct((m, w.shape[1]), F32) for w in ws_bf16[:-1]]
                  + [jax.ShapeDtypeStruct((nt, m, LANES), F32)],
        compiler_params=_cparams(("parallel",)), name="norm_proj_tiles",
    )(x, g.reshape(1, d), *ws_bf16)


def _out_proj_kernel(x_ref, ya_ref, yb_ref, wa_ref, wb_ref, o_ref):
    o_ref[...] = (x_ref[...] + _dg(ya_ref[...].astype(BF16), wa_ref[...], _NN)
                  + _dg(yb_ref[...].astype(BF16), wb_ref[...], _NN))


def out_proj(x, ya, yb, wa, wb, tm):
    m, d = x.shape
    return pl.pallas_call(
        _out_proj_kernel,
        grid=(m // tm,),
        in_specs=[pl.BlockSpec((tm, d), lambda i: (i, 0)),
                  pl.BlockSpec((tm, ya.shape[1]), lambda i: (i, 0)),
                  pl.BlockSpec((tm, yb.shape[1]), lambda i: (i, 0)),
                  pl.BlockSpec(wa.shape, lambda i: (0, 0)),
                  pl.BlockSpec(wb.shape, lambda i: (0, 0))],
        out_specs=pl.BlockSpec((tm, d), lambda i: (i, 0)),
        out_shape=jax.ShapeDtypeStruct((m, d), F32),
        compiler_params=_cparams(("parallel",)), name="out_proj",
    )(x, ya, yb, wa, wb)


def _ffn_kernel(x_ref, g_ref, w1_ref, w3_ref, w2_ref, o_ref, xn_scr):
    j = pl.program_id(1)

    @pl.when(j == 0)
    def _():
        x = x_ref[...]
        xn_scr[...] = _rms(x, g_ref[...]).astype(BF16)
        o_ref[...] = x

    xn = xn_scr[...]
    h = _silu(_dg(xn, w1_ref[0], _NN)) * _dg(xn, w3_ref[0], _NN)
    o_ref[...] += _dg(h.astype(BF16), w2_ref[...], _NN)


def chunk_major(w, fc):
    *lead, d, dff = w.shape
    return jnp.moveaxis(w.reshape(*lead, d, dff // fc, fc), -2, -3)


def ffn(x, g, w1, w3, w2, tm):
    m, d = x.shape
    nj, _, fc = w1.shape
    return pl.pallas_call(
        _ffn_kernel,
        grid=(m // tm, nj),
        in_specs=[pl.BlockSpec((tm, d), lambda i, j: (i, 0)),
                  pl.BlockSpec((1, d), lambda i, j: (0, 0)),
                  pl.BlockSpec((1, d, fc), lambda i, j: (j, 0, 0)),
                  pl.BlockSpec((1, d, fc), lambda i, j: (j, 0, 0)),
                  pl.BlockSpec((fc, d), lambda i, j: (j, 0))],
        out_specs=pl.BlockSpec((tm, d), lambda i, j: (i, 0)),
        out_shape=jax.ShapeDtypeStruct((m, d), F32),
        scratch_shapes=[pltpu.VMEM((tm, d), BF16)],
        compiler_params=_cparams(("parallel", "arbitrary")), name="ffn",
    )(x, g.reshape(1, d), w1, w3, w2)


def _router_kernel(n_exp, x_ref, g_ref, rw_ref, rb_ref, hn_ref, gate_ref, pos_ref, cnt_ref):
    xn = _rms(x_ref[...], g_ref[...])
    hn_ref[...] = xn.astype(BF16)
    logits = _hdot(xn, rw_ref[...]) + rb_ref[...]
    lane = _iota(logits.shape, 1)
    logits = jnp.where(lane < n_exp, logits, -jnp.inf)
    m1 = jnp.max(logits, axis=-1, keepdims=True)
    i1 = jnp.min(jnp.where(logits == m1, lane, LANES), axis=-1, keepdims=True)
    rest = jnp.where(lane == i1, -jnp.inf, logits)
    m2 = jnp.max(rest, axis=-1, keepdims=True)
    i2 = jnp.min(jnp.where(rest == m2, lane, LANES), axis=-1, keepdims=True)
    e2 = jnp.exp(m2 - m1)
    g1 = 1.0 / (1.0 + e2)
    g2 = e2 / (1.0 + e2)
    pick1 = lane == i1
    pick2 = lane == i2
    gate_ref[...] = jnp.where(pick1, g1, 0.0) + jnp.where(pick2, g2, 0.0)
    tm = logits.shape[0]
    sel = jnp.where(jnp.logical_or(pick1, pick2), 1.0, 0.0).astype(BF16)
    tr = _iota((tm, tm), 0)
    tc = _iota((tm, tm), 1)
    upper = jnp.where(tr <= tc, 1.0, 0.0).astype(BF16)
    eye = jnp.where(tr == tc, 1.0, 0.0).astype(BF16)
    rank_t = _dg(sel, upper, _TN)
    sel_t = _dg(sel, eye, _TN)
    pos_t = jnp.where(sel_t > 0.5, rank_t - 1.0, -1.0)
    pos_ref[...] = pos_t[0:SUBLANES, :]
    cnt_ref[0] = jnp.sum(sel.astype(F32), axis=0, keepdims=True)


def router(x, g, rw, rb, tm):
    m, d = x.shape
    n_exp = rw.shape[1]
    assert n_exp <= SUBLANES
    rw_pad = jnp.pad(rw, ((0, 0), (0, LANES - n_exp)))
    rb_pad = jnp.pad(rb, (0, LANES - n_exp)).reshape(1, LANES)
    return pl.pallas_call(
        functools.partial(_router_kernel, n_exp),
        grid=(m // tm,),
        in_specs=[pl.BlockSpec((tm, d), lambda i: (i, 0)),
                  pl.BlockSpec((1, d), lambda i: (0, 0)),
                  pl.BlockSpec((d, LANES), lambda i: (0, 0)),
                  pl.BlockSpec((1, LANES), lambda i: (0, 0))],
        out_specs=[pl.BlockSpec((tm, d), lambda i: (i, 0)),
                   pl.BlockSpec((tm, LANES), lambda i: (i, 0)),
                   pl.BlockSpec((SUBLANES, tm), lambda i: (0, i)),
                   pl.BlockSpec((1, 1, LANES), lambda i: (i, 0, 0))],
        out_shape=[jax.ShapeDtypeStruct((m, d), BF16), jax.ShapeDtypeStruct((m, LANES), F32),
                   jax.ShapeDtypeStruct((SUBLANES, m), F32),
                   jax.ShapeDtypeStruct((m // tm, 1, LANES), F32)],
        compiler_params=_cparams(("parallel",)), name="router",
    )(x, g.reshape(1, d), rw_pad, rb_pad)


MOE_ROWS = 128


def _moe_kernel(n_exp, cnt_ref, x_hbm, hn_ref, gate_ref, pos_ref, w1_ref, w3_ref, w2_ref, o_ref, xs_scr, y_scr):
    i = pl.program_id(0)
    e = pl.program_id(1)
    j = pl.program_id(2)
    nj = pl.num_programs(2)
    tm = hn_ref.shape[0]
    n_small = (cnt_ref[i * n_exp + e] + (MOE_ROWS - 1)) // MOE_ROWS
    n_big = n_small // 2
    tail = n_small % 2

    @pl.when(jnp.logical_and(e == 0, j == 0))
    def _():
        pltpu.sync_copy(x_hbm.at[pl.ds(pl.multiple_of(i * tm, tm), tm), :], o_ref)

    def select(start, nrows):
        pos = pos_ref[pl.ds(e, 1), :]
        want = (start + _iota((nrows, tm), 0)).astype(F32)
        return jnp.where(pos == want, 1.0, 0.0).astype(BF16)

    def blocks(body):
        def big(blk, carry):
            body(pl.multiple_of(blk * 2 * MOE_ROWS, 2 * MOE_ROWS), 2 * MOE_ROWS)
            return carry
        lax.fori_loop(0, n_big, big, 0)

        @pl.when(tail == 1)
        def _():
            body(pl.multiple_of(n_big * 2 * MOE_ROWS, 2 * MOE_ROWS), MOE_ROWS)

    @pl.when(j == 0)
    def _():
        def gather(start, nrows):
            rows = pl.ds(start, nrows)
            xs_scr[rows, :] = _dg(select(start, nrows), hn_ref[...], _NN).astype(BF16)
            y_scr[rows, :] = jnp.zeros((nrows, y_scr.shape[1]), F32)
        blocks(gather)

    def expert(start, nrows):
        rows = pl.ds(start, nrows)
        xs = xs_scr[rows, :]
        h = _silu(_dg(xs, w1_ref[0, 0], _NN)) * _dg(xs, w3_ref[0, 0], _NN)
        y_scr[rows, :] += _dg(h.astype(BF16), w2_ref[0], _NN)
    blocks(expert)

    @pl.when(j == nj - 1)
    def _():
        gt = gate_ref[...]
        gcol = jnp.sum(jnp.where(_iota(gt.shape, 1) == e, gt, 0.0), axis=-1, keepdims=True)

        def scatter(start, nrows):
            o_ref[...] += gcol * _dg(select(start, nrows), y_scr[pl.ds(start, nrows), :].astype(BF16), _TN)
        blocks(scatter)


def moe(x, hn, gates, pos, counts, w1, w3, w2, tm_router, tm):
    m, d = x.shape
    n_exp, nj, _, fc = w1.shape
    ratio = tm // tm_router
    cnt = counts[:, 0, :n_exp].astype(jnp.int32).reshape(m // tm, ratio, n_exp)
    before = (jnp.cumsum(cnt, axis=1) - cnt).astype(F32)
    shift = jnp.repeat(before.reshape(m // tm_router, n_exp).T, tm_router, axis=1)
    shift = jnp.pad(shift, ((0, SUBLANES - n_exp), (0, 0)))
    pos = jnp.where(pos >= 0, pos + shift, pos)
    cnt = cnt.sum(axis=1).reshape(-1)
    grid_spec = pltpu.PrefetchScalarGridSpec(
        num_scalar_prefetch=1,
        grid=(m // tm, n_exp, nj),
        in_specs=[pl.BlockSpec(memory_space=pl.ANY),
                  pl.BlockSpec((tm, d), lambda i, e, j, c: (i, 0)),
                  pl.BlockSpec((tm, LANES), lambda i, e, j, c: (i, 0)),
                  pl.BlockSpec((SUBLANES, tm), lambda i, e, j, c: (0, i)),
                  pl.BlockSpec((1, 1, d, fc), lambda i, e, j, c: (e, j, 0, 0)),
                  pl.BlockSpec((1, 1, d, fc), lambda i, e, j, c: (e, j, 0, 0)),
                  pl.BlockSpec((1, fc, d), lambda i, e, j, c: (e, j, 0))],
        out_specs=pl.BlockSpec((tm, d), lambda i, e, j, c: (i, 0)),
        scratch_shapes=[pltpu.VMEM((tm, d), BF16), pltpu.VMEM((tm, d), F32)])
    return pl.pallas_call(
        functools.partial(_moe_kernel, n_exp),
        grid_spec=grid_spec,
        out_shape=jax.ShapeDtypeStruct((m, d), F32),
        compiler_params=_cparams(("parallel", "arbitrary", "arbitrary")), name="moe",
    )(cnt, x, hn, gates, pos, w1, w3, w2)


RW = 512
RSHIFT = 1792


def _rwkv_pre_kernel(t_per_batch, tm, p_ref, pb_ref, prev_ref, mu_ref, w0_ref, wup_ref, a0_ref, aup_ref,
                     gup_ref, kk_ref, ka_ref, rk_ref,
                     r_o, lw_o, k_o, v_o, al_o, be_o, g_o, bo_o):
    i = pl.program_id(0)
    p = p_ref[...]
    rolled = pltpu.roll(p, 1, 0)
    row = _iota((tm, 1), 0)
    rolled = jnp.where(row == 0, pb_ref[SUBLANES - 1:SUBLANES, :], rolled)
    if t_per_batch >= tm:
        first = (i * tm) % t_per_batch == 0
        is_start = jnp.logical_and(row == 0, first)
        prev = jnp.where(is_start, prev_ref[0], rolled)
    else:
        is_start = (row % t_per_batch) == 0
        prev = jnp.where(is_start, prev_ref[...], rolled)
    xs = p + (prev - p) * mu_ref[...]
    r = xs[:, 0:RW]
    k = xs[:, RW:2 * RW]
    v = xs[:, 2 * RW:3 * RW]
    lr = xs[:, 3 * RW:3 * RW + LANES]
    gd = xs[:, 3 * RW + LANES:3 * RW + 2 * LANES]
    w_pre = w0_ref[...] + _bdot(jnp.tanh(lr), wup_ref[...])
    logw = -jnp.exp(-_softplus(-w_pre) - 0.5)
    a = _sigmoid(a0_ref[...] + _bdot(lr, aup_ref[...]))
    g = _bdot(_sigmoid(gd), gup_ref[...])
    kkr = k * kk_ref[...]
    kk = kkr / jnp.maximum(jnp.sqrt(_seg_sum(kkr * kkr, RWKV_HD)), 1e-12)
    k2 = k * (1.0 + (a - 1.0) * ka_ref[...])
    bonus = _seg_sum(r * k2 * rk_ref[...], RWKV_HD) * v
    r_o[...] = r
    lw_o[...] = logw
    k_o[...] = k2
    v_o[...] = v
    al_o[...] = -kk
    be_o[...] = kk * a
    g_o[...] = g
    bo_o[...] = bonus


def rwkv_pre(p_full, prev_rows, t_per_batch, tm, prm):
    m = p_full.shape[0]
    nb8 = tm // SUBLANES
    row_spec = pl.BlockSpec((1, RW), lambda i: (0, 0))
    if t_per_batch >= tm:
        prev_spec = pl.BlockSpec((1, 1, RSHIFT), lambda i: ((i * tm) // t_per_batch, 0, 0))
    else:
        prev_spec = pl.BlockSpec((tm, RSHIFT), lambda i: (i, 0))
    out_sds = jax.ShapeDtypeStruct((m, RW), F32)
    out_spec = pl.BlockSpec((tm, RW), lambda i: (i, 0))
    return pl.pallas_call(
        functools.partial(_rwkv_pre_kernel, t_per_batch, tm),
        grid=(m // tm,),
        in_specs=[pl.BlockSpec((tm, RSHIFT), lambda i: (i, 0)),
                  pl.BlockSpec((SUBLANES, RSHIFT), lambda i: (jnp.maximum(i * nb8 - 1, 0), 0)),
                  prev_spec,
                  pl.BlockSpec((1, RSHIFT), lambda i: (0, 0)),
                  row_spec,
                  pl.BlockSpec((LANES, RW), lambda i: (0, 0)),
                  row_spec,
                  pl.BlockSpec((LANES, RW), lambda i: (0, 0)),
                  pl.BlockSpec((LANES, RW), lambda i: (0, 0)),
                  row_spec, row_spec, row_spec],
        out_specs=[out_spec] * 8,
        out_shape=[out_sds] * 8,
        compiler_params=_cparams(("parallel",)), name="rwkv_pre",
    )(p_full, p_full, prev_rows, prm["mu"], prm["w0"], prm["wup"], prm["a0"], prm["aup"], prm["gup"],
      prm["k_k"], prm["k_a"], prm["r_k"])


_RW_DOT_A = _bdot
_RW_DOT_T = _bdot
_RW_DOT_W = _bdot
_RW_DOT_S = _hdot


def _rwkv_scan_kernel(L, nb, n_pairs, r_ref, lw_ref, k_ref, v_ref, al_ref, be_ref, s0_ref,
                      y_ref, sf_ref, s_scr):
    c = pl.program_id(1)
    nc = pl.num_programs(1)
    L2 = 2 * L
    lane = _iota((1, LANES), 1)
    m0 = jnp.where(lane < RWKV_HD, 1.0, 0.0)
    m1 = 1.0 - m0
    rr = _iota((L2, L2), 0)
    cc = _iota((L2, L2), 1)
    same = (rr // L) == (cc // L)
    strict = jnp.logical_and(same, (cc % L) < (rr % L))
    incl = jnp.logical_and(same, (cc % L) <= (rr % L))
    eye2 = jnp.where(rr == cc, 1.0, 0.0)
    r128 = _iota((LANES, LANES), 0)
    c128 = _iota((LANES, LANES), 1)
    blk128 = (r128 // RWKV_HD) == (c128 // RWKV_HD)
    diag128 = r128 == c128
    fmat = jnp.where(_iota((LANES, RWKV_HD), 0) % RWKV_HD == _iota((LANES, RWKV_HD), 1), 1.0, 0.0)
    tri = _tri_incl(L)

    def bd(x):
        return jnp.concatenate([x * m0, x * m1], axis=0)

    @pl.when(c == 0)
    def _():
        for bb in range(nb):
            for j in range(n_pairs):
                s0 = s0_ref[bb, j]
                st = _xdot_r(fmat, s0, _NT)
                s_scr[bb * n_pairs + j] = jnp.where(blk128, st, 0.0)

    pairs = range(nb * n_pairs)
    bbs = [i // n_pairs for i in pairs]
    sls = [slice((i % n_pairs) * LANES, (i % n_pairs + 1) * LANES) for i in pairs]
    lw = [lw_ref[bbs[j], :, sls[j]] for j in pairs]
    b = [_xdot_r(tri, x) for x in lw]
    bl = [x[L - 1:L, :] for x in b]
    e_b = [jnp.exp(x) for x in b]
    e_nb = [jnp.exp(-x) for x in b]
    e_lb = [jnp.exp(bl[j] - b[j]) for j in pairs]
    at = [bd(al_ref[bbs[j], :, sls[j]] * jnp.exp(b[j] - lw[j])) for j in pairs]
    rt = [bd(r_ref[bbs[j], :, sls[j]] * e_b[j]) for j in pairs]
    bh = [bd(be_ref[bbs[j], :, sls[j]] * e_nb[j]) for j in pairs]
    kh = [bd(k_ref[bbs[j], :, sls[j]] * e_nb[j]) for j in pairs]
    bt = [bd(be_ref[bbs[j], :, sls[j]] * e_lb[j]) for j in pairs]
    kt = [bd(k_ref[bbs[j], :, sls[j]] * e_lb[j]) for j in pairs]
    vb = [bd(v_ref[bbs[j], :, sls[j]]) for j in pairs]
    a_ab = [jnp.where(strict, _RW_DOT_A(at[j], bh[j], _NT), 0.0) for j in pairs]
    a_ak = [jnp.where(strict, _RW_DOT_A(at[j], kh[j], _NT), 0.0) for j in pairs]
    a_rb = [jnp.where(incl, _RW_DOT_A(rt[j], bh[j], _NT), 0.0) for j in pairs]
    a_rk = [jnp.where(incl, _RW_DOT_A(rt[j], kh[j], _NT), 0.0) for j in pairs]
    x = a_ab
    tinv = [eye2 + a for a in a_ab]
    span = 2
    while span < L:
        x = [_RW_DOT_T(xx, xx) for xx in x]
        tinv = [tinv[j] + _RW_DOT_T(tinv[j], x[j]) for j in pairs]
        span *= 2
    akv = [_RW_DOT_W(a_ak[j], vb[j]) for j in pairs]
    w = [_RW_DOT_W(tinv[j], at[j]) for j in pairs]
    uv = [_RW_DOT_W(tinv[j], akv[j]) for j in pairs]
    mm = [jnp.where(diag128, jnp.exp(bl[j]), 0.0) + _RW_DOT_W(bt[j], w[j], _TN) for j in pairs]
    nn = [_RW_DOT_W(bt[j], uv[j], _TN) + _RW_DOT_W(kt[j], vb[j], _TN) for j in pairs]
    q = [rt[j] + _RW_DOT_W(a_rb[j], w[j]) for j in pairs]
    yv = [_RW_DOT_W(a_rb[j], uv[j]) + _RW_DOT_W(a_rk[j], vb[j]) for j in pairs]
    s = [s_scr[j] for j in pairs]
    for j in pairs:
        ybd = _RW_DOT_S(q[j], s[j]) + yv[j]
        y_ref[bbs[j], :, sls[j]] = ybd[0:L] + ybd[L:L2]
    for j in pairs:
        s_scr[j] = _RW_DOT_S(mm[j], s[j]) + nn[j]

    @pl.when(c == nc - 1)
    def _():
        for i in pairs:
            sf_ref[bbs[i], i % n_pairs] = _xdot_l(s_scr[i], fmat, _TN)


def rwkv_scan(ops, s0, L, nb):
    bsz, t, _ = ops[0].shape
    n_pairs = RW // LANES
    blk = pl.BlockSpec((nb, L, RW), lambda b, c: (b, c, 0))
    st_spec = pl.BlockSpec((nb, n_pairs, LANES, RWKV_HD), lambda b, c: (b, 0, 0, 0))
    return pl.pallas_call(
        functools.partial(_rwkv_scan_kernel, L, nb, n_pairs),
        grid=(bsz // nb, t // L),
        in_specs=[blk] * 6 + [st_spec],
        out_specs=[blk, st_spec],
        out_shape=[jax.ShapeDtypeStruct((bsz, t, RW), F32),
                   jax.ShapeDtypeStruct((bsz, n_pairs, LANES, RWKV_HD), F32)],
        scratch_shapes=[pltpu.VMEM((nb * n_pairs, LANES, LANES), F32)],
        compiler_params=_cparams(("parallel", "arbitrary")), name="rwkv_scan",
    )(*ops, s0)


def _rwkv_post_kernel(y_ref, bo_ref, g_ref, lw_ref, lb_ref, o_ref):
    y = y_ref[...]
    mean = _seg_sum(y, RWKV_HD) * (1.0 / RWKV_HD)
    d = y - mean
    var = _seg_sum(d * d, RWKV_HD) * (1.0 / RWKV_HD)
    yn = d * lax.rsqrt(var + RWKV_GN_EPS) * lw_ref[...] + lb_ref[...]
    o_ref[...] = (yn + bo_ref[...]) * g_ref[...]


def rwkv_post(y, bonus, g, lnx_w, lnx_b, tm):
    m = y.shape[0]
    spec = pl.BlockSpec((tm, RW), lambda i: (i, 0))
    row = pl.BlockSpec((1, RW), lambda i: (0, 0))
    return pl.pallas_call(
        _rwkv_post_kernel,
        grid=(m // tm,),
        in_specs=[spec, spec, spec, row, row],
        out_specs=spec,
        out_shape=jax.ShapeDtypeStruct((m, RW), F32),
        compiler_params=_cparams(("parallel",)), name="rwkv_post",
    )(y, bonus, g, lnx_w.reshape(1, RW), lnx_b.reshape(1, RW))


def rwkv_params(w, i):
    z64 = jnp.zeros((RWKV_HD, RW), F32)
    return {
        "mu": w["e_mu"][i].reshape(1, RSHIFT),
        "w0": w["e_w0"][i].reshape(1, RW),
        "wup": jnp.concatenate([w["e_w_up"][i], z64], 0).astype(BF16),
        "a0": w["e_a0"][i].reshape(1, RW),
        "aup": jnp.concatenate([z64, w["e_a_up"][i]], 0).astype(BF16),
        "gup": w["e_g_up"][i].astype(BF16),
        "k_k": w["e_k_k"][i].reshape(1, RW),
        "k_a": w["e_k_a"][i].reshape(1, RW),
        "r_k": w["e_r_k"][i].reshape(1, RW),
    }


def rwkv_mix(p_flat, prev, s0, bsz, t, L, tm, prm, lnx_w, lnx_b, nb=1):
    m = bsz * t
    if t >= tm:
        prev_rows = prev.reshape(bsz, 1, RSHIFT)
    else:
        prev_rows = jnp.repeat(prev, t, axis=0)
    r, lw, k, v, al, be, g, bonus = rwkv_pre(p_flat, prev_rows, t, tm, prm)
    tp = -(-t // L) * L
    ops = [z.reshape(bsz, t, RW) for z in (r, lw, k, v, al, be)]
    if tp != t:
        ops = [jnp.pad(z, ((0, 0), (0, tp - t), (0, 0))) for z in ops]
    y, s_fin = rwkv_scan(ops, s0.reshape(bsz, RW // LANES, LANES, RWKV_HD), L, nb)
    y = y[:, :t].reshape(m, RW)
    ya = rwkv_post(y, bonus, g, lnx_w, lnx_b, tm)
    return ya, s_fin.reshape(bsz, RW // RWKV_HD, RWKV_HD, RWKV_HD)


N_Q_HEADS = 8
N_KV_HEADS = 2
Q_PER_KV = N_Q_HEADS // N_KV_HEADS
QW = N_Q_HEADS * ATT_HD
KW = N_KV_HEADS * ATT_HD
NEG = -1e30


def _t5_bucket_np(dist):
    n = np.maximum(dist, 0)
    max_exact = N_BUCKETS // 2
    nf = np.maximum(n, 1).astype(np.float32)
    large = max_exact + (np.log(nf / np.float32(max_exact)) / np.float32(math.log(BUCKET_MAX_DIST / max_exact))
                         * np.float32(N_BUCKETS - max_exact)).astype(np.int32)
    large = np.minimum(large, N_BUCKETS - 1)
    return np.where(n < max_exact, n, large)


def _bias_kernel(rt_ref, oh_ref, o_ref):
    o_ref[...] = _xdot_l(rt_ref[...], oh_ref[...])


def rel_bias(rel_table, dist):
    bucket = _t5_bucket_np(dist).reshape(-1)
    n = bucket.shape[0]
    onehot = jnp.asarray((np.arange(N_BUCKETS)[:, None] == bucket[None, :]).astype(np.float32), BF16)
    out = pl.pallas_call(
        _bias_kernel,
        out_shape=jax.ShapeDtypeStruct((N_Q_HEADS, n), F32),
    )(rel_table.T, onehot)
    return out.reshape((N_Q_HEADS,) + dist.shape)


def _head_norm(x, w_row):
    return x * lax.rsqrt(_seg_sum(x * x, ATT_HD) * (1.0 / ATT_HD) + EPS) * w_row


def _swa_prompt_kernel(q_ref, kvc_ref, kvp_ref, qw_ref, kw_ref, bias_ref, sink_ref, o_ref, ko_ref, vo_ref):
    i = pl.program_id(1)
    qn = _head_norm(q_ref[0], qw_ref[...])
    kvc = kvc_ref[0]
    kvp = kvp_ref[0]
    kcn = _head_norm(kvc[:, 0:KW], kw_ref[...])
    kpn = _head_norm(kvp[:, 0:KW], kw_ref[...])
    vc = kvc[:, KW:2 * KW]
    kcat = jnp.concatenate([kpn, kcn], axis=0).astype(BF16)
    vcat = jnp.concatenate([kvp[:, KW:2 * KW], vc], axis=0).astype(BF16)
    qi = _iota((WINDOW, 2 * WINDOW), 0)
    kj = _iota((WINDOW, 2 * WINDOW), 1)
    dist = qi + WINDOW - kj
    valid = jnp.logical_and(jnp.logical_and(dist >= 0, dist < WINDOW), jnp.logical_or(kj >= WINDOW, i > 0))
    lane = _iota((1, LANES), 1)
    masks = (jnp.where(lane < ATT_HD, 1.0, 0.0), jnp.where(lane < ATT_HD, 0.0, 1.0))
    heads = range(N_Q_HEADS)
    kvs = [h // Q_PER_KV for h in heads]
    qts = [qn[:, (h // 2) * LANES:(h // 2 + 1) * LANES] for h in heads]
    qts = [pltpu.roll(qts[h], ATT_HD, 1) if h % 2 != kvs[h] else qts[h] for h in heads]
    qms = [(qts[h] * masks[kvs[h]]).astype(BF16) for h in heads]
    s_all = _dg(jnp.concatenate(qms, axis=0), kcat, _NT) * (ATT_HD ** -0.5)
    logits = [jnp.where(valid, s_all[h * WINDOW:(h + 1) * WINDOW] + bias_ref[h], NEG) for h in heads]
    sinks = [sink_ref[h:h + 1, 0:1] for h in heads]
    mx = [jnp.maximum(jnp.max(logits[h], axis=-1, keepdims=True), sinks[h]) for h in heads]
    pr = [jnp.exp(logits[h] - mx[h]) for h in heads]
    den = [jnp.sum(pr[h], axis=-1, keepdims=True) + jnp.exp(sinks[h] - mx[h]) for h in heads]
    probs = jnp.concatenate([(pr[h] * (1.0 / den[h])).astype(BF16) for h in heads], axis=0)
    o_all = _dg(probs, vcat, _NN)
    os_ = [o_all[h * WINDOW:(h + 1) * WINDOW] for h in heads]
    os_ = [pltpu.roll(os_[h], ATT_HD, 1) if h % 2 != kvs[h] else os_[h] for h in heads]
    for jq in range(QW // LANES):
        o_ref[0, :, jq * LANES:(jq + 1) * LANES] = os_[2 * jq] * masks[0] + os_[2 * jq + 1] * masks[1]
    ko_ref[0] = kcn
    vo_ref[0] = vc


def swa_prompt(q, kv, bsz, t, q_norm, k_norm, bias, sinks):
    nb = t // WINDOW
    q3 = q.reshape(bsz, t, QW)
    kv3 = kv.reshape(bsz, t, 2 * KW)
    o, ko, vo = pl.pallas_call(
        _swa_prompt_kernel,
        grid=(bsz, nb),
        in_specs=[pl.BlockSpec((1, WINDOW, QW), lambda b, i: (b, i, 0)),
                  pl.BlockSpec((1, WINDOW, 2 * KW), lambda b, i: (b, i, 0)),
                  pl.BlockSpec((1, WINDOW, 2 * KW), lambda b, i: (b, jnp.maximum(i - 1, 0), 0)),
                  pl.BlockSpec((1, QW), lambda b, i: (0, 0)),
                  pl.BlockSpec((1, KW), lambda b, i: (0, 0)),
                  pl.BlockSpec((N_Q_HEADS, WINDOW, 2 * WINDOW), lambda b, i: (0, 0, 0)),
                  pl.BlockSpec((N_Q_HEADS, LANES), lambda b, i: (0, 0))],
        out_specs=[pl.BlockSpec((1, WINDOW, QW), lambda b, i: (b, i, 0)),
                   pl.BlockSpec((1, WINDOW, KW), lambda b, i: (b, 0, 0)),
                   pl.BlockSpec((1, WINDOW, KW), lambda b, i: (b, 0, 0))],
        out_shape=[jax.ShapeDtypeStruct((bsz, t, QW), F32),
                   jax.ShapeDtypeStruct((bsz, WINDOW, KW), F32),
                   jax.ShapeDtypeStruct((bsz, WINDOW, KW), F32)],
        compiler_params=_cparams(("parallel", "arbitrary")), name="swa_prompt",
    )(q3, kv3, kv3, jnp.tile(q_norm, N_Q_HEADS).reshape(1, QW), jnp.tile(k_norm, N_KV_HEADS).reshape(1, KW),
      bias, jnp.broadcast_to(sinks[:, None], (N_Q_HEADS, LANES)))
    return o.reshape(bsz * t, QW), ko, vo


DEC_TP = 8


def _swa_decode_kernel(nbt, t_real, q_ref, kv_ref, ck_ref, cv_ref, qw_ref, kw_ref, bc_ref, bn_ref, sink_ref,
                       o_ref, ko_ref, vo_ref):
    rows = N_Q_HEADS * DEC_TP
    tq = _iota((rows, WINDOW), 0) % DEC_TP
    valid_c = _iota((rows, WINDOW), 1) > tq
    jn = _iota((rows, DEC_TP), 1)
    valid_n = jnp.logical_and(jn <= _iota((rows, DEC_TP), 0) % DEC_TP, jn < t_real)
    lane = _iota((1, LANES), 1)
    masks = (jnp.where(lane < ATT_HD, 1.0, 0.0), jnp.where(lane < ATT_HD, 0.0, 1.0))
    row8 = _iota((DEC_TP, 1), 0)
    scale = ATT_HD ** -0.5
    bias_c = jnp.concatenate([bc_ref[kv] for kv in range(N_KV_HEADS)], axis=0)
    bias_n = jnp.concatenate([bn_ref[kv][:, 0:DEC_TP] for kv in range(N_KV_HEADS)], axis=0)
    sink = jnp.concatenate([sink_ref[kv][:, 0:1] for kv in range(N_KV_HEADS)], axis=0)
    bs = range(nbt)
    heads = range(N_Q_HEADS)
    qn_all = _head_norm(q_ref[...].reshape(nbt * DEC_TP, QW), qw_ref[...])
    kvn_all = kv_ref[...].reshape(nbt * DEC_TP, 2 * KW)
    knew_all = _head_norm(kvn_all[:, 0:KW], kw_ref[...])
    knew = [knew_all[b * DEC_TP:(b + 1) * DEC_TP] for b in bs]
    vnew = [kvn_all[b * DEC_TP:(b + 1) * DEC_TP, KW:2 * KW] for b in bs]
    kc = [ck_ref[b] for b in bs]
    vc = [cv_ref[b] for b in bs]

    def stack_q(b):
        pieces = []
        for h in heads:
            qt = qn_all[b * DEC_TP:(b + 1) * DEC_TP, (h // 2) * LANES:(h // 2 + 1) * LANES]
            if h % 2 != h // Q_PER_KV:
                qt = pltpu.roll(qt, ATT_HD, 1)
            pieces.append(qt * masks[h // Q_PER_KV])
        return jnp.concatenate(pieces, axis=0)

    qs = [stack_q(b) for b in bs]
    l_c = [jnp.where(valid_c, _bdot(qs[b], kc[b], _NT) * scale + bias_c, NEG) for b in bs]
    l_n = [jnp.where(valid_n, _bdot(qs[b], knew[b], _NT) * scale + bias_n, NEG) for b in bs]
    mx = [jnp.maximum(jnp.maximum(jnp.max(l_c[b], axis=-1, keepdims=True),
                                  jnp.max(l_n[b], axis=-1, keepdims=True)), sink) for b in bs]
    p_c = [jnp.exp(l_c[b] - mx[b]) for b in bs]
    p_n = [jnp.exp(l_n[b] - mx[b]) for b in bs]
    inv = [1.0 / (jnp.sum(p_c[b], axis=-1, keepdims=True) + jnp.sum(p_n[b], axis=-1, keepdims=True)
                  + jnp.exp(sink - mx[b])) for b in bs]
    o = [_bdot(p_c[b] * inv[b], vc[b]) + _bdot(p_n[b] * inv[b], vnew[b]) for b in bs]
    for b in bs:
        for jq in range(QW // LANES):
            parts = []
            for h in (2 * jq, 2 * jq + 1):
                piece = o[b][h * DEC_TP:(h + 1) * DEC_TP]
                if h % 2 != h // Q_PER_KV:
                    piece = pltpu.roll(piece, ATT_HD, 1)
                parts.append(piece * masks[h % 2])
            o_ref[b, :, jq * LANES:(jq + 1) * LANES] = parts[0] + parts[1]
    for b in bs:
        for cache, new, out in ((kc[b], knew[b], ko_ref), (vc[b], vnew[b], vo_ref)):
            shifted = pltpu.roll(cache, WINDOW - t_real, 0)
            new_r = pltpu.roll(new, DEC_TP - t_real, 0)
            out[b, 0:WINDOW - DEC_TP] = shifted[0:WINDOW - DEC_TP]
            out[b, WINDOW - DEC_TP:WINDOW] = jnp.where(row8 >= DEC_TP - t_real, new_r,
                                                       shifted[WINDOW - DEC_TP:WINDOW])


def swa_decode(q, kv, cache_k, cache_v, bsz, t, q_norm, k_norm, rel_table, sinks, nbt):
    pad = ((0, 0), (0, DEC_TP - t), (0, 0))
    q3 = jnp.pad(q.reshape(bsz, t, QW), pad)
    kv3 = jnp.pad(kv.reshape(bsz, t, 2 * KW), pad)
    kpos = np.concatenate([np.arange(WINDOW) - WINDOW, np.arange(DEC_TP)])
    dist = np.arange(DEC_TP)[:, None] - kpos[None, :]
    bias = rel_bias(rel_table, dist)
    rows = Q_PER_KV * DEC_TP
    bias = bias.reshape(N_KV_HEADS, rows, WINDOW + DEC_TP)
    bias_c = bias[:, :, :WINDOW]
    bias_n = jnp.pad(bias[:, :, WINDOW:], ((0, 0), (0, 0), (0, LANES - DEC_TP)))
    sink_rows = jnp.broadcast_to(sinks.reshape(N_KV_HEADS, Q_PER_KV, 1, 1),
                                 (N_KV_HEADS, Q_PER_KV, DEC_TP, LANES)).reshape(N_KV_HEADS, rows, LANES)
    full3 = lambda shape: pl.BlockSpec(shape, lambda i: (0, 0, 0))
    o, ko, vo = pl.pallas_call(
        functools.partial(_swa_decode_kernel, nbt, t),
        grid=(bsz // nbt,),
        in_specs=[pl.BlockSpec((nbt, DEC_TP, QW), lambda i: (i, 0, 0)),
                  pl.BlockSpec((nbt, DEC_TP, 2 * KW), lambda i: (i, 0, 0)),
                  pl.BlockSpec((nbt, WINDOW, KW), lambda i: (i, 0, 0)),
                  pl.BlockSpec((nbt, WINDOW, KW), lambda i: (i, 0, 0)),
                  pl.BlockSpec((1, QW), lambda i: (0, 0)),
                  pl.BlockSpec((1, KW), lambda i: (0, 0)),
                  full3((N_KV_HEADS, rows, WINDOW)),
                  full3((N_KV_HEADS, rows, LANES)),
                  full3((N_KV_HEADS, rows, LANES))],
        out_specs=[pl.BlockSpec((nbt, DEC_TP, QW), lambda i: (i, 0, 0)),
                   pl.BlockSpec((nbt, WINDOW, KW), lambda i: (i, 0, 0)),
                   pl.BlockSpec((nbt, WINDOW, KW), lambda i: (i, 0, 0))],
        out_shape=[jax.ShapeDtypeStruct((bsz, DEC_TP, QW), F32),
                   jax.ShapeDtypeStruct((bsz, WINDOW, KW), F32),
                   jax.ShapeDtypeStruct((bsz, WINDOW, KW), F32)],
        compiler_params=_cparams(("parallel",)), name="swa_decode",
    )(q3, kv3, cache_k, cache_v, jnp.tile(q_norm, N_Q_HEADS).reshape(1, QW),
      jnp.tile(k_norm, N_KV_HEADS).reshape(1, KW), bias_c, bias_n, sink_rows)
    return o[:, :t].reshape(bsz * t, QW), ko, vo


S5_G = 32
S5_W = S5_G * S5_P
S5_CP = S5_CHUNK * S5_P
S5_PK = 2 * S5_N
S5_HW = S5_G * S5_PK


def _s5_prep_kernel(t_eff, a_ref, ldt_ref, b1_ref, b2_ref, c1_ref, c2_ref, kmat_ref, kb_ref, kct_ref, al_ref):
    L = S5_CHUNK
    ar2 = a_ref[0, 0:1, :]
    ai2 = a_ref[0, 1:2, :]
    step = jnp.exp(ldt_ref[0])
    mi = _iota((3 * SUBLANES, S5_PK), 0).astype(F32)
    mag = jnp.exp(mi * (step * ar2))
    ang = mi * (step * ai2)
    pwa = mag * jnp.cos(ang)
    pwb = mag * jnp.sin(ang)
    abr = pwa[1:2]
    abi = pwb[1:2]
    den = ar2 * ar2 + ai2 * ai2
    fa = ((abr - 1.0) * ar2 + abi * ai2) / den
    fb = (abi * ar2 - (abr - 1.0) * ai2) / den
    b1 = b1_ref[0]
    b2 = b2_ref[0]
    bp1 = b1 * fa + b2 * fb
    bp2 = b2 * fa - b1 * fb
    c1 = c1_ref[0]
    c2 = c2_ref[0]
    cpow = [c1 * pwa[m:m + 1] + c2 * pwb[m:m + 1] for m in range(L + 1)]
    kern_t = _hdot(bp1, jnp.concatenate(cpow[0:L], axis=0), _NT)
    lane = _iota((S5_P, S5_CP), 1)
    blocks = [kern_t]
    for i in range(1, L):
        blocks.append(jnp.where(lane >= S5_P * i, pltpu.roll(kern_t, S5_P * i, 1), 0.0))
    kmat_ref[0] = jnp.concatenate(blocks, axis=0).astype(BF16)
    kb = []
    for i in range(L):
        e = max(t_eff - 1 - i, 0)
        kb.append(bp1 * pwa[e:e + 1] + bp2 * pwb[e:e + 1])
    kb_ref[0] = jnp.concatenate(kb, axis=0).astype(BF16)
    kct_ref[0] = jnp.concatenate(cpow[1:L + 1], axis=0).astype(BF16)
    sgn = jnp.where(_iota((1, S5_PK), 1) < S5_N, -1.0, 1.0)
    al_ref[0] = jnp.concatenate([pwa[t_eff:t_eff + 1], sgn * pwb[t_eff:t_eff + 1]], axis=0)


def s5_prep(w, i, t_eff):
    dup = lambda z: jnp.concatenate([z, z], axis=-1)
    a = jnp.stack([dup(w["o_a_re"][i]), dup(w["o_a_im"][i])], axis=1)
    ldt = jnp.broadcast_to(w["o_log_dt"][i][:, None, None], (S5_G, 1, S5_PK))
    bt_re = jnp.swapaxes(w["o_b_re"][i], 1, 2)
    bt_im = jnp.swapaxes(w["o_b_im"][i], 1, 2)
    b1 = jnp.concatenate([bt_re, bt_im], -1)
    b2 = jnp.concatenate([-bt_im, bt_re], -1)
    c_re, c_im = w["o_c_re"][i], w["o_c_im"][i]
    c1 = jnp.concatenate([c_re, -c_im], -1)
    c2 = jnp.concatenate([-c_im, -c_re], -1)
    g3 = lambda r, c: pl.BlockSpec((1, r, c), lambda g: (g, 0, 0))
    return pl.pallas_call(
        functools.partial(_s5_prep_kernel, t_eff),
        grid=(S5_G,),
        in_specs=[g3(2, S5_PK), g3(1, S5_PK), g3(S5_P, S5_PK), g3(S5_P, S5_PK), g3(S5_P, S5_PK), g3(S5_P, S5_PK)],
        out_specs=[g3(S5_CP, S5_CP), g3(S5_CP, S5_PK), g3(S5_CP, S5_PK), g3(2, S5_PK)],
        out_shape=[jax.ShapeDtypeStruct((S5_G, S5_CP, S5_CP), BF16),
                   jax.ShapeDtypeStruct((S5_G, S5_CP, S5_PK), BF16),
                   jax.ShapeDtypeStruct((S5_G, S5_CP, S5_PK), BF16),
                   jax.ShapeDtypeStruct((S5_G, 2, S5_PK), F32)],
        compiler_params=_cparams(("parallel",)), name="s5_prep",
    )(a, ldt, b1, b2, c1, c2)


def _s5_e_kernel(u_ref, kb_ref, e_ref):
    e_ref[...] = _bdot(u_ref[0], kb_ref[0])


def _s5_swap(h):
    n = h.shape[-1]
    lane = _iota(h.shape, 1)
    return jnp.where(lane % S5_PK < S5_N, pltpu.roll(h, n - S5_N, 1), pltpu.roll(h, S5_N, 1))


def _s5_scan_kernel(bsz, cg, e_ref, h0_ref, ala_ref, alb_ref, hp_ref, hf_ref, h_scr):
    @pl.when(pl.program_id(0) == 0)
    def _():
        h_scr[...] = h0_ref[...]

    ala = ala_ref[...]
    alb = alb_ref[...]

    def body(c, hs):
        out = []
        for b in range(bsz):
            hp_ref[b, pl.ds(c, 1), :] = hs[b]
            out.append(ala * hs[b] + alb * _s5_swap(hs[b]) + e_ref[b, pl.ds(c, 1), :])
        return tuple(out)

    hs = lax.fori_loop(0, cg, body, tuple(h_scr[b:b + 1, :] for b in range(bsz)))
    for b in range(bsz):
        h_scr[b:b + 1, :] = hs[b]
        hf_ref[b:b + 1, :] = hs[b]


def _s5_step_kernel(e_ref, h0_ref, ala_ref, alb_ref, hf_ref):
    h = h0_ref[...]
    hf_ref[...] = ala_ref[...] * h + alb_ref[...] * _s5_swap(h) + e_ref[...]


def _s5_y_kernel(u_ref, hp_ref, kmat_ref, kc_ref, y_ref):
    y_ref[0] = _bdot(u_ref[0], kmat_ref[0]) + _bdot(hp_ref[...], kc_ref[0])


S5_QT = S5_W // LANES
S5_GT = LANES // S5_P
S5_XW = S5_CHUNK * LANES
S5_HQ = S5_GT * S5_PK


def s5_expand(prep):
    kmat, kb, kct, al = prep
    L = S5_CHUNK
    eye = jnp.eye(S5_GT, dtype=kmat.dtype)
    k6 = kmat.reshape(S5_QT, S5_GT, L, S5_P, L, S5_P)
    kbig = jnp.einsum("qgiptr,gh->qigpthr", k6, eye).reshape(S5_QT, S5_XW, S5_XW)
    kb5 = kb.reshape(S5_QT, S5_GT, L, S5_P, S5_PK)
    kbbig = jnp.einsum("qgipc,gh->qigphc", kb5, eye).reshape(S5_QT, S5_XW, S5_HQ)
    kc5 = kct.reshape(S5_QT, S5_GT, L, S5_P, S5_PK)
    kcbig = jnp.einsum("qgtrc,gh->qgcthr", kc5, eye).reshape(S5_QT, S5_HQ, S5_XW)
    return kbig, kbbig, kcbig, al[:, 0, :].reshape(1, S5_HW), al[:, 1, :].reshape(1, S5_HW)


def s5_core(u4, h0, bsz, t, mats):
    kbig, kbbig, kcbig, ala, alb = mats
    L = S5_CHUNK
    tp = -(-t // L) * L
    x = u4.reshape(S5_QT, bsz, t, LANES)
    if tp != t:
        x = jnp.pad(x, ((0, 0), (0, 0), (0, tp - t), (0, 0)))
    nc = tp // L
    rows = nc * bsz
    x = x.reshape(S5_QT, rows, S5_XW)
    tr = math.gcd(rows, 512)
    e = pl.pallas_call(
        _s5_e_kernel,
        grid=(S5_QT, rows // tr),
        in_specs=[pl.BlockSpec((1, tr, S5_XW), lambda q, r: (q, r, 0)),
                  pl.BlockSpec((1, S5_XW, S5_HQ), lambda q, r: (q, 0, 0))],
        out_specs=pl.BlockSpec((tr, S5_HQ), lambda q, r: (r, q)),
        out_shape=jax.ShapeDtypeStruct((rows, S5_HW), F32),
        compiler_params=_cparams(("parallel", "parallel")), name="s5_e",
    )(x, kbbig)
    row = pl.BlockSpec((1, S5_HW), lambda i: (0, 0))
    if nc == 1:
        hp = h0
        hf = pl.pallas_call(
            _s5_step_kernel,
            out_shape=jax.ShapeDtypeStruct((bsz, S5_HW), F32), name="s5_step",
        )(e, h0, ala, alb)
    else:
        cg = math.gcd(nc, 64)
        hp, hf = pl.pallas_call(
            functools.partial(_s5_scan_kernel, bsz, cg),
            grid=(nc // cg,),
            in_specs=[pl.BlockSpec((bsz, cg, S5_HW), lambda i: (0, i, 0)),
                      pl.BlockSpec((bsz, S5_HW), lambda i: (0, 0)), row, row],
            out_specs=[pl.BlockSpec((bsz, cg, S5_HW), lambda i: (0, i, 0)),
                       pl.BlockSpec((bsz, S5_HW), lambda i: (0, 0))],
            out_shape=[jax.ShapeDtypeStruct((bsz, nc, S5_HW), F32),
                       jax.ShapeDtypeStruct((bsz, S5_HW), F32)],
            scratch_shapes=[pltpu.VMEM((bsz, S5_HW), F32)],
            compiler_params=_cparams(("arbitrary",)), name="s5_scan",
        )(e.reshape(bsz, nc, S5_HW), h0, ala, alb)
    y = pl.pallas_call(
        _s5_y_kernel,
        grid=(S5_QT, rows // tr),
        in_specs=[pl.BlockSpec((1, tr, S5_XW), lambda q, r: (q, r, 0)),
                  pl.BlockSpec((tr, S5_HQ), lambda q, r: (r, q)),
                  pl.BlockSpec((1, S5_XW, S5_XW), lambda q, r: (q, 0, 0)),
                  pl.BlockSpec((1, S5_HQ, S5_XW), lambda q, r: (q, 0, 0))],
        out_specs=pl.BlockSpec((1, tr, S5_XW), lambda q, r: (q, r, 0)),
        out_shape=jax.ShapeDtypeStruct((S5_QT, rows, S5_XW), F32),
        compiler_params=_cparams(("parallel", "parallel")), name="s5_y",
    )(x, hp.reshape(rows, S5_HW), kbig, kcbig)
    y = y.reshape(S5_QT, bsz, tp, LANES)[:, :, :t].reshape(S5_QT, bsz * t, LANES)
    return y, hf


def _s5_post_kernel(y_ref, u_ref, d_ref, gw_ref, gb_ref, o_ref):
    y = jnp.concatenate([y_ref[q] for q in range(S5_QT)], axis=1)
    u = jnp.concatenate([u_ref[q] for q in range(S5_QT)], axis=1)
    x = y + d_ref[...] * u
    z = 0.5 * x * (1.0 + jnp.tanh(math.sqrt(2.0 / math.pi) * (x + 0.044715 * (x * x * x))))
    o_ref[...] = z * _sigmoid(_dg(z.astype(BF16), gw_ref[...], _NN) + gb_ref[...])


def s5_post(y4, u4, d, glu_w_bf16, glu_b, tm):
    m = y4.shape[1]
    spec4 = pl.BlockSpec((S5_QT, tm, LANES), lambda i: (0, i, 0))
    row = pl.BlockSpec((1, S5_W), lambda i: (0, 0))
    return pl.pallas_call(
        _s5_post_kernel,
        grid=(m // tm,),
        in_specs=[spec4, spec4, row, pl.BlockSpec((S5_W, S5_W), lambda i: (0, 0)), row],
        out_specs=pl.BlockSpec((tm, S5_W), lambda i: (i, 0)),
        out_shape=jax.ShapeDtypeStruct((m, S5_W), F32),
        compiler_params=_cparams(("parallel",)), name="s5_post",
    )(y4, u4, d.reshape(1, S5_W), glu_w_bf16, glu_b.reshape(1, S5_W))


def s5_mix(u4, h_re, h_im, bsz, t, mats, d, glu_w_bf16, glu_b, tm):
    h0 = jnp.concatenate([h_re, h_im], axis=-1).reshape(bsz, S5_HW)
    y4, hf = s5_core(u4, h0, bsz, t, mats)
    out = s5_post(y4, u4, d, glu_w_bf16, glu_b, tm)
    hf = hf.reshape(bsz, S5_G, 2, S5_N)
    return out, hf[:, :, 0], hf[:, :, 1]


GLA_DK = 64
GLA_DV = 128
GLA_HEADS = 4
GLA_KW = GLA_HEADS * GLA_DK
GLA_VW = GLA_HEADS * GLA_DV
GLA_PW = 2 * GLA_KW + 2 * GLA_VW + LANES


def _gla_kernel(L, t_real, nb, p_ref, aup_ref, ab_ref, nw_ref, s0_ref, y_ref, sf_ref, s_scr):
    c = pl.program_id(1)
    nc = pl.num_programs(1)
    n_pairs = GLA_KW // LANES

    @pl.when(c == 0)
    def _():
        for bb in range(nb):
            for j in range(n_pairs):
                s_scr[bb * n_pairs + j] = s0_ref[bb, j]

    lane = _iota((1, LANES), 1)
    masks = (jnp.where(lane < GLA_DK, 1.0, 0.0), jnp.where(lane < GLA_DK, 0.0, 1.0))
    incl = _iota((L, L), 1) <= _iota((L, L), 0)
    diag128 = _iota((LANES, LANES), 0) == _iota((LANES, LANES), 1)
    ones128 = jnp.ones((LANES, LANES), BF16)
    tri = _tri_incl(L)
    rows = range(nb)
    ps = [p_ref[bb] for bb in rows]
    zs = [_bdot(p[:, 2 * GLA_KW + 2 * GLA_VW:], aup_ref[...]) + ab_ref[...] for p in ps]
    gk = [-_softplus(-z) * (1.0 / GLA_GATE_NORM) for z in zs]
    if t_real % L != 0:
        tok = c * L + _iota((L, 1), 0)
        gk = [jnp.where(tok < t_real, x, 0.0) for x in gk]
    b = [_xdot_r(tri, x) for x in gk]
    bl = [x[L - 1:L, :] for x in b]
    qd = [ps[i][:, 0:GLA_KW] * (GLA_DK ** -0.5) * jnp.exp(b[i]) for i in rows]
    kh = [ps[i][:, GLA_KW:2 * GLA_KW] * jnp.exp(-b[i]) for i in rows]
    kt = [ps[i][:, GLA_KW:2 * GLA_KW] * jnp.exp(bl[i] - b[i]) for i in rows]
    heads = [(bb, h) for bb in rows for h in range(GLA_HEADS)]
    sl = lambda h: slice((h // 2) * LANES, (h // 2 + 1) * LANES)
    hs = lambda h: slice(2 * GLA_KW + h * GLA_DV, 2 * GLA_KW + (h + 1) * GLA_DV)
    gs = lambda h: slice(2 * GLA_KW + GLA_VW + h * GLA_DV, 2 * GLA_KW + GLA_VW + (h + 1) * GLA_DV)
    st = [s_scr[bb * n_pairs + j] for bb in rows for j in range(n_pairs)]
    qm = [qd[bb][:, sl(h)] * masks[h % 2] for bb, h in heads]
    vh = [ps[bb][:, hs(h)] for bb, h in heads]
    attn = [jnp.where(incl, _bdot(qm[i], kh[bb][:, sl(h)], _NT), 0.0) for i, (bb, h) in enumerate(heads)]
    o = [_bdot(attn[i], vh[i]) + _bdot(qm[i], st[bb * n_pairs + h // 2]) for i, (bb, h) in enumerate(heads)]
    kv = [_bdot(kt[bb][:, sl(h)], vh[i], _TN) for i, (bb, h) in enumerate(heads)]
    for i, (bb, h) in enumerate(heads):
        of = o[i] * lax.rsqrt(jnp.mean(o[i] * o[i], axis=-1, keepdims=True) + EPS) * nw_ref[...]
        y_ref[bb, :, h * GLA_DV:(h + 1) * GLA_DV] = of * _silu(ps[bb][:, gs(h)])
    for bb in rows:
        for j in range(n_pairs):
            i0 = bb * GLA_HEADS + 2 * j
            pcol = _xdot_l(jnp.where(diag128, jnp.exp(bl[bb][:, j * LANES:(j + 1) * LANES]), 0.0), ones128)
            s_scr[bb * n_pairs + j] = pcol * st[bb * n_pairs + j] + jnp.concatenate(
                [kv[i0][0:GLA_DK], kv[i0 + 1][GLA_DK:2 * GLA_DK]], axis=0)

    @pl.when(c == nc - 1)
    def _():
        for bb in range(nb):
            for j in range(n_pairs):
                sf_ref[bb, j] = s_scr[bb * n_pairs + j]


def gla_mix(p_gla, s0, bsz, t, L, aup_pad, a_b, norm_w, nb=1):
    tp = -(-t // L) * L
    p3 = p_gla.reshape(bsz, t, GLA_PW)
    if tp != t:
        p3 = jnp.pad(p3, ((0, 0), (0, tp - t), (0, 0)))
    n_pairs = GLA_KW // LANES
    st_spec = pl.BlockSpec((nb, n_pairs, LANES, LANES), lambda b, c: (b, 0, 0, 0))
    y, s_fin = pl.pallas_call(
        functools.partial(_gla_kernel, L, t, nb),
        grid=(bsz // nb, tp // L),
        in_specs=[pl.BlockSpec((nb, L, GLA_PW), lambda b, c: (b, c, 0)),
                  pl.BlockSpec((LANES, GLA_KW), lambda b, c: (0, 0)),
                  pl.BlockSpec((1, GLA_KW), lambda b, c: (0, 0)),
                  pl.BlockSpec((1, GLA_DV), lambda b, c: (0, 0)),
                  st_spec],
        out_specs=[pl.BlockSpec((nb, L, GLA_VW), lambda b, c: (b, c, 0)), st_spec],
        out_shape=[jax.ShapeDtypeStruct((bsz, tp, GLA_VW), F32),
                   jax.ShapeDtypeStruct((bsz, n_pairs, LANES, LANES), F32)],
        scratch_shapes=[pltpu.VMEM((nb * n_pairs, LANES, LANES), F32)],
        compiler_params=_cparams(("parallel", "arbitrary")), name="gla",
    )(p3, aup_pad, a_b.reshape(1, GLA_KW), norm_w.reshape(1, GLA_DV),
      s0.reshape(bsz, n_pairs, LANES, LANES))
    return y[:, :t].reshape(bsz * t, GLA_VW), s_fin.reshape(bsz, GLA_HEADS, GLA_DK, GLA_DV)


GLA_LR = 16
D_FF_CHUNK = 1408
D_FF_EXPERT_CHUNK = 896
ROUTER_TOKENS = 1024
MOE_TOKENS = 2048


def _prepare_weights(w):
    win = w["e_w_in"][0]
    w_gla = jnp.pad(win[:, RSHIFT:], ((0, 0), (0, LANES - GLA_LR)))
    wo = w["o_w_in"][0]
    return {
        "e_w_rwkv": win[:, :RSHIFT].astype(BF16),
        "e_w_gla": w_gla.astype(BF16),
        "rwkv": rwkv_params(w, 0),
        "gla_aup": jnp.pad(w["e_gla_a_up"][0], ((0, LANES - GLA_LR), (0, 0))).astype(BF16),
        "e_wo_a": w["e_w_out"][0][:RW].astype(BF16),
        "e_wo_b": w["e_w_out"][0][RW:].astype(BF16),
        "ff_w1": chunk_major(w["e_ff_w1"][0].astype(BF16), D_FF_CHUNK),
        "ff_w3": chunk_major(w["e_ff_w3"][0].astype(BF16), D_FF_CHUNK),
        "ff_w2": w["e_ff_w2"][0].astype(BF16),
        "o_w_q": wo[:, :QW].astype(BF16),
        "o_w_kv": wo[:, QW:QW + 2 * KW].astype(BF16),
        "o_w_u": wo[:, QW + 2 * KW:].astype(BF16),
        "glu_w": w["o_glu_w"][0].astype(BF16),
        "o_wo_a": w["o_w_out"][0][:QW].astype(BF16),
        "o_wo_b": w["o_w_out"][0][QW:].astype(BF16),
        "moe_w1": chunk_major(w["o_moe_w1"][0].astype(BF16), D_FF_EXPERT_CHUNK),
        "moe_w3": chunk_major(w["o_moe_w3"][0].astype(BF16), D_FF_EXPERT_CHUNK),
        "moe_w2": w["o_moe_w2"][0].astype(BF16),
    }


def _trunk(x3, st, w, pw, tm, chunk, nb, s5_prep_t, prompt_bias):
    bsz, t, d = x3.shape
    m = bsz * t
    x = x3.reshape(m, d)
    p_r, p_g = norm_proj(x, w["e_norm1"][0], [pw["e_w_rwkv"], pw["e_w_gla"]], tm)
    ya, s_rwkv = rwkv_mix(p_r, st["shift"], st["rwkv"], bsz, t, chunk, tm, pw["rwkv"],
                          w["e_lnx_w"][0], w["e_lnx_b"][0], nb)
    s_shift = p_r.reshape(bsz, t, RSHIFT)[:, -1]
    yb, s_gla = gla_mix(p_g, st["gla"], bsz, t, chunk, pw["gla_aup"], w["e_gla_a_b"][0], w["e_gla_norm"][0], nb)
    x = out_proj(x, ya, yb, pw["e_wo_a"], pw["e_wo_b"], tm)
    x = ffn(x, w["e_norm2"][0], pw["ff_w1"], pw["ff_w3"], pw["ff_w2"], tm)
    q, kv, u = norm_proj_tiles(x, w["o_norm1"][0], [pw["o_w_q"], pw["o_w_kv"], pw["o_w_u"]], tm)
    if st["win_k"] is None:
        yc, nk, nv = swa_prompt(q, kv, bsz, t, w["o_q_norm"][0], w["o_k_norm"][0], prompt_bias, w["o_sinks"][0])
    else:
        yc, nk, nv = swa_decode(q, kv, st["win_k"].reshape(bsz, WINDOW, KW), st["win_v"].reshape(bsz, WINDOW, KW),
                                bsz, t, w["o_q_norm"][0], w["o_k_norm"][0], w["rel_table"], w["o_sinks"][0], 8)
    yd, s5r, s5i = s5_mix(u, st["s5_re"], st["s5_im"], bsz, t, s5_prep_t, w["o_d"][0], pw["glu_w"],
                          w["o_glu_b"][0], tm)
    x = out_proj(x, yc, yd, pw["o_wo_a"], pw["o_wo_b"], tm)
    t_router = min(m, ROUTER_TOKENS)
    hn, gates, pos, counts = router(x, w["o_norm2"][0], w["o_router_w"][0], w["o_router_b"][0], t_router)
    x = moe(x, hn, gates, pos, counts, pw["moe_w1"], pw["moe_w3"], pw["moe_w2"], t_router, min(m, MOE_TOKENS))
    kv_shape = (bsz, WINDOW, N_KV_HEADS, ATT_HD)
    return (x.reshape(bsz, t, d), s_rwkv[None], s_shift[None], s_gla[None], nk.reshape(kv_shape)[None],
            nv.reshape(kv_shape)[None], s5r[None], s5i[None])


def kernel(x_prompt, x_sample, state_rwkv, state_shift, state_gla, cache_win_k, cache_win_v, state_s5_re,
           state_s5_im, rel_table, e_norm1, e_w_in, e_mu, e_w0, e_w_up, e_a0, e_a_up, e_g_up, e_k_k, e_k_a, e_r_k,
           e_lnx_w, e_lnx_b, e_gla_a_up, e_gla_a_b, e_gla_norm, e_w_out, e_norm2, e_ff_w1, e_ff_w3, e_ff_w2,
           o_norm1, o_w_in, o_q_norm, o_k_norm, o_sinks, o_a_re, o_a_im, o_log_dt, o_b_re, o_b_im, o_c_re, o_c_im,
           o_d, o_glu_w, o_glu_b, o_w_out, o_norm2, o_router_w, o_router_b, o_moe_w1, o_moe_w3, o_moe_w2):
    w = dict(rel_table=rel_table, e_norm1=e_norm1, e_w_in=e_w_in, e_mu=e_mu, e_w0=e_w0, e_w_up=e_w_up, e_a0=e_a0,
             e_a_up=e_a_up, e_g_up=e_g_up, e_k_k=e_k_k, e_k_a=e_k_a, e_r_k=e_r_k, e_lnx_w=e_lnx_w, e_lnx_b=e_lnx_b,
             e_gla_a_up=e_gla_a_up, e_gla_a_b=e_gla_a_b, e_gla_norm=e_gla_norm, e_w_out=e_w_out, e_norm2=e_norm2,
             e_ff_w1=e_ff_w1, e_ff_w3=e_ff_w3, e_ff_w2=e_ff_w2, o_norm1=o_norm1, o_w_in=o_w_in, o_q_norm=o_q_norm,
             o_k_norm=o_k_norm, o_sinks=o_sinks, o_a_re=o_a_re, o_a_im=o_a_im, o_log_dt=o_log_dt, o_b_re=o_b_re,
             o_b_im=o_b_im, o_c_re=o_c_re, o_c_im=o_c_im, o_d=o_d, o_glu_w=o_glu_w, o_glu_b=o_glu_b,
             o_w_out=o_w_out, o_norm2=o_norm2, o_router_w=o_router_w, o_router_b=o_router_b, o_moe_w1=o_moe_w1,
             o_moe_w3=o_moe_w3, o_moe_w2=o_moe_w2)
    pw = _prepare_weights(w)
    bp, tp, _ = x_prompt.shape
    bs, ts, _ = x_sample.shape
    qi = np.arange(WINDOW)[:, None]
    kj = np.arange(2 * WINDOW)[None, :]
    prompt_bias = rel_bias(rel_table, qi + WINDOW - kj)
    zeros = lambda *shape: jnp.zeros(shape, F32)
    st_p = {"rwkv": zeros(bp, RW // RWKV_HD, RWKV_HD, RWKV_HD), "shift": zeros(bp, RSHIFT),
            "gla": zeros(bp, GLA_HEADS, GLA_DK, GLA_DV), "win_k": None, "win_v": None,
            "s5_re": zeros(bp, S5_G, S5_N), "s5_im": zeros(bp, S5_G, S5_N)}
    st_s = {"rwkv": state_rwkv[0], "shift": state_shift[0], "gla": state_gla[0], "win_k": cache_win_k[0],
            "win_v": cache_win_v[0], "s5_re": state_s5_re[0], "s5_im": state_s5_im[0]}
    out_p = _trunk(x_prompt, st_p, w, pw, 512, 64, math.gcd(bp, 2), s5_expand(s5_prep(w, 0, S5_CHUNK)),
                   prompt_bias)
    out_s = _trunk(x_sample, st_s, w, pw, bs * ts, 8, math.gcd(bs, 8), s5_expand(s5_prep(w, 0, ts)), None)
    res = [out_p[0], out_s[0]]
    for a, b in zip(out_p[1:], out_s[1:]):
        res += [a, b]
    return tuple(res)
```

```python
import functools
import math

import jax
import jax.numpy as jnp
import numpy as np
from jax import lax
from jax.experimental import pallas as pl
from jax.experimental.pallas import tpu as pltpu

F32 = jnp.float32
BF16 = jnp.bfloat16

LANES = 128
SUBLANES = 8
VMEM_LIMIT_BYTES = 56 * 1024 * 1024

EPS = 1e-6
RWKV_HD = 64
RWKV_GN_EPS = 64e-5
GLA_GATE_NORM = 16.0
ATT_HD = 64
WINDOW = 128
N_BUCKETS = 32
BUCKET_MAX_DIST = 128
S5_P = 16
S5_N = 64
S5_CHUNK = 16
TOP_K = 2


def _cparams(sem):
    return pltpu.CompilerParams(dimension_semantics=sem, vmem_limit_bytes=VMEM_LIMIT_BYTES)


_NN = (((1,), (0,)), ((), ()))
_NT = (((1,), (1,)), ((), ()))
_TN = (((0,), (0,)), ((), ()))


def _dg(a, b, dims):
    return lax.dot_general(a, b, dims, preferred_element_type=F32)


def _bdot(a, b, dims=_NN):
    return _dg(a.astype(BF16), b.astype(BF16), dims)


def _split(a, n):
    terms = []
    r = a
    for _ in range(n):
        t = r.astype(BF16)
        terms.append(t)
        r = r - t.astype(F32)
    return terms


def _hdot(a, b, dims=_NN):
    a0, a1 = _split(a, 2)
    b0, b1 = _split(b, 2)
    return _dg(a0, b0, dims) + (_dg(a0, b1, dims) + _dg(a1, b0, dims))


def _xdot_l(a, e, dims=_NN):
    e = e.astype(BF16)
    a0, a1, a2 = _split(a, 3)
    return _dg(a0, e, dims) + (_dg(a1, e, dims) + _dg(a2, e, dims))


def _xdot_r(e, b, dims=_NN):
    e = e.astype(BF16)
    b0, b1, b2 = _split(b, 3)
    return _dg(e, b0, dims) + (_dg(e, b1, dims) + _dg(e, b2, dims))


def _iota(shape, axis):
    return lax.broadcasted_iota(jnp.int32, shape, axis)


def _seg_ones(n, seg):
    r = _iota((n, n), 0) // seg
    c = _iota((n, n), 1) // seg
    return jnp.where(r == c, 1.0, 0.0).astype(BF16)


def _seg_sum(x, seg):
    n = x.shape[-1]
    return _xdot_l(x, _seg_ones(n, seg))


def _sigmoid(x):
    return 1.0 / (1.0 + jnp.exp(-x))


def _silu(x):
    return x * _sigmoid(x)


def _softplus(x):
    return jnp.maximum(x, 0.0) + jnp.log(1.0 + jnp.exp(-jnp.abs(x)))


def _tri_incl(n):
    r = _iota((n, n), 0)
    c = _iota((n, n), 1)
    return jnp.where(c <= r, 1.0, 0.0).astype(BF16)


def _rms(x, g):
    return x * lax.rsqrt(jnp.mean(x * x, axis=-1, keepdims=True) + EPS) * g


def _norm_proj_kernel(n_w, x_ref, g_ref, *refs):
    xn = _rms(x_ref[...], g_ref[...]).astype(BF16)
    for w_ref, o_ref in zip(refs[:n_w], refs[n_w:]):
        o_ref[...] = _dg(xn, w_ref[...], _NN)


def norm_proj(x, g, ws_bf16, tm):
    m, d = x.shape
    return pl.pallas_call(
        functools.partial(_norm_proj_kernel, len(ws_bf16)),
        grid=(m // tm,),
        in_specs=[pl.BlockSpec((tm, d), lambda i: (i, 0)),
                  pl.BlockSpec((1, d), lambda i: (0, 0))]
                 + [pl.BlockSpec(w.shape, lambda i: (0, 0)) for w in ws_bf16],
        out_specs=[pl.BlockSpec((tm, w.shape[1]), lambda i: (i, 0)) for w in ws_bf16],
        out_shape=[jax.ShapeDtypeStruct((m, w.shape[1]), F32) for w in ws_bf16],
        compiler_params=_cparams(("parallel",)), name="norm_proj",
    )(x, g.reshape(1, d), *ws_bf16)


def _norm_proj_tiles_kernel(n_w, x_ref, g_ref, *refs):
    xn = _rms(x_ref[...], g_ref[...]).astype(BF16)
    for w_ref, o_ref in zip(refs[:n_w - 1], refs[n_w:]):
        o_ref[...] = _dg(xn, w_ref[...], _NN)
    u = _dg(xn, refs[n_w - 1][...], _NN)
    u_ref = refs[-1]
    for q in range(u_ref.shape[0]):
        u_ref[q] = u[:, q * LANES:(q + 1) * LANES]


def norm_proj_tiles(x, g, ws_bf16, tm):
    m, d = x.shape
    nt = ws_bf16[-1].shape[1] // LANES
    return pl.pallas_call(
        functools.partial(_norm_proj_tiles_kernel, len(ws_bf16)),
        grid=(m // tm,),
        in_specs=[pl.BlockSpec((tm, d), lambda i: (i, 0)),
                  pl.BlockSpec((1, d), lambda i: (0, 0))]
                 + [pl.BlockSpec(w.shape, lambda i: (0, 0)) for w in ws_bf16],
        out_specs=[pl.BlockSpec((tm, w.shape[1]), lambda i: (i, 0)) for w in ws_bf16[:-1]]
                  + [pl.BlockSpec((nt, tm, LANES), lambda i: (0, i, 0))],
        out_shape=[jax.ShapeDtypeStruct((m, w.shape[1]), F32) for w in ws_bf16[:-1]]
                  + [jax.ShapeDtypeStruct((nt, m, LANES), F32)],
        compiler_params=_cparams(("parallel",)), name="norm_proj_tiles",
    )(x, g.reshape(1, d), *ws_bf16)


def _out_proj_kernel(x_ref, ya_ref, yb_ref, wa_ref, wb_ref, o_ref):
    o_ref[...] = (x_ref[...] + _dg(ya_ref[...].astype(BF16), wa_ref[...], _NN)
                  + _dg(yb_ref[...].astype(BF16), wb_ref[...], _NN))


def out_proj(x, ya, yb, wa, wb, tm):
    m, d = x.shape
    return pl.pallas_call(
        _out_proj_kernel,
        grid=(m // tm,),
        in_specs=[pl.BlockSpec((tm, d), lambda i: (i, 0)),
                  pl.BlockSpec((tm, ya.shape[1]), lambda i: (i, 0)),
                  pl.BlockSpec((tm, yb.shape[1]), lambda i: (i, 0)),
                  pl.BlockSpec(wa.shape, lambda i: (0, 0)),
                  pl.BlockSpec(wb.shape, lambda i: (0, 0))],
        out_specs=pl.BlockSpec((tm, d), lambda i: (i, 0)),
        out_shape=jax.ShapeDtypeStruct((m, d), F32),
        compiler_params=_cparams(("parallel",)), name="out_proj",
    )(x, ya, yb, wa, wb)


def _ffn_kernel(x_ref, g_ref, w1_ref, w3_ref, w2_ref, o_ref, xn_scr):
    j = pl.program_id(1)

    @pl.when(j == 0)
    def _():
        x = x_ref[...]
        xn_scr[...] = _rms(x, g_ref[...]).astype(BF16)
        o_ref[...] = x

    xn = xn_scr[...]
    h = _silu(_dg(xn, w1_ref[...], _NN)) * _dg(xn, w3_ref[...], _NN)
    o_ref[...] += _dg(h.astype(BF16), w2_ref[...], _NN)


def ffn(x, g, w1, w3, w2, tm, fc):
    m, d = x.shape
    dff = w1.shape[1]
    return pl.pallas_call(
        _ffn_kernel,
        grid=(m // tm, dff // fc),
        in_specs=[pl.BlockSpec((tm, d), lambda i, j: (i, 0)),
                  pl.BlockSpec((1, d), lambda i, j: (0, 0)),
                  pl.BlockSpec((d, fc), lambda i, j: (0, j)),
                  pl.BlockSpec((d, fc), lambda i, j: (0, j)),
                  pl.BlockSpec((fc, d), lambda i, j: (j, 0))],
        out_specs=pl.BlockSpec((tm, d), lambda i, j: (i, 0)),
        out_shape=jax.ShapeDtypeStruct((m, d), F32),
        scratch_shapes=[pltpu.VMEM((tm, d), BF16)],
        compiler_params=_cparams(("parallel", "arbitrary")), name="ffn",
    )(x, g.reshape(1, d), w1, w3, w2)


def _router_kernel(n_exp, x_ref, g_ref, rw_ref, rb_ref, hn_ref, gate_ref, pos_ref, cnt_ref):
    xn = _rms(x_ref[...], g_ref[...])
    hn_ref[...] = xn.astype(BF16)
    logits = _hdot(xn, rw_ref[...]) + rb_ref[...]
    lane = _iota(logits.shape, 1)
    logits = jnp.where(lane < n_exp, logits, -jnp.inf)
    m1 = jnp.max(logits, axis=-1, keepdims=True)
    i1 = jnp.min(jnp.where(logits == m1, lane, LANES), axis=-1, keepdims=True)
    rest = jnp.where(lane == i1, -jnp.inf, logits)
    m2 = jnp.max(rest, axis=-1, keepdims=True)
    i2 = jnp.min(jnp.where(rest == m2, lane, LANES), axis=-1, keepdims=True)
    e2 = jnp.exp(m2 - m1)
    g1 = 1.0 / (1.0 + e2)
    g2 = e2 / (1.0 + e2)
    pick1 = lane == i1
    pick2 = lane == i2
    gate_ref[...] = jnp.where(pick1, g1, 0.0) + jnp.where(pick2, g2, 0.0)
    tm = logits.shape[0]
    sel = jnp.where(jnp.logical_or(pick1, pick2), 1.0, 0.0).astype(BF16)
    tr = _iota((tm, tm), 0)
    tc = _iota((tm, tm), 1)
    upper = jnp.where(tr <= tc, 1.0, 0.0).astype(BF16)
    eye = jnp.where(tr == tc, 1.0, 0.0).astype(BF16)
    rank_t = _dg(sel, upper, _TN)
    sel_t = _dg(sel, eye, _TN)
    pos_t = jnp.where(sel_t > 0.5, rank_t - 1.0, -1.0)
    pos_ref[...] = pos_t[0:SUBLANES, :]
    cnt_ref[0] = jnp.sum(sel.astype(F32), axis=0, keepdims=True)


def router(x, g, rw, rb, tm):
    m, d = x.shape
    n_exp = rw.shape[1]
    assert n_exp <= SUBLANES
    rw_pad = jnp.pad(rw, ((0, 0), (0, LANES - n_exp)))
    rb_pad = jnp.pad(rb, (0, LANES - n_exp)).reshape(1, LANES)
    return pl.pallas_call(
        functools.partial(_router_kernel, n_exp),
        grid=(m // tm,),
        in_specs=[pl.BlockSpec((tm, d), lambda i: (i, 0)),
                  pl.BlockSpec((1, d), lambda i: (0, 0)),
                  pl.BlockSpec((d, LANES), lambda i: (0, 0)),
                  pl.BlockSpec((1, LANES), lambda i: (0, 0))],
        out_specs=[pl.BlockSpec((tm, d), lambda i: (i, 0)),
                   pl.BlockSpec((tm, LANES), lambda i: (i, 0)),
                   pl.BlockSpec((SUBLANES, tm), lambda i: (0, i)),
                   pl.BlockSpec((1, 1, LANES), lambda i: (i, 0, 0))],
        out_shape=[jax.ShapeDtypeStruct((m, d), BF16), jax.ShapeDtypeStruct((m, LANES), F32),
                   jax.ShapeDtypeStruct((SUBLANES, m), F32),
                   jax.ShapeDtypeStruct((m // tm, 1, LANES), F32)],
        compiler_params=_cparams(("parallel",)), name="router",
    )(x, g.reshape(1, d), rw_pad, rb_pad)


MOE_ROWS = 128


def _moe_kernel(n_exp, cnt_ref, x_hbm, hn_ref, gate_ref, pos_ref, w1_ref, w3_ref, w2_ref, o_ref, xs_scr, y_scr):
    i = pl.program_id(0)
    e = pl.program_id(1)
    j = pl.program_id(2)
    nj = pl.num_programs(2)
    tm = hn_ref.shape[0]
    n_small = (cnt_ref[i * n_exp + e] + (MOE_ROWS - 1)) // MOE_ROWS
    n_big = n_small // 2
    tail = n_small % 2

    @pl.when(jnp.logical_and(e == 0, j == 0))
    def _():
        pltpu.sync_copy(x_hbm.at[pl.ds(pl.multiple_of(i * tm, tm), tm), :], o_ref)

    def select(start, nrows):
        pos = pos_ref[pl.ds(e, 1), :]
        want = (start + _iota((nrows, tm), 0)).astype(F32)
        return jnp.where(pos == want, 1.0, 0.0).astype(BF16)

    def blocks(body):
        def big(blk, carry):
            body(pl.multiple_of(blk * 2 * MOE_ROWS, 2 * MOE_ROWS), 2 * MOE_ROWS)
            return carry
        lax.fori_loop(0, n_big, big, 0)

        @pl.when(tail == 1)
        def _():
            body(pl.multiple_of(n_big * 2 * MOE_ROWS, 2 * MOE_ROWS), MOE_ROWS)

    @pl.when(j == 0)
    def _():
        def gather(start, nrows):
            rows = pl.ds(start, nrows)
            xs_scr[rows, :] = _dg(select(start, nrows), hn_ref[...], _NN).astype(BF16)
            y_scr[rows, :] = jnp.zeros((nrows, y_scr.shape[1]), F32)
        blocks(gather)

    def expert(start, nrows):
        rows = pl.ds(start, nrows)
        xs = xs_scr[rows, :]
        h = _silu(_dg(xs, w1_ref[0], _NN)) * _dg(xs, w3_ref[0], _NN)
        y_scr[rows, :] += _dg(h.astype(BF16), w2_ref[0], _NN)
    blocks(expert)

    @pl.when(j == nj - 1)
    def _():
        gt = gate_ref[...]
        gcol = jnp.sum(jnp.where(_iota(gt.shape, 1) == e, gt, 0.0), axis=-1, keepdims=True)

        def scatter(start, nrows):
            o_ref[...] += gcol * _dg(select(start, nrows), y_scr[pl.ds(start, nrows), :].astype(BF16), _TN)
        blocks(scatter)


def moe(x, hn, gates, pos, counts, w1, w3, w2, tm_router, tm, fc):
    m, d = x.shape
    n_exp, _, dff = w1.shape
    nj = dff // fc
    ratio = tm // tm_router
    cnt = counts[:, 0, :n_exp].astype(jnp.int32).reshape(m // tm, ratio, n_exp)
    before = (jnp.cumsum(cnt, axis=1) - cnt).astype(F32)
    shift = jnp.repeat(before.reshape(m // tm_router, n_exp).T, tm_router, axis=1)
    shift = jnp.pad(shift, ((0, SUBLANES - n_exp), (0, 0)))
    pos = jnp.where(pos >= 0, pos + shift, pos)
    cnt = cnt.sum(axis=1).reshape(-1)
    grid_spec = pltpu.PrefetchScalarGridSpec(
        num_scalar_prefetch=1,
        grid=(m // tm, n_exp, nj),
        in_specs=[pl.BlockSpec(memory_space=pl.ANY),
                  pl.BlockSpec((tm, d), lambda i, e, j, c: (i, 0)),
                  pl.BlockSpec((tm, LANES), lambda i, e, j, c: (i, 0)),
                  pl.BlockSpec((SUBLANES, tm), lambda i, e, j, c: (0, i)),
                  pl.BlockSpec((1, d, fc), lambda i, e, j, c: (e, 0, j)),
                  pl.BlockSpec((1, d, fc), lambda i, e, j, c: (e, 0, j)),
                  pl.BlockSpec((1, fc, d), lambda i, e, j, c: (e, j, 0))],
        out_specs=pl.BlockSpec((tm, d), lambda i, e, j, c: (i, 0)),
        scratch_shapes=[pltpu.VMEM((tm, d), BF16), pltpu.VMEM((tm, d), F32)])
    return pl.pallas_call(
        functools.partial(_moe_kernel, n_exp),
        grid_spec=grid_spec,
        out_shape=jax.ShapeDtypeStruct((m, d), F32),
        compiler_params=_cparams(("parallel", "arbitrary", "arbitrary")), name="moe",
    )(cnt, x, hn, gates, pos, w1, w3, w2)


RW = 512
RSHIFT = 1792


def _rwkv_pre_kernel(t_per_batch, tm, p_ref, pb_ref, prev_ref, mu_ref, w0_ref, wup_ref, a0_ref, aup_ref,
                     gup_ref, kk_ref, ka_ref, rk_ref,
                     r_o, lw_o, k_o, v_o, al_o, be_o, g_o, bo_o):
    i = pl.program_id(0)
    p = p_ref[...]
    rolled = pltpu.roll(p, 1, 0)
    row = _iota((tm, 1), 0)
    rolled = jnp.where(row == 0, pb_ref[SUBLANES - 1:SUBLANES, :], rolled)
    if t_per_batch >= tm:
        first = (i * tm) % t_per_batch == 0
        is_start = jnp.logical_and(row == 0, first)
        prev = jnp.where(is_start, prev_ref[0], rolled)
    else:
        is_start = (row % t_per_batch) == 0
        prev = jnp.where(is_start, prev_ref[...], rolled)
    xs = p + (prev - p) * mu_ref[...]
    r = xs[:, 0:RW]
    k = xs[:, RW:2 * RW]
    v = xs[:, 2 * RW:3 * RW]
    lr = xs[:, 3 * RW:3 * RW + LANES]
    gd = xs[:, 3 * RW + LANES:3 * RW + 2 * LANES]
    w_pre = w0_ref[...] + _bdot(jnp.tanh(lr), wup_ref[...])
    logw = -jnp.exp(-_softplus(-w_pre) - 0.5)
    a = _sigmoid(a0_ref[...] + _bdot(lr, aup_ref[...]))
    g = _bdot(_sigmoid(gd), gup_ref[...])
    kkr = k * kk_ref[...]
    kk = kkr / jnp.maximum(jnp.sqrt(_seg_sum(kkr * kkr, RWKV_HD)), 1e-12)
    k2 = k * (1.0 + (a - 1.0) * ka_ref[...])
    bonus = _seg_sum(r * k2 * rk_ref[...], RWKV_HD) * v
    r_o[...] = r
    lw_o[...] = logw
    k_o[...] = k2
    v_o[...] = v
    al_o[...] = -kk
    be_o[...] = kk * a
    g_o[...] = g
    bo_o[...] = bonus


def rwkv_pre(p_full, prev_rows, t_per_batch, tm, prm):
    m = p_full.shape[0]
    nb8 = tm // SUBLANES
    row_spec = pl.BlockSpec((1, RW), lambda i: (0, 0))
    if t_per_batch >= tm:
        prev_spec = pl.BlockSpec((1, 1, RSHIFT), lambda i: ((i * tm) // t_per_batch, 0, 0))
    else:
        prev_spec = pl.BlockSpec((tm, RSHIFT), lambda i: (i, 0))
    out_sds = jax.ShapeDtypeStruct((m, RW), F32)
    out_spec = pl.BlockSpec((tm, RW), lambda i: (i, 0))
    return pl.pallas_call(
        functools.partial(_rwkv_pre_kernel, t_per_batch, tm),
        grid=(m // tm,),
        in_specs=[pl.BlockSpec((tm, RSHIFT), lambda i: (i, 0)),
                  pl.BlockSpec((SUBLANES, RSHIFT), lambda i: (jnp.maximum(i * nb8 - 1, 0), 0)),
                  prev_spec,
                  pl.BlockSpec((1, RSHIFT), lambda i: (0, 0)),
                  row_spec,
                  pl.BlockSpec((LANES, RW), lambda i: (0, 0)),
                  row_spec,
                  pl.BlockSpec((LANES, RW), lambda i: (0, 0)),
                  pl.BlockSpec((LANES, RW), lambda i: (0, 0)),
                  row_spec, row_spec, row_spec],
        out_specs=[out_spec] * 8,
        out_shape=[out_sds] * 8,
        compiler_params=_cparams(("parallel",)), name="rwkv_pre",
    )(p_full, p_full, prev_rows, prm["mu"], prm["w0"], prm["wup"], prm["a0"], prm["aup"], prm["gup"],
      prm["k_k"], prm["k_a"], prm["r_k"])


_RW_DOT_A = _bdot
_RW_DOT_T = _bdot
_RW_DOT_W = _bdot
_RW_DOT_S = _hdot


def _rwkv_scan_kernel(L, nb, n_pairs, r_ref, lw_ref, k_ref, v_ref, al_ref, be_ref, s0_ref,
                      y_ref, sf_ref, s_scr):
    c = pl.program_id(1)
    nc = pl.num_programs(1)
    L2 = 2 * L
    lane = _iota((1, LANES), 1)
    m0 = jnp.where(lane < RWKV_HD, 1.0, 0.0)
    m1 = 1.0 - m0
    rr = _iota((L2, L2), 0)
    cc = _iota((L2, L2), 1)
    same = (rr // L) == (cc // L)
    strict = jnp.logical_and(same, (cc % L) < (rr % L))
    incl = jnp.logical_and(same, (cc % L) <= (rr % L))
    eye2 = jnp.where(rr == cc, 1.0, 0.0)
    r128 = _iota((LANES, LANES), 0)
    c128 = _iota((LANES, LANES), 1)
    blk128 = (r128 // RWKV_HD) == (c128 // RWKV_HD)
    diag128 = r128 == c128
    fmat = jnp.where(_iota((LANES, RWKV_HD), 0) % RWKV_HD == _iota((LANES, RWKV_HD), 1), 1.0, 0.0)
    tri = _tri_incl(L)

    def bd(x):
        return jnp.concatenate([x * m0, x * m1], axis=0)

    @pl.when(c == 0)
    def _():
        for bb in range(nb):
            for j in range(n_pairs):
                s0 = s0_ref[bb, j]
                st = _xdot_r(fmat, s0, _NT)
                s_scr[bb * n_pairs + j] = jnp.where(blk128, st, 0.0)

    pairs = range(nb * n_pairs)
    bbs = [i // n_pairs for i in pairs]
    sls = [slice((i % n_pairs) * LANES, (i % n_pairs + 1) * LANES) for i in pairs]
    lw = [lw_ref[bbs[j], :, sls[j]] for j in pairs]
    b = [_xdot_r(tri, x) for x in lw]
    bl = [x[L - 1:L, :] for x in b]
    e_b = [jnp.exp(x) for x in b]
    e_nb = [jnp.exp(-x) for x in b]
    e_lb = [jnp.exp(bl[j] - b[j]) for j in pairs]
    at = [bd(al_ref[bbs[j], :, sls[j]] * jnp.exp(b[j] - lw[j])) for j in pairs]
    rt = [bd(r_ref[bbs[j], :, sls[j]] * e_b[j]) for j in pairs]
    bh = [bd(be_ref[bbs[j], :, sls[j]] * e_nb[j]) for j in pairs]
    kh = [bd(k_ref[bbs[j], :, sls[j]] * e_nb[j]) for j in pairs]
    bt = [bd(be_ref[bbs[j], :, sls[j]] * e_lb[j]) for j in pairs]
    kt = [bd(k_ref[bbs[j], :, sls[j]] * e_lb[j]) for j in pairs]
    vb = [bd(v_ref[bbs[j], :, sls[j]]) for j in pairs]
    a_ab = [jnp.where(strict, _RW_DOT_A(at[j], bh[j], _NT), 0.0) for j in pairs]
    a_ak = [jnp.where(strict, _RW_DOT_A(at[j], kh[j], _NT), 0.0) for j in pairs]
    a_rb = [jnp.where(incl, _RW_DOT_A(rt[j], bh[j], _NT), 0.0) for j in pairs]
    a_rk = [jnp.where(incl, _RW_DOT_A(rt[j], kh[j], _NT), 0.0) for j in pairs]
    x = a_ab
    tinv = [eye2 + a for a in a_ab]
    span = 2
    while span < L:
        x = [_RW_DOT_T(xx, xx) for xx in x]
        tinv = [tinv[j] + _RW_DOT_T(tinv[j], x[j]) for j in pairs]
        span *= 2
    akv = [_RW_DOT_W(a_ak[j], vb[j]) for j in pairs]
    w = [_RW_DOT_W(tinv[j], at[j]) for j in pairs]
    uv = [_RW_DOT_W(tinv[j], akv[j]) for j in pairs]
    mm = [jnp.where(diag128, jnp.exp(bl[j]), 0.0) + _RW_DOT_W(bt[j], w[j], _TN) for j in pairs]
    nn = [_RW_DOT_W(bt[j], uv[j], _TN) + _RW_DOT_W(kt[j], vb[j], _TN) for j in pairs]
    q = [rt[j] + _RW_DOT_W(a_rb[j], w[j]) for j in pairs]
    yv = [_RW_DOT_W(a_rb[j], uv[j]) + _RW_DOT_W(a_rk[j], vb[j]) for j in pairs]
    s = [s_scr[j] for j in pairs]
    for j in pairs:
        ybd = _RW_DOT_S(q[j], s[j]) + yv[j]
        y_ref[bbs[j], :, sls[j]] = ybd[0:L] + ybd[L:L2]
    for j in pairs:
        s_scr[j] = _RW_DOT_S(mm[j], s[j]) + nn[j]

    @pl.when(c == nc - 1)
    def _():
        for i in pairs:
            sf_ref[bbs[i], i % n_pairs] = _xdot_l(s_scr[i], fmat, _TN)


def rwkv_scan(ops, s0, L, nb):
    bsz, t, _ = ops[0].shape
    n_pairs = RW // LANES
    blk = pl.BlockSpec((nb, L, RW), lambda b, c: (b, c, 0))
    st_spec = pl.BlockSpec((nb, n_pairs, LANES, RWKV_HD), lambda b, c: (b, 0, 0, 0))
    return pl.pallas_call(
        functools.partial(_rwkv_scan_kernel, L, nb, n_pairs),
        grid=(bsz // nb, t // L),
        in_specs=[blk] * 6 + [st_spec],
        out_specs=[blk, st_spec],
        out_shape=[jax.ShapeDtypeStruct((bsz, t, RW), F32),
                   jax.ShapeDtypeStruct((bsz, n_pairs, LANES, RWKV_HD), F32)],
        scratch_shapes=[pltpu.VMEM((nb * n_pairs, LANES, LANES), F32)],
        compiler_params=_cparams(("parallel", "arbitrary")), name="rwkv_scan",
    )(*ops, s0)


def _rwkv_post_kernel(y_ref, bo_ref, g_ref, lw_ref, lb_ref, o_ref):
    y = y_ref[...]
    mean = _seg_sum(y, RWKV_HD) * (1.0 / RWKV_HD)
    d = y - mean
    var = _seg_sum(d * d, RWKV_HD) * (1.0 / RWKV_HD)
    yn = d * lax.rsqrt(var + RWKV_GN_EPS) * lw_ref[...] + lb_ref[...]
    o_ref[...] = (yn + bo_ref[...]) * g_ref[...]


def rwkv_post(y, bonus, g, lnx_w, lnx_b, tm):
    m = y.shape[0]
    spec = pl.BlockSpec((tm, RW), lambda i: (i, 0))
    row = pl.BlockSpec((1, RW), lambda i: (0, 0))
    return pl.pallas_call(
        _rwkv_post_kernel,
        grid=(m // tm,),
        in_specs=[spec, spec, spec, row, row],
        out_specs=spec,
        out_shape=jax.ShapeDtypeStruct((m, RW), F32),
        compiler_params=_cparams(("parallel",)), name="rwkv_post",
    )(y, bonus, g, lnx_w.reshape(1, RW), lnx_b.reshape(1, RW))


def rwkv_params(w, i):
    z64 = jnp.zeros((RWKV_HD, RW), F32)
    return {
        "mu": w["e_mu"][i].reshape(1, RSHIFT),
        "w0": w["e_w0"][i].reshape(1, RW),
        "wup": jnp.concatenate([w["e_w_up"][i], z64], 0).astype(BF16),
        "a0": w["e_a0"][i].reshape(1, RW),
        "aup": jnp.concatenate([z64, w["e_a_up"][i]], 0).astype(BF16),
        "gup": w["e_g_up"][i].astype(BF16),
        "k_k": w["e_k_k"][i].reshape(1, RW),
        "k_a": w["e_k_a"][i].reshape(1, RW),
        "r_k": w["e_r_k"][i].reshape(1, RW),
    }


def rwkv_mix(p_flat, prev, s0, bsz, t, L, tm, prm, lnx_w, lnx_b, nb=1):
    m = bsz * t
    if t >= tm:
        prev_rows = prev.reshape(bsz, 1, RSHIFT)
    else:
        prev_rows = jnp.repeat(prev, t, axis=0)
    r, lw, k, v, al, be, g, bonus = rwkv_pre(p_flat, prev_rows, t, tm, prm)
    tp = -(-t // L) * L
    ops = [z.reshape(bsz, t, RW) for z in (r, lw, k, v, al, be)]
    if tp != t:
        ops = [jnp.pad(z, ((0, 0), (0, tp - t), (0, 0))) for z in ops]
    y, s_fin = rwkv_scan(ops, s0.reshape(bsz, RW // LANES, LANES, RWKV_HD), L, nb)
    y = y[:, :t].reshape(m, RW)
    ya = rwkv_post(y, bonus, g, lnx_w, lnx_b, tm)
    return ya, s_fin.reshape(bsz, RW // RWKV_HD, RWKV_HD, RWKV_HD)


N_Q_HEADS = 8
N_KV_HEADS = 2
Q_PER_KV = N_Q_HEADS // N_KV_HEADS
QW = N_Q_HEADS * ATT_HD
KW = N_KV_HEADS * ATT_HD
NEG = -1e30


def _t5_bucket_np(dist):
    n = np.maximum(dist, 0)
    max_exact = N_BUCKETS // 2
    nf = np.maximum(n, 1).astype(np.float32)
    large = max_exact + (np.log(nf / np.float32(max_exact)) / np.float32(math.log(BUCKET_MAX_DIST / max_exact))
                         * np.float32(N_BUCKETS - max_exact)).astype(np.int32)
    large = np.minimum(large, N_BUCKETS - 1)
    return np.where(n < max_exact, n, large)


def _bias_kernel(rt_ref, oh_ref, o_ref):
    o_ref[...] = _xdot_l(rt_ref[...], oh_ref[...])


def rel_bias(rel_table, dist):
    bucket = _t5_bucket_np(dist).reshape(-1)
    n = bucket.shape[0]
    onehot = jnp.asarray((np.arange(N_BUCKETS)[:, None] == bucket[None, :]).astype(np.float32), BF16)
    out = pl.pallas_call(
        _bias_kernel,
        out_shape=jax.ShapeDtypeStruct((N_Q_HEADS, n), F32),
    )(rel_table.T, onehot)
    return out.reshape((N_Q_HEADS,) + dist.shape)


def _head_norm(x, w_row):
    return x * lax.rsqrt(_seg_sum(x * x, ATT_HD) * (1.0 / ATT_HD) + EPS) * w_row


def _swa_prompt_kernel(q_ref, kvc_ref, kvp_ref, qw_ref, kw_ref, bias_ref, sink_ref, o_ref, ko_ref, vo_ref):
    i = pl.program_id(1)
    qn = _head_norm(q_ref[0], qw_ref[...])
    kvc = kvc_ref[0]
    kvp = kvp_ref[0]
    kcn = _head_norm(kvc[:, 0:KW], kw_ref[...])
    kpn = _head_norm(kvp[:, 0:KW], kw_ref[...])
    vc = kvc[:, KW:2 * KW]
    kcat = jnp.concatenate([kpn, kcn], axis=0).astype(BF16)
    vcat = jnp.concatenate([kvp[:, KW:2 * KW], vc], axis=0).astype(BF16)
    qi = _iota((WINDOW, 2 * WINDOW), 0)
    kj = _iota((WINDOW, 2 * WINDOW), 1)
    dist = qi + WINDOW - kj
    valid = jnp.logical_and(jnp.logical_and(dist >= 0, dist < WINDOW), jnp.logical_or(kj >= WINDOW, i > 0))
    lane = _iota((1, LANES), 1)
    masks = (jnp.where(lane < ATT_HD, 1.0, 0.0), jnp.where(lane < ATT_HD, 0.0, 1.0))
    heads = range(N_Q_HEADS)
    kvs = [h // Q_PER_KV for h in heads]
    qts = [qn[:, (h // 2) * LANES:(h // 2 + 1) * LANES] for h in heads]
    qts = [pltpu.roll(qts[h], ATT_HD, 1) if h % 2 != kvs[h] else qts[h] for h in heads]
    qms = [(qts[h] * masks[kvs[h]]).astype(BF16) for h in heads]
    s_all = _dg(jnp.concatenate(qms, axis=0), kcat, _NT) * (ATT_HD ** -0.5)
    logits = [jnp.where(valid, s_all[h * WINDOW:(h + 1) * WINDOW] + bias_ref[h], NEG) for h in heads]
    sinks = [sink_ref[h:h + 1, 0:1] for h in heads]
    mx = [jnp.maximum(jnp.max(logits[h], axis=-1, keepdims=True), sinks[h]) for h in heads]
    pr = [jnp.exp(logits[h] - mx[h]) for h in heads]
    den = [jnp.sum(pr[h], axis=-1, keepdims=True) + jnp.exp(sinks[h] - mx[h]) for h in heads]
    probs = jnp.concatenate([(pr[h] * (1.0 / den[h])).astype(BF16) for h in heads], axis=0)
    o_all = _dg(probs, vcat, _NN)
    os_ = [o_all[h * WINDOW:(h + 1) * WINDOW] for h in heads]
    os_ = [pltpu.roll(os_[h], ATT_HD, 1) if h % 2 != kvs[h] else os_[h] for h in heads]
    for jq in range(QW // LANES):
        o_ref[0, :, jq * LANES:(jq + 1) * LANES] = os_[2 * jq] * masks[0] + os_[2 * jq + 1] * masks[1]
    ko_ref[0] = kcn
    vo_ref[0] = vc


def swa_prompt(q, kv, bsz, t, q_norm, k_norm, bias, sinks):
    nb = t // WINDOW
    q3 = q.reshape(bsz, t, QW)
    kv3 = kv.reshape(bsz, t, 2 * KW)
    o, ko, vo = pl.pallas_call(
        _swa_prompt_kernel,
        grid=(bsz, nb),
        in_specs=[pl.BlockSpec((1, WINDOW, QW), lambda b, i: (b, i, 0)),
                  pl.BlockSpec((1, WINDOW, 2 * KW), lambda b, i: (b, i, 0)),
                  pl.BlockSpec((1, WINDOW, 2 * KW), lambda b, i: (b, jnp.maximum(i - 1, 0), 0)),
                  pl.BlockSpec((1, QW), lambda b, i: (0, 0)),
                  pl.BlockSpec((1, KW), lambda b, i: (0, 0)),
                  pl.BlockSpec((N_Q_HEADS, WINDOW, 2 * WINDOW), lambda b, i: (0, 0, 0)),
                  pl.BlockSpec((N_Q_HEADS, LANES), lambda b, i: (0, 0))],
        out_specs=[pl.BlockSpec((1, WINDOW, QW), lambda b, i: (b, i, 0)),
                   pl.BlockSpec((1, WINDOW, KW), lambda b, i: (b, 0, 0)),
                   pl.BlockSpec((1, WINDOW, KW), lambda b, i: (b, 0, 0))],
        out_shape=[jax.ShapeDtypeStruct((bsz, t, QW), F32),
                   jax.ShapeDtypeStruct((bsz, WINDOW, KW), F32),
                   jax.ShapeDtypeStruct((bsz, WINDOW, KW), F32)],
        compiler_params=_cparams(("parallel", "arbitrary")), name="swa_prompt",
    )(q3, kv3, kv3, jnp.tile(q_norm, N_Q_HEADS).reshape(1, QW), jnp.tile(k_norm, N_KV_HEADS).reshape(1, KW),
      bias, jnp.broadcast_to(sinks[:, None], (N_Q_HEADS, LANES)))
    return o.reshape(bsz * t, QW), ko, vo


DEC_TP = 8


def _swa_decode_kernel(nbt, t_real, q_ref, kv_ref, ck_ref, cv_ref, qw_ref, kw_ref, bc_ref, bn_ref, sink_ref,
                       o_ref, ko_ref, vo_ref):
    rows = N_Q_HEADS * DEC_TP
    tq = _iota((rows, WINDOW), 0) % DEC_TP
    valid_c = _iota((rows, WINDOW), 1) > tq
    jn = _iota((rows, DEC_TP), 1)
    valid_n = jnp.logical_and(jn <= _iota((rows, DEC_TP), 0) % DEC_TP, jn < t_real)
    lane = _iota((1, LANES), 1)
    masks = (jnp.where(lane < ATT_HD, 1.0, 0.0), jnp.where(lane < ATT_HD, 0.0, 1.0))
    row8 = _iota((DEC_TP, 1), 0)
    scale = ATT_HD ** -0.5
    bias_c = jnp.concatenate([bc_ref[kv] for kv in range(N_KV_HEADS)], axis=0)
    bias_n = jnp.concatenate([bn_ref[kv][:, 0:DEC_TP] for kv in range(N_KV_HEADS)], axis=0)
    sink = jnp.concatenate([sink_ref[kv][:, 0:1] for kv in range(N_KV_HEADS)], axis=0)
    bs = range(nbt)
    heads = range(N_Q_HEADS)
    qn_all = _head_norm(q_ref[...].reshape(nbt * DEC_TP, QW), qw_ref[...])
    kvn_all = kv_ref[...].reshape(nbt * DEC_TP, 2 * KW)
    knew_all = _head_norm(kvn_all[:, 0:KW], kw_ref[...])
    knew = [knew_all[b * DEC_TP:(b + 1) * DEC_TP] for b in bs]
    vnew = [kvn_all[b * DEC_TP:(b + 1) * DEC_TP, KW:2 * KW] for b in bs]
    kc = [ck_ref[b] for b in bs]
    vc = [cv_ref[b] for b in bs]

    def stack_q(b):
        pieces = []
        for h in heads:
            qt = qn_all[b * DEC_TP:(b + 1) * DEC_TP, (h // 2) * LANES:(h // 2 + 1) * LANES]
            if h % 2 != h // Q_PER_KV:
                qt = pltpu.roll(qt, ATT_HD, 1)
            pieces.append(qt * masks[h // Q_PER_KV])
        return jnp.concatenate(pieces, axis=0)

    qs = [stack_q(b) for b in bs]
    l_c = [jnp.where(valid_c, _bdot(qs[b], kc[b], _NT) * scale + bias_c, NEG) for b in bs]
    l_n = [jnp.where(valid_n, _bdot(qs[b], knew[b], _NT) * scale + bias_n, NEG) for b in bs]
    mx = [jnp.maximum(jnp.maximum(jnp.max(l_c[b], axis=-1, keepdims=True),
                                  jnp.max(l_n[b], axis=-1, keepdims=True)), sink) for b in bs]
    p_c = [jnp.exp(l_c[b] - mx[b]) for b in bs]
    p_n = [jnp.exp(l_n[b] - mx[b]) for b in bs]
    inv = [1.0 / (jnp.sum(p_c[b], axis=-1, keepdims=True) + jnp.sum(p_n[b], axis=-1, keepdims=True)
                  + jnp.exp(sink - mx[b])) for b in bs]
    o = [_bdot(p_c[b] * inv[b], vc[b]) + _bdot(p_n[b] * inv[b], vnew[b]) for b in bs]
    for b in bs:
        for jq in range(QW // LANES):
            parts = []
            for h in (2 * jq, 2 * jq + 1):
                piece = o[b][h * DEC_TP:(h + 1) * DEC_TP]
                if h % 2 != h // Q_PER_KV:
                    piece = pltpu.roll(piece, ATT_HD, 1)
                parts.append(piece * masks[h % 2])
            o_ref[b, :, jq * LANES:(jq + 1) * LANES] = parts[0] + parts[1]
    for b in bs:
        for cache, new, out in ((kc[b], knew[b], ko_ref), (vc[b], vnew[b], vo_ref)):
            shifted = pltpu.roll(cache, WINDOW - t_real, 0)
            new_r = pltpu.roll(new, DEC_TP - t_real, 0)
            out[b, 0:WINDOW - DEC_TP] = shifted[0:WINDOW - DEC_TP]
            out[b, WINDOW - DEC_TP:WINDOW] = jnp.where(row8 >= DEC_TP - t_real, new_r,
                                                       shifted[WINDOW - DEC_TP:WINDOW])


def swa_decode(q, kv, cache_k, cache_v, bsz, t, q_norm, k_norm, rel_table, sinks, nbt):
    pad = ((0, 0), (0, DEC_TP - t), (0, 0))
    q3 = jnp.pad(q.reshape(bsz, t, QW), pad)
    kv3 = jnp.pad(kv.reshape(bsz, t, 2 * KW), pad)
    kpos = np.concatenate([np.arange(WINDOW) - WINDOW, np.arange(DEC_TP)])
    dist = np.arange(DEC_TP)[:, None] - kpos[None, :]
    bias = rel_bias(rel_table, dist)
    rows = Q_PER_KV * DEC_TP
    bias = bias.reshape(N_KV_HEADS, rows, WINDOW + DEC_TP)
    bias_c = bias[:, :, :WINDOW]
    bias_n = jnp.pad(bias[:, :, WINDOW:], ((0, 0), (0, 0), (0, LANES - DEC_TP)))
    sink_rows = jnp.broadcast_to(sinks.reshape(N_KV_HEADS, Q_PER_KV, 1, 1),
                                 (N_KV_HEADS, Q_PER_KV, DEC_TP, LANES)).reshape(N_KV_HEADS, rows, LANES)
    full3 = lambda shape: pl.BlockSpec(shape, lambda i: (0, 0, 0))
    o, ko, vo = pl.pallas_call(
        functools.partial(_swa_decode_kernel, nbt, t),
        grid=(bsz // nbt,),
        in_specs=[pl.BlockSpec((nbt, DEC_TP, QW), lambda i: (i, 0, 0)),
                  pl.BlockSpec((nbt, DEC_TP, 2 * KW), lambda i: (i, 0, 0)),
                  pl.BlockSpec((nbt, WINDOW, KW), lambda i: (i, 0, 0)),
                  pl.BlockSpec((nbt, WINDOW, KW), lambda i: (i, 0, 0)),
                  pl.BlockSpec((1, QW), lambda i: (0, 0)),
                  pl.BlockSpec((1, KW), lambda i: (0, 0)),
                  full3((N_KV_HEADS, rows, WINDOW)),
                  full3((N_KV_HEADS, rows, LANES)),
                  full3((N_KV_HEADS, rows, LANES))],
        out_specs=[pl.BlockSpec((nbt, DEC_TP, QW), lambda i: (i, 0, 0)),
                   pl.BlockSpec((nbt, WINDOW, KW), lambda i: (i, 0, 0)),
                   pl.BlockSpec((nbt, WINDOW, KW), lambda i: (i, 0, 0))],
        out_shape=[jax.ShapeDtypeStruct((bsz, DEC_TP, QW), F32),
                   jax.ShapeDtypeStruct((bsz, WINDOW, KW), F32),
                   jax.ShapeDtypeStruct((bsz, WINDOW, KW), F32)],
        compiler_params=_cparams(("parallel",)), name="swa_decode",
    )(q3, kv3, cache_k, cache_v, jnp.tile(q_norm, N_Q_HEADS).reshape(1, QW),
      jnp.tile(k_norm, N_KV_HEADS).reshape(1, KW), bias_c, bias_n, sink_rows)
    return o[:, :t].reshape(bsz * t, QW), ko, vo


S5_G = 32
S5_W = S5_G * S5_P
S5_CP = S5_CHUNK * S5_P
S5_PK = 2 * S5_N
S5_HW = S5_G * S5_PK


S5_QT = S5_W // LANES
S5_GT = LANES // S5_P
S5_XW = S5_CHUNK * LANES
S5_HQ = S5_GT * S5_PK


def _s5_group_maps(t_effs, a2, ldt, b1, b2, c1, c2):
    L = S5_CHUNK
    ar2 = a2[0:1, :]
    ai2 = a2[1:2, :]
    step = jnp.exp(ldt)
    mi = _iota((3 * SUBLANES, S5_PK), 0).astype(F32)
    mag = jnp.exp(mi * (step * ar2))
    ang = mi * (step * ai2)
    pwa = mag * jnp.cos(ang)
    pwb = mag * jnp.sin(ang)
    abr = pwa[1:2]
    abi = pwb[1:2]
    den = ar2 * ar2 + ai2 * ai2
    fa = ((abr - 1.0) * ar2 + abi * ai2) / den
    fb = (abi * ar2 - (abr - 1.0) * ai2) / den
    bp1 = b1 * fa + b2 * fb
    bp2 = b2 * fa - b1 * fb
    cpow = [c1 * pwa[m:m + 1] + c2 * pwb[m:m + 1] for m in range(L + 1)]
    kern_t = _hdot(bp1, jnp.concatenate(cpow[0:L], axis=0), _NT)
    sgn = jnp.where(_iota((1, S5_PK), 1) < S5_N, -1.0, 1.0)
    kbs, als = [], []
    for t_eff in t_effs:
        kbs.append([bp1 * pwa[max(t_eff - 1 - i, 0):max(t_eff - 1 - i, 0) + 1]
                    + bp2 * pwb[max(t_eff - 1 - i, 0):max(t_eff - 1 - i, 0) + 1] for i in range(L)])
        als.append(jnp.concatenate([pwa[t_eff:t_eff + 1], sgn * pwb[t_eff:t_eff + 1]], axis=0))
    return kern_t, kbs, als, cpow[1:L + 1]


def _s5_prep_kernel(t_effs, a_ref, ldt_ref, b1_ref, b2_ref, c1_ref, c2_ref, bd_ref, kc_ref, *rest):
    L = S5_CHUNK
    n_t = len(t_effs)
    kb_refs = rest[:n_t]
    al_refs = rest[n_t:2 * n_t]
    lane = _iota((S5_P, LANES), 1)
    for kb_ref in kb_refs:
        kb_ref[0] = jnp.zeros(kb_ref.shape[1:], kb_ref.dtype)
    bd_rows = [[] for _ in range(L)]
    kct_rows = [[] for _ in range(L)]
    for g in range(S5_GT):
        kern_t, kbs, als, kct = _s5_group_maps(t_effs, a_ref[g], ldt_ref[g], b1_ref[g], b2_ref[g],
                                               c1_ref[g], c2_ref[g])
        in_group = jnp.logical_and(lane >= g * S5_P, lane < (g + 1) * S5_P)
        for tau in range(L):
            shift = (g * S5_P - tau * S5_P) % S5_CP
            moved = pltpu.roll(kern_t, shift, 1) if shift else kern_t
            bd_rows[tau].append(jnp.where(in_group, moved[:, 0:LANES], 0.0))
        for k in range(n_t):
            for i in range(L):
                kb_refs[k][0, i * LANES + g * S5_P:i * LANES + (g + 1) * S5_P, g * S5_PK:(g + 1) * S5_PK] = (
                    kbs[k][i].astype(kb_refs[k].dtype))
            al_refs[k][g] = als[k]
        zl = jnp.zeros((S5_P, g * S5_PK), F32)
        zr = jnp.zeros((S5_P, (S5_GT - 1 - g) * S5_PK), F32)
        for t in range(L):
            parts = ([zl] if g else []) + [kct[t]] + ([zr] if g < S5_GT - 1 else [])
            kct_rows[t].append(jnp.concatenate(parts, axis=1))
    for tau in range(L):
        bd_ref[0, tau] = jnp.concatenate(bd_rows[tau], axis=0).astype(bd_ref.dtype)
    for t in range(L):
        blk_t = jnp.concatenate(kct_rows[t], axis=0)
        kc_ref[0, :, t * LANES:(t + 1) * LANES] = blk_t.T.astype(kc_ref.dtype)


def s5_prep(w, i, t_effs):
    dup = lambda z: jnp.concatenate([z, z], axis=-1)
    a = jnp.stack([dup(w["o_a_re"][i]), dup(w["o_a_im"][i])], axis=1)
    ldt = jnp.broadcast_to(w["o_log_dt"][i][:, None, None], (S5_G, 1, S5_PK))
    bt_re = jnp.swapaxes(w["o_b_re"][i], 1, 2)
    bt_im = jnp.swapaxes(w["o_b_im"][i], 1, 2)
    b1 = jnp.concatenate([bt_re, bt_im], -1)
    b2 = jnp.concatenate([-bt_im, bt_re], -1)
    c_re, c_im = w["o_c_re"][i], w["o_c_im"][i]
    c1 = jnp.concatenate([c_re, -c_im], -1)
    c2 = jnp.concatenate([-c_im, -c_re], -1)
    n_t = len(t_effs)
    g3 = lambda r, c: pl.BlockSpec((S5_GT, r, c), lambda q: (q, 0, 0))
    outs = pl.pallas_call(
        functools.partial(_s5_prep_kernel, tuple(t_effs)),
        grid=(S5_QT,),
        in_specs=[g3(2, S5_PK), g3(1, S5_PK), g3(S5_P, S5_PK), g3(S5_P, S5_PK), g3(S5_P, S5_PK), g3(S5_P, S5_PK)],
        out_specs=[pl.BlockSpec((1, S5_CHUNK, LANES, LANES), lambda q: (q, 0, 0, 0)),
                   pl.BlockSpec((1, S5_HQ, S5_XW), lambda q: (q, 0, 0))]
                  + [pl.BlockSpec((1, S5_XW, S5_HQ), lambda q: (q, 0, 0))] * n_t
                  + [g3(2, S5_PK)] * n_t,
        out_shape=[jax.ShapeDtypeStruct((S5_QT, S5_CHUNK, LANES, LANES), BF16),
                   jax.ShapeDtypeStruct((S5_QT, S5_HQ, S5_XW), BF16)]
                  + [jax.ShapeDtypeStruct((S5_QT, S5_XW, S5_HQ), BF16)] * n_t
                  + [jax.ShapeDtypeStruct((S5_G, 2, S5_PK), F32)] * n_t,
        compiler_params=_cparams(("parallel",)), name="s5_prep",
    )(a, ldt, b1, b2, c1, c2)
    bd, kc = outs[0], outs[1]
    mats = []
    for k in range(n_t):
        al = outs[2 + n_t + k]
        mats.append((bd, outs[2 + k], kc, al[:, 0, :].reshape(1, S5_HW), al[:, 1, :].reshape(1, S5_HW)))
    return mats


def _s5_e_kernel(u_ref, kb_ref, e_ref):
    e_ref[...] = _bdot(u_ref[0], kb_ref[0])


def _s5_swap(h):
    n = h.shape[-1]
    lane = _iota(h.shape, 1)
    return jnp.where(lane % S5_PK < S5_N, pltpu.roll(h, n - S5_N, 1), pltpu.roll(h, S5_N, 1))


def _s5_scan_kernel(bsz, cg, e_ref, h0_ref, ala_ref, alb_ref, hp_ref, hf_ref, h_scr):
    @pl.when(pl.program_id(0) == 0)
    def _():
        h_scr[...] = h0_ref[...]

    ala = ala_ref[...]
    alb = alb_ref[...]

    def body(c, hs):
        out = []
        for b in range(bsz):
            hp_ref[b, pl.ds(c, 1), :] = hs[b]
            out.append(ala * hs[b] + alb * _s5_swap(hs[b]) + e_ref[b, pl.ds(c, 1), :])
        return tuple(out)

    hs = lax.fori_loop(0, cg, body, tuple(h_scr[b:b + 1, :] for b in range(bsz)))
    for b in range(bsz):
        h_scr[b:b + 1, :] = hs[b]
        hf_ref[b:b + 1, :] = hs[b]


def _s5_step_kernel(e_ref, h0_ref, ala_ref, alb_ref, hf_ref):
    h = h0_ref[...]
    hf_ref[...] = ala_ref[...] * h + alb_ref[...] * _s5_swap(h) + e_ref[...]


def _s5_y_kernel(u_ref, hp_ref, bd_ref, kc_ref, y_ref, k_scr):
    @pl.when(pl.program_id(1) == 0)
    def _():
        zero = jnp.zeros((LANES, LANES), k_scr.dtype)
        for i in range(S5_CHUNK):
            for t in range(S5_CHUNK):
                k_scr[i * LANES:(i + 1) * LANES, t * LANES:(t + 1) * LANES] = bd_ref[0, t - i] if t >= i else zero

    y_ref[0] = _bdot(u_ref[0], k_scr[...]) + _bdot(hp_ref[...], kc_ref[0])


def s5_core(u4, h0, bsz, t, mats):
    bd, kbbig, kcbig, ala, alb = mats
    L = S5_CHUNK
    tp = -(-t // L) * L
    x = u4.reshape(S5_QT, bsz, t, LANES)
    if tp != t:
        x = jnp.pad(x, ((0, 0), (0, 0), (0, tp - t), (0, 0)))
    nc = tp // L
    rows = nc * bsz
    x = x.reshape(S5_QT, rows, S5_XW)
    tr = math.gcd(rows, 512)
    e = pl.pallas_call(
        _s5_e_kernel,
        grid=(S5_QT, rows // tr),
        in_specs=[pl.BlockSpec((1, tr, S5_XW), lambda q, r: (q, r, 0)),
                  pl.BlockSpec((1, S5_XW, S5_HQ), lambda q, r: (q, 0, 0))],
        out_specs=pl.BlockSpec((tr, S5_HQ), lambda q, r: (r, q)),
        out_shape=jax.ShapeDtypeStruct((rows, S5_HW), F32),
        compiler_params=_cparams(("parallel", "parallel")), name="s5_e",
    )(x, kbbig)
    row = pl.BlockSpec((1, S5_HW), lambda i: (0, 0))
    if nc == 1:
        hp = h0
        hf = pl.pallas_call(
            _s5_step_kernel,
            out_shape=jax.ShapeDtypeStruct((bsz, S5_HW), F32), name="s5_step",
        )(e, h0, ala, alb)
    else:
        cg = math.gcd(nc, 64)
        hp, hf = pl.pallas_call(
            functools.partial(_s5_scan_kernel, bsz, cg),
            grid=(nc // cg,),
            in_specs=[pl.BlockSpec((bsz, cg, S5_HW), lambda i: (0, i, 0)),
                      pl.BlockSpec((bsz, S5_HW), lambda i: (0, 0)), row, row],
            out_specs=[pl.BlockSpec((bsz, cg, S5_HW), lambda i: (0, i, 0)),
                       pl.BlockSpec((bsz, S5_HW), lambda i: (0, 0))],
            out_shape=[jax.ShapeDtypeStruct((bsz, nc, S5_HW), F32),
                       jax.ShapeDtypeStruct((bsz, S5_HW), F32)],
            scratch_shapes=[pltpu.VMEM((bsz, S5_HW), F32)],
            compiler_params=_cparams(("arbitrary",)), name="s5_scan",
        )(e.reshape(bsz, nc, S5_HW), h0, ala, alb)
    y = pl.pallas_call(
        _s5_y_kernel,
        grid=(S5_QT, rows // tr),
        in_specs=[pl.BlockSpec((1, tr, S5_XW), lambda q, r: (q, r, 0)),
                  pl.BlockSpec((tr, S5_HQ), lambda q, r: (r, q)),
                  pl.BlockSpec((1, S5_CHUNK, LANES, LANES), lambda q, r: (q, 0, 0, 0)),
                  pl.BlockSpec((1, S5_HQ, S5_XW), lambda q, r: (q, 0, 0))],
        out_specs=pl.BlockSpec((1, tr, S5_XW), lambda q, r: (q, r, 0)),
        out_shape=jax.ShapeDtypeStruct((S5_QT, rows, S5_XW), F32),
        scratch_shapes=[pltpu.VMEM((S5_XW, S5_XW), BF16)],
        compiler_params=_cparams(("parallel", "arbitrary")), name="s5_y",
    )(x, hp.reshape(rows, S5_HW), bd, kcbig)
    y = y.reshape(S5_QT, bsz, tp, LANES)[:, :, :t].reshape(S5_QT, bsz * t, LANES)
    return y, hf


def _s5_post_kernel(y_ref, u_ref, d_ref, gw_ref, gb_ref, o_ref):
    y = jnp.concatenate([y_ref[q] for q in range(S5_QT)], axis=1)
    u = jnp.concatenate([u_ref[q] for q in range(S5_QT)], axis=1)
    x = y + d_ref[...] * u
    z = 0.5 * x * (1.0 + jnp.tanh(math.sqrt(2.0 / math.pi) * (x + 0.044715 * (x * x * x))))
    o_ref[...] = z * _sigmoid(_dg(z.astype(BF16), gw_ref[...], _NN) + gb_ref[...])


def s5_post(y4, u4, d, glu_w_bf16, glu_b, tm):
    m = y4.shape[1]
    spec4 = pl.BlockSpec((S5_QT, tm, LANES), lambda i: (0, i, 0))
    row = pl.BlockSpec((1, S5_W), lambda i: (0, 0))
    return pl.pallas_call(
        _s5_post_kernel,
        grid=(m // tm,),
        in_specs=[spec4, spec4, row, pl.BlockSpec((S5_W, S5_W), lambda i: (0, 0)), row],
        out_specs=pl.BlockSpec((tm, S5_W), lambda i: (i, 0)),
        out_shape=jax.ShapeDtypeStruct((m, S5_W), F32),
        compiler_params=_cparams(("parallel",)), name="s5_post",
    )(y4, u4, d.reshape(1, S5_W), glu_w_bf16, glu_b.reshape(1, S5_W))


def s5_mix(u4, h_re, h_im, bsz, t, mats, d, glu_w_bf16, glu_b, tm):
    h0 = jnp.concatenate([h_re, h_im], axis=-1).reshape(bsz, S5_HW)
    y4, hf = s5_core(u4, h0, bsz, t, mats)
    out = s5_post(y4, u4, d, glu_w_bf16, glu_b, tm)
    hf = hf.reshape(bsz, S5_G, 2, S5_N)
    return out, hf[:, :, 0], hf[:, :, 1]


GLA_DK = 64
GLA_DV = 128
GLA_HEADS = 4
GLA_KW = GLA_HEADS * GLA_DK
GLA_VW = GLA_HEADS * GLA_DV
GLA_PW = 2 * GLA_KW + 2 * GLA_VW + LANES


def _gla_kernel(L, t_real, nb, p_ref, aup_ref, ab_ref, nw_ref, s0_ref, y_ref, sf_ref, s_scr):
    c = pl.program_id(1)
    nc = pl.num_programs(1)
    n_pairs = GLA_KW // LANES

    @pl.when(c == 0)
    def _():
        for bb in range(nb):
            for j in range(n_pairs):
                s_scr[bb * n_pairs + j] = s0_ref[bb, j]

    lane = _iota((1, LANES), 1)
    masks = (jnp.where(lane < GLA_DK, 1.0, 0.0), jnp.where(lane < GLA_DK, 0.0, 1.0))
    incl = _iota((L, L), 1) <= _iota((L, L), 0)
    diag128 = _iota((LANES, LANES), 0) == _iota((LANES, LANES), 1)
    ones128 = jnp.ones((LANES, LANES), BF16)
    tri = _tri_incl(L)
    rows = range(nb)
    ps = [p_ref[bb] for bb in rows]
    zs = [_bdot(p[:, 2 * GLA_KW + 2 * GLA_VW:], aup_ref[...]) + ab_ref[...] for p in ps]
    gk = [-_softplus(-z) * (1.0 / GLA_GATE_NORM) for z in zs]
    if t_real % L != 0:
        tok = c * L + _iota((L, 1), 0)
        gk = [jnp.where(tok < t_real, x, 0.0) for x in gk]
    b = [_xdot_r(tri, x) for x in gk]
    bl = [x[L - 1:L, :] for x in b]
    qd = [ps[i][:, 0:GLA_KW] * (GLA_DK ** -0.5) * jnp.exp(b[i]) for i in rows]
    kh = [ps[i][:, GLA_KW:2 * GLA_KW] * jnp.exp(-b[i]) for i in rows]
    kt = [ps[i][:, GLA_KW:2 * GLA_KW] * jnp.exp(bl[i] - b[i]) for i in rows]
    heads = [(bb, h) for bb in rows for h in range(GLA_HEADS)]
    sl = lambda h: slice((h // 2) * LANES, (h // 2 + 1) * LANES)
    hs = lambda h: slice(2 * GLA_KW + h * GLA_DV, 2 * GLA_KW + (h + 1) * GLA_DV)
    gs = lambda h: slice(2 * GLA_KW + GLA_VW + h * GLA_DV, 2 * GLA_KW + GLA_VW + (h + 1) * GLA_DV)
    st = [s_scr[bb * n_pairs + j] for bb in rows for j in range(n_pairs)]
    qm = [qd[bb][:, sl(h)] * masks[h % 2] for bb, h in heads]
    vh = [ps[bb][:, hs(h)] for bb, h in heads]
    attn = [jnp.where(incl, _bdot(qm[i], kh[bb][:, sl(h)], _NT), 0.0) for i, (bb, h) in enumerate(heads)]
    o = [_bdot(attn[i], vh[i]) + _bdot(qm[i], st[bb * n_pairs + h // 2]) for i, (bb, h) in enumerate(heads)]
    kv = [_bdot(kt[bb][:, sl(h)], vh[i], _TN) for i, (bb, h) in enumerate(heads)]
    for i, (bb, h) in enumerate(heads):
        of = o[i] * lax.rsqrt(jnp.mean(o[i] * o[i], axis=-1, keepdims=True) + EPS) * nw_ref[...]
        y_ref[bb, :, h * GLA_DV:(h + 1) * GLA_DV] = of * _silu(ps[bb][:, gs(h)])
    for bb in rows:
        for j in range(n_pairs):
            i0 = bb * GLA_HEADS + 2 * j
            pcol = _xdot_l(jnp.where(diag128, jnp.exp(bl[bb][:, j * LANES:(j + 1) * LANES]), 0.0), ones128)
            s_scr[bb * n_pairs + j] = pcol * st[bb * n_pairs + j] + jnp.concatenate(
                [kv[i0][0:GLA_DK], kv[i0 + 1][GLA_DK:2 * GLA_DK]], axis=0)

    @pl.when(c == nc - 1)
    def _():
        for bb in range(nb):
            for j in range(n_pairs):
                sf_ref[bb, j] = s_scr[bb * n_pairs + j]


def gla_mix(p_gla, s0, bsz, t, L, aup_pad, a_b, norm_w, nb=1):
    tp = -(-t // L) * L
    p3 = p_gla.reshape(bsz, t, GLA_PW)
    if tp != t:
        p3 = jnp.pad(p3, ((0, 0), (0, tp - t), (0, 0)))
    n_pairs = GLA_KW // LANES
    st_spec = pl.BlockSpec((nb, n_pairs, LANES, LANES), lambda b, c: (b, 0, 0, 0))
    y, s_fin = pl.pallas_call(
        functools.partial(_gla_kernel, L, t, nb),
        grid=(bsz // nb, tp // L),
        in_specs=[pl.BlockSpec((nb, L, GLA_PW), lambda b, c: (b, c, 0)),
                  pl.BlockSpec((LANES, GLA_KW), lambda b, c: (0, 0)),
                  pl.BlockSpec((1, GLA_KW), lambda b, c: (0, 0)),
                  pl.BlockSpec((1, GLA_DV), lambda b, c: (0, 0)),
                  st_spec],
        out_specs=[pl.BlockSpec((nb, L, GLA_VW), lambda b, c: (b, c, 0)), st_spec],
        out_shape=[jax.ShapeDtypeStruct((bsz, tp, GLA_VW), F32),
                   jax.ShapeDtypeStruct((bsz, n_pairs, LANES, LANES), F32)],
        scratch_shapes=[pltpu.VMEM((nb * n_pairs, LANES, LANES), F32)],
        compiler_params=_cparams(("parallel", "arbitrary")), name="gla",
    )(p3, aup_pad, a_b.reshape(1, GLA_KW), norm_w.reshape(1, GLA_DV),
      s0.reshape(bsz, n_pairs, LANES, LANES))
    return y[:, :t].reshape(bsz * t, GLA_VW), s_fin.reshape(bsz, GLA_HEADS, GLA_DK, GLA_DV)


GLA_LR = 16
D_FF_CHUNK = 1408
D_FF_EXPERT_CHUNK = 896
ROUTER_TOKENS = 1024
MOE_TOKENS = 2048


def _prepare_weights(w):
    win = w["e_w_in"][0]
    w_gla = jnp.pad(win[:, RSHIFT:], ((0, 0), (0, LANES - GLA_LR)))
    wo = w["o_w_in"][0]
    return {
        "e_w_rwkv": win[:, :RSHIFT].astype(BF16),
        "e_w_gla": w_gla.astype(BF16),
        "rwkv": rwkv_params(w, 0),
        "gla_aup": jnp.pad(w["e_gla_a_up"][0], ((0, LANES - GLA_LR), (0, 0))).astype(BF16),
        "e_wo_a": w["e_w_out"][0][:RW].astype(BF16),
        "e_wo_b": w["e_w_out"][0][RW:].astype(BF16),
        "ff_w1": w["e_ff_w1"][0].astype(BF16),
        "ff_w3": w["e_ff_w3"][0].astype(BF16),
        "ff_w2": w["e_ff_w2"][0].astype(BF16),
        "o_w_q": wo[:, :QW].astype(BF16),
        "o_w_kv": wo[:, QW:QW + 2 * KW].astype(BF16),
        "o_w_u": wo[:, QW + 2 * KW:].astype(BF16),
        "glu_w": w["o_glu_w"][0].astype(BF16),
        "o_wo_a": w["o_w_out"][0][:QW].astype(BF16),
        "o_wo_b": w["o_w_out"][0][QW:].astype(BF16),
        "moe_w1": w["o_moe_w1"][0].astype(BF16),
        "moe_w3": w["o_moe_w3"][0].astype(BF16),
        "moe_w2": w["o_moe_w2"][0].astype(BF16),
    }


def _trunk(x3, st, w, pw, tm, chunk, nb, s5_prep_t, prompt_bias):
    bsz, t, d = x3.shape
    m = bsz * t
    x = x3.reshape(m, d)
    p_r, p_g = norm_proj(x, w["e_norm1"][0], [pw["e_w_rwkv"], pw["e_w_gla"]], tm)
    ya, s_rwkv = rwkv_mix(p_r, st["shift"], st["rwkv"], bsz, t, chunk, tm, pw["rwkv"],
                          w["e_lnx_w"][0], w["e_lnx_b"][0], nb)
    s_shift = p_r.reshape(bsz, t, RSHIFT)[:, -1]
    yb, s_gla = gla_mix(p_g, st["gla"], bsz, t, chunk, pw["gla_aup"], w["e_gla_a_b"][0], w["e_gla_norm"][0], nb)
    x = out_proj(x, ya, yb, pw["e_wo_a"], pw["e_wo_b"], tm)
    x = ffn(x, w["e_norm2"][0], pw["ff_w1"], pw["ff_w3"], pw["ff_w2"], tm, D_FF_CHUNK)
    q, kv, u = norm_proj_tiles(x, w["o_norm1"][0], [pw["o_w_q"], pw["o_w_kv"], pw["o_w_u"]], tm)
    if st["win_k"] is None:
        yc, nk, nv = swa_prompt(q, kv, bsz, t, w["o_q_norm"][0], w["o_k_norm"][0], prompt_bias, w["o_sinks"][0])
    else:
        yc, nk, nv = swa_decode(q, kv, st["win_k"].reshape(bsz, WINDOW, KW), st["win_v"].reshape(bsz, WINDOW, KW),
                                bsz, t, w["o_q_norm"][0], w["o_k_norm"][0], w["rel_table"], w["o_sinks"][0], 8)
    yd, s5r, s5i = s5_mix(u, st["s5_re"], st["s5_im"], bsz, t, s5_prep_t, w["o_d"][0], pw["glu_w"],
                          w["o_glu_b"][0], tm)
    x = out_proj(x, yc, yd, pw["o_wo_a"], pw["o_wo_b"], tm)
    t_router = min(m, ROUTER_TOKENS)
    hn, gates, pos, counts = router(x, w["o_norm2"][0], w["o_router_w"][0], w["o_router_b"][0], t_router)
    x = moe(x, hn, gates, pos, counts, pw["moe_w1"], pw["moe_w3"], pw["moe_w2"], t_router, min(m, MOE_TOKENS),
            D_FF_EXPERT_CHUNK)
    kv_shape = (bsz, WINDOW, N_KV_HEADS, ATT_HD)
    return (x.reshape(bsz, t, d), s_rwkv[None], s_shift[None], s_gla[None], nk.reshape(kv_shape)[None],
            nv.reshape(kv_shape)[None], s5r[None], s5i[None])


def kernel(x_prompt, x_sample, state_rwkv, state_shift, state_gla, cache_win_k, cache_win_v, state_s5_re,
           state_s5_im, rel_table, e_norm1, e_w_in, e_mu, e_w0, e_w_up, e_a0, e_a_up, e_g_up, e_k_k, e_k_a, e_r_k,
           e_lnx_w, e_lnx_b, e_gla_a_up, e_gla_a_b, e_gla_norm, e_w_out, e_norm2, e_ff_w1, e_ff_w3, e_ff_w2,
           o_norm1, o_w_in, o_q_norm, o_k_norm, o_sinks, o_a_re, o_a_im, o_log_dt, o_b_re, o_b_im, o_c_re, o_c_im,
           o_d, o_glu_w, o_glu_b, o_w_out, o_norm2, o_router_w, o_router_b, o_moe_w1, o_moe_w3, o_moe_w2):
    w = dict(rel_table=rel_table, e_norm1=e_norm1, e_w_in=e_w_in, e_mu=e_mu, e_w0=e_w0, e_w_up=e_w_up, e_a0=e_a0,
             e_a_up=e_a_up, e_g_up=e_g_up, e_k_k=e_k_k, e_k_a=e_k_a, e_r_k=e_r_k, e_lnx_w=e_lnx_w, e_lnx_b=e_lnx_b,
             e_gla_a_up=e_gla_a_up, e_gla_a_b=e_gla_a_b, e_gla_norm=e_gla_norm, e_w_out=e_w_out, e_norm2=e_norm2,
             e_ff_w1=e_ff_w1, e_ff_w3=e_ff_w3, e_ff_w2=e_ff_w2, o_norm1=o_norm1, o_w_in=o_w_in, o_q_norm=o_q_norm,
             o_k_norm=o_k_norm, o_sinks=o_sinks, o_a_re=o_a_re, o_a_im=o_a_im, o_log_dt=o_log_dt, o_b_re=o_b_re,
             o_b_im=o_b_im, o_c_re=o_c_re, o_c_im=o_c_im, o_d=o_d, o_glu_w=o_glu_w, o_glu_b=o_glu_b,
             o_w_out=o_w_out, o_norm2=o_norm2, o_router_w=o_router_w, o_router_b=o_router_b, o_moe_w1=o_moe_w1,
             o_moe_w3=o_moe_w3, o_moe_w2=o_moe_w2)
    pw = _prepare_weights(w)
    bp, tp, _ = x_prompt.shape
    bs, ts, _ = x_sample.shape
    qi = np.arange(WINDOW)[:, None]
    kj = np.arange(2 * WINDOW)[None, :]
    prompt_bias = rel_bias(rel_table, qi + WINDOW - kj)
    zeros = lambda *shape: jnp.zeros(shape, F32)
    st_p = {"rwkv": zeros(bp, RW // RWKV_HD, RWKV_HD, RWKV_HD), "shift": zeros(bp, RSHIFT),
            "gla": zeros(bp, GLA_HEADS, GLA_DK, GLA_DV), "win_k": None, "win_v": None,
            "s5_re": zeros(bp, S5_G, S5_N), "s5_im": zeros(bp, S5_G, S5_N)}
    st_s = {"rwkv": state_rwkv[0], "shift": state_shift[0], "gla": state_gla[0], "win_k": cache_win_k[0],
            "win_v": cache_win_v[0], "s5_re": state_s5_re[0], "s5_im": state_s5_im[0]}
    s5_p, s5_s = s5_prep(w, 0, (S5_CHUNK, ts))
    out_p = _trunk(x_prompt, st_p, w, pw, 512, 64, math.gcd(bp, 2), s5_p, prompt_bias)
    out_s = _trunk(x_sample, st_s, w, pw, bs * ts, 8, math.gcd(bs, 8), s5_s, None)
    res = [out_p[0], out_s[0]]
    for a, b in zip(out_p[1:], out_s[1:]):
        res += [a, b]
    return tuple(res)
```

```python
import functools
import math

import jax
import jax.numpy as jnp
import numpy as np
from jax import lax
from jax.experimental import pallas as pl
from jax.experimental.pallas import tpu as pltpu

F32 = jnp.float32
BF16 = jnp.bfloat16

LANES = 128
SUBLANES = 8
VMEM_LIMIT_BYTES = 56 * 1024 * 1024

EPS = 1e-6
RWKV_HD = 64
RWKV_GN_EPS = 64e-5
GLA_GATE_NORM = 16.0
ATT_HD = 64
WINDOW = 128
N_BUCKETS = 32
BUCKET_MAX_DIST = 128
S5_P = 16
S5_N = 64
S5_CHUNK = 16
TOP_K = 2


def _cparams(sem):
    return pltpu.CompilerParams(dimension_semantics=sem, vmem_limit_bytes=VMEM_LIMIT_BYTES)


_NN = (((1,), (0,)), ((), ()))
_NT = (((1,), (1,)), ((), ()))
_TN = (((0,), (0,)), ((), ()))


def _dg(a, b, dims):
    return lax.dot_general(a, b, dims, preferred_element_type=F32)


def _bdot(a, b, dims=_NN):
    return _dg(a.astype(BF16), b.astype(BF16), dims)


def _split(a, n):
    terms = []
    r = a
    for _ in range(n):
        t = r.astype(BF16)
        terms.append(t)
        r = r - t.astype(F32)
    return terms


def _hdot(a, b, dims=_NN):
    a0, a1 = _split(a, 2)
    b0, b1 = _split(b, 2)
    return _dg(a0, b0, dims) + (_dg(a0, b1, dims) + _dg(a1, b0, dims))


def _xdot_l(a, e, dims=_NN):
    e = e.astype(BF16)
    a0, a1, a2 = _split(a, 3)
    return _dg(a0, e, dims) + (_dg(a1, e, dims) + _dg(a2, e, dims))


def _xdot_r(e, b, dims=_NN):
    e = e.astype(BF16)
    b0, b1, b2 = _split(b, 3)
    return _dg(e, b0, dims) + (_dg(e, b1, dims) + _dg(e, b2, dims))


def _iota(shape, axis):
    return lax.broadcasted_iota(jnp.int32, shape, axis)


def _seg_ones(n, seg):
    r = _iota((n, n), 0) // seg
    c = _iota((n, n), 1) // seg
    return jnp.where(r == c, 1.0, 0.0).astype(BF16)


def _seg_sum(x, seg):
    n = x.shape[-1]
    return _xdot_l(x, _seg_ones(n, seg))


def _sigmoid(x):
    return 1.0 / (1.0 + jnp.exp(-x))


def _silu(x):
    return x * _sigmoid(x)


def _softplus(x):
    return jnp.maximum(x, 0.0) + jnp.log(1.0 + jnp.exp(-jnp.abs(x)))


def _tri_incl(n):
    r = _iota((n, n), 0)
    c = _iota((n, n), 1)
    return jnp.where(c <= r, 1.0, 0.0).astype(BF16)


def _rms(x, g):
    return x * lax.rsqrt(jnp.mean(x * x, axis=-1, keepdims=True) + EPS) * g


def _norm_proj_kernel(n_w, x_ref, g_ref, *refs):
    xn = _rms(x_ref[...], g_ref[...]).astype(BF16)
    for w_ref, o_ref in zip(refs[:n_w], refs[n_w:]):
        o_ref[...] = _dg(xn, w_ref[...], _NN)


def norm_proj(x, g, ws_bf16, tm):
    m, d = x.shape
    return pl.pallas_call(
        functools.partial(_norm_proj_kernel, len(ws_bf16)),
        grid=(m // tm,),
        in_specs=[pl.BlockSpec((tm, d), lambda i: (i, 0)),
                  pl.BlockSpec((1, d), lambda i: (0, 0))]
                 + [pl.BlockSpec(w.shape, lambda i: (0, 0)) for w in ws_bf16],
        out_specs=[pl.BlockSpec((tm, w.shape[1]), lambda i: (i, 0)) for w in ws_bf16],
        out_shape=[jax.ShapeDtypeStruct((m, w.shape[1]), F32) for w in ws_bf16],
        compiler_params=_cparams(("parallel",)), name="norm_proj",
    )(x, g.reshape(1, d), *ws_bf16)


def _norm_proj_tiles_kernel(n_w, x_ref, g_ref, *refs):
    xn = _rms(x_ref[...], g_ref[...]).astype(BF16)
    for w_ref, o_ref in zip(refs[:n_w - 1], refs[n_w:]):
        o_ref[...] = _dg(xn, w_ref[...], _NN)
    u = _dg(xn, refs[n_w - 1][...], _NN)
    u_ref = refs[-1]
    for q in range(u_ref.shape[0]):
        u_ref[q] = u[:, q * LANES:(q + 1) * LANES]


def norm_proj_tiles(x, g, ws_bf16, tm):
    m, d = x.shape
    nt = ws_bf16[-1].shape[1] // LANES
    return pl.pallas_call(
        functools.partial(_norm_proj_tiles_kernel, len(ws_bf16)),
        grid=(m // tm,),
        in_specs=[pl.BlockSpec((tm, d), lambda i: (i, 0)),
                  pl.BlockSpec((1, d), lambda i: (0, 0))]
                 + [pl.BlockSpec(w.shape, lambda i: (0, 0)) for w in ws_bf16],
        out_specs=[pl.BlockSpec((tm, w.shape[1]), lambda i: (i, 0)) for w in ws_bf16[:-1]]
                  + [pl.BlockSpec((nt, tm, LANES), lambda i: (0, i, 0))],
        out_shape=[jax.ShapeDtypeStruct((m, w.shape[1]), F32) for w in ws_bf16[:-1]]
                  + [jax.ShapeDtypeStruct((nt, m, LANES), F32)],
        compiler_params=_cparams(("parallel",)), name="norm_proj_tiles",
    )(x, g.reshape(1, d), *ws_bf16)


def _out_proj_kernel(x_ref, ya_ref, yb_ref, wa_ref, wb_ref, o_ref):
    o_ref[...] = (x_ref[...] + _dg(ya_ref[...].astype(BF16), wa_ref[...], _NN)
                  + _dg(yb_ref[...].astype(BF16), wb_ref[...], _NN))


def out_proj(x, ya, yb, wa, wb, tm):
    m, d = x.shape
    return pl.pallas_call(
        _out_proj_kernel,
        grid=(m // tm,),
        in_specs=[pl.BlockSpec((tm, d), lambda i: (i, 0)),
                  pl.BlockSpec((tm, ya.shape[1]), lambda i: (i, 0)),
                  pl.BlockSpec((tm, yb.shape[1]), lambda i: (i, 0)),
                  pl.BlockSpec(wa.shape, lambda i: (0, 0)),
                  pl.BlockSpec(wb.shape, lambda i: (0, 0))],
        out_specs=pl.BlockSpec((tm, d), lambda i: (i, 0)),
        out_shape=jax.ShapeDtypeStruct((m, d), F32),
        compiler_params=_cparams(("parallel",)), name="out_proj",
    )(x, ya, yb, wa, wb)


def _ffn_kernel(x_ref, g_ref, w1_ref, w3_ref, w2_ref, o_ref, xn_scr):
    j = pl.program_id(1)

    @pl.when(j == 0)
    def _():
        x = x_ref[...]
        xn_scr[...] = _rms(x, g_ref[...]).astype(BF16)
        o_ref[...] = x

    xn = xn_scr[...]
    h = _silu(_dg(xn, w1_ref[...], _NN)) * _dg(xn, w3_ref[...], _NN)
    o_ref[...] += _dg(h.astype(BF16), w2_ref[...], _NN)


def ffn(x, g, w1, w3, w2, tm, fc):
    m, d = x.shape
    dff = w1.shape[1]
    return pl.pallas_call(
        _ffn_kernel,
        grid=(m // tm, dff // fc),
        in_specs=[pl.BlockSpec((tm, d), lambda i, j: (i, 0)),
                  pl.BlockSpec((1, d), lambda i, j: (0, 0)),
                  pl.BlockSpec((d, fc), lambda i, j: (0, j)),
                  pl.BlockSpec((d, fc), lambda i, j: (0, j)),
                  pl.BlockSpec((fc, d), lambda i, j: (j, 0))],
        out_specs=pl.BlockSpec((tm, d), lambda i, j: (i, 0)),
        out_shape=jax.ShapeDtypeStruct((m, d), F32),
        scratch_shapes=[pltpu.VMEM((tm, d), BF16)],
        compiler_params=_cparams(("parallel", "arbitrary")), name="ffn",
    )(x, g.reshape(1, d), w1, w3, w2)


def _router_kernel(n_exp, x_ref, g_ref, rw_ref, rb_ref, hn_ref, gate_ref, pos_ref, cnt_ref):
    xn = _rms(x_ref[...], g_ref[...])
    hn_ref[...] = xn.astype(BF16)
    logits = _hdot(xn, rw_ref[...]) + rb_ref[...]
    lane = _iota(logits.shape, 1)
    logits = jnp.where(lane < n_exp, logits, -jnp.inf)
    m1 = jnp.max(logits, axis=-1, keepdims=True)
    i1 = jnp.min(jnp.where(logits == m1, lane, LANES), axis=-1, keepdims=True)
    rest = jnp.where(lane == i1, -jnp.inf, logits)
    m2 = jnp.max(rest, axis=-1, keepdims=True)
    i2 = jnp.min(jnp.where(rest == m2, lane, LANES), axis=-1, keepdims=True)
    e2 = jnp.exp(m2 - m1)
    g1 = 1.0 / (1.0 + e2)
    g2 = e2 / (1.0 + e2)
    pick1 = lane == i1
    pick2 = lane == i2
    gate_ref[...] = jnp.where(pick1, g1, 0.0) + jnp.where(pick2, g2, 0.0)
    tm = logits.shape[0]
    sel = jnp.where(jnp.logical_or(pick1, pick2), 1.0, 0.0).astype(BF16)
    tr = _iota((tm, tm), 0)
    tc = _iota((tm, tm), 1)
    upper = jnp.where(tr <= tc, 1.0, 0.0).astype(BF16)
    eye = jnp.where(tr == tc, 1.0, 0.0).astype(BF16)
    rank_t = _dg(sel, upper, _TN)
    sel_t = _dg(sel, eye, _TN)
    pos_t = jnp.where(sel_t > 0.5, rank_t - 1.0, -1.0)
    pos_ref[...] = pos_t[0:SUBLANES, :]
    cnt_ref[0] = jnp.sum(sel.astype(F32), axis=0, keepdims=True)


def router(x, g, rw, rb, tm):
    m, d = x.shape
    n_exp = rw.shape[1]
    assert n_exp <= SUBLANES
    rw_pad = jnp.pad(rw, ((0, 0), (0, LANES - n_exp)))
    rb_pad = jnp.pad(rb, (0, LANES - n_exp)).reshape(1, LANES)
    return pl.pallas_call(
        functools.partial(_router_kernel, n_exp),
        grid=(m // tm,),
        in_specs=[pl.BlockSpec((tm, d), lambda i: (i, 0)),
                  pl.BlockSpec((1, d), lambda i: (0, 0)),
                  pl.BlockSpec((d, LANES), lambda i: (0, 0)),
                  pl.BlockSpec((1, LANES), lambda i: (0, 0))],
        out_specs=[pl.BlockSpec((tm, d), lambda i: (i, 0)),
                   pl.BlockSpec((tm, LANES), lambda i: (i, 0)),
                   pl.BlockSpec((SUBLANES, tm), lambda i: (0, i)),
                   pl.BlockSpec((1, 1, LANES), lambda i: (i, 0, 0))],
        out_shape=[jax.ShapeDtypeStruct((m, d), BF16), jax.ShapeDtypeStruct((m, LANES), F32),
                   jax.ShapeDtypeStruct((SUBLANES, m), F32),
                   jax.ShapeDtypeStruct((m // tm, 1, LANES), F32)],
        compiler_params=_cparams(("parallel",)), name="router",
    )(x, g.reshape(1, d), rw_pad, rb_pad)


MOE_ROWS = 128


def _moe_kernel(n_exp, cnt_ref, x_hbm, hn_ref, gate_ref, pos_ref, w1_ref, w3_ref, w2_ref, o_ref, xs_scr, y_scr):
    i = pl.program_id(0)
    e = pl.program_id(1)
    j = pl.program_id(2)
    nj = pl.num_programs(2)
    tm = hn_ref.shape[0]
    n_small = (cnt_ref[i * n_exp + e] + (MOE_ROWS - 1)) // MOE_ROWS
    n_big = n_small // 2
    tail = n_small % 2

    @pl.when(jnp.logical_and(e == 0, j == 0))
    def _():
        pltpu.sync_copy(x_hbm.at[pl.ds(pl.multiple_of(i * tm, tm), tm), :], o_ref)

    def select(start, nrows):
        pos = pos_ref[pl.ds(e, 1), :]
        want = (start + _iota((nrows, tm), 0)).astype(F32)
        return jnp.where(pos == want, 1.0, 0.0).astype(BF16)

    def blocks(body):
        def big(blk, carry):
            body(pl.multiple_of(blk * 2 * MOE_ROWS, 2 * MOE_ROWS), 2 * MOE_ROWS)
            return carry
        lax.fori_loop(0, n_big, big, 0)

        @pl.when(tail == 1)
        def _():
            body(pl.multiple_of(n_big * 2 * MOE_ROWS, 2 * MOE_ROWS), MOE_ROWS)

    @pl.when(j == 0)
    def _():
        def gather(start, nrows):
            rows = pl.ds(start, nrows)
            xs_scr[rows, :] = _dg(select(start, nrows), hn_ref[...], _NN).astype(BF16)
            y_scr[rows, :] = jnp.zeros((nrows, y_scr.shape[1]), F32)
        blocks(gather)

    def expert(start, nrows):
        rows = pl.ds(start, nrows)
        xs = xs_scr[rows, :]
        h = _silu(_dg(xs, w1_ref[0], _NN)) * _dg(xs, w3_ref[0], _NN)
        y_scr[rows, :] += _dg(h.astype(BF16), w2_ref[0], _NN)
    blocks(expert)

    @pl.when(j == nj - 1)
    def _():
        gt = gate_ref[...]
        gcol = jnp.sum(jnp.where(_iota(gt.shape, 1) == e, gt, 0.0), axis=-1, keepdims=True)

        def scatter(start, nrows):
            o_ref[...] += gcol * _dg(select(start, nrows), y_scr[pl.ds(start, nrows), :].astype(BF16), _TN)
        blocks(scatter)


def moe(x, hn, gates, pos, counts, w1, w3, w2, tm_router, tm, fc):
    m, d = x.shape
    n_exp, _, dff = w1.shape
    nj = dff // fc
    ratio = tm // tm_router
    cnt = counts[:, 0, :n_exp].astype(jnp.int32).reshape(m // tm, ratio, n_exp)
    before = (jnp.cumsum(cnt, axis=1) - cnt).astype(F32)
    shift = jnp.repeat(before.reshape(m // tm_router, n_exp).T, tm_router, axis=1)
    shift = jnp.pad(shift, ((0, SUBLANES - n_exp), (0, 0)))
    pos = jnp.where(pos >= 0, pos + shift, pos)
    cnt = cnt.sum(axis=1).reshape(-1)
    grid_spec = pltpu.PrefetchScalarGridSpec(
        num_scalar_prefetch=1,
        grid=(m // tm, n_exp, nj),
        in_specs=[pl.BlockSpec(memory_space=pl.ANY),
                  pl.BlockSpec((tm, d), lambda i, e, j, c: (i, 0)),
                  pl.BlockSpec((tm, LANES), lambda i, e, j, c: (i, 0)),
                  pl.BlockSpec((SUBLANES, tm), lambda i, e, j, c: (0, i)),
                  pl.BlockSpec((1, d, fc), lambda i, e, j, c: (e, 0, j)),
                  pl.BlockSpec((1, d, fc), lambda i, e, j, c: (e, 0, j)),
                  pl.BlockSpec((1, fc, d), lambda i, e, j, c: (e, j, 0))],
        out_specs=pl.BlockSpec((tm, d), lambda i, e, j, c: (i, 0)),
        scratch_shapes=[pltpu.VMEM((tm, d), BF16), pltpu.VMEM((tm, d), F32)])
    return pl.pallas_call(
        functools.partial(_moe_kernel, n_exp),
        grid_spec=grid_spec,
        out_shape=jax.ShapeDtypeStruct((m, d), F32),
        compiler_params=_cparams(("parallel", "arbitrary", "arbitrary")), name="moe",
    )(cnt, x, hn, gates, pos, w1, w3, w2)


RW = 512
RSHIFT = 1792


def _rwkv_pre_kernel(t_per_batch, tm, p_ref, pb_ref, prev_ref, mu_ref, w0_ref, wup_ref, a0_ref, aup_ref,
                     gup_ref, kk_ref, ka_ref, rk_ref,
                     r_o, lw_o, k_o, v_o, al_o, be_o, g_o, bo_o):
    i = pl.program_id(0)
    p = p_ref[...]
    rolled = pltpu.roll(p, 1, 0)
    row = _iota((tm, 1), 0)
    rolled = jnp.where(row == 0, pb_ref[SUBLANES - 1:SUBLANES, :], rolled)
    if t_per_batch >= tm:
        first = (i * tm) % t_per_batch == 0
        is_start = jnp.logical_and(row == 0, first)
        prev = jnp.where(is_start, prev_ref[0], rolled)
    else:
        is_start = (row % t_per_batch) == 0
        prev = jnp.where(is_start, prev_ref[...], rolled)
    xs = p + (prev - p) * mu_ref[...]
    r = xs[:, 0:RW]
    k = xs[:, RW:2 * RW]
    v = xs[:, 2 * RW:3 * RW]
    lr = xs[:, 3 * RW:3 * RW + LANES]
    gd = xs[:, 3 * RW + LANES:3 * RW + 2 * LANES]
    w_pre = w0_ref[...] + _bdot(jnp.tanh(lr), wup_ref[...])
    logw = -jnp.exp(-_softplus(-w_pre) - 0.5)
    a = _sigmoid(a0_ref[...] + _bdot(lr, aup_ref[...]))
    g = _bdot(_sigmoid(gd), gup_ref[...])
    kkr = k * kk_ref[...]
    kk = kkr / jnp.maximum(jnp.sqrt(_seg_sum(kkr * kkr, RWKV_HD)), 1e-12)
    k2 = k * (1.0 + (a - 1.0) * ka_ref[...])
    bonus = _seg_sum(r * k2 * rk_ref[...], RWKV_HD) * v
    r_o[...] = r
    lw_o[...] = logw
    k_o[...] = k2
    v_o[...] = v
    al_o[...] = -kk
    be_o[...] = kk * a
    g_o[...] = g
    bo_o[...] = bonus


def rwkv_pre(p_full, prev_rows, t_per_batch, tm, prm):
    m = p_full.shape[0]
    nb8 = tm // SUBLANES
    row_spec = pl.BlockSpec((1, RW), lambda i: (0, 0))
    if t_per_batch >= tm:
        prev_spec = pl.BlockSpec((1, 1, RSHIFT), lambda i: ((i * tm) // t_per_batch, 0, 0))
    else:
        prev_spec = pl.BlockSpec((tm, RSHIFT), lambda i: (i, 0))
    out_sds = jax.ShapeDtypeStruct((m, RW), F32)
    out_spec = pl.BlockSpec((tm, RW), lambda i: (i, 0))
    return pl.pallas_call(
        functools.partial(_rwkv_pre_kernel, t_per_batch, tm),
        grid=(m // tm,),
        in_specs=[pl.BlockSpec((tm, RSHIFT), lambda i: (i, 0)),
                  pl.BlockSpec((SUBLANES, RSHIFT), lambda i: (jnp.maximum(i * nb8 - 1, 0), 0)),
                  prev_spec,
                  pl.BlockSpec((1, RSHIFT), lambda i: (0, 0)),
                  row_spec,
                  pl.BlockSpec((LANES, RW), lambda i: (0, 0)),
                  row_spec,
                  pl.BlockSpec((LANES, RW), lambda i: (0, 0)),
                  pl.BlockSpec((LANES, RW), lambda i: (0, 0)),
                  row_spec, row_spec, row_spec],
        out_specs=[out_spec] * 8,
        out_shape=[out_sds] * 8,
        compiler_params=_cparams(("parallel",)), name="rwkv_pre",
    )(p_full, p_full, prev_rows, prm["mu"], prm["w0"], prm["wup"], prm["a0"], prm["aup"], prm["gup"],
      prm["k_k"], prm["k_a"], prm["r_k"])


_RW_DOT_A = _bdot
_RW_DOT_T = _bdot
_RW_DOT_W = _bdot
_RW_DOT_S = _bdot


def _rwkv_scan_kernel(L, nb, n_pairs, r_ref, lw_ref, k_ref, v_ref, al_ref, be_ref, s0_ref,
                      y_ref, sf_ref, s_scr):
    c = pl.program_id(1)
    nc = pl.num_programs(1)
    L2 = 2 * L
    lane = _iota((1, LANES), 1)
    m0 = jnp.where(lane < RWKV_HD, 1.0, 0.0)
    m1 = 1.0 - m0
    rr = _iota((L2, L2), 0)
    cc = _iota((L2, L2), 1)
    same = (rr // L) == (cc // L)
    strict = jnp.logical_and(same, (cc % L) < (rr % L))
    incl = jnp.logical_and(same, (cc % L) <= (rr % L))
    eye2 = jnp.where(rr == cc, 1.0, 0.0)
    r128 = _iota((LANES, LANES), 0)
    c128 = _iota((LANES, LANES), 1)
    blk128 = (r128 // RWKV_HD) == (c128 // RWKV_HD)
    diag128 = r128 == c128
    fmat = jnp.where(_iota((LANES, RWKV_HD), 0) % RWKV_HD == _iota((LANES, RWKV_HD), 1), 1.0, 0.0)
    tri = _tri_incl(L)

    def bd(x):
        return jnp.concatenate([x * m0, x * m1], axis=0)

    @pl.when(c == 0)
    def _():
        for bb in range(nb):
            for j in range(n_pairs):
                s0 = s0_ref[bb, j]
                st = _xdot_r(fmat, s0, _NT)
                s_scr[bb * n_pairs + j] = jnp.where(blk128, st, 0.0)

    pairs = range(nb * n_pairs)
    bbs = [i // n_pairs for i in pairs]
    sls = [slice((i % n_pairs) * LANES, (i % n_pairs + 1) * LANES) for i in pairs]
    lw = [lw_ref[bbs[j], :, sls[j]] for j in pairs]
    b = [_xdot_r(tri, x) for x in lw]
    bl = [x[L - 1:L, :] for x in b]
    e_b = [jnp.exp(x) for x in b]
    e_nb = [jnp.exp(-x) for x in b]
    e_lb = [jnp.exp(bl[j] - b[j]) for j in pairs]
    at = [bd(al_ref[bbs[j], :, sls[j]] * jnp.exp(b[j] - lw[j])) for j in pairs]
    rt = [bd(r_ref[bbs[j], :, sls[j]] * e_b[j]) for j in pairs]
    bh = [bd(be_ref[bbs[j], :, sls[j]] * e_nb[j]) for j in pairs]
    kh = [bd(k_ref[bbs[j], :, sls[j]] * e_nb[j]) for j in pairs]
    bt = [bd(be_ref[bbs[j], :, sls[j]] * e_lb[j]) for j in pairs]
    kt = [bd(k_ref[bbs[j], :, sls[j]] * e_lb[j]) for j in pairs]
    vb = [bd(v_ref[bbs[j], :, sls[j]]) for j in pairs]
    if L2 % LANES == 0:
        gq = [_RW_DOT_A(jnp.concatenate([at[j], rt[j]], axis=0), jnp.concatenate([bh[j], kh[j]], axis=0), _NT)
              for j in pairs]
        a_ab = [jnp.where(strict, g[0:L2, 0:L2], 0.0) for g in gq]
        a_ak = [jnp.where(strict, g[0:L2, L2:2 * L2], 0.0) for g in gq]
        a_rb = [jnp.where(incl, g[L2:2 * L2, 0:L2], 0.0) for g in gq]
        a_rk = [jnp.where(incl, g[L2:2 * L2, L2:2 * L2], 0.0) for g in gq]
    else:
        a_ab = [jnp.where(strict, _RW_DOT_A(at[j], bh[j], _NT), 0.0) for j in pairs]
        a_ak = [jnp.where(strict, _RW_DOT_A(at[j], kh[j], _NT), 0.0) for j in pairs]
        a_rb = [jnp.where(incl, _RW_DOT_A(rt[j], bh[j], _NT), 0.0) for j in pairs]
        a_rk = [jnp.where(incl, _RW_DOT_A(rt[j], kh[j], _NT), 0.0) for j in pairs]
    x = a_ab
    tinv = [eye2 + a for a in a_ab]
    span = 2
    while span < L:
        x = [_RW_DOT_T(xx, xx) for xx in x]
        tinv = [tinv[j] + _RW_DOT_T(tinv[j], x[j]) for j in pairs]
        span *= 2
    akv = [_RW_DOT_W(a_ak[j], vb[j]) for j in pairs]
    wuv = [_RW_DOT_W(tinv[j], jnp.concatenate([at[j], akv[j]], axis=1)) for j in pairs]
    mn = [_RW_DOT_W(bt[j], wuv[j], _TN) for j in pairs]
    qy = [_RW_DOT_W(a_rb[j], wuv[j]) for j in pairs]
    mm = [jnp.where(diag128, jnp.exp(bl[j]), 0.0) + mn[j][:, 0:LANES] for j in pairs]
    nn = [mn[j][:, LANES:2 * LANES] + _RW_DOT_W(kt[j], vb[j], _TN) for j in pairs]
    q = [rt[j] + qy[j][:, 0:LANES] for j in pairs]
    yv = [qy[j][:, LANES:2 * LANES] + _RW_DOT_W(a_rk[j], vb[j]) for j in pairs]
    s = [s_scr[j] for j in pairs]
    for j in pairs:
        ybd = _RW_DOT_S(q[j], s[j]) + yv[j]
        y_ref[bbs[j], :, sls[j]] = ybd[0:L] + ybd[L:L2]
    for j in pairs:
        s_scr[j] = _RW_DOT_S(mm[j], s[j]) + nn[j]

    @pl.when(c == nc - 1)
    def _():
        for i in pairs:
            sf_ref[bbs[i], i % n_pairs] = _xdot_l(s_scr[i], fmat, _TN)


def rwkv_scan(ops, s0, L, nb):
    bsz, t, _ = ops[0].shape
    n_pairs = RW // LANES
    blk = pl.BlockSpec((nb, L, RW), lambda b, c: (b, c, 0))
    st_spec = pl.BlockSpec((nb, n_pairs, LANES, RWKV_HD), lambda b, c: (b, 0, 0, 0))
    return pl.pallas_call(
        functools.partial(_rwkv_scan_kernel, L, nb, n_pairs),
        grid=(bsz // nb, t // L),
        in_specs=[blk] * 6 + [st_spec],
        out_specs=[blk, st_spec],
        out_shape=[jax.ShapeDtypeStruct((bsz, t, RW), F32),
                   jax.ShapeDtypeStruct((bsz, n_pairs, LANES, RWKV_HD), F32)],
        scratch_shapes=[pltpu.VMEM((nb * n_pairs, LANES, LANES), F32)],
        compiler_params=_cparams(("parallel", "arbitrary")), name="rwkv_scan",
    )(*ops, s0)


def _rwkv_post_kernel(y_ref, bo_ref, g_ref, lw_ref, lb_ref, o_ref):
    y = y_ref[...]
    mean = _seg_sum(y, RWKV_HD) * (1.0 / RWKV_HD)
    d = y - mean
    var = _seg_sum(d * d, RWKV_HD) * (1.0 / RWKV_HD)
    yn = d * lax.rsqrt(var + RWKV_GN_EPS) * lw_ref[...] + lb_ref[...]
    o_ref[...] = (yn + bo_ref[...]) * g_ref[...]


def rwkv_post(y, bonus, g, lnx_w, lnx_b, tm):
    m = y.shape[0]
    spec = pl.BlockSpec((tm, RW), lambda i: (i, 0))
    row = pl.BlockSpec((1, RW), lambda i: (0, 0))
    return pl.pallas_call(
        _rwkv_post_kernel,
        grid=(m // tm,),
        in_specs=[spec, spec, spec, row, row],
        out_specs=spec,
        out_shape=jax.ShapeDtypeStruct((m, RW), F32),
        compiler_params=_cparams(("parallel",)), name="rwkv_post",
    )(y, bonus, g, lnx_w.reshape(1, RW), lnx_b.reshape(1, RW))


def rwkv_params(w, i):
    z64 = jnp.zeros((RWKV_HD, RW), F32)
    return {
        "mu": w["e_mu"][i].reshape(1, RSHIFT),
        "w0": w["e_w0"][i].reshape(1, RW),
        "wup": jnp.concatenate([w["e_w_up"][i], z64], 0).astype(BF16),
        "a0": w["e_a0"][i].reshape(1, RW),
        "aup": jnp.concatenate([z64, w["e_a_up"][i]], 0).astype(BF16),
        "gup": w["e_g_up"][i].astype(BF16),
        "k_k": w["e_k_k"][i].reshape(1, RW),
        "k_a": w["e_k_a"][i].reshape(1, RW),
        "r_k": w["e_r_k"][i].reshape(1, RW),
    }


def rwkv_mix(p_flat, prev, s0, bsz, t, L, tm, prm, lnx_w, lnx_b, nb=1):
    m = bsz * t
    if t >= tm:
        prev_rows = prev.reshape(bsz, 1, RSHIFT)
    else:
        prev_rows = jnp.repeat(prev, t, axis=0)
    r, lw, k, v, al, be, g, bonus = rwkv_pre(p_flat, prev_rows, t, tm, prm)
    tp = -(-t // L) * L
    ops = [z.reshape(bsz, t, RW) for z in (r, lw, k, v, al, be)]
    if tp != t:
        ops = [jnp.pad(z, ((0, 0), (0, tp - t), (0, 0))) for z in ops]
    y, s_fin = rwkv_scan(ops, s0.reshape(bsz, RW // LANES, LANES, RWKV_HD), L, nb)
    y = y[:, :t].reshape(m, RW)
    ya = rwkv_post(y, bonus, g, lnx_w, lnx_b, tm)
    return ya, s_fin.reshape(bsz, RW // RWKV_HD, RWKV_HD, RWKV_HD)


N_Q_HEADS = 8
N_KV_HEADS = 2
Q_PER_KV = N_Q_HEADS // N_KV_HEADS
QW = N_Q_HEADS * ATT_HD
KW = N_KV_HEADS * ATT_HD
NEG = -1e30


def _t5_bucket_np(dist):
    n = np.maximum(dist, 0)
    max_exact = N_BUCKETS // 2
    nf = np.maximum(n, 1).astype(np.float32)
    large = max_exact + (np.log(nf / np.float32(max_exact)) / np.float32(math.log(BUCKET_MAX_DIST / max_exact))
                         * np.float32(N_BUCKETS - max_exact)).astype(np.int32)
    large = np.minimum(large, N_BUCKETS - 1)
    return np.where(n < max_exact, n, large)


def _bias_kernel(rt_ref, oh_ref, o_ref):
    o_ref[...] = _xdot_l(rt_ref[...], oh_ref[...])


def rel_bias(rel_table, dist):
    bucket = _t5_bucket_np(dist).reshape(-1)
    n = bucket.shape[0]
    onehot = jnp.asarray((np.arange(N_BUCKETS)[:, None] == bucket[None, :]).astype(np.float32), BF16)
    out = pl.pallas_call(
        _bias_kernel,
        out_shape=jax.ShapeDtypeStruct((N_Q_HEADS, n), F32),
    )(rel_table.T, onehot)
    return out.reshape((N_Q_HEADS,) + dist.shape)


def _head_norm(x, w_row):
    return x * lax.rsqrt(_seg_sum(x * x, ATT_HD) * (1.0 / ATT_HD) + EPS) * w_row


def _swa_prompt_kernel(nqb, q_ref, kvc_ref, kvp_ref, qw_ref, kw_ref, bias_ref, sink_ref, o_ref, ko_ref, vo_ref):
    i = pl.program_id(1)
    qn = _head_norm(q_ref[0], qw_ref[...])
    kvc = kvc_ref[0]
    kvp = kvp_ref[0]
    kn = [_head_norm(kvp[:, 0:KW], kw_ref[...])]
    vs = [kvp[:, KW:2 * KW]]
    kcn = _head_norm(kvc[:, 0:KW], kw_ref[...])
    for j in range(nqb):
        kn.append(kcn[j * WINDOW:(j + 1) * WINDOW])
        vs.append(kvc[j * WINDOW:(j + 1) * WINDOW, KW:2 * KW])
    kcat = [jnp.concatenate([kn[j], kn[j + 1]], axis=0).astype(BF16) for j in range(nqb)]
    vcat = [jnp.concatenate([vs[j], vs[j + 1]], axis=0).astype(BF16) for j in range(nqb)]
    qi = _iota((WINDOW, 2 * WINDOW), 0)
    kj = _iota((WINDOW, 2 * WINDOW), 1)
    dist = qi + WINDOW - kj
    band = jnp.logical_and(dist >= 0, dist < WINDOW)
    valid = [jnp.logical_and(band, jnp.logical_or(kj >= WINDOW, i > 0))] + [band] * (nqb - 1)
    lane = _iota((1, LANES), 1)
    masks = (jnp.where(lane < ATT_HD, 1.0, 0.0), jnp.where(lane < ATT_HD, 0.0, 1.0))
    scale = ATT_HD ** -0.5
    heads = range(N_Q_HEADS)
    units = [(j, h) for j in range(nqb) for h in heads]
    kv_of = lambda h: h // Q_PER_KV
    qts = [qn[j * WINDOW:(j + 1) * WINDOW, (h // 2) * LANES:(h // 2 + 1) * LANES] for j, h in units]
    qts = [pltpu.roll(qts[u], ATT_HD, 1) if h % 2 != kv_of(h) else qts[u] for u, (j, h) in enumerate(units)]
    qms = [(qts[u] * masks[kv_of(h)]).astype(BF16) for u, (j, h) in enumerate(units)]
    s_all = [_dg(jnp.concatenate(qms[j * N_Q_HEADS:(j + 1) * N_Q_HEADS], axis=0), kcat[j], _NT) * scale
             for j in range(nqb)]
    logits = [jnp.where(valid[j], s_all[j][h * WINDOW:(h + 1) * WINDOW] + bias_ref[h], NEG) for j, h in units]
    sinks = [sink_ref[h:h + 1, 0:1] for j, h in units]
    mx = [jnp.maximum(jnp.max(logits[u], axis=-1, keepdims=True), sinks[u]) for u in range(len(units))]
    pr = [jnp.exp(logits[u] - mx[u]) for u in range(len(units))]
    den = [jnp.sum(pr[u], axis=-1, keepdims=True) + jnp.exp(sinks[u] - mx[u]) for u in range(len(units))]
    probs = [(pr[u] * (1.0 / den[u])).astype(BF16) for u in range(len(units))]
    o_all = [_dg(jnp.concatenate(probs[j * N_Q_HEADS:(j + 1) * N_Q_HEADS], axis=0), vcat[j], _NN)
             for j in range(nqb)]
    os_ = [o_all[j][h * WINDOW:(h + 1) * WINDOW] for j, h in units]
    os_ = [pltpu.roll(os_[u], ATT_HD, 1) if h % 2 != kv_of(h) else os_[u] for u, (j, h) in enumerate(units)]
    for j in range(nqb):
        for jq in range(QW // LANES):
            o_ref[0, j * WINDOW:(j + 1) * WINDOW, jq * LANES:(jq + 1) * LANES] = (
                os_[j * N_Q_HEADS + 2 * jq] * masks[0] + os_[j * N_Q_HEADS + 2 * jq + 1] * masks[1])
    ko_ref[0] = kn[nqb]
    vo_ref[0] = vs[nqb]


SWA_QB = 2


def swa_prompt(q, kv, bsz, t, q_norm, k_norm, bias, sinks):
    nqb = math.gcd(t // WINDOW, SWA_QB)
    rows = nqb * WINDOW
    nb = t // rows
    q3 = q.reshape(bsz, t, QW)
    kv3 = kv.reshape(bsz, t, 2 * KW)
    o, ko, vo = pl.pallas_call(
        functools.partial(_swa_prompt_kernel, nqb),
        grid=(bsz, nb),
        in_specs=[pl.BlockSpec((1, rows, QW), lambda b, i: (b, i, 0)),
                  pl.BlockSpec((1, rows, 2 * KW), lambda b, i: (b, i, 0)),
                  pl.BlockSpec((1, WINDOW, 2 * KW), lambda b, i: (b, jnp.maximum(i * nqb - 1, 0), 0)),
                  pl.BlockSpec((1, QW), lambda b, i: (0, 0)),
                  pl.BlockSpec((1, KW), lambda b, i: (0, 0)),
                  pl.BlockSpec((N_Q_HEADS, WINDOW, 2 * WINDOW), lambda b, i: (0, 0, 0)),
                  pl.BlockSpec((N_Q_HEADS, LANES), lambda b, i: (0, 0))],
        out_specs=[pl.BlockSpec((1, rows, QW), lambda b, i: (b, i, 0)),
                   pl.BlockSpec((1, WINDOW, KW), lambda b, i: (b, 0, 0)),
                   pl.BlockSpec((1, WINDOW, KW), lambda b, i: (b, 0, 0))],
        out_shape=[jax.ShapeDtypeStruct((bsz, t, QW), F32),
                   jax.ShapeDtypeStruct((bsz, WINDOW, KW), F32),
                   jax.ShapeDtypeStruct((bsz, WINDOW, KW), F32)],
        compiler_params=_cparams(("parallel", "arbitrary")), name="swa_prompt",
    )(q3, kv3, kv3, jnp.tile(q_norm, N_Q_HEADS).reshape(1, QW), jnp.tile(k_norm, N_KV_HEADS).reshape(1, KW),
      bias, jnp.broadcast_to(sinks[:, None], (N_Q_HEADS, LANES)))
    return o.reshape(bsz * t, QW), ko, vo


DEC_TP = 8


def _swa_decode_kernel(nbt, t_real, q_ref, kv_ref, ck_ref, cv_ref, qw_ref, kw_ref, bc_ref, bn_ref, sink_ref,
                       o_ref, ko_ref, vo_ref):
    rows = N_Q_HEADS * DEC_TP
    tq = _iota((rows, WINDOW), 0) % DEC_TP
    valid_c = _iota((rows, WINDOW), 1) > tq
    jn = _iota((rows, DEC_TP), 1)
    valid_n = jnp.logical_and(jn <= _iota((rows, DEC_TP), 0) % DEC_TP, jn < t_real)
    lane = _iota((1, LANES), 1)
    masks = (jnp.where(lane < ATT_HD, 1.0, 0.0), jnp.where(lane < ATT_HD, 0.0, 1.0))
    row8 = _iota((DEC_TP, 1), 0)
    scale = ATT_HD ** -0.5
    bias_c = jnp.concatenate([bc_ref[kv] for kv in range(N_KV_HEADS)], axis=0)
    bias_n = jnp.concatenate([bn_ref[kv][:, 0:DEC_TP] for kv in range(N_KV_HEADS)], axis=0)
    sink = jnp.concatenate([sink_ref[kv][:, 0:1] for kv in range(N_KV_HEADS)], axis=0)
    bs = range(nbt)
    heads = range(N_Q_HEADS)
    qn_all = _head_norm(q_ref[...].reshape(nbt * DEC_TP, QW), qw_ref[...])
    kvn_all = kv_ref[...].reshape(nbt * DEC_TP, 2 * KW)
    knew_all = _head_norm(kvn_all[:, 0:KW], kw_ref[...])
    knew = [knew_all[b * DEC_TP:(b + 1) * DEC_TP] for b in bs]
    vnew = [kvn_all[b * DEC_TP:(b + 1) * DEC_TP, KW:2 * KW] for b in bs]
    kc = [ck_ref[b] for b in bs]
    vc = [cv_ref[b] for b in bs]

    def stack_q(b):
        pieces = []
        for h in heads:
            qt = qn_all[b * DEC_TP:(b + 1) * DEC_TP, (h // 2) * LANES:(h // 2 + 1) * LANES]
            if h % 2 != h // Q_PER_KV:
                qt = pltpu.roll(qt, ATT_HD, 1)
            pieces.append(qt * masks[h // Q_PER_KV])
        return jnp.concatenate(pieces, axis=0)

    qs = [stack_q(b) for b in bs]
    l_c = [jnp.where(valid_c, _bdot(qs[b], kc[b], _NT) * scale + bias_c, NEG) for b in bs]
    l_n = [jnp.where(valid_n, _bdot(qs[b], knew[b], _NT) * scale + bias_n, NEG) for b in bs]
    mx = [jnp.maximum(jnp.maximum(jnp.max(l_c[b], axis=-1, keepdims=True),
                                  jnp.max(l_n[b], axis=-1, keepdims=True)), sink) for b in bs]
    p_c = [jnp.exp(l_c[b] - mx[b]) for b in bs]
    p_n = [jnp.exp(l_n[b] - mx[b]) for b in bs]
    inv = [1.0 / (jnp.sum(p_c[b], axis=-1, keepdims=True) + jnp.sum(p_n[b], axis=-1, keepdims=True)
                  + jnp.exp(sink - mx[b])) for b in bs]
    o = [_bdot(p_c[b] * inv[b], vc[b]) + _bdot(p_n[b] * inv[b], vnew[b]) for b in bs]
    for b in bs:
        for jq in range(QW // LANES):
            parts = []
            for h in (2 * jq, 2 * jq + 1):
                piece = o[b][h * DEC_TP:(h + 1) * DEC_TP]
                if h % 2 != h // Q_PER_KV:
                    piece = pltpu.roll(piece, ATT_HD, 1)
                parts.append(piece * masks[h % 2])
            o_ref[b, :, jq * LANES:(jq + 1) * LANES] = parts[0] + parts[1]
    for b in bs:
        for cache, new, out in ((kc[b], knew[b], ko_ref), (vc[b], vnew[b], vo_ref)):
            shifted = pltpu.roll(cache, WINDOW - t_real, 0)
            new_r = pltpu.roll(new, DEC_TP - t_real, 0)
            out[b, 0:WINDOW - DEC_TP] = shifted[0:WINDOW - DEC_TP]
            out[b, WINDOW - DEC_TP:WINDOW] = jnp.where(row8 >= DEC_TP - t_real, new_r,
                                                       shifted[WINDOW - DEC_TP:WINDOW])


def swa_decode(q, kv, cache_k, cache_v, bsz, t, q_norm, k_norm, rel_table, sinks, nbt):
    pad = ((0, 0), (0, DEC_TP - t), (0, 0))
    q3 = jnp.pad(q.reshape(bsz, t, QW), pad)
    kv3 = jnp.pad(kv.reshape(bsz, t, 2 * KW), pad)
    kpos = np.concatenate([np.arange(WINDOW) - WINDOW, np.arange(DEC_TP)])
    dist = np.arange(DEC_TP)[:, None] - kpos[None, :]
    bias = rel_bias(rel_table, dist)
    rows = Q_PER_KV * DEC_TP
    bias = bias.reshape(N_KV_HEADS, rows, WINDOW + DEC_TP)
    bias_c = bias[:, :, :WINDOW]
    bias_n = jnp.pad(bias[:, :, WINDOW:], ((0, 0), (0, 0), (0, LANES - DEC_TP)))
    sink_rows = jnp.broadcast_to(sinks.reshape(N_KV_HEADS, Q_PER_KV, 1, 1),
                                 (N_KV_HEADS, Q_PER_KV, DEC_TP, LANES)).reshape(N_KV_HEADS, rows, LANES)
    full3 = lambda shape: pl.BlockSpec(shape, lambda i: (0, 0, 0))
    o, ko, vo = pl.pallas_call(
        functools.partial(_swa_decode_kernel, nbt, t),
        grid=(bsz // nbt,),
        in_specs=[pl.BlockSpec((nbt, DEC_TP, QW), lambda i: (i, 0, 0)),
                  pl.BlockSpec((nbt, DEC_TP, 2 * KW), lambda i: (i, 0, 0)),
                  pl.BlockSpec((nbt, WINDOW, KW), lambda i: (i, 0, 0)),
                  pl.BlockSpec((nbt, WINDOW, KW), lambda i: (i, 0, 0)),
                  pl.BlockSpec((1, QW), lambda i: (0, 0)),
                  pl.BlockSpec((1, KW), lambda i: (0, 0)),
                  full3((N_KV_HEADS, rows, WINDOW)),
                  full3((N_KV_HEADS, rows, LANES)),
                  full3((N_KV_HEADS, rows, LANES))],
        out_specs=[pl.BlockSpec((nbt, DEC_TP, QW), lambda i: (i, 0, 0)),
                   pl.BlockSpec((nbt, WINDOW, KW), lambda i: (i, 0, 0)),
                   pl.BlockSpec((nbt, WINDOW, KW), lambda i: (i, 0, 0))],
        out_shape=[jax.ShapeDtypeStruct((bsz, DEC_TP, QW), F32),
                   jax.ShapeDtypeStruct((bsz, WINDOW, KW), F32),
                   jax.ShapeDtypeStruct((bsz, WINDOW, KW), F32)],
        compiler_params=_cparams(("parallel",)), name="swa_decode",
    )(q3, kv3, cache_k, cache_v, jnp.tile(q_norm, N_Q_HEADS).reshape(1, QW),
      jnp.tile(k_norm, N_KV_HEADS).reshape(1, KW), bias_c, bias_n, sink_rows)
    return o[:, :t].reshape(bsz * t, QW), ko, vo


S5_G = 32
S5_W = S5_G * S5_P
S5_CP = S5_CHUNK * S5_P
S5_PK = 2 * S5_N
S5_HW = S5_G * S5_PK


S5_QT = S5_W // LANES
S5_GT = LANES // S5_P
S5_XW = S5_CHUNK * LANES
S5_HQ = S5_GT * S5_PK


def _s5_group_maps(t_effs, a2, ldt, b1, b2, c1, c2):
    L = S5_CHUNK
    ar2 = a2[0:1, :]
    ai2 = a2[1:2, :]
    step = jnp.exp(ldt)
    mi = _iota((3 * SUBLANES, S5_PK), 0).astype(F32)
    mag = jnp.exp(mi * (step * ar2))
    ang = mi * (step * ai2)
    pwa = mag * jnp.cos(ang)
    pwb = mag * jnp.sin(ang)
    abr = pwa[1:2]
    abi = pwb[1:2]
    den = ar2 * ar2 + ai2 * ai2
    fa = ((abr - 1.0) * ar2 + abi * ai2) / den
    fb = (abi * ar2 - (abr - 1.0) * ai2) / den
    bp1 = b1 * fa + b2 * fb
    bp2 = b2 * fa - b1 * fb
    cpow = [c1 * pwa[m:m + 1] + c2 * pwb[m:m + 1] for m in range(L + 1)]
    kern_t = _hdot(bp1, jnp.concatenate(cpow[0:L], axis=0), _NT)
    sgn = jnp.where(_iota((1, S5_PK), 1) < S5_N, -1.0, 1.0)
    kbs, als = [], []
    for t_eff in t_effs:
        kbs.append([bp1 * pwa[max(t_eff - 1 - i, 0):max(t_eff - 1 - i, 0) + 1]
                    + bp2 * pwb[max(t_eff - 1 - i, 0):max(t_eff - 1 - i, 0) + 1] for i in range(L)])
        als.append(jnp.concatenate([pwa[t_eff:t_eff + 1], sgn * pwb[t_eff:t_eff + 1]], axis=0))
    return kern_t, kbs, als, cpow[1:L + 1]


def _s5_prep_kernel(t_effs, a_ref, ldt_ref, b1_ref, b2_ref, c1_ref, c2_ref, bd_ref, kc_ref, *rest):
    L = S5_CHUNK
    n_t = len(t_effs)
    kb_refs = rest[:n_t]
    al_refs = rest[n_t:2 * n_t]
    lane = _iota((S5_P, LANES), 1)
    for kb_ref in kb_refs:
        kb_ref[0] = jnp.zeros(kb_ref.shape[1:], kb_ref.dtype)
    bd_rows = [[] for _ in range(L)]
    kct_rows = [[] for _ in range(L)]
    for g in range(S5_GT):
        kern_t, kbs, als, kct = _s5_group_maps(t_effs, a_ref[g], ldt_ref[g], b1_ref[g], b2_ref[g],
                                               c1_ref[g], c2_ref[g])
        in_group = jnp.logical_and(lane >= g * S5_P, lane < (g + 1) * S5_P)
        for tau in range(L):
            shift = (g * S5_P - tau * S5_P) % S5_CP
            moved = pltpu.roll(kern_t, shift, 1) if shift else kern_t
            bd_rows[tau].append(jnp.where(in_group, moved[:, 0:LANES], 0.0))
        for k in range(n_t):
            for i in range(L):
                kb_refs[k][0, i * LANES + g * S5_P:i * LANES + (g + 1) * S5_P, g * S5_PK:(g + 1) * S5_PK] = (
                    kbs[k][i].astype(kb_refs[k].dtype))
            al_refs[k][g] = als[k]
        zl = jnp.zeros((S5_P, g * S5_PK), F32)
        zr = jnp.zeros((S5_P, (S5_GT - 1 - g) * S5_PK), F32)
        for t in range(L):
            parts = ([zl] if g else []) + [kct[t]] + ([zr] if g < S5_GT - 1 else [])
            kct_rows[t].append(jnp.concatenate(parts, axis=1))
    for tau in range(L):
        bd_ref[0, tau] = jnp.concatenate(bd_rows[tau], axis=0).astype(bd_ref.dtype)
    for t in range(L):
        blk_t = jnp.concatenate(kct_rows[t], axis=0)
        kc_ref[0, :, t * LANES:(t + 1) * LANES] = blk_t.T.astype(kc_ref.dtype)


def s5_prep(w, i, t_effs):
    dup = lambda z: jnp.concatenate([z, z], axis=-1)
    a = jnp.stack([dup(w["o_a_re"][i]), dup(w["o_a_im"][i])], axis=1)
    ldt = jnp.broadcast_to(w["o_log_dt"][i][:, None, None], (S5_G, 1, S5_PK))
    bt_re = jnp.swapaxes(w["o_b_re"][i], 1, 2)
    bt_im = jnp.swapaxes(w["o_b_im"][i], 1, 2)
    b1 = jnp.concatenate([bt_re, bt_im], -1)
    b2 = jnp.concatenate([-bt_im, bt_re], -1)
    c_re, c_im = w["o_c_re"][i], w["o_c_im"][i]
    c1 = jnp.concatenate([c_re, -c_im], -1)
    c2 = jnp.concatenate([-c_im, -c_re], -1)
    n_t = len(t_effs)
    g3 = lambda r, c: pl.BlockSpec((S5_GT, r, c), lambda q: (q, 0, 0))
    outs = pl.pallas_call(
        functools.partial(_s5_prep_kernel, tuple(t_effs)),
        grid=(S5_QT,),
        in_specs=[g3(2, S5_PK), g3(1, S5_PK), g3(S5_P, S5_PK), g3(S5_P, S5_PK), g3(S5_P, S5_PK), g3(S5_P, S5_PK)],
        out_specs=[pl.BlockSpec((1, S5_CHUNK, LANES, LANES), lambda q: (q, 0, 0, 0)),
                   pl.BlockSpec((1, S5_HQ, S5_XW), lambda q: (q, 0, 0))]
                  + [pl.BlockSpec((1, S5_XW, S5_HQ), lambda q: (q, 0, 0))] * n_t
                  + [g3(2, S5_PK)] * n_t,
        out_shape=[jax.ShapeDtypeStruct((S5_QT, S5_CHUNK, LANES, LANES), BF16),
                   jax.ShapeDtypeStruct((S5_QT, S5_HQ, S5_XW), BF16)]
                  + [jax.ShapeDtypeStruct((S5_QT, S5_XW, S5_HQ), BF16)] * n_t
                  + [jax.ShapeDtypeStruct((S5_G, 2, S5_PK), F32)] * n_t,
        compiler_params=_cparams(("parallel",)), name="s5_prep",
    )(a, ldt, b1, b2, c1, c2)
    bd, kc = outs[0], outs[1]
    mats = []
    for k in range(n_t):
        al = outs[2 + n_t + k]
        mats.append((bd, outs[2 + k], kc, al[:, 0, :].reshape(1, S5_HW), al[:, 1, :].reshape(1, S5_HW)))
    return mats


def _s5_e_kernel(u_ref, kb_ref, e_ref):
    e_ref[...] = _bdot(u_ref[0], kb_ref[0])


def _s5_swap(h):
    n = h.shape[-1]
    lane = _iota(h.shape, 1)
    return jnp.where(lane % S5_PK < S5_N, pltpu.roll(h, n - S5_N, 1), pltpu.roll(h, S5_N, 1))


def _s5_scan_kernel(bsz, cg, e_ref, h0_ref, ala_ref, alb_ref, hp_ref, hf_ref, h_scr):
    @pl.when(pl.program_id(0) == 0)
    def _():
        h_scr[...] = h0_ref[...]

    ala = ala_ref[...]
    alb = alb_ref[...]

    def body(c, hs):
        out = []
        for b in range(bsz):
            hp_ref[b, pl.ds(c, 1), :] = hs[b]
            out.append(ala * hs[b] + alb * _s5_swap(hs[b]) + e_ref[b, pl.ds(c, 1), :])
        return tuple(out)

    hs = lax.fori_loop(0, cg, body, tuple(h_scr[b:b + 1, :] for b in range(bsz)))
    for b in range(bsz):
        h_scr[b:b + 1, :] = hs[b]
        hf_ref[b:b + 1, :] = hs[b]


def _s5_step_kernel(e_ref, h0_ref, ala_ref, alb_ref, hf_ref):
    h = h0_ref[...]
    hf_ref[...] = ala_ref[...] * h + alb_ref[...] * _s5_swap(h) + e_ref[...]


def _s5_y_kernel(u_ref, hp_ref, bd_ref, kc_ref, y_ref, k_scr):
    @pl.when(pl.program_id(1) == 0)
    def _():
        zero = jnp.zeros((LANES, LANES), k_scr.dtype)
        for i in range(S5_CHUNK):
            for t in range(S5_CHUNK):
                k_scr[i * LANES:(i + 1) * LANES, t * LANES:(t + 1) * LANES] = bd_ref[0, t - i] if t >= i else zero

    y_ref[0] = _bdot(u_ref[0], k_scr[...]) + _bdot(hp_ref[...], kc_ref[0])


def s5_core(u4, h0, bsz, t, mats):
    bd, kbbig, kcbig, ala, alb = mats
    L = S5_CHUNK
    tp = -(-t // L) * L
    x = u4.reshape(S5_QT, bsz, t, LANES)
    if tp != t:
        x = jnp.pad(x, ((0, 0), (0, 0), (0, tp - t), (0, 0)))
    nc = tp // L
    rows = nc * bsz
    x = x.reshape(S5_QT, rows, S5_XW)
    tr = math.gcd(rows, 512)
    e = pl.pallas_call(
        _s5_e_kernel,
        grid=(S5_QT, rows // tr),
        in_specs=[pl.BlockSpec((1, tr, S5_XW), lambda q, r: (q, r, 0)),
                  pl.BlockSpec((1, S5_XW, S5_HQ), lambda q, r: (q, 0, 0))],
        out_specs=pl.BlockSpec((tr, S5_HQ), lambda q, r: (r, q)),
        out_shape=jax.ShapeDtypeStruct((rows, S5_HW), F32),
        compiler_params=_cparams(("parallel", "parallel")), name="s5_e",
    )(x, kbbig)
    row = pl.BlockSpec((1, S5_HW), lambda i: (0, 0))
    if nc == 1:
        hp = h0
        hf = pl.pallas_call(
            _s5_step_kernel,
            out_shape=jax.ShapeDtypeStruct((bsz, S5_HW), F32), name="s5_step",
        )(e, h0, ala, alb)
    else:
        cg = math.gcd(nc, 64)
        hp, hf = pl.pallas_call(
            functools.partial(_s5_scan_kernel, bsz, cg),
            grid=(nc // cg,),
            in_specs=[pl.BlockSpec((bsz, cg, S5_HW), lambda i: (0, i, 0)),
                      pl.BlockSpec((bsz, S5_HW), lambda i: (0, 0)), row, row],
            out_specs=[pl.BlockSpec((bsz, cg, S5_HW), lambda i: (0, i, 0)),
                       pl.BlockSpec((bsz, S5_HW), lambda i: (0, 0))],
            out_shape=[jax.ShapeDtypeStruct((bsz, nc, S5_HW), F32),
                       jax.ShapeDtypeStruct((bsz, S5_HW), F32)],
            scratch_shapes=[pltpu.VMEM((bsz, S5_HW), F32)],
            compiler_params=_cparams(("arbitrary",)), name="s5_scan",
        )(e.reshape(bsz, nc, S5_HW), h0, ala, alb)
    y = pl.pallas_call(
        _s5_y_kernel,
        grid=(S5_QT, rows // tr),
        in_specs=[pl.BlockSpec((1, tr, S5_XW), lambda q, r: (q, r, 0)),
                  pl.BlockSpec((tr, S5_HQ), lambda q, r: (r, q)),
                  pl.BlockSpec((1, S5_CHUNK, LANES, LANES), lambda q, r: (q, 0, 0, 0)),
                  pl.BlockSpec((1, S5_HQ, S5_XW), lambda q, r: (q, 0, 0))],
        out_specs=pl.BlockSpec((1, tr, S5_XW), lambda q, r: (q, r, 0)),
        out_shape=jax.ShapeDtypeStruct((S5_QT, rows, S5_XW), F32),
        scratch_shapes=[pltpu.VMEM((S5_XW, S5_XW), BF16)],
        compiler_params=_cparams(("parallel", "arbitrary")), name="s5_y",
    )(x, hp.reshape(rows, S5_HW), bd, kcbig)
    y = y.reshape(S5_QT, bsz, tp, LANES)[:, :, :t].reshape(S5_QT, bsz * t, LANES)
    return y, hf


def _s5_post_kernel(y_ref, u_ref, d_ref, gw_ref, gb_ref, o_ref):
    y = jnp.concatenate([y_ref[q] for q in range(S5_QT)], axis=1)
    u = jnp.concatenate([u_ref[q] for q in range(S5_QT)], axis=1)
    x = y + d_ref[...] * u
    z = 0.5 * x * (1.0 + jnp.tanh(math.sqrt(2.0 / math.pi) * (x + 0.044715 * (x * x * x))))
    o_ref[...] = z * _sigmoid(_dg(z.astype(BF16), gw_ref[...], _NN) + gb_ref[...])


def s5_post(y4, u4, d, glu_w_bf16, glu_b, tm):
    m = y4.shape[1]
    spec4 = pl.BlockSpec((S5_QT, tm, LANES), lambda i: (0, i, 0))
    row = pl.BlockSpec((1, S5_W), lambda i: (0, 0))
    return pl.pallas_call(
        _s5_post_kernel,
        grid=(m // tm,),
        in_specs=[spec4, spec4, row, pl.BlockSpec((S5_W, S5_W), lambda i: (0, 0)), row],
        out_specs=pl.BlockSpec((tm, S5_W), lambda i: (i, 0)),
        out_shape=jax.ShapeDtypeStruct((m, S5_W), F32),
        compiler_params=_cparams(("parallel",)), name="s5_post",
    )(y4, u4, d.reshape(1, S5_W), glu_w_bf16, glu_b.reshape(1, S5_W))


def s5_mix(u4, h_re, h_im, bsz, t, mats, d, glu_w_bf16, glu_b, tm):
    h0 = jnp.concatenate([h_re, h_im], axis=-1).reshape(bsz, S5_HW)
    y4, hf = s5_core(u4, h0, bsz, t, mats)
    out = s5_post(y4, u4, d, glu_w_bf16, glu_b, tm)
    hf = hf.reshape(bsz, S5_G, 2, S5_N)
    return out, hf[:, :, 0], hf[:, :, 1]


GLA_DK = 64
GLA_DV = 128
GLA_HEADS = 4
GLA_KW = GLA_HEADS * GLA_DK
GLA_VW = GLA_HEADS * GLA_DV
GLA_PW = 2 * GLA_KW + 2 * GLA_VW + LANES


def _gla_kernel(L, t_real, nb, p_ref, aup_ref, ab_ref, nw_ref, s0_ref, y_ref, sf_ref, s_scr):
    c = pl.program_id(1)
    nc = pl.num_programs(1)
    n_pairs = GLA_KW // LANES

    @pl.when(c == 0)
    def _():
        for bb in range(nb):
            for j in range(n_pairs):
                s_scr[bb * n_pairs + j] = s0_ref[bb, j]

    lane = _iota((1, LANES), 1)
    masks = (jnp.where(lane < GLA_DK, 1.0, 0.0), jnp.where(lane < GLA_DK, 0.0, 1.0))
    incl = _iota((L, L), 1) <= _iota((L, L), 0)
    diag128 = _iota((LANES, LANES), 0) == _iota((LANES, LANES), 1)
    ones128 = jnp.ones((LANES, LANES), BF16)
    tri = _tri_incl(L)
    rows = range(nb)
    ps = [p_ref[bb] for bb in rows]
    zs = [_bdot(p[:, 2 * GLA_KW + 2 * GLA_VW:], aup_ref[...]) + ab_ref[...] for p in ps]
    gk = [-_softplus(-z) * (1.0 / GLA_GATE_NORM) for z in zs]
    if t_real % L != 0:
        tok = c * L + _iota((L, 1), 0)
        gk = [jnp.where(tok < t_real, x, 0.0) for x in gk]
    b = [_xdot_r(tri, x) for x in gk]
    bl = [x[L - 1:L, :] for x in b]
    qd = [ps[i][:, 0:GLA_KW] * (GLA_DK ** -0.5) * jnp.exp(b[i]) for i in rows]
    kh = [ps[i][:, GLA_KW:2 * GLA_KW] * jnp.exp(-b[i]) for i in rows]
    kt = [ps[i][:, GLA_KW:2 * GLA_KW] * jnp.exp(bl[i] - b[i]) for i in rows]
    heads = [(bb, h) for bb in rows for h in range(GLA_HEADS)]
    sl = lambda h: slice((h // 2) * LANES, (h // 2 + 1) * LANES)
    hs = lambda h: slice(2 * GLA_KW + h * GLA_DV, 2 * GLA_KW + (h + 1) * GLA_DV)
    gs = lambda h: slice(2 * GLA_KW + GLA_VW + h * GLA_DV, 2 * GLA_KW + GLA_VW + (h + 1) * GLA_DV)
    st = [s_scr[bb * n_pairs + j] for bb in rows for j in range(n_pairs)]
    qm = [qd[bb][:, sl(h)] * masks[h % 2] for bb, h in heads]
    vh = [ps[bb][:, hs(h)] for bb, h in heads]
    attn = [jnp.where(incl, _bdot(qm[i], kh[bb][:, sl(h)], _NT), 0.0) for i, (bb, h) in enumerate(heads)]
    o = [_bdot(attn[i], vh[i]) + _bdot(qm[i], st[bb * n_pairs + h // 2]) for i, (bb, h) in enumerate(heads)]
    kv = [_bdot(kt[bb][:, sl(h)], vh[i], _TN) for i, (bb, h) in enumerate(heads)]
    for i, (bb, h) in enumerate(heads):
        of = o[i] * lax.rsqrt(jnp.mean(o[i] * o[i], axis=-1, keepdims=True) + EPS) * nw_ref[...]
        y_ref[bb, :, h * GLA_DV:(h + 1) * GLA_DV] = of * _silu(ps[bb][:, gs(h)])
    for bb in rows:
        for j in range(n_pairs):
            i0 = bb * GLA_HEADS + 2 * j
            pcol = _xdot_l(jnp.where(diag128, jnp.exp(bl[bb][:, j * LANES:(j + 1) * LANES]), 0.0), ones128)
            s_scr[bb * n_pairs + j] = pcol * st[bb * n_pairs + j] + jnp.concatenate(
                [kv[i0][0:GLA_DK], kv[i0 + 1][GLA_DK:2 * GLA_DK]], axis=0)

    @pl.when(c == nc - 1)
    def _():
        for bb in range(nb):
            for j in range(n_pairs):
                sf_ref[bb, j] = s_scr[bb * n_pairs + j]


def gla_mix(p_gla, s0, bsz, t, L, aup_pad, a_b, norm_w, nb=1):
    tp = -(-t // L) * L
    p3 = p_gla.reshape(bsz, t, GLA_PW)
    if tp != t:
        p3 = jnp.pad(p3, ((0, 0), (0, tp - t), (0, 0)))
    n_pairs = GLA_KW // LANES
    st_spec = pl.BlockSpec((nb, n_pairs, LANES, LANES), lambda b, c: (b, 0, 0, 0))
    y, s_fin = pl.pallas_call(
        functools.partial(_gla_kernel, L, t, nb),
        grid=(bsz // nb, tp // L),
        in_specs=[pl.BlockSpec((nb, L, GLA_PW), lambda b, c: (b, c, 0)),
                  pl.BlockSpec((LANES, GLA_KW), lambda b, c: (0, 0)),
                  pl.BlockSpec((1, GLA_KW), lambda b, c: (0, 0)),
                  pl.BlockSpec((1, GLA_DV), lambda b, c: (0, 0)),
                  st_spec],
        out_specs=[pl.BlockSpec((nb, L, GLA_VW), lambda b, c: (b, c, 0)), st_spec],
        out_shape=[jax.ShapeDtypeStruct((bsz, tp, GLA_VW), F32),
                   jax.ShapeDtypeStruct((bsz, n_pairs, LANES, LANES), F32)],
        scratch_shapes=[pltpu.VMEM((nb * n_pairs, LANES, LANES), F32)],
        compiler_params=_cparams(("parallel", "arbitrary")), name="gla",
    )(p3, aup_pad, a_b.reshape(1, GLA_KW), norm_w.reshape(1, GLA_DV),
      s0.reshape(bsz, n_pairs, LANES, LANES))
    return y[:, :t].reshape(bsz * t, GLA_VW), s_fin.reshape(bsz, GLA_HEADS, GLA_DK, GLA_DV)


GLA_LR = 16
D_FF_CHUNK = 1408
D_FF_EXPERT_CHUNK = 896
FFN_TOKENS = 1024
ROUTER_TOKENS = 1024
MOE_TOKENS = 2048


def _prepare_weights(w):
    win = w["e_w_in"][0]
    w_gla = jnp.pad(win[:, RSHIFT:], ((0, 0), (0, LANES - GLA_LR)))
    wo = w["o_w_in"][0]
    return {
        "e_w_rwkv": win[:, :RSHIFT].astype(BF16),
        "e_w_gla": w_gla.astype(BF16),
        "rwkv": rwkv_params(w, 0),
        "gla_aup": jnp.pad(w["e_gla_a_up"][0], ((0, LANES - GLA_LR), (0, 0))).astype(BF16),
        "e_wo_a": w["e_w_out"][0][:RW].astype(BF16),
        "e_wo_b": w["e_w_out"][0][RW:].astype(BF16),
        "ff_w1": w["e_ff_w1"][0].astype(BF16),
        "ff_w3": w["e_ff_w3"][0].astype(BF16),
        "ff_w2": w["e_ff_w2"][0].astype(BF16),
        "o_w_q": wo[:, :QW].astype(BF16),
        "o_w_kv": wo[:, QW:QW + 2 * KW].astype(BF16),
        "o_w_u": wo[:, QW + 2 * KW:].astype(BF16),
        "glu_w": w["o_glu_w"][0].astype(BF16),
        "o_wo_a": w["o_w_out"][0][:QW].astype(BF16),
        "o_wo_b": w["o_w_out"][0][QW:].astype(BF16),
        "moe_w1": w["o_moe_w1"][0].astype(BF16),
        "moe_w3": w["o_moe_w3"][0].astype(BF16),
        "moe_w2": w["o_moe_w2"][0].astype(BF16),
    }


def _trunk(x3, st, w, pw, tm, chunk, nb, s5_prep_t, prompt_bias):
    bsz, t, d = x3.shape
    m = bsz * t
    x = x3.reshape(m, d)
    p_r, p_g = norm_proj(x, w["e_norm1"][0], [pw["e_w_rwkv"], pw["e_w_gla"]], tm)
    ya, s_rwkv = rwkv_mix(p_r, st["shift"], st["rwkv"], bsz, t, chunk, tm, pw["rwkv"],
                          w["e_lnx_w"][0], w["e_lnx_b"][0], nb)
    s_shift = p_r.reshape(bsz, t, RSHIFT)[:, -1]
    yb, s_gla = gla_mix(p_g, st["gla"], bsz, t, chunk, pw["gla_aup"], w["e_gla_a_b"][0], w["e_gla_norm"][0], nb)
    x = out_proj(x, ya, yb, pw["e_wo_a"], pw["e_wo_b"], tm)
    x = ffn(x, w["e_norm2"][0], pw["ff_w1"], pw["ff_w3"], pw["ff_w2"], math.gcd(m, FFN_TOKENS), D_FF_CHUNK)
    q, kv, u = norm_proj_tiles(x, w["o_norm1"][0], [pw["o_w_q"], pw["o_w_kv"], pw["o_w_u"]], tm)
    if st["win_k"] is None:
        yc, nk, nv = swa_prompt(q, kv, bsz, t, w["o_q_norm"][0], w["o_k_norm"][0], prompt_bias, w["o_sinks"][0])
    else:
        yc, nk, nv = swa_decode(q, kv, st["win_k"].reshape(bsz, WINDOW, KW), st["win_v"].reshape(bsz, WINDOW, KW),
                                bsz, t, w["o_q_norm"][0], w["o_k_norm"][0], w["rel_table"], w["o_sinks"][0], 8)
    yd, s5r, s5i = s5_mix(u, st["s5_re"], st["s5_im"], bsz, t, s5_prep_t, w["o_d"][0], pw["glu_w"],
                          w["o_glu_b"][0], tm)
    x = out_proj(x, yc, yd, pw["o_wo_a"], pw["o_wo_b"], tm)
    t_router = min(m, ROUTER_TOKENS)
    hn, gates, pos, counts = router(x, w["o_norm2"][0], w["o_router_w"][0], w["o_router_b"][0], t_router)
    x = moe(x, hn, gates, pos, counts, pw["moe_w1"], pw["moe_w3"], pw["moe_w2"], t_router, min(m, MOE_TOKENS),
            D_FF_EXPERT_CHUNK)
    kv_shape = (bsz, WINDOW, N_KV_HEADS, ATT_HD)
    return (x.reshape(bsz, t, d), s_rwkv[None], s_shift[None], s_gla[None], nk.reshape(kv_shape)[None],
            nv.reshape(kv_shape)[None], s5r[None], s5i[None])


def kernel(x_prompt, x_sample, state_rwkv, state_shift, state_gla, cache_win_k, cache_win_v, state_s5_re,
           state_s5_im, rel_table, e_norm1, e_w_in, e_mu, e_w0, e_w_up, e_a0, e_a_up, e_g_up, e_k_k, e_k_a, e_r_k,
           e_lnx_w, e_lnx_b, e_gla_a_up, e_gla_a_b, e_gla_norm, e_w_out, e_norm2, e_ff_w1, e_ff_w3, e_ff_w2,
           o_norm1, o_w_in, o_q_norm, o_k_norm, o_sinks, o_a_re, o_a_im, o_log_dt, o_b_re, o_b_im, o_c_re, o_c_im,
           o_d, o_glu_w, o_glu_b, o_w_out, o_norm2, o_router_w, o_router_b, o_moe_w1, o_moe_w3, o_moe_w2):
    w = dict(rel_table=rel_table, e_norm1=e_norm1, e_w_in=e_w_in, e_mu=e_mu, e_w0=e_w0, e_w_up=e_w_up, e_a0=e_a0,
             e_a_up=e_a_up, e_g_up=e_g_up, e_k_k=e_k_k, e_k_a=e_k_a, e_r_k=e_r_k, e_lnx_w=e_lnx_w, e_lnx_b=e_lnx_b,
             e_gla_a_up=e_gla_a_up, e_gla_a_b=e_gla_a_b, e_gla_norm=e_gla_norm, e_w_out=e_w_out, e_norm2=e_norm2,
             e_ff_w1=e_ff_w1, e_ff_w3=e_ff_w3, e_ff_w2=e_ff_w2, o_norm1=o_norm1, o_w_in=o_w_in, o_q_norm=o_q_norm,
             o_k_norm=o_k_norm, o_sinks=o_sinks, o_a_re=o_a_re, o_a_im=o_a_im, o_log_dt=o_log_dt, o_b_re=o_b_re,
             o_b_im=o_b_im, o_c_re=o_c_re, o_c_im=o_c_im, o_d=o_d, o_glu_w=o_glu_w, o_glu_b=o_glu_b,
             o_w_out=o_w_out, o_norm2=o_norm2, o_router_w=o_router_w, o_router_b=o_router_b, o_moe_w1=o_moe_w1,
             o_moe_w3=o_moe_w3, o_moe_w2=o_moe_w2)
    pw = _prepare_weights(w)
    bp, tp, _ = x_prompt.shape
    bs, ts, _ = x_sample.shape
    qi = np.arange(WINDOW)[:, None]
    kj = np.arange(2 * WINDOW)[None, :]
    prompt_bias = rel_bias(rel_table, qi + WINDOW - kj)
    zeros = lambda *shape: jnp.zeros(shape, F32)
    st_p = {"rwkv": zeros(bp, RW // RWKV_HD, RWKV_HD, RWKV_HD), "shift": zeros(bp, RSHIFT),
            "gla": zeros(bp, GLA_HEADS, GLA_DK, GLA_DV), "win_k": None, "win_v": None,
            "s5_re": zeros(bp, S5_G, S5_N), "s5_im": zeros(bp, S5_G, S5_N)}
    st_s = {"rwkv": state_rwkv[0], "shift": state_shift[0], "gla": state_gla[0], "win_k": cache_win_k[0],
            "win_v": cache_win_v[0], "s5_re": state_s5_re[0], "s5_im": state_s5_im[0]}
    s5_p, s5_s = s5_prep(w, 0, (S5_CHUNK, ts))
    out_p = _trunk(x_prompt, st_p, w, pw, 512, 64, math.gcd(bp, 2), s5_p, prompt_bias)
    out_s = _trunk(x_sample, st_s, w, pw, bs * ts, 8, math.gcd(bs, 8), s5_s, None)
    res = [out_p[0], out_s[0]]
    for a, b in zip(out_p[1:], out_s[1:]):
        res += [a, b]
    return tuple(res)
```

```python
import functools
import math

import jax
import jax.numpy as jnp
import numpy as np
from jax import lax
from jax.experimental import pallas as pl
from jax.experimental.pallas import tpu as pltpu

F32 = jnp.float32
BF16 = jnp.bfloat16

LANES = 128
SUBLANES = 8
VMEM_LIMIT_BYTES = 56 * 1024 * 1024

EPS = 1e-6
RWKV_HD = 64
RWKV_GN_EPS = 64e-5
GLA_GATE_NORM = 16.0
ATT_HD = 64
WINDOW = 128
N_BUCKETS = 32
BUCKET_MAX_DIST = 128
S5_P = 16
S5_N = 64
S5_CHUNK = 16
TOP_K = 2


def _cparams(sem):
    return pltpu.CompilerParams(dimension_semantics=sem, vmem_limit_bytes=VMEM_LIMIT_BYTES)


_NN = (((1,), (0,)), ((), ()))
_NT = (((1,), (1,)), ((), ()))
_TN = (((0,), (0,)), ((), ()))


def _dg(a, b, dims):
    return lax.dot_general(a, b, dims, preferred_element_type=F32)


def _bdot(a, b, dims=_NN):
    return _dg(a.astype(BF16), b.astype(BF16), dims)


def _split(a, n):
    terms = []
    r = a
    for _ in range(n):
        t = r.astype(BF16)
        terms.append(t)
        r = r - t.astype(F32)
    return terms


def _hdot(a, b, dims=_NN):
    a0, a1 = _split(a, 2)
    b0, b1 = _split(b, 2)
    return _dg(a0, b0, dims) + (_dg(a0, b1, dims) + _dg(a1, b0, dims))


def _xdot_l(a, e, dims=_NN):
    e = e.astype(BF16)
    a0, a1, a2 = _split(a, 3)
    return _dg(a0, e, dims) + (_dg(a1, e, dims) + _dg(a2, e, dims))


def _xdot_r(e, b, dims=_NN):
    e = e.astype(BF16)
    b0, b1, b2 = _split(b, 3)
    return _dg(e, b0, dims) + (_dg(e, b1, dims) + _dg(e, b2, dims))


def _iota(shape, axis):
    return lax.broadcasted_iota(jnp.int32, shape, axis)


def _seg_ones(n, seg):
    r = _iota((n, n), 0) // seg
    c = _iota((n, n), 1) // seg
    return jnp.where(r == c, 1.0, 0.0).astype(BF16)


def _seg_sum(x, seg):
    n = x.shape[-1]
    return _xdot_l(x, _seg_ones(n, seg))


def _sigmoid(x):
    return 1.0 / (1.0 + jnp.exp(-x))


def _silu(x):
    return x * _sigmoid(x)


def _softplus(x):
    return jnp.maximum(x, 0.0) + jnp.log(1.0 + jnp.exp(-jnp.abs(x)))


def _tri_incl(n):
    r = _iota((n, n), 0)
    c = _iota((n, n), 1)
    return jnp.where(c <= r, 1.0, 0.0).astype(BF16)


def _rms(x, g):
    return x * lax.rsqrt(jnp.mean(x * x, axis=-1, keepdims=True) + EPS) * g


def _norm_proj_kernel(n_w, x_ref, g_ref, *refs):
    xn = _rms(x_ref[...], g_ref[...]).astype(BF16)
    for w_ref, o_ref in zip(refs[:n_w], refs[n_w:]):
        o_ref[...] = _dg(xn, w_ref[...], _NN)


def norm_proj(x, g, ws_bf16, tm):
    m, d = x.shape
    return pl.pallas_call(
        functools.partial(_norm_proj_kernel, len(ws_bf16)),
        grid=(m // tm,),
        in_specs=[pl.BlockSpec((tm, d), lambda i: (i, 0)),
                  pl.BlockSpec((1, d), lambda i: (0, 0))]
                 + [pl.BlockSpec(w.shape, lambda i: (0, 0)) for w in ws_bf16],
        out_specs=[pl.BlockSpec((tm, w.shape[1]), lambda i: (i, 0)) for w in ws_bf16],
        out_shape=[jax.ShapeDtypeStruct((m, w.shape[1]), F32) for w in ws_bf16],
        compiler_params=_cparams(("parallel",)), name="norm_proj",
    )(x, g.reshape(1, d), *ws_bf16)


CHUNK_ROW_TOKENS = 16


def _to_chunk_rows(src_ref, dst_ref, tpr):
    rows = src_ref.shape[1] // tpr
    for q in range(dst_ref.shape[0]):
        for i in range(CHUNK_ROW_TOKENS):
            cols = slice(i * LANES, (i + 1) * LANES)
            if i < tpr:
                dst_ref[q, :, cols] = src_ref[q, pl.ds(i, rows, stride=tpr), :]
            else:
                dst_ref[q, :, cols] = jnp.zeros((rows, LANES), dst_ref.dtype)


def _from_chunk_rows(src_ref, dst_ref, tpr):
    rows = dst_ref.shape[1] // tpr
    for q in range(src_ref.shape[0]):
        for i in range(tpr):
            dst_ref[q, pl.ds(i, rows, stride=tpr), :] = src_ref[q, :, i * LANES:(i + 1) * LANES]


def _norm_proj_tiles_kernel(n_w, tpr, x_ref, g_ref, *refs):
    xn = _rms(x_ref[...], g_ref[...]).astype(BF16)
    for w_ref, o_ref in zip(refs[:n_w - 1], refs[n_w:]):
        o_ref[...] = _dg(xn, w_ref[...], _NN)
    u_ref, u_scr = refs[-2], refs[-1]
    u = _dg(xn, refs[n_w - 1][...], _NN)
    for q in range(u_scr.shape[0]):
        u_scr[q] = u[:, q * LANES:(q + 1) * LANES]
    _to_chunk_rows(u_scr, u_ref, tpr)


def norm_proj_tiles(x, g, ws_bf16, tm, tpr):
    m, d = x.shape
    nu = ws_bf16[-1].shape[1]
    nt = nu // LANES
    cw = CHUNK_ROW_TOKENS * LANES
    return pl.pallas_call(
        functools.partial(_norm_proj_tiles_kernel, len(ws_bf16), tpr),
        grid=(m // tm,),
        in_specs=[pl.BlockSpec((tm, d), lambda i: (i, 0)),
                  pl.BlockSpec((1, d), lambda i: (0, 0))]
                 + [pl.BlockSpec(w.shape, lambda i: (0, 0)) for w in ws_bf16],
        out_specs=[pl.BlockSpec((tm, w.shape[1]), lambda i: (i, 0)) for w in ws_bf16[:-1]]
                  + [pl.BlockSpec((nt, tm // tpr, cw), lambda i: (0, i, 0))],
        out_shape=[jax.ShapeDtypeStruct((m, w.shape[1]), F32) for w in ws_bf16[:-1]]
                  + [jax.ShapeDtypeStruct((nt, m // tpr, cw), F32)],
        scratch_shapes=[pltpu.VMEM((nt, tm, LANES), F32)],
        compiler_params=_cparams(("parallel",)), name="norm_proj_tiles",
    )(x, g.reshape(1, d), *ws_bf16)


def _out_proj_kernel(x_ref, ya_ref, yb_ref, wa_ref, wb_ref, o_ref):
    o_ref[...] = (x_ref[...] + _dg(ya_ref[...].astype(BF16), wa_ref[...], _NN)
                  + _dg(yb_ref[...].astype(BF16), wb_ref[...], _NN))


def out_proj(x, ya, yb, wa, wb, tm):
    m, d = x.shape
    return pl.pallas_call(
        _out_proj_kernel,
        grid=(m // tm,),
        in_specs=[pl.BlockSpec((tm, d), lambda i: (i, 0)),
                  pl.BlockSpec((tm, ya.shape[1]), lambda i: (i, 0)),
                  pl.BlockSpec((tm, yb.shape[1]), lambda i: (i, 0)),
                  pl.BlockSpec(wa.shape, lambda i: (0, 0)),
                  pl.BlockSpec(wb.shape, lambda i: (0, 0))],
        out_specs=pl.BlockSpec((tm, d), lambda i: (i, 0)),
        out_shape=jax.ShapeDtypeStruct((m, d), F32),
        compiler_params=_cparams(("parallel",)), name="out_proj",
    )(x, ya, yb, wa, wb)


def _ffn_kernel(x_ref, g_ref, w1_ref, w3_ref, w2_ref, o_ref, xn_scr):
    j = pl.program_id(1)

    @pl.when(j == 0)
    def _():
        x = x_ref[...]
        xn_scr[...] = _rms(x, g_ref[...]).astype(BF16)
        o_ref[...] = x

    xn = xn_scr[...]
    h = _silu(_dg(xn, w1_ref[...], _NN)) * _dg(xn, w3_ref[...], _NN)
    o_ref[...] += _dg(h.astype(BF16), w2_ref[...], _NN)


def ffn(x, g, w1, w3, w2, tm, fc):
    m, d = x.shape
    dff = w1.shape[1]
    return pl.pallas_call(
        _ffn_kernel,
        grid=(m // tm, dff // fc),
        in_specs=[pl.BlockSpec((tm, d), lambda i, j: (i, 0)),
                  pl.BlockSpec((1, d), lambda i, j: (0, 0)),
                  pl.BlockSpec((d, fc), lambda i, j: (0, j)),
                  pl.BlockSpec((d, fc), lambda i, j: (0, j)),
                  pl.BlockSpec((fc, d), lambda i, j: (j, 0))],
        out_specs=pl.BlockSpec((tm, d), lambda i, j: (i, 0)),
        out_shape=jax.ShapeDtypeStruct((m, d), F32),
        scratch_shapes=[pltpu.VMEM((tm, d), BF16)],
        compiler_params=_cparams(("parallel", "arbitrary")), name="ffn",
    )(x, g.reshape(1, d), w1, w3, w2)


def _router_kernel(n_exp, x_ref, g_ref, rw_ref, rb_ref, hn_ref, gate_ref, pos_ref, cnt_ref):
    xn = _rms(x_ref[...], g_ref[...])
    hn_ref[...] = xn.astype(BF16)
    logits = _hdot(xn, rw_ref[...]) + rb_ref[...]
    lane = _iota(logits.shape, 1)
    logits = jnp.where(lane < n_exp, logits, -jnp.inf)
    m1 = jnp.max(logits, axis=-1, keepdims=True)
    i1 = jnp.min(jnp.where(logits == m1, lane, LANES), axis=-1, keepdims=True)
    rest = jnp.where(lane == i1, -jnp.inf, logits)
    m2 = jnp.max(rest, axis=-1, keepdims=True)
    i2 = jnp.min(jnp.where(rest == m2, lane, LANES), axis=-1, keepdims=True)
    e2 = jnp.exp(m2 - m1)
    g1 = 1.0 / (1.0 + e2)
    g2 = e2 / (1.0 + e2)
    pick1 = lane == i1
    pick2 = lane == i2
    gate_ref[...] = jnp.where(pick1, g1, 0.0) + jnp.where(pick2, g2, 0.0)
    tm = logits.shape[0]
    sel = jnp.where(jnp.logical_or(pick1, pick2), 1.0, 0.0).astype(BF16)
    tr = _iota((tm, tm), 0)
    tc = _iota((tm, tm), 1)
    upper = jnp.where(tr <= tc, 1.0, 0.0).astype(BF16)
    eye = jnp.where(tr == tc, 1.0, 0.0).astype(BF16)
    rank_t = _dg(sel, upper, _TN)
    sel_t = _dg(sel, eye, _TN)
    pos_t = jnp.where(sel_t > 0.5, rank_t - 1.0, -1.0)
    pos_ref[...] = pos_t[0:SUBLANES, :]
    cnt_ref[0] = jnp.sum(sel.astype(F32), axis=0, keepdims=True)


def router(x, g, rw, rb, tm):
    m, d = x.shape
    n_exp = rw.shape[1]
    assert n_exp <= SUBLANES
    rw_pad = jnp.pad(rw, ((0, 0), (0, LANES - n_exp)))
    rb_pad = jnp.pad(rb, (0, LANES - n_exp)).reshape(1, LANES)
    return pl.pallas_call(
        functools.partial(_router_kernel, n_exp),
        grid=(m // tm,),
        in_specs=[pl.BlockSpec((tm, d), lambda i: (i, 0)),
                  pl.BlockSpec((1, d), lambda i: (0, 0)),
                  pl.BlockSpec((d, LANES), lambda i: (0, 0)),
                  pl.BlockSpec((1, LANES), lambda i: (0, 0))],
        out_specs=[pl.BlockSpec((tm, d), lambda i: (i, 0)),
                   pl.BlockSpec((tm, LANES), lambda i: (i, 0)),
                   pl.BlockSpec((SUBLANES, tm), lambda i: (0, i)),
                   pl.BlockSpec((1, 1, LANES), lambda i: (i, 0, 0))],
        out_shape=[jax.ShapeDtypeStruct((m, d), BF16), jax.ShapeDtypeStruct((m, LANES), F32),
                   jax.ShapeDtypeStruct((SUBLANES, m), F32),
                   jax.ShapeDtypeStruct((m // tm, 1, LANES), F32)],
        compiler_params=_cparams(("parallel",)), name="router",
    )(x, g.reshape(1, d), rw_pad, rb_pad)


MOE_ROWS = 128


def _moe_kernel(n_exp, cnt_ref, x_hbm, hn_ref, gate_ref, pos_ref, w1_ref, w3_ref, w2_ref, o_ref, xs_scr, y_scr):
    i = pl.program_id(0)
    e = pl.program_id(1)
    j = pl.program_id(2)
    nj = pl.num_programs(2)
    tm = hn_ref.shape[0]
    n_small = (cnt_ref[i * n_exp + e] + (MOE_ROWS - 1)) // MOE_ROWS

    @pl.when(jnp.logical_and(e == 0, j == 0))
    def _():
        pltpu.sync_copy(x_hbm.at[pl.ds(pl.multiple_of(i * tm, tm), tm), :], o_ref)

    def select(start, nrows):
        pos = pos_ref[pl.ds(e, 1), :]
        want = (start + _iota((nrows, tm), 0)).astype(F32)
        return jnp.where(pos == want, 1.0, 0.0).astype(BF16)

    def blocks(body):
        big = 4 * MOE_ROWS
        assert tm % big == 0

        def run_big(blk, carry):
            body(pl.multiple_of(blk * big, big), big)
            return carry
        lax.fori_loop(0, n_small // 4, run_big, 0)
        done = (n_small // 4) * big

        @pl.when(n_small % 4 >= 2)
        def _():
            body(pl.multiple_of(done, big), 2 * MOE_ROWS)

        @pl.when(n_small % 2 == 1)
        def _():
            body(pl.multiple_of(done + (n_small % 4 // 2) * 2 * MOE_ROWS, MOE_ROWS), MOE_ROWS)

    @pl.when(j == 0)
    def _():
        def gather(start, nrows):
            rows = pl.ds(start, nrows)
            xs_scr[rows, :] = _dg(select(start, nrows), hn_ref[...], _NN).astype(BF16)
            y_scr[rows, :] = jnp.zeros((nrows, y_scr.shape[1]), F32)
        blocks(gather)

    def expert(start, nrows):
        rows = pl.ds(start, nrows)
        xs = xs_scr[rows, :]
        h = _silu(_dg(xs, w1_ref[0], _NN)) * _dg(xs, w3_ref[0], _NN)
        y_scr[rows, :] += _dg(h.astype(BF16), w2_ref[0], _NN)
    blocks(expert)

    @pl.when(j == nj - 1)
    def _():
        gt = gate_ref[...]
        gcol = jnp.sum(jnp.where(_iota(gt.shape, 1) == e, gt, 0.0), axis=-1, keepdims=True)

        def scatter(start, nrows):
            o_ref[...] += gcol * _dg(select(start, nrows), y_scr[pl.ds(start, nrows), :].astype(BF16), _TN)
        blocks(scatter)


def moe(x, hn, gates, pos, counts, w1, w3, w2, tm_router, tm, fc):
    m, d = x.shape
    n_exp, _, dff = w1.shape
    nj = dff // fc
    ratio = tm // tm_router
    cnt = counts[:, 0, :n_exp].astype(jnp.int32).reshape(m // tm, ratio, n_exp)
    before = (jnp.cumsum(cnt, axis=1) - cnt).astype(F32)
    shift = jnp.repeat(before.reshape(m // tm_router, n_exp).T, tm_router, axis=1)
    shift = jnp.pad(shift, ((0, SUBLANES - n_exp), (0, 0)))
    pos = jnp.where(pos >= 0, pos + shift, pos)
    cnt = cnt.sum(axis=1).reshape(-1)
    grid_spec = pltpu.PrefetchScalarGridSpec(
        num_scalar_prefetch=1,
        grid=(m // tm, n_exp, nj),
        in_specs=[pl.BlockSpec(memory_space=pl.ANY),
                  pl.BlockSpec((tm, d), lambda i, e, j, c: (i, 0)),
                  pl.BlockSpec((tm, LANES), lambda i, e, j, c: (i, 0)),
                  pl.BlockSpec((SUBLANES, tm), lambda i, e, j, c: (0, i)),
                  pl.BlockSpec((1, d, fc), lambda i, e, j, c: (e, 0, j)),
                  pl.BlockSpec((1, d, fc), lambda i, e, j, c: (e, 0, j)),
                  pl.BlockSpec((1, fc, d), lambda i, e, j, c: (e, j, 0))],
        out_specs=pl.BlockSpec((tm, d), lambda i, e, j, c: (i, 0)),
        scratch_shapes=[pltpu.VMEM((tm, d), BF16), pltpu.VMEM((tm, d), F32)])
    return pl.pallas_call(
        functools.partial(_moe_kernel, n_exp),
        grid_spec=grid_spec,
        out_shape=jax.ShapeDtypeStruct((m, d), F32),
        compiler_params=_cparams(("parallel", "arbitrary", "arbitrary")), name="moe",
    )(cnt, x, hn, gates, pos, w1, w3, w2)


RW = 512
RSHIFT = 1792


def _rwkv_pre_kernel(t_per_batch, tm, p_ref, pb_ref, prev_ref, mu_ref, w0_ref, wup_ref, a0_ref, aup_ref,
                     gup_ref, kk_ref, ka_ref, rk_ref,
                     r_o, lw_o, k_o, v_o, al_o, be_o, g_o, bo_o):
    i = pl.program_id(0)
    p = p_ref[...]
    rolled = pltpu.roll(p, 1, 0)
    row = _iota((tm, 1), 0)
    rolled = jnp.where(row == 0, pb_ref[SUBLANES - 1:SUBLANES, :], rolled)
    if t_per_batch >= tm:
        first = (i * tm) % t_per_batch == 0
        is_start = jnp.logical_and(row == 0, first)
        prev = jnp.where(is_start, prev_ref[0], rolled)
    else:
        is_start = (row % t_per_batch) == 0
        prev = jnp.where(is_start, prev_ref[...], rolled)
    xs = p + (prev - p) * mu_ref[...]
    r = xs[:, 0:RW]
    k = xs[:, RW:2 * RW]
    v = xs[:, 2 * RW:3 * RW]
    lr = xs[:, 3 * RW:3 * RW + LANES]
    gd = xs[:, 3 * RW + LANES:3 * RW + 2 * LANES]
    w_pre = w0_ref[...] + _bdot(jnp.tanh(lr), wup_ref[...])
    logw = -jnp.exp(-_softplus(-w_pre) - 0.5)
    a = _sigmoid(a0_ref[...] + _bdot(lr, aup_ref[...]))
    g = _bdot(_sigmoid(gd), gup_ref[...])
    kkr = k * kk_ref[...]
    kk = kkr / jnp.maximum(jnp.sqrt(_seg_sum(kkr * kkr, RWKV_HD)), 1e-12)
    k2 = k * (1.0 + (a - 1.0) * ka_ref[...])
    bonus = _seg_sum(r * k2 * rk_ref[...], RWKV_HD) * v
    r_o[...] = r
    lw_o[...] = logw
    k_o[...] = k2
    v_o[...] = v
    al_o[...] = -kk
    be_o[...] = kk * a
    g_o[...] = g
    bo_o[...] = bonus


def rwkv_pre(p_full, prev_rows, t_per_batch, tm, prm):
    m = p_full.shape[0]
    nb8 = tm // SUBLANES
    row_spec = pl.BlockSpec((1, RW), lambda i: (0, 0))
    if t_per_batch >= tm:
        prev_spec = pl.BlockSpec((1, 1, RSHIFT), lambda i: ((i * tm) // t_per_batch, 0, 0))
    else:
        prev_spec = pl.BlockSpec((tm, RSHIFT), lambda i: (i, 0))
    out_sds = jax.ShapeDtypeStruct((m, RW), F32)
    out_spec = pl.BlockSpec((tm, RW), lambda i: (i, 0))
    return pl.pallas_call(
        functools.partial(_rwkv_pre_kernel, t_per_batch, tm),
        grid=(m // tm,),
        in_specs=[pl.BlockSpec((tm, RSHIFT), lambda i: (i, 0)),
                  pl.BlockSpec((SUBLANES, RSHIFT), lambda i: (jnp.maximum(i * nb8 - 1, 0), 0)),
                  prev_spec,
                  pl.BlockSpec((1, RSHIFT), lambda i: (0, 0)),
                  row_spec,
                  pl.BlockSpec((LANES, RW), lambda i: (0, 0)),
                  row_spec,
                  pl.BlockSpec((LANES, RW), lambda i: (0, 0)),
                  pl.BlockSpec((LANES, RW), lambda i: (0, 0)),
                  row_spec, row_spec, row_spec],
        out_specs=[out_spec] * 8,
        out_shape=[out_sds] * 8,
        compiler_params=_cparams(("parallel",)), name="rwkv_pre",
    )(p_full, p_full, prev_rows, prm["mu"], prm["w0"], prm["wup"], prm["a0"], prm["aup"], prm["gup"],
      prm["k_k"], prm["k_a"], prm["r_k"])


_RW_DOT_A = _bdot
_RW_DOT_T = _bdot
_RW_DOT_W = _bdot
_RW_DOT_S = _bdot


def _rwkv_scan_kernel(L, nb, n_pairs, r_ref, lw_ref, k_ref, v_ref, al_ref, be_ref, s0_ref,
                      y_ref, sf_ref, s_scr):
    c = pl.program_id(1)
    nc = pl.num_programs(1)
    L2 = 2 * L
    lane = _iota((1, LANES), 1)
    m0 = jnp.where(lane < RWKV_HD, 1.0, 0.0)
    m1 = 1.0 - m0
    rr = _iota((L2, L2), 0)
    cc = _iota((L2, L2), 1)
    same = (rr // L) == (cc // L)
    strict = jnp.logical_and(same, (cc % L) < (rr % L))
    incl = jnp.logical_and(same, (cc % L) <= (rr % L))
    eye2 = jnp.where(rr == cc, 1.0, 0.0)
    r128 = _iota((LANES, LANES), 0)
    c128 = _iota((LANES, LANES), 1)
    blk128 = (r128 // RWKV_HD) == (c128 // RWKV_HD)
    diag128 = r128 == c128
    fmat = jnp.where(_iota((LANES, RWKV_HD), 0) % RWKV_HD == _iota((LANES, RWKV_HD), 1), 1.0, 0.0)
    tri = _tri_incl(L)

    def bd(x):
        return jnp.concatenate([x * m0, x * m1], axis=0)

    @pl.when(c == 0)
    def _():
        for bb in range(nb):
            for j in range(n_pairs):
                s0 = s0_ref[bb, j]
                st = _xdot_r(fmat, s0, _NT)
                s_scr[bb * n_pairs + j] = jnp.where(blk128, st, 0.0)

    pairs = range(nb * n_pairs)
    bbs = [i // n_pairs for i in pairs]
    sls = [slice((i % n_pairs) * LANES, (i % n_pairs + 1) * LANES) for i in pairs]
    lw = [lw_ref[bbs[j], :, sls[j]] for j in pairs]
    b = [_xdot_r(tri, x) for x in lw]
    bl = [x[L - 1:L, :] for x in b]
    e_b = [jnp.exp(x) for x in b]
    e_nb = [jnp.exp(-x) for x in b]
    e_lb = [jnp.exp(bl[j] - b[j]) for j in pairs]
    at = [bd(al_ref[bbs[j], :, sls[j]] * jnp.exp(b[j] - lw[j])) for j in pairs]
    rt = [bd(r_ref[bbs[j], :, sls[j]] * e_b[j]) for j in pairs]
    bh = [bd(be_ref[bbs[j], :, sls[j]] * e_nb[j]) for j in pairs]
    kh = [bd(k_ref[bbs[j], :, sls[j]] * e_nb[j]) for j in pairs]
    bt = [bd(be_ref[bbs[j], :, sls[j]] * e_lb[j]) for j in pairs]
    kt = [bd(k_ref[bbs[j], :, sls[j]] * e_lb[j]) for j in pairs]
    vb = [bd(v_ref[bbs[j], :, sls[j]]) for j in pairs]
    if L2 % LANES == 0:
        gq = [_RW_DOT_A(jnp.concatenate([at[j], rt[j]], axis=0), jnp.concatenate([bh[j], kh[j]], axis=0), _NT)
              for j in pairs]
        a_ab = [jnp.where(strict, g[0:L2, 0:L2], 0.0) for g in gq]
        a_ak = [jnp.where(strict, g[0:L2, L2:2 * L2], 0.0) for g in gq]
        a_rb = [jnp.where(incl, g[L2:2 * L2, 0:L2], 0.0) for g in gq]
        a_rk = [jnp.where(incl, g[L2:2 * L2, L2:2 * L2], 0.0) for g in gq]
    else:
        a_ab = [jnp.where(strict, _RW_DOT_A(at[j], bh[j], _NT), 0.0) for j in pairs]
        a_ak = [jnp.where(strict, _RW_DOT_A(at[j], kh[j], _NT), 0.0) for j in pairs]
        a_rb = [jnp.where(incl, _RW_DOT_A(rt[j], bh[j], _NT), 0.0) for j in pairs]
        a_rk = [jnp.where(incl, _RW_DOT_A(rt[j], kh[j], _NT), 0.0) for j in pairs]
    x = a_ab
    tinv = [eye2 + a for a in a_ab]
    span = 2
    while span < L:
        x = [_RW_DOT_T(xx, xx) for xx in x]
        tinv = [tinv[j] + _RW_DOT_T(tinv[j], x[j]) for j in pairs]
        span *= 2
    akv = [_RW_DOT_W(a_ak[j], vb[j]) for j in pairs]
    wuv = [_RW_DOT_W(tinv[j], jnp.concatenate([at[j], akv[j]], axis=1)) for j in pairs]
    mn = [_RW_DOT_W(bt[j], wuv[j], _TN) for j in pairs]
    qy = [_RW_DOT_W(a_rb[j], wuv[j]) for j in pairs]
    mm = [jnp.where(diag128, jnp.exp(bl[j]), 0.0) + mn[j][:, 0:LANES] for j in pairs]
    nn = [mn[j][:, LANES:2 * LANES] + _RW_DOT_W(kt[j], vb[j], _TN) for j in pairs]
    q = [rt[j] + qy[j][:, 0:LANES] for j in pairs]
    yv = [qy[j][:, LANES:2 * LANES] + _RW_DOT_W(a_rk[j], vb[j]) for j in pairs]
    s = [s_scr[j] for j in pairs]
    for j in pairs:
        ybd = _RW_DOT_S(q[j], s[j]) + yv[j]
        y_ref[bbs[j], :, sls[j]] = ybd[0:L] + ybd[L:L2]
    for j in pairs:
        s_scr[j] = _RW_DOT_S(mm[j], s[j]) + nn[j]

    @pl.when(c == nc - 1)
    def _():
        for i in pairs:
            sf_ref[bbs[i], i % n_pairs] = _xdot_l(s_scr[i], fmat, _TN)


def rwkv_scan(ops, s0, L, nb):
    bsz, t, _ = ops[0].shape
    n_pairs = RW // LANES
    blk = pl.BlockSpec((nb, L, RW), lambda b, c: (b, c, 0))
    st_spec = pl.BlockSpec((nb, n_pairs, LANES, RWKV_HD), lambda b, c: (b, 0, 0, 0))
    return pl.pallas_call(
        functools.partial(_rwkv_scan_kernel, L, nb, n_pairs),
        grid=(bsz // nb, t // L),
        in_specs=[blk] * 6 + [st_spec],
        out_specs=[blk, st_spec],
        out_shape=[jax.ShapeDtypeStruct((bsz, t, RW), F32),
                   jax.ShapeDtypeStruct((bsz, n_pairs, LANES, RWKV_HD), F32)],
        scratch_shapes=[pltpu.VMEM((nb * n_pairs, LANES, LANES), F32)],
        compiler_params=_cparams(("parallel", "arbitrary")), name="rwkv_scan",
    )(*ops, s0)


def _rwkv_post_kernel(y_ref, bo_ref, g_ref, lw_ref, lb_ref, o_ref):
    y = y_ref[...]
    mean = _seg_sum(y, RWKV_HD) * (1.0 / RWKV_HD)
    d = y - mean
    var = _seg_sum(d * d, RWKV_HD) * (1.0 / RWKV_HD)
    yn = d * lax.rsqrt(var + RWKV_GN_EPS) * lw_ref[...] + lb_ref[...]
    o_ref[...] = (yn + bo_ref[...]) * g_ref[...]


def rwkv_post(y, bonus, g, lnx_w, lnx_b, tm):
    m = y.shape[0]
    spec = pl.BlockSpec((tm, RW), lambda i: (i, 0))
    row = pl.BlockSpec((1, RW), lambda i: (0, 0))
    return pl.pallas_call(
        _rwkv_post_kernel,
        grid=(m // tm,),
        in_specs=[spec, spec, spec, row, row],
        out_specs=spec,
        out_shape=jax.ShapeDtypeStruct((m, RW), F32),
        compiler_params=_cparams(("parallel",)), name="rwkv_post",
    )(y, bonus, g, lnx_w.reshape(1, RW), lnx_b.reshape(1, RW))


def rwkv_params(w, i):
    z64 = jnp.zeros((RWKV_HD, RW), F32)
    return {
        "mu": w["e_mu"][i].reshape(1, RSHIFT),
        "w0": w["e_w0"][i].reshape(1, RW),
        "wup": jnp.concatenate([w["e_w_up"][i], z64], 0).astype(BF16),
        "a0": w["e_a0"][i].reshape(1, RW),
        "aup": jnp.concatenate([z64, w["e_a_up"][i]], 0).astype(BF16),
        "gup": w["e_g_up"][i].astype(BF16),
        "k_k": w["e_k_k"][i].reshape(1, RW),
        "k_a": w["e_k_a"][i].reshape(1, RW),
        "r_k": w["e_r_k"][i].reshape(1, RW),
    }


def rwkv_mix(p_flat, prev, s0, bsz, t, L, tm, prm, lnx_w, lnx_b, nb=1):
    m = bsz * t
    if t >= tm:
        prev_rows = prev.reshape(bsz, 1, RSHIFT)
    else:
        prev_rows = jnp.repeat(prev, t, axis=0)
    r, lw, k, v, al, be, g, bonus = rwkv_pre(p_flat, prev_rows, t, tm, prm)
    tp = -(-t // L) * L
    ops = [z.reshape(bsz, t, RW) for z in (r, lw, k, v, al, be)]
    if tp != t:
        ops = [jnp.pad(z, ((0, 0), (0, tp - t), (0, 0))) for z in ops]
    y, s_fin = rwkv_scan(ops, s0.reshape(bsz, RW // LANES, LANES, RWKV_HD), L, nb)
    y = y[:, :t].reshape(m, RW)
    ya = rwkv_post(y, bonus, g, lnx_w, lnx_b, tm)
    return ya, s_fin.reshape(bsz, RW // RWKV_HD, RWKV_HD, RWKV_HD)


N_Q_HEADS = 8
N_KV_HEADS = 2
Q_PER_KV = N_Q_HEADS // N_KV_HEADS
QW = N_Q_HEADS * ATT_HD
KW = N_KV_HEADS * ATT_HD
NEG = -1e30


def _t5_bucket_np(dist):
    n = np.maximum(dist, 0)
    max_exact = N_BUCKETS // 2
    nf = np.maximum(n, 1).astype(np.float32)
    large = max_exact + (np.log(nf / np.float32(max_exact)) / np.float32(math.log(BUCKET_MAX_DIST / max_exact))
                         * np.float32(N_BUCKETS - max_exact)).astype(np.int32)
    large = np.minimum(large, N_BUCKETS - 1)
    return np.where(n < max_exact, n, large)


def _bias_kernel(rt_ref, oh_ref, o_ref):
    o_ref[...] = _xdot_l(rt_ref[...], oh_ref[...])


def rel_bias(rel_table, dist):
    bucket = _t5_bucket_np(dist).reshape(-1)
    n = bucket.shape[0]
    onehot = jnp.asarray((np.arange(N_BUCKETS)[:, None] == bucket[None, :]).astype(np.float32), BF16)
    out = pl.pallas_call(
        _bias_kernel,
        out_shape=jax.ShapeDtypeStruct((N_Q_HEADS, n), F32),
    )(rel_table.T, onehot)
    return out.reshape((N_Q_HEADS,) + dist.shape)


def _head_norm(x, w_row):
    return x * lax.rsqrt(_seg_sum(x * x, ATT_HD) * (1.0 / ATT_HD) + EPS) * w_row


def _swa_prompt_kernel(nqb, q_ref, kvc_ref, kvp_ref, qw_ref, kw_ref, bias_ref, sink_ref, o_ref, ko_ref, vo_ref):
    i = pl.program_id(1)
    qn = _head_norm(q_ref[0], qw_ref[...])
    kvc = kvc_ref[0]
    kvp = kvp_ref[0]
    kn = [_head_norm(kvp[:, 0:KW], kw_ref[...])]
    vs = [kvp[:, KW:2 * KW]]
    kcn = _head_norm(kvc[:, 0:KW], kw_ref[...])
    for j in range(nqb):
        kn.append(kcn[j * WINDOW:(j + 1) * WINDOW])
        vs.append(kvc[j * WINDOW:(j + 1) * WINDOW, KW:2 * KW])
    kcat = [jnp.concatenate([kn[j], kn[j + 1]], axis=0).astype(BF16) for j in range(nqb)]
    vcat = [jnp.concatenate([vs[j], vs[j + 1]], axis=0).astype(BF16) for j in range(nqb)]
    qi = _iota((WINDOW, 2 * WINDOW), 0)
    kj = _iota((WINDOW, 2 * WINDOW), 1)
    dist = qi + WINDOW - kj
    band = jnp.logical_and(dist >= 0, dist < WINDOW)
    valid = [jnp.logical_and(band, jnp.logical_or(kj >= WINDOW, i > 0))] + [band] * (nqb - 1)
    lane = _iota((1, LANES), 1)
    masks = (jnp.where(lane < ATT_HD, 1.0, 0.0), jnp.where(lane < ATT_HD, 0.0, 1.0))
    scale = ATT_HD ** -0.5
    heads = range(N_Q_HEADS)
    units = [(j, h) for j in range(nqb) for h in heads]
    kv_of = lambda h: h // Q_PER_KV
    qts = [qn[j * WINDOW:(j + 1) * WINDOW, (h // 2) * LANES:(h // 2 + 1) * LANES] for j, h in units]
    qts = [pltpu.roll(qts[u], ATT_HD, 1) if h % 2 != kv_of(h) else qts[u] for u, (j, h) in enumerate(units)]
    qms = [(qts[u] * masks[kv_of(h)]).astype(BF16) for u, (j, h) in enumerate(units)]
    s_all = [_dg(jnp.concatenate(qms[j * N_Q_HEADS:(j + 1) * N_Q_HEADS], axis=0), kcat[j], _NT) * scale
             for j in range(nqb)]
    logits = [jnp.where(valid[j], s_all[j][h * WINDOW:(h + 1) * WINDOW] + bias_ref[h], NEG) for j, h in units]
    sinks = [sink_ref[h:h + 1, 0:1] for j, h in units]
    mx = [jnp.maximum(jnp.max(logits[u], axis=-1, keepdims=True), sinks[u]) for u in range(len(units))]
    pr = [jnp.exp(logits[u] - mx[u]) for u in range(len(units))]
    den = [jnp.sum(pr[u], axis=-1, keepdims=True) + jnp.exp(sinks[u] - mx[u]) for u in range(len(units))]
    probs = [(pr[u] * (1.0 / den[u])).astype(BF16) for u in range(len(units))]
    o_all = [_dg(jnp.concatenate(probs[j * N_Q_HEADS:(j + 1) * N_Q_HEADS], axis=0), vcat[j], _NN)
             for j in range(nqb)]
    os_ = [o_all[j][h * WINDOW:(h + 1) * WINDOW] for j, h in units]
    os_ = [pltpu.roll(os_[u], ATT_HD, 1) if h % 2 != kv_of(h) else os_[u] for u, (j, h) in enumerate(units)]
    for j in range(nqb):
        for jq in range(QW // LANES):
            o_ref[0, j * WINDOW:(j + 1) * WINDOW, jq * LANES:(jq + 1) * LANES] = (
                os_[j * N_Q_HEADS + 2 * jq] * masks[0] + os_[j * N_Q_HEADS + 2 * jq + 1] * masks[1])
    ko_ref[0] = kn[nqb]
    vo_ref[0] = vs[nqb]


SWA_QB = 2


def swa_prompt(q, kv, bsz, t, q_norm, k_norm, bias, sinks):
    nqb = math.gcd(t // WINDOW, SWA_QB)
    rows = nqb * WINDOW
    nb = t // rows
    q3 = q.reshape(bsz, t, QW)
    kv3 = kv.reshape(bsz, t, 2 * KW)
    o, ko, vo = pl.pallas_call(
        functools.partial(_swa_prompt_kernel, nqb),
        grid=(bsz, nb),
        in_specs=[pl.BlockSpec((1, rows, QW), lambda b, i: (b, i, 0)),
                  pl.BlockSpec((1, rows, 2 * KW), lambda b, i: (b, i, 0)),
                  pl.BlockSpec((1, WINDOW, 2 * KW), lambda b, i: (b, jnp.maximum(i * nqb - 1, 0), 0)),
                  pl.BlockSpec((1, QW), lambda b, i: (0, 0)),
                  pl.BlockSpec((1, KW), lambda b, i: (0, 0)),
                  pl.BlockSpec((N_Q_HEADS, WINDOW, 2 * WINDOW), lambda b, i: (0, 0, 0)),
                  pl.BlockSpec((N_Q_HEADS, LANES), lambda b, i: (0, 0))],
        out_specs=[pl.BlockSpec((1, rows, QW), lambda b, i: (b, i, 0)),
                   pl.BlockSpec((1, WINDOW, KW), lambda b, i: (b, 0, 0)),
                   pl.BlockSpec((1, WINDOW, KW), lambda b, i: (b, 0, 0))],
        out_shape=[jax.ShapeDtypeStruct((bsz, t, QW), F32),
                   jax.ShapeDtypeStruct((bsz, WINDOW, KW), F32),
                   jax.ShapeDtypeStruct((bsz, WINDOW, KW), F32)],
        compiler_params=_cparams(("parallel", "arbitrary")), name="swa_prompt",
    )(q3, kv3, kv3, jnp.tile(q_norm, N_Q_HEADS).reshape(1, QW), jnp.tile(k_norm, N_KV_HEADS).reshape(1, KW),
      bias, jnp.broadcast_to(sinks[:, None], (N_Q_HEADS, LANES)))
    return o.reshape(bsz * t, QW), ko, vo


DEC_TP = 8


def _swa_decode_kernel(nbt, t_real, q_ref, kv_ref, ck_ref, cv_ref, qw_ref, kw_ref, bc_ref, bn_ref, sink_ref,
                       o_ref, ko_ref, vo_ref):
    rows = N_Q_HEADS * DEC_TP
    tq = _iota((rows, WINDOW), 0) % DEC_TP
    valid_c = _iota((rows, WINDOW), 1) > tq
    jn = _iota((rows, DEC_TP), 1)
    valid_n = jnp.logical_and(jn <= _iota((rows, DEC_TP), 0) % DEC_TP, jn < t_real)
    lane = _iota((1, LANES), 1)
    masks = (jnp.where(lane < ATT_HD, 1.0, 0.0), jnp.where(lane < ATT_HD, 0.0, 1.0))
    row8 = _iota((DEC_TP, 1), 0)
    scale = ATT_HD ** -0.5
    bias_c = jnp.concatenate([bc_ref[kv] for kv in range(N_KV_HEADS)], axis=0)
    bias_n = jnp.concatenate([bn_ref[kv][:, 0:DEC_TP] for kv in range(N_KV_HEADS)], axis=0)
    sink = jnp.concatenate([sink_ref[kv][:, 0:1] for kv in range(N_KV_HEADS)], axis=0)
    bs = range(nbt)
    heads = range(N_Q_HEADS)
    qn_all = _head_norm(q_ref[...].reshape(nbt * DEC_TP, QW), qw_ref[...])
    kvn_all = kv_ref[...].reshape(nbt * DEC_TP, 2 * KW)
    knew_all = _head_norm(kvn_all[:, 0:KW], kw_ref[...])
    knew = [knew_all[b * DEC_TP:(b + 1) * DEC_TP] for b in bs]
    vnew = [kvn_all[b * DEC_TP:(b + 1) * DEC_TP, KW:2 * KW] for b in bs]
    kc = [ck_ref[b] for b in bs]
    vc = [cv_ref[b] for b in bs]

    def stack_q(b):
        pieces = []
        for h in heads:
            qt = qn_all[b * DEC_TP:(b + 1) * DEC_TP, (h // 2) * LANES:(h // 2 + 1) * LANES]
            if h % 2 != h // Q_PER_KV:
                qt = pltpu.roll(qt, ATT_HD, 1)
            pieces.append(qt * masks[h // Q_PER_KV])
        return jnp.concatenate(pieces, axis=0)

    qs = [stack_q(b) for b in bs]
    l_c = [jnp.where(valid_c, _bdot(qs[b], kc[b], _NT) * scale + bias_c, NEG) for b in bs]
    l_n = [jnp.where(valid_n, _bdot(qs[b], knew[b], _NT) * scale + bias_n, NEG) for b in bs]
    mx = [jnp.maximum(jnp.maximum(jnp.max(l_c[b], axis=-1, keepdims=True),
                                  jnp.max(l_n[b], axis=-1, keepdims=True)), sink) for b in bs]
    p_c = [jnp.exp(l_c[b] - mx[b]) for b in bs]
    p_n = [jnp.exp(l_n[b] - mx[b]) for b in bs]
    inv = [1.0 / (jnp.sum(p_c[b], axis=-1, keepdims=True) + jnp.sum(p_n[b], axis=-1, keepdims=True)
                  + jnp.exp(sink - mx[b])) for b in bs]
    o = [_bdot(p_c[b] * inv[b], vc[b]) + _bdot(p_n[b] * inv[b], vnew[b]) for b in bs]
    for b in bs:
        for jq in range(QW // LANES):
            parts = []
            for h in (2 * jq, 2 * jq + 1):
                piece = o[b][h * DEC_TP:(h + 1) * DEC_TP]
                if h % 2 != h // Q_PER_KV:
                    piece = pltpu.roll(piece, ATT_HD, 1)
                parts.append(piece * masks[h % 2])
            o_ref[b, :, jq * LANES:(jq + 1) * LANES] = parts[0] + parts[1]
    for b in bs:
        for cache, new, out in ((kc[b], knew[b], ko_ref), (vc[b], vnew[b], vo_ref)):
            shifted = pltpu.roll(cache, WINDOW - t_real, 0)
            new_r = pltpu.roll(new, DEC_TP - t_real, 0)
            out[b, 0:WINDOW - DEC_TP] = shifted[0:WINDOW - DEC_TP]
            out[b, WINDOW - DEC_TP:WINDOW] = jnp.where(row8 >= DEC_TP - t_real, new_r,
                                                       shifted[WINDOW - DEC_TP:WINDOW])


def swa_decode(q, kv, cache_k, cache_v, bsz, t, q_norm, k_norm, rel_table, sinks, nbt):
    pad = ((0, 0), (0, DEC_TP - t), (0, 0))
    q3 = jnp.pad(q.reshape(bsz, t, QW), pad)
    kv3 = jnp.pad(kv.reshape(bsz, t, 2 * KW), pad)
    kpos = np.concatenate([np.arange(WINDOW) - WINDOW, np.arange(DEC_TP)])
    dist = np.arange(DEC_TP)[:, None] - kpos[None, :]
    bias = rel_bias(rel_table, dist)
    rows = Q_PER_KV * DEC_TP
    bias = bias.reshape(N_KV_HEADS, rows, WINDOW + DEC_TP)
    bias_c = bias[:, :, :WINDOW]
    bias_n = jnp.pad(bias[:, :, WINDOW:], ((0, 0), (0, 0), (0, LANES - DEC_TP)))
    sink_rows = jnp.broadcast_to(sinks.reshape(N_KV_HEADS, Q_PER_KV, 1, 1),
                                 (N_KV_HEADS, Q_PER_KV, DEC_TP, LANES)).reshape(N_KV_HEADS, rows, LANES)
    full3 = lambda shape: pl.BlockSpec(shape, lambda i: (0, 0, 0))
    o, ko, vo = pl.pallas_call(
        functools.partial(_swa_decode_kernel, nbt, t),
        grid=(bsz // nbt,),
        in_specs=[pl.BlockSpec((nbt, DEC_TP, QW), lambda i: (i, 0, 0)),
                  pl.BlockSpec((nbt, DEC_TP, 2 * KW), lambda i: (i, 0, 0)),
                  pl.BlockSpec((nbt, WINDOW, KW), lambda i: (i, 0, 0)),
                  pl.BlockSpec((nbt, WINDOW, KW), lambda i: (i, 0, 0)),
                  pl.BlockSpec((1, QW), lambda i: (0, 0)),
                  pl.BlockSpec((1, KW), lambda i: (0, 0)),
                  full3((N_KV_HEADS, rows, WINDOW)),
                  full3((N_KV_HEADS, rows, LANES)),
                  full3((N_KV_HEADS, rows, LANES))],
        out_specs=[pl.BlockSpec((nbt, DEC_TP, QW), lambda i: (i, 0, 0)),
                   pl.BlockSpec((nbt, WINDOW, KW), lambda i: (i, 0, 0)),
                   pl.BlockSpec((nbt, WINDOW, KW), lambda i: (i, 0, 0))],
        out_shape=[jax.ShapeDtypeStruct((bsz, DEC_TP, QW), F32),
                   jax.ShapeDtypeStruct((bsz, WINDOW, KW), F32),
                   jax.ShapeDtypeStruct((bsz, WINDOW, KW), F32)],
        compiler_params=_cparams(("parallel",)), name="swa_decode",
    )(q3, kv3, cache_k, cache_v, jnp.tile(q_norm, N_Q_HEADS).reshape(1, QW),
      jnp.tile(k_norm, N_KV_HEADS).reshape(1, KW), bias_c, bias_n, sink_rows)
    return o[:, :t].reshape(bsz * t, QW), ko, vo


S5_G = 32
S5_W = S5_G * S5_P
S5_CP = S5_CHUNK * S5_P
S5_PK = 2 * S5_N
S5_HW = S5_G * S5_PK


S5_QT = S5_W // LANES
S5_GT = LANES // S5_P
S5_XW = S5_CHUNK * LANES
S5_HQ = S5_GT * S5_PK


def _s5_group_maps(t_effs, a2, ldt, b1, b2, c1, c2):
    L = S5_CHUNK
    ar2 = a2[0:1, :]
    ai2 = a2[1:2, :]
    step = jnp.exp(ldt)
    mi = _iota((3 * SUBLANES, S5_PK), 0).astype(F32)
    mag = jnp.exp(mi * (step * ar2))
    ang = mi * (step * ai2)
    pwa = mag * jnp.cos(ang)
    pwb = mag * jnp.sin(ang)
    abr = pwa[1:2]
    abi = pwb[1:2]
    den = ar2 * ar2 + ai2 * ai2
    fa = ((abr - 1.0) * ar2 + abi * ai2) / den
    fb = (abi * ar2 - (abr - 1.0) * ai2) / den
    bp1 = b1 * fa + b2 * fb
    bp2 = b2 * fa - b1 * fb
    cpow = [c1 * pwa[m:m + 1] + c2 * pwb[m:m + 1] for m in range(L + 1)]
    kern_t = _hdot(bp1, jnp.concatenate(cpow[0:L], axis=0), _NT)
    sgn = jnp.where(_iota((1, S5_PK), 1) < S5_N, -1.0, 1.0)
    kbs, als = [], []
    for t_eff in t_effs:
        kbs.append([bp1 * pwa[max(t_eff - 1 - i, 0):max(t_eff - 1 - i, 0) + 1]
                    + bp2 * pwb[max(t_eff - 1 - i, 0):max(t_eff - 1 - i, 0) + 1] for i in range(L)])
        als.append(jnp.concatenate([pwa[t_eff:t_eff + 1], sgn * pwb[t_eff:t_eff + 1]], axis=0))
    return kern_t, kbs, als, cpow[1:L + 1]


def _s5_prep_kernel(t_effs, a_ref, ldt_ref, b1_ref, b2_ref, c1_ref, c2_ref, bd_ref, kc_ref, *rest):
    L = S5_CHUNK
    n_t = len(t_effs)
    kb_refs = rest[:n_t]
    al_refs = rest[n_t:2 * n_t]
    lane = _iota((S5_P, LANES), 1)
    for kb_ref in kb_refs:
        kb_ref[0] = jnp.zeros(kb_ref.shape[1:], kb_ref.dtype)
    bd_rows = [[] for _ in range(L)]
    kct_rows = [[] for _ in range(L)]
    for g in range(S5_GT):
        kern_t, kbs, als, kct = _s5_group_maps(t_effs, a_ref[g], ldt_ref[g], b1_ref[g], b2_ref[g],
                                               c1_ref[g], c2_ref[g])
        in_group = jnp.logical_and(lane >= g * S5_P, lane < (g + 1) * S5_P)
        for tau in range(L):
            shift = (g * S5_P - tau * S5_P) % S5_CP
            moved = pltpu.roll(kern_t, shift, 1) if shift else kern_t
            bd_rows[tau].append(jnp.where(in_group, moved[:, 0:LANES], 0.0))
        for k in range(n_t):
            for i in range(L):
                kb_refs[k][0, i * LANES + g * S5_P:i * LANES + (g + 1) * S5_P, g * S5_PK:(g + 1) * S5_PK] = (
                    kbs[k][i].astype(kb_refs[k].dtype))
            al_refs[k][g] = als[k]
        zl = jnp.zeros((S5_P, g * S5_PK), F32)
        zr = jnp.zeros((S5_P, (S5_GT - 1 - g) * S5_PK), F32)
        for t in range(L):
            parts = ([zl] if g else []) + [kct[t]] + ([zr] if g < S5_GT - 1 else [])
            kct_rows[t].append(jnp.concatenate(parts, axis=1))
    for tau in range(L):
        bd_ref[0, tau] = jnp.concatenate(bd_rows[tau], axis=0).astype(bd_ref.dtype)
    for t in range(L):
        blk_t = jnp.concatenate(kct_rows[t], axis=0)
        kc_ref[0, :, t * LANES:(t + 1) * LANES] = blk_t.T.astype(kc_ref.dtype)


def s5_prep(w, i, t_effs):
    dup = lambda z: jnp.concatenate([z, z], axis=-1)
    a = jnp.stack([dup(w["o_a_re"][i]), dup(w["o_a_im"][i])], axis=1)
    ldt = jnp.broadcast_to(w["o_log_dt"][i][:, None, None], (S5_G, 1, S5_PK))
    bt_re = jnp.swapaxes(w["o_b_re"][i], 1, 2)
    bt_im = jnp.swapaxes(w["o_b_im"][i], 1, 2)
    b1 = jnp.concatenate([bt_re, bt_im], -1)
    b2 = jnp.concatenate([-bt_im, bt_re], -1)
    c_re, c_im = w["o_c_re"][i], w["o_c_im"][i]
    c1 = jnp.concatenate([c_re, -c_im], -1)
    c2 = jnp.concatenate([-c_im, -c_re], -1)
    n_t = len(t_effs)
    g3 = lambda r, c: pl.BlockSpec((S5_GT, r, c), lambda q: (q, 0, 0))
    outs = pl.pallas_call(
        functools.partial(_s5_prep_kernel, tuple(t_effs)),
        grid=(S5_QT,),
        in_specs=[g3(2, S5_PK), g3(1, S5_PK), g3(S5_P, S5_PK), g3(S5_P, S5_PK), g3(S5_P, S5_PK), g3(S5_P, S5_PK)],
        out_specs=[pl.BlockSpec((1, S5_CHUNK, LANES, LANES), lambda q: (q, 0, 0, 0)),
                   pl.BlockSpec((1, S5_HQ, S5_XW), lambda q: (q, 0, 0))]
                  + [pl.BlockSpec((1, S5_XW, S5_HQ), lambda q: (q, 0, 0))] * n_t
                  + [g3(2, S5_PK)] * n_t,
        out_shape=[jax.ShapeDtypeStruct((S5_QT, S5_CHUNK, LANES, LANES), BF16),
                   jax.ShapeDtypeStruct((S5_QT, S5_HQ, S5_XW), BF16)]
                  + [jax.ShapeDtypeStruct((S5_QT, S5_XW, S5_HQ), BF16)] * n_t
                  + [jax.ShapeDtypeStruct((S5_G, 2, S5_PK), F32)] * n_t,
        compiler_params=_cparams(("parallel",)), name="s5_prep",
    )(a, ldt, b1, b2, c1, c2)
    bd, kc = outs[0], outs[1]
    mats = []
    for k in range(n_t):
        al = outs[2 + n_t + k]
        mats.append((bd, outs[2 + k], kc, al[:, 0, :].reshape(1, S5_HW), al[:, 1, :].reshape(1, S5_HW)))
    return mats


def _s5_e_kernel(u_ref, kb_ref, e_ref):
    e_ref[...] = _bdot(u_ref[0], kb_ref[0])


def _s5_swap(h):
    n = h.shape[-1]
    lane = _iota(h.shape, 1)
    return jnp.where(lane % S5_PK < S5_N, pltpu.roll(h, n - S5_N, 1), pltpu.roll(h, S5_N, 1))


def _s5_scan_kernel(bsz, cg, e_ref, h0_ref, ala_ref, alb_ref, hp_ref, hf_ref, h_scr):
    @pl.when(pl.program_id(0) == 0)
    def _():
        h_scr[...] = h0_ref[...]

    ala = ala_ref[...]
    alb = alb_ref[...]

    def body(c, hs):
        out = []
        for b in range(bsz):
            hp_ref[b, pl.ds(c, 1), :] = hs[b]
            out.append(ala * hs[b] + alb * _s5_swap(hs[b]) + e_ref[b, pl.ds(c, 1), :])
        return tuple(out)

    hs = lax.fori_loop(0, cg, body, tuple(h_scr[b:b + 1, :] for b in range(bsz)))
    for b in range(bsz):
        h_scr[b:b + 1, :] = hs[b]
        hf_ref[b:b + 1, :] = hs[b]


def _s5_step_kernel(e_ref, h0_ref, ala_ref, alb_ref, hf_ref):
    h = h0_ref[...]
    hf_ref[...] = ala_ref[...] * h + alb_ref[...] * _s5_swap(h) + e_ref[...]


def _s5_y_kernel(u_ref, hp_ref, bd_ref, kc_ref, y_ref, k_scr):
    @pl.when(pl.program_id(1) == 0)
    def _():
        zero = jnp.zeros((LANES, LANES), k_scr.dtype)
        for i in range(S5_CHUNK):
            for t in range(S5_CHUNK):
                k_scr[i * LANES:(i + 1) * LANES, t * LANES:(t + 1) * LANES] = bd_ref[0, t - i] if t >= i else zero

    y_ref[0] = _bdot(u_ref[0], k_scr[...]) + _bdot(hp_ref[...], kc_ref[0])


def s5_core(x, h0, bsz, mats):
    bd, kbbig, kcbig, ala, alb = mats
    rows = x.shape[1]
    nc = rows // bsz
    tr = math.gcd(rows, 512)
    e = pl.pallas_call(
        _s5_e_kernel,
        grid=(S5_QT, rows // tr),
        in_specs=[pl.BlockSpec((1, tr, S5_XW), lambda q, r: (q, r, 0)),
                  pl.BlockSpec((1, S5_XW, S5_HQ), lambda q, r: (q, 0, 0))],
        out_specs=pl.BlockSpec((tr, S5_HQ), lambda q, r: (r, q)),
        out_shape=jax.ShapeDtypeStruct((rows, S5_HW), F32),
        compiler_params=_cparams(("parallel", "parallel")), name="s5_e",
    )(x, kbbig)
    row = pl.BlockSpec((1, S5_HW), lambda i: (0, 0))
    if nc == 1:
        hp = h0
        hf = pl.pallas_call(
            _s5_step_kernel,
            out_shape=jax.ShapeDtypeStruct((bsz, S5_HW), F32), name="s5_step",
        )(e, h0, ala, alb)
    else:
        cg = math.gcd(nc, 64)
        hp, hf = pl.pallas_call(
            functools.partial(_s5_scan_kernel, bsz, cg),
            grid=(nc // cg,),
            in_specs=[pl.BlockSpec((bsz, cg, S5_HW), lambda i: (0, i, 0)),
                      pl.BlockSpec((bsz, S5_HW), lambda i: (0, 0)), row, row],
            out_specs=[pl.BlockSpec((bsz, cg, S5_HW), lambda i: (0, i, 0)),
                       pl.BlockSpec((bsz, S5_HW), lambda i: (0, 0))],
            out_shape=[jax.ShapeDtypeStruct((bsz, nc, S5_HW), F32),
                       jax.ShapeDtypeStruct((bsz, S5_HW), F32)],
            scratch_shapes=[pltpu.VMEM((bsz, S5_HW), F32)],
            compiler_params=_cparams(("arbitrary",)), name="s5_scan",
        )(e.reshape(bsz, nc, S5_HW), h0, ala, alb)
    y = pl.pallas_call(
        _s5_y_kernel,
        grid=(S5_QT, rows // tr),
        in_specs=[pl.BlockSpec((1, tr, S5_XW), lambda q, r: (q, r, 0)),
                  pl.BlockSpec((tr, S5_HQ), lambda q, r: (r, q)),
                  pl.BlockSpec((1, S5_CHUNK, LANES, LANES), lambda q, r: (q, 0, 0, 0)),
                  pl.BlockSpec((1, S5_HQ, S5_XW), lambda q, r: (q, 0, 0))],
        out_specs=pl.BlockSpec((1, tr, S5_XW), lambda q, r: (q, r, 0)),
        out_shape=jax.ShapeDtypeStruct((S5_QT, rows, S5_XW), F32),
        scratch_shapes=[pltpu.VMEM((S5_XW, S5_XW), BF16)],
        compiler_params=_cparams(("parallel", "arbitrary")), name="s5_y",
    )(x, hp.reshape(rows, S5_HW), bd, kcbig)
    return y, hf


def _s5_post_kernel(tpr, y_ref, u_ref, d_ref, gw_ref, gb_ref, o_ref, y_scr, u_scr):
    _from_chunk_rows(y_ref, y_scr, tpr)
    _from_chunk_rows(u_ref, u_scr, tpr)
    y = jnp.concatenate([y_scr[q] for q in range(S5_QT)], axis=1)
    u = jnp.concatenate([u_scr[q] for q in range(S5_QT)], axis=1)
    x = y + d_ref[...] * u
    z = 0.5 * x * (1.0 + jnp.tanh(math.sqrt(2.0 / math.pi) * (x + 0.044715 * (x * x * x))))
    o_ref[...] = z * _sigmoid(_dg(z.astype(BF16), gw_ref[...], _NN) + gb_ref[...])


def s5_post(y4, u4, m, tpr, d, glu_w_bf16, glu_b, tm):
    spec4 = pl.BlockSpec((S5_QT, tm // tpr, S5_XW), lambda i: (0, i, 0))
    row = pl.BlockSpec((1, S5_W), lambda i: (0, 0))
    return pl.pallas_call(
        functools.partial(_s5_post_kernel, tpr),
        grid=(m // tm,),
        in_specs=[spec4, spec4, row, pl.BlockSpec((S5_W, S5_W), lambda i: (0, 0)), row],
        out_specs=pl.BlockSpec((tm, S5_W), lambda i: (i, 0)),
        out_shape=jax.ShapeDtypeStruct((m, S5_W), F32),
        scratch_shapes=[pltpu.VMEM((S5_QT, tm, LANES), F32), pltpu.VMEM((S5_QT, tm, LANES), F32)],
        compiler_params=_cparams(("parallel",)), name="s5_post",
    )(y4, u4, d.reshape(1, S5_W), glu_w_bf16, glu_b.reshape(1, S5_W))


def s5_mix(u4, h_re, h_im, bsz, t, mats, d, glu_w_bf16, glu_b, tm):
    tpr = min(t, CHUNK_ROW_TOKENS)
    h0 = jnp.concatenate([h_re, h_im], axis=-1).reshape(bsz, S5_HW)
    y4, hf = s5_core(u4, h0, bsz, mats)
    out = s5_post(y4, u4, bsz * t, tpr, d, glu_w_bf16, glu_b, tm)
    hf = hf.reshape(bsz, S5_G, 2, S5_N)
    return out, hf[:, :, 0], hf[:, :, 1]


GLA_DK = 64
GLA_DV = 128
GLA_HEADS = 4
GLA_KW = GLA_HEADS * GLA_DK
GLA_VW = GLA_HEADS * GLA_DV
GLA_PW = 2 * GLA_KW + 2 * GLA_VW + LANES


def _gla_kernel(L, t_real, nb, p_ref, aup_ref, ab_ref, nw_ref, s0_ref, y_ref, sf_ref, s_scr):
    c = pl.program_id(1)
    nc = pl.num_programs(1)
    n_pairs = GLA_KW // LANES

    @pl.when(c == 0)
    def _():
        for bb in range(nb):
            for j in range(n_pairs):
                s_scr[bb * n_pairs + j] = s0_ref[bb, j]

    lane = _iota((1, LANES), 1)
    masks = (jnp.where(lane < GLA_DK, 1.0, 0.0), jnp.where(lane < GLA_DK, 0.0, 1.0))
    incl = _iota((L, L), 1) <= _iota((L, L), 0)
    diag128 = _iota((LANES, LANES), 0) == _iota((LANES, LANES), 1)
    ones128 = jnp.ones((LANES, LANES), BF16)
    tri = _tri_incl(L)
    rows = range(nb)
    ps = [p_ref[bb] for bb in rows]
    zs = [_bdot(p[:, 2 * GLA_KW + 2 * GLA_VW:], aup_ref[...]) + ab_ref[...] for p in ps]
    gk = [-_softplus(-z) * (1.0 / GLA_GATE_NORM) for z in zs]
    if t_real % L != 0:
        tok = c * L + _iota((L, 1), 0)
        gk = [jnp.where(tok < t_real, x, 0.0) for x in gk]
    b = [_xdot_r(tri, x) for x in gk]
    bl = [x[L - 1:L, :] for x in b]
    qd = [ps[i][:, 0:GLA_KW] * (GLA_DK ** -0.5) * jnp.exp(b[i]) for i in rows]
    kh = [ps[i][:, GLA_KW:2 * GLA_KW] * jnp.exp(-b[i]) for i in rows]
    kt = [ps[i][:, GLA_KW:2 * GLA_KW] * jnp.exp(bl[i] - b[i]) for i in rows]
    heads = [(bb, h) for bb in rows for h in range(GLA_HEADS)]
    sl = lambda h: slice((h // 2) * LANES, (h // 2 + 1) * LANES)
    hs = lambda h: slice(2 * GLA_KW + h * GLA_DV, 2 * GLA_KW + (h + 1) * GLA_DV)
    gs = lambda h: slice(2 * GLA_KW + GLA_VW + h * GLA_DV, 2 * GLA_KW + GLA_VW + (h + 1) * GLA_DV)
    st = [s_scr[bb * n_pairs + j] for bb in rows for j in range(n_pairs)]
    qm = [qd[bb][:, sl(h)] * masks[h % 2] for bb, h in heads]
    vh = [ps[bb][:, hs(h)] for bb, h in heads]
    attn = [jnp.where(incl, _bdot(qm[i], kh[bb][:, sl(h)], _NT), 0.0) for i, (bb, h) in enumerate(heads)]
    o = [_bdot(attn[i], vh[i]) + _bdot(qm[i], st[bb * n_pairs + h // 2]) for i, (bb, h) in enumerate(heads)]
    kv = [_bdot(kt[bb][:, sl(h)], vh[i], _TN) for i, (bb, h) in enumerate(heads)]
    for i, (bb, h) in enumerate(heads):
        of = o[i] * lax.rsqrt(jnp.mean(o[i] * o[i], axis=-1, keepdims=True) + EPS) * nw_ref[...]
        y_ref[bb, :, h * GLA_DV:(h + 1) * GLA_DV] = of * _silu(ps[bb][:, gs(h)])
    for bb in rows:
        for j in range(n_pairs):
            i0 = bb * GLA_HEADS + 2 * j
            pcol = _xdot_l(jnp.where(diag128, jnp.exp(bl[bb][:, j * LANES:(j + 1) * LANES]), 0.0), ones128)
            s_scr[bb * n_pairs + j] = pcol * st[bb * n_pairs + j] + jnp.concatenate(
                [kv[i0][0:GLA_DK], kv[i0 + 1][GLA_DK:2 * GLA_DK]], axis=0)

    @pl.when(c == nc - 1)
    def _():
        for bb in range(nb):
            for j in range(n_pairs):
                sf_ref[bb, j] = s_scr[bb * n_pairs + j]


def gla_mix(p_gla, s0, bsz, t, L, aup_pad, a_b, norm_w, nb=1):
    tp = -(-t // L) * L
    p3 = p_gla.reshape(bsz, t, GLA_PW)
    if tp != t:
        p3 = jnp.pad(p3, ((0, 0), (0, tp - t), (0, 0)))
    n_pairs = GLA_KW // LANES
    st_spec = pl.BlockSpec((nb, n_pairs, LANES, LANES), lambda b, c: (b, 0, 0, 0))
    y, s_fin = pl.pallas_call(
        functools.partial(_gla_kernel, L, t, nb),
        grid=(bsz // nb, tp // L),
        in_specs=[pl.BlockSpec((nb, L, GLA_PW), lambda b, c: (b, c, 0)),
                  pl.BlockSpec((LANES, GLA_KW), lambda b, c: (0, 0)),
                  pl.BlockSpec((1, GLA_KW), lambda b, c: (0, 0)),
                  pl.BlockSpec((1, GLA_DV), lambda b, c: (0, 0)),
                  st_spec],
        out_specs=[pl.BlockSpec((nb, L, GLA_VW), lambda b, c: (b, c, 0)), st_spec],
        out_shape=[jax.ShapeDtypeStruct((bsz, tp, GLA_VW), F32),
                   jax.ShapeDtypeStruct((bsz, n_pairs, LANES, LANES), F32)],
        scratch_shapes=[pltpu.VMEM((nb * n_pairs, LANES, LANES), F32)],
        compiler_params=_cparams(("parallel", "arbitrary")), name="gla",
    )(p3, aup_pad, a_b.reshape(1, GLA_KW), norm_w.reshape(1, GLA_DV),
      s0.reshape(bsz, n_pairs, LANES, LANES))
    return y[:, :t].reshape(bsz * t, GLA_VW), s_fin.reshape(bsz, GLA_HEADS, GLA_DK, GLA_DV)


GLA_LR = 16
D_FF_CHUNK = 1408
D_FF_EXPERT_CHUNK = 896
FFN_TOKENS = 1024
ROUTER_TOKENS = 1024
MOE_TOKENS = 2048


def _prepare_weights(w):
    win = w["e_w_in"][0]
    w_gla = jnp.pad(win[:, RSHIFT:], ((0, 0), (0, LANES - GLA_LR)))
    wo = w["o_w_in"][0]
    return {
        "e_w_rwkv": win[:, :RSHIFT].astype(BF16),
        "e_w_gla": w_gla.astype(BF16),
        "rwkv": rwkv_params(w, 0),
        "gla_aup": jnp.pad(w["e_gla_a_up"][0], ((0, LANES - GLA_LR), (0, 0))).astype(BF16),
        "e_wo_a": w["e_w_out"][0][:RW].astype(BF16),
        "e_wo_b": w["e_w_out"][0][RW:].astype(BF16),
        "ff_w1": w["e_ff_w1"][0].astype(BF16),
        "ff_w3": w["e_ff_w3"][0].astype(BF16),
        "ff_w2": w["e_ff_w2"][0].astype(BF16),
        "o_w_q": wo[:, :QW].astype(BF16),
        "o_w_kv": wo[:, QW:QW + 2 * KW].astype(BF16),
        "o_w_u": wo[:, QW + 2 * KW:].astype(BF16),
        "glu_w": w["o_glu_w"][0].astype(BF16),
        "o_wo_a": w["o_w_out"][0][:QW].astype(BF16),
        "o_wo_b": w["o_w_out"][0][QW:].astype(BF16),
        "moe_w1": w["o_moe_w1"][0].astype(BF16),
        "moe_w3": w["o_moe_w3"][0].astype(BF16),
        "moe_w2": w["o_moe_w2"][0].astype(BF16),
    }


def _trunk(x3, st, w, pw, tm, chunk, nb, s5_prep_t, prompt_bias):
    bsz, t, d = x3.shape
    m = bsz * t
    x = x3.reshape(m, d)
    p_r, p_g = norm_proj(x, w["e_norm1"][0], [pw["e_w_rwkv"], pw["e_w_gla"]], tm)
    ya, s_rwkv = rwkv_mix(p_r, st["shift"], st["rwkv"], bsz, t, chunk, tm, pw["rwkv"],
                          w["e_lnx_w"][0], w["e_lnx_b"][0], nb)
    s_shift = p_r.reshape(bsz, t, RSHIFT)[:, -1]
    yb, s_gla = gla_mix(p_g, st["gla"], bsz, t, chunk, pw["gla_aup"], w["e_gla_a_b"][0], w["e_gla_norm"][0], nb)
    x = out_proj(x, ya, yb, pw["e_wo_a"], pw["e_wo_b"], tm)
    x = ffn(x, w["e_norm2"][0], pw["ff_w1"], pw["ff_w3"], pw["ff_w2"], math.gcd(m, FFN_TOKENS), D_FF_CHUNK)
    q, kv, u = norm_proj_tiles(x, w["o_norm1"][0], [pw["o_w_q"], pw["o_w_kv"], pw["o_w_u"]], tm,
                               min(t, CHUNK_ROW_TOKENS))
    if st["win_k"] is None:
        yc, nk, nv = swa_prompt(q, kv, bsz, t, w["o_q_norm"][0], w["o_k_norm"][0], prompt_bias, w["o_sinks"][0])
    else:
        yc, nk, nv = swa_decode(q, kv, st["win_k"].reshape(bsz, WINDOW, KW), st["win_v"].reshape(bsz, WINDOW, KW),
                                bsz, t, w["o_q_norm"][0], w["o_k_norm"][0], w["rel_table"], w["o_sinks"][0], 8)
    yd, s5r, s5i = s5_mix(u, st["s5_re"], st["s5_im"], bsz, t, s5_prep_t, w["o_d"][0], pw["glu_w"],
                          w["o_glu_b"][0], tm)
    x = out_proj(x, yc, yd, pw["o_wo_a"], pw["o_wo_b"], tm)
    t_router = min(m, ROUTER_TOKENS)
    hn, gates, pos, counts = router(x, w["o_norm2"][0], w["o_router_w"][0], w["o_router_b"][0], t_router)
    x = moe(x, hn, gates, pos, counts, pw["moe_w1"], pw["moe_w3"], pw["moe_w2"], t_router, min(m, MOE_TOKENS),
            D_FF_EXPERT_CHUNK)
    kv_shape = (bsz, WINDOW, N_KV_HEADS, ATT_HD)
    return (x.reshape(bsz, t, d), s_rwkv[None], s_shift[None], s_gla[None], nk.reshape(kv_shape)[None],
            nv.reshape(kv_shape)[None], s5r[None], s5i[None])


def kernel(x_prompt, x_sample, state_rwkv, state_shift, state_gla, cache_win_k, cache_win_v, state_s5_re,
           state_s5_im, rel_table, e_norm1, e_w_in, e_mu, e_w0, e_w_up, e_a0, e_a_up, e_g_up, e_k_k, e_k_a, e_r_k,
           e_lnx_w, e_lnx_b, e_gla_a_up, e_gla_a_b, e_gla_norm, e_w_out, e_norm2, e_ff_w1, e_ff_w3, e_ff_w2,
           o_norm1, o_w_in, o_q_norm, o_k_norm, o_sinks, o_a_re, o_a_im, o_log_dt, o_b_re, o_b_im, o_c_re, o_c_im,
           o_d, o_glu_w, o_glu_b, o_w_out, o_norm2, o_router_w, o_router_b, o_moe_w1, o_moe_w3, o_moe_w2):
    w = dict(rel_table=rel_table, e_norm1=e_norm1, e_w_in=e_w_in, e_mu=e_mu, e_w0=e_w0, e_w_up=e_w_up, e_a0=e_a0,
             e_a_up=e_a_up, e_g_up=e_g_up, e_k_k=e_k_k, e_k_a=e_k_a, e_r_k=e_r_k, e_lnx_w=e_lnx_w, e_lnx_b=e_lnx_b,
             e_gla_a_up=e_gla_a_up, e_gla_a_b=e_gla_a_b, e_gla_norm=e_gla_norm, e_w_out=e_w_out, e_norm2=e_norm2,
             e_ff_w1=e_ff_w1, e_ff_w3=e_ff_w3, e_ff_w2=e_ff_w2, o_norm1=o_norm1, o_w_in=o_w_in, o_q_norm=o_q_norm,
             o_k_norm=o_k_norm, o_sinks=o_sinks, o_a_re=o_a_re, o_a_im=o_a_im, o_log_dt=o_log_dt, o_b_re=o_b_re,
             o_b_im=o_b_im, o_c_re=o_c_re, o_c_im=o_c_im, o_d=o_d, o_glu_w=o_glu_w, o_glu_b=o_glu_b,
             o_w_out=o_w_out, o_norm2=o_norm2, o_router_w=o_router_w, o_router_b=o_router_b, o_moe_w1=o_moe_w1,
             o_moe_w3=o_moe_w3, o_moe_w2=o_moe_w2)
    pw = _prepare_weights(w)
    bp, tp, _ = x_prompt.shape
    bs, ts, _ = x_sample.shape
    qi = np.arange(WINDOW)[:, None]
    kj = np.arange(2 * WINDOW)[None, :]
    prompt_bias = rel_bias(rel_table, qi + WINDOW - kj)
    zeros = lambda *shape: jnp.zeros(shape, F32)
    st_p = {"rwkv": zeros(bp, RW // RWKV_HD, RWKV_HD, RWKV_HD), "shift": zeros(bp, RSHIFT),
            "gla": zeros(bp, GLA_HEADS, GLA_DK, GLA_DV), "win_k": None, "win_v": None,
            "s5_re": zeros(bp, S5_G, S5_N), "s5_im": zeros(bp, S5_G, S5_N)}
    st_s = {"rwkv": state_rwkv[0], "shift": state_shift[0], "gla": state_gla[0], "win_k": cache_win_k[0],
            "win_v": cache_win_v[0], "s5_re": state_s5_re[0], "s5_im": state_s5_im[0]}
    s5_p, s5_s = s5_prep(w, 0, (S5_CHUNK, ts))
    out_p = _trunk(x_prompt, st_p, w, pw, 512, 64, math.gcd(bp, 2), s5_p, prompt_bias)
    out_s = _trunk(x_sample, st_s, w, pw, bs * ts, 8, math.gcd(bs, 8), s5_s, None)
    res = [out_p[0], out_s[0]]
    for a, b in zip(out_p[1:], out_s[1:]):
        res += [a, b]
    return tuple(res)
```

```python
import functools
import math

import jax
import jax.numpy as jnp
import numpy as np
from jax import lax
from jax.experimental import pallas as pl
from jax.experimental.pallas import tpu as pltpu

F32 = jnp.float32
BF16 = jnp.bfloat16

LANES = 128
SUBLANES = 8
VMEM_LIMIT_BYTES = 56 * 1024 * 1024

EPS = 1e-6
RWKV_HD = 64
RWKV_GN_EPS = 64e-5
GLA_GATE_NORM = 16.0
ATT_HD = 64
WINDOW = 128
N_BUCKETS = 32
BUCKET_MAX_DIST = 128
S5_P = 16
S5_N = 64
S5_CHUNK = 16
TOP_K = 2


def _cparams(sem):
    return pltpu.CompilerParams(dimension_semantics=sem, vmem_limit_bytes=VMEM_LIMIT_BYTES)


_NN = (((1,), (0,)), ((), ()))
_NT = (((1,), (1,)), ((), ()))
_TN = (((0,), (0,)), ((), ()))


def _dg(a, b, dims):
    return lax.dot_general(a, b, dims, preferred_element_type=F32)


def _bdot(a, b, dims=_NN):
    return _dg(a.astype(BF16), b.astype(BF16), dims)


def _split(a, n):
    terms = []
    r = a
    for _ in range(n):
        t = r.astype(BF16)
        terms.append(t)
        r = r - t.astype(F32)
    return terms


def _hdot(a, b, dims=_NN):
    a0, a1 = _split(a, 2)
    b0, b1 = _split(b, 2)
    return _dg(a0, b0, dims) + (_dg(a0, b1, dims) + _dg(a1, b0, dims))


def _xdot_l(a, e, dims=_NN):
    e = e.astype(BF16)
    a0, a1, a2 = _split(a, 3)
    return _dg(a0, e, dims) + (_dg(a1, e, dims) + _dg(a2, e, dims))


def _xdot_r(e, b, dims=_NN):
    e = e.astype(BF16)
    b0, b1, b2 = _split(b, 3)
    return _dg(e, b0, dims) + (_dg(e, b1, dims) + _dg(e, b2, dims))


def _iota(shape, axis):
    return lax.broadcasted_iota(jnp.int32, shape, axis)


def _seg_ones(n, seg):
    r = _iota((n, n), 0) // seg
    c = _iota((n, n), 1) // seg
    return jnp.where(r == c, 1.0, 0.0).astype(BF16)


def _seg_sum(x, seg):
    n = x.shape[-1]
    ones = _seg_ones(n, seg)
    x0, x1 = _split(x, 2)
    return _dg(x0, ones, _NN) + _dg(x1, ones, _NN)


def _sigmoid(x):
    return 1.0 / (1.0 + jnp.exp(-x))


def _silu(x):
    return x * _sigmoid(x)


def _softplus(x):
    return jnp.maximum(x, 0.0) + jnp.log(1.0 + jnp.exp(-jnp.abs(x)))


def _tri_incl(n):
    r = _iota((n, n), 0)
    c = _iota((n, n), 1)
    return jnp.where(c <= r, 1.0, 0.0).astype(BF16)


def _rms(x, g):
    return x * lax.rsqrt(jnp.mean(x * x, axis=-1, keepdims=True) + EPS) * g


def _norm_proj_kernel(n_w, x_ref, g_ref, *refs):
    xn = _rms(x_ref[...], g_ref[...]).astype(BF16)
    for w_ref, o_ref in zip(refs[:n_w], refs[n_w:]):
        o_ref[...] = _dg(xn, w_ref[...], _NN)


def norm_proj(x, g, ws_bf16, tm):
    m, d = x.shape
    return pl.pallas_call(
        functools.partial(_norm_proj_kernel, len(ws_bf16)),
        grid=(m // tm,),
        in_specs=[pl.BlockSpec((tm, d), lambda i: (i, 0)),
                  pl.BlockSpec((1, d), lambda i: (0, 0))]
                 + [pl.BlockSpec(w.shape, lambda i: (0, 0)) for w in ws_bf16],
        out_specs=[pl.BlockSpec((tm, w.shape[1]), lambda i: (i, 0)) for w in ws_bf16],
        out_shape=[jax.ShapeDtypeStruct((m, w.shape[1]), F32) for w in ws_bf16],
        compiler_params=_cparams(("parallel",)), name="norm_proj",
    )(x, g.reshape(1, d), *ws_bf16)


CHUNK_ROW_TOKENS = 16


def _to_chunk_rows(src_ref, dst_ref, tpr):
    rows = src_ref.shape[1] // tpr
    for q in range(dst_ref.shape[0]):
        for i in range(CHUNK_ROW_TOKENS):
            cols = slice(i * LANES, (i + 1) * LANES)
            if i < tpr:
                dst_ref[q, :, cols] = src_ref[q, pl.ds(i, rows, stride=tpr), :]
            else:
                dst_ref[q, :, cols] = jnp.zeros((rows, LANES), dst_ref.dtype)


def _from_chunk_rows(src_ref, dst_ref, tpr):
    rows = dst_ref.shape[1] // tpr
    for q in range(src_ref.shape[0]):
        for i in range(tpr):
            dst_ref[q, pl.ds(i, rows, stride=tpr), :] = src_ref[q, :, i * LANES:(i + 1) * LANES]


def _norm_proj_tiles_kernel(n_w, tpr, x_ref, g_ref, *refs):
    xn = _rms(x_ref[...], g_ref[...]).astype(BF16)
    for w_ref, o_ref in zip(refs[:n_w - 1], refs[n_w:]):
        o_ref[...] = _dg(xn, w_ref[...], _NN)
    u_ref, u_scr = refs[-2], refs[-1]
    u = _dg(xn, refs[n_w - 1][...], _NN)
    for q in range(u_scr.shape[0]):
        u_scr[q] = u[:, q * LANES:(q + 1) * LANES]
    _to_chunk_rows(u_scr, u_ref, tpr)


def norm_proj_tiles(x, g, ws_bf16, tm, tpr):
    m, d = x.shape
    nu = ws_bf16[-1].shape[1]
    nt = nu // LANES
    cw = CHUNK_ROW_TOKENS * LANES
    return pl.pallas_call(
        functools.partial(_norm_proj_tiles_kernel, len(ws_bf16), tpr),
        grid=(m // tm,),
        in_specs=[pl.BlockSpec((tm, d), lambda i: (i, 0)),
                  pl.BlockSpec((1, d), lambda i: (0, 0))]
                 + [pl.BlockSpec(w.shape, lambda i: (0, 0)) for w in ws_bf16],
        out_specs=[pl.BlockSpec((tm, w.shape[1]), lambda i: (i, 0)) for w in ws_bf16[:-1]]
                  + [pl.BlockSpec((nt, tm // tpr, cw), lambda i: (0, i, 0))],
        out_shape=[jax.ShapeDtypeStruct((m, w.shape[1]), F32) for w in ws_bf16[:-1]]
                  + [jax.ShapeDtypeStruct((nt, m // tpr, cw), F32)],
        scratch_shapes=[pltpu.VMEM((nt, tm, LANES), F32)],
        compiler_params=_cparams(("parallel",)), name="norm_proj_tiles",
    )(x, g.reshape(1, d), *ws_bf16)


def _out_proj_kernel(x_ref, ya_ref, yb_ref, wa_ref, wb_ref, o_ref):
    o_ref[...] = (x_ref[...] + _dg(ya_ref[...].astype(BF16), wa_ref[...], _NN)
                  + _dg(yb_ref[...].astype(BF16), wb_ref[...], _NN))


def out_proj(x, ya, yb, wa, wb, tm):
    m, d = x.shape
    return pl.pallas_call(
        _out_proj_kernel,
        grid=(m // tm,),
        in_specs=[pl.BlockSpec((tm, d), lambda i: (i, 0)),
                  pl.BlockSpec((tm, ya.shape[1]), lambda i: (i, 0)),
                  pl.BlockSpec((tm, yb.shape[1]), lambda i: (i, 0)),
                  pl.BlockSpec(wa.shape, lambda i: (0, 0)),
                  pl.BlockSpec(wb.shape, lambda i: (0, 0))],
        out_specs=pl.BlockSpec((tm, d), lambda i: (i, 0)),
        out_shape=jax.ShapeDtypeStruct((m, d), F32),
        compiler_params=_cparams(("parallel",)), name="out_proj",
    )(x, ya, yb, wa, wb)


def _ffn_kernel(x_ref, g_ref, w1_ref, w3_ref, w2_ref, o_ref, xn_scr):
    j = pl.program_id(1)

    @pl.when(j == 0)
    def _():
        x = x_ref[...]
        xn_scr[...] = _rms(x, g_ref[...]).astype(BF16)
        o_ref[...] = x

    xn = xn_scr[...]
    h = _silu(_dg(xn, w1_ref[...], _NN)) * _dg(xn, w3_ref[...], _NN)
    o_ref[...] += _dg(h.astype(BF16), w2_ref[...], _NN)


def ffn(x, g, w1, w3, w2, tm, fc):
    m, d = x.shape
    dff = w1.shape[1]
    return pl.pallas_call(
        _ffn_kernel,
        grid=(m // tm, dff // fc),
        in_specs=[pl.BlockSpec((tm, d), lambda i, j: (i, 0)),
                  pl.BlockSpec((1, d), lambda i, j: (0, 0)),
                  pl.BlockSpec((d, fc), lambda i, j: (0, j)),
                  pl.BlockSpec((d, fc), lambda i, j: (0, j)),
                  pl.BlockSpec((fc, d), lambda i, j: (j, 0))],
        out_specs=pl.BlockSpec((tm, d), lambda i, j: (i, 0)),
        out_shape=jax.ShapeDtypeStruct((m, d), F32),
        scratch_shapes=[pltpu.VMEM((tm, d), BF16)],
        compiler_params=_cparams(("parallel", "arbitrary")), name="ffn",
    )(x, g.reshape(1, d), w1, w3, w2)


def _router_kernel(n_exp, x_ref, g_ref, rw_ref, rb_ref, hn_ref, gate_ref, pos_ref, cnt_ref):
    xn = _rms(x_ref[...], g_ref[...])
    hn_ref[...] = xn.astype(BF16)
    logits = _hdot(xn, rw_ref[...]) + rb_ref[...]
    lane = _iota(logits.shape, 1)
    logits = jnp.where(lane < n_exp, logits, -jnp.inf)
    m1 = jnp.max(logits, axis=-1, keepdims=True)
    i1 = jnp.min(jnp.where(logits == m1, lane, LANES), axis=-1, keepdims=True)
    rest = jnp.where(lane == i1, -jnp.inf, logits)
    m2 = jnp.max(rest, axis=-1, keepdims=True)
    i2 = jnp.min(jnp.where(rest == m2, lane, LANES), axis=-1, keepdims=True)
    e2 = jnp.exp(m2 - m1)
    g1 = 1.0 / (1.0 + e2)
    g2 = e2 / (1.0 + e2)
    pick1 = lane == i1
    pick2 = lane == i2
    gate_ref[...] = jnp.where(pick1, g1, 0.0) + jnp.where(pick2, g2, 0.0)
    tm = logits.shape[0]
    sel = jnp.where(jnp.logical_or(pick1, pick2), 1.0, 0.0).astype(BF16)
    tr = _iota((tm, tm), 0)
    tc = _iota((tm, tm), 1)
    upper = jnp.where(tr <= tc, 1.0, 0.0).astype(BF16)
    eye = jnp.where(tr == tc, 1.0, 0.0).astype(BF16)
    rank_t = _dg(sel, upper, _TN)
    sel_t = _dg(sel, eye, _TN)
    pos_t = jnp.where(sel_t > 0.5, rank_t - 1.0, -1.0)
    pos_ref[...] = pos_t[0:SUBLANES, :]
    cnt_ref[0] = jnp.sum(sel.astype(F32), axis=0, keepdims=True)


def router(x, g, rw, rb, tm):
    m, d = x.shape
    n_exp = rw.shape[1]
    assert n_exp <= SUBLANES
    rw_pad = jnp.pad(rw, ((0, 0), (0, LANES - n_exp)))
    rb_pad = jnp.pad(rb, (0, LANES - n_exp)).reshape(1, LANES)
    return pl.pallas_call(
        functools.partial(_router_kernel, n_exp),
        grid=(m // tm,),
        in_specs=[pl.BlockSpec((tm, d), lambda i: (i, 0)),
                  pl.BlockSpec((1, d), lambda i: (0, 0)),
                  pl.BlockSpec((d, LANES), lambda i: (0, 0)),
                  pl.BlockSpec((1, LANES), lambda i: (0, 0))],
        out_specs=[pl.BlockSpec((tm, d), lambda i: (i, 0)),
                   pl.BlockSpec((tm, LANES), lambda i: (i, 0)),
                   pl.BlockSpec((SUBLANES, tm), lambda i: (0, i)),
                   pl.BlockSpec((1, 1, LANES), lambda i: (i, 0, 0))],
        out_shape=[jax.ShapeDtypeStruct((m, d), BF16), jax.ShapeDtypeStruct((m, LANES), F32),
                   jax.ShapeDtypeStruct((SUBLANES, m), F32),
                   jax.ShapeDtypeStruct((m // tm, 1, LANES), F32)],
        compiler_params=_cparams(("parallel",)), name="router",
    )(x, g.reshape(1, d), rw_pad, rb_pad)


MOE_ROWS = 128


def _moe_kernel(n_exp, cnt_ref, x_hbm, hn_ref, gate_ref, pos_ref, w1_ref, w3_ref, w2_ref, o_ref, xs_scr, y_scr):
    i = pl.program_id(0)
    e = pl.program_id(1)
    j = pl.program_id(2)
    nj = pl.num_programs(2)
    tm = hn_ref.shape[0]
    n_small = (cnt_ref[i * n_exp + e] + (MOE_ROWS - 1)) // MOE_ROWS

    @pl.when(jnp.logical_and(e == 0, j == 0))
    def _():
        pltpu.sync_copy(x_hbm.at[pl.ds(pl.multiple_of(i * tm, tm), tm), :], o_ref)

    def select(start, nrows):
        pos = pos_ref[pl.ds(e, 1), :]
        want = (start + _iota((nrows, tm), 0)).astype(F32)
        return jnp.where(pos == want, 1.0, 0.0).astype(BF16)

    def blocks(body):
        big = 4 * MOE_ROWS
        assert tm % big == 0

        def run_big(blk, carry):
            body(pl.multiple_of(blk * big, big), big)
            return carry
        lax.fori_loop(0, n_small // 4, run_big, 0)
        done = (n_small // 4) * big

        @pl.when(n_small % 4 >= 2)
        def _():
            body(pl.multiple_of(done, big), 2 * MOE_ROWS)

        @pl.when(n_small % 2 == 1)
        def _():
            body(pl.multiple_of(done + (n_small % 4 // 2) * 2 * MOE_ROWS, MOE_ROWS), MOE_ROWS)

    @pl.when(j == 0)
    def _():
        def gather(start, nrows):
            rows = pl.ds(start, nrows)
            xs_scr[rows, :] = _dg(select(start, nrows), hn_ref[...], _NN).astype(BF16)
            y_scr[rows, :] = jnp.zeros((nrows, y_scr.shape[1]), F32)
        blocks(gather)

    def expert(start, nrows):
        rows = pl.ds(start, nrows)
        xs = xs_scr[rows, :]
        h = _silu(_dg(xs, w1_ref[0], _NN)) * _dg(xs, w3_ref[0], _NN)
        y_scr[rows, :] += _dg(h.astype(BF16), w2_ref[0], _NN)
    blocks(expert)

    @pl.when(j == nj - 1)
    def _():
        gt = gate_ref[...]
        gcol = jnp.sum(jnp.where(_iota(gt.shape, 1) == e, gt, 0.0), axis=-1, keepdims=True)

        def scatter(start, nrows):
            o_ref[...] += gcol * _dg(select(start, nrows), y_scr[pl.ds(start, nrows), :].astype(BF16), _TN)
        blocks(scatter)


def moe(x, hn, gates, pos, counts, w1, w3, w2, tm_router, tm, fc):
    m, d = x.shape
    n_exp, _, dff = w1.shape
    nj = dff // fc
    ratio = tm // tm_router
    cnt = counts[:, 0, :n_exp].astype(jnp.int32).reshape(m // tm, ratio, n_exp)
    before = (jnp.cumsum(cnt, axis=1) - cnt).astype(F32)
    shift = jnp.repeat(before.reshape(m // tm_router, n_exp).T, tm_router, axis=1)
    shift = jnp.pad(shift, ((0, SUBLANES - n_exp), (0, 0)))
    pos = jnp.where(pos >= 0, pos + shift, pos)
    cnt = cnt.sum(axis=1).reshape(-1)
    grid_spec = pltpu.PrefetchScalarGridSpec(
        num_scalar_prefetch=1,
        grid=(m // tm, n_exp, nj),
        in_specs=[pl.BlockSpec(memory_space=pl.ANY),
                  pl.BlockSpec((tm, d), lambda i, e, j, c: (i, 0)),
                  pl.BlockSpec((tm, LANES), lambda i, e, j, c: (i, 0)),
                  pl.BlockSpec((SUBLANES, tm), lambda i, e, j, c: (0, i)),
                  pl.BlockSpec((1, d, fc), lambda i, e, j, c: (e, 0, j)),
                  pl.BlockSpec((1, d, fc), lambda i, e, j, c: (e, 0, j)),
                  pl.BlockSpec((1, fc, d), lambda i, e, j, c: (e, j, 0))],
        out_specs=pl.BlockSpec((tm, d), lambda i, e, j, c: (i, 0)),
        scratch_shapes=[pltpu.VMEM((tm, d), BF16), pltpu.VMEM((tm, d), F32)])
    return pl.pallas_call(
        functools.partial(_moe_kernel, n_exp),
        grid_spec=grid_spec,
        out_shape=jax.ShapeDtypeStruct((m, d), F32),
        compiler_params=_cparams(("parallel", "arbitrary", "arbitrary")), name="moe",
    )(cnt, x, hn, gates, pos, w1, w3, w2)


_RW_DOT_A = _bdot
_RW_DOT_T = _bdot
_RW_DOT_W = _bdot
_RW_DOT_S = _bdot


RW = 512
RSHIFT = 1792


def _rwkv_token_maps(xs, w0_ref, wup_ref, a0_ref, aup_ref, gup_ref, kk_ref, ka_ref, rk_ref):
    r = xs[:, 0:RW]
    k = xs[:, RW:2 * RW]
    v = xs[:, 2 * RW:3 * RW]
    lr = xs[:, 3 * RW:3 * RW + LANES]
    gd = xs[:, 3 * RW + LANES:3 * RW + 2 * LANES]
    w_pre = w0_ref[...] + _bdot(jnp.tanh(lr), wup_ref[...])
    logw = -jnp.exp(-_softplus(-w_pre) - 0.5)
    a = _sigmoid(a0_ref[...] + _bdot(lr, aup_ref[...]))
    g = _bdot(_sigmoid(gd), gup_ref[...])
    kkr = k * kk_ref[...]
    kk = kkr / jnp.maximum(jnp.sqrt(_seg_sum(kkr * kkr, RWKV_HD)), 1e-12)
    k2 = k * (1.0 + (a - 1.0) * ka_ref[...])
    bonus = _seg_sum(r * k2 * rk_ref[...], RWKV_HD) * v
    return r, logw, k2, v, -kk, kk * a, g, bonus


def _rwkv_kernel(L, nb, n_pairs, t_real, p_ref, prev0_ref, s0_ref, mu_ref, w0_ref, wup_ref, a0_ref, aup_ref,
                 gup_ref, kk_ref, ka_ref, rk_ref, lnw_ref, lnb_ref, y_ref, sf_ref, s_scr, prev_scr):
    c = pl.program_id(1)
    nc = pl.num_programs(1)
    L2 = 2 * L
    lane = _iota((1, LANES), 1)
    m0 = jnp.where(lane < RWKV_HD, 1.0, 0.0)
    m1 = 1.0 - m0
    rr = _iota((L2, L2), 0)
    cc = _iota((L2, L2), 1)
    same = (rr // L) == (cc // L)
    strict = jnp.logical_and(same, (cc % L) < (rr % L))
    incl = jnp.logical_and(same, (cc % L) <= (rr % L))
    eye2 = jnp.where(rr == cc, 1.0, 0.0)
    r128 = _iota((LANES, LANES), 0)
    c128 = _iota((LANES, LANES), 1)
    blk128 = (r128 // RWKV_HD) == (c128 // RWKV_HD)
    diag128 = r128 == c128
    fmat = jnp.where(_iota((LANES, RWKV_HD), 0) % RWKV_HD == _iota((LANES, RWKV_HD), 1), 1.0, 0.0)
    tri = _tri_incl(L)

    def bd(x):
        return jnp.concatenate([x * m0, x * m1], axis=0)

    @pl.when(c == 0)
    def _():
        for bb in range(nb):
            prev_scr[bb, 0:1, :] = prev0_ref[bb]
            for j in range(n_pairs):
                s0 = s0_ref[bb, j]
                st = _xdot_r(fmat, s0, _NT)
                s_scr[bb * n_pairs + j] = jnp.where(blk128, st, 0.0)

    rowi = _iota((L, 1), 0)
    xs = []
    for bb in range(nb):
        p = p_ref[bb]
        prev = jnp.where(rowi == 0, prev_scr[bb, 0:1, :], pltpu.roll(p, 1, 0))
        prev_scr[bb, 0:1, :] = p[L - 1:L, :]
        xs.append(p + (prev - p) * mu_ref[...])
    maps = _rwkv_token_maps(jnp.concatenate(xs, axis=0), w0_ref, wup_ref, a0_ref, aup_ref, gup_ref,
                            kk_ref, ka_ref, rk_ref)
    if t_real % L != 0:
        valid = (c * L + _iota((nb * L, 1), 0) % L) < t_real
        maps = tuple(jnp.where(valid, z, 0.0) for z in maps)
    toks = [tuple(z[bb * L:(bb + 1) * L] for z in maps) for bb in range(nb)]

    pairs = range(nb * n_pairs)
    bbs = [i // n_pairs for i in pairs]
    sls = [slice((i % n_pairs) * LANES, (i % n_pairs + 1) * LANES) for i in pairs]
    tok = lambda j, which: toks[bbs[j]][which][:, sls[j]]
    lw = [tok(j, 1) for j in pairs]
    b = [_xdot_r(tri, x) for x in lw]
    bl = [x[L - 1:L, :] for x in b]
    e_b = [jnp.exp(x) for x in b]
    e_nb = [jnp.exp(-x) for x in b]
    e_lb = [jnp.exp(bl[j] - b[j]) for j in pairs]
    at = [bd(tok(j, 4) * jnp.exp(b[j] - lw[j])) for j in pairs]
    rt = [bd(tok(j, 0) * e_b[j]) for j in pairs]
    bh = [bd(tok(j, 5) * e_nb[j]) for j in pairs]
    kh = [bd(tok(j, 2) * e_nb[j]) for j in pairs]
    bt = [bd(tok(j, 5) * e_lb[j]) for j in pairs]
    kt = [bd(tok(j, 2) * e_lb[j]) for j in pairs]
    vb = [bd(tok(j, 3)) for j in pairs]
    if L2 % LANES == 0:
        gq = [_RW_DOT_A(jnp.concatenate([at[j], rt[j]], axis=0), jnp.concatenate([bh[j], kh[j]], axis=0), _NT)
              for j in pairs]
        a_ab = [jnp.where(strict, g[0:L2, 0:L2], 0.0) for g in gq]
        a_ak = [jnp.where(strict, g[0:L2, L2:2 * L2], 0.0) for g in gq]
        a_rb = [jnp.where(incl, g[L2:2 * L2, 0:L2], 0.0) for g in gq]
        a_rk = [jnp.where(incl, g[L2:2 * L2, L2:2 * L2], 0.0) for g in gq]
    else:
        a_ab = [jnp.where(strict, _RW_DOT_A(at[j], bh[j], _NT), 0.0) for j in pairs]
        a_ak = [jnp.where(strict, _RW_DOT_A(at[j], kh[j], _NT), 0.0) for j in pairs]
        a_rb = [jnp.where(incl, _RW_DOT_A(rt[j], bh[j], _NT), 0.0) for j in pairs]
        a_rk = [jnp.where(incl, _RW_DOT_A(rt[j], kh[j], _NT), 0.0) for j in pairs]
    x = a_ab
    tinv = [eye2 + a for a in a_ab]
    span = 2
    while span < L:
        x = [_RW_DOT_T(xx, xx) for xx in x]
        tinv = [tinv[j] + _RW_DOT_T(tinv[j], x[j]) for j in pairs]
        span *= 2
    akv = [_RW_DOT_W(a_ak[j], vb[j]) for j in pairs]
    wuv = [_RW_DOT_W(tinv[j], jnp.concatenate([at[j], akv[j]], axis=1)) for j in pairs]
    mn = [_RW_DOT_W(bt[j], wuv[j], _TN) for j in pairs]
    qy = [_RW_DOT_W(a_rb[j], wuv[j]) for j in pairs]
    mm = [jnp.where(diag128, jnp.exp(bl[j]), 0.0) + mn[j][:, 0:LANES] for j in pairs]
    nn = [mn[j][:, LANES:2 * LANES] + _RW_DOT_W(kt[j], vb[j], _TN) for j in pairs]
    q = [rt[j] + qy[j][:, 0:LANES] for j in pairs]
    yv = [qy[j][:, LANES:2 * LANES] + _RW_DOT_W(a_rk[j], vb[j]) for j in pairs]
    s = [s_scr[j] for j in pairs]
    ybd = [_RW_DOT_S(q[j], s[j]) + yv[j] for j in pairs]
    for j in pairs:
        s_scr[j] = _RW_DOT_S(mm[j], s[j]) + nn[j]

    y = jnp.concatenate(
        [jnp.concatenate([ybd[bb * n_pairs + jj][0:L] + ybd[bb * n_pairs + jj][L:L2] for jj in range(n_pairs)],
                         axis=1) for bb in range(nb)], axis=0)
    mean = _seg_sum(y, RWKV_HD) * (1.0 / RWKV_HD)
    d = y - mean
    var = _seg_sum(d * d, RWKV_HD) * (1.0 / RWKV_HD)
    ya = (d * lax.rsqrt(var + RWKV_GN_EPS) * lnw_ref[...] + lnb_ref[...] + maps[7]) * maps[6]
    for bb in range(nb):
        y_ref[bb] = ya[bb * L:(bb + 1) * L]

    @pl.when(c == nc - 1)
    def _():
        for i in pairs:
            sf_ref[bbs[i], i % n_pairs] = _xdot_l(s_scr[i], fmat, _TN)


def rwkv_params(w, i):
    z64 = jnp.zeros((RWKV_HD, RW), F32)
    return {
        "mu": w["e_mu"][i].reshape(1, RSHIFT),
        "w0": w["e_w0"][i].reshape(1, RW),
        "wup": jnp.concatenate([w["e_w_up"][i], z64], 0).astype(BF16),
        "a0": w["e_a0"][i].reshape(1, RW),
        "aup": jnp.concatenate([z64, w["e_a_up"][i]], 0).astype(BF16),
        "gup": w["e_g_up"][i].astype(BF16),
        "k_k": w["e_k_k"][i].reshape(1, RW),
        "k_a": w["e_k_a"][i].reshape(1, RW),
        "r_k": w["e_r_k"][i].reshape(1, RW),
    }


def rwkv_mix(p_flat, prev, s0, bsz, t, L, prm, lnx_w, lnx_b, nb=1):
    n_pairs = RW // LANES
    tp = -(-t // L) * L
    p3 = p_flat.reshape(bsz, t, RSHIFT)
    if tp != t:
        p3 = jnp.pad(p3, ((0, 0), (0, tp - t), (0, 0)))
    row = lambda n: pl.BlockSpec((1, n), lambda b, c: (0, 0))
    lora = pl.BlockSpec((LANES, RW), lambda b, c: (0, 0))
    st_spec = pl.BlockSpec((nb, n_pairs, LANES, RWKV_HD), lambda b, c: (b, 0, 0, 0))
    y, s_fin = pl.pallas_call(
        functools.partial(_rwkv_kernel, L, nb, n_pairs, t),
        grid=(bsz // nb, tp // L),
        in_specs=[pl.BlockSpec((nb, L, RSHIFT), lambda b, c: (b, c, 0)),
                  pl.BlockSpec((nb, 1, RSHIFT), lambda b, c: (b, 0, 0)),
                  st_spec,
                  row(RSHIFT), row(RW), lora, row(RW), lora, lora, row(RW), row(RW), row(RW), row(RW), row(RW)],
        out_specs=[pl.BlockSpec((nb, L, RW), lambda b, c: (b, c, 0)), st_spec],
        out_shape=[jax.ShapeDtypeStruct((bsz, tp, RW), F32),
                   jax.ShapeDtypeStruct((bsz, n_pairs, LANES, RWKV_HD), F32)],
        scratch_shapes=[pltpu.VMEM((nb * n_pairs, LANES, LANES), F32),
                        pltpu.VMEM((nb, SUBLANES, RSHIFT), F32)],
        compiler_params=_cparams(("parallel", "arbitrary")), name="rwkv",
    )(p3, prev.reshape(bsz, 1, RSHIFT), s0.reshape(bsz, n_pairs, LANES, RWKV_HD),
      prm["mu"], prm["w0"], prm["wup"], prm["a0"], prm["aup"], prm["gup"], prm["k_k"], prm["k_a"], prm["r_k"],
      lnx_w.reshape(1, RW), lnx_b.reshape(1, RW))
    return y[:, :t].reshape(bsz * t, RW), s_fin.reshape(bsz, RW // RWKV_HD, RWKV_HD, RWKV_HD)


N_Q_HEADS = 8
N_KV_HEADS = 2
Q_PER_KV = N_Q_HEADS // N_KV_HEADS
QW = N_Q_HEADS * ATT_HD
KW = N_KV_HEADS * ATT_HD
NEG = -1e30


def _t5_bucket_np(dist):
    n = np.maximum(dist, 0)
    max_exact = N_BUCKETS // 2
    nf = np.maximum(n, 1).astype(np.float32)
    large = max_exact + (np.log(nf / np.float32(max_exact)) / np.float32(math.log(BUCKET_MAX_DIST / max_exact))
                         * np.float32(N_BUCKETS - max_exact)).astype(np.int32)
    large = np.minimum(large, N_BUCKETS - 1)
    return np.where(n < max_exact, n, large)


def _bias_kernel(rt_ref, oh_ref, o_ref):
    o_ref[...] = _xdot_l(rt_ref[...], oh_ref[...])


def rel_bias(rel_table, dist):
    bucket = _t5_bucket_np(dist).reshape(-1)
    n = bucket.shape[0]
    onehot = jnp.asarray((np.arange(N_BUCKETS)[:, None] == bucket[None, :]).astype(np.float32), BF16)
    out = pl.pallas_call(
        _bias_kernel,
        out_shape=jax.ShapeDtypeStruct((N_Q_HEADS, n), F32),
    )(rel_table.T, onehot)
    return out.reshape((N_Q_HEADS,) + dist.shape)


def _head_norm(x, w_row):
    return x * lax.rsqrt(_seg_sum(x * x, ATT_HD) * (1.0 / ATT_HD) + EPS) * w_row


def _swa_prompt_kernel(nqb, q_ref, kvc_ref, kvp_ref, qw_ref, kw_ref, bias_ref, sink_ref, o_ref, ko_ref, vo_ref):
    i = pl.program_id(1)
    qn = _head_norm(q_ref[0], qw_ref[...])
    kvc = kvc_ref[0]
    kvp = kvp_ref[0]
    kn = [_head_norm(kvp[:, 0:KW], kw_ref[...])]
    vs = [kvp[:, KW:2 * KW]]
    kcn = _head_norm(kvc[:, 0:KW], kw_ref[...])
    for j in range(nqb):
        kn.append(kcn[j * WINDOW:(j + 1) * WINDOW])
        vs.append(kvc[j * WINDOW:(j + 1) * WINDOW, KW:2 * KW])
    kcat = [jnp.concatenate([kn[j], kn[j + 1]], axis=0).astype(BF16) for j in range(nqb)]
    vcat = [jnp.concatenate([vs[j], vs[j + 1]], axis=0).astype(BF16) for j in range(nqb)]
    qi = _iota((WINDOW, 2 * WINDOW), 0)
    kj = _iota((WINDOW, 2 * WINDOW), 1)
    dist = qi + WINDOW - kj
    band = jnp.logical_and(dist >= 0, dist < WINDOW)
    valid = [jnp.logical_and(band, jnp.logical_or(kj >= WINDOW, i > 0))] + [band] * (nqb - 1)
    lane = _iota((1, LANES), 1)
    masks = (jnp.where(lane < ATT_HD, 1.0, 0.0), jnp.where(lane < ATT_HD, 0.0, 1.0))
    scale = ATT_HD ** -0.5
    heads = range(N_Q_HEADS)
    units = [(j, h) for j in range(nqb) for h in heads]
    kv_of = lambda h: h // Q_PER_KV
    qts = [qn[j * WINDOW:(j + 1) * WINDOW, (h // 2) * LANES:(h // 2 + 1) * LANES] for j, h in units]
    qts = [pltpu.roll(qts[u], ATT_HD, 1) if h % 2 != kv_of(h) else qts[u] for u, (j, h) in enumerate(units)]
    qms = [(qts[u] * masks[kv_of(h)]).astype(BF16) for u, (j, h) in enumerate(units)]
    s_all = [_dg(jnp.concatenate(qms[j * N_Q_HEADS:(j + 1) * N_Q_HEADS], axis=0), kcat[j], _NT) * scale
             for j in range(nqb)]
    logits = [jnp.where(valid[j], s_all[j][h * WINDOW:(h + 1) * WINDOW] + bias_ref[h], NEG) for j, h in units]
    sinks = [sink_ref[h:h + 1, 0:1] for j, h in units]
    mx = [jnp.maximum(jnp.max(logits[u], axis=-1, keepdims=True), sinks[u]) for u in range(len(units))]
    pr = [jnp.exp(logits[u] - mx[u]) for u in range(len(units))]
    den = [jnp.sum(pr[u], axis=-1, keepdims=True) + jnp.exp(sinks[u] - mx[u]) for u in range(len(units))]
    probs = [(pr[u] * (1.0 / den[u])).astype(BF16) for u in range(len(units))]
    o_all = [_dg(jnp.concatenate(probs[j * N_Q_HEADS:(j + 1) * N_Q_HEADS], axis=0), vcat[j], _NN)
             for j in range(nqb)]
    os_ = [o_all[j][h * WINDOW:(h + 1) * WINDOW] for j, h in units]
    os_ = [pltpu.roll(os_[u], ATT_HD, 1) if h % 2 != kv_of(h) else os_[u] for u, (j, h) in enumerate(units)]
    for j in range(nqb):
        for jq in range(QW // LANES):
            o_ref[0, j * WINDOW:(j + 1) * WINDOW, jq * LANES:(jq + 1) * LANES] = (
                os_[j * N_Q_HEADS + 2 * jq] * masks[0] + os_[j * N_Q_HEADS + 2 * jq + 1] * masks[1])
    ko_ref[0] = kn[nqb]
    vo_ref[0] = vs[nqb]


SWA_QB = 2


def swa_prompt(q, kv, bsz, t, q_norm, k_norm, bias, sinks):
    nqb = math.gcd(t // WINDOW, SWA_QB)
    rows = nqb * WINDOW
    nb = t // rows
    q3 = q.reshape(bsz, t, QW)
    kv3 = kv.reshape(bsz, t, 2 * KW)
    o, ko, vo = pl.pallas_call(
        functools.partial(_swa_prompt_kernel, nqb),
        grid=(bsz, nb),
        in_specs=[pl.BlockSpec((1, rows, QW), lambda b, i: (b, i, 0)),
                  pl.BlockSpec((1, rows, 2 * KW), lambda b, i: (b, i, 0)),
                  pl.BlockSpec((1, WINDOW, 2 * KW), lambda b, i: (b, jnp.maximum(i * nqb - 1, 0), 0)),
                  pl.BlockSpec((1, QW), lambda b, i: (0, 0)),
                  pl.BlockSpec((1, KW), lambda b, i: (0, 0)),
                  pl.BlockSpec((N_Q_HEADS, WINDOW, 2 * WINDOW), lambda b, i: (0, 0, 0)),
                  pl.BlockSpec((N_Q_HEADS, LANES), lambda b, i: (0, 0))],
        out_specs=[pl.BlockSpec((1, rows, QW), lambda b, i: (b, i, 0)),
                   pl.BlockSpec((1, WINDOW, KW), lambda b, i: (b, 0, 0)),
                   pl.BlockSpec((1, WINDOW, KW), lambda b, i: (b, 0, 0))],
        out_shape=[jax.ShapeDtypeStruct((bsz, t, QW), F32),
                   jax.ShapeDtypeStruct((bsz, WINDOW, KW), F32),
                   jax.ShapeDtypeStruct((bsz, WINDOW, KW), F32)],
        compiler_params=_cparams(("parallel", "arbitrary")), name="swa_prompt",
    )(q3, kv3, kv3, jnp.tile(q_norm, N_Q_HEADS).reshape(1, QW), jnp.tile(k_norm, N_KV_HEADS).reshape(1, KW),
      bias, jnp.broadcast_to(sinks[:, None], (N_Q_HEADS, LANES)))
    return o.reshape(bsz * t, QW), ko, vo


DEC_TP = 8


def _swa_decode_kernel(nbt, t_real, q_ref, kv_ref, ck_ref, cv_ref, qw_ref, kw_ref, bc_ref, bn_ref, sink_ref,
                       o_ref, ko_ref, vo_ref):
    rows = N_Q_HEADS * DEC_TP
    tq = _iota((rows, WINDOW), 0) % DEC_TP
    valid_c = _iota((rows, WINDOW), 1) > tq
    jn = _iota((rows, DEC_TP), 1)
    valid_n = jnp.logical_and(jn <= _iota((rows, DEC_TP), 0) % DEC_TP, jn < t_real)
    lane = _iota((1, LANES), 1)
    masks = (jnp.where(lane < ATT_HD, 1.0, 0.0), jnp.where(lane < ATT_HD, 0.0, 1.0))
    row8 = _iota((DEC_TP, 1), 0)
    scale = ATT_HD ** -0.5
    bias_c = jnp.concatenate([bc_ref[kv] for kv in range(N_KV_HEADS)], axis=0)
    bias_n = jnp.concatenate([bn_ref[kv][:, 0:DEC_TP] for kv in range(N_KV_HEADS)], axis=0)
    sink = jnp.concatenate([sink_ref[kv][:, 0:1] for kv in range(N_KV_HEADS)], axis=0)
    bs = range(nbt)
    heads = range(N_Q_HEADS)
    qn_all = _head_norm(q_ref[...].reshape(nbt * DEC_TP, QW), qw_ref[...])
    kvn_all = kv_ref[...].reshape(nbt * DEC_TP, 2 * KW)
    knew_all = _head_norm(kvn_all[:, 0:KW], kw_ref[...])
    knew = [knew_all[b * DEC_TP:(b + 1) * DEC_TP] for b in bs]
    vnew = [kvn_all[b * DEC_TP:(b + 1) * DEC_TP, KW:2 * KW] for b in bs]
    kc = [ck_ref[b] for b in bs]
    vc = [cv_ref[b] for b in bs]

    def stack_q(b):
        pieces = []
        for h in heads:
            qt = qn_all[b * DEC_TP:(b + 1) * DEC_TP, (h // 2) * LANES:(h // 2 + 1) * LANES]
            if h % 2 != h // Q_PER_KV:
                qt = pltpu.roll(qt, ATT_HD, 1)
            pieces.append(qt * masks[h // Q_PER_KV])
        return jnp.concatenate(pieces, axis=0)

    qs = [stack_q(b) for b in bs]
    l_c = [jnp.where(valid_c, _bdot(qs[b], kc[b], _NT) * scale + bias_c, NEG) for b in bs]
    l_n = [jnp.where(valid_n, _bdot(qs[b], knew[b], _NT) * scale + bias_n, NEG) for b in bs]
    mx = [jnp.maximum(jnp.maximum(jnp.max(l_c[b], axis=-1, keepdims=True),
                                  jnp.max(l_n[b], axis=-1, keepdims=True)), sink) for b in bs]
    p_c = [jnp.exp(l_c[b] - mx[b]) for b in bs]
    p_n = [jnp.exp(l_n[b] - mx[b]) for b in bs]
    inv = [1.0 / (jnp.sum(p_c[b], axis=-1, keepdims=True) + jnp.sum(p_n[b], axis=-1, keepdims=True)
                  + jnp.exp(sink - mx[b])) for b in bs]
    o = [_bdot(p_c[b] * inv[b], vc[b]) + _bdot(p_n[b] * inv[b], vnew[b]) for b in bs]
    for b in bs:
        for jq in range(QW // LANES):
            parts = []
            for h in (2 * jq, 2 * jq + 1):
                piece = o[b][h * DEC_TP:(h + 1) * DEC_TP]
                if h % 2 != h // Q_PER_KV:
                    piece = pltpu.roll(piece, ATT_HD, 1)
                parts.append(piece * masks[h % 2])
            o_ref[b, :, jq * LANES:(jq + 1) * LANES] = parts[0] + parts[1]
    for b in bs:
        for cache, new, out in ((kc[b], knew[b], ko_ref), (vc[b], vnew[b], vo_ref)):
            shifted = pltpu.roll(cache, WINDOW - t_real, 0)
            new_r = pltpu.roll(new, DEC_TP - t_real, 0)
            out[b, 0:WINDOW - DEC_TP] = shifted[0:WINDOW - DEC_TP]
            out[b, WINDOW - DEC_TP:WINDOW] = jnp.where(row8 >= DEC_TP - t_real, new_r,
                                                       shifted[WINDOW - DEC_TP:WINDOW])


def swa_decode(q, kv, cache_k, cache_v, bsz, t, q_norm, k_norm, rel_table, sinks, nbt):
    pad = ((0, 0), (0, DEC_TP - t), (0, 0))
    q3 = jnp.pad(q.reshape(bsz, t, QW), pad)
    kv3 = jnp.pad(kv.reshape(bsz, t, 2 * KW), pad)
    kpos = np.concatenate([np.arange(WINDOW) - WINDOW, np.arange(DEC_TP)])
    dist = np.arange(DEC_TP)[:, None] - kpos[None, :]
    bias = rel_bias(rel_table, dist)
    rows = Q_PER_KV * DEC_TP
    bias = bias.reshape(N_KV_HEADS, rows, WINDOW + DEC_TP)
    bias_c = bias[:, :, :WINDOW]
    bias_n = jnp.pad(bias[:, :, WINDOW:], ((0, 0), (0, 0), (0, LANES - DEC_TP)))
    sink_rows = jnp.broadcast_to(sinks.reshape(N_KV_HEADS, Q_PER_KV, 1, 1),
                                 (N_KV_HEADS, Q_PER_KV, DEC_TP, LANES)).reshape(N_KV_HEADS, rows, LANES)
    full3 = lambda shape: pl.BlockSpec(shape, lambda i: (0, 0, 0))
    o, ko, vo = pl.pallas_call(
        functools.partial(_swa_decode_kernel, nbt, t),
        grid=(bsz // nbt,),
        in_specs=[pl.BlockSpec((nbt, DEC_TP, QW), lambda i: (i, 0, 0)),
                  pl.BlockSpec((nbt, DEC_TP, 2 * KW), lambda i: (i, 0, 0)),
                  pl.BlockSpec((nbt, WINDOW, KW), lambda i: (i, 0, 0)),
                  pl.BlockSpec((nbt, WINDOW, KW), lambda i: (i, 0, 0)),
                  pl.BlockSpec((1, QW), lambda i: (0, 0)),
                  pl.BlockSpec((1, KW), lambda i: (0, 0)),
                  full3((N_KV_HEADS, rows, WINDOW)),
                  full3((N_KV_HEADS, rows, LANES)),
                  full3((N_KV_HEADS, rows, LANES))],
        out_specs=[pl.BlockSpec((nbt, DEC_TP, QW), lambda i: (i, 0, 0)),
                   pl.BlockSpec((nbt, WINDOW, KW), lambda i: (i, 0, 0)),
                   pl.BlockSpec((nbt, WINDOW, KW), lambda i: (i, 0, 0))],
        out_shape=[jax.ShapeDtypeStruct((bsz, DEC_TP, QW), F32),
                   jax.ShapeDtypeStruct((bsz, WINDOW, KW), F32),
                   jax.ShapeDtypeStruct((bsz, WINDOW, KW), F32)],
        compiler_params=_cparams(("parallel",)), name="swa_decode",
    )(q3, kv3, cache_k, cache_v, jnp.tile(q_norm, N_Q_HEADS).reshape(1, QW),
      jnp.tile(k_norm, N_KV_HEADS).reshape(1, KW), bias_c, bias_n, sink_rows)
    return o[:, :t].reshape(bsz * t, QW), ko, vo


S5_G = 32
S5_W = S5_G * S5_P
S5_CP = S5_CHUNK * S5_P
S5_PK = 2 * S5_N
S5_HW = S5_G * S5_PK


S5_QT = S5_W // LANES
S5_GT = LANES // S5_P
S5_XW = S5_CHUNK * LANES
S5_HQ = S5_GT * S5_PK


def _s5_group_maps(t_effs, a2, ldt, b1, b2, c1, c2):
    L = S5_CHUNK
    ar2 = a2[0:1, :]
    ai2 = a2[1:2, :]
    step = jnp.exp(ldt)
    mi = _iota((3 * SUBLANES, S5_PK), 0).astype(F32)
    mag = jnp.exp(mi * (step * ar2))
    ang = mi * (step * ai2)
    pwa = mag * jnp.cos(ang)
    pwb = mag * jnp.sin(ang)
    abr = pwa[1:2]
    abi = pwb[1:2]
    den = ar2 * ar2 + ai2 * ai2
    fa = ((abr - 1.0) * ar2 + abi * ai2) / den
    fb = (abi * ar2 - (abr - 1.0) * ai2) / den
    bp1 = b1 * fa + b2 * fb
    bp2 = b2 * fa - b1 * fb
    cpow = [c1 * pwa[m:m + 1] + c2 * pwb[m:m + 1] for m in range(L + 1)]
    kern_t = _hdot(bp1, jnp.concatenate(cpow[0:L], axis=0), _NT)
    sgn = jnp.where(_iota((1, S5_PK), 1) < S5_N, -1.0, 1.0)
    kbs, als = [], []
    for t_eff in t_effs:
        kbs.append([bp1 * pwa[max(t_eff - 1 - i, 0):max(t_eff - 1 - i, 0) + 1]
                    + bp2 * pwb[max(t_eff - 1 - i, 0):max(t_eff - 1 - i, 0) + 1] for i in range(L)])
        als.append(jnp.concatenate([pwa[t_eff:t_eff + 1], sgn * pwb[t_eff:t_eff + 1]], axis=0))
    return kern_t, kbs, als, cpow[1:L + 1]


def _s5_prep_kernel(t_effs, a_ref, ldt_ref, b1_ref, b2_ref, c1_ref, c2_ref, bd_ref, kc_ref, *rest):
    L = S5_CHUNK
    n_t = len(t_effs)
    kb_refs = rest[:n_t]
    al_refs = rest[n_t:2 * n_t]
    lane = _iota((S5_P, LANES), 1)
    for kb_ref in kb_refs:
        kb_ref[0] = jnp.zeros(kb_ref.shape[1:], kb_ref.dtype)
    bd_rows = [[] for _ in range(L)]
    kct_rows = [[] for _ in range(L)]
    for g in range(S5_GT):
        kern_t, kbs, als, kct = _s5_group_maps(t_effs, a_ref[g], ldt_ref[g], b1_ref[g], b2_ref[g],
                                               c1_ref[g], c2_ref[g])
        in_group = jnp.logical_and(lane >= g * S5_P, lane < (g + 1) * S5_P)
        for tau in range(L):
            shift = (g * S5_P - tau * S5_P) % S5_CP
            moved = pltpu.roll(kern_t, shift, 1) if shift else kern_t
            bd_rows[tau].append(jnp.where(in_group, moved[:, 0:LANES], 0.0))
        for k in range(n_t):
            for i in range(L):
                kb_refs[k][0, i * LANES + g * S5_P:i * LANES + (g + 1) * S5_P, g * S5_PK:(g + 1) * S5_PK] = (
                    kbs[k][i].astype(kb_refs[k].dtype))
            al_refs[k][g] = als[k]
        zl = jnp.zeros((S5_P, g * S5_PK), F32)
        zr = jnp.zeros((S5_P, (S5_GT - 1 - g) * S5_PK), F32)
        for t in range(L):
            parts = ([zl] if g else []) + [kct[t]] + ([zr] if g < S5_GT - 1 else [])
            kct_rows[t].append(jnp.concatenate(parts, axis=1))
    for tau in range(L):
        bd_ref[0, tau] = jnp.concatenate(bd_rows[tau], axis=0).astype(bd_ref.dtype)
    for t in range(L):
        blk_t = jnp.concatenate(kct_rows[t], axis=0)
        kc_ref[0, :, t * LANES:(t + 1) * LANES] = blk_t.T.astype(kc_ref.dtype)


def s5_prep(w, i, t_effs):
    dup = lambda z: jnp.concatenate([z, z], axis=-1)
    a = jnp.stack([dup(w["o_a_re"][i]), dup(w["o_a_im"][i])], axis=1)
    ldt = jnp.broadcast_to(w["o_log_dt"][i][:, None, None], (S5_G, 1, S5_PK))
    bt_re = jnp.swapaxes(w["o_b_re"][i], 1, 2)
    bt_im = jnp.swapaxes(w["o_b_im"][i], 1, 2)
    b1 = jnp.concatenate([bt_re, bt_im], -1)
    b2 = jnp.concatenate([-bt_im, bt_re], -1)
    c_re, c_im = w["o_c_re"][i], w["o_c_im"][i]
    c1 = jnp.concatenate([c_re, -c_im], -1)
    c2 = jnp.concatenate([-c_im, -c_re], -1)
    n_t = len(t_effs)
    g3 = lambda r, c: pl.BlockSpec((S5_GT, r, c), lambda q: (q, 0, 0))
    outs = pl.pallas_call(
        functools.partial(_s5_prep_kernel, tuple(t_effs)),
        grid=(S5_QT,),
        in_specs=[g3(2, S5_PK), g3(1, S5_PK), g3(S5_P, S5_PK), g3(S5_P, S5_PK), g3(S5_P, S5_PK), g3(S5_P, S5_PK)],
        out_specs=[pl.BlockSpec((1, S5_CHUNK, LANES, LANES), lambda q: (q, 0, 0, 0)),
                   pl.BlockSpec((1, S5_HQ, S5_XW), lambda q: (q, 0, 0))]
                  + [pl.BlockSpec((1, S5_XW, S5_HQ), lambda q: (q, 0, 0))] * n_t
                  + [g3(2, S5_PK)] * n_t,
        out_shape=[jax.ShapeDtypeStruct((S5_QT, S5_CHUNK, LANES, LANES), BF16),
                   jax.ShapeDtypeStruct((S5_QT, S5_HQ, S5_XW), BF16)]
                  + [jax.ShapeDtypeStruct((S5_QT, S5_XW, S5_HQ), BF16)] * n_t
                  + [jax.ShapeDtypeStruct((S5_G, 2, S5_PK), F32)] * n_t,
        compiler_params=_cparams(("parallel",)), name="s5_prep",
    )(a, ldt, b1, b2, c1, c2)
    bd, kc = outs[0], outs[1]
    mats = []
    for k in range(n_t):
        al = outs[2 + n_t + k]
        mats.append((bd, outs[2 + k], kc, al[:, 0, :].reshape(1, S5_HW), al[:, 1, :].reshape(1, S5_HW)))
    return mats


def _s5_e_kernel(u_ref, kb_ref, e_ref):
    e_ref[...] = _bdot(u_ref[0], kb_ref[0])


def _s5_swap(h):
    n = h.shape[-1]
    lane = _iota(h.shape, 1)
    return jnp.where(lane % S5_PK < S5_N, pltpu.roll(h, n - S5_N, 1), pltpu.roll(h, S5_N, 1))


def _s5_scan_kernel(bsz, cg, e_ref, h0_ref, ala_ref, alb_ref, hp_ref, hf_ref, h_scr):
    @pl.when(pl.program_id(0) == 0)
    def _():
        h_scr[...] = h0_ref[...]

    ala = ala_ref[...]
    alb = alb_ref[...]

    def body(c, hs):
        out = []
        for b in range(bsz):
            hp_ref[b, pl.ds(c, 1), :] = hs[b]
            out.append(ala * hs[b] + alb * _s5_swap(hs[b]) + e_ref[b, pl.ds(c, 1), :])
        return tuple(out)

    hs = lax.fori_loop(0, cg, body, tuple(h_scr[b:b + 1, :] for b in range(bsz)))
    for b in range(bsz):
        h_scr[b:b + 1, :] = hs[b]
        hf_ref[b:b + 1, :] = hs[b]


def _s5_step_kernel(e_ref, h0_ref, ala_ref, alb_ref, hf_ref):
    h = h0_ref[...]
    hf_ref[...] = ala_ref[...] * h + alb_ref[...] * _s5_swap(h) + e_ref[...]


def _s5_y_kernel(u_ref, hp_ref, bd_ref, kc_ref, y_ref, k_scr):
    @pl.when(pl.program_id(1) == 0)
    def _():
        zero = jnp.zeros((LANES, LANES), k_scr.dtype)
        for i in range(S5_CHUNK):
            for t in range(S5_CHUNK):
                k_scr[i * LANES:(i + 1) * LANES, t * LANES:(t + 1) * LANES] = bd_ref[0, t - i] if t >= i else zero

    y_ref[0] = _bdot(u_ref[0], k_scr[...]) + _bdot(hp_ref[...], kc_ref[0])


def s5_core(x, h0, bsz, mats):
    bd, kbbig, kcbig, ala, alb = mats
    rows = x.shape[1]
    nc = rows // bsz
    tr = math.gcd(rows, 512)
    e = pl.pallas_call(
        _s5_e_kernel,
        grid=(S5_QT, rows // tr),
        in_specs=[pl.BlockSpec((1, tr, S5_XW), lambda q, r: (q, r, 0)),
                  pl.BlockSpec((1, S5_XW, S5_HQ), lambda q, r: (q, 0, 0))],
        out_specs=pl.BlockSpec((tr, S5_HQ), lambda q, r: (r, q)),
        out_shape=jax.ShapeDtypeStruct((rows, S5_HW), F32),
        compiler_params=_cparams(("parallel", "parallel")), name="s5_e",
    )(x, kbbig)
    row = pl.BlockSpec((1, S5_HW), lambda i: (0, 0))
    if nc == 1:
        hp = h0
        hf = pl.pallas_call(
            _s5_step_kernel,
            out_shape=jax.ShapeDtypeStruct((bsz, S5_HW), F32), name="s5_step",
        )(e, h0, ala, alb)
    else:
        cg = math.gcd(nc, 64)
        hp, hf = pl.pallas_call(
            functools.partial(_s5_scan_kernel, bsz, cg),
            grid=(nc // cg,),
            in_specs=[pl.BlockSpec((bsz, cg, S5_HW), lambda i: (0, i, 0)),
                      pl.BlockSpec((bsz, S5_HW), lambda i: (0, 0)), row, row],
            out_specs=[pl.BlockSpec((bsz, cg, S5_HW), lambda i: (0, i, 0)),
                       pl.BlockSpec((bsz, S5_HW), lambda i: (0, 0))],
            out_shape=[jax.ShapeDtypeStruct((bsz, nc, S5_HW), F32),
                       jax.ShapeDtypeStruct((bsz, S5_HW), F32)],
            scratch_shapes=[pltpu.VMEM((bsz, S5_HW), F32)],
            compiler_params=_cparams(("arbitrary",)), name="s5_scan",
        )(e.reshape(bsz, nc, S5_HW), h0, ala, alb)
    y = pl.pallas_call(
        _s5_y_kernel,
        grid=(S5_QT, rows // tr),
        in_specs=[pl.BlockSpec((1, tr, S5_XW), lambda q, r: (q, r, 0)),
                  pl.BlockSpec((tr, S5_HQ), lambda q, r: (r, q)),
                  pl.BlockSpec((1, S5_CHUNK, LANES, LANES), lambda q, r: (q, 0, 0, 0)),
                  pl.BlockSpec((1, S5_HQ, S5_XW), lambda q, r: (q, 0, 0))],
        out_specs=pl.BlockSpec((1, tr, S5_XW), lambda q, r: (q, r, 0)),
        out_shape=jax.ShapeDtypeStruct((S5_QT, rows, S5_XW), F32),
        scratch_shapes=[pltpu.VMEM((S5_XW, S5_XW), BF16)],
        compiler_params=_cparams(("parallel", "arbitrary")), name="s5_y",
    )(x, hp.reshape(rows, S5_HW), bd, kcbig)
    return y, hf


def _s5_post_kernel(tpr, y_ref, u_ref, d_ref, gw_ref, gb_ref, o_ref, y_scr, u_scr):
    _from_chunk_rows(y_ref, y_scr, tpr)
    _from_chunk_rows(u_ref, u_scr, tpr)
    y = jnp.concatenate([y_scr[q] for q in range(S5_QT)], axis=1)
    u = jnp.concatenate([u_scr[q] for q in range(S5_QT)], axis=1)
    x = y + d_ref[...] * u
    z = 0.5 * x * (1.0 + jnp.tanh(math.sqrt(2.0 / math.pi) * (x + 0.044715 * (x * x * x))))
    o_ref[...] = z * _sigmoid(_dg(z.astype(BF16), gw_ref[...], _NN) + gb_ref[...])


def s5_post(y4, u4, m, tpr, d, glu_w_bf16, glu_b, tm):
    spec4 = pl.BlockSpec((S5_QT, tm // tpr, S5_XW), lambda i: (0, i, 0))
    row = pl.BlockSpec((1, S5_W), lambda i: (0, 0))
    return pl.pallas_call(
        functools.partial(_s5_post_kernel, tpr),
        grid=(m // tm,),
        in_specs=[spec4, spec4, row, pl.BlockSpec((S5_W, S5_W), lambda i: (0, 0)), row],
        out_specs=pl.BlockSpec((tm, S5_W), lambda i: (i, 0)),
        out_shape=jax.ShapeDtypeStruct((m, S5_W), F32),
        scratch_shapes=[pltpu.VMEM((S5_QT, tm, LANES), F32), pltpu.VMEM((S5_QT, tm, LANES), F32)],
        compiler_params=_cparams(("parallel",)), name="s5_post",
    )(y4, u4, d.reshape(1, S5_W), glu_w_bf16, glu_b.reshape(1, S5_W))


def s5_mix(u4, h_re, h_im, bsz, t, mats, d, glu_w_bf16, glu_b, tm):
    tpr = min(t, CHUNK_ROW_TOKENS)
    h0 = jnp.concatenate([h_re, h_im], axis=-1).reshape(bsz, S5_HW)
    y4, hf = s5_core(u4, h0, bsz, mats)
    out = s5_post(y4, u4, bsz * t, tpr, d, glu_w_bf16, glu_b, tm)
    hf = hf.reshape(bsz, S5_G, 2, S5_N)
    return out, hf[:, :, 0], hf[:, :, 1]


GLA_DK = 64
GLA_DV = 128
GLA_HEADS = 4
GLA_KW = GLA_HEADS * GLA_DK
GLA_VW = GLA_HEADS * GLA_DV
GLA_PW = 2 * GLA_KW + 2 * GLA_VW + LANES


def _gla_kernel(L, t_real, nb, p_ref, aup_ref, ab_ref, nw_ref, s0_ref, y_ref, sf_ref, s_scr):
    c = pl.program_id(1)
    nc = pl.num_programs(1)
    n_pairs = GLA_KW // LANES

    @pl.when(c == 0)
    def _():
        for bb in range(nb):
            for j in range(n_pairs):
                s_scr[bb * n_pairs + j] = s0_ref[bb, j]

    lane = _iota((1, LANES), 1)
    masks = (jnp.where(lane < GLA_DK, 1.0, 0.0), jnp.where(lane < GLA_DK, 0.0, 1.0))
    incl = _iota((L, L), 1) <= _iota((L, L), 0)
    diag128 = _iota((LANES, LANES), 0) == _iota((LANES, LANES), 1)
    ones128 = jnp.ones((LANES, LANES), BF16)
    tri = _tri_incl(L)
    rows = range(nb)
    ps = [p_ref[bb] for bb in rows]
    zs = [_bdot(p[:, 2 * GLA_KW + 2 * GLA_VW:], aup_ref[...]) + ab_ref[...] for p in ps]
    gk = [-_softplus(-z) * (1.0 / GLA_GATE_NORM) for z in zs]
    if t_real % L != 0:
        tok = c * L + _iota((L, 1), 0)
        gk = [jnp.where(tok < t_real, x, 0.0) for x in gk]
    b = [_xdot_r(tri, x) for x in gk]
    bl = [x[L - 1:L, :] for x in b]
    qd = [ps[i][:, 0:GLA_KW] * (GLA_DK ** -0.5) * jnp.exp(b[i]) for i in rows]
    kh = [ps[i][:, GLA_KW:2 * GLA_KW] * jnp.exp(-b[i]) for i in rows]
    kt = [ps[i][:, GLA_KW:2 * GLA_KW] * jnp.exp(bl[i] - b[i]) for i in rows]
    heads = [(bb, h) for bb in rows for h in range(GLA_HEADS)]
    sl = lambda h: slice((h // 2) * LANES, (h // 2 + 1) * LANES)
    hs = lambda h: slice(2 * GLA_KW + h * GLA_DV, 2 * GLA_KW + (h + 1) * GLA_DV)
    gs = lambda h: slice(2 * GLA_KW + GLA_VW + h * GLA_DV, 2 * GLA_KW + GLA_VW + (h + 1) * GLA_DV)
    st = [s_scr[bb * n_pairs + j] for bb in rows for j in range(n_pairs)]
    qm = [qd[bb][:, sl(h)] * masks[h % 2] for bb, h in heads]
    vh = [ps[bb][:, hs(h)] for bb, h in heads]
    attn = [jnp.where(incl, _bdot(qm[i], kh[bb][:, sl(h)], _NT), 0.0) for i, (bb, h) in enumerate(heads)]
    o = [_bdot(attn[i], vh[i]) + _bdot(qm[i], st[bb * n_pairs + h // 2]) for i, (bb, h) in enumerate(heads)]
    kv = [_bdot(kt[bb][:, sl(h)], vh[i], _TN) for i, (bb, h) in enumerate(heads)]
    for i, (bb, h) in enumerate(heads):
        of = o[i] * lax.rsqrt(jnp.mean(o[i] * o[i], axis=-1, keepdims=True) + EPS) * nw_ref[...]
        y_ref[bb, :, h * GLA_DV:(h + 1) * GLA_DV] = of * _silu(ps[bb][:, gs(h)])
    for bb in rows:
        for j in range(n_pairs):
            i0 = bb * GLA_HEADS + 2 * j
            pcol = _xdot_l(jnp.where(diag128, jnp.exp(bl[bb][:, j * LANES:(j + 1) * LANES]), 0.0), ones128)
            s_scr[bb * n_pairs + j] = pcol * st[bb * n_pairs + j] + jnp.concatenate(
                [kv[i0][0:GLA_DK], kv[i0 + 1][GLA_DK:2 * GLA_DK]], axis=0)

    @pl.when(c == nc - 1)
    def _():
        for bb in range(nb):
            for j in range(n_pairs):
                sf_ref[bb, j] = s_scr[bb * n_pairs + j]


def gla_mix(p_gla, s0, bsz, t, L, aup_pad, a_b, norm_w, nb=1):
    tp = -(-t // L) * L
    p3 = p_gla.reshape(bsz, t, GLA_PW)
    if tp != t:
        p3 = jnp.pad(p3, ((0, 0), (0, tp - t), (0, 0)))
    n_pairs = GLA_KW // LANES
    st_spec = pl.BlockSpec((nb, n_pairs, LANES, LANES), lambda b, c: (b, 0, 0, 0))
    y, s_fin = pl.pallas_call(
        functools.partial(_gla_kernel, L, t, nb),
        grid=(bsz // nb, tp // L),
        in_specs=[pl.BlockSpec((nb, L, GLA_PW), lambda b, c: (b, c, 0)),
                  pl.BlockSpec((LANES, GLA_KW), lambda b, c: (0, 0)),
                  pl.BlockSpec((1, GLA_KW), lambda b, c: (0, 0)),
                  pl.BlockSpec((1, GLA_DV), lambda b, c: (0, 0)),
                  st_spec],
        out_specs=[pl.BlockSpec((nb, L, GLA_VW), lambda b, c: (b, c, 0)), st_spec],
        out_shape=[jax.ShapeDtypeStruct((bsz, tp, GLA_VW), F32),
                   jax.ShapeDtypeStruct((bsz, n_pairs, LANES, LANES), F32)],
        scratch_shapes=[pltpu.VMEM((nb * n_pairs, LANES, LANES), F32)],
        compiler_params=_cparams(("parallel", "arbitrary")), name="gla",
    )(p3, aup_pad, a_b.reshape(1, GLA_KW), norm_w.reshape(1, GLA_DV),
      s0.reshape(bsz, n_pairs, LANES, LANES))
    return y[:, :t].reshape(bsz * t, GLA_VW), s_fin.reshape(bsz, GLA_HEADS, GLA_DK, GLA_DV)


GLA_LR = 16
D_FF_CHUNK = 1408
D_FF_EXPERT_CHUNK = 896
FFN_TOKENS = 1024
ROUTER_TOKENS = 1024
MOE_TOKENS = 2048


def _prepare_weights(w):
    win = w["e_w_in"][0]
    w_gla = jnp.pad(win[:, RSHIFT:], ((0, 0), (0, LANES - GLA_LR)))
    wo = w["o_w_in"][0]
    return {
        "e_w_rwkv": win[:, :RSHIFT].astype(BF16),
        "e_w_gla": w_gla.astype(BF16),
        "rwkv": rwkv_params(w, 0),
        "gla_aup": jnp.pad(w["e_gla_a_up"][0], ((0, LANES - GLA_LR), (0, 0))).astype(BF16),
        "e_wo_a": w["e_w_out"][0][:RW].astype(BF16),
        "e_wo_b": w["e_w_out"][0][RW:].astype(BF16),
        "ff_w1": w["e_ff_w1"][0].astype(BF16),
        "ff_w3": w["e_ff_w3"][0].astype(BF16),
        "ff_w2": w["e_ff_w2"][0].astype(BF16),
        "o_w_q": wo[:, :QW].astype(BF16),
        "o_w_kv": wo[:, QW:QW + 2 * KW].astype(BF16),
        "o_w_u": wo[:, QW + 2 * KW:].astype(BF16),
        "glu_w": w["o_glu_w"][0].astype(BF16),
        "o_wo_a": w["o_w_out"][0][:QW].astype(BF16),
        "o_wo_b": w["o_w_out"][0][QW:].astype(BF16),
        "moe_w1": w["o_moe_w1"][0].astype(BF16),
        "moe_w3": w["o_moe_w3"][0].astype(BF16),
        "moe_w2": w["o_moe_w2"][0].astype(BF16),
    }


def _trunk(x3, st, w, pw, tm, chunk, nb, s5_prep_t, prompt_bias):
    bsz, t, d = x3.shape
    m = bsz * t
    x = x3.reshape(m, d)
    p_r, p_g = norm_proj(x, w["e_norm1"][0], [pw["e_w_rwkv"], pw["e_w_gla"]], tm)
    ya, s_rwkv = rwkv_mix(p_r, st["shift"], st["rwkv"], bsz, t, chunk, pw["rwkv"],
                          w["e_lnx_w"][0], w["e_lnx_b"][0], nb)
    s_shift = p_r.reshape(bsz, t, RSHIFT)[:, -1]
    yb, s_gla = gla_mix(p_g, st["gla"], bsz, t, chunk, pw["gla_aup"], w["e_gla_a_b"][0], w["e_gla_norm"][0], nb)
    x = out_proj(x, ya, yb, pw["e_wo_a"], pw["e_wo_b"], tm)
    x = ffn(x, w["e_norm2"][0], pw["ff_w1"], pw["ff_w3"], pw["ff_w2"], math.gcd(m, FFN_TOKENS), D_FF_CHUNK)
    q, kv, u = norm_proj_tiles(x, w["o_norm1"][0], [pw["o_w_q"], pw["o_w_kv"], pw["o_w_u"]], tm,
                               min(t, CHUNK_ROW_TOKENS))
    if st["win_k"] is None:
        yc, nk, nv = swa_prompt(q, kv, bsz, t, w["o_q_norm"][0], w["o_k_norm"][0], prompt_bias, w["o_sinks"][0])
    else:
        yc, nk, nv = swa_decode(q, kv, st["win_k"].reshape(bsz, WINDOW, KW), st["win_v"].reshape(bsz, WINDOW, KW),
                                bsz, t, w["o_q_norm"][0], w["o_k_norm"][0], w["rel_table"], w["o_sinks"][0], 8)
    yd, s5r, s5i = s5_mix(u, st["s5_re"], st["s5_im"], bsz, t, s5_prep_t, w["o_d"][0], pw["glu_w"],
                          w["o_glu_b"][0], tm)
    x = out_proj(x, yc, yd, pw["o_wo_a"], pw["o_wo_b"], tm)
    t_router = min(m, ROUTER_TOKENS)
    hn, gates, pos, counts = router(x, w["o_norm2"][0], w["o_router_w"][0], w["o_router_b"][0], t_router)
    x = moe(x, hn, gates, pos, counts, pw["moe_w1"], pw["moe_w3"], pw["moe_w2"], t_router, min(m, MOE_TOKENS),
            D_FF_EXPERT_CHUNK)
    kv_shape = (bsz, WINDOW, N_KV_HEADS, ATT_HD)
    return (x.reshape(bsz, t, d), s_rwkv[None], s_shift[None], s_gla[None], nk.reshape(kv_shape)[None],
            nv.reshape(kv_shape)[None], s5r[None], s5i[None])


def kernel(x_prompt, x_sample, state_rwkv, state_shift, state_gla, cache_win_k, cache_win_v, state_s5_re,
           state_s5_im, rel_table, e_norm1, e_w_in, e_mu, e_w0, e_w_up, e_a0, e_a_up, e_g_up, e_k_k, e_k_a, e_r_k,
           e_lnx_w, e_lnx_b, e_gla_a_up, e_gla_a_b, e_gla_norm, e_w_out, e_norm2, e_ff_w1, e_ff_w3, e_ff_w2,
           o_norm1, o_w_in, o_q_norm, o_k_norm, o_sinks, o_a_re, o_a_im, o_log_dt, o_b_re, o_b_im, o_c_re, o_c_im,
           o_d, o_glu_w, o_glu_b, o_w_out, o_norm2, o_router_w, o_router_b, o_moe_w1, o_moe_w3, o_moe_w2):
    w = dict(rel_table=rel_table, e_norm1=e_norm1, e_w_in=e_w_in, e_mu=e_mu, e_w0=e_w0, e_w_up=e_w_up, e_a0=e_a0,
             e_a_up=e_a_up, e_g_up=e_g_up, e_k_k=e_k_k, e_k_a=e_k_a, e_r_k=e_r_k, e_lnx_w=e_lnx_w, e_lnx_b=e_lnx_b,
             e_gla_a_up=e_gla_a_up, e_gla_a_b=e_gla_a_b, e_gla_norm=e_gla_norm, e_w_out=e_w_out, e_norm2=e_norm2,
             e_ff_w1=e_ff_w1, e_ff_w3=e_ff_w3, e_ff_w2=e_ff_w2, o_norm1=o_norm1, o_w_in=o_w_in, o_q_norm=o_q_norm,
             o_k_norm=o_k_norm, o_sinks=o_sinks, o_a_re=o_a_re, o_a_im=o_a_im, o_log_dt=o_log_dt, o_b_re=o_b_re,
             o_b_im=o_b_im, o_c_re=o_c_re, o_c_im=o_c_im, o_d=o_d, o_glu_w=o_glu_w, o_glu_b=o_glu_b,
             o_w_out=o_w_out, o_norm2=o_norm2, o_router_w=o_router_w, o_router_b=o_router_b, o_moe_w1=o_moe_w1,
             o_moe_w3=o_moe_w3, o_moe_w2=o_moe_w2)
    pw = _prepare_weights(w)
    bp, tp, _ = x_prompt.shape
    bs, ts, _ = x_sample.shape
    qi = np.arange(WINDOW)[:, None]
    kj = np.arange(2 * WINDOW)[None, :]
    prompt_bias = rel_bias(rel_table, qi + WINDOW - kj)
    zeros = lambda *shape: jnp.zeros(shape, F32)
    st_p = {"rwkv": zeros(bp, RW // RWKV_HD, RWKV_HD, RWKV_HD), "shift": zeros(bp, RSHIFT),
            "gla": zeros(bp, GLA_HEADS, GLA_DK, GLA_DV), "win_k": None, "win_v": None,
            "s5_re": zeros(bp, S5_G, S5_N), "s5_im": zeros(bp, S5_G, S5_N)}
    st_s = {"rwkv": state_rwkv[0], "shift": state_shift[0], "gla": state_gla[0], "win_k": cache_win_k[0],
            "win_v": cache_win_v[0], "s5_re": state_s5_re[0], "s5_im": state_s5_im[0]}
    s5_p, s5_s = s5_prep(w, 0, (S5_CHUNK, ts))
    out_p = _trunk(x_prompt, st_p, w, pw, 512, 64, math.gcd(bp, 2), s5_p, prompt_bias)
    out_s = _trunk(x_sample, st_s, w, pw, bs * ts, 8, math.gcd(bs, 8), s5_s, None)
    res = [out_p[0], out_s[0]]
    for a, b in zip(out_p[1:], out_s[1:]):
        res += [a, b]
    return tuple(res)
```

```python
import functools
import math

import jax
import jax.numpy as jnp
import numpy as np
from jax import lax
from jax.experimental import pallas as pl
from jax.experimental.pallas import tpu as pltpu

F32 = jnp.float32
BF16 = jnp.bfloat16

LANES = 128
SUBLANES = 8
VMEM_LIMIT_BYTES = 56 * 1024 * 1024

EPS = 1e-6
RWKV_HD = 64
RWKV_GN_EPS = 64e-5
GLA_GATE_NORM = 16.0
ATT_HD = 64
WINDOW = 128
N_BUCKETS = 32
BUCKET_MAX_DIST = 128
S5_P = 16
S5_N = 64
S5_CHUNK = 16
TOP_K = 2


def _cparams(sem):
    return pltpu.CompilerParams(dimension_semantics=sem, vmem_limit_bytes=VMEM_LIMIT_BYTES)


_NN = (((1,), (0,)), ((), ()))
_NT = (((1,), (1,)), ((), ()))
_TN = (((0,), (0,)), ((), ()))


def _dg(a, b, dims):
    return lax.dot_general(a, b, dims, preferred_element_type=F32)


def _bdot(a, b, dims=_NN):
    return _dg(a.astype(BF16), b.astype(BF16), dims)


def _split(a, n):
    terms = []
    r = a
    for _ in range(n):
        t = r.astype(BF16)
        terms.append(t)
        r = r - t.astype(F32)
    return terms


def _hdot(a, b, dims=_NN):
    a0, a1 = _split(a, 2)
    b0, b1 = _split(b, 2)
    return _dg(a0, b0, dims) + (_dg(a0, b1, dims) + _dg(a1, b0, dims))


def _xdot_l(a, e, dims=_NN):
    e = e.astype(BF16)
    a0, a1, a2 = _split(a, 3)
    return _dg(a0, e, dims) + (_dg(a1, e, dims) + _dg(a2, e, dims))


def _xdot_r(e, b, dims=_NN):
    e = e.astype(BF16)
    b0, b1, b2 = _split(b, 3)
    return _dg(e, b0, dims) + (_dg(e, b1, dims) + _dg(e, b2, dims))


def _iota(shape, axis):
    return lax.broadcasted_iota(jnp.int32, shape, axis)


def _seg_ones(n, seg):
    r = _iota((n, n), 0) // seg
    c = _iota((n, n), 1) // seg
    return jnp.where(r == c, 1.0, 0.0).astype(BF16)


def _seg_sum(x, seg):
    n = x.shape[-1]
    ones = _seg_ones(n, seg)
    x0, x1 = _split(x, 2)
    return _dg(x0, ones, _NN) + _dg(x1, ones, _NN)


def _sigmoid(x):
    return 1.0 / (1.0 + jnp.exp(-x))


def _silu(x):
    return x * _sigmoid(x)


def _softplus(x):
    return jnp.maximum(x, 0.0) + jnp.log(1.0 + jnp.exp(-jnp.abs(x)))


def _tri_incl(n):
    r = _iota((n, n), 0)
    c = _iota((n, n), 1)
    return jnp.where(c <= r, 1.0, 0.0).astype(BF16)


def _rms(x, g):
    return x * lax.rsqrt(jnp.mean(x * x, axis=-1, keepdims=True) + EPS) * g


def _norm_proj_kernel(n_w, x_ref, g_ref, *refs):
    xn = _rms(x_ref[...], g_ref[...]).astype(BF16)
    for w_ref, o_ref in zip(refs[:n_w], refs[n_w:]):
        o_ref[...] = _dg(xn, w_ref[...], _NN)


def norm_proj(x, g, ws_bf16, tm):
    m, d = x.shape
    return pl.pallas_call(
        functools.partial(_norm_proj_kernel, len(ws_bf16)),
        grid=(m // tm,),
        in_specs=[pl.BlockSpec((tm, d), lambda i: (i, 0)),
                  pl.BlockSpec((1, d), lambda i: (0, 0))]
                 + [pl.BlockSpec(w.shape, lambda i: (0, 0)) for w in ws_bf16],
        out_specs=[pl.BlockSpec((tm, w.shape[1]), lambda i: (i, 0)) for w in ws_bf16],
        out_shape=[jax.ShapeDtypeStruct((m, w.shape[1]), F32) for w in ws_bf16],
        compiler_params=_cparams(("parallel",)), name="norm_proj",
    )(x, g.reshape(1, d), *ws_bf16)


CHUNK_ROW_TOKENS = 16


def _to_chunk_rows(src_ref, dst_ref, tpr):
    rows = src_ref.shape[1] // tpr
    for q in range(dst_ref.shape[0]):
        for i in range(CHUNK_ROW_TOKENS):
            cols = slice(i * LANES, (i + 1) * LANES)
            if i < tpr:
                dst_ref[q, :, cols] = src_ref[q, pl.ds(i, rows, stride=tpr), :]
            else:
                dst_ref[q, :, cols] = jnp.zeros((rows, LANES), dst_ref.dtype)


def _from_chunk_rows(src_ref, dst_ref, tpr):
    rows = dst_ref.shape[1] // tpr
    for q in range(src_ref.shape[0]):
        for i in range(tpr):
            dst_ref[q, pl.ds(i, rows, stride=tpr), :] = src_ref[q, :, i * LANES:(i + 1) * LANES]


def _norm_proj_tiles_kernel(n_w, tpr, x_ref, g_ref, *refs):
    xn = _rms(x_ref[...], g_ref[...]).astype(BF16)
    for w_ref, o_ref in zip(refs[:n_w - 1], refs[n_w:]):
        o_ref[...] = _dg(xn, w_ref[...], _NN)
    u_ref, u_scr = refs[-2], refs[-1]
    u = _dg(xn, refs[n_w - 1][...], _NN)
    for q in range(u_scr.shape[0]):
        u_scr[q] = u[:, q * LANES:(q + 1) * LANES]
    _to_chunk_rows(u_scr, u_ref, tpr)


def norm_proj_tiles(x, g, ws_bf16, tm, tpr):
    m, d = x.shape
    nu = ws_bf16[-1].shape[1]
    nt = nu // LANES
    cw = CHUNK_ROW_TOKENS * LANES
    return pl.pallas_call(
        functools.partial(_norm_proj_tiles_kernel, len(ws_bf16), tpr),
        grid=(m // tm,),
        in_specs=[pl.BlockSpec((tm, d), lambda i: (i, 0)),
                  pl.BlockSpec((1, d), lambda i: (0, 0))]
                 + [pl.BlockSpec(w.shape, lambda i: (0, 0)) for w in ws_bf16],
        out_specs=[pl.BlockSpec((tm, w.shape[1]), lambda i: (i, 0)) for w in ws_bf16[:-1]]
                  + [pl.BlockSpec((nt, tm // tpr, cw), lambda i: (0, i, 0))],
        out_shape=[jax.ShapeDtypeStruct((m, w.shape[1]), F32) for w in ws_bf16[:-1]]
                  + [jax.ShapeDtypeStruct((nt, m // tpr, cw), F32)],
        scratch_shapes=[pltpu.VMEM((nt, tm, LANES), F32)],
        compiler_params=_cparams(("parallel",)), name="norm_proj_tiles",
    )(x, g.reshape(1, d), *ws_bf16)


def _mixer_out(x_ref, ya_ref, yb_ref, wa_ref, wb_ref):
    return (x_ref[...] + _dg(ya_ref[...].astype(BF16), wa_ref[...], _NN)
            + _dg(yb_ref[...].astype(BF16), wb_ref[...], _NN))


def _ffn_kernel(x_ref, ya_ref, yb_ref, wa_ref, wb_ref, g_ref, w1_ref, w3_ref, w2_ref, o_ref, xn_scr):
    j = pl.program_id(1)

    @pl.when(j == 0)
    def _():
        x = _mixer_out(x_ref, ya_ref, yb_ref, wa_ref, wb_ref)
        xn_scr[...] = _rms(x, g_ref[...]).astype(BF16)
        o_ref[...] = x

    xn = xn_scr[...]
    h = _silu(_dg(xn, w1_ref[...], _NN)) * _dg(xn, w3_ref[...], _NN)
    o_ref[...] += _dg(h.astype(BF16), w2_ref[...], _NN)


def ffn(x, ya, yb, wa, wb, g, w1, w3, w2, tm, fc):
    m, d = x.shape
    dff = w1.shape[1]
    tok = lambda n: pl.BlockSpec((tm, n), lambda i, j: (i, 0))
    full = lambda a: pl.BlockSpec(a.shape, lambda i, j: (0, 0))
    return pl.pallas_call(
        _ffn_kernel,
        grid=(m // tm, dff // fc),
        in_specs=[tok(d), tok(ya.shape[1]), tok(yb.shape[1]), full(wa), full(wb),
                  pl.BlockSpec((1, d), lambda i, j: (0, 0)),
                  pl.BlockSpec((d, fc), lambda i, j: (0, j)),
                  pl.BlockSpec((d, fc), lambda i, j: (0, j)),
                  pl.BlockSpec((fc, d), lambda i, j: (j, 0))],
        out_specs=pl.BlockSpec((tm, d), lambda i, j: (i, 0)),
        out_shape=jax.ShapeDtypeStruct((m, d), F32),
        scratch_shapes=[pltpu.VMEM((tm, d), BF16)],
        compiler_params=_cparams(("parallel", "arbitrary")), name="ffn",
    )(x, ya, yb, wa, wb, g.reshape(1, d), w1, w3, w2)


def _router_kernel(n_exp, x_ref, ya_ref, yb_ref, wa_ref, wb_ref, g_ref, rw_ref, rb_ref,
                   xo_ref, hn_ref, gate_ref, pos_ref, cnt_ref):
    x = _mixer_out(x_ref, ya_ref, yb_ref, wa_ref, wb_ref)
    xo_ref[...] = x
    xn = _rms(x, g_ref[...])
    hn_ref[...] = xn.astype(BF16)
    logits = _hdot(xn, rw_ref[...]) + rb_ref[...]
    lane = _iota(logits.shape, 1)
    logits = jnp.where(lane < n_exp, logits, -jnp.inf)
    m1 = jnp.max(logits, axis=-1, keepdims=True)
    i1 = jnp.min(jnp.where(logits == m1, lane, LANES), axis=-1, keepdims=True)
    rest = jnp.where(lane == i1, -jnp.inf, logits)
    m2 = jnp.max(rest, axis=-1, keepdims=True)
    i2 = jnp.min(jnp.where(rest == m2, lane, LANES), axis=-1, keepdims=True)
    e2 = jnp.exp(m2 - m1)
    g1 = 1.0 / (1.0 + e2)
    g2 = e2 / (1.0 + e2)
    pick1 = lane == i1
    pick2 = lane == i2
    gate_ref[...] = jnp.where(pick1, g1, 0.0) + jnp.where(pick2, g2, 0.0)
    tm = logits.shape[0]
    sel = jnp.where(jnp.logical_or(pick1, pick2), 1.0, 0.0).astype(BF16)
    tr = _iota((tm, tm), 0)
    tc = _iota((tm, tm), 1)
    upper = jnp.where(tr <= tc, 1.0, 0.0).astype(BF16)
    eye = jnp.where(tr == tc, 1.0, 0.0).astype(BF16)
    rank_t = _dg(sel, upper, _TN)
    sel_t = _dg(sel, eye, _TN)
    pos_t = jnp.where(sel_t > 0.5, rank_t - 1.0, -1.0)
    pos_ref[...] = pos_t[0:SUBLANES, :]
    cnt_ref[0] = jnp.sum(sel.astype(F32), axis=0, keepdims=True)


def router(x, ya, yb, wa, wb, g, rw, rb, tm):
    m, d = x.shape
    n_exp = rw.shape[1]
    assert n_exp <= SUBLANES
    rw_pad = jnp.pad(rw, ((0, 0), (0, LANES - n_exp)))
    rb_pad = jnp.pad(rb, (0, LANES - n_exp)).reshape(1, LANES)
    tok = lambda n: pl.BlockSpec((tm, n), lambda i: (i, 0))
    full = lambda a: pl.BlockSpec(a.shape, lambda i: (0, 0))
    return pl.pallas_call(
        functools.partial(_router_kernel, n_exp),
        grid=(m // tm,),
        in_specs=[tok(d), tok(ya.shape[1]), tok(yb.shape[1]), full(wa), full(wb),
                  pl.BlockSpec((1, d), lambda i: (0, 0)),
                  pl.BlockSpec((d, LANES), lambda i: (0, 0)),
                  pl.BlockSpec((1, LANES), lambda i: (0, 0))],
        out_specs=[tok(d), tok(d), tok(LANES),
                   pl.BlockSpec((SUBLANES, tm), lambda i: (0, i)),
                   pl.BlockSpec((1, 1, LANES), lambda i: (i, 0, 0))],
        out_shape=[jax.ShapeDtypeStruct((m, d), F32),
                   jax.ShapeDtypeStruct((m, d), BF16), jax.ShapeDtypeStruct((m, LANES), F32),
                   jax.ShapeDtypeStruct((SUBLANES, m), F32),
                   jax.ShapeDtypeStruct((m // tm, 1, LANES), F32)],
        compiler_params=_cparams(("parallel",)), name="router",
    )(x, ya, yb, wa, wb, g.reshape(1, d), rw_pad, rb_pad)


MOE_ROWS = 128


def _moe_kernel(n_exp, cnt_ref, x_hbm, hn_ref, gate_ref, pos_ref, w1_ref, w3_ref, w2_ref, o_ref, xs_scr, y_scr):
    i = pl.program_id(0)
    e = pl.program_id(1)
    j = pl.program_id(2)
    nj = pl.num_programs(2)
    tm = hn_ref.shape[0]
    n_small = (cnt_ref[i * n_exp + e] + (MOE_ROWS - 1)) // MOE_ROWS

    @pl.when(jnp.logical_and(e == 0, j == 0))
    def _():
        pltpu.sync_copy(x_hbm.at[pl.ds(pl.multiple_of(i * tm, tm), tm), :], o_ref)

    def select(start, nrows):
        pos = pos_ref[pl.ds(e, 1), :]
        want = (start + _iota((nrows, tm), 0)).astype(F32)
        return jnp.where(pos == want, 1.0, 0.0).astype(BF16)

    def blocks(body):
        big = 4 * MOE_ROWS
        assert tm % big == 0

        def run_big(blk, carry):
            body(pl.multiple_of(blk * big, big), big)
            return carry
        lax.fori_loop(0, n_small // 4, run_big, 0)
        done = (n_small // 4) * big

        @pl.when(n_small % 4 >= 2)
        def _():
            body(pl.multiple_of(done, big), 2 * MOE_ROWS)

        @pl.when(n_small % 2 == 1)
        def _():
            body(pl.multiple_of(done + (n_small % 4 // 2) * 2 * MOE_ROWS, MOE_ROWS), MOE_ROWS)

    @pl.when(j == 0)
    def _():
        def gather(start, nrows):
            rows = pl.ds(start, nrows)
            xs_scr[rows, :] = _dg(select(start, nrows), hn_ref[...], _NN).astype(BF16)
            y_scr[rows, :] = jnp.zeros((nrows, y_scr.shape[1]), F32)
        blocks(gather)

    def expert(start, nrows):
        rows = pl.ds(start, nrows)
        xs = xs_scr[rows, :]
        h = _silu(_dg(xs, w1_ref[0], _NN)) * _dg(xs, w3_ref[0], _NN)
        y_scr[rows, :] += _dg(h.astype(BF16), w2_ref[0], _NN)
    blocks(expert)

    @pl.when(j == nj - 1)
    def _():
        gt = gate_ref[...]
        gcol = jnp.sum(jnp.where(_iota(gt.shape, 1) == e, gt, 0.0), axis=-1, keepdims=True)

        def scatter(start, nrows):
            o_ref[...] += gcol * _dg(select(start, nrows), y_scr[pl.ds(start, nrows), :].astype(BF16), _TN)
        blocks(scatter)


def moe(x, hn, gates, pos, counts, w1, w3, w2, tm_router, tm, fc):
    m, d = x.shape
    n_exp, _, dff = w1.shape
    nj = dff // fc
    ratio = tm // tm_router
    cnt = counts[:, 0, :n_exp].astype(jnp.int32).reshape(m // tm, ratio, n_exp)
    before = (jnp.cumsum(cnt, axis=1) - cnt).astype(F32)
    shift = jnp.repeat(before.reshape(m // tm_router, n_exp).T, tm_router, axis=1)
    shift = jnp.pad(shift, ((0, SUBLANES - n_exp), (0, 0)))
    pos = jnp.where(pos >= 0, pos + shift, pos)
    cnt = cnt.sum(axis=1).reshape(-1)
    grid_spec = pltpu.PrefetchScalarGridSpec(
        num_scalar_prefetch=1,
        grid=(m // tm, n_exp, nj),
        in_specs=[pl.BlockSpec(memory_space=pl.ANY),
                  pl.BlockSpec((tm, d), lambda i, e, j, c: (i, 0)),
                  pl.BlockSpec((tm, LANES), lambda i, e, j, c: (i, 0)),
                  pl.BlockSpec((SUBLANES, tm), lambda i, e, j, c: (0, i)),
                  pl.BlockSpec((1, d, fc), lambda i, e, j, c: (e, 0, j)),
                  pl.BlockSpec((1, d, fc), lambda i, e, j, c: (e, 0, j)),
                  pl.BlockSpec((1, fc, d), lambda i, e, j, c: (e, j, 0))],
        out_specs=pl.BlockSpec((tm, d), lambda i, e, j, c: (i, 0)),
        scratch_shapes=[pltpu.VMEM((tm, d), BF16), pltpu.VMEM((tm, d), F32)])
    return pl.pallas_call(
        functools.partial(_moe_kernel, n_exp),
        grid_spec=grid_spec,
        out_shape=jax.ShapeDtypeStruct((m, d), F32),
        compiler_params=_cparams(("parallel", "arbitrary", "arbitrary")), name="moe",
    )(cnt, x, hn, gates, pos, w1, w3, w2)


_RW_DOT_A = _bdot
_RW_DOT_T = _bdot
_RW_DOT_W = _bdot
_RW_DOT_S = _bdot


RW = 512
RSHIFT = 1792


def _rwkv_token_maps(xs, w0_ref, wup_ref, a0_ref, aup_ref, gup_ref, kk_ref, ka_ref, rk_ref):
    r = xs[:, 0:RW]
    k = xs[:, RW:2 * RW]
    v = xs[:, 2 * RW:3 * RW]
    lr = xs[:, 3 * RW:3 * RW + LANES]
    gd = xs[:, 3 * RW + LANES:3 * RW + 2 * LANES]
    w_pre = w0_ref[...] + _bdot(jnp.tanh(lr), wup_ref[...])
    logw = -jnp.exp(-_softplus(-w_pre) - 0.5)
    a = _sigmoid(a0_ref[...] + _bdot(lr, aup_ref[...]))
    g = _bdot(_sigmoid(gd), gup_ref[...])
    kkr = k * kk_ref[...]
    kk = kkr / jnp.maximum(jnp.sqrt(_seg_sum(kkr * kkr, RWKV_HD)), 1e-12)
    k2 = k * (1.0 + (a - 1.0) * ka_ref[...])
    bonus = _seg_sum(r * k2 * rk_ref[...], RWKV_HD) * v
    return r, logw, k2, v, -kk, kk * a, g, bonus


def _rwkv_kernel(L, nb, n_pairs, t_real, p_ref, prev0_ref, s0_ref, mu_ref, w0_ref, wup_ref, a0_ref, aup_ref,
                 gup_ref, kk_ref, ka_ref, rk_ref, lnw_ref, lnb_ref, y_ref, sf_ref, s_scr, prev_scr):
    c = pl.program_id(1)
    nc = pl.num_programs(1)
    L2 = 2 * L
    lane = _iota((1, LANES), 1)
    m0 = jnp.where(lane < RWKV_HD, 1.0, 0.0)
    m1 = 1.0 - m0
    rr = _iota((L2, L2), 0)
    cc = _iota((L2, L2), 1)
    same = (rr // L) == (cc // L)
    strict = jnp.logical_and(same, (cc % L) < (rr % L))
    incl = jnp.logical_and(same, (cc % L) <= (rr % L))
    eye2 = jnp.where(rr == cc, 1.0, 0.0)
    r128 = _iota((LANES, LANES), 0)
    c128 = _iota((LANES, LANES), 1)
    blk128 = (r128 // RWKV_HD) == (c128 // RWKV_HD)
    diag128 = r128 == c128
    fmat = jnp.where(_iota((LANES, RWKV_HD), 0) % RWKV_HD == _iota((LANES, RWKV_HD), 1), 1.0, 0.0)
    tri = _tri_incl(L)

    def bd(x):
        return jnp.concatenate([x * m0, x * m1], axis=0)

    @pl.when(c == 0)
    def _():
        for bb in range(nb):
            prev_scr[bb, 0:1, :] = prev0_ref[bb]
            for j in range(n_pairs):
                s0 = s0_ref[bb, j]
                st = _xdot_r(fmat, s0, _NT)
                s_scr[bb * n_pairs + j] = jnp.where(blk128, st, 0.0)

    rowi = _iota((L, 1), 0)
    xs = []
    for bb in range(nb):
        p = p_ref[bb]
        prev = jnp.where(rowi == 0, prev_scr[bb, 0:1, :], pltpu.roll(p, 1, 0))
        prev_scr[bb, 0:1, :] = p[L - 1:L, :]
        xs.append(p + (prev - p) * mu_ref[...])
    maps = _rwkv_token_maps(jnp.concatenate(xs, axis=0), w0_ref, wup_ref, a0_ref, aup_ref, gup_ref,
                            kk_ref, ka_ref, rk_ref)
    if t_real % L != 0:
        valid = (c * L + _iota((nb * L, 1), 0) % L) < t_real
        maps = tuple(jnp.where(valid, z, 0.0) for z in maps)
    toks = [tuple(z[bb * L:(bb + 1) * L] for z in maps) for bb in range(nb)]

    pairs = range(nb * n_pairs)
    bbs = [i // n_pairs for i in pairs]
    sls = [slice((i % n_pairs) * LANES, (i % n_pairs + 1) * LANES) for i in pairs]
    tok = lambda j, which: toks[bbs[j]][which][:, sls[j]]
    lw = [tok(j, 1) for j in pairs]
    b = [_xdot_r(tri, x) for x in lw]
    bl = [x[L - 1:L, :] for x in b]
    e_b = [jnp.exp(x) for x in b]
    e_nb = [jnp.exp(-x) for x in b]
    e_lb = [jnp.exp(bl[j] - b[j]) for j in pairs]
    at = [bd(tok(j, 4) * jnp.exp(b[j] - lw[j])) for j in pairs]
    rt = [bd(tok(j, 0) * e_b[j]) for j in pairs]
    bh = [bd(tok(j, 5) * e_nb[j]) for j in pairs]
    kh = [bd(tok(j, 2) * e_nb[j]) for j in pairs]
    bt = [bd(tok(j, 5) * e_lb[j]) for j in pairs]
    kt = [bd(tok(j, 2) * e_lb[j]) for j in pairs]
    vb = [bd(tok(j, 3)) for j in pairs]
    if L2 % LANES == 0:
        gq = [_RW_DOT_A(jnp.concatenate([at[j], rt[j]], axis=0), jnp.concatenate([bh[j], kh[j]], axis=0), _NT)
              for j in pairs]
        a_ab = [jnp.where(strict, g[0:L2, 0:L2], 0.0) for g in gq]
        a_ak = [jnp.where(strict, g[0:L2, L2:2 * L2], 0.0) for g in gq]
        a_rb = [jnp.where(incl, g[L2:2 * L2, 0:L2], 0.0) for g in gq]
        a_rk = [jnp.where(incl, g[L2:2 * L2, L2:2 * L2], 0.0) for g in gq]
    else:
        a_ab = [jnp.where(strict, _RW_DOT_A(at[j], bh[j], _NT), 0.0) for j in pairs]
        a_ak = [jnp.where(strict, _RW_DOT_A(at[j], kh[j], _NT), 0.0) for j in pairs]
        a_rb = [jnp.where(incl, _RW_DOT_A(rt[j], bh[j], _NT), 0.0) for j in pairs]
        a_rk = [jnp.where(incl, _RW_DOT_A(rt[j], kh[j], _NT), 0.0) for j in pairs]
    x = a_ab
    tinv = [eye2 + a for a in a_ab]
    span = 2
    while span < L:
        x = [_RW_DOT_T(xx, xx) for xx in x]
        tinv = [tinv[j] + _RW_DOT_T(tinv[j], x[j]) for j in pairs]
        span *= 2
    akv = [_RW_DOT_W(a_ak[j], vb[j]) for j in pairs]
    wuv = [_RW_DOT_W(tinv[j], jnp.concatenate([at[j], akv[j]], axis=1)) for j in pairs]
    mn = [_RW_DOT_W(bt[j], wuv[j], _TN) for j in pairs]
    qy = [_RW_DOT_W(a_rb[j], wuv[j]) for j in pairs]
    mm = [jnp.where(diag128, jnp.exp(bl[j]), 0.0) + mn[j][:, 0:LANES] for j in pairs]
    nn = [mn[j][:, LANES:2 * LANES] + _RW_DOT_W(kt[j], vb[j], _TN) for j in pairs]
    q = [rt[j] + qy[j][:, 0:LANES] for j in pairs]
    yv = [qy[j][:, LANES:2 * LANES] + _RW_DOT_W(a_rk[j], vb[j]) for j in pairs]
    s = [s_scr[j] for j in pairs]
    ybd = [_RW_DOT_S(q[j], s[j]) + yv[j] for j in pairs]
    for j in pairs:
        s_scr[j] = _RW_DOT_S(mm[j], s[j]) + nn[j]

    y = jnp.concatenate(
        [jnp.concatenate([ybd[bb * n_pairs + jj][0:L] + ybd[bb * n_pairs + jj][L:L2] for jj in range(n_pairs)],
                         axis=1) for bb in range(nb)], axis=0)
    mean = _seg_sum(y, RWKV_HD) * (1.0 / RWKV_HD)
    d = y - mean
    var = _seg_sum(d * d, RWKV_HD) * (1.0 / RWKV_HD)
    ya = (d * lax.rsqrt(var + RWKV_GN_EPS) * lnw_ref[...] + lnb_ref[...] + maps[7]) * maps[6]
    for bb in range(nb):
        y_ref[bb] = ya[bb * L:(bb + 1) * L]

    @pl.when(c == nc - 1)
    def _():
        for i in pairs:
            sf_ref[bbs[i], i % n_pairs] = _xdot_l(s_scr[i], fmat, _TN)


def rwkv_params(w, i):
    z64 = jnp.zeros((RWKV_HD, RW), F32)
    return {
        "mu": w["e_mu"][i].reshape(1, RSHIFT),
        "w0": w["e_w0"][i].reshape(1, RW),
        "wup": jnp.concatenate([w["e_w_up"][i], z64], 0).astype(BF16),
        "a0": w["e_a0"][i].reshape(1, RW),
        "aup": jnp.concatenate([z64, w["e_a_up"][i]], 0).astype(BF16),
        "gup": w["e_g_up"][i].astype(BF16),
        "k_k": w["e_k_k"][i].reshape(1, RW),
        "k_a": w["e_k_a"][i].reshape(1, RW),
        "r_k": w["e_r_k"][i].reshape(1, RW),
    }


def rwkv_mix(p_flat, prev, s0, bsz, t, L, prm, lnx_w, lnx_b, nb=1):
    n_pairs = RW // LANES
    tp = -(-t // L) * L
    p3 = p_flat.reshape(bsz, t, RSHIFT)
    if tp != t:
        p3 = jnp.pad(p3, ((0, 0), (0, tp - t), (0, 0)))
    row = lambda n: pl.BlockSpec((1, n), lambda b, c: (0, 0))
    lora = pl.BlockSpec((LANES, RW), lambda b, c: (0, 0))
    st_spec = pl.BlockSpec((nb, n_pairs, LANES, RWKV_HD), lambda b, c: (b, 0, 0, 0))
    y, s_fin = pl.pallas_call(
        functools.partial(_rwkv_kernel, L, nb, n_pairs, t),
        grid=(bsz // nb, tp // L),
        in_specs=[pl.BlockSpec((nb, L, RSHIFT), lambda b, c: (b, c, 0)),
                  pl.BlockSpec((nb, 1, RSHIFT), lambda b, c: (b, 0, 0)),
                  st_spec,
                  row(RSHIFT), row(RW), lora, row(RW), lora, lora, row(RW), row(RW), row(RW), row(RW), row(RW)],
        out_specs=[pl.BlockSpec((nb, L, RW), lambda b, c: (b, c, 0)), st_spec],
        out_shape=[jax.ShapeDtypeStruct((bsz, tp, RW), F32),
                   jax.ShapeDtypeStruct((bsz, n_pairs, LANES, RWKV_HD), F32)],
        scratch_shapes=[pltpu.VMEM((nb * n_pairs, LANES, LANES), F32),
                        pltpu.VMEM((nb, SUBLANES, RSHIFT), F32)],
        compiler_params=_cparams(("parallel", "arbitrary")), name="rwkv",
    )(p3, prev.reshape(bsz, 1, RSHIFT), s0.reshape(bsz, n_pairs, LANES, RWKV_HD),
      prm["mu"], prm["w0"], prm["wup"], prm["a0"], prm["aup"], prm["gup"], prm["k_k"], prm["k_a"], prm["r_k"],
      lnx_w.reshape(1, RW), lnx_b.reshape(1, RW))
    return y[:, :t].reshape(bsz * t, RW), s_fin.reshape(bsz, RW // RWKV_HD, RWKV_HD, RWKV_HD)


N_Q_HEADS = 8
N_KV_HEADS = 2
Q_PER_KV = N_Q_HEADS // N_KV_HEADS
QW = N_Q_HEADS * ATT_HD
KW = N_KV_HEADS * ATT_HD
NEG = -1e30


def _t5_bucket_np(dist):
    n = np.maximum(dist, 0)
    max_exact = N_BUCKETS // 2
    nf = np.maximum(n, 1).astype(np.float32)
    large = max_exact + (np.log(nf / np.float32(max_exact)) / np.float32(math.log(BUCKET_MAX_DIST / max_exact))
                         * np.float32(N_BUCKETS - max_exact)).astype(np.int32)
    large = np.minimum(large, N_BUCKETS - 1)
    return np.where(n < max_exact, n, large)


def _bias_kernel(rt_ref, oh_ref, o_ref):
    o_ref[...] = _xdot_l(rt_ref[...], oh_ref[...])


def rel_bias(rel_table, dist):
    bucket = _t5_bucket_np(dist).reshape(-1)
    n = bucket.shape[0]
    onehot = jnp.asarray((np.arange(N_BUCKETS)[:, None] == bucket[None, :]).astype(np.float32), BF16)
    out = pl.pallas_call(
        _bias_kernel,
        out_shape=jax.ShapeDtypeStruct((N_Q_HEADS, n), F32),
    )(rel_table.T, onehot)
    return out.reshape((N_Q_HEADS,) + dist.shape)


def _head_norm(x, w_row):
    return x * lax.rsqrt(_seg_sum(x * x, ATT_HD) * (1.0 / ATT_HD) + EPS) * w_row


def _swa_prompt_kernel(nqb, q_ref, kvc_ref, kvp_ref, qw_ref, kw_ref, bias_ref, sink_ref, o_ref, ko_ref, vo_ref):
    i = pl.program_id(1)
    qn = _head_norm(q_ref[0], qw_ref[...])
    kvc = kvc_ref[0]
    kvp = kvp_ref[0]
    kn = [_head_norm(kvp[:, 0:KW], kw_ref[...])]
    vs = [kvp[:, KW:2 * KW]]
    kcn = _head_norm(kvc[:, 0:KW], kw_ref[...])
    for j in range(nqb):
        kn.append(kcn[j * WINDOW:(j + 1) * WINDOW])
        vs.append(kvc[j * WINDOW:(j + 1) * WINDOW, KW:2 * KW])
    kcat = [jnp.concatenate([kn[j], kn[j + 1]], axis=0).astype(BF16) for j in range(nqb)]
    vcat = [jnp.concatenate([vs[j], vs[j + 1]], axis=0).astype(BF16) for j in range(nqb)]
    qi = _iota((WINDOW, 2 * WINDOW), 0)
    kj = _iota((WINDOW, 2 * WINDOW), 1)
    dist = qi + WINDOW - kj
    band = jnp.logical_and(dist >= 0, dist < WINDOW)
    valid = [jnp.logical_and(band, jnp.logical_or(kj >= WINDOW, i > 0))] + [band] * (nqb - 1)
    lane = _iota((1, LANES), 1)
    masks = (jnp.where(lane < ATT_HD, 1.0, 0.0), jnp.where(lane < ATT_HD, 0.0, 1.0))
    scale = ATT_HD ** -0.5
    heads = range(N_Q_HEADS)
    units = [(j, h) for j in range(nqb) for h in heads]
    kv_of = lambda h: h // Q_PER_KV
    qts = [qn[j * WINDOW:(j + 1) * WINDOW, (h // 2) * LANES:(h // 2 + 1) * LANES] for j, h in units]
    qts = [pltpu.roll(qts[u], ATT_HD, 1) if h % 2 != kv_of(h) else qts[u] for u, (j, h) in enumerate(units)]
    qms = [(qts[u] * masks[kv_of(h)]).astype(BF16) for u, (j, h) in enumerate(units)]
    s_all = [_dg(jnp.concatenate(qms[j * N_Q_HEADS:(j + 1) * N_Q_HEADS], axis=0), kcat[j], _NT) * scale
             for j in range(nqb)]
    logits = [jnp.where(valid[j], s_all[j][h * WINDOW:(h + 1) * WINDOW] + bias_ref[h], NEG) for j, h in units]
    sinks = [sink_ref[h:h + 1, 0:1] for j, h in units]
    mx = [jnp.maximum(jnp.max(logits[u], axis=-1, keepdims=True), sinks[u]) for u in range(len(units))]
    pr = [jnp.exp(logits[u] - mx[u]) for u in range(len(units))]
    den = [jnp.sum(pr[u], axis=-1, keepdims=True) + jnp.exp(sinks[u] - mx[u]) for u in range(len(units))]
    probs = [(pr[u] * (1.0 / den[u])).astype(BF16) for u in range(len(units))]
    o_all = [_dg(jnp.concatenate(probs[j * N_Q_HEADS:(j + 1) * N_Q_HEADS], axis=0), vcat[j], _NN)
             for j in range(nqb)]
    os_ = [o_all[j][h * WINDOW:(h + 1) * WINDOW] for j, h in units]
    os_ = [pltpu.roll(os_[u], ATT_HD, 1) if h % 2 != kv_of(h) else os_[u] for u, (j, h) in enumerate(units)]
    for j in range(nqb):
        for jq in range(QW // LANES):
            o_ref[0, j * WINDOW:(j + 1) * WINDOW, jq * LANES:(jq + 1) * LANES] = (
                os_[j * N_Q_HEADS + 2 * jq] * masks[0] + os_[j * N_Q_HEADS + 2 * jq + 1] * masks[1])
    ko_ref[0] = kn[nqb]
    vo_ref[0] = vs[nqb]


SWA_QB = 2


def swa_prompt(q, kv, bsz, t, q_norm, k_norm, bias, sinks):
    nqb = math.gcd(t // WINDOW, SWA_QB)
    rows = nqb * WINDOW
    nb = t // rows
    q3 = q.reshape(bsz, t, QW)
    kv3 = kv.reshape(bsz, t, 2 * KW)
    o, ko, vo = pl.pallas_call(
        functools.partial(_swa_prompt_kernel, nqb),
        grid=(bsz, nb),
        in_specs=[pl.BlockSpec((1, rows, QW), lambda b, i: (b, i, 0)),
                  pl.BlockSpec((1, rows, 2 * KW), lambda b, i: (b, i, 0)),
                  pl.BlockSpec((1, WINDOW, 2 * KW), lambda b, i: (b, jnp.maximum(i * nqb - 1, 0), 0)),
                  pl.BlockSpec((1, QW), lambda b, i: (0, 0)),
                  pl.BlockSpec((1, KW), lambda b, i: (0, 0)),
                  pl.BlockSpec((N_Q_HEADS, WINDOW, 2 * WINDOW), lambda b, i: (0, 0, 0)),
                  pl.BlockSpec((N_Q_HEADS, LANES), lambda b, i: (0, 0))],
        out_specs=[pl.BlockSpec((1, rows, QW), lambda b, i: (b, i, 0)),
                   pl.BlockSpec((1, WINDOW, KW), lambda b, i: (b, 0, 0)),
                   pl.BlockSpec((1, WINDOW, KW), lambda b, i: (b, 0, 0))],
        out_shape=[jax.ShapeDtypeStruct((bsz, t, QW), F32),
                   jax.ShapeDtypeStruct((bsz, WINDOW, KW), F32),
                   jax.ShapeDtypeStruct((bsz, WINDOW, KW), F32)],
        compiler_params=_cparams(("parallel", "arbitrary")), name="swa_prompt",
    )(q3, kv3, kv3, jnp.tile(q_norm, N_Q_HEADS).reshape(1, QW), jnp.tile(k_norm, N_KV_HEADS).reshape(1, KW),
      bias, jnp.broadcast_to(sinks[:, None], (N_Q_HEADS, LANES)))
    return o.reshape(bsz * t, QW), ko, vo


DEC_TP = 8


def _swa_decode_kernel(nbt, t_real, q_ref, kv_ref, ck_ref, cv_ref, qw_ref, kw_ref, bc_ref, bn_ref, sink_ref,
                       o_ref, ko_ref, vo_ref):
    rows = N_Q_HEADS * DEC_TP
    tq = _iota((rows, WINDOW), 0) % DEC_TP
    valid_c = _iota((rows, WINDOW), 1) > tq
    jn = _iota((rows, DEC_TP), 1)
    valid_n = jnp.logical_and(jn <= _iota((rows, DEC_TP), 0) % DEC_TP, jn < t_real)
    lane = _iota((1, LANES), 1)
    masks = (jnp.where(lane < ATT_HD, 1.0, 0.0), jnp.where(lane < ATT_HD, 0.0, 1.0))
    row8 = _iota((DEC_TP, 1), 0)
    scale = ATT_HD ** -0.5
    bias_c = jnp.concatenate([bc_ref[kv] for kv in range(N_KV_HEADS)], axis=0)
    bias_n = jnp.concatenate([bn_ref[kv][:, 0:DEC_TP] for kv in range(N_KV_HEADS)], axis=0)
    sink = jnp.concatenate([sink_ref[kv][:, 0:1] for kv in range(N_KV_HEADS)], axis=0)
    bs = range(nbt)
    heads = range(N_Q_HEADS)
    qn_all = _head_norm(q_ref[...].reshape(nbt * DEC_TP, QW), qw_ref[...])
    kvn_all = kv_ref[...].reshape(nbt * DEC_TP, 2 * KW)
    knew_all = _head_norm(kvn_all[:, 0:KW], kw_ref[...])
    knew = [knew_all[b * DEC_TP:(b + 1) * DEC_TP] for b in bs]
    vnew = [kvn_all[b * DEC_TP:(b + 1) * DEC_TP, KW:2 * KW] for b in bs]
    kc = [ck_ref[b] for b in bs]
    vc = [cv_ref[b] for b in bs]

    def stack_q(b):
        pieces = []
        for h in heads:
            qt = qn_all[b * DEC_TP:(b + 1) * DEC_TP, (h // 2) * LANES:(h // 2 + 1) * LANES]
            if h % 2 != h // Q_PER_KV:
                qt = pltpu.roll(qt, ATT_HD, 1)
            pieces.append(qt * masks[h // Q_PER_KV])
        return jnp.concatenate(pieces, axis=0)

    qs = [stack_q(b) for b in bs]
    l_c = [jnp.where(valid_c, _bdot(qs[b], kc[b], _NT) * scale + bias_c, NEG) for b in bs]
    l_n = [jnp.where(valid_n, _bdot(qs[b], knew[b], _NT) * scale + bias_n, NEG) for b in bs]
    mx = [jnp.maximum(jnp.maximum(jnp.max(l_c[b], axis=-1, keepdims=True),
                                  jnp.max(l_n[b], axis=-1, keepdims=True)), sink) for b in bs]
    p_c = [jnp.exp(l_c[b] - mx[b]) for b in bs]
    p_n = [jnp.exp(l_n[b] - mx[b]) for b in bs]
    inv = [1.0 / (jnp.sum(p_c[b], axis=-1, keepdims=True) + jnp.sum(p_n[b], axis=-1, keepdims=True)
                  + jnp.exp(sink - mx[b])) for b in bs]
    o = [_bdot(p_c[b] * inv[b], vc[b]) + _bdot(p_n[b] * inv[b], vnew[b]) for b in bs]
    for b in bs:
        for jq in range(QW // LANES):
            parts = []
            for h in (2 * jq, 2 * jq + 1):
                piece = o[b][h * DEC_TP:(h + 1) * DEC_TP]
                if h % 2 != h // Q_PER_KV:
                    piece = pltpu.roll(piece, ATT_HD, 1)
                parts.append(piece * masks[h % 2])
            o_ref[b, :, jq * LANES:(jq + 1) * LANES] = parts[0] + parts[1]
    for b in bs:
        for cache, new, out in ((kc[b], knew[b], ko_ref), (vc[b], vnew[b], vo_ref)):
            shifted = pltpu.roll(cache, WINDOW - t_real, 0)
            new_r = pltpu.roll(new, DEC_TP - t_real, 0)
            out[b, 0:WINDOW - DEC_TP] = shifted[0:WINDOW - DEC_TP]
            out[b, WINDOW - DEC_TP:WINDOW] = jnp.where(row8 >= DEC_TP - t_real, new_r,
                                                       shifted[WINDOW - DEC_TP:WINDOW])


def swa_decode(q, kv, cache_k, cache_v, bsz, t, q_norm, k_norm, rel_table, sinks, nbt):
    pad = ((0, 0), (0, DEC_TP - t), (0, 0))
    q3 = jnp.pad(q.reshape(bsz, t, QW), pad)
    kv3 = jnp.pad(kv.reshape(bsz, t, 2 * KW), pad)
    kpos = np.concatenate([np.arange(WINDOW) - WINDOW, np.arange(DEC_TP)])
    dist = np.arange(DEC_TP)[:, None] - kpos[None, :]
    bias = rel_bias(rel_table, dist)
    rows = Q_PER_KV * DEC_TP
    bias = bias.reshape(N_KV_HEADS, rows, WINDOW + DEC_TP)
    bias_c = bias[:, :, :WINDOW]
    bias_n = jnp.pad(bias[:, :, WINDOW:], ((0, 0), (0, 0), (0, LANES - DEC_TP)))
    sink_rows = jnp.broadcast_to(sinks.reshape(N_KV_HEADS, Q_PER_KV, 1, 1),
                                 (N_KV_HEADS, Q_PER_KV, DEC_TP, LANES)).reshape(N_KV_HEADS, rows, LANES)
    full3 = lambda shape: pl.BlockSpec(shape, lambda i: (0, 0, 0))
    o, ko, vo = pl.pallas_call(
        functools.partial(_swa_decode_kernel, nbt, t),
        grid=(bsz // nbt,),
        in_specs=[pl.BlockSpec((nbt, DEC_TP, QW), lambda i: (i, 0, 0)),
                  pl.BlockSpec((nbt, DEC_TP, 2 * KW), lambda i: (i, 0, 0)),
                  pl.BlockSpec((nbt, WINDOW, KW), lambda i: (i, 0, 0)),
                  pl.BlockSpec((nbt, WINDOW, KW), lambda i: (i, 0, 0)),
                  pl.BlockSpec((1, QW), lambda i: (0, 0)),
                  pl.BlockSpec((1, KW), lambda i: (0, 0)),
                  full3((N_KV_HEADS, rows, WINDOW)),
                  full3((N_KV_HEADS, rows, LANES)),
                  full3((N_KV_HEADS, rows, LANES))],
        out_specs=[pl.BlockSpec((nbt, DEC_TP, QW), lambda i: (i, 0, 0)),
                   pl.BlockSpec((nbt, WINDOW, KW), lambda i: (i, 0, 0)),
                   pl.BlockSpec((nbt, WINDOW, KW), lambda i: (i, 0, 0))],
        out_shape=[jax.ShapeDtypeStruct((bsz, DEC_TP, QW), F32),
                   jax.ShapeDtypeStruct((bsz, WINDOW, KW), F32),
                   jax.ShapeDtypeStruct((bsz, WINDOW, KW), F32)],
        compiler_params=_cparams(("parallel",)), name="swa_decode",
    )(q3, kv3, cache_k, cache_v, jnp.tile(q_norm, N_Q_HEADS).reshape(1, QW),
      jnp.tile(k_norm, N_KV_HEADS).reshape(1, KW), bias_c, bias_n, sink_rows)
    return o[:, :t].reshape(bsz * t, QW), ko, vo


S5_G = 32
S5_W = S5_G * S5_P
S5_CP = S5_CHUNK * S5_P
S5_PK = 2 * S5_N
S5_HW = S5_G * S5_PK


S5_QT = S5_W // LANES
S5_GT = LANES // S5_P
S5_XW = S5_CHUNK * LANES
S5_HQ = S5_GT * S5_PK


def _s5_group_maps(t_effs, a2, ldt, b1, b2, c1, c2):
    L = S5_CHUNK
    ar2 = a2[0:1, :]
    ai2 = a2[1:2, :]
    step = jnp.exp(ldt)
    mi = _iota((3 * SUBLANES, S5_PK), 0).astype(F32)
    mag = jnp.exp(mi * (step * ar2))
    ang = mi * (step * ai2)
    pwa = mag * jnp.cos(ang)
    pwb = mag * jnp.sin(ang)
    abr = pwa[1:2]
    abi = pwb[1:2]
    den = ar2 * ar2 + ai2 * ai2
    fa = ((abr - 1.0) * ar2 + abi * ai2) / den
    fb = (abi * ar2 - (abr - 1.0) * ai2) / den
    bp1 = b1 * fa + b2 * fb
    bp2 = b2 * fa - b1 * fb
    cpow = [c1 * pwa[m:m + 1] + c2 * pwb[m:m + 1] for m in range(L + 1)]
    kern_t = _hdot(bp1, jnp.concatenate(cpow[0:L], axis=0), _NT)
    sgn = jnp.where(_iota((1, S5_PK), 1) < S5_N, -1.0, 1.0)
    kbs, als = [], []
    for t_eff in t_effs:
        kbs.append([bp1 * pwa[max(t_eff - 1 - i, 0):max(t_eff - 1 - i, 0) + 1]
                    + bp2 * pwb[max(t_eff - 1 - i, 0):max(t_eff - 1 - i, 0) + 1] for i in range(L)])
        als.append(jnp.concatenate([pwa[t_eff:t_eff + 1], sgn * pwb[t_eff:t_eff + 1]], axis=0))
    return kern_t, kbs, als, cpow[1:L + 1]


def _s5_prep_kernel(t_effs, a_ref, ldt_ref, b1_ref, b2_ref, c1_ref, c2_ref, bd_ref, kc_ref, *rest):
    L = S5_CHUNK
    n_t = len(t_effs)
    kb_refs = rest[:n_t]
    al_refs = rest[n_t:2 * n_t]
    lane = _iota((S5_P, LANES), 1)
    for kb_ref in kb_refs:
        kb_ref[0] = jnp.zeros(kb_ref.shape[1:], kb_ref.dtype)
    bd_rows = [[] for _ in range(L)]
    kct_rows = [[] for _ in range(L)]
    for g in range(S5_GT):
        kern_t, kbs, als, kct = _s5_group_maps(t_effs, a_ref[g], ldt_ref[g], b1_ref[g], b2_ref[g],
                                               c1_ref[g], c2_ref[g])
        in_group = jnp.logical_and(lane >= g * S5_P, lane < (g + 1) * S5_P)
        for tau in range(L):
            shift = (g * S5_P - tau * S5_P) % S5_CP
            moved = pltpu.roll(kern_t, shift, 1) if shift else kern_t
            bd_rows[tau].append(jnp.where(in_group, moved[:, 0:LANES], 0.0))
        for k in range(n_t):
            for i in range(L):
                kb_refs[k][0, i * LANES + g * S5_P:i * LANES + (g + 1) * S5_P, g * S5_PK:(g + 1) * S5_PK] = (
                    kbs[k][i].astype(kb_refs[k].dtype))
            al_refs[k][g] = als[k]
        zl = jnp.zeros((S5_P, g * S5_PK), F32)
        zr = jnp.zeros((S5_P, (S5_GT - 1 - g) * S5_PK), F32)
        for t in range(L):
            parts = ([zl] if g else []) + [kct[t]] + ([zr] if g < S5_GT - 1 else [])
            kct_rows[t].append(jnp.concatenate(parts, axis=1))
    for tau in range(L):
        bd_ref[0, tau] = jnp.concatenate(bd_rows[tau], axis=0).astype(bd_ref.dtype)
    for t in range(L):
        blk_t = jnp.concatenate(kct_rows[t], axis=0)
        kc_ref[0, :, t * LANES:(t + 1) * LANES] = blk_t.T.astype(kc_ref.dtype)


def s5_prep(w, i, t_effs):
    dup = lambda z: jnp.concatenate([z, z], axis=-1)
    a = jnp.stack([dup(w["o_a_re"][i]), dup(w["o_a_im"][i])], axis=1)
    ldt = jnp.broadcast_to(w["o_log_dt"][i][:, None, None], (S5_G, 1, S5_PK))
    bt_re = jnp.swapaxes(w["o_b_re"][i], 1, 2)
    bt_im = jnp.swapaxes(w["o_b_im"][i], 1, 2)
    b1 = jnp.concatenate([bt_re, bt_im], -1)
    b2 = jnp.concatenate([-bt_im, bt_re], -1)
    c_re, c_im = w["o_c_re"][i], w["o_c_im"][i]
    c1 = jnp.concatenate([c_re, -c_im], -1)
    c2 = jnp.concatenate([-c_im, -c_re], -1)
    n_t = len(t_effs)
    g3 = lambda r, c: pl.BlockSpec((S5_GT, r, c), lambda q: (q, 0, 0))
    outs = pl.pallas_call(
        functools.partial(_s5_prep_kernel, tuple(t_effs)),
        grid=(S5_QT,),
        in_specs=[g3(2, S5_PK), g3(1, S5_PK), g3(S5_P, S5_PK), g3(S5_P, S5_PK), g3(S5_P, S5_PK), g3(S5_P, S5_PK)],
        out_specs=[pl.BlockSpec((1, S5_CHUNK, LANES, LANES), lambda q: (q, 0, 0, 0)),
                   pl.BlockSpec((1, S5_HQ, S5_XW), lambda q: (q, 0, 0))]
                  + [pl.BlockSpec((1, S5_XW, S5_HQ), lambda q: (q, 0, 0))] * n_t
                  + [g3(2, S5_PK)] * n_t,
        out_shape=[jax.ShapeDtypeStruct((S5_QT, S5_CHUNK, LANES, LANES), BF16),
                   jax.ShapeDtypeStruct((S5_QT, S5_HQ, S5_XW), BF16)]
                  + [jax.ShapeDtypeStruct((S5_QT, S5_XW, S5_HQ), BF16)] * n_t
                  + [jax.ShapeDtypeStruct((S5_G, 2, S5_PK), F32)] * n_t,
        compiler_params=_cparams(("parallel",)), name="s5_prep",
    )(a, ldt, b1, b2, c1, c2)
    bd, kc = outs[0], outs[1]
    mats = []
    for k in range(n_t):
        al = outs[2 + n_t + k]
        mats.append((bd, outs[2 + k], kc, al[:, 0, :].reshape(1, S5_HW), al[:, 1, :].reshape(1, S5_HW)))
    return mats


def _s5_e_kernel(u_ref, kb_ref, e_ref):
    e_ref[...] = _bdot(u_ref[0], kb_ref[0])


def _s5_swap(h):
    n = h.shape[-1]
    lane = _iota(h.shape, 1)
    return jnp.where(lane % S5_PK < S5_N, pltpu.roll(h, n - S5_N, 1), pltpu.roll(h, S5_N, 1))


def _s5_scan_kernel(bsz, cg, e_ref, h0_ref, ala_ref, alb_ref, hp_ref, hf_ref, h_scr):
    @pl.when(pl.program_id(0) == 0)
    def _():
        h_scr[...] = h0_ref[...]

    ala = ala_ref[...]
    alb = alb_ref[...]

    def body(c, hs):
        out = []
        for b in range(bsz):
            hp_ref[b, pl.ds(c, 1), :] = hs[b]
            out.append(ala * hs[b] + alb * _s5_swap(hs[b]) + e_ref[b, pl.ds(c, 1), :])
        return tuple(out)

    hs = lax.fori_loop(0, cg, body, tuple(h_scr[b:b + 1, :] for b in range(bsz)))
    for b in range(bsz):
        h_scr[b:b + 1, :] = hs[b]
        hf_ref[b:b + 1, :] = hs[b]


def _s5_step_kernel(e_ref, h0_ref, ala_ref, alb_ref, hf_ref):
    h = h0_ref[...]
    hf_ref[...] = ala_ref[...] * h + alb_ref[...] * _s5_swap(h) + e_ref[...]


def _s5_y_kernel(u_ref, hp_ref, bd_ref, kc_ref, y_ref, k_scr):
    @pl.when(pl.program_id(1) == 0)
    def _():
        zero = jnp.zeros((LANES, LANES), k_scr.dtype)
        for i in range(S5_CHUNK):
            for t in range(S5_CHUNK):
                k_scr[i * LANES:(i + 1) * LANES, t * LANES:(t + 1) * LANES] = bd_ref[0, t - i] if t >= i else zero

    y_ref[0] = _bdot(u_ref[0], k_scr[...]) + _bdot(hp_ref[...], kc_ref[0])


def s5_core(x, h0, bsz, mats):
    bd, kbbig, kcbig, ala, alb = mats
    rows = x.shape[1]
    nc = rows // bsz
    tr = math.gcd(rows, 512)
    e = pl.pallas_call(
        _s5_e_kernel,
        grid=(S5_QT, rows // tr),
        in_specs=[pl.BlockSpec((1, tr, S5_XW), lambda q, r: (q, r, 0)),
                  pl.BlockSpec((1, S5_XW, S5_HQ), lambda q, r: (q, 0, 0))],
        out_specs=pl.BlockSpec((tr, S5_HQ), lambda q, r: (r, q)),
        out_shape=jax.ShapeDtypeStruct((rows, S5_HW), F32),
        compiler_params=_cparams(("parallel", "parallel")), name="s5_e",
    )(x, kbbig)
    row = pl.BlockSpec((1, S5_HW), lambda i: (0, 0))
    if nc == 1:
        hp = h0
        hf = pl.pallas_call(
            _s5_step_kernel,
            out_shape=jax.ShapeDtypeStruct((bsz, S5_HW), F32), name="s5_step",
        )(e, h0, ala, alb)
    else:
        cg = math.gcd(nc, 64)
        hp, hf = pl.pallas_call(
            functools.partial(_s5_scan_kernel, bsz, cg),
            grid=(nc // cg,),
            in_specs=[pl.BlockSpec((bsz, cg, S5_HW), lambda i: (0, i, 0)),
                      pl.BlockSpec((bsz, S5_HW), lambda i: (0, 0)), row, row],
            out_specs=[pl.BlockSpec((bsz, cg, S5_HW), lambda i: (0, i, 0)),
                       pl.BlockSpec((bsz, S5_HW), lambda i: (0, 0))],
            out_shape=[jax.ShapeDtypeStruct((bsz, nc, S5_HW), F32),
                       jax.ShapeDtypeStruct((bsz, S5_HW), F32)],
            scratch_shapes=[pltpu.VMEM((bsz, S5_HW), F32)],
            compiler_params=_cparams(("arbitrary",)), name="s5_scan",
        )(e.reshape(bsz, nc, S5_HW), h0, ala, alb)
    y = pl.pallas_call(
        _s5_y_kernel,
        grid=(S5_QT, rows // tr),
        in_specs=[pl.BlockSpec((1, tr, S5_XW), lambda q, r: (q, r, 0)),
                  pl.BlockSpec((tr, S5_HQ), lambda q, r: (r, q)),
                  pl.BlockSpec((1, S5_CHUNK, LANES, LANES), lambda q, r: (q, 0, 0, 0)),
                  pl.BlockSpec((1, S5_HQ, S5_XW), lambda q, r: (q, 0, 0))],
        out_specs=pl.BlockSpec((1, tr, S5_XW), lambda q, r: (q, r, 0)),
        out_shape=jax.ShapeDtypeStruct((S5_QT, rows, S5_XW), F32),
        scratch_shapes=[pltpu.VMEM((S5_XW, S5_XW), BF16)],
        compiler_params=_cparams(("parallel", "arbitrary")), name="s5_y",
    )(x, hp.reshape(rows, S5_HW), bd, kcbig)
    return y, hf


def _s5_post_kernel(tpr, y_ref, u_ref, d_ref, gw_ref, gb_ref, o_ref, y_scr, u_scr):
    _from_chunk_rows(y_ref, y_scr, tpr)
    _from_chunk_rows(u_ref, u_scr, tpr)
    y = jnp.concatenate([y_scr[q] for q in range(S5_QT)], axis=1)
    u = jnp.concatenate([u_scr[q] for q in range(S5_QT)], axis=1)
    x = y + d_ref[...] * u
    z = 0.5 * x * (1.0 + jnp.tanh(math.sqrt(2.0 / math.pi) * (x + 0.044715 * (x * x * x))))
    o_ref[...] = z * _sigmoid(_dg(z.astype(BF16), gw_ref[...], _NN) + gb_ref[...])


def s5_post(y4, u4, m, tpr, d, glu_w_bf16, glu_b, tm):
    spec4 = pl.BlockSpec((S5_QT, tm // tpr, S5_XW), lambda i: (0, i, 0))
    row = pl.BlockSpec((1, S5_W), lambda i: (0, 0))
    return pl.pallas_call(
        functools.partial(_s5_post_kernel, tpr),
        grid=(m // tm,),
        in_specs=[spec4, spec4, row, pl.BlockSpec((S5_W, S5_W), lambda i: (0, 0)), row],
        out_specs=pl.BlockSpec((tm, S5_W), lambda i: (i, 0)),
        out_shape=jax.ShapeDtypeStruct((m, S5_W), F32),
        scratch_shapes=[pltpu.VMEM((S5_QT, tm, LANES), F32), pltpu.VMEM((S5_QT, tm, LANES), F32)],
        compiler_params=_cparams(("parallel",)), name="s5_post",
    )(y4, u4, d.reshape(1, S5_W), glu_w_bf16, glu_b.reshape(1, S5_W))


def s5_mix(u4, h_re, h_im, bsz, t, mats, d, glu_w_bf16, glu_b, tm):
    tpr = min(t, CHUNK_ROW_TOKENS)
    h0 = jnp.concatenate([h_re, h_im], axis=-1).reshape(bsz, S5_HW)
    y4, hf = s5_core(u4, h0, bsz, mats)
    out = s5_post(y4, u4, bsz * t, tpr, d, glu_w_bf16, glu_b, tm)
    hf = hf.reshape(bsz, S5_G, 2, S5_N)
    return out, hf[:, :, 0], hf[:, :, 1]


GLA_DK = 64
GLA_DV = 128
GLA_HEADS = 4
GLA_KW = GLA_HEADS * GLA_DK
GLA_VW = GLA_HEADS * GLA_DV
GLA_PW = 2 * GLA_KW + 2 * GLA_VW + LANES


def _gla_kernel(L, t_real, nb, p_ref, aup_ref, ab_ref, nw_ref, s0_ref, y_ref, sf_ref, s_scr):
    c = pl.program_id(1)
    nc = pl.num_programs(1)
    n_pairs = GLA_KW // LANES

    @pl.when(c == 0)
    def _():
        for bb in range(nb):
            for j in range(n_pairs):
                s_scr[bb * n_pairs + j] = s0_ref[bb, j]

    lane = _iota((1, LANES), 1)
    masks = (jnp.where(lane < GLA_DK, 1.0, 0.0), jnp.where(lane < GLA_DK, 0.0, 1.0))
    incl = _iota((L, L), 1) <= _iota((L, L), 0)
    diag128 = _iota((LANES, LANES), 0) == _iota((LANES, LANES), 1)
    ones128 = jnp.ones((LANES, LANES), BF16)
    tri = _tri_incl(L)
    rows = range(nb)
    ps = [p_ref[bb] for bb in rows]
    zs = [_bdot(p[:, 2 * GLA_KW + 2 * GLA_VW:], aup_ref[...]) + ab_ref[...] for p in ps]
    gk = [-_softplus(-z) * (1.0 / GLA_GATE_NORM) for z in zs]
    if t_real % L != 0:
        tok = c * L + _iota((L, 1), 0)
        gk = [jnp.where(tok < t_real, x, 0.0) for x in gk]
    b = [_xdot_r(tri, x) for x in gk]
    bl = [x[L - 1:L, :] for x in b]
    qd = [ps[i][:, 0:GLA_KW] * (GLA_DK ** -0.5) * jnp.exp(b[i]) for i in rows]
    kh = [ps[i][:, GLA_KW:2 * GLA_KW] * jnp.exp(-b[i]) for i in rows]
    kt = [ps[i][:, GLA_KW:2 * GLA_KW] * jnp.exp(bl[i] - b[i]) for i in rows]
    heads = [(bb, h) for bb in rows for h in range(GLA_HEADS)]
    sl = lambda h: slice((h // 2) * LANES, (h // 2 + 1) * LANES)
    hs = lambda h: slice(2 * GLA_KW + h * GLA_DV, 2 * GLA_KW + (h + 1) * GLA_DV)
    gs = lambda h: slice(2 * GLA_KW + GLA_VW + h * GLA_DV, 2 * GLA_KW + GLA_VW + (h + 1) * GLA_DV)
    st = [s_scr[bb * n_pairs + j] for bb in rows for j in range(n_pairs)]
    qm = [qd[bb][:, sl(h)] * masks[h % 2] for bb, h in heads]
    vh = [ps[bb][:, hs(h)] for bb, h in heads]
    attn = [jnp.where(incl, _bdot(qm[i], kh[bb][:, sl(h)], _NT), 0.0) for i, (bb, h) in enumerate(heads)]
    o = [_bdot(attn[i], vh[i]) + _bdot(qm[i], st[bb * n_pairs + h // 2]) for i, (bb, h) in enumerate(heads)]
    kv = [_bdot(kt[bb][:, sl(h)], vh[i], _TN) for i, (bb, h) in enumerate(heads)]
    for i, (bb, h) in enumerate(heads):
        of = o[i] * lax.rsqrt(jnp.mean(o[i] * o[i], axis=-1, keepdims=True) + EPS) * nw_ref[...]
        y_ref[bb, :, h * GLA_DV:(h + 1) * GLA_DV] = of * _silu(ps[bb][:, gs(h)])
    for bb in rows:
        for j in range(n_pairs):
            i0 = bb * GLA_HEADS + 2 * j
            pcol = _xdot_l(jnp.where(diag128, jnp.exp(bl[bb][:, j * LANES:(j + 1) * LANES]), 0.0), ones128)
            s_scr[bb * n_pairs + j] = pcol * st[bb * n_pairs + j] + jnp.concatenate(
                [kv[i0][0:GLA_DK], kv[i0 + 1][GLA_DK:2 * GLA_DK]], axis=0)

    @pl.when(c == nc - 1)
    def _():
        for bb in range(nb):
            for j in range(n_pairs):
                sf_ref[bb, j] = s_scr[bb * n_pairs + j]


def gla_mix(p_gla, s0, bsz, t, L, aup_pad, a_b, norm_w, nb=1):
    tp = -(-t // L) * L
    p3 = p_gla.reshape(bsz, t, GLA_PW)
    if tp != t:
        p3 = jnp.pad(p3, ((0, 0), (0, tp - t), (0, 0)))
    n_pairs = GLA_KW // LANES
    st_spec = pl.BlockSpec((nb, n_pairs, LANES, LANES), lambda b, c: (b, 0, 0, 0))
    y, s_fin = pl.pallas_call(
        functools.partial(_gla_kernel, L, t, nb),
        grid=(bsz // nb, tp // L),
        in_specs=[pl.BlockSpec((nb, L, GLA_PW), lambda b, c: (b, c, 0)),
                  pl.BlockSpec((LANES, GLA_KW), lambda b, c: (0, 0)),
                  pl.BlockSpec((1, GLA_KW), lambda b, c: (0, 0)),
                  pl.BlockSpec((1, GLA_DV), lambda b, c: (0, 0)),
                  st_spec],
        out_specs=[pl.BlockSpec((nb, L, GLA_VW), lambda b, c: (b, c, 0)), st_spec],
        out_shape=[jax.ShapeDtypeStruct((bsz, tp, GLA_VW), F32),
                   jax.ShapeDtypeStruct((bsz, n_pairs, LANES, LANES), F32)],
        scratch_shapes=[pltpu.VMEM((nb * n_pairs, LANES, LANES), F32)],
        compiler_params=_cparams(("parallel", "arbitrary")), name="gla",
    )(p3, aup_pad, a_b.reshape(1, GLA_KW), norm_w.reshape(1, GLA_DV),
      s0.reshape(bsz, n_pairs, LANES, LANES))
    return y[:, :t].reshape(bsz * t, GLA_VW), s_fin.reshape(bsz, GLA_HEADS, GLA_DK, GLA_DV)


GLA_LR = 16
D_FF_CHUNK = 1408
D_FF_EXPERT_CHUNK = 896
FFN_TOKENS = 512
ROUTER_TOKENS = 1024
MOE_TOKENS = 2048


def _prepare_weights(w):
    win = w["e_w_in"][0]
    w_gla = jnp.pad(win[:, RSHIFT:], ((0, 0), (0, LANES - GLA_LR)))
    wo = w["o_w_in"][0]
    return {
        "e_w_rwkv": win[:, :RSHIFT].astype(BF16),
        "e_w_gla": w_gla.astype(BF16),
        "rwkv": rwkv_params(w, 0),
        "gla_aup": jnp.pad(w["e_gla_a_up"][0], ((0, LANES - GLA_LR), (0, 0))).astype(BF16),
        "e_wo_a": w["e_w_out"][0][:RW].astype(BF16),
        "e_wo_b": w["e_w_out"][0][RW:].astype(BF16),
        "ff_w1": w["e_ff_w1"][0].astype(BF16),
        "ff_w3": w["e_ff_w3"][0].astype(BF16),
        "ff_w2": w["e_ff_w2"][0].astype(BF16),
        "o_w_q": wo[:, :QW].astype(BF16),
        "o_w_kv": wo[:, QW:QW + 2 * KW].astype(BF16),
        "o_w_u": wo[:, QW + 2 * KW:].astype(BF16),
        "glu_w": w["o_glu_w"][0].astype(BF16),
        "o_wo_a": w["o_w_out"][0][:QW].astype(BF16),
        "o_wo_b": w["o_w_out"][0][QW:].astype(BF16),
        "moe_w1": w["o_moe_w1"][0].astype(BF16),
        "moe_w3": w["o_moe_w3"][0].astype(BF16),
        "moe_w2": w["o_moe_w2"][0].astype(BF16),
    }


def _trunk(x3, st, w, pw, tm, chunk, nb, s5_prep_t, prompt_bias):
    bsz, t, d = x3.shape
    m = bsz * t
    x = x3.reshape(m, d)
    p_r, p_g = norm_proj(x, w["e_norm1"][0], [pw["e_w_rwkv"], pw["e_w_gla"]], tm)
    ya, s_rwkv = rwkv_mix(p_r, st["shift"], st["rwkv"], bsz, t, chunk, pw["rwkv"],
                          w["e_lnx_w"][0], w["e_lnx_b"][0], nb)
    s_shift = p_r.reshape(bsz, t, RSHIFT)[:, -1]
    yb, s_gla = gla_mix(p_g, st["gla"], bsz, t, chunk, pw["gla_aup"], w["e_gla_a_b"][0], w["e_gla_norm"][0], nb)
    x = ffn(x, ya, yb, pw["e_wo_a"], pw["e_wo_b"], w["e_norm2"][0], pw["ff_w1"], pw["ff_w3"], pw["ff_w2"],
            math.gcd(m, FFN_TOKENS), D_FF_CHUNK)
    q, kv, u = norm_proj_tiles(x, w["o_norm1"][0], [pw["o_w_q"], pw["o_w_kv"], pw["o_w_u"]], tm,
                               min(t, CHUNK_ROW_TOKENS))
    if st["win_k"] is None:
        yc, nk, nv = swa_prompt(q, kv, bsz, t, w["o_q_norm"][0], w["o_k_norm"][0], prompt_bias, w["o_sinks"][0])
    else:
        yc, nk, nv = swa_decode(q, kv, st["win_k"].reshape(bsz, WINDOW, KW), st["win_v"].reshape(bsz, WINDOW, KW),
                                bsz, t, w["o_q_norm"][0], w["o_k_norm"][0], w["rel_table"], w["o_sinks"][0], 8)
    yd, s5r, s5i = s5_mix(u, st["s5_re"], st["s5_im"], bsz, t, s5_prep_t, w["o_d"][0], pw["glu_w"],
                          w["o_glu_b"][0], tm)
    t_router = min(m, ROUTER_TOKENS)
    x, hn, gates, pos, counts = router(x, yc, yd, pw["o_wo_a"], pw["o_wo_b"], w["o_norm2"][0],
                                       w["o_router_w"][0], w["o_router_b"][0], t_router)
    x = moe(x, hn, gates, pos, counts, pw["moe_w1"], pw["moe_w3"], pw["moe_w2"], t_router, min(m, MOE_TOKENS),
            D_FF_EXPERT_CHUNK)
    kv_shape = (bsz, WINDOW, N_KV_HEADS, ATT_HD)
    return (x.reshape(bsz, t, d), s_rwkv[None], s_shift[None], s_gla[None], nk.reshape(kv_shape)[None],
            nv.reshape(kv_shape)[None], s5r[None], s5i[None])


def kernel(x_prompt, x_sample, state_rwkv, state_shift, state_gla, cache_win_k, cache_win_v, state_s5_re,
           state_s5_im, rel_table, e_norm1, e_w_in, e_mu, e_w0, e_w_up, e_a0, e_a_up, e_g_up, e_k_k, e_k_a, e_r_k,
           e_lnx_w, e_lnx_b, e_gla_a_up, e_gla_a_b, e_gla_norm, e_w_out, e_norm2, e_ff_w1, e_ff_w3, e_ff_w2,
           o_norm1, o_w_in, o_q_norm, o_k_norm, o_sinks, o_a_re, o_a_im, o_log_dt, o_b_re, o_b_im, o_c_re, o_c_im,
           o_d, o_glu_w, o_glu_b, o_w_out, o_norm2, o_router_w, o_router_b, o_moe_w1, o_moe_w3, o_moe_w2):
    w = dict(rel_table=rel_table, e_norm1=e_norm1, e_w_in=e_w_in, e_mu=e_mu, e_w0=e_w0, e_w_up=e_w_up, e_a0=e_a0,
             e_a_up=e_a_up, e_g_up=e_g_up, e_k_k=e_k_k, e_k_a=e_k_a, e_r_k=e_r_k, e_lnx_w=e_lnx_w, e_lnx_b=e_lnx_b,
             e_gla_a_up=e_gla_a_up, e_gla_a_b=e_gla_a_b, e_gla_norm=e_gla_norm, e_w_out=e_w_out, e_norm2=e_norm2,
             e_ff_w1=e_ff_w1, e_ff_w3=e_ff_w3, e_ff_w2=e_ff_w2, o_norm1=o_norm1, o_w_in=o_w_in, o_q_norm=o_q_norm,
             o_k_norm=o_k_norm, o_sinks=o_sinks, o_a_re=o_a_re, o_a_im=o_a_im, o_log_dt=o_log_dt, o_b_re=o_b_re,
             o_b_im=o_b_im, o_c_re=o_c_re, o_c_im=o_c_im, o_d=o_d, o_glu_w=o_glu_w, o_glu_b=o_glu_b,
             o_w_out=o_w_out, o_norm2=o_norm2, o_router_w=o_router_w, o_router_b=o_router_b, o_moe_w1=o_moe_w1,
             o_moe_w3=o_moe_w3, o_moe_w2=o_moe_w2)
    pw = _prepare_weights(w)
    bp, tp, _ = x_prompt.shape
    bs, ts, _ = x_sample.shape
    qi = np.arange(WINDOW)[:, None]
    kj = np.arange(2 * WINDOW)[None, :]
    prompt_bias = rel_bias(rel_table, qi + WINDOW - kj)
    zeros = lambda *shape: jnp.zeros(shape, F32)
    st_p = {"rwkv": zeros(bp, RW // RWKV_HD, RWKV_HD, RWKV_HD), "shift": zeros(bp, RSHIFT),
            "gla": zeros(bp, GLA_HEADS, GLA_DK, GLA_DV), "win_k": None, "win_v": None,
            "s5_re": zeros(bp, S5_G, S5_N), "s5_im": zeros(bp, S5_G, S5_N)}
    st_s = {"rwkv": state_rwkv[0], "shift": state_shift[0], "gla": state_gla[0], "win_k": cache_win_k[0],
            "win_v": cache_win_v[0], "s5_re": state_s5_re[0], "s5_im": state_s5_im[0]}
    s5_p, s5_s = s5_prep(w, 0, (S5_CHUNK, ts))
    out_p = _trunk(x_prompt, st_p, w, pw, 512, 64, math.gcd(bp, 2), s5_p, prompt_bias)
    out_s = _trunk(x_sample, st_s, w, pw, bs * ts, 8, math.gcd(bs, 16), s5_s, None)
    res = [out_p[0], out_s[0]]
    for a, b in zip(out_p[1:], out_s[1:]):
        res += [a, b]
    return tuple(res)
```

```python
import functools
import math

import jax
import jax.numpy as jnp
import numpy as np
from jax import lax
from jax.experimental import pallas as pl
from jax.experimental.pallas import tpu as pltpu

F32 = jnp.float32
BF16 = jnp.bfloat16

LANES = 128
SUBLANES = 8
VMEM_LIMIT_BYTES = 56 * 1024 * 1024

EPS = 1e-6
RWKV_HD = 64
RWKV_GN_EPS = 64e-5
GLA_GATE_NORM = 16.0
ATT_HD = 64
WINDOW = 128
N_BUCKETS = 32
BUCKET_MAX_DIST = 128
S5_P = 16
S5_N = 64
S5_CHUNK = 16
TOP_K = 2


def _cparams(sem):
    return pltpu.CompilerParams(dimension_semantics=sem, vmem_limit_bytes=VMEM_LIMIT_BYTES)


_NN = (((1,), (0,)), ((), ()))
_NT = (((1,), (1,)), ((), ()))
_TN = (((0,), (0,)), ((), ()))


def _dg(a, b, dims):
    return lax.dot_general(a, b, dims, preferred_element_type=F32)


def _bdot(a, b, dims=_NN):
    return _dg(a.astype(BF16), b.astype(BF16), dims)


def _split(a, n):
    terms = []
    r = a
    for _ in range(n):
        t = r.astype(BF16)
        terms.append(t)
        r = r - t.astype(F32)
    return terms


def _hdot(a, b, dims=_NN):
    a0, a1 = _split(a, 2)
    b0, b1 = _split(b, 2)
    return _dg(a0, b0, dims) + (_dg(a0, b1, dims) + _dg(a1, b0, dims))


def _xdot_l(a, e, dims=_NN):
    e = e.astype(BF16)
    a0, a1, a2 = _split(a, 3)
    return _dg(a0, e, dims) + (_dg(a1, e, dims) + _dg(a2, e, dims))


def _xdot_r(e, b, dims=_NN):
    e = e.astype(BF16)
    b0, b1, b2 = _split(b, 3)
    return _dg(e, b0, dims) + (_dg(e, b1, dims) + _dg(e, b2, dims))


def _iota(shape, axis):
    return lax.broadcasted_iota(jnp.int32, shape, axis)


def _seg_ones(n, seg):
    r = _iota((n, n), 0) // seg
    c = _iota((n, n), 1) // seg
    return jnp.where(r == c, 1.0, 0.0).astype(BF16)


def _seg_sum(x, seg):
    n = x.shape[-1]
    ones = _seg_ones(n, seg)
    x0, x1 = _split(x, 2)
    return _dg(x0, ones, _NN) + _dg(x1, ones, _NN)


def _sigmoid(x):
    return 1.0 / (1.0 + jnp.exp(-x))


def _silu(x):
    return x * _sigmoid(x)


def _softplus(x):
    return jnp.maximum(x, 0.0) + jnp.log(1.0 + jnp.exp(-jnp.abs(x)))


def _tri_incl(n):
    r = _iota((n, n), 0)
    c = _iota((n, n), 1)
    return jnp.where(c <= r, 1.0, 0.0).astype(BF16)


def _rms(x, g):
    return x * lax.rsqrt(jnp.mean(x * x, axis=-1, keepdims=True) + EPS) * g


def _norm_proj_kernel(n_w, x_ref, g_ref, *refs):
    xn = _rms(x_ref[...], g_ref[...]).astype(BF16)
    for w_ref, o_ref in zip(refs[:n_w], refs[n_w:]):
        o_ref[...] = _dg(xn, w_ref[...], _NN)


def norm_proj(x, g, ws_bf16, tm):
    m, d = x.shape
    return pl.pallas_call(
        functools.partial(_norm_proj_kernel, len(ws_bf16)),
        grid=(m // tm,),
        in_specs=[pl.BlockSpec((tm, d), lambda i: (i, 0)),
                  pl.BlockSpec((1, d), lambda i: (0, 0))]
                 + [pl.BlockSpec(w.shape, lambda i: (0, 0)) for w in ws_bf16],
        out_specs=[pl.BlockSpec((tm, w.shape[1]), lambda i: (i, 0)) for w in ws_bf16],
        out_shape=[jax.ShapeDtypeStruct((m, w.shape[1]), F32) for w in ws_bf16],
        compiler_params=_cparams(("parallel",)), name="norm_proj",
    )(x, g.reshape(1, d), *ws_bf16)


CHUNK_ROW_TOKENS = 16


def _to_chunk_rows(src_ref, dst_ref, tpr):
    rows = src_ref.shape[1] // tpr
    for q in range(dst_ref.shape[0]):
        for i in range(CHUNK_ROW_TOKENS):
            cols = slice(i * LANES, (i + 1) * LANES)
            if i < tpr:
                dst_ref[q, :, cols] = src_ref[q, pl.ds(i, rows, stride=tpr), :]
            else:
                dst_ref[q, :, cols] = jnp.zeros((rows, LANES), dst_ref.dtype)


def _from_chunk_rows(src_ref, dst_ref, tpr):
    rows = dst_ref.shape[1] // tpr
    for q in range(src_ref.shape[0]):
        for i in range(tpr):
            dst_ref[q, pl.ds(i, rows, stride=tpr), :] = src_ref[q, :, i * LANES:(i + 1) * LANES]


def _norm_proj_tiles_kernel(n_w, tpr, x_ref, g_ref, *refs):
    xn = _rms(x_ref[...], g_ref[...]).astype(BF16)
    for w_ref, o_ref in zip(refs[:n_w - 1], refs[n_w:]):
        o_ref[...] = _dg(xn, w_ref[...], _NN)
    u_ref, u_scr = refs[-2], refs[-1]
    u = _dg(xn, refs[n_w - 1][...], _NN)
    for q in range(u_scr.shape[0]):
        u_scr[q] = u[:, q * LANES:(q + 1) * LANES]
    _to_chunk_rows(u_scr, u_ref, tpr)


def norm_proj_tiles(x, g, ws_bf16, tm, tpr):
    m, d = x.shape
    nu = ws_bf16[-1].shape[1]
    nt = nu // LANES
    cw = CHUNK_ROW_TOKENS * LANES
    return pl.pallas_call(
        functools.partial(_norm_proj_tiles_kernel, len(ws_bf16), tpr),
        grid=(m // tm,),
        in_specs=[pl.BlockSpec((tm, d), lambda i: (i, 0)),
                  pl.BlockSpec((1, d), lambda i: (0, 0))]
                 + [pl.BlockSpec(w.shape, lambda i: (0, 0)) for w in ws_bf16],
        out_specs=[pl.BlockSpec((tm, w.shape[1]), lambda i: (i, 0)) for w in ws_bf16[:-1]]
                  + [pl.BlockSpec((nt, tm // tpr, cw), lambda i: (0, i, 0))],
        out_shape=[jax.ShapeDtypeStruct((m, w.shape[1]), F32) for w in ws_bf16[:-1]]
                  + [jax.ShapeDtypeStruct((nt, m // tpr, cw), F32)],
        scratch_shapes=[pltpu.VMEM((nt, tm, LANES), F32)],
        compiler_params=_cparams(("parallel",)), name="norm_proj_tiles",
    )(x, g.reshape(1, d), *ws_bf16)


def _mixer_out(x_ref, ya_ref, yb_ref, wa_ref, wb_ref):
    return (x_ref[...] + _dg(ya_ref[...].astype(BF16), wa_ref[...], _NN)
            + _dg(yb_ref[...].astype(BF16), wb_ref[...], _NN))


def _ffn_kernel(x_ref, ya_ref, yb_ref, wa_ref, wb_ref, g_ref, w1_ref, w3_ref, w2_ref, o_ref, xn_scr):
    j = pl.program_id(1)

    @pl.when(j == 0)
    def _():
        x = _mixer_out(x_ref, ya_ref, yb_ref, wa_ref, wb_ref)
        xn_scr[...] = _rms(x, g_ref[...]).astype(BF16)
        o_ref[...] = x

    xn = xn_scr[...]
    h = _silu(_dg(xn, w1_ref[...], _NN)) * _dg(xn, w3_ref[...], _NN)
    o_ref[...] += _dg(h.astype(BF16), w2_ref[...], _NN)


def ffn(x, ya, yb, wa, wb, g, w1, w3, w2, tm, fc):
    m, d = x.shape
    dff = w1.shape[1]
    tok = lambda n: pl.BlockSpec((tm, n), lambda i, j: (i, 0))
    full = lambda a: pl.BlockSpec(a.shape, lambda i, j: (0, 0))
    wmode = pl.Buffered(1) if dff == fc else None
    return pl.pallas_call(
        _ffn_kernel,
        grid=(m // tm, dff // fc),
        in_specs=[tok(d), tok(ya.shape[1]), tok(yb.shape[1]), full(wa), full(wb),
                  pl.BlockSpec((1, d), lambda i, j: (0, 0)),
                  pl.BlockSpec((d, fc), lambda i, j: (0, j), pipeline_mode=wmode),
                  pl.BlockSpec((d, fc), lambda i, j: (0, j), pipeline_mode=wmode),
                  pl.BlockSpec((fc, d), lambda i, j: (j, 0), pipeline_mode=wmode)],
        out_specs=pl.BlockSpec((tm, d), lambda i, j: (i, 0)),
        out_shape=jax.ShapeDtypeStruct((m, d), F32),
        scratch_shapes=[pltpu.VMEM((tm, d), BF16)],
        compiler_params=_cparams(("parallel", "arbitrary")), name="ffn",
    )(x, ya, yb, wa, wb, g.reshape(1, d), w1, w3, w2)


def _router_kernel(n_exp, x_ref, ya_ref, yb_ref, wa_ref, wb_ref, g_ref, rw_ref, rb_ref,
                   xo_ref, hn_ref, gate_ref, pos_ref, cnt_ref):
    x = _mixer_out(x_ref, ya_ref, yb_ref, wa_ref, wb_ref)
    xo_ref[...] = x
    xn = _rms(x, g_ref[...])
    hn_ref[...] = xn.astype(BF16)
    logits = _hdot(xn, rw_ref[...]) + rb_ref[...]
    lane = _iota(logits.shape, 1)
    logits = jnp.where(lane < n_exp, logits, -jnp.inf)
    m1 = jnp.max(logits, axis=-1, keepdims=True)
    i1 = jnp.min(jnp.where(logits == m1, lane, LANES), axis=-1, keepdims=True)
    rest = jnp.where(lane == i1, -jnp.inf, logits)
    m2 = jnp.max(rest, axis=-1, keepdims=True)
    i2 = jnp.min(jnp.where(rest == m2, lane, LANES), axis=-1, keepdims=True)
    e2 = jnp.exp(m2 - m1)
    g1 = 1.0 / (1.0 + e2)
    g2 = e2 / (1.0 + e2)
    pick1 = lane == i1
    pick2 = lane == i2
    gate_ref[...] = jnp.where(pick1, g1, 0.0) + jnp.where(pick2, g2, 0.0)
    tm = logits.shape[0]
    sel = jnp.where(jnp.logical_or(pick1, pick2), 1.0, 0.0).astype(BF16)
    tr = _iota((tm, tm), 0)
    tc = _iota((tm, tm), 1)
    upper = jnp.where(tr <= tc, 1.0, 0.0).astype(BF16)
    eye = jnp.where(tr == tc, 1.0, 0.0).astype(BF16)
    rank_t = _dg(sel, upper, _TN)
    sel_t = _dg(sel, eye, _TN)
    pos_t = jnp.where(sel_t > 0.5, rank_t - 1.0, -1.0)
    pos_ref[...] = pos_t[0:SUBLANES, :]
    cnt_ref[0] = jnp.sum(sel.astype(F32), axis=0, keepdims=True)


def router(x, ya, yb, wa, wb, g, rw, rb, tm):
    m, d = x.shape
    n_exp = rw.shape[1]
    assert n_exp <= SUBLANES
    rw_pad = jnp.pad(rw, ((0, 0), (0, LANES - n_exp)))
    rb_pad = jnp.pad(rb, (0, LANES - n_exp)).reshape(1, LANES)
    tok = lambda n: pl.BlockSpec((tm, n), lambda i: (i, 0))
    full = lambda a: pl.BlockSpec(a.shape, lambda i: (0, 0))
    return pl.pallas_call(
        functools.partial(_router_kernel, n_exp),
        grid=(m // tm,),
        in_specs=[tok(d), tok(ya.shape[1]), tok(yb.shape[1]), full(wa), full(wb),
                  pl.BlockSpec((1, d), lambda i: (0, 0)),
                  pl.BlockSpec((d, LANES), lambda i: (0, 0)),
                  pl.BlockSpec((1, LANES), lambda i: (0, 0))],
        out_specs=[tok(d), tok(d), tok(LANES),
                   pl.BlockSpec((SUBLANES, tm), lambda i: (0, i)),
                   pl.BlockSpec((1, 1, LANES), lambda i: (i, 0, 0))],
        out_shape=[jax.ShapeDtypeStruct((m, d), F32),
                   jax.ShapeDtypeStruct((m, d), BF16), jax.ShapeDtypeStruct((m, LANES), F32),
                   jax.ShapeDtypeStruct((SUBLANES, m), F32),
                   jax.ShapeDtypeStruct((m // tm, 1, LANES), F32)],
        compiler_params=_cparams(("parallel",)), name="router",
    )(x, ya, yb, wa, wb, g.reshape(1, d), rw_pad, rb_pad)


MOE_ROWS = 128


def _moe_kernel(n_exp, cnt_ref, x_hbm, hn_ref, gate_ref, pos_ref, w1_ref, w3_ref, w2_ref, o_ref, xs_scr, y_scr):
    i = pl.program_id(0)
    e = pl.program_id(1)
    j = pl.program_id(2)
    nj = pl.num_programs(2)
    tm = hn_ref.shape[0]
    n_small = (cnt_ref[i * n_exp + e] + (MOE_ROWS - 1)) // MOE_ROWS

    @pl.when(jnp.logical_and(e == 0, j == 0))
    def _():
        pltpu.sync_copy(x_hbm.at[pl.ds(pl.multiple_of(i * tm, tm), tm), :], o_ref)

    def select(start, nrows):
        pos = pos_ref[pl.ds(e, 1), :]
        want = (start + _iota((nrows, tm), 0)).astype(F32)
        return jnp.where(pos == want, 1.0, 0.0).astype(BF16)

    def blocks(body):
        big = 4 * MOE_ROWS
        assert tm % big == 0

        def run_big(blk, carry):
            body(pl.multiple_of(blk * big, big), big)
            return carry
        lax.fori_loop(0, n_small // 4, run_big, 0)
        done = (n_small // 4) * big

        @pl.when(n_small % 4 >= 2)
        def _():
            body(pl.multiple_of(done, big), 2 * MOE_ROWS)

        @pl.when(n_small % 2 == 1)
        def _():
            body(pl.multiple_of(done + (n_small % 4 // 2) * 2 * MOE_ROWS, MOE_ROWS), MOE_ROWS)

    @pl.when(j == 0)
    def _():
        def gather(start, nrows):
            rows = pl.ds(start, nrows)
            xs_scr[rows, :] = _dg(select(start, nrows), hn_ref[...], _NN).astype(BF16)
            y_scr[rows, :] = jnp.zeros((nrows, y_scr.shape[1]), F32)
        blocks(gather)

    def expert(start, nrows):
        rows = pl.ds(start, nrows)
        xs = xs_scr[rows, :]
        h = _silu(_dg(xs, w1_ref[0], _NN)) * _dg(xs, w3_ref[0], _NN)
        y_scr[rows, :] += _dg(h.astype(BF16), w2_ref[0], _NN)
    blocks(expert)

    @pl.when(j == nj - 1)
    def _():
        gt = gate_ref[...]
        gcol = jnp.sum(jnp.where(_iota(gt.shape, 1) == e, gt, 0.0), axis=-1, keepdims=True)

        def scatter(start, nrows):
            o_ref[...] += gcol * _dg(select(start, nrows), y_scr[pl.ds(start, nrows), :].astype(BF16), _TN)
        blocks(scatter)


def moe(x, hn, gates, pos, counts, w1, w3, w2, tm_router, tm, fc):
    m, d = x.shape
    n_exp, _, dff = w1.shape
    nj = dff // fc
    ratio = tm // tm_router
    cnt = counts[:, 0, :n_exp].astype(jnp.int32).reshape(m // tm, ratio, n_exp)
    before = (jnp.cumsum(cnt, axis=1) - cnt).astype(F32)
    shift = jnp.repeat(before.reshape(m // tm_router, n_exp).T, tm_router, axis=1)
    shift = jnp.pad(shift, ((0, SUBLANES - n_exp), (0, 0)))
    pos = jnp.where(pos >= 0, pos + shift, pos)
    cnt = cnt.sum(axis=1).reshape(-1)
    grid_spec = pltpu.PrefetchScalarGridSpec(
        num_scalar_prefetch=1,
        grid=(m // tm, n_exp, nj),
        in_specs=[pl.BlockSpec(memory_space=pl.ANY),
                  pl.BlockSpec((tm, d), lambda i, e, j, c: (i, 0), pipeline_mode=pl.Buffered(1)),
                  pl.BlockSpec((tm, LANES), lambda i, e, j, c: (i, 0), pipeline_mode=pl.Buffered(1)),
                  pl.BlockSpec((SUBLANES, tm), lambda i, e, j, c: (0, i), pipeline_mode=pl.Buffered(1)),
                  pl.BlockSpec((1, d, fc), lambda i, e, j, c: (e, 0, j)),
                  pl.BlockSpec((1, d, fc), lambda i, e, j, c: (e, 0, j)),
                  pl.BlockSpec((1, fc, d), lambda i, e, j, c: (e, j, 0))],
        out_specs=pl.BlockSpec((tm, d), lambda i, e, j, c: (i, 0), pipeline_mode=pl.Buffered(1)),
        scratch_shapes=[pltpu.VMEM((tm, d), BF16), pltpu.VMEM((tm, d), F32)])
    return pl.pallas_call(
        functools.partial(_moe_kernel, n_exp),
        grid_spec=grid_spec,
        out_shape=jax.ShapeDtypeStruct((m, d), F32),
        compiler_params=_cparams(("parallel", "arbitrary", "arbitrary")), name="moe",
    )(cnt, x, hn, gates, pos, w1, w3, w2)


_RW_DOT_A = _bdot
_RW_DOT_T = _bdot
_RW_DOT_W = _bdot
_RW_DOT_S = _bdot


RW = 512
RSHIFT = 1792


def _rwkv_token_maps(xs, w0_ref, wup_ref, a0_ref, aup_ref, gup_ref, kk_ref, ka_ref, rk_ref):
    r = xs[:, 0:RW]
    k = xs[:, RW:2 * RW]
    v = xs[:, 2 * RW:3 * RW]
    lr = xs[:, 3 * RW:3 * RW + LANES]
    gd = xs[:, 3 * RW + LANES:3 * RW + 2 * LANES]
    w_pre = w0_ref[...] + _bdot(jnp.tanh(lr), wup_ref[...])
    logw = -jnp.exp(-_softplus(-w_pre) - 0.5)
    a = _sigmoid(a0_ref[...] + _bdot(lr, aup_ref[...]))
    g = _bdot(_sigmoid(gd), gup_ref[...])
    kkr = k * kk_ref[...]
    kk = kkr / jnp.maximum(jnp.sqrt(_seg_sum(kkr * kkr, RWKV_HD)), 1e-12)
    k2 = k * (1.0 + (a - 1.0) * ka_ref[...])
    bonus = _seg_sum(r * k2 * rk_ref[...], RWKV_HD) * v
    return r, logw, k2, v, -kk, kk * a, g, bonus


def _rwkv_kernel(L, nb, n_pairs, t_real, p_ref, prev0_ref, s0_ref, mu_ref, w0_ref, wup_ref, a0_ref, aup_ref,
                 gup_ref, kk_ref, ka_ref, rk_ref, lnw_ref, lnb_ref, y_ref, sf_ref, s_scr, prev_scr):
    c = pl.program_id(1)
    nc = pl.num_programs(1)
    L2 = 2 * L
    lane = _iota((1, LANES), 1)
    m0 = jnp.where(lane < RWKV_HD, 1.0, 0.0)
    m1 = 1.0 - m0
    rr = _iota((L2, L2), 0)
    cc = _iota((L2, L2), 1)
    same = (rr // L) == (cc // L)
    strict = jnp.logical_and(same, (cc % L) < (rr % L))
    incl = jnp.logical_and(same, (cc % L) <= (rr % L))
    eye2 = jnp.where(rr == cc, 1.0, 0.0)
    r128 = _iota((LANES, LANES), 0)
    c128 = _iota((LANES, LANES), 1)
    blk128 = (r128 // RWKV_HD) == (c128 // RWKV_HD)
    diag128 = r128 == c128
    fmat = jnp.where(_iota((LANES, RWKV_HD), 0) % RWKV_HD == _iota((LANES, RWKV_HD), 1), 1.0, 0.0)
    tri = _tri_incl(L)

    def bd(x):
        return jnp.concatenate([x * m0, x * m1], axis=0)

    @pl.when(c == 0)
    def _():
        for bb in range(nb):
            prev_scr[bb, 0:1, :] = prev0_ref[bb]
            for j in range(n_pairs):
                s0 = s0_ref[bb, j]
                st = _xdot_r(fmat, s0, _NT)
                s_scr[bb * n_pairs + j] = jnp.where(blk128, st, 0.0)

    rowi = _iota((L, 1), 0)
    xs = []
    for bb in range(nb):
        p = p_ref[bb]
        prev = jnp.where(rowi == 0, prev_scr[bb, 0:1, :], pltpu.roll(p, 1, 0))
        prev_scr[bb, 0:1, :] = p[L - 1:L, :]
        xs.append(p + (prev - p) * mu_ref[...])
    maps = _rwkv_token_maps(jnp.concatenate(xs, axis=0), w0_ref, wup_ref, a0_ref, aup_ref, gup_ref,
                            kk_ref, ka_ref, rk_ref)
    if t_real % L != 0:
        valid = (c * L + _iota((nb * L, 1), 0) % L) < t_real
        maps = tuple(jnp.where(valid, z, 0.0) for z in maps)
    toks = [tuple(z[bb * L:(bb + 1) * L] for z in maps) for bb in range(nb)]

    pairs = range(nb * n_pairs)
    bbs = [i // n_pairs for i in pairs]
    sls = [slice((i % n_pairs) * LANES, (i % n_pairs + 1) * LANES) for i in pairs]
    tok = lambda j, which: toks[bbs[j]][which][:, sls[j]]
    lw = [tok(j, 1) for j in pairs]
    b = [_xdot_r(tri, x) for x in lw]
    bl = [x[L - 1:L, :] for x in b]
    e_b = [jnp.exp(x) for x in b]
    e_nb = [jnp.exp(-x) for x in b]
    e_lb = [jnp.exp(bl[j] - b[j]) for j in pairs]
    at = [bd(tok(j, 4) * jnp.exp(b[j] - lw[j])) for j in pairs]
    rt = [bd(tok(j, 0) * e_b[j]) for j in pairs]
    bh = [bd(tok(j, 5) * e_nb[j]) for j in pairs]
    kh = [bd(tok(j, 2) * e_nb[j]) for j in pairs]
    bt = [bd(tok(j, 5) * e_lb[j]) for j in pairs]
    kt = [bd(tok(j, 2) * e_lb[j]) for j in pairs]
    vb = [bd(tok(j, 3)) for j in pairs]
    if L2 % LANES == 0:
        gq = [_RW_DOT_A(jnp.concatenate([at[j], rt[j]], axis=0), jnp.concatenate([bh[j], kh[j]], axis=0), _NT)
              for j in pairs]
        a_ab = [jnp.where(strict, g[0:L2, 0:L2], 0.0) for g in gq]
        a_ak = [jnp.where(strict, g[0:L2, L2:2 * L2], 0.0) for g in gq]
        a_rb = [jnp.where(incl, g[L2:2 * L2, 0:L2], 0.0) for g in gq]
        a_rk = [jnp.where(incl, g[L2:2 * L2, L2:2 * L2], 0.0) for g in gq]
    else:
        a_ab = [jnp.where(strict, _RW_DOT_A(at[j], bh[j], _NT), 0.0) for j in pairs]
        a_ak = [jnp.where(strict, _RW_DOT_A(at[j], kh[j], _NT), 0.0) for j in pairs]
        a_rb = [jnp.where(incl, _RW_DOT_A(rt[j], bh[j], _NT), 0.0) for j in pairs]
        a_rk = [jnp.where(incl, _RW_DOT_A(rt[j], kh[j], _NT), 0.0) for j in pairs]
    x = a_ab
    tinv = [eye2 + a for a in a_ab]
    span = 2
    while span < L:
        x = [_RW_DOT_T(xx, xx) for xx in x]
        tinv = [tinv[j] + _RW_DOT_T(tinv[j], x[j]) for j in pairs]
        span *= 2
    akv = [_RW_DOT_W(a_ak[j], vb[j]) for j in pairs]
    wuv = [_RW_DOT_W(tinv[j], jnp.concatenate([at[j], akv[j]], axis=1)) for j in pairs]
    mn = [_RW_DOT_W(bt[j], wuv[j], _TN) for j in pairs]
    qy = [_RW_DOT_W(a_rb[j], wuv[j]) for j in pairs]
    mm = [jnp.where(diag128, jnp.exp(bl[j]), 0.0) + mn[j][:, 0:LANES] for j in pairs]
    nn = [mn[j][:, LANES:2 * LANES] + _RW_DOT_W(kt[j], vb[j], _TN) for j in pairs]
    q = [rt[j] + qy[j][:, 0:LANES] for j in pairs]
    yv = [qy[j][:, LANES:2 * LANES] + _RW_DOT_W(a_rk[j], vb[j]) for j in pairs]
    s = [s_scr[j] for j in pairs]
    ybd = [_RW_DOT_S(q[j], s[j]) + yv[j] for j in pairs]
    for j in pairs:
        s_scr[j] = _RW_DOT_S(mm[j], s[j]) + nn[j]

    y = jnp.concatenate(
        [jnp.concatenate([ybd[bb * n_pairs + jj][0:L] + ybd[bb * n_pairs + jj][L:L2] for jj in range(n_pairs)],
                         axis=1) for bb in range(nb)], axis=0)
    mean = _seg_sum(y, RWKV_HD) * (1.0 / RWKV_HD)
    d = y - mean
    var = _seg_sum(d * d, RWKV_HD) * (1.0 / RWKV_HD)
    ya = (d * lax.rsqrt(var + RWKV_GN_EPS) * lnw_ref[...] + lnb_ref[...] + maps[7]) * maps[6]
    for bb in range(nb):
        y_ref[bb] = ya[bb * L:(bb + 1) * L]

    @pl.when(c == nc - 1)
    def _():
        for i in pairs:
            sf_ref[bbs[i], i % n_pairs] = _xdot_l(s_scr[i], fmat, _TN)


def rwkv_params(w, i):
    z64 = jnp.zeros((RWKV_HD, RW), F32)
    return {
        "mu": w["e_mu"][i].reshape(1, RSHIFT),
        "w0": w["e_w0"][i].reshape(1, RW),
        "wup": jnp.concatenate([w["e_w_up"][i], z64], 0).astype(BF16),
        "a0": w["e_a0"][i].reshape(1, RW),
        "aup": jnp.concatenate([z64, w["e_a_up"][i]], 0).astype(BF16),
        "gup": w["e_g_up"][i].astype(BF16),
        "k_k": w["e_k_k"][i].reshape(1, RW),
        "k_a": w["e_k_a"][i].reshape(1, RW),
        "r_k": w["e_r_k"][i].reshape(1, RW),
    }


def rwkv_mix(p_flat, prev, s0, bsz, t, L, prm, lnx_w, lnx_b, nb=1):
    n_pairs = RW // LANES
    tp = -(-t // L) * L
    p3 = p_flat.reshape(bsz, t, RSHIFT)
    if tp != t:
        p3 = jnp.pad(p3, ((0, 0), (0, tp - t), (0, 0)))
    row = lambda n: pl.BlockSpec((1, n), lambda b, c: (0, 0))
    lora = pl.BlockSpec((LANES, RW), lambda b, c: (0, 0))
    st_spec = pl.BlockSpec((nb, n_pairs, LANES, RWKV_HD), lambda b, c: (b, 0, 0, 0))
    y, s_fin = pl.pallas_call(
        functools.partial(_rwkv_kernel, L, nb, n_pairs, t),
        grid=(bsz // nb, tp // L),
        in_specs=[pl.BlockSpec((nb, L, RSHIFT), lambda b, c: (b, c, 0)),
                  pl.BlockSpec((nb, 1, RSHIFT), lambda b, c: (b, 0, 0)),
                  st_spec,
                  row(RSHIFT), row(RW), lora, row(RW), lora, lora, row(RW), row(RW), row(RW), row(RW), row(RW)],
        out_specs=[pl.BlockSpec((nb, L, RW), lambda b, c: (b, c, 0)), st_spec],
        out_shape=[jax.ShapeDtypeStruct((bsz, tp, RW), F32),
                   jax.ShapeDtypeStruct((bsz, n_pairs, LANES, RWKV_HD), F32)],
        scratch_shapes=[pltpu.VMEM((nb * n_pairs, LANES, LANES), F32),
                        pltpu.VMEM((nb, SUBLANES, RSHIFT), F32)],
        compiler_params=_cparams(("parallel", "arbitrary")), name="rwkv",
    )(p3, prev.reshape(bsz, 1, RSHIFT), s0.reshape(bsz, n_pairs, LANES, RWKV_HD),
      prm["mu"], prm["w0"], prm["wup"], prm["a0"], prm["aup"], prm["gup"], prm["k_k"], prm["k_a"], prm["r_k"],
      lnx_w.reshape(1, RW), lnx_b.reshape(1, RW))
    return y[:, :t].reshape(bsz * t, RW), s_fin.reshape(bsz, RW // RWKV_HD, RWKV_HD, RWKV_HD)


N_Q_HEADS = 8
N_KV_HEADS = 2
Q_PER_KV = N_Q_HEADS // N_KV_HEADS
QW = N_Q_HEADS * ATT_HD
KW = N_KV_HEADS * ATT_HD
NEG = -1e30


def _t5_bucket_np(dist):
    n = np.maximum(dist, 0)
    max_exact = N_BUCKETS // 2
    nf = np.maximum(n, 1).astype(np.float32)
    large = max_exact + (np.log(nf / np.float32(max_exact)) / np.float32(math.log(BUCKET_MAX_DIST / max_exact))
                         * np.float32(N_BUCKETS - max_exact)).astype(np.int32)
    large = np.minimum(large, N_BUCKETS - 1)
    return np.where(n < max_exact, n, large)


def _bias_kernel(rt_ref, oh_ref, o_ref):
    o_ref[...] = _xdot_l(rt_ref[...], oh_ref[...])


def rel_bias(rel_table, dist):
    bucket = _t5_bucket_np(dist).reshape(-1)
    n = bucket.shape[0]
    onehot = jnp.asarray((np.arange(N_BUCKETS)[:, None] == bucket[None, :]).astype(np.float32), BF16)
    out = pl.pallas_call(
        _bias_kernel,
        out_shape=jax.ShapeDtypeStruct((N_Q_HEADS, n), F32),
    )(rel_table.T, onehot)
    return out.reshape((N_Q_HEADS,) + dist.shape)


def _head_norm(x, w_row):
    return x * lax.rsqrt(_seg_sum(x * x, ATT_HD) * (1.0 / ATT_HD) + EPS) * w_row


def _swa_prompt_kernel(nqb, q_ref, kvc_ref, kvp_ref, qw_ref, kw_ref, bias_ref, sink_ref, o_ref, ko_ref, vo_ref):
    i = pl.program_id(1)
    qn = _head_norm(q_ref[0], qw_ref[...])
    kvc = kvc_ref[0]
    kvp = kvp_ref[0]
    kn = [_head_norm(kvp[:, 0:KW], kw_ref[...])]
    vs = [kvp[:, KW:2 * KW]]
    kcn = _head_norm(kvc[:, 0:KW], kw_ref[...])
    for j in range(nqb):
        kn.append(kcn[j * WINDOW:(j + 1) * WINDOW])
        vs.append(kvc[j * WINDOW:(j + 1) * WINDOW, KW:2 * KW])
    kcat = [jnp.concatenate([kn[j], kn[j + 1]], axis=0).astype(BF16) for j in range(nqb)]
    vcat = [jnp.concatenate([vs[j], vs[j + 1]], axis=0).astype(BF16) for j in range(nqb)]
    qi = _iota((WINDOW, 2 * WINDOW), 0)
    kj = _iota((WINDOW, 2 * WINDOW), 1)
    dist = qi + WINDOW - kj
    band = jnp.logical_and(dist >= 0, dist < WINDOW)
    valid = [jnp.logical_and(band, jnp.logical_or(kj >= WINDOW, i > 0))] + [band] * (nqb - 1)
    lane = _iota((1, LANES), 1)
    masks = (jnp.where(lane < ATT_HD, 1.0, 0.0), jnp.where(lane < ATT_HD, 0.0, 1.0))
    scale = ATT_HD ** -0.5
    heads = range(N_Q_HEADS)
    units = [(j, h) for j in range(nqb) for h in heads]
    kv_of = lambda h: h // Q_PER_KV
    qts = [qn[j * WINDOW:(j + 1) * WINDOW, (h // 2) * LANES:(h // 2 + 1) * LANES] for j, h in units]
    qts = [pltpu.roll(qts[u], ATT_HD, 1) if h % 2 != kv_of(h) else qts[u] for u, (j, h) in enumerate(units)]
    qms = [(qts[u] * masks[kv_of(h)]).astype(BF16) for u, (j, h) in enumerate(units)]
    s_all = [_dg(jnp.concatenate(qms[j * N_Q_HEADS:(j + 1) * N_Q_HEADS], axis=0), kcat[j], _NT) * scale
             for j in range(nqb)]
    logits = [jnp.where(valid[j], s_all[j][h * WINDOW:(h + 1) * WINDOW] + bias_ref[h], NEG) for j, h in units]
    sinks = [sink_ref[h:h + 1, 0:1] for j, h in units]
    mx = [jnp.maximum(jnp.max(logits[u], axis=-1, keepdims=True), sinks[u]) for u in range(len(units))]
    pr = [jnp.exp(logits[u] - mx[u]) for u in range(len(units))]
    den = [jnp.sum(pr[u], axis=-1, keepdims=True) + jnp.exp(sinks[u] - mx[u]) for u in range(len(units))]
    probs = [(pr[u] * (1.0 / den[u])).astype(BF16) for u in range(len(units))]
    o_all = [_dg(jnp.concatenate(probs[j * N_Q_HEADS:(j + 1) * N_Q_HEADS], axis=0), vcat[j], _NN)
             for j in range(nqb)]
    os_ = [o_all[j][h * WINDOW:(h + 1) * WINDOW] for j, h in units]
    os_ = [pltpu.roll(os_[u], ATT_HD, 1) if h % 2 != kv_of(h) else os_[u] for u, (j, h) in enumerate(units)]
    for j in range(nqb):
        for jq in range(QW // LANES):
            o_ref[0, j * WINDOW:(j + 1) * WINDOW, jq * LANES:(jq + 1) * LANES] = (
                os_[j * N_Q_HEADS + 2 * jq] * masks[0] + os_[j * N_Q_HEADS + 2 * jq + 1] * masks[1])
    ko_ref[0] = kn[nqb]
    vo_ref[0] = vs[nqb]


SWA_QB = 2


def swa_prompt(q, kv, bsz, t, q_norm, k_norm, bias, sinks):
    nqb = math.gcd(t // WINDOW, SWA_QB)
    rows = nqb * WINDOW
    nb = t // rows
    q3 = q.reshape(bsz, t, QW)
    kv3 = kv.reshape(bsz, t, 2 * KW)
    o, ko, vo = pl.pallas_call(
        functools.partial(_swa_prompt_kernel, nqb),
        grid=(bsz, nb),
        in_specs=[pl.BlockSpec((1, rows, QW), lambda b, i: (b, i, 0)),
                  pl.BlockSpec((1, rows, 2 * KW), lambda b, i: (b, i, 0)),
                  pl.BlockSpec((1, WINDOW, 2 * KW), lambda b, i: (b, jnp.maximum(i * nqb - 1, 0), 0)),
                  pl.BlockSpec((1, QW), lambda b, i: (0, 0)),
                  pl.BlockSpec((1, KW), lambda b, i: (0, 0)),
                  pl.BlockSpec((N_Q_HEADS, WINDOW, 2 * WINDOW), lambda b, i: (0, 0, 0)),
                  pl.BlockSpec((N_Q_HEADS, LANES), lambda b, i: (0, 0))],
        out_specs=[pl.BlockSpec((1, rows, QW), lambda b, i: (b, i, 0)),
                   pl.BlockSpec((1, WINDOW, KW), lambda b, i: (b, 0, 0)),
                   pl.BlockSpec((1, WINDOW, KW), lambda b, i: (b, 0, 0))],
        out_shape=[jax.ShapeDtypeStruct((bsz, t, QW), F32),
                   jax.ShapeDtypeStruct((bsz, WINDOW, KW), F32),
                   jax.ShapeDtypeStruct((bsz, WINDOW, KW), F32)],
        compiler_params=_cparams(("parallel", "arbitrary")), name="swa_prompt",
    )(q3, kv3, kv3, jnp.tile(q_norm, N_Q_HEADS).reshape(1, QW), jnp.tile(k_norm, N_KV_HEADS).reshape(1, KW),
      bias, jnp.broadcast_to(sinks[:, None], (N_Q_HEADS, LANES)))
    return o.reshape(bsz * t, QW), ko, vo


DEC_TP = 8


def _swa_decode_kernel(nbt, t_real, q_ref, kv_ref, ck_ref, cv_ref, qw_ref, kw_ref, bc_ref, bn_ref, sink_ref,
                       o_ref, ko_ref, vo_ref):
    rows = N_Q_HEADS * DEC_TP
    tq = _iota((rows, WINDOW), 0) % DEC_TP
    valid_c = _iota((rows, WINDOW), 1) > tq
    jn = _iota((rows, DEC_TP), 1)
    valid_n = jnp.logical_and(jn <= _iota((rows, DEC_TP), 0) % DEC_TP, jn < t_real)
    lane = _iota((1, LANES), 1)
    masks = (jnp.where(lane < ATT_HD, 1.0, 0.0), jnp.where(lane < ATT_HD, 0.0, 1.0))
    row8 = _iota((DEC_TP, 1), 0)
    scale = ATT_HD ** -0.5
    bias_c = jnp.concatenate([bc_ref[kv] for kv in range(N_KV_HEADS)], axis=0)
    bias_n = jnp.concatenate([bn_ref[kv][:, 0:DEC_TP] for kv in range(N_KV_HEADS)], axis=0)
    sink = jnp.concatenate([sink_ref[kv][:, 0:1] for kv in range(N_KV_HEADS)], axis=0)
    bs = range(nbt)
    heads = range(N_Q_HEADS)
    qn_all = _head_norm(q_ref[...].reshape(nbt * DEC_TP, QW), qw_ref[...])
    kvn_all = kv_ref[...].reshape(nbt * DEC_TP, 2 * KW)
    knew_all = _head_norm(kvn_all[:, 0:KW], kw_ref[...])
    knew = [knew_all[b * DEC_TP:(b + 1) * DEC_TP] for b in bs]
    vnew = [kvn_all[b * DEC_TP:(b + 1) * DEC_TP, KW:2 * KW] for b in bs]
    kc = [ck_ref[b] for b in bs]
    vc = [cv_ref[b] for b in bs]

    def stack_q(b):
        pieces = []
        for h in heads:
            qt = qn_all[b * DEC_TP:(b + 1) * DEC_TP, (h // 2) * LANES:(h // 2 + 1) * LANES]
            if h % 2 != h // Q_PER_KV:
                qt = pltpu.roll(qt, ATT_HD, 1)
            pieces.append(qt * masks[h // Q_PER_KV])
        return jnp.concatenate(pieces, axis=0)

    qs = [stack_q(b) for b in bs]
    l_c = [jnp.where(valid_c, _bdot(qs[b], kc[b], _NT) * scale + bias_c, NEG) for b in bs]
    l_n = [jnp.where(valid_n, _bdot(qs[b], knew[b], _NT) * scale + bias_n, NEG) for b in bs]
    mx = [jnp.maximum(jnp.maximum(jnp.max(l_c[b], axis=-1, keepdims=True),
                                  jnp.max(l_n[b], axis=-1, keepdims=True)), sink) for b in bs]
    p_c = [jnp.exp(l_c[b] - mx[b]) for b in bs]
    p_n = [jnp.exp(l_n[b] - mx[b]) for b in bs]
    inv = [1.0 / (jnp.sum(p_c[b], axis=-1, keepdims=True) + jnp.sum(p_n[b], axis=-1, keepdims=True)
                  + jnp.exp(sink - mx[b])) for b in bs]
    o = [_bdot(p_c[b] * inv[b], vc[b]) + _bdot(p_n[b] * inv[b], vnew[b]) for b in bs]
    for b in bs:
        for jq in range(QW // LANES):
            parts = []
            for h in (2 * jq, 2 * jq + 1):
                piece = o[b][h * DEC_TP:(h + 1) * DEC_TP]
                if h % 2 != h // Q_PER_KV:
                    piece = pltpu.roll(piece, ATT_HD, 1)
                parts.append(piece * masks[h % 2])
            o_ref[b, :, jq * LANES:(jq + 1) * LANES] = parts[0] + parts[1]
    for b in bs:
        for cache, new, out in ((kc[b], knew[b], ko_ref), (vc[b], vnew[b], vo_ref)):
            shifted = pltpu.roll(cache, WINDOW - t_real, 0)
            new_r = pltpu.roll(new, DEC_TP - t_real, 0)
            out[b, 0:WINDOW - DEC_TP] = shifted[0:WINDOW - DEC_TP]
            out[b, WINDOW - DEC_TP:WINDOW] = jnp.where(row8 >= DEC_TP - t_real, new_r,
                                                       shifted[WINDOW - DEC_TP:WINDOW])


def swa_decode(q, kv, cache_k, cache_v, bsz, t, q_norm, k_norm, rel_table, sinks, nbt):
    pad = ((0, 0), (0, DEC_TP - t), (0, 0))
    q3 = jnp.pad(q.reshape(bsz, t, QW), pad)
    kv3 = jnp.pad(kv.reshape(bsz, t, 2 * KW), pad)
    kpos = np.concatenate([np.arange(WINDOW) - WINDOW, np.arange(DEC_TP)])
    dist = np.arange(DEC_TP)[:, None] - kpos[None, :]
    bias = rel_bias(rel_table, dist)
    rows = Q_PER_KV * DEC_TP
    bias = bias.reshape(N_KV_HEADS, rows, WINDOW + DEC_TP)
    bias_c = bias[:, :, :WINDOW]
    bias_n = jnp.pad(bias[:, :, WINDOW:], ((0, 0), (0, 0), (0, LANES - DEC_TP)))
    sink_rows = jnp.broadcast_to(sinks.reshape(N_KV_HEADS, Q_PER_KV, 1, 1),
                                 (N_KV_HEADS, Q_PER_KV, DEC_TP, LANES)).reshape(N_KV_HEADS, rows, LANES)
    full3 = lambda shape: pl.BlockSpec(shape, lambda i: (0, 0, 0))
    o, ko, vo = pl.pallas_call(
        functools.partial(_swa_decode_kernel, nbt, t),
        grid=(bsz // nbt,),
        in_specs=[pl.BlockSpec((nbt, DEC_TP, QW), lambda i: (i, 0, 0)),
                  pl.BlockSpec((nbt, DEC_TP, 2 * KW), lambda i: (i, 0, 0)),
                  pl.BlockSpec((nbt, WINDOW, KW), lambda i: (i, 0, 0)),
                  pl.BlockSpec((nbt, WINDOW, KW), lambda i: (i, 0, 0)),
                  pl.BlockSpec((1, QW), lambda i: (0, 0)),
                  pl.BlockSpec((1, KW), lambda i: (0, 0)),
                  full3((N_KV_HEADS, rows, WINDOW)),
                  full3((N_KV_HEADS, rows, LANES)),
                  full3((N_KV_HEADS, rows, LANES))],
        out_specs=[pl.BlockSpec((nbt, DEC_TP, QW), lambda i: (i, 0, 0)),
                   pl.BlockSpec((nbt, WINDOW, KW), lambda i: (i, 0, 0)),
                   pl.BlockSpec((nbt, WINDOW, KW), lambda i: (i, 0, 0))],
        out_shape=[jax.ShapeDtypeStruct((bsz, DEC_TP, QW), F32),
                   jax.ShapeDtypeStruct((bsz, WINDOW, KW), F32),
                   jax.ShapeDtypeStruct((bsz, WINDOW, KW), F32)],
        compiler_params=_cparams(("parallel",)), name="swa_decode",
    )(q3, kv3, cache_k, cache_v, jnp.tile(q_norm, N_Q_HEADS).reshape(1, QW),
      jnp.tile(k_norm, N_KV_HEADS).reshape(1, KW), bias_c, bias_n, sink_rows)
    return o[:, :t].reshape(bsz * t, QW), ko, vo


S5_G = 32
S5_W = S5_G * S5_P
S5_CP = S5_CHUNK * S5_P
S5_PK = 2 * S5_N
S5_HW = S5_G * S5_PK


S5_QT = S5_W // LANES
S5_GT = LANES // S5_P
S5_XW = S5_CHUNK * LANES
S5_HQ = S5_GT * S5_PK


def _s5_group_maps(t_effs, a2, ldt, b1, b2, c1, c2):
    L = S5_CHUNK
    ar2 = a2[0:1, :]
    ai2 = a2[1:2, :]
    step = jnp.exp(ldt)
    mi = _iota((3 * SUBLANES, S5_PK), 0).astype(F32)
    mag = jnp.exp(mi * (step * ar2))
    ang = mi * (step * ai2)
    pwa = mag * jnp.cos(ang)
    pwb = mag * jnp.sin(ang)
    abr = pwa[1:2]
    abi = pwb[1:2]
    den = ar2 * ar2 + ai2 * ai2
    fa = ((abr - 1.0) * ar2 + abi * ai2) / den
    fb = (abi * ar2 - (abr - 1.0) * ai2) / den
    bp1 = b1 * fa + b2 * fb
    bp2 = b2 * fa - b1 * fb
    cpow = [c1 * pwa[m:m + 1] + c2 * pwb[m:m + 1] for m in range(L + 1)]
    kern_t = _hdot(bp1, jnp.concatenate(cpow[0:L], axis=0), _NT)
    sgn = jnp.where(_iota((1, S5_PK), 1) < S5_N, -1.0, 1.0)
    kbs, als = [], []
    for t_eff in t_effs:
        kbs.append([bp1 * pwa[max(t_eff - 1 - i, 0):max(t_eff - 1 - i, 0) + 1]
                    + bp2 * pwb[max(t_eff - 1 - i, 0):max(t_eff - 1 - i, 0) + 1] for i in range(L)])
        als.append(jnp.concatenate([pwa[t_eff:t_eff + 1], sgn * pwb[t_eff:t_eff + 1]], axis=0))
    return kern_t, kbs, als, cpow[1:L + 1]


def _s5_prep_kernel(t_effs, a_ref, ldt_ref, b1_ref, b2_ref, c1_ref, c2_ref, bd_ref, kc_ref, *rest):
    L = S5_CHUNK
    n_t = len(t_effs)
    kb_refs = rest[:n_t]
    al_refs = rest[n_t:2 * n_t]
    lane = _iota((S5_P, LANES), 1)
    for kb_ref in kb_refs:
        kb_ref[0] = jnp.zeros(kb_ref.shape[1:], kb_ref.dtype)
    bd_rows = [[] for _ in range(L)]
    kct_rows = [[] for _ in range(L)]
    for g in range(S5_GT):
        kern_t, kbs, als, kct = _s5_group_maps(t_effs, a_ref[g], ldt_ref[g], b1_ref[g], b2_ref[g],
                                               c1_ref[g], c2_ref[g])
        in_group = jnp.logical_and(lane >= g * S5_P, lane < (g + 1) * S5_P)
        for tau in range(L):
            shift = (g * S5_P - tau * S5_P) % S5_CP
            moved = pltpu.roll(kern_t, shift, 1) if shift else kern_t
            bd_rows[tau].append(jnp.where(in_group, moved[:, 0:LANES], 0.0))
        for k in range(n_t):
            for i in range(L):
                kb_refs[k][0, i * LANES + g * S5_P:i * LANES + (g + 1) * S5_P, g * S5_PK:(g + 1) * S5_PK] = (
                    kbs[k][i].astype(kb_refs[k].dtype))
            al_refs[k][g] = als[k]
        zl = jnp.zeros((S5_P, g * S5_PK), F32)
        zr = jnp.zeros((S5_P, (S5_GT - 1 - g) * S5_PK), F32)
        for t in range(L):
            parts = ([zl] if g else []) + [kct[t]] + ([zr] if g < S5_GT - 1 else [])
            kct_rows[t].append(jnp.concatenate(parts, axis=1))
    for tau in range(L):
        bd_ref[0, tau] = jnp.concatenate(bd_rows[tau], axis=0).astype(bd_ref.dtype)
    for t in range(L):
        blk_t = jnp.concatenate(kct_rows[t], axis=0)
        kc_ref[0, :, t * LANES:(t + 1) * LANES] = blk_t.T.astype(kc_ref.dtype)


def s5_prep(w, i, t_effs):
    dup = lambda z: jnp.concatenate([z, z], axis=-1)
    a = jnp.stack([dup(w["o_a_re"][i]), dup(w["o_a_im"][i])], axis=1)
    ldt = jnp.broadcast_to(w["o_log_dt"][i][:, None, None], (S5_G, 1, S5_PK))
    bt_re = jnp.swapaxes(w["o_b_re"][i], 1, 2)
    bt_im = jnp.swapaxes(w["o_b_im"][i], 1, 2)
    b1 = jnp.concatenate([bt_re, bt_im], -1)
    b2 = jnp.concatenate([-bt_im, bt_re], -1)
    c_re, c_im = w["o_c_re"][i], w["o_c_im"][i]
    c1 = jnp.concatenate([c_re, -c_im], -1)
    c2 = jnp.concatenate([-c_im, -c_re], -1)
    n_t = len(t_effs)
    g3 = lambda r, c: pl.BlockSpec((S5_GT, r, c), lambda q: (q, 0, 0))
    outs = pl.pallas_call(
        functools.partial(_s5_prep_kernel, tuple(t_effs)),
        grid=(S5_QT,),
        in_specs=[g3(2, S5_PK), g3(1, S5_PK), g3(S5_P, S5_PK), g3(S5_P, S5_PK), g3(S5_P, S5_PK), g3(S5_P, S5_PK)],
        out_specs=[pl.BlockSpec((1, S5_CHUNK, LANES, LANES), lambda q: (q, 0, 0, 0)),
                   pl.BlockSpec((1, S5_HQ, S5_XW), lambda q: (q, 0, 0))]
                  + [pl.BlockSpec((1, S5_XW, S5_HQ), lambda q: (q, 0, 0))] * n_t
                  + [g3(2, S5_PK)] * n_t,
        out_shape=[jax.ShapeDtypeStruct((S5_QT, S5_CHUNK, LANES, LANES), BF16),
                   jax.ShapeDtypeStruct((S5_QT, S5_HQ, S5_XW), BF16)]
                  + [jax.ShapeDtypeStruct((S5_QT, S5_XW, S5_HQ), BF16)] * n_t
                  + [jax.ShapeDtypeStruct((S5_G, 2, S5_PK), F32)] * n_t,
        compiler_params=_cparams(("parallel",)), name="s5_prep",
    )(a, ldt, b1, b2, c1, c2)
    bd, kc = outs[0], outs[1]
    mats = []
    for k in range(n_t):
        al = outs[2 + n_t + k]
        mats.append((bd, outs[2 + k], kc, al[:, 0, :].reshape(1, S5_HW), al[:, 1, :].reshape(1, S5_HW)))
    return mats


def _s5_e_kernel(u_ref, kb_ref, e_ref):
    e_ref[...] = _bdot(u_ref[0], kb_ref[0])


def _s5_swap(h):
    n = h.shape[-1]
    lane = _iota(h.shape, 1)
    return jnp.where(lane % S5_PK < S5_N, pltpu.roll(h, n - S5_N, 1), pltpu.roll(h, S5_N, 1))


def _s5_scan_kernel(bsz, cg, e_ref, h0_ref, ala_ref, alb_ref, hp_ref, hf_ref, h_scr):
    @pl.when(pl.program_id(0) == 0)
    def _():
        h_scr[...] = h0_ref[...]

    ala = ala_ref[...]
    alb = alb_ref[...]

    def body(c, hs):
        out = []
        for b in range(bsz):
            hp_ref[b, pl.ds(c, 1), :] = hs[b]
            out.append(ala * hs[b] + alb * _s5_swap(hs[b]) + e_ref[b, pl.ds(c, 1), :])
        return tuple(out)

    hs = lax.fori_loop(0, cg, body, tuple(h_scr[b:b + 1, :] for b in range(bsz)))
    for b in range(bsz):
        h_scr[b:b + 1, :] = hs[b]
        hf_ref[b:b + 1, :] = hs[b]


def _s5_step_kernel(e_ref, h0_ref, ala_ref, alb_ref, hf_ref):
    h = h0_ref[...]
    hf_ref[...] = ala_ref[...] * h + alb_ref[...] * _s5_swap(h) + e_ref[...]


def _s5_y_kernel(u_ref, hp_ref, bd_ref, kc_ref, y_ref, k_scr):
    @pl.when(pl.program_id(1) == 0)
    def _():
        zero = jnp.zeros((LANES, LANES), k_scr.dtype)
        for i in range(S5_CHUNK):
            for t in range(S5_CHUNK):
                k_scr[i * LANES:(i + 1) * LANES, t * LANES:(t + 1) * LANES] = bd_ref[0, t - i] if t >= i else zero

    y_ref[0] = _bdot(u_ref[0], k_scr[...]) + _bdot(hp_ref[...], kc_ref[0])


def s5_core(x, h0, bsz, mats):
    bd, kbbig, kcbig, ala, alb = mats
    rows = x.shape[1]
    nc = rows // bsz
    tr = math.gcd(rows, 512)
    e = pl.pallas_call(
        _s5_e_kernel,
        grid=(S5_QT, rows // tr),
        in_specs=[pl.BlockSpec((1, tr, S5_XW), lambda q, r: (q, r, 0)),
                  pl.BlockSpec((1, S5_XW, S5_HQ), lambda q, r: (q, 0, 0))],
        out_specs=pl.BlockSpec((tr, S5_HQ), lambda q, r: (r, q)),
        out_shape=jax.ShapeDtypeStruct((rows, S5_HW), F32),
        compiler_params=_cparams(("parallel", "parallel")), name="s5_e",
    )(x, kbbig)
    row = pl.BlockSpec((1, S5_HW), lambda i: (0, 0))
    if nc == 1:
        hp = h0
        hf = pl.pallas_call(
            _s5_step_kernel,
            out_shape=jax.ShapeDtypeStruct((bsz, S5_HW), F32), name="s5_step",
        )(e, h0, ala, alb)
    else:
        cg = math.gcd(nc, 64)
        hp, hf = pl.pallas_call(
            functools.partial(_s5_scan_kernel, bsz, cg),
            grid=(nc // cg,),
            in_specs=[pl.BlockSpec((bsz, cg, S5_HW), lambda i: (0, i, 0)),
                      pl.BlockSpec((bsz, S5_HW), lambda i: (0, 0)), row, row],
            out_specs=[pl.BlockSpec((bsz, cg, S5_HW), lambda i: (0, i, 0)),
                       pl.BlockSpec((bsz, S5_HW), lambda i: (0, 0))],
            out_shape=[jax.ShapeDtypeStruct((bsz, nc, S5_HW), F32),
                       jax.ShapeDtypeStruct((bsz, S5_HW), F32)],
            scratch_shapes=[pltpu.VMEM((bsz, S5_HW), F32)],
            compiler_params=_cparams(("arbitrary",)), name="s5_scan",
        )(e.reshape(bsz, nc, S5_HW), h0, ala, alb)
    y = pl.pallas_call(
        _s5_y_kernel,
        grid=(S5_QT, rows // tr),
        in_specs=[pl.BlockSpec((1, tr, S5_XW), lambda q, r: (q, r, 0)),
                  pl.BlockSpec((tr, S5_HQ), lambda q, r: (r, q)),
                  pl.BlockSpec((1, S5_CHUNK, LANES, LANES), lambda q, r: (q, 0, 0, 0)),
                  pl.BlockSpec((1, S5_HQ, S5_XW), lambda q, r: (q, 0, 0))],
        out_specs=pl.BlockSpec((1, tr, S5_XW), lambda q, r: (q, r, 0)),
        out_shape=jax.ShapeDtypeStruct((S5_QT, rows, S5_XW), F32),
        scratch_shapes=[pltpu.VMEM((S5_XW, S5_XW), BF16)],
        compiler_params=_cparams(("parallel", "arbitrary")), name="s5_y",
    )(x, hp.reshape(rows, S5_HW), bd, kcbig)
    return y, hf


def _s5_post_kernel(tpr, y_ref, u_ref, d_ref, gw_ref, gb_ref, o_ref, y_scr, u_scr):
    _from_chunk_rows(y_ref, y_scr, tpr)
    _from_chunk_rows(u_ref, u_scr, tpr)
    y = jnp.concatenate([y_scr[q] for q in range(S5_QT)], axis=1)
    u = jnp.concatenate([u_scr[q] for q in range(S5_QT)], axis=1)
    x = y + d_ref[...] * u
    z = 0.5 * x * (1.0 + jnp.tanh(math.sqrt(2.0 / math.pi) * (x + 0.044715 * (x * x * x))))
    o_ref[...] = z * _sigmoid(_dg(z.astype(BF16), gw_ref[...], _NN) + gb_ref[...])


def s5_post(y4, u4, m, tpr, d, glu_w_bf16, glu_b, tm):
    spec4 = pl.BlockSpec((S5_QT, tm // tpr, S5_XW), lambda i: (0, i, 0))
    row = pl.BlockSpec((1, S5_W), lambda i: (0, 0))
    return pl.pallas_call(
        functools.partial(_s5_post_kernel, tpr),
        grid=(m // tm,),
        in_specs=[spec4, spec4, row, pl.BlockSpec((S5_W, S5_W), lambda i: (0, 0)), row],
        out_specs=pl.BlockSpec((tm, S5_W), lambda i: (i, 0)),
        out_shape=jax.ShapeDtypeStruct((m, S5_W), F32),
        scratch_shapes=[pltpu.VMEM((S5_QT, tm, LANES), F32), pltpu.VMEM((S5_QT, tm, LANES), F32)],
        compiler_params=_cparams(("parallel",)), name="s5_post",
    )(y4, u4, d.reshape(1, S5_W), glu_w_bf16, glu_b.reshape(1, S5_W))


def s5_mix(u4, h_re, h_im, bsz, t, mats, d, glu_w_bf16, glu_b, tm):
    tpr = min(t, CHUNK_ROW_TOKENS)
    h0 = jnp.concatenate([h_re, h_im], axis=-1).reshape(bsz, S5_HW)
    y4, hf = s5_core(u4, h0, bsz, mats)
    out = s5_post(y4, u4, bsz * t, tpr, d, glu_w_bf16, glu_b, tm)
    hf = hf.reshape(bsz, S5_G, 2, S5_N)
    return out, hf[:, :, 0], hf[:, :, 1]


GLA_DK = 64
GLA_DV = 128
GLA_HEADS = 4
GLA_KW = GLA_HEADS * GLA_DK
GLA_VW = GLA_HEADS * GLA_DV
GLA_PW = 2 * GLA_KW + 2 * GLA_VW + LANES


def _gla_kernel(L, t_real, nb, p_ref, aup_ref, ab_ref, nw_ref, s0_ref, y_ref, sf_ref, s_scr):
    c = pl.program_id(1)
    nc = pl.num_programs(1)
    n_pairs = GLA_KW // LANES

    @pl.when(c == 0)
    def _():
        for bb in range(nb):
            for j in range(n_pairs):
                s_scr[bb * n_pairs + j] = s0_ref[bb, j]

    lane = _iota((1, LANES), 1)
    masks = (jnp.where(lane < GLA_DK, 1.0, 0.0), jnp.where(lane < GLA_DK, 0.0, 1.0))
    incl = _iota((L, L), 1) <= _iota((L, L), 0)
    diag128 = _iota((LANES, LANES), 0) == _iota((LANES, LANES), 1)
    ones128 = jnp.ones((LANES, LANES), BF16)
    tri = _tri_incl(L)
    rows = range(nb)
    ps = [p_ref[bb] for bb in rows]
    zs = [_bdot(p[:, 2 * GLA_KW + 2 * GLA_VW:], aup_ref[...]) + ab_ref[...] for p in ps]
    gk = [-_softplus(-z) * (1.0 / GLA_GATE_NORM) for z in zs]
    if t_real % L != 0:
        tok = c * L + _iota((L, 1), 0)
        gk = [jnp.where(tok < t_real, x, 0.0) for x in gk]
    b = [_xdot_r(tri, x) for x in gk]
    bl = [x[L - 1:L, :] for x in b]
    qd = [ps[i][:, 0:GLA_KW] * (GLA_DK ** -0.5) * jnp.exp(b[i]) for i in rows]
    kh = [ps[i][:, GLA_KW:2 * GLA_KW] * jnp.exp(-b[i]) for i in rows]
    kt = [ps[i][:, GLA_KW:2 * GLA_KW] * jnp.exp(bl[i] - b[i]) for i in rows]
    heads = [(bb, h) for bb in rows for h in range(GLA_HEADS)]
    sl = lambda h: slice((h // 2) * LANES, (h // 2 + 1) * LANES)
    hs = lambda h: slice(2 * GLA_KW + h * GLA_DV, 2 * GLA_KW + (h + 1) * GLA_DV)
    gs = lambda h: slice(2 * GLA_KW + GLA_VW + h * GLA_DV, 2 * GLA_KW + GLA_VW + (h + 1) * GLA_DV)
    st = [s_scr[bb * n_pairs + j] for bb in rows for j in range(n_pairs)]
    qm = [qd[bb][:, sl(h)] * masks[h % 2] for bb, h in heads]
    vh = [ps[bb][:, hs(h)] for bb, h in heads]
    attn = [jnp.where(incl, _bdot(qm[i], kh[bb][:, sl(h)], _NT), 0.0) for i, (bb, h) in enumerate(heads)]
    o = [_bdot(attn[i], vh[i]) + _bdot(qm[i], st[bb * n_pairs + h // 2]) for i, (bb, h) in enumerate(heads)]
    kv = [_bdot(kt[bb][:, sl(h)], vh[i], _TN) for i, (bb, h) in enumerate(heads)]
    for i, (bb, h) in enumerate(heads):
        of = o[i] * lax.rsqrt(jnp.mean(o[i] * o[i], axis=-1, keepdims=True) + EPS) * nw_ref[...]
        y_ref[bb, :, h * GLA_DV:(h + 1) * GLA_DV] = of * _silu(ps[bb][:, gs(h)])
    for bb in rows:
        for j in range(n_pairs):
            i0 = bb * GLA_HEADS + 2 * j
            pcol = _xdot_l(jnp.where(diag128, jnp.exp(bl[bb][:, j * LANES:(j + 1) * LANES]), 0.0), ones128)
            s_scr[bb * n_pairs + j] = pcol * st[bb * n_pairs + j] + jnp.concatenate(
                [kv[i0][0:GLA_DK], kv[i0 + 1][GLA_DK:2 * GLA_DK]], axis=0)

    @pl.when(c == nc - 1)
    def _():
        for bb in range(nb):
            for j in range(n_pairs):
                sf_ref[bb, j] = s_scr[bb * n_pairs + j]


def gla_mix(p_gla, s0, bsz, t, L, aup_pad, a_b, norm_w, nb=1):
    tp = -(-t // L) * L
    p3 = p_gla.reshape(bsz, t, GLA_PW)
    if tp != t:
        p3 = jnp.pad(p3, ((0, 0), (0, tp - t), (0, 0)))
    n_pairs = GLA_KW // LANES
    st_spec = pl.BlockSpec((nb, n_pairs, LANES, LANES), lambda b, c: (b, 0, 0, 0))
    y, s_fin = pl.pallas_call(
        functools.partial(_gla_kernel, L, t, nb),
        grid=(bsz // nb, tp // L),
        in_specs=[pl.BlockSpec((nb, L, GLA_PW), lambda b, c: (b, c, 0)),
                  pl.BlockSpec((LANES, GLA_KW), lambda b, c: (0, 0)),
                  pl.BlockSpec((1, GLA_KW), lambda b, c: (0, 0)),
                  pl.BlockSpec((1, GLA_DV), lambda b, c: (0, 0)),
                  st_spec],
        out_specs=[pl.BlockSpec((nb, L, GLA_VW), lambda b, c: (b, c, 0)), st_spec],
        out_shape=[jax.ShapeDtypeStruct((bsz, tp, GLA_VW), F32),
                   jax.ShapeDtypeStruct((bsz, n_pairs, LANES, LANES), F32)],
        scratch_shapes=[pltpu.VMEM((nb * n_pairs, LANES, LANES), F32)],
        compiler_params=_cparams(("parallel", "arbitrary")), name="gla",
    )(p3, aup_pad, a_b.reshape(1, GLA_KW), norm_w.reshape(1, GLA_DV),
      s0.reshape(bsz, n_pairs, LANES, LANES))
    return y[:, :t].reshape(bsz * t, GLA_VW), s_fin.reshape(bsz, GLA_HEADS, GLA_DK, GLA_DV)


GLA_LR = 16
D_FF_CHUNK = 2816
D_FF_EXPERT_CHUNK = 1792
FFN_TOKENS = 512
ROUTER_TOKENS = 1024
MOE_TOKENS = 2048


def _prepare_weights(w):
    win = w["e_w_in"][0]
    w_gla = jnp.pad(win[:, RSHIFT:], ((0, 0), (0, LANES - GLA_LR)))
    wo = w["o_w_in"][0]
    return {
        "e_w_rwkv": win[:, :RSHIFT].astype(BF16),
        "e_w_gla": w_gla.astype(BF16),
        "rwkv": rwkv_params(w, 0),
        "gla_aup": jnp.pad(w["e_gla_a_up"][0], ((0, LANES - GLA_LR), (0, 0))).astype(BF16),
        "e_wo_a": w["e_w_out"][0][:RW].astype(BF16),
        "e_wo_b": w["e_w_out"][0][RW:].astype(BF16),
        "ff_w1": w["e_ff_w1"][0].astype(BF16),
        "ff_w3": w["e_ff_w3"][0].astype(BF16),
        "ff_w2": w["e_ff_w2"][0].astype(BF16),
        "o_w_q": wo[:, :QW].astype(BF16),
        "o_w_kv": wo[:, QW:QW + 2 * KW].astype(BF16),
        "o_w_u": wo[:, QW + 2 * KW:].astype(BF16),
        "glu_w": w["o_glu_w"][0].astype(BF16),
        "o_wo_a": w["o_w_out"][0][:QW].astype(BF16),
        "o_wo_b": w["o_w_out"][0][QW:].astype(BF16),
        "moe_w1": w["o_moe_w1"][0].astype(BF16),
        "moe_w3": w["o_moe_w3"][0].astype(BF16),
        "moe_w2": w["o_moe_w2"][0].astype(BF16),
    }


def _trunk(x3, st, w, pw, tm, chunk, nb, s5_prep_t, prompt_bias):
    bsz, t, d = x3.shape
    m = bsz * t
    x = x3.reshape(m, d)
    p_r, p_g = norm_proj(x, w["e_norm1"][0], [pw["e_w_rwkv"], pw["e_w_gla"]], tm)
    ya, s_rwkv = rwkv_mix(p_r, st["shift"], st["rwkv"], bsz, t, chunk, pw["rwkv"],
                          w["e_lnx_w"][0], w["e_lnx_b"][0], nb)
    s_shift = p_r.reshape(bsz, t, RSHIFT)[:, -1]
    yb, s_gla = gla_mix(p_g, st["gla"], bsz, t, chunk, pw["gla_aup"], w["e_gla_a_b"][0], w["e_gla_norm"][0], nb)
    x = ffn(x, ya, yb, pw["e_wo_a"], pw["e_wo_b"], w["e_norm2"][0], pw["ff_w1"], pw["ff_w3"], pw["ff_w2"],
            math.gcd(m, FFN_TOKENS), D_FF_CHUNK)
    q, kv, u = norm_proj_tiles(x, w["o_norm1"][0], [pw["o_w_q"], pw["o_w_kv"], pw["o_w_u"]], tm,
                               min(t, CHUNK_ROW_TOKENS))
    if st["win_k"] is None:
        yc, nk, nv = swa_prompt(q, kv, bsz, t, w["o_q_norm"][0], w["o_k_norm"][0], prompt_bias, w["o_sinks"][0])
    else:
        yc, nk, nv = swa_decode(q, kv, st["win_k"].reshape(bsz, WINDOW, KW), st["win_v"].reshape(bsz, WINDOW, KW),
                                bsz, t, w["o_q_norm"][0], w["o_k_norm"][0], w["rel_table"], w["o_sinks"][0], 8)
    yd, s5r, s5i = s5_mix(u, st["s5_re"], st["s5_im"], bsz, t, s5_prep_t, w["o_d"][0], pw["glu_w"],
                          w["o_glu_b"][0], tm)
    t_router = min(m, ROUTER_TOKENS)
    x, hn, gates, pos, counts = router(x, yc, yd, pw["o_wo_a"], pw["o_wo_b"], w["o_norm2"][0],
                                       w["o_router_w"][0], w["o_router_b"][0], t_router)
    x = moe(x, hn, gates, pos, counts, pw["moe_w1"], pw["moe_w3"], pw["moe_w2"], t_router, min(m, MOE_TOKENS),
            D_FF_EXPERT_CHUNK)
    kv_shape = (bsz, WINDOW, N_KV_HEADS, ATT_HD)
    return (x.reshape(bsz, t, d), s_rwkv[None], s_shift[None], s_gla[None], nk.reshape(kv_shape)[None],
            nv.reshape(kv_shape)[None], s5r[None], s5i[None])


def kernel(x_prompt, x_sample, state_rwkv, state_shift, state_gla, cache_win_k, cache_win_v, state_s5_re,
           state_s5_im, rel_table, e_norm1, e_w_in, e_mu, e_w0, e_w_up, e_a0, e_a_up, e_g_up, e_k_k, e_k_a, e_r_k,
           e_lnx_w, e_lnx_b, e_gla_a_up, e_gla_a_b, e_gla_norm, e_w_out, e_norm2, e_ff_w1, e_ff_w3, e_ff_w2,
           o_norm1, o_w_in, o_q_norm, o_k_norm, o_sinks, o_a_re, o_a_im, o_log_dt, o_b_re, o_b_im, o_c_re, o_c_im,
           o_d, o_glu_w, o_glu_b, o_w_out, o_norm2, o_router_w, o_router_b, o_moe_w1, o_moe_w3, o_moe_w2):
    w = dict(rel_table=rel_table, e_norm1=e_norm1, e_w_in=e_w_in, e_mu=e_mu, e_w0=e_w0, e_w_up=e_w_up, e_a0=e_a0,
             e_a_up=e_a_up, e_g_up=e_g_up, e_k_k=e_k_k, e_k_a=e_k_a, e_r_k=e_r_k, e_lnx_w=e_lnx_w, e_lnx_b=e_lnx_b,
             e_gla_a_up=e_gla_a_up, e_gla_a_b=e_gla_a_b, e_gla_norm=e_gla_norm, e_w_out=e_w_out, e_norm2=e_norm2,
             e_ff_w1=e_ff_w1, e_ff_w3=e_ff_w3, e_ff_w2=e_ff_w2, o_norm1=o_norm1, o_w_in=o_w_in, o_q_norm=o_q_norm,
             o_k_norm=o_k_norm, o_sinks=o_sinks, o_a_re=o_a_re, o_a_im=o_a_im, o_log_dt=o_log_dt, o_b_re=o_b_re,
             o_b_im=o_b_im, o_c_re=o_c_re, o_c_im=o_c_im, o_d=o_d, o_glu_w=o_glu_w, o_glu_b=o_glu_b,
             o_w_out=o_w_out, o_norm2=o_norm2, o_router_w=o_router_w, o_router_b=o_router_b, o_moe_w1=o_moe_w1,
             o_moe_w3=o_moe_w3, o_moe_w2=o_moe_w2)
    pw = _prepare_weights(w)
    bp, tp, _ = x_prompt.shape
    bs, ts, _ = x_sample.shape
    qi = np.arange(WINDOW)[:, None]
    kj = np.arange(2 * WINDOW)[None, :]
    prompt_bias = rel_bias(rel_table, qi + WINDOW - kj)
    zeros = lambda *shape: jnp.zeros(shape, F32)
    st_p = {"rwkv": zeros(bp, RW // RWKV_HD, RWKV_HD, RWKV_HD), "shift": zeros(bp, RSHIFT),
            "gla": zeros(bp, GLA_HEADS, GLA_DK, GLA_DV), "win_k": None, "win_v": None,
            "s5_re": zeros(bp, S5_G, S5_N), "s5_im": zeros(bp, S5_G, S5_N)}
    st_s = {"rwkv": state_rwkv[0], "shift": state_shift[0], "gla": state_gla[0], "win_k": cache_win_k[0],
            "win_v": cache_win_v[0], "s5_re": state_s5_re[0], "s5_im": state_s5_im[0]}
    s5_p, s5_s = s5_prep(w, 0, (S5_CHUNK, ts))
    out_p = _trunk(x_prompt, st_p, w, pw, 512, 64, math.gcd(bp, 2), s5_p, prompt_bias)
    out_s = _trunk(x_sample, st_s, w, pw, bs * ts, 8, math.gcd(bs, 16), s5_s, None)
    res = [out_p[0], out_s[0]]
    for a, b in zip(out_p[1:], out_s[1:]):
        res += [a, b]
    return tuple(res)
```

```python
import functools
import math

import jax
import jax.numpy as jnp
import numpy as np
from jax import lax
from jax.experimental import pallas as pl
from jax.experimental.pallas import tpu as pltpu

F32 = jnp.float32
BF16 = jnp.bfloat16

LANES = 128
SUBLANES = 8
VMEM_LIMIT_BYTES = 56 * 1024 * 1024

EPS = 1e-6
RWKV_HD = 64
RWKV_GN_EPS = 64e-5
GLA_GATE_NORM = 16.0
ATT_HD = 64
WINDOW = 128
N_BUCKETS = 32
BUCKET_MAX_DIST = 128
S5_P = 16
S5_N = 64
S5_CHUNK = 16
TOP_K = 2


def _cparams(sem):
    return pltpu.CompilerParams(dimension_semantics=sem, vmem_limit_bytes=VMEM_LIMIT_BYTES)


_NN = (((1,), (0,)), ((), ()))
_NT = (((1,), (1,)), ((), ()))
_TN = (((0,), (0,)), ((), ()))


def _dg(a, b, dims):
    return lax.dot_general(a, b, dims, preferred_element_type=F32)


def _bdot(a, b, dims=_NN):
    return _dg(a.astype(BF16), b.astype(BF16), dims)


def _split(a, n):
    terms = []
    r = a
    for _ in range(n):
        t = r.astype(BF16)
        terms.append(t)
        r = r - t.astype(F32)
    return terms


def _hdot(a, b, dims=_NN):
    a0, a1 = _split(a, 2)
    b0, b1 = _split(b, 2)
    return _dg(a0, b0, dims) + (_dg(a0, b1, dims) + _dg(a1, b0, dims))


def _xdot_l(a, e, dims=_NN):
    e = e.astype(BF16)
    a0, a1, a2 = _split(a, 3)
    return _dg(a0, e, dims) + (_dg(a1, e, dims) + _dg(a2, e, dims))


def _xdot_r(e, b, dims=_NN):
    e = e.astype(BF16)
    b0, b1, b2 = _split(b, 3)
    return _dg(e, b0, dims) + (_dg(e, b1, dims) + _dg(e, b2, dims))


def _iota(shape, axis):
    return lax.broadcasted_iota(jnp.int32, shape, axis)


def _seg_ones(n, seg):
    r = _iota((n, n), 0) // seg
    c = _iota((n, n), 1) // seg
    return jnp.where(r == c, 1.0, 0.0).astype(BF16)


def _seg_sum(x, seg):
    n = x.shape[-1]
    ones = _seg_ones(n, seg)
    x0, x1 = _split(x, 2)
    return _dg(x0, ones, _NN) + _dg(x1, ones, _NN)


def _sigmoid(x):
    return 1.0 / (1.0 + jnp.exp(-x))


def _silu(x):
    return x * _sigmoid(x)


def _softplus(x):
    return jnp.maximum(x, 0.0) + jnp.log(1.0 + jnp.exp(-jnp.abs(x)))


def _tri_incl(n):
    r = _iota((n, n), 0)
    c = _iota((n, n), 1)
    return jnp.where(c <= r, 1.0, 0.0).astype(BF16)


def _rms(x, g):
    return x * lax.rsqrt(jnp.mean(x * x, axis=-1, keepdims=True) + EPS) * g


def _norm_proj_kernel(n_w, x_ref, g_ref, *refs):
    xn = _rms(x_ref[...], g_ref[...]).astype(BF16)
    for w_ref, o_ref in zip(refs[:n_w], refs[n_w:]):
        o_ref[...] = _dg(xn, w_ref[...], _NN)


def norm_proj(x, g, ws_bf16, tm):
    m, d = x.shape
    return pl.pallas_call(
        functools.partial(_norm_proj_kernel, len(ws_bf16)),
        grid=(m // tm,),
        in_specs=[pl.BlockSpec((tm, d), lambda i: (i, 0)),
                  pl.BlockSpec((1, d), lambda i: (0, 0))]
                 + [pl.BlockSpec(w.shape, lambda i: (0, 0)) for w in ws_bf16],
        out_specs=[pl.BlockSpec((tm, w.shape[1]), lambda i: (i, 0)) for w in ws_bf16],
        out_shape=[jax.ShapeDtypeStruct((m, w.shape[1]), F32) for w in ws_bf16],
        compiler_params=_cparams(("parallel",)), name="norm_proj",
    )(x, g.reshape(1, d), *ws_bf16)


CHUNK_ROW_TOKENS = 16


def _to_chunk_rows(src_ref, dst_ref, tpr):
    rows = src_ref.shape[1] // tpr
    for q in range(dst_ref.shape[0]):
        for i in range(CHUNK_ROW_TOKENS):
            cols = slice(i * LANES, (i + 1) * LANES)
            if i < tpr:
                dst_ref[q, :, cols] = src_ref[q, pl.ds(i, rows, stride=tpr), :]
            else:
                dst_ref[q, :, cols] = jnp.zeros((rows, LANES), dst_ref.dtype)


def _from_chunk_rows(src_ref, dst_ref, tpr):
    rows = dst_ref.shape[1] // tpr
    for q in range(src_ref.shape[0]):
        for i in range(tpr):
            dst_ref[q, pl.ds(i, rows, stride=tpr), :] = src_ref[q, :, i * LANES:(i + 1) * LANES]


def _norm_proj_tiles_kernel(n_w, tpr, x_ref, g_ref, *refs):
    xn = _rms(x_ref[...], g_ref[...]).astype(BF16)
    for w_ref, o_ref in zip(refs[:n_w - 1], refs[n_w:]):
        o_ref[...] = _dg(xn, w_ref[...], _NN)
    u_ref, u_scr = refs[-2], refs[-1]
    u = _dg(xn, refs[n_w - 1][...], _NN)
    for q in range(u_scr.shape[0]):
        u_scr[q] = u[:, q * LANES:(q + 1) * LANES]
    _to_chunk_rows(u_scr, u_ref, tpr)


def norm_proj_tiles(x, g, ws_bf16, tm, tpr):
    m, d = x.shape
    nu = ws_bf16[-1].shape[1]
    nt = nu // LANES
    cw = CHUNK_ROW_TOKENS * LANES
    return pl.pallas_call(
        functools.partial(_norm_proj_tiles_kernel, len(ws_bf16), tpr),
        grid=(m // tm,),
        in_specs=[pl.BlockSpec((tm, d), lambda i: (i, 0)),
                  pl.BlockSpec((1, d), lambda i: (0, 0))]
                 + [pl.BlockSpec(w.shape, lambda i: (0, 0)) for w in ws_bf16],
        out_specs=[pl.BlockSpec((tm, w.shape[1]), lambda i: (i, 0)) for w in ws_bf16[:-1]]
                  + [pl.BlockSpec((nt, tm // tpr, cw), lambda i: (0, i, 0))],
        out_shape=[jax.ShapeDtypeStruct((m, w.shape[1]), F32) for w in ws_bf16[:-1]]
                  + [jax.ShapeDtypeStruct((nt, m // tpr, cw), F32)],
        scratch_shapes=[pltpu.VMEM((nt, tm, LANES), F32)],
        compiler_params=_cparams(("parallel",)), name="norm_proj_tiles",
    )(x, g.reshape(1, d), *ws_bf16)


def _mixer_out(x_ref, ya_ref, yb_ref, wa_ref, wb_ref):
    return (x_ref[...] + _dg(ya_ref[...].astype(BF16), wa_ref[...], _NN)
            + _dg(yb_ref[...].astype(BF16), wb_ref[...], _NN))


def _ffn_kernel(x_ref, ya_ref, yb_ref, wa_ref, wb_ref, g_ref, w1_ref, w3_ref, w2_ref, o_ref, xn_scr):
    j = pl.program_id(1)

    @pl.when(j == 0)
    def _():
        x = _mixer_out(x_ref, ya_ref, yb_ref, wa_ref, wb_ref)
        xn_scr[...] = _rms(x, g_ref[...]).astype(BF16)
        o_ref[...] = x

    xn = xn_scr[...]
    h = _silu(_dg(xn, w1_ref[...], _NN)) * _dg(xn, w3_ref[...], _NN)
    o_ref[...] += _dg(h.astype(BF16), w2_ref[...], _NN)


def ffn(x, ya, yb, wa, wb, g, w1, w3, w2, tm, fc):
    m, d = x.shape
    dff = w1.shape[1]
    tok = lambda n: pl.BlockSpec((tm, n), lambda i, j: (i, 0))
    full = lambda a: pl.BlockSpec(a.shape, lambda i, j: (0, 0))
    wmode = pl.Buffered(1) if dff == fc else None
    return pl.pallas_call(
        _ffn_kernel,
        grid=(m // tm, dff // fc),
        in_specs=[tok(d), tok(ya.shape[1]), tok(yb.shape[1]), full(wa), full(wb),
                  pl.BlockSpec((1, d), lambda i, j: (0, 0)),
                  pl.BlockSpec((d, fc), lambda i, j: (0, j), pipeline_mode=wmode),
                  pl.BlockSpec((d, fc), lambda i, j: (0, j), pipeline_mode=wmode),
                  pl.BlockSpec((fc, d), lambda i, j: (j, 0), pipeline_mode=wmode)],
        out_specs=pl.BlockSpec((tm, d), lambda i, j: (i, 0)),
        out_shape=jax.ShapeDtypeStruct((m, d), F32),
        scratch_shapes=[pltpu.VMEM((tm, d), BF16)],
        compiler_params=_cparams(("parallel", "arbitrary")), name="ffn",
    )(x, ya, yb, wa, wb, g.reshape(1, d), w1, w3, w2)


def _router_kernel(n_exp, x_ref, ya_ref, yb_ref, wa_ref, wb_ref, g_ref, rw_ref, rb_ref,
                   xo_ref, hn_ref, gate_ref, pos_ref, cnt_ref):
    x = _mixer_out(x_ref, ya_ref, yb_ref, wa_ref, wb_ref)
    xo_ref[...] = x
    xn = _rms(x, g_ref[...])
    hn_ref[...] = xn.astype(BF16)
    logits = _hdot(xn, rw_ref[...]) + rb_ref[...]
    lane = _iota(logits.shape, 1)
    logits = jnp.where(lane < n_exp, logits, -jnp.inf)
    m1 = jnp.max(logits, axis=-1, keepdims=True)
    i1 = jnp.min(jnp.where(logits == m1, lane, LANES), axis=-1, keepdims=True)
    rest = jnp.where(lane == i1, -jnp.inf, logits)
    m2 = jnp.max(rest, axis=-1, keepdims=True)
    i2 = jnp.min(jnp.where(rest == m2, lane, LANES), axis=-1, keepdims=True)
    e2 = jnp.exp(m2 - m1)
    g1 = 1.0 / (1.0 + e2)
    g2 = e2 / (1.0 + e2)
    pick1 = lane == i1
    pick2 = lane == i2
    gate_ref[...] = jnp.where(pick1, g1, 0.0) + jnp.where(pick2, g2, 0.0)
    tm = logits.shape[0]
    sel = jnp.where(jnp.logical_or(pick1, pick2), 1.0, 0.0).astype(BF16)
    tr = _iota((tm, tm), 0)
    tc = _iota((tm, tm), 1)
    upper = jnp.where(tr <= tc, 1.0, 0.0).astype(BF16)
    eye = jnp.where(tr == tc, 1.0, 0.0).astype(BF16)
    rank_t = _dg(sel, upper, _TN)
    sel_t = _dg(sel, eye, _TN)
    pos_t = jnp.where(sel_t > 0.5, rank_t - 1.0, -1.0)
    pos_ref[...] = pos_t[0:SUBLANES, :]
    cnt_ref[0] = jnp.sum(sel.astype(F32), axis=0, keepdims=True)


def router(x, ya, yb, wa, wb, g, rw, rb, tm):
    m, d = x.shape
    n_exp = rw.shape[1]
    assert n_exp <= SUBLANES
    rw_pad = jnp.pad(rw, ((0, 0), (0, LANES - n_exp)))
    rb_pad = jnp.pad(rb, (0, LANES - n_exp)).reshape(1, LANES)
    tok = lambda n: pl.BlockSpec((tm, n), lambda i: (i, 0))
    full = lambda a: pl.BlockSpec(a.shape, lambda i: (0, 0))
    return pl.pallas_call(
        functools.partial(_router_kernel, n_exp),
        grid=(m // tm,),
        in_specs=[tok(d), tok(ya.shape[1]), tok(yb.shape[1]), full(wa), full(wb),
                  pl.BlockSpec((1, d), lambda i: (0, 0)),
                  pl.BlockSpec((d, LANES), lambda i: (0, 0)),
                  pl.BlockSpec((1, LANES), lambda i: (0, 0))],
        out_specs=[tok(d), tok(d), tok(LANES),
                   pl.BlockSpec((SUBLANES, tm), lambda i: (0, i)),
                   pl.BlockSpec((1, 1, LANES), lambda i: (i, 0, 0))],
        out_shape=[jax.ShapeDtypeStruct((m, d), F32),
                   jax.ShapeDtypeStruct((m, d), BF16), jax.ShapeDtypeStruct((m, LANES), F32),
                   jax.ShapeDtypeStruct((SUBLANES, m), F32),
                   jax.ShapeDtypeStruct((m // tm, 1, LANES), F32)],
        compiler_params=_cparams(("parallel",)), name="router",
    )(x, ya, yb, wa, wb, g.reshape(1, d), rw_pad, rb_pad)


MOE_ROWS = 128


def _moe_kernel(n_exp, cnt_ref, x_hbm, hn_ref, gate_ref, pos_ref, w1_ref, w3_ref, w2_ref, o_ref, xs_scr, y_scr):
    i = pl.program_id(0)
    e = pl.program_id(1)
    j = pl.program_id(2)
    nj = pl.num_programs(2)
    tm = hn_ref.shape[0]
    n_small = (cnt_ref[i * n_exp + e] + (MOE_ROWS - 1)) // MOE_ROWS

    @pl.when(jnp.logical_and(e == 0, j == 0))
    def _():
        pltpu.sync_copy(x_hbm.at[pl.ds(pl.multiple_of(i * tm, tm), tm), :], o_ref)

    def select(start, nrows):
        pos = pos_ref[pl.ds(e, 1), :]
        want = (start + _iota((nrows, tm), 0)).astype(F32)
        return jnp.where(pos == want, 1.0, 0.0).astype(BF16)

    def blocks(body):
        big = 4 * MOE_ROWS
        assert tm % big == 0

        def run_big(blk, carry):
            body(pl.multiple_of(blk * big, big), big)
            return carry
        lax.fori_loop(0, n_small // 4, run_big, 0)
        done = (n_small // 4) * big

        @pl.when(n_small % 4 >= 2)
        def _():
            body(pl.multiple_of(done, big), 2 * MOE_ROWS)

        @pl.when(n_small % 2 == 1)
        def _():
            body(pl.multiple_of(done + (n_small % 4 // 2) * 2 * MOE_ROWS, MOE_ROWS), MOE_ROWS)

    @pl.when(j == 0)
    def _():
        def gather(start, nrows):
            rows = pl.ds(start, nrows)
            xs_scr[rows, :] = _dg(select(start, nrows), hn_ref[...], _NN).astype(BF16)
            y_scr[rows, :] = jnp.zeros((nrows, y_scr.shape[1]), F32)
        blocks(gather)

    def expert(start, nrows):
        rows = pl.ds(start, nrows)
        xs = xs_scr[rows, :]
        h = _silu(_dg(xs, w1_ref[0], _NN)) * _dg(xs, w3_ref[0], _NN)
        y_scr[rows, :] += _dg(h.astype(BF16), w2_ref[0], _NN)
    blocks(expert)

    @pl.when(j == nj - 1)
    def _():
        gt = gate_ref[...]
        gcol = jnp.sum(jnp.where(_iota(gt.shape, 1) == e, gt, 0.0), axis=-1, keepdims=True)

        def scatter(start, nrows):
            o_ref[...] += gcol * _dg(select(start, nrows), y_scr[pl.ds(start, nrows), :].astype(BF16), _TN)
        blocks(scatter)


def moe(x, hn, gates, pos, counts, w1, w3, w2, tm_router, tm, fc):
    m, d = x.shape
    n_exp, _, dff = w1.shape
    nj = dff // fc
    ratio = tm // tm_router
    cnt = counts[:, 0, :n_exp].astype(jnp.int32).reshape(m // tm, ratio, n_exp)
    before = (jnp.cumsum(cnt, axis=1) - cnt).astype(F32)
    shift = jnp.repeat(before.reshape(m // tm_router, n_exp).T, tm_router, axis=1)
    shift = jnp.pad(shift, ((0, SUBLANES - n_exp), (0, 0)))
    pos = jnp.where(pos >= 0, pos + shift, pos)
    cnt = cnt.sum(axis=1).reshape(-1)
    grid_spec = pltpu.PrefetchScalarGridSpec(
        num_scalar_prefetch=1,
        grid=(m // tm, n_exp, nj),
        in_specs=[pl.BlockSpec(memory_space=pl.ANY),
                  pl.BlockSpec((tm, d), lambda i, e, j, c: (i, 0), pipeline_mode=pl.Buffered(1)),
                  pl.BlockSpec((tm, LANES), lambda i, e, j, c: (i, 0), pipeline_mode=pl.Buffered(1)),
                  pl.BlockSpec((SUBLANES, tm), lambda i, e, j, c: (0, i), pipeline_mode=pl.Buffered(1)),
                  pl.BlockSpec((1, d, fc), lambda i, e, j, c: (e, 0, j)),
                  pl.BlockSpec((1, d, fc), lambda i, e, j, c: (e, 0, j)),
                  pl.BlockSpec((1, fc, d), lambda i, e, j, c: (e, j, 0))],
        out_specs=pl.BlockSpec((tm, d), lambda i, e, j, c: (i, 0), pipeline_mode=pl.Buffered(1)),
        scratch_shapes=[pltpu.VMEM((tm, d), BF16), pltpu.VMEM((tm, d), F32)])
    return pl.pallas_call(
        functools.partial(_moe_kernel, n_exp),
        grid_spec=grid_spec,
        out_shape=jax.ShapeDtypeStruct((m, d), F32),
        compiler_params=_cparams(("parallel", "arbitrary", "arbitrary")), name="moe",
    )(cnt, x, hn, gates, pos, w1, w3, w2)


_RW_DOT_A = _bdot
_RW_DOT_T = _bdot
_RW_DOT_W = _bdot
_RW_DOT_S = _bdot


RW = 512
RSHIFT = 1792


def _rwkv_token_maps(xs, w0_ref, wup_ref, a0_ref, aup_ref, gup_ref, kk_ref, ka_ref, rk_ref):
    r = xs[:, 0:RW]
    k = xs[:, RW:2 * RW]
    v = xs[:, 2 * RW:3 * RW]
    lr = xs[:, 3 * RW:3 * RW + LANES]
    gd = xs[:, 3 * RW + LANES:3 * RW + 2 * LANES]
    w_pre = w0_ref[...] + _bdot(jnp.tanh(lr), wup_ref[...])
    logw = -jnp.exp(-_softplus(-w_pre) - 0.5)
    a = _sigmoid(a0_ref[...] + _bdot(lr, aup_ref[...]))
    g = _bdot(_sigmoid(gd), gup_ref[...])
    kkr = k * kk_ref[...]
    kk = kkr / jnp.maximum(jnp.sqrt(_seg_sum(kkr * kkr, RWKV_HD)), 1e-12)
    k2 = k * (1.0 + (a - 1.0) * ka_ref[...])
    bonus = _seg_sum(r * k2 * rk_ref[...], RWKV_HD) * v
    return r, logw, k2, v, -kk, kk * a, g, bonus


def _rwkv_kernel(L, nb, n_pairs, t_real, p_ref, prev0_ref, s0_ref, mu_ref, w0_ref, wup_ref, a0_ref, aup_ref,
                 gup_ref, kk_ref, ka_ref, rk_ref, lnw_ref, lnb_ref, y_ref, sf_ref, s_scr, prev_scr):
    c = pl.program_id(1)
    nc = pl.num_programs(1)
    L2 = 2 * L
    lane = _iota((1, LANES), 1)
    m0 = jnp.where(lane < RWKV_HD, 1.0, 0.0)
    m1 = 1.0 - m0
    rr = _iota((L2, L2), 0)
    cc = _iota((L2, L2), 1)
    same = (rr // L) == (cc // L)
    strict = jnp.logical_and(same, (cc % L) < (rr % L))
    incl = jnp.logical_and(same, (cc % L) <= (rr % L))
    eye2 = jnp.where(rr == cc, 1.0, 0.0)
    r128 = _iota((LANES, LANES), 0)
    c128 = _iota((LANES, LANES), 1)
    blk128 = (r128 // RWKV_HD) == (c128 // RWKV_HD)
    diag128 = r128 == c128
    fmat = jnp.where(_iota((LANES, RWKV_HD), 0) % RWKV_HD == _iota((LANES, RWKV_HD), 1), 1.0, 0.0)
    tri = _tri_incl(L)

    def bd(x):
        return jnp.concatenate([x * m0, x * m1], axis=0)

    @pl.when(c == 0)
    def _():
        for bb in range(nb):
            prev_scr[bb, 0:1, :] = prev0_ref[bb]
            for j in range(n_pairs):
                s0 = s0_ref[bb, j]
                st = _xdot_r(fmat, s0, _NT)
                s_scr[bb * n_pairs + j] = jnp.where(blk128, st, 0.0)

    rowi = _iota((L, 1), 0)
    xs = []
    for bb in range(nb):
        p = p_ref[bb]
        prev = jnp.where(rowi == 0, prev_scr[bb, 0:1, :], pltpu.roll(p, 1, 0))
        prev_scr[bb, 0:1, :] = p[L - 1:L, :]
        xs.append(p + (prev - p) * mu_ref[...])
    maps = _rwkv_token_maps(jnp.concatenate(xs, axis=0), w0_ref, wup_ref, a0_ref, aup_ref, gup_ref,
                            kk_ref, ka_ref, rk_ref)
    if t_real % L != 0:
        valid = (c * L + _iota((nb * L, 1), 0) % L) < t_real
        maps = tuple(jnp.where(valid, z, 0.0) for z in maps)
    toks = [tuple(z[bb * L:(bb + 1) * L] for z in maps) for bb in range(nb)]

    pairs = range(nb * n_pairs)
    bbs = [i // n_pairs for i in pairs]
    sls = [slice((i % n_pairs) * LANES, (i % n_pairs + 1) * LANES) for i in pairs]
    tok = lambda j, which: toks[bbs[j]][which][:, sls[j]]
    lw = [tok(j, 1) for j in pairs]
    b = [_xdot_r(tri, x) for x in lw]
    bl = [x[L - 1:L, :] for x in b]
    e_b = [jnp.exp(x) for x in b]
    e_nb = [jnp.exp(-x) for x in b]
    e_lb = [jnp.exp(bl[j] - b[j]) for j in pairs]
    at = [bd(tok(j, 4) * jnp.exp(b[j] - lw[j])) for j in pairs]
    rt = [bd(tok(j, 0) * e_b[j]) for j in pairs]
    bh = [bd(tok(j, 5) * e_nb[j]) for j in pairs]
    kh = [bd(tok(j, 2) * e_nb[j]) for j in pairs]
    bt = [bd(tok(j, 5) * e_lb[j]) for j in pairs]
    kt = [bd(tok(j, 2) * e_lb[j]) for j in pairs]
    vb = [bd(tok(j, 3)) for j in pairs]
    if L2 % LANES == 0:
        gq = [_RW_DOT_A(jnp.concatenate([at[j], rt[j]], axis=0), jnp.concatenate([bh[j], kh[j]], axis=0), _NT)
              for j in pairs]
        a_ab = [jnp.where(strict, g[0:L2, 0:L2], 0.0) for g in gq]
        a_ak = [jnp.where(strict, g[0:L2, L2:2 * L2], 0.0) for g in gq]
        a_rb = [jnp.where(incl, g[L2:2 * L2, 0:L2], 0.0) for g in gq]
        a_rk = [jnp.where(incl, g[L2:2 * L2, L2:2 * L2], 0.0) for g in gq]
    else:
        a_ab = [jnp.where(strict, _RW_DOT_A(at[j], bh[j], _NT), 0.0) for j in pairs]
        a_ak = [jnp.where(strict, _RW_DOT_A(at[j], kh[j], _NT), 0.0) for j in pairs]
        a_rb = [jnp.where(incl, _RW_DOT_A(rt[j], bh[j], _NT), 0.0) for j in pairs]
        a_rk = [jnp.where(incl, _RW_DOT_A(rt[j], kh[j], _NT), 0.0) for j in pairs]
    x = a_ab
    tinv = [eye2 + a for a in a_ab]
    span = 2
    while span < L:
        x = [_RW_DOT_T(xx, xx) for xx in x]
        tinv = [tinv[j] + _RW_DOT_T(tinv[j], x[j]) for j in pairs]
        span *= 2
    akv = [_RW_DOT_W(a_ak[j], vb[j]) for j in pairs]
    wuv = [_RW_DOT_W(tinv[j], jnp.concatenate([at[j], akv[j]], axis=1)) for j in pairs]
    mn = [_RW_DOT_W(bt[j], wuv[j], _TN) for j in pairs]
    qy = [_RW_DOT_W(a_rb[j], wuv[j]) for j in pairs]
    mm = [jnp.where(diag128, jnp.exp(bl[j]), 0.0) + mn[j][:, 0:LANES] for j in pairs]
    nn = [mn[j][:, LANES:2 * LANES] + _RW_DOT_W(kt[j], vb[j], _TN) for j in pairs]
    q = [rt[j] + qy[j][:, 0:LANES] for j in pairs]
    yv = [qy[j][:, LANES:2 * LANES] + _RW_DOT_W(a_rk[j], vb[j]) for j in pairs]
    s = [s_scr[j] for j in pairs]
    ybd = [_RW_DOT_S(q[j], s[j]) + yv[j] for j in pairs]
    for j in pairs:
        s_scr[j] = _RW_DOT_S(mm[j], s[j]) + nn[j]

    y = jnp.concatenate(
        [jnp.concatenate([ybd[bb * n_pairs + jj][0:L] + ybd[bb * n_pairs + jj][L:L2] for jj in range(n_pairs)],
                         axis=1) for bb in range(nb)], axis=0)
    mean = _seg_sum(y, RWKV_HD) * (1.0 / RWKV_HD)
    d = y - mean
    var = _seg_sum(d * d, RWKV_HD) * (1.0 / RWKV_HD)
    ya = (d * lax.rsqrt(var + RWKV_GN_EPS) * lnw_ref[...] + lnb_ref[...] + maps[7]) * maps[6]
    for bb in range(nb):
        y_ref[bb] = ya[bb * L:(bb + 1) * L]

    @pl.when(c == nc - 1)
    def _():
        for i in pairs:
            sf_ref[bbs[i], i % n_pairs] = _xdot_l(s_scr[i], fmat, _TN)


def rwkv_params(w, i):
    z64 = jnp.zeros((RWKV_HD, RW), F32)
    return {
        "mu": w["e_mu"][i].reshape(1, RSHIFT),
        "w0": w["e_w0"][i].reshape(1, RW),
        "wup": jnp.concatenate([w["e_w_up"][i], z64], 0).astype(BF16),
        "a0": w["e_a0"][i].reshape(1, RW),
        "aup": jnp.concatenate([z64, w["e_a_up"][i]], 0).astype(BF16),
        "gup": w["e_g_up"][i].astype(BF16),
        "k_k": w["e_k_k"][i].reshape(1, RW),
        "k_a": w["e_k_a"][i].reshape(1, RW),
        "r_k": w["e_r_k"][i].reshape(1, RW),
    }


def rwkv_mix(p_flat, prev, s0, bsz, t, L, prm, lnx_w, lnx_b, nb=1):
    n_pairs = RW // LANES
    tp = -(-t // L) * L
    p3 = p_flat.reshape(bsz, t, RSHIFT)
    if tp != t:
        p3 = jnp.pad(p3, ((0, 0), (0, tp - t), (0, 0)))
    row = lambda n: pl.BlockSpec((1, n), lambda b, c: (0, 0))
    lora = pl.BlockSpec((LANES, RW), lambda b, c: (0, 0))
    st_spec = pl.BlockSpec((nb, n_pairs, LANES, RWKV_HD), lambda b, c: (b, 0, 0, 0))
    y, s_fin = pl.pallas_call(
        functools.partial(_rwkv_kernel, L, nb, n_pairs, t),
        grid=(bsz // nb, tp // L),
        in_specs=[pl.BlockSpec((nb, L, RSHIFT), lambda b, c: (b, c, 0)),
                  pl.BlockSpec((nb, 1, RSHIFT), lambda b, c: (b, 0, 0)),
                  st_spec,
                  row(RSHIFT), row(RW), lora, row(RW), lora, lora, row(RW), row(RW), row(RW), row(RW), row(RW)],
        out_specs=[pl.BlockSpec((nb, L, RW), lambda b, c: (b, c, 0)), st_spec],
        out_shape=[jax.ShapeDtypeStruct((bsz, tp, RW), F32),
                   jax.ShapeDtypeStruct((bsz, n_pairs, LANES, RWKV_HD), F32)],
        scratch_shapes=[pltpu.VMEM((nb * n_pairs, LANES, LANES), F32),
                        pltpu.VMEM((nb, SUBLANES, RSHIFT), F32)],
        compiler_params=_cparams(("parallel", "arbitrary")), name="rwkv",
    )(p3, prev.reshape(bsz, 1, RSHIFT), s0.reshape(bsz, n_pairs, LANES, RWKV_HD),
      prm["mu"], prm["w0"], prm["wup"], prm["a0"], prm["aup"], prm["gup"], prm["k_k"], prm["k_a"], prm["r_k"],
      lnx_w.reshape(1, RW), lnx_b.reshape(1, RW))
    return y[:, :t].reshape(bsz * t, RW), s_fin.reshape(bsz, RW // RWKV_HD, RWKV_HD, RWKV_HD)


N_Q_HEADS = 8
N_KV_HEADS = 2
Q_PER_KV = N_Q_HEADS // N_KV_HEADS
QW = N_Q_HEADS * ATT_HD
KW = N_KV_HEADS * ATT_HD
NEG = -1e30


def _t5_bucket_np(dist):
    n = np.maximum(dist, 0)
    max_exact = N_BUCKETS // 2
    nf = np.maximum(n, 1).astype(np.float32)
    large = max_exact + (np.log(nf / np.float32(max_exact)) / np.float32(math.log(BUCKET_MAX_DIST / max_exact))
                         * np.float32(N_BUCKETS - max_exact)).astype(np.int32)
    large = np.minimum(large, N_BUCKETS - 1)
    return np.where(n < max_exact, n, large)


def _bias_kernel(rt_ref, oh_ref, o_ref):
    o_ref[...] = _xdot_l(rt_ref[...], oh_ref[...])


def rel_bias(rel_table, dist):
    bucket = _t5_bucket_np(dist).reshape(-1)
    n = bucket.shape[0]
    onehot = jnp.asarray((np.arange(N_BUCKETS)[:, None] == bucket[None, :]).astype(np.float32), BF16)
    out = pl.pallas_call(
        _bias_kernel,
        out_shape=jax.ShapeDtypeStruct((N_Q_HEADS, n), F32),
    )(rel_table.T, onehot)
    return out.reshape((N_Q_HEADS,) + dist.shape)


def _head_norm(x, w_row):
    return x * lax.rsqrt(_seg_sum(x * x, ATT_HD) * (1.0 / ATT_HD) + EPS) * w_row


def _swa_prompt_kernel(nqb, q_ref, kvc_ref, kvp_ref, qw_ref, kw_ref, bias_ref, sink_ref, o_ref, ko_ref, vo_ref):
    i = pl.program_id(1)
    qn = _head_norm(q_ref[0], qw_ref[...])
    kvc = kvc_ref[0]
    kvp = kvp_ref[0]
    kn = [_head_norm(kvp[:, 0:KW], kw_ref[...])]
    vs = [kvp[:, KW:2 * KW]]
    kcn = _head_norm(kvc[:, 0:KW], kw_ref[...])
    for j in range(nqb):
        kn.append(kcn[j * WINDOW:(j + 1) * WINDOW])
        vs.append(kvc[j * WINDOW:(j + 1) * WINDOW, KW:2 * KW])
    kcat = [jnp.concatenate([kn[j], kn[j + 1]], axis=0).astype(BF16) for j in range(nqb)]
    vcat = [jnp.concatenate([vs[j], vs[j + 1]], axis=0).astype(BF16) for j in range(nqb)]
    qi = _iota((WINDOW, 2 * WINDOW), 0)
    kj = _iota((WINDOW, 2 * WINDOW), 1)
    dist = qi + WINDOW - kj
    band = jnp.logical_and(dist >= 0, dist < WINDOW)
    valid = [jnp.logical_and(band, jnp.logical_or(kj >= WINDOW, i > 0))] + [band] * (nqb - 1)
    lane = _iota((1, LANES), 1)
    masks = (jnp.where(lane < ATT_HD, 1.0, 0.0), jnp.where(lane < ATT_HD, 0.0, 1.0))
    scale = ATT_HD ** -0.5
    heads = range(N_Q_HEADS)
    units = [(j, h) for j in range(nqb) for h in heads]
    kv_of = lambda h: h // Q_PER_KV
    qts = [qn[j * WINDOW:(j + 1) * WINDOW, (h // 2) * LANES:(h // 2 + 1) * LANES] for j, h in units]
    qts = [pltpu.roll(qts[u], ATT_HD, 1) if h % 2 != kv_of(h) else qts[u] for u, (j, h) in enumerate(units)]
    qms = [(qts[u] * masks[kv_of(h)]).astype(BF16) for u, (j, h) in enumerate(units)]
    s_all = [_dg(jnp.concatenate(qms[j * N_Q_HEADS:(j + 1) * N_Q_HEADS], axis=0), kcat[j], _NT) * scale
             for j in range(nqb)]
    logits = [jnp.where(valid[j], s_all[j][h * WINDOW:(h + 1) * WINDOW] + bias_ref[h], NEG) for j, h in units]
    sinks = [sink_ref[h:h + 1, 0:1] for j, h in units]
    mx = [jnp.maximum(jnp.max(logits[u], axis=-1, keepdims=True), sinks[u]) for u in range(len(units))]
    pr = [jnp.exp(logits[u] - mx[u]) for u in range(len(units))]
    den = [jnp.sum(pr[u], axis=-1, keepdims=True) + jnp.exp(sinks[u] - mx[u]) for u in range(len(units))]
    probs = [(pr[u] * (1.0 / den[u])).astype(BF16) for u in range(len(units))]
    o_all = [_dg(jnp.concatenate(probs[j * N_Q_HEADS:(j + 1) * N_Q_HEADS], axis=0), vcat[j], _NN)
             for j in range(nqb)]
    os_ = [o_all[j][h * WINDOW:(h + 1) * WINDOW] for j, h in units]
    os_ = [pltpu.roll(os_[u], ATT_HD, 1) if h % 2 != kv_of(h) else os_[u] for u, (j, h) in enumerate(units)]
    for j in range(nqb):
        for jq in range(QW // LANES):
            o_ref[0, j * WINDOW:(j + 1) * WINDOW, jq * LANES:(jq + 1) * LANES] = (
                os_[j * N_Q_HEADS + 2 * jq] * masks[0] + os_[j * N_Q_HEADS + 2 * jq + 1] * masks[1])
    ko_ref[0] = kn[nqb]
    vo_ref[0] = vs[nqb]


SWA_QB = 2


def swa_prompt(q, kv, bsz, t, q_norm, k_norm, bias, sinks):
    nqb = math.gcd(t // WINDOW, SWA_QB)
    rows = nqb * WINDOW
    nb = t // rows
    q3 = q.reshape(bsz, t, QW)
    kv3 = kv.reshape(bsz, t, 2 * KW)
    o, ko, vo = pl.pallas_call(
        functools.partial(_swa_prompt_kernel, nqb),
        grid=(bsz, nb),
        in_specs=[pl.BlockSpec((1, rows, QW), lambda b, i: (b, i, 0)),
                  pl.BlockSpec((1, rows, 2 * KW), lambda b, i: (b, i, 0)),
                  pl.BlockSpec((1, WINDOW, 2 * KW), lambda b, i: (b, jnp.maximum(i * nqb - 1, 0), 0)),
                  pl.BlockSpec((1, QW), lambda b, i: (0, 0)),
                  pl.BlockSpec((1, KW), lambda b, i: (0, 0)),
                  pl.BlockSpec((N_Q_HEADS, WINDOW, 2 * WINDOW), lambda b, i: (0, 0, 0)),
                  pl.BlockSpec((N_Q_HEADS, LANES), lambda b, i: (0, 0))],
        out_specs=[pl.BlockSpec((1, rows, QW), lambda b, i: (b, i, 0)),
                   pl.BlockSpec((1, WINDOW, KW), lambda b, i: (b, 0, 0)),
                   pl.BlockSpec((1, WINDOW, KW), lambda b, i: (b, 0, 0))],
        out_shape=[jax.ShapeDtypeStruct((bsz, t, QW), F32),
                   jax.ShapeDtypeStruct((bsz, WINDOW, KW), F32),
                   jax.ShapeDtypeStruct((bsz, WINDOW, KW), F32)],
        compiler_params=_cparams(("parallel", "arbitrary")), name="swa_prompt",
    )(q3, kv3, kv3, jnp.tile(q_norm, N_Q_HEADS).reshape(1, QW), jnp.tile(k_norm, N_KV_HEADS).reshape(1, KW),
      bias, jnp.broadcast_to(sinks[:, None], (N_Q_HEADS, LANES)))
    return o.reshape(bsz * t, QW), ko, vo


DEC_TP = 8


def _swa_decode_kernel(nbt, t_real, q_ref, kv_ref, ck_ref, cv_ref, qw_ref, kw_ref, bc_ref, bn_ref, sink_ref,
                       o_ref, ko_ref, vo_ref):
    rows = N_Q_HEADS * DEC_TP
    tq = _iota((rows, WINDOW), 0) % DEC_TP
    valid_c = _iota((rows, WINDOW), 1) > tq
    jn = _iota((rows, DEC_TP), 1)
    valid_n = jnp.logical_and(jn <= _iota((rows, DEC_TP), 0) % DEC_TP, jn < t_real)
    lane = _iota((1, LANES), 1)
    masks = (jnp.where(lane < ATT_HD, 1.0, 0.0), jnp.where(lane < ATT_HD, 0.0, 1.0))
    row8 = _iota((DEC_TP, 1), 0)
    scale = ATT_HD ** -0.5
    bias_c = jnp.concatenate([bc_ref[kv] for kv in range(N_KV_HEADS)], axis=0)
    bias_n = jnp.concatenate([bn_ref[kv][:, 0:DEC_TP] for kv in range(N_KV_HEADS)], axis=0)
    sink = jnp.concatenate([sink_ref[kv][:, 0:1] for kv in range(N_KV_HEADS)], axis=0)
    bs = range(nbt)
    heads = range(N_Q_HEADS)
    qn_all = _head_norm(q_ref[...].reshape(nbt * DEC_TP, QW), qw_ref[...])
    kvn_all = kv_ref[...].reshape(nbt * DEC_TP, 2 * KW)
    knew_all = _head_norm(kvn_all[:, 0:KW], kw_ref[...])
    knew = [knew_all[b * DEC_TP:(b + 1) * DEC_TP] for b in bs]
    vnew = [kvn_all[b * DEC_TP:(b + 1) * DEC_TP, KW:2 * KW] for b in bs]
    kc = [ck_ref[b] for b in bs]
    vc = [cv_ref[b] for b in bs]

    def stack_q(b):
        pieces = []
        for h in heads:
            qt = qn_all[b * DEC_TP:(b + 1) * DEC_TP, (h // 2) * LANES:(h // 2 + 1) * LANES]
            if h % 2 != h // Q_PER_KV:
                qt = pltpu.roll(qt, ATT_HD, 1)
            pieces.append(qt * masks[h // Q_PER_KV])
        return jnp.concatenate(pieces, axis=0)

    qs = [stack_q(b) for b in bs]
    l_c = [jnp.where(valid_c, _bdot(qs[b], kc[b], _NT) * scale + bias_c, NEG) for b in bs]
    l_n = [jnp.where(valid_n, _bdot(qs[b], knew[b], _NT) * scale + bias_n, NEG) for b in bs]
    mx = [jnp.maximum(jnp.maximum(jnp.max(l_c[b], axis=-1, keepdims=True),
                                  jnp.max(l_n[b], axis=-1, keepdims=True)), sink) for b in bs]
    p_c = [jnp.exp(l_c[b] - mx[b]) for b in bs]
    p_n = [jnp.exp(l_n[b] - mx[b]) for b in bs]
    inv = [1.0 / (jnp.sum(p_c[b], axis=-1, keepdims=True) + jnp.sum(p_n[b], axis=-1, keepdims=True)
                  + jnp.exp(sink - mx[b])) for b in bs]
    o = [_bdot(p_c[b] * inv[b], vc[b]) + _bdot(p_n[b] * inv[b], vnew[b]) for b in bs]
    for b in bs:
        for jq in range(QW // LANES):
            parts = []
            for h in (2 * jq, 2 * jq + 1):
                piece = o[b][h * DEC_TP:(h + 1) * DEC_TP]
                if h % 2 != h // Q_PER_KV:
                    piece = pltpu.roll(piece, ATT_HD, 1)
                parts.append(piece * masks[h % 2])
            o_ref[b, :, jq * LANES:(jq + 1) * LANES] = parts[0] + parts[1]
    for b in bs:
        for cache, new, out in ((kc[b], knew[b], ko_ref), (vc[b], vnew[b], vo_ref)):
            shifted = pltpu.roll(cache, WINDOW - t_real, 0)
            new_r = pltpu.roll(new, DEC_TP - t_real, 0)
            out[b, 0:WINDOW - DEC_TP] = shifted[0:WINDOW - DEC_TP]
            out[b, WINDOW - DEC_TP:WINDOW] = jnp.where(row8 >= DEC_TP - t_real, new_r,
                                                       shifted[WINDOW - DEC_TP:WINDOW])


def swa_decode(q, kv, cache_k, cache_v, bsz, t, q_norm, k_norm, rel_table, sinks, nbt):
    pad = ((0, 0), (0, DEC_TP - t), (0, 0))
    q3 = jnp.pad(q.reshape(bsz, t, QW), pad)
    kv3 = jnp.pad(kv.reshape(bsz, t, 2 * KW), pad)
    kpos = np.concatenate([np.arange(WINDOW) - WINDOW, np.arange(DEC_TP)])
    dist = np.arange(DEC_TP)[:, None] - kpos[None, :]
    bias = rel_bias(rel_table, dist)
    rows = Q_PER_KV * DEC_TP
    bias = bias.reshape(N_KV_HEADS, rows, WINDOW + DEC_TP)
    bias_c = bias[:, :, :WINDOW]
    bias_n = jnp.pad(bias[:, :, WINDOW:], ((0, 0), (0, 0), (0, LANES - DEC_TP)))
    sink_rows = jnp.broadcast_to(sinks.reshape(N_KV_HEADS, Q_PER_KV, 1, 1),
                                 (N_KV_HEADS, Q_PER_KV, DEC_TP, LANES)).reshape(N_KV_HEADS, rows, LANES)
    full3 = lambda shape: pl.BlockSpec(shape, lambda i: (0, 0, 0))
    o, ko, vo = pl.pallas_call(
        functools.partial(_swa_decode_kernel, nbt, t),
        grid=(bsz // nbt,),
        in_specs=[pl.BlockSpec((nbt, DEC_TP, QW), lambda i: (i, 0, 0)),
                  pl.BlockSpec((nbt, DEC_TP, 2 * KW), lambda i: (i, 0, 0)),
                  pl.BlockSpec((nbt, WINDOW, KW), lambda i: (i, 0, 0)),
                  pl.BlockSpec((nbt, WINDOW, KW), lambda i: (i, 0, 0)),
                  pl.BlockSpec((1, QW), lambda i: (0, 0)),
                  pl.BlockSpec((1, KW), lambda i: (0, 0)),
                  full3((N_KV_HEADS, rows, WINDOW)),
                  full3((N_KV_HEADS, rows, LANES)),
                  full3((N_KV_HEADS, rows, LANES))],
        out_specs=[pl.BlockSpec((nbt, DEC_TP, QW), lambda i: (i, 0, 0)),
                   pl.BlockSpec((nbt, WINDOW, KW), lambda i: (i, 0, 0)),
                   pl.BlockSpec((nbt, WINDOW, KW), lambda i: (i, 0, 0))],
        out_shape=[jax.ShapeDtypeStruct((bsz, DEC_TP, QW), F32),
                   jax.ShapeDtypeStruct((bsz, WINDOW, KW), F32),
                   jax.ShapeDtypeStruct((bsz, WINDOW, KW), F32)],
        compiler_params=_cparams(("parallel",)), name="swa_decode",
    )(q3, kv3, cache_k, cache_v, jnp.tile(q_norm, N_Q_HEADS).reshape(1, QW),
      jnp.tile(k_norm, N_KV_HEADS).reshape(1, KW), bias_c, bias_n, sink_rows)
    return o[:, :t].reshape(bsz * t, QW), ko, vo


S5_G = 32
S5_W = S5_G * S5_P
S5_CP = S5_CHUNK * S5_P
S5_PK = 2 * S5_N
S5_HW = S5_G * S5_PK


S5_QT = S5_W // LANES
S5_GT = LANES // S5_P
S5_XW = S5_CHUNK * LANES
S5_HQ = S5_GT * S5_PK


def _s5_group_maps(t_effs, a2, ldt, b1, b2, c1, c2):
    L = S5_CHUNK
    ar2 = a2[0:1, :]
    ai2 = a2[1:2, :]
    step = jnp.exp(ldt)
    mi = _iota((3 * SUBLANES, S5_PK), 0).astype(F32)
    mag = jnp.exp(mi * (step * ar2))
    ang = mi * (step * ai2)
    pwa = mag * jnp.cos(ang)
    pwb = mag * jnp.sin(ang)
    abr = pwa[1:2]
    abi = pwb[1:2]
    den = ar2 * ar2 + ai2 * ai2
    fa = ((abr - 1.0) * ar2 + abi * ai2) / den
    fb = (abi * ar2 - (abr - 1.0) * ai2) / den
    bp1 = b1 * fa + b2 * fb
    bp2 = b2 * fa - b1 * fb
    cpow = [c1 * pwa[m:m + 1] + c2 * pwb[m:m + 1] for m in range(L + 1)]
    kern_t = _hdot(bp1, jnp.concatenate(cpow[0:L], axis=0), _NT)
    sgn = jnp.where(_iota((1, S5_PK), 1) < S5_N, -1.0, 1.0)
    kbs, als = [], []
    for t_eff in t_effs:
        kbs.append([bp1 * pwa[max(t_eff - 1 - i, 0):max(t_eff - 1 - i, 0) + 1]
                    + bp2 * pwb[max(t_eff - 1 - i, 0):max(t_eff - 1 - i, 0) + 1] for i in range(L)])
        als.append(jnp.concatenate([pwa[t_eff:t_eff + 1], sgn * pwb[t_eff:t_eff + 1]], axis=0))
    return kern_t, kbs, als, cpow[1:L + 1]


def _s5_prep_kernel(t_effs, a_ref, ldt_ref, b1_ref, b2_ref, c1_ref, c2_ref, bd_ref, kc_ref, *rest):
    L = S5_CHUNK
    n_t = len(t_effs)
    kb_refs = rest[:n_t]
    al_refs = rest[n_t:2 * n_t]
    lane = _iota((S5_P, LANES), 1)
    for kb_ref in kb_refs:
        kb_ref[0] = jnp.zeros(kb_ref.shape[1:], kb_ref.dtype)
    bd_rows = [[] for _ in range(L)]
    kct_rows = [[] for _ in range(L)]
    for g in range(S5_GT):
        kern_t, kbs, als, kct = _s5_group_maps(t_effs, a_ref[g], ldt_ref[g], b1_ref[g], b2_ref[g],
                                               c1_ref[g], c2_ref[g])
        in_group = jnp.logical_and(lane >= g * S5_P, lane < (g + 1) * S5_P)
        for tau in range(L):
            shift = (g * S5_P - tau * S5_P) % S5_CP
            moved = pltpu.roll(kern_t, shift, 1) if shift else kern_t
            bd_rows[tau].append(jnp.where(in_group, moved[:, 0:LANES], 0.0))
        for k in range(n_t):
            for i in range(L):
                kb_refs[k][0, i * LANES + g * S5_P:i * LANES + (g + 1) * S5_P, g * S5_PK:(g + 1) * S5_PK] = (
                    kbs[k][i].astype(kb_refs[k].dtype))
            al_refs[k][g] = als[k]
        zl = jnp.zeros((S5_P, g * S5_PK), F32)
        zr = jnp.zeros((S5_P, (S5_GT - 1 - g) * S5_PK), F32)
        for t in range(L):
            parts = ([zl] if g else []) + [kct[t]] + ([zr] if g < S5_GT - 1 else [])
            kct_rows[t].append(jnp.concatenate(parts, axis=1))
    for tau in range(L):
        bd_ref[0, tau] = jnp.concatenate(bd_rows[tau], axis=0).astype(bd_ref.dtype)
    for t in range(L):
        blk_t = jnp.concatenate(kct_rows[t], axis=0)
        kc_ref[0, :, t * LANES:(t + 1) * LANES] = blk_t.T.astype(kc_ref.dtype)


def s5_prep(w, i, t_effs):
    dup = lambda z: jnp.concatenate([z, z], axis=-1)
    a = jnp.stack([dup(w["o_a_re"][i]), dup(w["o_a_im"][i])], axis=1)
    ldt = jnp.broadcast_to(w["o_log_dt"][i][:, None, None], (S5_G, 1, S5_PK))
    bt_re = jnp.swapaxes(w["o_b_re"][i], 1, 2)
    bt_im = jnp.swapaxes(w["o_b_im"][i], 1, 2)
    b1 = jnp.concatenate([bt_re, bt_im], -1)
    b2 = jnp.concatenate([-bt_im, bt_re], -1)
    c_re, c_im = w["o_c_re"][i], w["o_c_im"][i]
    c1 = jnp.concatenate([c_re, -c_im], -1)
    c2 = jnp.concatenate([-c_im, -c_re], -1)
    n_t = len(t_effs)
    g3 = lambda r, c: pl.BlockSpec((S5_GT, r, c), lambda q: (q, 0, 0))
    outs = pl.pallas_call(
        functools.partial(_s5_prep_kernel, tuple(t_effs)),
        grid=(S5_QT,),
        in_specs=[g3(2, S5_PK), g3(1, S5_PK), g3(S5_P, S5_PK), g3(S5_P, S5_PK), g3(S5_P, S5_PK), g3(S5_P, S5_PK)],
        out_specs=[pl.BlockSpec((1, S5_CHUNK, LANES, LANES), lambda q: (q, 0, 0, 0)),
                   pl.BlockSpec((1, S5_HQ, S5_XW), lambda q: (q, 0, 0))]
                  + [pl.BlockSpec((1, S5_XW, S5_HQ), lambda q: (q, 0, 0))] * n_t
                  + [g3(2, S5_PK)] * n_t,
        out_shape=[jax.ShapeDtypeStruct((S5_QT, S5_CHUNK, LANES, LANES), BF16),
                   jax.ShapeDtypeStruct((S5_QT, S5_HQ, S5_XW), BF16)]
                  + [jax.ShapeDtypeStruct((S5_QT, S5_XW, S5_HQ), BF16)] * n_t
                  + [jax.ShapeDtypeStruct((S5_G, 2, S5_PK), F32)] * n_t,
        compiler_params=_cparams(("parallel",)), name="s5_prep",
    )(a, ldt, b1, b2, c1, c2)
    bd, kc = outs[0], outs[1]
    mats = []
    for k in range(n_t):
        al = outs[2 + n_t + k]
        mats.append((bd, outs[2 + k], kc, al[:, 0, :].reshape(1, S5_HW), al[:, 1, :].reshape(1, S5_HW)))
    return mats


def _s5_e_kernel(u_ref, kb_ref, e_ref):
    e_ref[...] = _bdot(u_ref[0], kb_ref[0])


def _s5_swap(h):
    n = h.shape[-1]
    lane = _iota(h.shape, 1)
    return jnp.where(lane % S5_PK < S5_N, pltpu.roll(h, n - S5_N, 1), pltpu.roll(h, S5_N, 1))


def _s5_scan_kernel(bsz, cg, e_ref, h0_ref, ala_ref, alb_ref, hp_ref, hf_ref, h_scr):
    @pl.when(pl.program_id(0) == 0)
    def _():
        h_scr[...] = h0_ref[...]

    ala = ala_ref[...]
    alb = alb_ref[...]

    def body(c, hs):
        out = []
        for b in range(bsz):
            hp_ref[b, pl.ds(c, 1), :] = hs[b]
            out.append(ala * hs[b] + alb * _s5_swap(hs[b]) + e_ref[b, pl.ds(c, 1), :])
        return tuple(out)

    hs = lax.fori_loop(0, cg, body, tuple(h_scr[b:b + 1, :] for b in range(bsz)))
    for b in range(bsz):
        h_scr[b:b + 1, :] = hs[b]
        hf_ref[b:b + 1, :] = hs[b]


def _s5_step_kernel(e_ref, h0_ref, ala_ref, alb_ref, hf_ref):
    h = h0_ref[...]
    hf_ref[...] = ala_ref[...] * h + alb_ref[...] * _s5_swap(h) + e_ref[...]


def _s5_y_kernel(u_ref, hp_ref, bd_ref, kc_ref, y_ref, k_scr):
    @pl.when(pl.program_id(1) == 0)
    def _():
        zero = jnp.zeros((LANES, LANES), k_scr.dtype)
        for i in range(S5_CHUNK):
            for t in range(S5_CHUNK):
                k_scr[i * LANES:(i + 1) * LANES, t * LANES:(t + 1) * LANES] = bd_ref[0, t - i] if t >= i else zero

    y_ref[0] = _bdot(u_ref[0], k_scr[...]) + _bdot(hp_ref[...], kc_ref[0])


def s5_core(x, h0, bsz, mats):
    bd, kbbig, kcbig, ala, alb = mats
    rows = x.shape[1]
    nc = rows // bsz
    tr = math.gcd(rows, 512)
    e = pl.pallas_call(
        _s5_e_kernel,
        grid=(S5_QT, rows // tr),
        in_specs=[pl.BlockSpec((1, tr, S5_XW), lambda q, r: (q, r, 0)),
                  pl.BlockSpec((1, S5_XW, S5_HQ), lambda q, r: (q, 0, 0))],
        out_specs=pl.BlockSpec((tr, S5_HQ), lambda q, r: (r, q)),
        out_shape=jax.ShapeDtypeStruct((rows, S5_HW), F32),
        compiler_params=_cparams(("parallel", "parallel")), name="s5_e",
    )(x, kbbig)
    row = pl.BlockSpec((1, S5_HW), lambda i: (0, 0))
    if nc == 1:
        hp = h0
        hf = pl.pallas_call(
            _s5_step_kernel,
            out_shape=jax.ShapeDtypeStruct((bsz, S5_HW), F32), name="s5_step",
        )(e, h0, ala, alb)
    else:
        cg = math.gcd(nc, 64)
        hp, hf = pl.pallas_call(
            functools.partial(_s5_scan_kernel, bsz, cg),
            grid=(nc // cg,),
            in_specs=[pl.BlockSpec((bsz, cg, S5_HW), lambda i: (0, i, 0)),
                      pl.BlockSpec((bsz, S5_HW), lambda i: (0, 0)), row, row],
            out_specs=[pl.BlockSpec((bsz, cg, S5_HW), lambda i: (0, i, 0)),
                       pl.BlockSpec((bsz, S5_HW), lambda i: (0, 0))],
            out_shape=[jax.ShapeDtypeStruct((bsz, nc, S5_HW), F32),
                       jax.ShapeDtypeStruct((bsz, S5_HW), F32)],
            scratch_shapes=[pltpu.VMEM((bsz, S5_HW), F32)],
            compiler_params=_cparams(("arbitrary",)), name="s5_scan",
        )(e.reshape(bsz, nc, S5_HW), h0, ala, alb)
    y = pl.pallas_call(
        _s5_y_kernel,
        grid=(S5_QT, rows // tr),
        in_specs=[pl.BlockSpec((1, tr, S5_XW), lambda q, r: (q, r, 0)),
                  pl.BlockSpec((tr, S5_HQ), lambda q, r: (r, q)),
                  pl.BlockSpec((1, S5_CHUNK, LANES, LANES), lambda q, r: (q, 0, 0, 0)),
                  pl.BlockSpec((1, S5_HQ, S5_XW), lambda q, r: (q, 0, 0))],
        out_specs=pl.BlockSpec((1, tr, S5_XW), lambda q, r: (q, r, 0)),
        out_shape=jax.ShapeDtypeStruct((S5_QT, rows, S5_XW), F32),
        scratch_shapes=[pltpu.VMEM((S5_XW, S5_XW), BF16)],
        compiler_params=_cparams(("parallel", "arbitrary")), name="s5_y",
    )(x, hp.reshape(rows, S5_HW), bd, kcbig)
    return y, hf


def _s5_post_kernel(tpr, y_ref, u_ref, d_ref, gw_ref, gb_ref, o_ref, y_scr, u_scr):
    _from_chunk_rows(y_ref, y_scr, tpr)
    _from_chunk_rows(u_ref, u_scr, tpr)
    y = jnp.concatenate([y_scr[q] for q in range(S5_QT)], axis=1)
    u = jnp.concatenate([u_scr[q] for q in range(S5_QT)], axis=1)
    x = y + d_ref[...] * u
    z = 0.5 * x * (1.0 + jnp.tanh(math.sqrt(2.0 / math.pi) * (x + 0.044715 * (x * x * x))))
    o_ref[...] = z * _sigmoid(_dg(z.astype(BF16), gw_ref[...], _NN) + gb_ref[...])


def s5_post(y4, u4, m, tpr, d, glu_w_bf16, glu_b, tm):
    spec4 = pl.BlockSpec((S5_QT, tm // tpr, S5_XW), lambda i: (0, i, 0))
    row = pl.BlockSpec((1, S5_W), lambda i: (0, 0))
    return pl.pallas_call(
        functools.partial(_s5_post_kernel, tpr),
        grid=(m // tm,),
        in_specs=[spec4, spec4, row, pl.BlockSpec((S5_W, S5_W), lambda i: (0, 0)), row],
        out_specs=pl.BlockSpec((tm, S5_W), lambda i: (i, 0)),
        out_shape=jax.ShapeDtypeStruct((m, S5_W), F32),
        scratch_shapes=[pltpu.VMEM((S5_QT, tm, LANES), F32), pltpu.VMEM((S5_QT, tm, LANES), F32)],
        compiler_params=_cparams(("parallel",)), name="s5_post",
    )(y4, u4, d.reshape(1, S5_W), glu_w_bf16, glu_b.reshape(1, S5_W))


def s5_mix(u4, h_re, h_im, bsz, t, mats, d, glu_w_bf16, glu_b, tm):
    tpr = min(t, CHUNK_ROW_TOKENS)
    h0 = jnp.concatenate([h_re, h_im], axis=-1).reshape(bsz, S5_HW)
    y4, hf = s5_core(u4, h0, bsz, mats)
    out = s5_post(y4, u4, bsz * t, tpr, d, glu_w_bf16, glu_b, tm)
    hf = hf.reshape(bsz, S5_G, 2, S5_N)
    return out, hf[:, :, 0], hf[:, :, 1]


GLA_DK = 64
GLA_DV = 128
GLA_HEADS = 4
GLA_KW = GLA_HEADS * GLA_DK
GLA_VW = GLA_HEADS * GLA_DV
GLA_PW = 2 * GLA_KW + 2 * GLA_VW + LANES


def _gla_kernel(L, sub, t_real, nb, p_ref, aup_ref, ab_ref, nw_ref, s0_ref, y_ref, sf_ref, s_scr):
    c = pl.program_id(1)
    nc = pl.num_programs(1)
    n_pairs = GLA_KW // LANES

    @pl.when(c == 0)
    def _():
        for bb in range(nb):
            for j in range(n_pairs):
                s_scr[bb * n_pairs + j] = s0_ref[bb, j]

    lane = _iota((1, LANES), 1)
    masks = (jnp.where(lane < GLA_DK, 1.0, 0.0), jnp.where(lane < GLA_DK, 0.0, 1.0))
    incl = _iota((L, L), 1) <= _iota((L, L), 0)
    diag128 = _iota((LANES, LANES), 0) == _iota((LANES, LANES), 1)
    ones128 = jnp.ones((LANES, LANES), BF16)
    tri = _tri_incl(L)
    rows = range(nb)
    for part in range(sub):
        _gla_chunk(L, t_real, n_pairs, c * sub + part, slice(part * L, (part + 1) * L), rows, masks, incl,
                   diag128, ones128, tri, p_ref, aup_ref, ab_ref, nw_ref, y_ref, s_scr)

    @pl.when(c == nc - 1)
    def _():
        for bb in range(nb):
            for j in range(n_pairs):
                sf_ref[bb, j] = s_scr[bb * n_pairs + j]


def _gla_chunk(L, t_real, n_pairs, chunk, tsl, rows, masks, incl, diag128, ones128, tri,
               p_ref, aup_ref, ab_ref, nw_ref, y_ref, s_scr):
    ps = [p_ref[bb, tsl, :] for bb in rows]
    zs = [_bdot(p[:, 2 * GLA_KW + 2 * GLA_VW:], aup_ref[...]) + ab_ref[...] for p in ps]
    gk = [-_softplus(-z) * (1.0 / GLA_GATE_NORM) for z in zs]
    if t_real % L != 0:
        tok = chunk * L + _iota((L, 1), 0)
        gk = [jnp.where(tok < t_real, x, 0.0) for x in gk]
    b = [_xdot_r(tri, x) for x in gk]
    bl = [x[L - 1:L, :] for x in b]
    qd = [ps[i][:, 0:GLA_KW] * (GLA_DK ** -0.5) * jnp.exp(b[i]) for i in rows]
    kh = [ps[i][:, GLA_KW:2 * GLA_KW] * jnp.exp(-b[i]) for i in rows]
    kt = [ps[i][:, GLA_KW:2 * GLA_KW] * jnp.exp(bl[i] - b[i]) for i in rows]
    heads = [(bb, h) for bb in rows for h in range(GLA_HEADS)]
    sl = lambda h: slice((h // 2) * LANES, (h // 2 + 1) * LANES)
    hs = lambda h: slice(2 * GLA_KW + h * GLA_DV, 2 * GLA_KW + (h + 1) * GLA_DV)
    gs = lambda h: slice(2 * GLA_KW + GLA_VW + h * GLA_DV, 2 * GLA_KW + GLA_VW + (h + 1) * GLA_DV)
    st = [s_scr[bb * n_pairs + j] for bb in rows for j in range(n_pairs)]
    qm = [qd[bb][:, sl(h)] * masks[h % 2] for bb, h in heads]
    vh = [ps[bb][:, hs(h)] for bb, h in heads]
    attn = [jnp.where(incl, _bdot(qm[i], kh[bb][:, sl(h)], _NT), 0.0) for i, (bb, h) in enumerate(heads)]
    o = [_bdot(attn[i], vh[i]) + _bdot(qm[i], st[bb * n_pairs + h // 2]) for i, (bb, h) in enumerate(heads)]
    kv = [_bdot(kt[bb][:, sl(h)], vh[i], _TN) for i, (bb, h) in enumerate(heads)]
    for i, (bb, h) in enumerate(heads):
        of = o[i] * lax.rsqrt(jnp.mean(o[i] * o[i], axis=-1, keepdims=True) + EPS) * nw_ref[...]
        y_ref[bb, tsl, h * GLA_DV:(h + 1) * GLA_DV] = of * _silu(ps[bb][:, gs(h)])
    for bb in rows:
        for j in range(n_pairs):
            i0 = bb * GLA_HEADS + 2 * j
            pcol = _xdot_l(jnp.where(diag128, jnp.exp(bl[bb][:, j * LANES:(j + 1) * LANES]), 0.0), ones128)
            s_scr[bb * n_pairs + j] = pcol * st[bb * n_pairs + j] + jnp.concatenate(
                [kv[i0][0:GLA_DK], kv[i0 + 1][GLA_DK:2 * GLA_DK]], axis=0)


GLA_SUB = 4


def gla_mix(p_gla, s0, bsz, t, L, aup_pad, a_b, norm_w, nb=1):
    sub = math.gcd(-(-t // L), GLA_SUB)
    step = sub * L
    tp = -(-t // L) * L
    p3 = p_gla.reshape(bsz, t, GLA_PW)
    if tp != t:
        p3 = jnp.pad(p3, ((0, 0), (0, tp - t), (0, 0)))
    n_pairs = GLA_KW // LANES
    st_spec = pl.BlockSpec((nb, n_pairs, LANES, LANES), lambda b, c: (b, 0, 0, 0))
    y, s_fin = pl.pallas_call(
        functools.partial(_gla_kernel, L, sub, t, nb),
        grid=(bsz // nb, tp // step),
        in_specs=[pl.BlockSpec((nb, step, GLA_PW), lambda b, c: (b, c, 0)),
                  pl.BlockSpec((LANES, GLA_KW), lambda b, c: (0, 0)),
                  pl.BlockSpec((1, GLA_KW), lambda b, c: (0, 0)),
                  pl.BlockSpec((1, GLA_DV), lambda b, c: (0, 0)),
                  st_spec],
        out_specs=[pl.BlockSpec((nb, step, GLA_VW), lambda b, c: (b, c, 0)), st_spec],
        out_shape=[jax.ShapeDtypeStruct((bsz, tp, GLA_VW), F32),
                   jax.ShapeDtypeStruct((bsz, n_pairs, LANES, LANES), F32)],
        scratch_shapes=[pltpu.VMEM((nb * n_pairs, LANES, LANES), F32)],
        compiler_params=_cparams(("parallel", "arbitrary")), name="gla",
    )(p3, aup_pad, a_b.reshape(1, GLA_KW), norm_w.reshape(1, GLA_DV),
      s0.reshape(bsz, n_pairs, LANES, LANES))
    return y[:, :t].reshape(bsz * t, GLA_VW), s_fin.reshape(bsz, GLA_HEADS, GLA_DK, GLA_DV)


GLA_LR = 16
D_FF_CHUNK = 2816
D_FF_EXPERT_CHUNK = 1792
FFN_TOKENS = 512
ROUTER_TOKENS = 1024
MOE_TOKENS = 2048


def _prepare_weights(w):
    win = w["e_w_in"][0]
    w_gla = jnp.pad(win[:, RSHIFT:], ((0, 0), (0, LANES - GLA_LR)))
    wo = w["o_w_in"][0]
    return {
        "e_w_rwkv": win[:, :RSHIFT].astype(BF16),
        "e_w_gla": w_gla.astype(BF16),
        "rwkv": rwkv_params(w, 0),
        "gla_aup": jnp.pad(w["e_gla_a_up"][0], ((0, LANES - GLA_LR), (0, 0))).astype(BF16),
        "e_wo_a": w["e_w_out"][0][:RW].astype(BF16),
        "e_wo_b": w["e_w_out"][0][RW:].astype(BF16),
        "ff_w1": w["e_ff_w1"][0].astype(BF16),
        "ff_w3": w["e_ff_w3"][0].astype(BF16),
        "ff_w2": w["e_ff_w2"][0].astype(BF16),
        "o_w_q": wo[:, :QW].astype(BF16),
        "o_w_kv": wo[:, QW:QW + 2 * KW].astype(BF16),
        "o_w_u": wo[:, QW + 2 * KW:].astype(BF16),
        "glu_w": w["o_glu_w"][0].astype(BF16),
        "o_wo_a": w["o_w_out"][0][:QW].astype(BF16),
        "o_wo_b": w["o_w_out"][0][QW:].astype(BF16),
        "moe_w1": w["o_moe_w1"][0].astype(BF16),
        "moe_w3": w["o_moe_w3"][0].astype(BF16),
        "moe_w2": w["o_moe_w2"][0].astype(BF16),
    }


def _trunk(x3, st, w, pw, tm, chunk, nb, s5_prep_t, prompt_bias):
    bsz, t, d = x3.shape
    m = bsz * t
    x = x3.reshape(m, d)
    p_r, p_g = norm_proj(x, w["e_norm1"][0], [pw["e_w_rwkv"], pw["e_w_gla"]], tm)
    ya, s_rwkv = rwkv_mix(p_r, st["shift"], st["rwkv"], bsz, t, chunk, pw["rwkv"],
                          w["e_lnx_w"][0], w["e_lnx_b"][0], nb)
    s_shift = p_r.reshape(bsz, t, RSHIFT)[:, -1]
    yb, s_gla = gla_mix(p_g, st["gla"], bsz, t, chunk, pw["gla_aup"], w["e_gla_a_b"][0], w["e_gla_norm"][0], nb)
    x = ffn(x, ya, yb, pw["e_wo_a"], pw["e_wo_b"], w["e_norm2"][0], pw["ff_w1"], pw["ff_w3"], pw["ff_w2"],
            math.gcd(m, FFN_TOKENS), D_FF_CHUNK)
    q, kv, u = norm_proj_tiles(x, w["o_norm1"][0], [pw["o_w_q"], pw["o_w_kv"], pw["o_w_u"]], tm,
                               min(t, CHUNK_ROW_TOKENS))
    if st["win_k"] is None:
        yc, nk, nv = swa_prompt(q, kv, bsz, t, w["o_q_norm"][0], w["o_k_norm"][0], prompt_bias, w["o_sinks"][0])
    else:
        yc, nk, nv = swa_decode(q, kv, st["win_k"].reshape(bsz, WINDOW, KW), st["win_v"].reshape(bsz, WINDOW, KW),
                                bsz, t, w["o_q_norm"][0], w["o_k_norm"][0], w["rel_table"], w["o_sinks"][0], 8)
    yd, s5r, s5i = s5_mix(u, st["s5_re"], st["s5_im"], bsz, t, s5_prep_t, w["o_d"][0], pw["glu_w"],
                          w["o_glu_b"][0], tm)
    t_router = min(m, ROUTER_TOKENS)
    x, hn, gates, pos, counts = router(x, yc, yd, pw["o_wo_a"], pw["o_wo_b"], w["o_norm2"][0],
                                       w["o_router_w"][0], w["o_router_b"][0], t_router)
    x = moe(x, hn, gates, pos, counts, pw["moe_w1"], pw["moe_w3"], pw["moe_w2"], t_router, min(m, MOE_TOKENS),
            D_FF_EXPERT_CHUNK)
    kv_shape = (bsz, WINDOW, N_KV_HEADS, ATT_HD)
    return (x.reshape(bsz, t, d), s_rwkv[None], s_shift[None], s_gla[None], nk.reshape(kv_shape)[None],
            nv.reshape(kv_shape)[None], s5r[None], s5i[None])


def kernel(x_prompt, x_sample, state_rwkv, state_shift, state_gla, cache_win_k, cache_win_v, state_s5_re,
           state_s5_im, rel_table, e_norm1, e_w_in, e_mu, e_w0, e_w_up, e_a0, e_a_up, e_g_up, e_k_k, e_k_a, e_r_k,
           e_lnx_w, e_lnx_b, e_gla_a_up, e_gla_a_b, e_gla_norm, e_w_out, e_norm2, e_ff_w1, e_ff_w3, e_ff_w2,
           o_norm1, o_w_in, o_q_norm, o_k_norm, o_sinks, o_a_re, o_a_im, o_log_dt, o_b_re, o_b_im, o_c_re, o_c_im,
           o_d, o_glu_w, o_glu_b, o_w_out, o_norm2, o_router_w, o_router_b, o_moe_w1, o_moe_w3, o_moe_w2):
    w = dict(rel_table=rel_table, e_norm1=e_norm1, e_w_in=e_w_in, e_mu=e_mu, e_w0=e_w0, e_w_up=e_w_up, e_a0=e_a0,
             e_a_up=e_a_up, e_g_up=e_g_up, e_k_k=e_k_k, e_k_a=e_k_a, e_r_k=e_r_k, e_lnx_w=e_lnx_w, e_lnx_b=e_lnx_b,
             e_gla_a_up=e_gla_a_up, e_gla_a_b=e_gla_a_b, e_gla_norm=e_gla_norm, e_w_out=e_w_out, e_norm2=e_norm2,
             e_ff_w1=e_ff_w1, e_ff_w3=e_ff_w3, e_ff_w2=e_ff_w2, o_norm1=o_norm1, o_w_in=o_w_in, o_q_norm=o_q_norm,
             o_k_norm=o_k_norm, o_sinks=o_sinks, o_a_re=o_a_re, o_a_im=o_a_im, o_log_dt=o_log_dt, o_b_re=o_b_re,
             o_b_im=o_b_im, o_c_re=o_c_re, o_c_im=o_c_im, o_d=o_d, o_glu_w=o_glu_w, o_glu_b=o_glu_b,
             o_w_out=o_w_out, o_norm2=o_norm2, o_router_w=o_router_w, o_router_b=o_router_b, o_moe_w1=o_moe_w1,
             o_moe_w3=o_moe_w3, o_moe_w2=o_moe_w2)
    pw = _prepare_weights(w)
    bp, tp, _ = x_prompt.shape
    bs, ts, _ = x_sample.shape
    qi = np.arange(WINDOW)[:, None]
    kj = np.arange(2 * WINDOW)[None, :]
    prompt_bias = rel_bias(rel_table, qi + WINDOW - kj)
    zeros = lambda *shape: jnp.zeros(shape, F32)
    st_p = {"rwkv": zeros(bp, RW // RWKV_HD, RWKV_HD, RWKV_HD), "shift": zeros(bp, RSHIFT),
            "gla": zeros(bp, GLA_HEADS, GLA_DK, GLA_DV), "win_k": None, "win_v": None,
            "s5_re": zeros(bp, S5_G, S5_N), "s5_im": zeros(bp, S5_G, S5_N)}
    st_s = {"rwkv": state_rwkv[0], "shift": state_shift[0], "gla": state_gla[0], "win_k": cache_win_k[0],
            "win_v": cache_win_v[0], "s5_re": state_s5_re[0], "s5_im": state_s5_im[0]}
    s5_p, s5_s = s5_prep(w, 0, (S5_CHUNK, ts))
    out_p = _trunk(x_prompt, st_p, w, pw, 512, 64, math.gcd(bp, 2), s5_p, prompt_bias)
    out_s = _trunk(x_sample, st_s, w, pw, bs * ts, 8, math.gcd(bs, 16), s5_s, None)
    res = [out_p[0], out_s[0]]
    for a, b in zip(out_p[1:], out_s[1:]):
        res += [a, b]
    return tuple(res)
```

```python
import functools
import math

import jax
import jax.numpy as jnp
import numpy as np
from jax import lax
from jax.experimental import pallas as pl
from jax.experimental.pallas import tpu as pltpu

F32 = jnp.float32
BF16 = jnp.bfloat16

LANES = 128
SUBLANES = 8
VMEM_LIMIT_BYTES = 56 * 1024 * 1024

EPS = 1e-6
RWKV_HD = 64
RWKV_GN_EPS = 64e-5
GLA_GATE_NORM = 16.0
ATT_HD = 64
WINDOW = 128
N_BUCKETS = 32
BUCKET_MAX_DIST = 128
S5_P = 16
S5_N = 64
S5_CHUNK = 16


def _cparams(sem):
    return pltpu.CompilerParams(dimension_semantics=sem, vmem_limit_bytes=VMEM_LIMIT_BYTES)


_NN = (((1,), (0,)), ((), ()))
_NT = (((1,), (1,)), ((), ()))
_TN = (((0,), (0,)), ((), ()))


def _dg(a, b, dims):
    return lax.dot_general(a, b, dims, preferred_element_type=F32)


def _bdot(a, b, dims=_NN):
    return _dg(a.astype(BF16), b.astype(BF16), dims)


def _split(a, n):
    terms = []
    r = a
    for _ in range(n):
        t = r.astype(BF16)
        terms.append(t)
        r = r - t.astype(F32)
    return terms


def _hdot(a, b, dims=_NN):
    a0, a1 = _split(a, 2)
    b0, b1 = _split(b, 2)
    return _dg(a0, b0, dims) + (_dg(a0, b1, dims) + _dg(a1, b0, dims))


def _xdot_l(a, e, dims=_NN):
    e = e.astype(BF16)
    a0, a1, a2 = _split(a, 3)
    return _dg(a0, e, dims) + (_dg(a1, e, dims) + _dg(a2, e, dims))


def _xdot_r(e, b, dims=_NN):
    e = e.astype(BF16)
    b0, b1, b2 = _split(b, 3)
    return _dg(e, b0, dims) + (_dg(e, b1, dims) + _dg(e, b2, dims))


def _iota(shape, axis):
    return lax.broadcasted_iota(jnp.int32, shape, axis)


def _seg_ones(n, seg):
    r = _iota((n, n), 0) // seg
    c = _iota((n, n), 1) // seg
    return jnp.where(r == c, 1.0, 0.0).astype(BF16)


def _seg_sum(x, seg):
    n = x.shape[-1]
    ones = _seg_ones(n, seg)
    x0, x1 = _split(x, 2)
    return _dg(x0, ones, _NN) + _dg(x1, ones, _NN)


def _sigmoid(x):
    return 1.0 / (1.0 + jnp.exp(-x))


def _silu(x):
    return x * _sigmoid(x)


def _softplus(x):
    return jnp.maximum(x, 0.0) + jnp.log(1.0 + jnp.exp(-jnp.abs(x)))


def _tri_incl(n):
    r = _iota((n, n), 0)
    c = _iota((n, n), 1)
    return jnp.where(c <= r, 1.0, 0.0).astype(BF16)


def _rms(x, g):
    return x * lax.rsqrt(jnp.mean(x * x, axis=-1, keepdims=True) + EPS) * g


def _norm_proj_kernel(n_w, x_ref, g_ref, *refs):
    xn = _rms(x_ref[...], g_ref[...]).astype(BF16)
    for w_ref, o_ref in zip(refs[:n_w], refs[n_w:]):
        o_ref[...] = _dg(xn, w_ref[...], _NN)


def norm_proj(x, g, ws_bf16, tm):
    m, d = x.shape
    return pl.pallas_call(
        functools.partial(_norm_proj_kernel, len(ws_bf16)),
        grid=(m // tm,),
        in_specs=[pl.BlockSpec((tm, d), lambda i: (i, 0)),
                  pl.BlockSpec((1, d), lambda i: (0, 0))]
                 + [pl.BlockSpec(w.shape, lambda i: (0, 0)) for w in ws_bf16],
        out_specs=[pl.BlockSpec((tm, w.shape[1]), lambda i: (i, 0)) for w in ws_bf16],
        out_shape=[jax.ShapeDtypeStruct((m, w.shape[1]), F32) for w in ws_bf16],
        compiler_params=_cparams(("parallel",)), name="norm_proj",
    )(x, g.reshape(1, d), *ws_bf16)


CHUNK_ROW_TOKENS = 16


def _to_chunk_rows(src_ref, dst_ref, tpr):
    rows = src_ref.shape[1] // tpr
    for q in range(dst_ref.shape[0]):
        for i in range(CHUNK_ROW_TOKENS):
            cols = slice(i * LANES, (i + 1) * LANES)
            if i < tpr:
                dst_ref[q, :, cols] = src_ref[q, pl.ds(i, rows, stride=tpr), :]
            else:
                dst_ref[q, :, cols] = jnp.zeros((rows, LANES), dst_ref.dtype)


def _from_chunk_rows(src_ref, dst_ref, tpr):
    rows = dst_ref.shape[1] // tpr
    for q in range(src_ref.shape[0]):
        for i in range(tpr):
            dst_ref[q, pl.ds(i, rows, stride=tpr), :] = src_ref[q, :, i * LANES:(i + 1) * LANES]


def _norm_proj_tiles_kernel(n_w, tpr, x_ref, g_ref, *refs):
    xn = _rms(x_ref[...], g_ref[...]).astype(BF16)
    for w_ref, o_ref in zip(refs[:n_w - 1], refs[n_w:]):
        o_ref[...] = _dg(xn, w_ref[...], _NN)
    u_ref, u_scr = refs[-2], refs[-1]
    u = _dg(xn, refs[n_w - 1][...], _NN)
    for q in range(u_scr.shape[0]):
        u_scr[q] = u[:, q * LANES:(q + 1) * LANES]
    _to_chunk_rows(u_scr, u_ref, tpr)


def norm_proj_tiles(x, g, ws_bf16, tm, tpr):
    m, d = x.shape
    nu = ws_bf16[-1].shape[1]
    nt = nu // LANES
    cw = CHUNK_ROW_TOKENS * LANES
    return pl.pallas_call(
        functools.partial(_norm_proj_tiles_kernel, len(ws_bf16), tpr),
        grid=(m // tm,),
        in_specs=[pl.BlockSpec((tm, d), lambda i: (i, 0)),
                  pl.BlockSpec((1, d), lambda i: (0, 0))]
                 + [pl.BlockSpec(w.shape, lambda i: (0, 0)) for w in ws_bf16],
        out_specs=[pl.BlockSpec((tm, w.shape[1]), lambda i: (i, 0)) for w in ws_bf16[:-1]]
                  + [pl.BlockSpec((nt, tm // tpr, cw), lambda i: (0, i, 0))],
        out_shape=[jax.ShapeDtypeStruct((m, w.shape[1]), F32) for w in ws_bf16[:-1]]
                  + [jax.ShapeDtypeStruct((nt, m // tpr, cw), F32)],
        scratch_shapes=[pltpu.VMEM((nt, tm, LANES), F32)],
        compiler_params=_cparams(("parallel",)), name="norm_proj_tiles",
    )(x, g.reshape(1, d), *ws_bf16)


def _mixer_out(x_ref, ya_ref, yb_ref, wa_ref, wb_ref):
    return (x_ref[...] + _dg(ya_ref[...].astype(BF16), wa_ref[...], _NN)
            + _dg(yb_ref[...].astype(BF16), wb_ref[...], _NN))


def _ffn_kernel(x_ref, ya_ref, yb_ref, wa_ref, wb_ref, g_ref, w1_ref, w3_ref, w2_ref, o_ref, xn_scr):
    j = pl.program_id(1)

    @pl.when(j == 0)
    def _():
        x = _mixer_out(x_ref, ya_ref, yb_ref, wa_ref, wb_ref)
        xn_scr[...] = _rms(x, g_ref[...]).astype(BF16)
        o_ref[...] = x

    xn = xn_scr[...]
    h = _silu(_dg(xn, w1_ref[...], _NN)) * _dg(xn, w3_ref[...], _NN)
    o_ref[...] += _dg(h.astype(BF16), w2_ref[...], _NN)


def ffn(x, ya, yb, wa, wb, g, w1, w3, w2, tm, fc):
    m, d = x.shape
    dff = w1.shape[1]
    tok = lambda n: pl.BlockSpec((tm, n), lambda i, j: (i, 0))
    full = lambda a: pl.BlockSpec(a.shape, lambda i, j: (0, 0))
    wmode = pl.Buffered(1) if dff == fc else None
    return pl.pallas_call(
        _ffn_kernel,
        grid=(m // tm, dff // fc),
        in_specs=[tok(d), tok(ya.shape[1]), tok(yb.shape[1]), full(wa), full(wb),
                  pl.BlockSpec((1, d), lambda i, j: (0, 0)),
                  pl.BlockSpec((d, fc), lambda i, j: (0, j), pipeline_mode=wmode),
                  pl.BlockSpec((d, fc), lambda i, j: (0, j), pipeline_mode=wmode),
                  pl.BlockSpec((fc, d), lambda i, j: (j, 0), pipeline_mode=wmode)],
        out_specs=pl.BlockSpec((tm, d), lambda i, j: (i, 0)),
        out_shape=jax.ShapeDtypeStruct((m, d), F32),
        scratch_shapes=[pltpu.VMEM((tm, d), BF16)],
        compiler_params=_cparams(("parallel", "arbitrary")), name="ffn",
    )(x, ya, yb, wa, wb, g.reshape(1, d), w1, w3, w2)


def _router_kernel(n_exp, x_ref, ya_ref, yb_ref, wa_ref, wb_ref, g_ref, rw_ref, rb_ref,
                   xo_ref, hn_ref, gate_ref, pos_ref, cnt_ref):
    x = _mixer_out(x_ref, ya_ref, yb_ref, wa_ref, wb_ref)
    xo_ref[...] = x
    xn = _rms(x, g_ref[...])
    hn_ref[...] = xn.astype(BF16)
    logits = _hdot(xn, rw_ref[...]) + rb_ref[...]
    lane = _iota(logits.shape, 1)
    logits = jnp.where(lane < n_exp, logits, -jnp.inf)
    m1 = jnp.max(logits, axis=-1, keepdims=True)
    i1 = jnp.min(jnp.where(logits == m1, lane, LANES), axis=-1, keepdims=True)
    rest = jnp.where(lane == i1, -jnp.inf, logits)
    m2 = jnp.max(rest, axis=-1, keepdims=True)
    i2 = jnp.min(jnp.where(rest == m2, lane, LANES), axis=-1, keepdims=True)
    e2 = jnp.exp(m2 - m1)
    g1 = 1.0 / (1.0 + e2)
    g2 = e2 / (1.0 + e2)
    pick1 = lane == i1
    pick2 = lane == i2
    gate_ref[...] = jnp.where(pick1, g1, 0.0) + jnp.where(pick2, g2, 0.0)
    tm = logits.shape[0]
    sel = jnp.where(jnp.logical_or(pick1, pick2), 1.0, 0.0).astype(BF16)
    tr = _iota((tm, tm), 0)
    tc = _iota((tm, tm), 1)
    upper = jnp.where(tr <= tc, 1.0, 0.0).astype(BF16)
    eye = jnp.where(tr == tc, 1.0, 0.0).astype(BF16)
    rank_t = _dg(sel, upper, _TN)
    sel_t = _dg(sel, eye, _TN)
    pos_t = jnp.where(sel_t > 0.5, rank_t - 1.0, -1.0)
    pos_ref[...] = pos_t[0:SUBLANES, :]
    cnt_ref[0] = jnp.sum(sel.astype(F32), axis=0, keepdims=True)


def router(x, ya, yb, wa, wb, g, rw, rb, tm):
    m, d = x.shape
    n_exp = rw.shape[1]
    assert n_exp <= SUBLANES
    rw_pad = jnp.pad(rw, ((0, 0), (0, LANES - n_exp)))
    rb_pad = jnp.pad(rb, (0, LANES - n_exp)).reshape(1, LANES)
    tok = lambda n: pl.BlockSpec((tm, n), lambda i: (i, 0))
    full = lambda a: pl.BlockSpec(a.shape, lambda i: (0, 0))
    return pl.pallas_call(
        functools.partial(_router_kernel, n_exp),
        grid=(m // tm,),
        in_specs=[tok(d), tok(ya.shape[1]), tok(yb.shape[1]), full(wa), full(wb),
                  pl.BlockSpec((1, d), lambda i: (0, 0)),
                  pl.BlockSpec((d, LANES), lambda i: (0, 0)),
                  pl.BlockSpec((1, LANES), lambda i: (0, 0))],
        out_specs=[tok(d), tok(d), tok(LANES),
                   pl.BlockSpec((SUBLANES, tm), lambda i: (0, i)),
                   pl.BlockSpec((1, 1, LANES), lambda i: (i, 0, 0))],
        out_shape=[jax.ShapeDtypeStruct((m, d), F32),
                   jax.ShapeDtypeStruct((m, d), BF16), jax.ShapeDtypeStruct((m, LANES), F32),
                   jax.ShapeDtypeStruct((SUBLANES, m), F32),
                   jax.ShapeDtypeStruct((m // tm, 1, LANES), F32)],
        compiler_params=_cparams(("parallel",)), name="router",
    )(x, ya, yb, wa, wb, g.reshape(1, d), rw_pad, rb_pad)


MOE_ROWS = 128


def _moe_kernel(n_exp, cnt_ref, x_hbm, hn_ref, gate_ref, pos_ref, w1_ref, w3_ref, w2_ref, o_ref, xs_scr, y_scr):
    i = pl.program_id(0)
    e = pl.program_id(1)
    j = pl.program_id(2)
    nj = pl.num_programs(2)
    tm = hn_ref.shape[0]
    n_small = (cnt_ref[i * n_exp + e] + (MOE_ROWS - 1)) // MOE_ROWS

    @pl.when(jnp.logical_and(e == 0, j == 0))
    def _():
        pltpu.sync_copy(x_hbm.at[pl.ds(pl.multiple_of(i * tm, tm), tm), :], o_ref)

    def select(start, nrows):
        pos = pos_ref[pl.ds(e, 1), :]
        want = (start + _iota((nrows, tm), 0)).astype(F32)
        return jnp.where(pos == want, 1.0, 0.0).astype(BF16)

    def blocks(body):
        big = 4 * MOE_ROWS
        assert tm % big == 0

        def run_big(blk, carry):
            body(pl.multiple_of(blk * big, big), big)
            return carry
        lax.fori_loop(0, n_small // 4, run_big, 0)
        done = (n_small // 4) * big

        @pl.when(n_small % 4 >= 2)
        def _():
            body(pl.multiple_of(done, big), 2 * MOE_ROWS)

        @pl.when(n_small % 2 == 1)
        def _():
            body(pl.multiple_of(done + (n_small % 4 // 2) * 2 * MOE_ROWS, MOE_ROWS), MOE_ROWS)

    @pl.when(j == 0)
    def _():
        def gather(start, nrows):
            rows = pl.ds(start, nrows)
            xs_scr[rows, :] = _dg(select(start, nrows), hn_ref[...], _NN).astype(BF16)
            y_scr[rows, :] = jnp.zeros((nrows, y_scr.shape[1]), F32)
        blocks(gather)

    def expert(start, nrows):
        rows = pl.ds(start, nrows)
        xs = xs_scr[rows, :]
        h = _silu(_dg(xs, w1_ref[0], _NN)) * _dg(xs, w3_ref[0], _NN)
        y_scr[rows, :] += _dg(h.astype(BF16), w2_ref[0], _NN)
    blocks(expert)

    @pl.when(j == nj - 1)
    def _():
        gt = gate_ref[...]
        gcol = jnp.sum(jnp.where(_iota(gt.shape, 1) == e, gt, 0.0), axis=-1, keepdims=True)

        def scatter(start, nrows):
            o_ref[...] += gcol * _dg(select(start, nrows), y_scr[pl.ds(start, nrows), :].astype(BF16), _TN)
        blocks(scatter)


def moe(x, hn, gates, pos, counts, w1, w3, w2, tm_router, tm, fc):
    m, d = x.shape
    n_exp, _, dff = w1.shape
    nj = dff // fc
    ratio = tm // tm_router
    cnt = counts[:, 0, :n_exp].astype(jnp.int32).reshape(m // tm, ratio, n_exp)
    before = (jnp.cumsum(cnt, axis=1) - cnt).astype(F32)
    shift = jnp.repeat(before.reshape(m // tm_router, n_exp).T, tm_router, axis=1)
    shift = jnp.pad(shift, ((0, SUBLANES - n_exp), (0, 0)))
    pos = jnp.where(pos >= 0, pos + shift, pos)
    cnt = cnt.sum(axis=1).reshape(-1)
    grid_spec = pltpu.PrefetchScalarGridSpec(
        num_scalar_prefetch=1,
        grid=(m // tm, n_exp, nj),
        in_specs=[pl.BlockSpec(memory_space=pl.ANY),
                  pl.BlockSpec((tm, d), lambda i, e, j, c: (i, 0), pipeline_mode=pl.Buffered(1)),
                  pl.BlockSpec((tm, LANES), lambda i, e, j, c: (i, 0), pipeline_mode=pl.Buffered(1)),
                  pl.BlockSpec((SUBLANES, tm), lambda i, e, j, c: (0, i), pipeline_mode=pl.Buffered(1)),
                  pl.BlockSpec((1, d, fc), lambda i, e, j, c: (e, 0, j)),
                  pl.BlockSpec((1, d, fc), lambda i, e, j, c: (e, 0, j)),
                  pl.BlockSpec((1, fc, d), lambda i, e, j, c: (e, j, 0))],
        out_specs=pl.BlockSpec((tm, d), lambda i, e, j, c: (i, 0), pipeline_mode=pl.Buffered(1)),
        scratch_shapes=[pltpu.VMEM((tm, d), BF16), pltpu.VMEM((tm, d), F32)])
    return pl.pallas_call(
        functools.partial(_moe_kernel, n_exp),
        grid_spec=grid_spec,
        out_shape=jax.ShapeDtypeStruct((m, d), F32),
        compiler_params=_cparams(("parallel", "arbitrary", "arbitrary")), name="moe",
    )(cnt, x, hn, gates, pos, w1, w3, w2)


_RW_DOT_A = _bdot
_RW_DOT_T = _bdot
_RW_DOT_W = _bdot
_RW_DOT_S = _bdot


RWKV_SUB = 4
RW = 512
RSHIFT = 1792


def _rwkv_token_maps(xs, w0_ref, wup_ref, a0_ref, aup_ref, gup_ref, kk_ref, ka_ref, rk_ref):
    r = xs[:, 0:RW]
    k = xs[:, RW:2 * RW]
    v = xs[:, 2 * RW:3 * RW]
    lr = xs[:, 3 * RW:3 * RW + LANES]
    gd = xs[:, 3 * RW + LANES:3 * RW + 2 * LANES]
    w_pre = w0_ref[...] + _bdot(jnp.tanh(lr), wup_ref[...])
    logw = -jnp.exp(-_softplus(-w_pre) - 0.5)
    a = _sigmoid(a0_ref[...] + _bdot(lr, aup_ref[...]))
    g = _bdot(_sigmoid(gd), gup_ref[...])
    kkr = k * kk_ref[...]
    kk = kkr / jnp.maximum(jnp.sqrt(_seg_sum(kkr * kkr, RWKV_HD)), 1e-12)
    k2 = k * (1.0 + (a - 1.0) * ka_ref[...])
    bonus = _seg_sum(r * k2 * rk_ref[...], RWKV_HD) * v
    return r, logw, k2, v, -kk, kk * a, g, bonus


def _rwkv_kernel(L, sub, nb, n_pairs, t_real, p_ref, prev0_ref, s0_ref, mu_ref, w0_ref, wup_ref, a0_ref, aup_ref,
                 gup_ref, kk_ref, ka_ref, rk_ref, lnw_ref, lnb_ref, y_ref, sf_ref, s_scr, prev_scr):
    c = pl.program_id(1)
    nc = pl.num_programs(1)
    L2 = 2 * L
    lane = _iota((1, LANES), 1)
    m0 = jnp.where(lane < RWKV_HD, 1.0, 0.0)
    m1 = 1.0 - m0
    rr = _iota((L2, L2), 0)
    cc = _iota((L2, L2), 1)
    same = (rr // L) == (cc // L)
    strict = jnp.logical_and(same, (cc % L) < (rr % L))
    incl = jnp.logical_and(same, (cc % L) <= (rr % L))
    eye2 = jnp.where(rr == cc, 1.0, 0.0)
    r128 = _iota((LANES, LANES), 0)
    c128 = _iota((LANES, LANES), 1)
    blk128 = (r128 // RWKV_HD) == (c128 // RWKV_HD)
    diag128 = r128 == c128
    fmat = jnp.where(_iota((LANES, RWKV_HD), 0) % RWKV_HD == _iota((LANES, RWKV_HD), 1), 1.0, 0.0)
    tri = _tri_incl(L)

    @pl.when(c == 0)
    def _():
        for bb in range(nb):
            prev_scr[bb, 0:1, :] = prev0_ref[bb]
            for j in range(n_pairs):
                s0 = s0_ref[bb, j]
                st = _xdot_r(fmat, s0, _NT)
                s_scr[bb * n_pairs + j] = jnp.where(blk128, st, 0.0)

    pairs = range(nb * n_pairs)
    bbs = [i // n_pairs for i in pairs]
    sls = [slice((i % n_pairs) * LANES, (i % n_pairs + 1) * LANES) for i in pairs]
    rowi = _iota((L, 1), 0)
    consts = (m0, m1, strict, incl, eye2, diag128, tri, rowi, pairs, bbs, sls)
    refs = (p_ref, mu_ref, w0_ref, wup_ref, a0_ref, aup_ref, gup_ref, kk_ref, ka_ref, rk_ref, lnw_ref, lnb_ref,
            y_ref, s_scr, prev_scr)
    for part in range(sub):
        _rwkv_chunk_body(L, nb, n_pairs, t_real, c * sub + part, slice(part * L, (part + 1) * L), consts, refs)

    @pl.when(c == nc - 1)
    def _():
        for i in pairs:
            sf_ref[bbs[i], i % n_pairs] = _xdot_l(s_scr[i], fmat, _TN)


def _rwkv_chunk_body(L, nb, n_pairs, t_real, chunk, tsl, consts, refs):
    (m0, m1, strict, incl, eye2, diag128, tri, rowi, pairs, bbs, sls) = consts
    (p_ref, mu_ref, w0_ref, wup_ref, a0_ref, aup_ref, gup_ref, kk_ref, ka_ref, rk_ref, lnw_ref, lnb_ref,
     y_ref, s_scr, prev_scr) = refs
    L2 = 2 * L

    def bd(x):
        return jnp.concatenate([x * m0, x * m1], axis=0)

    xs = []
    for bb in range(nb):
        p = p_ref[bb, tsl, :]
        prev = jnp.where(rowi == 0, prev_scr[bb, 0:1, :], pltpu.roll(p, 1, 0))
        prev_scr[bb, 0:1, :] = p[L - 1:L, :]
        xs.append(p + (prev - p) * mu_ref[...])
    maps = _rwkv_token_maps(jnp.concatenate(xs, axis=0), w0_ref, wup_ref, a0_ref, aup_ref, gup_ref,
                            kk_ref, ka_ref, rk_ref)
    if t_real % L != 0:
        valid = (chunk * L + _iota((nb * L, 1), 0) % L) < t_real
        maps = tuple(jnp.where(valid, z, 0.0) for z in maps)
    toks = [tuple(z[bb * L:(bb + 1) * L] for z in maps) for bb in range(nb)]

    tok = lambda j, which: toks[bbs[j]][which][:, sls[j]]
    lw = [tok(j, 1) for j in pairs]
    b = [_xdot_r(tri, x) for x in lw]
    bl = [x[L - 1:L, :] for x in b]
    e_b = [jnp.exp(x) for x in b]
    e_nb = [jnp.exp(-x) for x in b]
    e_lb = [jnp.exp(bl[j] - b[j]) for j in pairs]
    at = [bd(tok(j, 4) * jnp.exp(b[j] - lw[j])) for j in pairs]
    rt = [bd(tok(j, 0) * e_b[j]) for j in pairs]
    bh = [bd(tok(j, 5) * e_nb[j]) for j in pairs]
    kh = [bd(tok(j, 2) * e_nb[j]) for j in pairs]
    bt = [bd(tok(j, 5) * e_lb[j]) for j in pairs]
    kt = [bd(tok(j, 2) * e_lb[j]) for j in pairs]
    vb = [bd(tok(j, 3)) for j in pairs]
    if L2 % LANES == 0:
        gq = [_RW_DOT_A(jnp.concatenate([at[j], rt[j]], axis=0), jnp.concatenate([bh[j], kh[j]], axis=0), _NT)
              for j in pairs]
        a_ab = [jnp.where(strict, g[0:L2, 0:L2], 0.0) for g in gq]
        a_ak = [jnp.where(strict, g[0:L2, L2:2 * L2], 0.0) for g in gq]
        a_rb = [jnp.where(incl, g[L2:2 * L2, 0:L2], 0.0) for g in gq]
        a_rk = [jnp.where(incl, g[L2:2 * L2, L2:2 * L2], 0.0) for g in gq]
    else:
        a_ab = [jnp.where(strict, _RW_DOT_A(at[j], bh[j], _NT), 0.0) for j in pairs]
        a_ak = [jnp.where(strict, _RW_DOT_A(at[j], kh[j], _NT), 0.0) for j in pairs]
        a_rb = [jnp.where(incl, _RW_DOT_A(rt[j], bh[j], _NT), 0.0) for j in pairs]
        a_rk = [jnp.where(incl, _RW_DOT_A(rt[j], kh[j], _NT), 0.0) for j in pairs]
    x = a_ab
    tinv = [eye2 + a for a in a_ab]
    span = 2
    while span < L:
        x = [_RW_DOT_T(xx, xx) for xx in x]
        tinv = [tinv[j] + _RW_DOT_T(tinv[j], x[j]) for j in pairs]
        span *= 2
    akv = [_RW_DOT_W(a_ak[j], vb[j]) for j in pairs]
    wuv = [_RW_DOT_W(tinv[j], jnp.concatenate([at[j], akv[j]], axis=1)) for j in pairs]
    mn = [_RW_DOT_W(bt[j], wuv[j], _TN) for j in pairs]
    qy = [_RW_DOT_W(a_rb[j], wuv[j]) for j in pairs]
    mm = [jnp.where(diag128, jnp.exp(bl[j]), 0.0) + mn[j][:, 0:LANES] for j in pairs]
    nn = [mn[j][:, LANES:2 * LANES] + _RW_DOT_W(kt[j], vb[j], _TN) for j in pairs]
    q = [rt[j] + qy[j][:, 0:LANES] for j in pairs]
    yv = [qy[j][:, LANES:2 * LANES] + _RW_DOT_W(a_rk[j], vb[j]) for j in pairs]
    s = [s_scr[j] for j in pairs]
    ybd = [_RW_DOT_S(q[j], s[j]) + yv[j] for j in pairs]
    for j in pairs:
        s_scr[j] = _RW_DOT_S(mm[j], s[j]) + nn[j]

    y = jnp.concatenate(
        [jnp.concatenate([ybd[bb * n_pairs + jj][0:L] + ybd[bb * n_pairs + jj][L:L2] for jj in range(n_pairs)],
                         axis=1) for bb in range(nb)], axis=0)
    mean = _seg_sum(y, RWKV_HD) * (1.0 / RWKV_HD)
    d = y - mean
    var = _seg_sum(d * d, RWKV_HD) * (1.0 / RWKV_HD)
    ya = (d * lax.rsqrt(var + RWKV_GN_EPS) * lnw_ref[...] + lnb_ref[...] + maps[7]) * maps[6]
    for bb in range(nb):
        y_ref[bb, tsl, :] = ya[bb * L:(bb + 1) * L]


def rwkv_params(w, i):
    z64 = jnp.zeros((RWKV_HD, RW), F32)
    return {
        "mu": w["e_mu"][i].reshape(1, RSHIFT),
        "w0": w["e_w0"][i].reshape(1, RW),
        "wup": jnp.concatenate([w["e_w_up"][i], z64], 0).astype(BF16),
        "a0": w["e_a0"][i].reshape(1, RW),
        "aup": jnp.concatenate([z64, w["e_a_up"][i]], 0).astype(BF16),
        "gup": w["e_g_up"][i].astype(BF16),
        "k_k": w["e_k_k"][i].reshape(1, RW),
        "k_a": w["e_k_a"][i].reshape(1, RW),
        "r_k": w["e_r_k"][i].reshape(1, RW),
    }


def rwkv_mix(p_flat, prev, s0, bsz, t, L, prm, lnx_w, lnx_b, nb=1):
    n_pairs = RW // LANES
    tp = -(-t // L) * L
    sub = math.gcd(tp // L, RWKV_SUB)
    p3 = p_flat.reshape(bsz, t, RSHIFT)
    if tp != t:
        p3 = jnp.pad(p3, ((0, 0), (0, tp - t), (0, 0)))
    row = lambda n: pl.BlockSpec((1, n), lambda b, c: (0, 0))
    lora = pl.BlockSpec((LANES, RW), lambda b, c: (0, 0))
    st_spec = pl.BlockSpec((nb, n_pairs, LANES, RWKV_HD), lambda b, c: (b, 0, 0, 0))
    y, s_fin = pl.pallas_call(
        functools.partial(_rwkv_kernel, L, sub, nb, n_pairs, t),
        grid=(bsz // nb, tp // (sub * L)),
        in_specs=[pl.BlockSpec((nb, sub * L, RSHIFT), lambda b, c: (b, c, 0)),
                  pl.BlockSpec((nb, 1, RSHIFT), lambda b, c: (b, 0, 0)),
                  st_spec,
                  row(RSHIFT), row(RW), lora, row(RW), lora, lora, row(RW), row(RW), row(RW), row(RW), row(RW)],
        out_specs=[pl.BlockSpec((nb, sub * L, RW), lambda b, c: (b, c, 0)), st_spec],
        out_shape=[jax.ShapeDtypeStruct((bsz, tp, RW), F32),
                   jax.ShapeDtypeStruct((bsz, n_pairs, LANES, RWKV_HD), F32)],
        scratch_shapes=[pltpu.VMEM((nb * n_pairs, LANES, LANES), F32),
                        pltpu.VMEM((nb, SUBLANES, RSHIFT), F32)],
        compiler_params=_cparams(("parallel", "arbitrary")), name="rwkv",
    )(p3, prev.reshape(bsz, 1, RSHIFT), s0.reshape(bsz, n_pairs, LANES, RWKV_HD),
      prm["mu"], prm["w0"], prm["wup"], prm["a0"], prm["aup"], prm["gup"], prm["k_k"], prm["k_a"], prm["r_k"],
      lnx_w.reshape(1, RW), lnx_b.reshape(1, RW))
    return y[:, :t].reshape(bsz * t, RW), s_fin.reshape(bsz, RW // RWKV_HD, RWKV_HD, RWKV_HD)


N_Q_HEADS = 8
N_KV_HEADS = 2
Q_PER_KV = N_Q_HEADS // N_KV_HEADS
QW = N_Q_HEADS * ATT_HD
KW = N_KV_HEADS * ATT_HD
NEG = -1e30


def _t5_bucket_np(dist):
    n = np.maximum(dist, 0)
    max_exact = N_BUCKETS // 2
    nf = np.maximum(n, 1).astype(np.float32)
    large = max_exact + (np.log(nf / np.float32(max_exact)) / np.float32(math.log(BUCKET_MAX_DIST / max_exact))
                         * np.float32(N_BUCKETS - max_exact)).astype(np.int32)
    large = np.minimum(large, N_BUCKETS - 1)
    return np.where(n < max_exact, n, large)


def _bias_kernel(rt_ref, oh_ref, o_ref):
    o_ref[...] = _xdot_l(rt_ref[...], oh_ref[...])


def rel_bias(rel_table, dist):
    bucket = _t5_bucket_np(dist).reshape(-1)
    n = bucket.shape[0]
    onehot = jnp.asarray((np.arange(N_BUCKETS)[:, None] == bucket[None, :]).astype(np.float32), BF16)
    out = pl.pallas_call(
        _bias_kernel,
        out_shape=jax.ShapeDtypeStruct((N_Q_HEADS, n), F32),
    )(rel_table.T, onehot)
    return out.reshape((N_Q_HEADS,) + dist.shape)


def _head_norm(x, w_row):
    return x * lax.rsqrt(_seg_sum(x * x, ATT_HD) * (1.0 / ATT_HD) + EPS) * w_row


def _swa_prompt_kernel(nqb, q_ref, kvc_ref, kvp_ref, qw_ref, kw_ref, bias_ref, sink_ref, o_ref, ko_ref, vo_ref):
    i = pl.program_id(1)
    qn = _head_norm(q_ref[0], qw_ref[...])
    kvc = kvc_ref[0]
    kvp = kvp_ref[0]
    kn = [_head_norm(kvp[:, 0:KW], kw_ref[...])]
    vs = [kvp[:, KW:2 * KW]]
    kcn = _head_norm(kvc[:, 0:KW], kw_ref[...])
    for j in range(nqb):
        kn.append(kcn[j * WINDOW:(j + 1) * WINDOW])
        vs.append(kvc[j * WINDOW:(j + 1) * WINDOW, KW:2 * KW])
    kcat = [jnp.concatenate([kn[j], kn[j + 1]], axis=0).astype(BF16) for j in range(nqb)]
    vcat = [jnp.concatenate([vs[j], vs[j + 1]], axis=0).astype(BF16) for j in range(nqb)]
    qi = _iota((WINDOW, 2 * WINDOW), 0)
    kj = _iota((WINDOW, 2 * WINDOW), 1)
    dist = qi + WINDOW - kj
    band = jnp.logical_and(dist >= 0, dist < WINDOW)
    valid = [jnp.logical_and(band, jnp.logical_or(kj >= WINDOW, i > 0))] + [band] * (nqb - 1)
    lane = _iota((1, LANES), 1)
    masks = (jnp.where(lane < ATT_HD, 1.0, 0.0), jnp.where(lane < ATT_HD, 0.0, 1.0))
    scale = ATT_HD ** -0.5
    heads = range(N_Q_HEADS)
    units = [(j, h) for j in range(nqb) for h in heads]
    kv_of = lambda h: h // Q_PER_KV
    qts = [qn[j * WINDOW:(j + 1) * WINDOW, (h // 2) * LANES:(h // 2 + 1) * LANES] for j, h in units]
    qts = [pltpu.roll(qts[u], ATT_HD, 1) if h % 2 != kv_of(h) else qts[u] for u, (j, h) in enumerate(units)]
    qms = [(qts[u] * masks[kv_of(h)]).astype(BF16) for u, (j, h) in enumerate(units)]
    s_all = [_dg(jnp.concatenate(qms[j * N_Q_HEADS:(j + 1) * N_Q_HEADS], axis=0), kcat[j], _NT) * scale
             for j in range(nqb)]
    logits = [jnp.where(valid[j], s_all[j][h * WINDOW:(h + 1) * WINDOW] + bias_ref[h], NEG) for j, h in units]
    sinks = [sink_ref[h:h + 1, 0:1] for j, h in units]
    mx = [jnp.maximum(jnp.max(logits[u], axis=-1, keepdims=True), sinks[u]) for u in range(len(units))]
    pr = [jnp.exp(logits[u] - mx[u]) for u in range(len(units))]
    den = [jnp.sum(pr[u], axis=-1, keepdims=True) + jnp.exp(sinks[u] - mx[u]) for u in range(len(units))]
    probs = [(pr[u] * (1.0 / den[u])).astype(BF16) for u in range(len(units))]
    o_all = [_dg(jnp.concatenate(probs[j * N_Q_HEADS:(j + 1) * N_Q_HEADS], axis=0), vcat[j], _NN)
             for j in range(nqb)]
    os_ = [o_all[j][h * WINDOW:(h + 1) * WINDOW] for j, h in units]
    os_ = [pltpu.roll(os_[u], ATT_HD, 1) if h % 2 != kv_of(h) else os_[u] for u, (j, h) in enumerate(units)]
    for j in range(nqb):
        for jq in range(QW // LANES):
            o_ref[0, j * WINDOW:(j + 1) * WINDOW, jq * LANES:(jq + 1) * LANES] = (
                os_[j * N_Q_HEADS + 2 * jq] * masks[0] + os_[j * N_Q_HEADS + 2 * jq + 1] * masks[1])
    ko_ref[0] = kn[nqb]
    vo_ref[0] = vs[nqb]


SWA_QB = 2


def swa_prompt(q, kv, bsz, t, q_norm, k_norm, bias, sinks):
    nqb = math.gcd(t // WINDOW, SWA_QB)
    rows = nqb * WINDOW
    nb = t // rows
    q3 = q.reshape(bsz, t, QW)
    kv3 = kv.reshape(bsz, t, 2 * KW)
    o, ko, vo = pl.pallas_call(
        functools.partial(_swa_prompt_kernel, nqb),
        grid=(bsz, nb),
        in_specs=[pl.BlockSpec((1, rows, QW), lambda b, i: (b, i, 0)),
                  pl.BlockSpec((1, rows, 2 * KW), lambda b, i: (b, i, 0)),
                  pl.BlockSpec((1, WINDOW, 2 * KW), lambda b, i: (b, jnp.maximum(i * nqb - 1, 0), 0)),
                  pl.BlockSpec((1, QW), lambda b, i: (0, 0)),
                  pl.BlockSpec((1, KW), lambda b, i: (0, 0)),
                  pl.BlockSpec((N_Q_HEADS, WINDOW, 2 * WINDOW), lambda b, i: (0, 0, 0)),
                  pl.BlockSpec((N_Q_HEADS, LANES), lambda b, i: (0, 0))],
        out_specs=[pl.BlockSpec((1, rows, QW), lambda b, i: (b, i, 0)),
                   pl.BlockSpec((1, WINDOW, KW), lambda b, i: (b, 0, 0)),
                   pl.BlockSpec((1, WINDOW, KW), lambda b, i: (b, 0, 0))],
        out_shape=[jax.ShapeDtypeStruct((bsz, t, QW), F32),
                   jax.ShapeDtypeStruct((bsz, WINDOW, KW), F32),
                   jax.ShapeDtypeStruct((bsz, WINDOW, KW), F32)],
        compiler_params=_cparams(("parallel", "arbitrary")), name="swa_prompt",
    )(q3, kv3, kv3, jnp.tile(q_norm, N_Q_HEADS).reshape(1, QW), jnp.tile(k_norm, N_KV_HEADS).reshape(1, KW),
      bias, jnp.broadcast_to(sinks[:, None], (N_Q_HEADS, LANES)))
    return o.reshape(bsz * t, QW), ko, vo


DEC_TP = 8


def _swa_decode_kernel(nbt, t_real, q_ref, kv_ref, ck_ref, cv_ref, qw_ref, kw_ref, bc_ref, bn_ref, sink_ref,
                       o_ref, ko_ref, vo_ref):
    rows = N_Q_HEADS * DEC_TP
    tq = _iota((rows, WINDOW), 0) % DEC_TP
    valid_c = _iota((rows, WINDOW), 1) > tq
    jn = _iota((rows, DEC_TP), 1)
    valid_n = jnp.logical_and(jn <= _iota((rows, DEC_TP), 0) % DEC_TP, jn < t_real)
    lane = _iota((1, LANES), 1)
    masks = (jnp.where(lane < ATT_HD, 1.0, 0.0), jnp.where(lane < ATT_HD, 0.0, 1.0))
    row8 = _iota((DEC_TP, 1), 0)
    scale = ATT_HD ** -0.5
    bias_c = jnp.concatenate([bc_ref[kv] for kv in range(N_KV_HEADS)], axis=0)
    bias_n = jnp.concatenate([bn_ref[kv][:, 0:DEC_TP] for kv in range(N_KV_HEADS)], axis=0)
    sink = jnp.concatenate([sink_ref[kv][:, 0:1] for kv in range(N_KV_HEADS)], axis=0)
    bs = range(nbt)
    heads = range(N_Q_HEADS)
    qn_all = _head_norm(q_ref[...].reshape(nbt * DEC_TP, QW), qw_ref[...])
    kvn_all = kv_ref[...].reshape(nbt * DEC_TP, 2 * KW)
    knew_all = _head_norm(kvn_all[:, 0:KW], kw_ref[...])
    knew = [knew_all[b * DEC_TP:(b + 1) * DEC_TP] for b in bs]
    vnew = [kvn_all[b * DEC_TP:(b + 1) * DEC_TP, KW:2 * KW] for b in bs]
    kc = [ck_ref[b] for b in bs]
    vc = [cv_ref[b] for b in bs]

    def stack_q(b):
        pieces = []
        for h in heads:
            qt = qn_all[b * DEC_TP:(b + 1) * DEC_TP, (h // 2) * LANES:(h // 2 + 1) * LANES]
            if h % 2 != h // Q_PER_KV:
                qt = pltpu.roll(qt, ATT_HD, 1)
            pieces.append(qt * masks[h // Q_PER_KV])
        return jnp.concatenate(pieces, axis=0)

    qs = [stack_q(b) for b in bs]
    l_c = [jnp.where(valid_c, _bdot(qs[b], kc[b], _NT) * scale + bias_c, NEG) for b in bs]
    l_n = [jnp.where(valid_n, _bdot(qs[b], knew[b], _NT) * scale + bias_n, NEG) for b in bs]
    mx = [jnp.maximum(jnp.maximum(jnp.max(l_c[b], axis=-1, keepdims=True),
                                  jnp.max(l_n[b], axis=-1, keepdims=True)), sink) for b in bs]
    p_c = [jnp.exp(l_c[b] - mx[b]) for b in bs]
    p_n = [jnp.exp(l_n[b] - mx[b]) for b in bs]
    inv = [1.0 / (jnp.sum(p_c[b], axis=-1, keepdims=True) + jnp.sum(p_n[b], axis=-1, keepdims=True)
                  + jnp.exp(sink - mx[b])) for b in bs]
    o = [_bdot(p_c[b] * inv[b], vc[b]) + _bdot(p_n[b] * inv[b], vnew[b]) for b in bs]
    for b in bs:
        for jq in range(QW // LANES):
            parts = []
            for h in (2 * jq, 2 * jq + 1):
                piece = o[b][h * DEC_TP:(h + 1) * DEC_TP]
                if h % 2 != h // Q_PER_KV:
                    piece = pltpu.roll(piece, ATT_HD, 1)
                parts.append(piece * masks[h % 2])
            o_ref[b, :, jq * LANES:(jq + 1) * LANES] = parts[0] + parts[1]
    for b in bs:
        for cache, new, out in ((kc[b], knew[b], ko_ref), (vc[b], vnew[b], vo_ref)):
            shifted = pltpu.roll(cache, WINDOW - t_real, 0)
            new_r = pltpu.roll(new, DEC_TP - t_real, 0)
            out[b, 0:WINDOW - DEC_TP] = shifted[0:WINDOW - DEC_TP]
            out[b, WINDOW - DEC_TP:WINDOW] = jnp.where(row8 >= DEC_TP - t_real, new_r,
                                                       shifted[WINDOW - DEC_TP:WINDOW])


def swa_decode(q, kv, cache_k, cache_v, bsz, t, q_norm, k_norm, rel_table, sinks, nbt):
    pad = ((0, 0), (0, DEC_TP - t), (0, 0))
    q3 = jnp.pad(q.reshape(bsz, t, QW), pad)
    kv3 = jnp.pad(kv.reshape(bsz, t, 2 * KW), pad)
    kpos = np.concatenate([np.arange(WINDOW) - WINDOW, np.arange(DEC_TP)])
    dist = np.arange(DEC_TP)[:, None] - kpos[None, :]
    bias = rel_bias(rel_table, dist)
    rows = Q_PER_KV * DEC_TP
    bias = bias.reshape(N_KV_HEADS, rows, WINDOW + DEC_TP)
    bias_c = bias[:, :, :WINDOW]
    bias_n = jnp.pad(bias[:, :, WINDOW:], ((0, 0), (0, 0), (0, LANES - DEC_TP)))
    sink_rows = jnp.broadcast_to(sinks.reshape(N_KV_HEADS, Q_PER_KV, 1, 1),
                                 (N_KV_HEADS, Q_PER_KV, DEC_TP, LANES)).reshape(N_KV_HEADS, rows, LANES)
    full3 = lambda shape: pl.BlockSpec(shape, lambda i: (0, 0, 0))
    o, ko, vo = pl.pallas_call(
        functools.partial(_swa_decode_kernel, nbt, t),
        grid=(bsz // nbt,),
        in_specs=[pl.BlockSpec((nbt, DEC_TP, QW), lambda i: (i, 0, 0)),
                  pl.BlockSpec((nbt, DEC_TP, 2 * KW), lambda i: (i, 0, 0)),
                  pl.BlockSpec((nbt, WINDOW, KW), lambda i: (i, 0, 0)),
                  pl.BlockSpec((nbt, WINDOW, KW), lambda i: (i, 0, 0)),
                  pl.BlockSpec((1, QW), lambda i: (0, 0)),
                  pl.BlockSpec((1, KW), lambda i: (0, 0)),
                  full3((N_KV_HEADS, rows, WINDOW)),
                  full3((N_KV_HEADS, rows, LANES)),
                  full3((N_KV_HEADS, rows, LANES))],
        out_specs=[pl.BlockSpec((nbt, DEC_TP, QW), lambda i: (i, 0, 0)),
                   pl.BlockSpec((nbt, WINDOW, KW), lambda i: (i, 0, 0)),
                   pl.BlockSpec((nbt, WINDOW, KW), lambda i: (i, 0, 0))],
        out_shape=[jax.ShapeDtypeStruct((bsz, DEC_TP, QW), F32),
                   jax.ShapeDtypeStruct((bsz, WINDOW, KW), F32),
                   jax.ShapeDtypeStruct((bsz, WINDOW, KW), F32)],
        compiler_params=_cparams(("parallel",)), name="swa_decode",
    )(q3, kv3, cache_k, cache_v, jnp.tile(q_norm, N_Q_HEADS).reshape(1, QW),
      jnp.tile(k_norm, N_KV_HEADS).reshape(1, KW), bias_c, bias_n, sink_rows)
    return o[:, :t].reshape(bsz * t, QW), ko, vo


S5_G = 32
S5_W = S5_G * S5_P
S5_CP = S5_CHUNK * S5_P
S5_PK = 2 * S5_N
S5_HW = S5_G * S5_PK


S5_QT = S5_W // LANES
S5_GT = LANES // S5_P
S5_XW = S5_CHUNK * LANES
S5_HQ = S5_GT * S5_PK


def _s5_group_maps(t_effs, a2, ldt, b1, b2, c1, c2):
    L = S5_CHUNK
    ar2 = a2[0:1, :]
    ai2 = a2[1:2, :]
    step = jnp.exp(ldt)
    mi = _iota((3 * SUBLANES, S5_PK), 0).astype(F32)
    mag = jnp.exp(mi * (step * ar2))
    ang = mi * (step * ai2)
    pwa = mag * jnp.cos(ang)
    pwb = mag * jnp.sin(ang)
    abr = pwa[1:2]
    abi = pwb[1:2]
    den = ar2 * ar2 + ai2 * ai2
    fa = ((abr - 1.0) * ar2 + abi * ai2) / den
    fb = (abi * ar2 - (abr - 1.0) * ai2) / den
    bp1 = b1 * fa + b2 * fb
    bp2 = b2 * fa - b1 * fb
    cpow = [c1 * pwa[m:m + 1] + c2 * pwb[m:m + 1] for m in range(L + 1)]
    kern_t = _hdot(bp1, jnp.concatenate(cpow[0:L], axis=0), _NT)
    sgn = jnp.where(_iota((1, S5_PK), 1) < S5_N, -1.0, 1.0)
    kbs, als = [], []
    for t_eff in t_effs:
        kbs.append([bp1 * pwa[max(t_eff - 1 - i, 0):max(t_eff - 1 - i, 0) + 1]
                    + bp2 * pwb[max(t_eff - 1 - i, 0):max(t_eff - 1 - i, 0) + 1] for i in range(L)])
        als.append(jnp.concatenate([pwa[t_eff:t_eff + 1], sgn * pwb[t_eff:t_eff + 1]], axis=0))
    return kern_t, kbs, als, cpow[1:L + 1]


def _s5_prep_kernel(t_effs, a_ref, ldt_ref, b1_ref, b2_ref, c1_ref, c2_ref, bd_ref, kc_ref, *rest):
    L = S5_CHUNK
    n_t = len(t_effs)
    kb_refs = rest[:n_t]
    al_refs = rest[n_t:2 * n_t]
    lane = _iota((S5_P, LANES), 1)
    for kb_ref in kb_refs:
        kb_ref[0] = jnp.zeros(kb_ref.shape[1:], kb_ref.dtype)
    bd_rows = [[] for _ in range(L)]
    kct_rows = [[] for _ in range(L)]
    for g in range(S5_GT):
        kern_t, kbs, als, kct = _s5_group_maps(t_effs, a_ref[g], ldt_ref[g], b1_ref[g], b2_ref[g],
                                               c1_ref[g], c2_ref[g])
        in_group = jnp.logical_and(lane >= g * S5_P, lane < (g + 1) * S5_P)
        for tau in range(L):
            shift = (g * S5_P - tau * S5_P) % S5_CP
            moved = pltpu.roll(kern_t, shift, 1) if shift else kern_t
            bd_rows[tau].append(jnp.where(in_group, moved[:, 0:LANES], 0.0))
        for k in range(n_t):
            for i in range(L):
                kb_refs[k][0, i * LANES + g * S5_P:i * LANES + (g + 1) * S5_P, g * S5_PK:(g + 1) * S5_PK] = (
                    kbs[k][i].astype(kb_refs[k].dtype))
            al_refs[k][g] = als[k]
        zl = jnp.zeros((S5_P, g * S5_PK), F32)
        zr = jnp.zeros((S5_P, (S5_GT - 1 - g) * S5_PK), F32)
        for t in range(L):
            parts = ([zl] if g else []) + [kct[t]] + ([zr] if g < S5_GT - 1 else [])
            kct_rows[t].append(jnp.concatenate(parts, axis=1))
    for tau in range(L):
        bd_ref[0, tau] = jnp.concatenate(bd_rows[tau], axis=0).astype(bd_ref.dtype)
    for t in range(L):
        blk_t = jnp.concatenate(kct_rows[t], axis=0)
        kc_ref[0, :, t * LANES:(t + 1) * LANES] = blk_t.T.astype(kc_ref.dtype)


def s5_prep(w, i, t_effs):
    dup = lambda z: jnp.concatenate([z, z], axis=-1)
    a = jnp.stack([dup(w["o_a_re"][i]), dup(w["o_a_im"][i])], axis=1)
    ldt = jnp.broadcast_to(w["o_log_dt"][i][:, None, None], (S5_G, 1, S5_PK))
    bt_re = jnp.swapaxes(w["o_b_re"][i], 1, 2)
    bt_im = jnp.swapaxes(w["o_b_im"][i], 1, 2)
    b1 = jnp.concatenate([bt_re, bt_im], -1)
    b2 = jnp.concatenate([-bt_im, bt_re], -1)
    c_re, c_im = w["o_c_re"][i], w["o_c_im"][i]
    c1 = jnp.concatenate([c_re, -c_im], -1)
    c2 = jnp.concatenate([-c_im, -c_re], -1)
    n_t = len(t_effs)
    g3 = lambda r, c: pl.BlockSpec((S5_GT, r, c), lambda q: (q, 0, 0))
    outs = pl.pallas_call(
        functools.partial(_s5_prep_kernel, tuple(t_effs)),
        grid=(S5_QT,),
        in_specs=[g3(2, S5_PK), g3(1, S5_PK), g3(S5_P, S5_PK), g3(S5_P, S5_PK), g3(S5_P, S5_PK), g3(S5_P, S5_PK)],
        out_specs=[pl.BlockSpec((1, S5_CHUNK, LANES, LANES), lambda q: (q, 0, 0, 0)),
                   pl.BlockSpec((1, S5_HQ, S5_XW), lambda q: (q, 0, 0))]
                  + [pl.BlockSpec((1, S5_XW, S5_HQ), lambda q: (q, 0, 0))] * n_t
                  + [g3(2, S5_PK)] * n_t,
        out_shape=[jax.ShapeDtypeStruct((S5_QT, S5_CHUNK, LANES, LANES), BF16),
                   jax.ShapeDtypeStruct((S5_QT, S5_HQ, S5_XW), BF16)]
                  + [jax.ShapeDtypeStruct((S5_QT, S5_XW, S5_HQ), BF16)] * n_t
                  + [jax.ShapeDtypeStruct((S5_G, 2, S5_PK), F32)] * n_t,
        compiler_params=_cparams(("parallel",)), name="s5_prep",
    )(a, ldt, b1, b2, c1, c2)
    bd, kc = outs[0], outs[1]
    mats = []
    for k in range(n_t):
        al = outs[2 + n_t + k]
        mats.append((bd, outs[2 + k], kc, al[:, 0, :].reshape(1, S5_HW), al[:, 1, :].reshape(1, S5_HW)))
    return mats


def _s5_e_kernel(u_ref, kb_ref, e_ref):
    e_ref[...] = _bdot(u_ref[0], kb_ref[0])


def _s5_swap(h):
    n = h.shape[-1]
    lane = _iota(h.shape, 1)
    return jnp.where(lane % S5_PK < S5_N, pltpu.roll(h, n - S5_N, 1), pltpu.roll(h, S5_N, 1))


def _s5_scan_kernel(bsz, cg, e_ref, h0_ref, ala_ref, alb_ref, hp_ref, hf_ref, h_scr):
    @pl.when(pl.program_id(0) == 0)
    def _():
        h_scr[...] = h0_ref[...]

    ala = ala_ref[...]
    alb = alb_ref[...]

    def body(c, hs):
        out = []
        for b in range(bsz):
            hp_ref[b, pl.ds(c, 1), :] = hs[b]
            out.append(ala * hs[b] + alb * _s5_swap(hs[b]) + e_ref[b, pl.ds(c, 1), :])
        return tuple(out)

    hs = lax.fori_loop(0, cg, body, tuple(h_scr[b:b + 1, :] for b in range(bsz)))
    for b in range(bsz):
        h_scr[b:b + 1, :] = hs[b]
        hf_ref[b:b + 1, :] = hs[b]


def _s5_step_kernel(e_ref, h0_ref, ala_ref, alb_ref, hf_ref):
    h = h0_ref[...]
    hf_ref[...] = ala_ref[...] * h + alb_ref[...] * _s5_swap(h) + e_ref[...]


def _s5_y_kernel(u_ref, hp_ref, bd_ref, kc_ref, y_ref, k_scr):
    @pl.when(pl.program_id(1) == 0)
    def _():
        zero = jnp.zeros((LANES, LANES), k_scr.dtype)
        for i in range(S5_CHUNK):
            for t in range(S5_CHUNK):
                k_scr[i * LANES:(i + 1) * LANES, t * LANES:(t + 1) * LANES] = bd_ref[0, t - i] if t >= i else zero

    y_ref[0] = _bdot(u_ref[0], k_scr[...]) + _bdot(hp_ref[...], kc_ref[0])


def s5_core(x, h0, bsz, mats):
    bd, kbbig, kcbig, ala, alb = mats
    rows = x.shape[1]
    nc = rows // bsz
    tr = math.gcd(rows, 512)
    e = pl.pallas_call(
        _s5_e_kernel,
        grid=(S5_QT, rows // tr),
        in_specs=[pl.BlockSpec((1, tr, S5_XW), lambda q, r: (q, r, 0)),
                  pl.BlockSpec((1, S5_XW, S5_HQ), lambda q, r: (q, 0, 0))],
        out_specs=pl.BlockSpec((tr, S5_HQ), lambda q, r: (r, q)),
        out_shape=jax.ShapeDtypeStruct((rows, S5_HW), F32),
        compiler_params=_cparams(("parallel", "parallel")), name="s5_e",
    )(x, kbbig)
    row = pl.BlockSpec((1, S5_HW), lambda i: (0, 0))
    if nc == 1:
        hp = h0
        hf = pl.pallas_call(
            _s5_step_kernel,
            out_shape=jax.ShapeDtypeStruct((bsz, S5_HW), F32), name="s5_step",
        )(e, h0, ala, alb)
    else:
        cg = math.gcd(nc, 64)
        hp, hf = pl.pallas_call(
            functools.partial(_s5_scan_kernel, bsz, cg),
            grid=(nc // cg,),
            in_specs=[pl.BlockSpec((bsz, cg, S5_HW), lambda i: (0, i, 0)),
                      pl.BlockSpec((bsz, S5_HW), lambda i: (0, 0)), row, row],
            out_specs=[pl.BlockSpec((bsz, cg, S5_HW), lambda i: (0, i, 0)),
                       pl.BlockSpec((bsz, S5_HW), lambda i: (0, 0))],
            out_shape=[jax.ShapeDtypeStruct((bsz, nc, S5_HW), F32),
                       jax.ShapeDtypeStruct((bsz, S5_HW), F32)],
            scratch_shapes=[pltpu.VMEM((bsz, S5_HW), F32)],
            compiler_params=_cparams(("arbitrary",)), name="s5_scan",
        )(e.reshape(bsz, nc, S5_HW), h0, ala, alb)
    y = pl.pallas_call(
        _s5_y_kernel,
        grid=(S5_QT, rows // tr),
        in_specs=[pl.BlockSpec((1, tr, S5_XW), lambda q, r: (q, r, 0)),
                  pl.BlockSpec((tr, S5_HQ), lambda q, r: (r, q)),
                  pl.BlockSpec((1, S5_CHUNK, LANES, LANES), lambda q, r: (q, 0, 0, 0)),
                  pl.BlockSpec((1, S5_HQ, S5_XW), lambda q, r: (q, 0, 0))],
        out_specs=pl.BlockSpec((1, tr, S5_XW), lambda q, r: (q, r, 0)),
        out_shape=jax.ShapeDtypeStruct((S5_QT, rows, S5_XW), F32),
        scratch_shapes=[pltpu.VMEM((S5_XW, S5_XW), BF16)],
        compiler_params=_cparams(("parallel", "arbitrary")), name="s5_y",
    )(x, hp.reshape(rows, S5_HW), bd, kcbig)
    return y, hf


def _s5_post_kernel(tpr, y_ref, u_ref, d_ref, gw_ref, gb_ref, o_ref, y_scr, u_scr):
    _from_chunk_rows(y_ref, y_scr, tpr)
    _from_chunk_rows(u_ref, u_scr, tpr)
    y = jnp.concatenate([y_scr[q] for q in range(S5_QT)], axis=1)
    u = jnp.concatenate([u_scr[q] for q in range(S5_QT)], axis=1)
    x = y + d_ref[...] * u
    z = 0.5 * x * (1.0 + jnp.tanh(math.sqrt(2.0 / math.pi) * (x + 0.044715 * (x * x * x))))
    o_ref[...] = z * _sigmoid(_dg(z.astype(BF16), gw_ref[...], _NN) + gb_ref[...])


def s5_post(y4, u4, m, tpr, d, glu_w_bf16, glu_b, tm):
    spec4 = pl.BlockSpec((S5_QT, tm // tpr, S5_XW), lambda i: (0, i, 0))
    row = pl.BlockSpec((1, S5_W), lambda i: (0, 0))
    return pl.pallas_call(
        functools.partial(_s5_post_kernel, tpr),
        grid=(m // tm,),
        in_specs=[spec4, spec4, row, pl.BlockSpec((S5_W, S5_W), lambda i: (0, 0)), row],
        out_specs=pl.BlockSpec((tm, S5_W), lambda i: (i, 0)),
        out_shape=jax.ShapeDtypeStruct((m, S5_W), F32),
        scratch_shapes=[pltpu.VMEM((S5_QT, tm, LANES), F32), pltpu.VMEM((S5_QT, tm, LANES), F32)],
        compiler_params=_cparams(("parallel",)), name="s5_post",
    )(y4, u4, d.reshape(1, S5_W), glu_w_bf16, glu_b.reshape(1, S5_W))


def s5_mix(u4, h_re, h_im, bsz, t, mats, d, glu_w_bf16, glu_b, tm):
    tpr = min(t, CHUNK_ROW_TOKENS)
    h0 = jnp.concatenate([h_re, h_im], axis=-1).reshape(bsz, S5_HW)
    y4, hf = s5_core(u4, h0, bsz, mats)
    out = s5_post(y4, u4, bsz * t, tpr, d, glu_w_bf16, glu_b, tm)
    hf = hf.reshape(bsz, S5_G, 2, S5_N)
    return out, hf[:, :, 0], hf[:, :, 1]


GLA_DK = 64
GLA_DV = 128
GLA_HEADS = 4
GLA_KW = GLA_HEADS * GLA_DK
GLA_VW = GLA_HEADS * GLA_DV
GLA_PW = 2 * GLA_KW + 2 * GLA_VW + LANES


def _gla_kernel(L, sub, t_real, nb, p_ref, aup_ref, ab_ref, nw_ref, s0_ref, y_ref, sf_ref, s_scr):
    c = pl.program_id(1)
    nc = pl.num_programs(1)
    n_pairs = GLA_KW // LANES

    @pl.when(c == 0)
    def _():
        for bb in range(nb):
            for j in range(n_pairs):
                s_scr[bb * n_pairs + j] = s0_ref[bb, j]

    lane = _iota((1, LANES), 1)
    masks = (jnp.where(lane < GLA_DK, 1.0, 0.0), jnp.where(lane < GLA_DK, 0.0, 1.0))
    incl = _iota((L, L), 1) <= _iota((L, L), 0)
    diag128 = _iota((LANES, LANES), 0) == _iota((LANES, LANES), 1)
    ones128 = jnp.ones((LANES, LANES), BF16)
    tri = _tri_incl(L)
    rows = range(nb)
    for part in range(sub):
        _gla_chunk(L, t_real, n_pairs, c * sub + part, slice(part * L, (part + 1) * L), rows, masks, incl,
                   diag128, ones128, tri, p_ref, aup_ref, ab_ref, nw_ref, y_ref, s_scr)

    @pl.when(c == nc - 1)
    def _():
        for bb in range(nb):
            for j in range(n_pairs):
                sf_ref[bb, j] = s_scr[bb * n_pairs + j]


def _gla_chunk(L, t_real, n_pairs, chunk, tsl, rows, masks, incl, diag128, ones128, tri,
               p_ref, aup_ref, ab_ref, nw_ref, y_ref, s_scr):
    ps = [p_ref[bb, tsl, :] for bb in rows]
    zs = [_bdot(p[:, 2 * GLA_KW + 2 * GLA_VW:], aup_ref[...]) + ab_ref[...] for p in ps]
    gk = [-_softplus(-z) * (1.0 / GLA_GATE_NORM) for z in zs]
    if t_real % L != 0:
        tok = chunk * L + _iota((L, 1), 0)
        gk = [jnp.where(tok < t_real, x, 0.0) for x in gk]
    b = [_xdot_r(tri, x) for x in gk]
    bl = [x[L - 1:L, :] for x in b]
    qd = [ps[i][:, 0:GLA_KW] * (GLA_DK ** -0.5) * jnp.exp(b[i]) for i in rows]
    kh = [ps[i][:, GLA_KW:2 * GLA_KW] * jnp.exp(-b[i]) for i in rows]
    kt = [ps[i][:, GLA_KW:2 * GLA_KW] * jnp.exp(bl[i] - b[i]) for i in rows]
    heads = [(bb, h) for bb in rows for h in range(GLA_HEADS)]
    sl = lambda h: slice((h // 2) * LANES, (h // 2 + 1) * LANES)
    hs = lambda h: slice(2 * GLA_KW + h * GLA_DV, 2 * GLA_KW + (h + 1) * GLA_DV)
    gs = lambda h: slice(2 * GLA_KW + GLA_VW + h * GLA_DV, 2 * GLA_KW + GLA_VW + (h + 1) * GLA_DV)
    st = [s_scr[bb * n_pairs + j] for bb in rows for j in range(n_pairs)]
    qm = [qd[bb][:, sl(h)] * masks[h % 2] for bb, h in heads]
    vh = [ps[bb][:, hs(h)] for bb, h in heads]
    attn = [jnp.where(incl, _bdot(qm[i], kh[bb][:, sl(h)], _NT), 0.0) for i, (bb, h) in enumerate(heads)]
    o = [_bdot(attn[i], vh[i]) + _bdot(qm[i], st[bb * n_pairs + h // 2]) for i, (bb, h) in enumerate(heads)]
    kv = [_bdot(kt[bb][:, sl(h)], vh[i], _TN) for i, (bb, h) in enumerate(heads)]
    for i, (bb, h) in enumerate(heads):
        of = o[i] * lax.rsqrt(jnp.mean(o[i] * o[i], axis=-1, keepdims=True) + EPS) * nw_ref[...]
        y_ref[bb, tsl, h * GLA_DV:(h + 1) * GLA_DV] = of * _silu(ps[bb][:, gs(h)])
    for bb in rows:
        for j in range(n_pairs):
            i0 = bb * GLA_HEADS + 2 * j
            pcol = _xdot_l(jnp.where(diag128, jnp.exp(bl[bb][:, j * LANES:(j + 1) * LANES]), 0.0), ones128)
            s_scr[bb * n_pairs + j] = pcol * st[bb * n_pairs + j] + jnp.concatenate(
                [kv[i0][0:GLA_DK], kv[i0 + 1][GLA_DK:2 * GLA_DK]], axis=0)


GLA_SUB = 4


def gla_mix(p_gla, s0, bsz, t, L, aup_pad, a_b, norm_w, nb=1):
    sub = math.gcd(-(-t // L), GLA_SUB)
    step = sub * L
    tp = -(-t // L) * L
    p3 = p_gla.reshape(bsz, t, GLA_PW)
    if tp != t:
        p3 = jnp.pad(p3, ((0, 0), (0, tp - t), (0, 0)))
    n_pairs = GLA_KW // LANES
    st_spec = pl.BlockSpec((nb, n_pairs, LANES, LANES), lambda b, c: (b, 0, 0, 0))
    y, s_fin = pl.pallas_call(
        functools.partial(_gla_kernel, L, sub, t, nb),
        grid=(bsz // nb, tp // step),
        in_specs=[pl.BlockSpec((nb, step, GLA_PW), lambda b, c: (b, c, 0)),
                  pl.BlockSpec((LANES, GLA_KW), lambda b, c: (0, 0)),
                  pl.BlockSpec((1, GLA_KW), lambda b, c: (0, 0)),
                  pl.BlockSpec((1, GLA_DV), lambda b, c: (0, 0)),
                  st_spec],
        out_specs=[pl.BlockSpec((nb, step, GLA_VW), lambda b, c: (b, c, 0)), st_spec],
        out_shape=[jax.ShapeDtypeStruct((bsz, tp, GLA_VW), F32),
                   jax.ShapeDtypeStruct((bsz, n_pairs, LANES, LANES), F32)],
        scratch_shapes=[pltpu.VMEM((nb * n_pairs, LANES, LANES), F32)],
        compiler_params=_cparams(("parallel", "arbitrary")), name="gla",
    )(p3, aup_pad, a_b.reshape(1, GLA_KW), norm_w.reshape(1, GLA_DV),
      s0.reshape(bsz, n_pairs, LANES, LANES))
    return y[:, :t].reshape(bsz * t, GLA_VW), s_fin.reshape(bsz, GLA_HEADS, GLA_DK, GLA_DV)


GLA_LR = 16
D_FF_CHUNK = 2816
D_FF_EXPERT_CHUNK = 1792
TOKEN_TILE = 512
PROMPT_CHUNK = 64
PROMPT_ROWS_PER_STEP = 2
SAMPLE_ROWS_PER_STEP = 16
FFN_TOKENS = 512
ROUTER_TOKENS = 1024
MOE_TOKENS = 2048


def _prepare_weights(w):
    win = w["e_w_in"][0]
    w_gla = jnp.pad(win[:, RSHIFT:], ((0, 0), (0, LANES - GLA_LR)))
    wo = w["o_w_in"][0]
    return {
        "e_w_rwkv": win[:, :RSHIFT].astype(BF16),
        "e_w_gla": w_gla.astype(BF16),
        "rwkv": rwkv_params(w, 0),
        "gla_aup": jnp.pad(w["e_gla_a_up"][0], ((0, LANES - GLA_LR), (0, 0))).astype(BF16),
        "e_wo_a": w["e_w_out"][0][:RW].astype(BF16),
        "e_wo_b": w["e_w_out"][0][RW:].astype(BF16),
        "ff_w1": w["e_ff_w1"][0].astype(BF16),
        "ff_w3": w["e_ff_w3"][0].astype(BF16),
        "ff_w2": w["e_ff_w2"][0].astype(BF16),
        "o_w_q": wo[:, :QW].astype(BF16),
        "o_w_kv": wo[:, QW:QW + 2 * KW].astype(BF16),
        "o_w_u": wo[:, QW + 2 * KW:].astype(BF16),
        "glu_w": w["o_glu_w"][0].astype(BF16),
        "o_wo_a": w["o_w_out"][0][:QW].astype(BF16),
        "o_wo_b": w["o_w_out"][0][QW:].astype(BF16),
        "moe_w1": w["o_moe_w1"][0].astype(BF16),
        "moe_w3": w["o_moe_w3"][0].astype(BF16),
        "moe_w2": w["o_moe_w2"][0].astype(BF16),
    }


def _trunk(x3, st, w, pw, tm, chunk, nb, s5_prep_t, prompt_bias):
    bsz, t, d = x3.shape
    m = bsz * t
    x = x3.reshape(m, d)
    p_r, p_g = norm_proj(x, w["e_norm1"][0], [pw["e_w_rwkv"], pw["e_w_gla"]], tm)
    ya, s_rwkv = rwkv_mix(p_r, st["shift"], st["rwkv"], bsz, t, chunk, pw["rwkv"],
                          w["e_lnx_w"][0], w["e_lnx_b"][0], nb)
    s_shift = p_r.reshape(bsz, t, RSHIFT)[:, -1]
    yb, s_gla = gla_mix(p_g, st["gla"], bsz, t, chunk, pw["gla_aup"], w["e_gla_a_b"][0], w["e_gla_norm"][0], nb)
    x = ffn(x, ya, yb, pw["e_wo_a"], pw["e_wo_b"], w["e_norm2"][0], pw["ff_w1"], pw["ff_w3"], pw["ff_w2"],
            math.gcd(m, FFN_TOKENS), D_FF_CHUNK)
    q, kv, u = norm_proj_tiles(x, w["o_norm1"][0], [pw["o_w_q"], pw["o_w_kv"], pw["o_w_u"]], tm,
                               min(t, CHUNK_ROW_TOKENS))
    if st["win_k"] is None:
        yc, nk, nv = swa_prompt(q, kv, bsz, t, w["o_q_norm"][0], w["o_k_norm"][0], prompt_bias, w["o_sinks"][0])
    else:
        yc, nk, nv = swa_decode(q, kv, st["win_k"].reshape(bsz, WINDOW, KW), st["win_v"].reshape(bsz, WINDOW, KW),
                                bsz, t, w["o_q_norm"][0], w["o_k_norm"][0], w["rel_table"], w["o_sinks"][0], 8)
    yd, s5r, s5i = s5_mix(u, st["s5_re"], st["s5_im"], bsz, t, s5_prep_t, w["o_d"][0], pw["glu_w"],
                          w["o_glu_b"][0], tm)
    t_router = min(m, ROUTER_TOKENS)
    x, hn, gates, pos, counts = router(x, yc, yd, pw["o_wo_a"], pw["o_wo_b"], w["o_norm2"][0],
                                       w["o_router_w"][0], w["o_router_b"][0], t_router)
    x = moe(x, hn, gates, pos, counts, pw["moe_w1"], pw["moe_w3"], pw["moe_w2"], t_router, min(m, MOE_TOKENS),
            D_FF_EXPERT_CHUNK)
    kv_shape = (bsz, WINDOW, N_KV_HEADS, ATT_HD)
    return (x.reshape(bsz, t, d), s_rwkv[None], s_shift[None], s_gla[None], nk.reshape(kv_shape)[None],
            nv.reshape(kv_shape)[None], s5r[None], s5i[None])


def kernel(x_prompt, x_sample, state_rwkv, state_shift, state_gla, cache_win_k, cache_win_v, state_s5_re,
           state_s5_im, rel_table, e_norm1, e_w_in, e_mu, e_w0, e_w_up, e_a0, e_a_up, e_g_up, e_k_k, e_k_a, e_r_k,
           e_lnx_w, e_lnx_b, e_gla_a_up, e_gla_a_b, e_gla_norm, e_w_out, e_norm2, e_ff_w1, e_ff_w3, e_ff_w2,
           o_norm1, o_w_in, o_q_norm, o_k_norm, o_sinks, o_a_re, o_a_im, o_log_dt, o_b_re, o_b_im, o_c_re, o_c_im,
           o_d, o_glu_w, o_glu_b, o_w_out, o_norm2, o_router_w, o_router_b, o_moe_w1, o_moe_w3, o_moe_w2):
    w = dict(rel_table=rel_table, e_norm1=e_norm1, e_w_in=e_w_in, e_mu=e_mu, e_w0=e_w0, e_w_up=e_w_up, e_a0=e_a0,
             e_a_up=e_a_up, e_g_up=e_g_up, e_k_k=e_k_k, e_k_a=e_k_a, e_r_k=e_r_k, e_lnx_w=e_lnx_w, e_lnx_b=e_lnx_b,
             e_gla_a_up=e_gla_a_up, e_gla_a_b=e_gla_a_b, e_gla_norm=e_gla_norm, e_w_out=e_w_out, e_norm2=e_norm2,
             e_ff_w1=e_ff_w1, e_ff_w3=e_ff_w3, e_ff_w2=e_ff_w2, o_norm1=o_norm1, o_w_in=o_w_in, o_q_norm=o_q_norm,
             o_k_norm=o_k_norm, o_sinks=o_sinks, o_a_re=o_a_re, o_a_im=o_a_im, o_log_dt=o_log_dt, o_b_re=o_b_re,
             o_b_im=o_b_im, o_c_re=o_c_re, o_c_im=o_c_im, o_d=o_d, o_glu_w=o_glu_w, o_glu_b=o_glu_b,
             o_w_out=o_w_out, o_norm2=o_norm2, o_router_w=o_router_w, o_router_b=o_router_b, o_moe_w1=o_moe_w1,
             o_moe_w3=o_moe_w3, o_moe_w2=o_moe_w2)
    pw = _prepare_weights(w)
    bp, tp, _ = x_prompt.shape
    bs, ts, _ = x_sample.shape
    qi = np.arange(WINDOW)[:, None]
    kj = np.arange(2 * WINDOW)[None, :]
    prompt_bias = rel_bias(rel_table, qi + WINDOW - kj)
    zeros = lambda *shape: jnp.zeros(shape, F32)
    st_p = {"rwkv": zeros(bp, RW // RWKV_HD, RWKV_HD, RWKV_HD), "shift": zeros(bp, RSHIFT),
            "gla": zeros(bp, GLA_HEADS, GLA_DK, GLA_DV), "win_k": None, "win_v": None,
            "s5_re": zeros(bp, S5_G, S5_N), "s5_im": zeros(bp, S5_G, S5_N)}
    st_s = {"rwkv": state_rwkv[0], "shift": state_shift[0], "gla": state_gla[0], "win_k": cache_win_k[0],
            "win_v": cache_win_v[0], "s5_re": state_s5_re[0], "s5_im": state_s5_im[0]}
    s5_p, s5_s = s5_prep(w, 0, (S5_CHUNK, ts))
    out_p = _trunk(x_prompt, st_p, w, pw, math.gcd(bp * tp, TOKEN_TILE), PROMPT_CHUNK,
                   math.gcd(bp, PROMPT_ROWS_PER_STEP), s5_p, prompt_bias)
    out_s = _trunk(x_sample, st_s, w, pw, math.gcd(bs * ts, TOKEN_TILE), -(-ts // SUBLANES) * SUBLANES,
                   math.gcd(bs, SAMPLE_ROWS_PER_STEP), s5_s, None)
    res = [out_p[0], out_s[0]]
    for a, b in zip(out_p[1:], out_s[1:]):
        res += [a, b]
    return tuple(res)
```

```python
import functools
import math

import jax
import jax.numpy as jnp
import numpy as np
from jax import lax
from jax.experimental import pallas as pl
from jax.experimental.pallas import tpu as pltpu

F32 = jnp.float32
BF16 = jnp.bfloat16

LANES = 128
SUBLANES = 8
VMEM_LIMIT_BYTES = 56 * 1024 * 1024

EPS = 1e-6
RWKV_HD = 64
RWKV_GN_EPS = 64e-5
GLA_GATE_NORM = 16.0
ATT_HD = 64
WINDOW = 128
N_BUCKETS = 32
BUCKET_MAX_DIST = 128
S5_P = 16
S5_N = 64
S5_CHUNK = 16


def _cparams(sem):
    return pltpu.CompilerParams(dimension_semantics=sem, vmem_limit_bytes=VMEM_LIMIT_BYTES)


_NN = (((1,), (0,)), ((), ()))
_NT = (((1,), (1,)), ((), ()))
_TN = (((0,), (0,)), ((), ()))


def _dg(a, b, dims):
    return lax.dot_general(a, b, dims, preferred_element_type=F32)


def _bdot(a, b, dims=_NN):
    return _dg(a.astype(BF16), b.astype(BF16), dims)


def _split(a, n):
    terms = []
    r = a
    for _ in range(n):
        t = r.astype(BF16)
        terms.append(t)
        r = r - t.astype(F32)
    return terms


def _hdot(a, b, dims=_NN):
    a0, a1 = _split(a, 2)
    b0, b1 = _split(b, 2)
    return _dg(a0, b0, dims) + (_dg(a0, b1, dims) + _dg(a1, b0, dims))


def _xdot_l(a, e, dims=_NN):
    e = e.astype(BF16)
    a0, a1, a2 = _split(a, 3)
    return _dg(a0, e, dims) + (_dg(a1, e, dims) + _dg(a2, e, dims))


def _xdot_r(e, b, dims=_NN):
    e = e.astype(BF16)
    b0, b1, b2 = _split(b, 3)
    return _dg(e, b0, dims) + (_dg(e, b1, dims) + _dg(e, b2, dims))


def _iota(shape, axis):
    return lax.broadcasted_iota(jnp.int32, shape, axis)


def _seg_ones(n, seg):
    r = _iota((n, n), 0) // seg
    c = _iota((n, n), 1) // seg
    return jnp.where(r == c, 1.0, 0.0).astype(BF16)


def _seg_sum(x, seg):
    n = x.shape[-1]
    ones = _seg_ones(n, seg)
    x0, x1 = _split(x, 2)
    return _dg(x0, ones, _NN) + _dg(x1, ones, _NN)


def _sigmoid(x):
    return 1.0 / (1.0 + jnp.exp(-x))


def _silu(x):
    return x * _sigmoid(x)


def _softplus(x):
    return jnp.maximum(x, 0.0) + jnp.log(1.0 + jnp.exp(-jnp.abs(x)))


def _tri_incl(n):
    r = _iota((n, n), 0)
    c = _iota((n, n), 1)
    return jnp.where(c <= r, 1.0, 0.0).astype(BF16)


def _rms(x, g):
    return x * lax.rsqrt(jnp.mean(x * x, axis=-1, keepdims=True) + EPS) * g


def _norm_proj_kernel(n_w, x_ref, g_ref, *refs):
    xn = _rms(x_ref[...], g_ref[...]).astype(BF16)
    for w_ref, o_ref in zip(refs[:n_w], refs[n_w:]):
        o_ref[...] = _dg(xn, w_ref[...], _NN).astype(o_ref.dtype)


def norm_proj(x, g, ws_bf16, tm, out_dtypes=None):
    m, d = x.shape
    out_dtypes = out_dtypes or [F32] * len(ws_bf16)
    return pl.pallas_call(
        functools.partial(_norm_proj_kernel, len(ws_bf16)),
        grid=(m // tm,),
        in_specs=[pl.BlockSpec((tm, d), lambda i: (i, 0)),
                  pl.BlockSpec((1, d), lambda i: (0, 0))]
                 + [pl.BlockSpec(w.shape, lambda i: (0, 0)) for w in ws_bf16],
        out_specs=[pl.BlockSpec((tm, w.shape[1]), lambda i: (i, 0)) for w in ws_bf16],
        out_shape=[jax.ShapeDtypeStruct((m, w.shape[1]), dt) for w, dt in zip(ws_bf16, out_dtypes)],
        compiler_params=_cparams(("parallel",)), name="norm_proj",
    )(x, g.reshape(1, d), *ws_bf16)


CHUNK_ROW_TOKENS = 16


def _to_chunk_rows(src_ref, dst_ref, tpr):
    rows = src_ref.shape[1] // tpr
    for q in range(dst_ref.shape[0]):
        for i in range(CHUNK_ROW_TOKENS):
            cols = slice(i * LANES, (i + 1) * LANES)
            if i < tpr:
                dst_ref[q, :, cols] = src_ref[q, pl.ds(i, rows, stride=tpr), :]
            else:
                dst_ref[q, :, cols] = jnp.zeros((rows, LANES), dst_ref.dtype)


def _from_chunk_rows(src_ref, dst_ref, tpr):
    rows = dst_ref.shape[1] // tpr
    for q in range(src_ref.shape[0]):
        for i in range(tpr):
            dst_ref[q, pl.ds(i, rows, stride=tpr), :] = src_ref[q, :, i * LANES:(i + 1) * LANES]


def _norm_proj_tiles_kernel(n_w, tpr, x_ref, g_ref, *refs):
    xn = _rms(x_ref[...], g_ref[...]).astype(BF16)
    for w_ref, o_ref in zip(refs[:n_w - 1], refs[n_w:]):
        o_ref[...] = _dg(xn, w_ref[...], _NN)
    u_ref, u_scr = refs[-2], refs[-1]
    u = _dg(xn, refs[n_w - 1][...], _NN)
    for q in range(u_scr.shape[0]):
        u_scr[q] = u[:, q * LANES:(q + 1) * LANES]
    _to_chunk_rows(u_scr, u_ref, tpr)


def norm_proj_tiles(x, g, ws_bf16, tm, tpr):
    m, d = x.shape
    nu = ws_bf16[-1].shape[1]
    nt = nu // LANES
    cw = CHUNK_ROW_TOKENS * LANES
    return pl.pallas_call(
        functools.partial(_norm_proj_tiles_kernel, len(ws_bf16), tpr),
        grid=(m // tm,),
        in_specs=[pl.BlockSpec((tm, d), lambda i: (i, 0)),
                  pl.BlockSpec((1, d), lambda i: (0, 0))]
                 + [pl.BlockSpec(w.shape, lambda i: (0, 0)) for w in ws_bf16],
        out_specs=[pl.BlockSpec((tm, w.shape[1]), lambda i: (i, 0)) for w in ws_bf16[:-1]]
                  + [pl.BlockSpec((nt, tm // tpr, cw), lambda i: (0, i, 0))],
        out_shape=[jax.ShapeDtypeStruct((m, w.shape[1]), F32) for w in ws_bf16[:-1]]
                  + [jax.ShapeDtypeStruct((nt, m // tpr, cw), F32)],
        scratch_shapes=[pltpu.VMEM((nt, tm, LANES), F32)],
        compiler_params=_cparams(("parallel",)), name="norm_proj_tiles",
    )(x, g.reshape(1, d), *ws_bf16)


def _mixer_out(x_ref, ya_ref, yb_ref, wa_ref, wb_ref):
    return (x_ref[...] + _dg(ya_ref[...].astype(BF16), wa_ref[...], _NN)
            + _dg(yb_ref[...].astype(BF16), wb_ref[...], _NN))


def _ffn_kernel(x_ref, ya_ref, yb_ref, wa_ref, wb_ref, g_ref, w1_ref, w3_ref, w2_ref, o_ref, xn_scr):
    j = pl.program_id(1)

    @pl.when(j == 0)
    def _():
        x = _mixer_out(x_ref, ya_ref, yb_ref, wa_ref, wb_ref)
        xn_scr[...] = _rms(x, g_ref[...]).astype(BF16)
        o_ref[...] = x

    xn = xn_scr[...]
    h = _silu(_dg(xn, w1_ref[...], _NN)) * _dg(xn, w3_ref[...], _NN)
    o_ref[...] += _dg(h.astype(BF16), w2_ref[...], _NN)


def ffn(x, ya, yb, wa, wb, g, w1, w3, w2, tm, fc):
    m, d = x.shape
    dff = w1.shape[1]
    tok = lambda n: pl.BlockSpec((tm, n), lambda i, j: (i, 0))
    full = lambda a: pl.BlockSpec(a.shape, lambda i, j: (0, 0))
    wmode = pl.Buffered(1) if dff == fc else None
    return pl.pallas_call(
        _ffn_kernel,
        grid=(m // tm, dff // fc),
        in_specs=[tok(d), tok(ya.shape[1]), tok(yb.shape[1]), full(wa), full(wb),
                  pl.BlockSpec((1, d), lambda i, j: (0, 0)),
                  pl.BlockSpec((d, fc), lambda i, j: (0, j), pipeline_mode=wmode),
                  pl.BlockSpec((d, fc), lambda i, j: (0, j), pipeline_mode=wmode),
                  pl.BlockSpec((fc, d), lambda i, j: (j, 0), pipeline_mode=wmode)],
        out_specs=pl.BlockSpec((tm, d), lambda i, j: (i, 0)),
        out_shape=jax.ShapeDtypeStruct((m, d), F32),
        scratch_shapes=[pltpu.VMEM((tm, d), BF16)],
        compiler_params=_cparams(("parallel", "arbitrary")), name="ffn",
    )(x, ya, yb, wa, wb, g.reshape(1, d), w1, w3, w2)


def _router_kernel(n_exp, x_ref, ya_ref, yb_ref, wa_ref, wb_ref, g_ref, rw_ref, rb_ref,
                   xo_ref, hn_ref, gate_ref, pos_ref, cnt_ref):
    x = _mixer_out(x_ref, ya_ref, yb_ref, wa_ref, wb_ref)
    xo_ref[...] = x
    xn = _rms(x, g_ref[...])
    hn_ref[...] = xn.astype(BF16)
    logits = _hdot(xn, rw_ref[...]) + rb_ref[...]
    lane = _iota(logits.shape, 1)
    logits = jnp.where(lane < n_exp, logits, -jnp.inf)
    m1 = jnp.max(logits, axis=-1, keepdims=True)
    i1 = jnp.min(jnp.where(logits == m1, lane, LANES), axis=-1, keepdims=True)
    rest = jnp.where(lane == i1, -jnp.inf, logits)
    m2 = jnp.max(rest, axis=-1, keepdims=True)
    i2 = jnp.min(jnp.where(rest == m2, lane, LANES), axis=-1, keepdims=True)
    e2 = jnp.exp(m2 - m1)
    g1 = 1.0 / (1.0 + e2)
    g2 = e2 / (1.0 + e2)
    pick1 = lane == i1
    pick2 = lane == i2
    gate_ref[...] = jnp.where(pick1, g1, 0.0) + jnp.where(pick2, g2, 0.0)
    tm = logits.shape[0]
    sel = jnp.where(jnp.logical_or(pick1, pick2), 1.0, 0.0).astype(BF16)
    tr = _iota((tm, tm), 0)
    tc = _iota((tm, tm), 1)
    upper = jnp.where(tr <= tc, 1.0, 0.0).astype(BF16)
    eye = jnp.where(tr == tc, 1.0, 0.0).astype(BF16)
    rank_t = _dg(sel, upper, _TN)
    sel_t = _dg(sel, eye, _TN)
    pos_t = jnp.where(sel_t > 0.5, rank_t - 1.0, -1.0)
    pos_ref[...] = pos_t[0:SUBLANES, :]
    cnt_ref[0] = jnp.sum(sel.astype(F32), axis=0, keepdims=True)


def router(x, ya, yb, wa, wb, g, rw, rb, tm):
    m, d = x.shape
    n_exp = rw.shape[1]
    assert n_exp <= SUBLANES
    rw_pad = jnp.pad(rw, ((0, 0), (0, LANES - n_exp)))
    rb_pad = jnp.pad(rb, (0, LANES - n_exp)).reshape(1, LANES)
    tok = lambda n: pl.BlockSpec((tm, n), lambda i: (i, 0))
    full = lambda a: pl.BlockSpec(a.shape, lambda i: (0, 0))
    return pl.pallas_call(
        functools.partial(_router_kernel, n_exp),
        grid=(m // tm,),
        in_specs=[tok(d), tok(ya.shape[1]), tok(yb.shape[1]), full(wa), full(wb),
                  pl.BlockSpec((1, d), lambda i: (0, 0)),
                  pl.BlockSpec((d, LANES), lambda i: (0, 0)),
                  pl.BlockSpec((1, LANES), lambda i: (0, 0))],
        out_specs=[tok(d), tok(d), tok(LANES),
                   pl.BlockSpec((SUBLANES, tm), lambda i: (0, i)),
                   pl.BlockSpec((1, 1, LANES), lambda i: (i, 0, 0))],
        out_shape=[jax.ShapeDtypeStruct((m, d), F32),
                   jax.ShapeDtypeStruct((m, d), BF16), jax.ShapeDtypeStruct((m, LANES), F32),
                   jax.ShapeDtypeStruct((SUBLANES, m), F32),
                   jax.ShapeDtypeStruct((m // tm, 1, LANES), F32)],
        compiler_params=_cparams(("parallel",)), name="router",
    )(x, ya, yb, wa, wb, g.reshape(1, d), rw_pad, rb_pad)


MOE_ROWS = 128


def _moe_kernel(n_exp, cnt_ref, x_hbm, hn_ref, gate_ref, pos_ref, w1_ref, w3_ref, w2_ref, o_ref, xs_scr, y_scr):
    i = pl.program_id(0)
    e = pl.program_id(1)
    j = pl.program_id(2)
    nj = pl.num_programs(2)
    tm = hn_ref.shape[0]
    n_small = (cnt_ref[i * n_exp + e] + (MOE_ROWS - 1)) // MOE_ROWS

    @pl.when(jnp.logical_and(e == 0, j == 0))
    def _():
        pltpu.sync_copy(x_hbm.at[pl.ds(pl.multiple_of(i * tm, tm), tm), :], o_ref)

    def select(start, nrows):
        pos = pos_ref[pl.ds(e, 1), :]
        want = (start + _iota((nrows, tm), 0)).astype(F32)
        return jnp.where(pos == want, 1.0, 0.0).astype(BF16)

    def blocks(body):
        big = 4 * MOE_ROWS
        assert tm % big == 0

        def run_big(blk, carry):
            body(pl.multiple_of(blk * big, big), big)
            return carry
        lax.fori_loop(0, n_small // 4, run_big, 0)
        done = (n_small // 4) * big

        @pl.when(n_small % 4 >= 2)
        def _():
            body(pl.multiple_of(done, big), 2 * MOE_ROWS)

        @pl.when(n_small % 2 == 1)
        def _():
            body(pl.multiple_of(done + (n_small % 4 // 2) * 2 * MOE_ROWS, MOE_ROWS), MOE_ROWS)

    @pl.when(j == 0)
    def _():
        def gather(start, nrows):
            rows = pl.ds(start, nrows)
            xs_scr[rows, :] = _dg(select(start, nrows), hn_ref[...], _NN).astype(BF16)
            y_scr[rows, :] = jnp.zeros((nrows, y_scr.shape[1]), F32)
        blocks(gather)

    def expert(start, nrows):
        rows = pl.ds(start, nrows)
        xs = xs_scr[rows, :]
        h = _silu(_dg(xs, w1_ref[0], _NN)) * _dg(xs, w3_ref[0], _NN)
        y_scr[rows, :] += _dg(h.astype(BF16), w2_ref[0], _NN)
    blocks(expert)

    @pl.when(j == nj - 1)
    def _():
        gt = gate_ref[...]
        gcol = jnp.sum(jnp.where(_iota(gt.shape, 1) == e, gt, 0.0), axis=-1, keepdims=True)

        def scatter(start, nrows):
            o_ref[...] += gcol * _dg(select(start, nrows), y_scr[pl.ds(start, nrows), :].astype(BF16), _TN)
        blocks(scatter)


def moe(x, hn, gates, pos, counts, w1, w3, w2, tm_router, tm, fc):
    m, d = x.shape
    n_exp, _, dff = w1.shape
    nj = dff // fc
    ratio = tm // tm_router
    cnt = counts[:, 0, :n_exp].astype(jnp.int32).reshape(m // tm, ratio, n_exp)
    before = (jnp.cumsum(cnt, axis=1) - cnt).astype(F32)
    shift = jnp.repeat(before.reshape(m // tm_router, n_exp).T, tm_router, axis=1)
    shift = jnp.pad(shift, ((0, SUBLANES - n_exp), (0, 0)))
    pos = jnp.where(pos >= 0, pos + shift, pos)
    cnt = cnt.sum(axis=1).reshape(-1)
    grid_spec = pltpu.PrefetchScalarGridSpec(
        num_scalar_prefetch=1,
        grid=(m // tm, n_exp, nj),
        in_specs=[pl.BlockSpec(memory_space=pl.ANY),
                  pl.BlockSpec((tm, d), lambda i, e, j, c: (i, 0), pipeline_mode=pl.Buffered(1)),
                  pl.BlockSpec((tm, LANES), lambda i, e, j, c: (i, 0), pipeline_mode=pl.Buffered(1)),
                  pl.BlockSpec((SUBLANES, tm), lambda i, e, j, c: (0, i), pipeline_mode=pl.Buffered(1)),
                  pl.BlockSpec((1, d, fc), lambda i, e, j, c: (e, 0, j)),
                  pl.BlockSpec((1, d, fc), lambda i, e, j, c: (e, 0, j)),
                  pl.BlockSpec((1, fc, d), lambda i, e, j, c: (e, j, 0))],
        out_specs=pl.BlockSpec((tm, d), lambda i, e, j, c: (i, 0), pipeline_mode=pl.Buffered(1)),
        scratch_shapes=[pltpu.VMEM((tm, d), BF16), pltpu.VMEM((tm, d), F32)])
    return pl.pallas_call(
        functools.partial(_moe_kernel, n_exp),
        grid_spec=grid_spec,
        out_shape=jax.ShapeDtypeStruct((m, d), F32),
        compiler_params=_cparams(("parallel", "arbitrary", "arbitrary")), name="moe",
    )(cnt, x, hn, gates, pos, w1, w3, w2)


_RW_DOT_A = _bdot
_RW_DOT_T = _bdot
_RW_DOT_W = _bdot
_RW_DOT_S = _bdot


RWKV_SUB = 4
RW = 512
RSHIFT = 1792


def _rwkv_token_maps(xs, w0_ref, wup_ref, a0_ref, aup_ref, gup_ref, kk_ref, ka_ref, rk_ref):
    r = xs[:, 0:RW]
    k = xs[:, RW:2 * RW]
    v = xs[:, 2 * RW:3 * RW]
    lr = xs[:, 3 * RW:3 * RW + LANES]
    gd = xs[:, 3 * RW + LANES:3 * RW + 2 * LANES]
    w_pre = w0_ref[...] + _bdot(jnp.tanh(lr), wup_ref[...])
    logw = -jnp.exp(-_softplus(-w_pre) - 0.5)
    a = _sigmoid(a0_ref[...] + _bdot(lr, aup_ref[...]))
    g = _bdot(_sigmoid(gd), gup_ref[...])
    kkr = k * kk_ref[...]
    kk = kkr / jnp.maximum(jnp.sqrt(_seg_sum(kkr * kkr, RWKV_HD)), 1e-12)
    k2 = k * (1.0 + (a - 1.0) * ka_ref[...])
    bonus = _seg_sum(r * k2 * rk_ref[...], RWKV_HD) * v
    return r, logw, k2, v, -kk, kk * a, g, bonus


def _rwkv_kernel(L, sub, nb, n_pairs, t_real, p_ref, prev0_ref, s0_ref, mu_ref, w0_ref, wup_ref, a0_ref, aup_ref,
                 gup_ref, kk_ref, ka_ref, rk_ref, lnw_ref, lnb_ref, y_ref, sf_ref, s_scr, prev_scr):
    c = pl.program_id(1)
    nc = pl.num_programs(1)
    L2 = 2 * L
    lane = _iota((1, LANES), 1)
    m0 = jnp.where(lane < RWKV_HD, 1.0, 0.0)
    m1 = 1.0 - m0
    rr = _iota((L2, L2), 0)
    cc = _iota((L2, L2), 1)
    same = (rr // L) == (cc // L)
    strict = jnp.logical_and(same, (cc % L) < (rr % L))
    incl = jnp.logical_and(same, (cc % L) <= (rr % L))
    eye2 = jnp.where(rr == cc, 1.0, 0.0)
    r128 = _iota((LANES, LANES), 0)
    c128 = _iota((LANES, LANES), 1)
    blk128 = (r128 // RWKV_HD) == (c128 // RWKV_HD)
    diag128 = r128 == c128
    fmat = jnp.where(_iota((LANES, RWKV_HD), 0) % RWKV_HD == _iota((LANES, RWKV_HD), 1), 1.0, 0.0)
    tri = _tri_incl(L)

    @pl.when(c == 0)
    def _():
        for bb in range(nb):
            prev_scr[bb, 0:1, :] = prev0_ref[bb]
            for j in range(n_pairs):
                s0 = s0_ref[bb, j]
                st = _xdot_r(fmat, s0, _NT)
                s_scr[bb * n_pairs + j] = jnp.where(blk128, st, 0.0)

    pairs = range(nb * n_pairs)
    bbs = [i // n_pairs for i in pairs]
    sls = [slice((i % n_pairs) * LANES, (i % n_pairs + 1) * LANES) for i in pairs]
    rowi = _iota((L, 1), 0)
    consts = (m0, m1, strict, incl, eye2, diag128, tri, rowi, pairs, bbs, sls)
    refs = (p_ref, mu_ref, w0_ref, wup_ref, a0_ref, aup_ref, gup_ref, kk_ref, ka_ref, rk_ref, lnw_ref, lnb_ref,
            y_ref, s_scr, prev_scr)
    for part in range(sub):
        _rwkv_chunk_body(L, nb, n_pairs, t_real, c * sub + part, slice(part * L, (part + 1) * L), consts, refs)

    @pl.when(c == nc - 1)
    def _():
        for i in pairs:
            sf_ref[bbs[i], i % n_pairs] = _xdot_l(s_scr[i], fmat, _TN)


def _rwkv_chunk_body(L, nb, n_pairs, t_real, chunk, tsl, consts, refs):
    (m0, m1, strict, incl, eye2, diag128, tri, rowi, pairs, bbs, sls) = consts
    (p_ref, mu_ref, w0_ref, wup_ref, a0_ref, aup_ref, gup_ref, kk_ref, ka_ref, rk_ref, lnw_ref, lnb_ref,
     y_ref, s_scr, prev_scr) = refs
    L2 = 2 * L

    def bd(x):
        return jnp.concatenate([x * m0, x * m1], axis=0)

    xs = []
    for bb in range(nb):
        p = p_ref[bb, tsl, :]
        prev = jnp.where(rowi == 0, prev_scr[bb, 0:1, :], pltpu.roll(p, 1, 0))
        prev_scr[bb, 0:1, :] = p[L - 1:L, :]
        xs.append(p + (prev - p) * mu_ref[...])
    maps = _rwkv_token_maps(jnp.concatenate(xs, axis=0), w0_ref, wup_ref, a0_ref, aup_ref, gup_ref,
                            kk_ref, ka_ref, rk_ref)
    if t_real % L != 0:
        valid = (chunk * L + _iota((nb * L, 1), 0) % L) < t_real
        maps = tuple(jnp.where(valid, z, 0.0) for z in maps)
    toks = [tuple(z[bb * L:(bb + 1) * L] for z in maps) for bb in range(nb)]

    tok = lambda j, which: toks[bbs[j]][which][:, sls[j]]
    lw = [tok(j, 1) for j in pairs]
    b = [_xdot_r(tri, x) for x in lw]
    bl = [x[L - 1:L, :] for x in b]
    e_b = [jnp.exp(x) for x in b]
    e_nb = [jnp.exp(-x) for x in b]
    e_lb = [jnp.exp(bl[j] - b[j]) for j in pairs]
    at = [bd(tok(j, 4) * jnp.exp(b[j] - lw[j])) for j in pairs]
    rt = [bd(tok(j, 0) * e_b[j]) for j in pairs]
    bh = [bd(tok(j, 5) * e_nb[j]) for j in pairs]
    kh = [bd(tok(j, 2) * e_nb[j]) for j in pairs]
    bt = [bd(tok(j, 5) * e_lb[j]) for j in pairs]
    kt = [bd(tok(j, 2) * e_lb[j]) for j in pairs]
    vb = [bd(tok(j, 3)) for j in pairs]
    if L2 % LANES == 0:
        gq = [_RW_DOT_A(jnp.concatenate([at[j], rt[j]], axis=0), jnp.concatenate([bh[j], kh[j]], axis=0), _NT)
              for j in pairs]
        a_ab = [jnp.where(strict, g[0:L2, 0:L2], 0.0) for g in gq]
        a_ak = [jnp.where(strict, g[0:L2, L2:2 * L2], 0.0) for g in gq]
        a_rb = [jnp.where(incl, g[L2:2 * L2, 0:L2], 0.0) for g in gq]
        a_rk = [jnp.where(incl, g[L2:2 * L2, L2:2 * L2], 0.0) for g in gq]
    else:
        a_ab = [jnp.where(strict, _RW_DOT_A(at[j], bh[j], _NT), 0.0) for j in pairs]
        a_ak = [jnp.where(strict, _RW_DOT_A(at[j], kh[j], _NT), 0.0) for j in pairs]
        a_rb = [jnp.where(incl, _RW_DOT_A(rt[j], bh[j], _NT), 0.0) for j in pairs]
        a_rk = [jnp.where(incl, _RW_DOT_A(rt[j], kh[j], _NT), 0.0) for j in pairs]
    x = a_ab
    tinv = [eye2 + a for a in a_ab]
    span = 2
    while span < L:
        x = [_RW_DOT_T(xx, xx) for xx in x]
        tinv = [tinv[j] + _RW_DOT_T(tinv[j], x[j]) for j in pairs]
        span *= 2
    akv = [_RW_DOT_W(a_ak[j], vb[j]) for j in pairs]
    wuv = [_RW_DOT_W(tinv[j], jnp.concatenate([at[j], akv[j]], axis=1)) for j in pairs]
    mn = [_RW_DOT_W(bt[j], wuv[j], _TN) for j in pairs]
    qy = [_RW_DOT_W(a_rb[j], wuv[j]) for j in pairs]
    mm = [jnp.where(diag128, jnp.exp(bl[j]), 0.0) + mn[j][:, 0:LANES] for j in pairs]
    nn = [mn[j][:, LANES:2 * LANES] + _RW_DOT_W(kt[j], vb[j], _TN) for j in pairs]
    q = [rt[j] + qy[j][:, 0:LANES] for j in pairs]
    yv = [qy[j][:, LANES:2 * LANES] + _RW_DOT_W(a_rk[j], vb[j]) for j in pairs]
    s = [s_scr[j] for j in pairs]
    ybd = [_RW_DOT_S(q[j], s[j]) + yv[j] for j in pairs]
    for j in pairs:
        s_scr[j] = _RW_DOT_S(mm[j], s[j]) + nn[j]

    y = jnp.concatenate(
        [jnp.concatenate([ybd[bb * n_pairs + jj][0:L] + ybd[bb * n_pairs + jj][L:L2] for jj in range(n_pairs)],
                         axis=1) for bb in range(nb)], axis=0)
    mean = _seg_sum(y, RWKV_HD) * (1.0 / RWKV_HD)
    d = y - mean
    var = _seg_sum(d * d, RWKV_HD) * (1.0 / RWKV_HD)
    ya = (d * lax.rsqrt(var + RWKV_GN_EPS) * lnw_ref[...] + lnb_ref[...] + maps[7]) * maps[6]
    for bb in range(nb):
        y_ref[bb, tsl, :] = ya[bb * L:(bb + 1) * L]


def rwkv_params(w, i):
    z64 = jnp.zeros((RWKV_HD, RW), F32)
    return {
        "mu": w["e_mu"][i].reshape(1, RSHIFT),
        "w0": w["e_w0"][i].reshape(1, RW),
        "wup": jnp.concatenate([w["e_w_up"][i], z64], 0).astype(BF16),
        "a0": w["e_a0"][i].reshape(1, RW),
        "aup": jnp.concatenate([z64, w["e_a_up"][i]], 0).astype(BF16),
        "gup": w["e_g_up"][i].astype(BF16),
        "k_k": w["e_k_k"][i].reshape(1, RW),
        "k_a": w["e_k_a"][i].reshape(1, RW),
        "r_k": w["e_r_k"][i].reshape(1, RW),
    }


def rwkv_mix(p_flat, prev, s0, bsz, t, L, prm, lnx_w, lnx_b, nb=1):
    n_pairs = RW // LANES
    tp = -(-t // L) * L
    sub = math.gcd(tp // L, RWKV_SUB)
    p3 = p_flat.reshape(bsz, t, RSHIFT)
    if tp != t:
        p3 = jnp.pad(p3, ((0, 0), (0, tp - t), (0, 0)))
    row = lambda n: pl.BlockSpec((1, n), lambda b, c: (0, 0))
    lora = pl.BlockSpec((LANES, RW), lambda b, c: (0, 0))
    st_spec = pl.BlockSpec((nb, n_pairs, LANES, RWKV_HD), lambda b, c: (b, 0, 0, 0))
    y, s_fin = pl.pallas_call(
        functools.partial(_rwkv_kernel, L, sub, nb, n_pairs, t),
        grid=(bsz // nb, tp // (sub * L)),
        in_specs=[pl.BlockSpec((nb, sub * L, RSHIFT), lambda b, c: (b, c, 0)),
                  pl.BlockSpec((nb, 1, RSHIFT), lambda b, c: (b, 0, 0)),
                  st_spec,
                  row(RSHIFT), row(RW), lora, row(RW), lora, lora, row(RW), row(RW), row(RW), row(RW), row(RW)],
        out_specs=[pl.BlockSpec((nb, sub * L, RW), lambda b, c: (b, c, 0)), st_spec],
        out_shape=[jax.ShapeDtypeStruct((bsz, tp, RW), F32),
                   jax.ShapeDtypeStruct((bsz, n_pairs, LANES, RWKV_HD), F32)],
        scratch_shapes=[pltpu.VMEM((nb * n_pairs, LANES, LANES), F32),
                        pltpu.VMEM((nb, SUBLANES, RSHIFT), F32)],
        compiler_params=_cparams(("parallel", "arbitrary")), name="rwkv",
    )(p3, prev.reshape(bsz, 1, RSHIFT), s0.reshape(bsz, n_pairs, LANES, RWKV_HD),
      prm["mu"], prm["w0"], prm["wup"], prm["a0"], prm["aup"], prm["gup"], prm["k_k"], prm["k_a"], prm["r_k"],
      lnx_w.reshape(1, RW), lnx_b.reshape(1, RW))
    return y[:, :t].reshape(bsz * t, RW), s_fin.reshape(bsz, RW // RWKV_HD, RWKV_HD, RWKV_HD)


N_Q_HEADS = 8
N_KV_HEADS = 2
Q_PER_KV = N_Q_HEADS // N_KV_HEADS
QW = N_Q_HEADS * ATT_HD
KW = N_KV_HEADS * ATT_HD
NEG = -1e30


def _t5_bucket_np(dist):
    n = np.maximum(dist, 0)
    max_exact = N_BUCKETS // 2
    nf = np.maximum(n, 1).astype(np.float32)
    large = max_exact + (np.log(nf / np.float32(max_exact)) / np.float32(math.log(BUCKET_MAX_DIST / max_exact))
                         * np.float32(N_BUCKETS - max_exact)).astype(np.int32)
    large = np.minimum(large, N_BUCKETS - 1)
    return np.where(n < max_exact, n, large)


def _bias_kernel(rt_ref, oh_ref, o_ref):
    o_ref[...] = _xdot_l(rt_ref[...], oh_ref[...])


def rel_bias(rel_table, dist):
    bucket = _t5_bucket_np(dist).reshape(-1)
    n = bucket.shape[0]
    onehot = jnp.asarray((np.arange(N_BUCKETS)[:, None] == bucket[None, :]).astype(np.float32), BF16)
    out = pl.pallas_call(
        _bias_kernel,
        out_shape=jax.ShapeDtypeStruct((N_Q_HEADS, n), F32),
    )(rel_table.T, onehot)
    return out.reshape((N_Q_HEADS,) + dist.shape)


def _head_norm(x, w_row):
    return x * lax.rsqrt(_seg_sum(x * x, ATT_HD) * (1.0 / ATT_HD) + EPS) * w_row


def _swa_prompt_kernel(nqb, q_ref, kvc_ref, kvp_ref, qw_ref, kw_ref, bias_ref, sink_ref, o_ref, ko_ref, vo_ref):
    i = pl.program_id(1)
    qn = _head_norm(q_ref[0], qw_ref[...])
    kvc = kvc_ref[0]
    kvp = kvp_ref[0]
    kn = [_head_norm(kvp[:, 0:KW], kw_ref[...])]
    vs = [kvp[:, KW:2 * KW]]
    kcn = _head_norm(kvc[:, 0:KW], kw_ref[...])
    for j in range(nqb):
        kn.append(kcn[j * WINDOW:(j + 1) * WINDOW])
        vs.append(kvc[j * WINDOW:(j + 1) * WINDOW, KW:2 * KW])
    kcat = [jnp.concatenate([kn[j], kn[j + 1]], axis=0).astype(BF16) for j in range(nqb)]
    vcat = [jnp.concatenate([vs[j], vs[j + 1]], axis=0).astype(BF16) for j in range(nqb)]
    qi = _iota((WINDOW, 2 * WINDOW), 0)
    kj = _iota((WINDOW, 2 * WINDOW), 1)
    dist = qi + WINDOW - kj
    band = jnp.logical_and(dist >= 0, dist < WINDOW)
    valid = [jnp.logical_and(band, jnp.logical_or(kj >= WINDOW, i > 0))] + [band] * (nqb - 1)
    lane = _iota((1, LANES), 1)
    masks = (jnp.where(lane < ATT_HD, 1.0, 0.0), jnp.where(lane < ATT_HD, 0.0, 1.0))
    scale = ATT_HD ** -0.5
    heads = range(N_Q_HEADS)
    units = [(j, h) for j in range(nqb) for h in heads]
    kv_of = lambda h: h // Q_PER_KV
    qts = [qn[j * WINDOW:(j + 1) * WINDOW, (h // 2) * LANES:(h // 2 + 1) * LANES] for j, h in units]
    qts = [pltpu.roll(qts[u], ATT_HD, 1) if h % 2 != kv_of(h) else qts[u] for u, (j, h) in enumerate(units)]
    qms = [(qts[u] * masks[kv_of(h)]).astype(BF16) for u, (j, h) in enumerate(units)]
    s_all = [_dg(jnp.concatenate(qms[j * N_Q_HEADS:(j + 1) * N_Q_HEADS], axis=0), kcat[j], _NT) * scale
             for j in range(nqb)]
    logits = [jnp.where(valid[j], s_all[j][h * WINDOW:(h + 1) * WINDOW] + bias_ref[h], NEG) for j, h in units]
    sinks = [sink_ref[h:h + 1, 0:1] for j, h in units]
    mx = [jnp.maximum(jnp.max(logits[u], axis=-1, keepdims=True), sinks[u]) for u in range(len(units))]
    pr = [jnp.exp(logits[u] - mx[u]) for u in range(len(units))]
    den = [jnp.sum(pr[u], axis=-1, keepdims=True) + jnp.exp(sinks[u] - mx[u]) for u in range(len(units))]
    probs = [(pr[u] * (1.0 / den[u])).astype(BF16) for u in range(len(units))]
    o_all = [_dg(jnp.concatenate(probs[j * N_Q_HEADS:(j + 1) * N_Q_HEADS], axis=0), vcat[j], _NN)
             for j in range(nqb)]
    os_ = [o_all[j][h * WINDOW:(h + 1) * WINDOW] for j, h in units]
    os_ = [pltpu.roll(os_[u], ATT_HD, 1) if h % 2 != kv_of(h) else os_[u] for u, (j, h) in enumerate(units)]
    for j in range(nqb):
        for jq in range(QW // LANES):
            o_ref[0, j * WINDOW:(j + 1) * WINDOW, jq * LANES:(jq + 1) * LANES] = (
                os_[j * N_Q_HEADS + 2 * jq] * masks[0] + os_[j * N_Q_HEADS + 2 * jq + 1] * masks[1])
    ko_ref[0] = kn[nqb]
    vo_ref[0] = vs[nqb]


SWA_QB = 2


def swa_prompt(q, kv, bsz, t, q_norm, k_norm, bias, sinks):
    nqb = math.gcd(t // WINDOW, SWA_QB)
    rows = nqb * WINDOW
    nb = t // rows
    q3 = q.reshape(bsz, t, QW)
    kv3 = kv.reshape(bsz, t, 2 * KW)
    o, ko, vo = pl.pallas_call(
        functools.partial(_swa_prompt_kernel, nqb),
        grid=(bsz, nb),
        in_specs=[pl.BlockSpec((1, rows, QW), lambda b, i: (b, i, 0)),
                  pl.BlockSpec((1, rows, 2 * KW), lambda b, i: (b, i, 0)),
                  pl.BlockSpec((1, WINDOW, 2 * KW), lambda b, i: (b, jnp.maximum(i * nqb - 1, 0), 0)),
                  pl.BlockSpec((1, QW), lambda b, i: (0, 0)),
                  pl.BlockSpec((1, KW), lambda b, i: (0, 0)),
                  pl.BlockSpec((N_Q_HEADS, WINDOW, 2 * WINDOW), lambda b, i: (0, 0, 0)),
                  pl.BlockSpec((N_Q_HEADS, LANES), lambda b, i: (0, 0))],
        out_specs=[pl.BlockSpec((1, rows, QW), lambda b, i: (b, i, 0)),
                   pl.BlockSpec((1, WINDOW, KW), lambda b, i: (b, 0, 0)),
                   pl.BlockSpec((1, WINDOW, KW), lambda b, i: (b, 0, 0))],
        out_shape=[jax.ShapeDtypeStruct((bsz, t, QW), F32),
                   jax.ShapeDtypeStruct((bsz, WINDOW, KW), F32),
                   jax.ShapeDtypeStruct((bsz, WINDOW, KW), F32)],
        compiler_params=_cparams(("parallel", "arbitrary")), name="swa_prompt",
    )(q3, kv3, kv3, jnp.tile(q_norm, N_Q_HEADS).reshape(1, QW), jnp.tile(k_norm, N_KV_HEADS).reshape(1, KW),
      bias, jnp.broadcast_to(sinks[:, None], (N_Q_HEADS, LANES)))
    return o.reshape(bsz * t, QW), ko, vo


DEC_TP = 8


def _swa_decode_kernel(nbt, t_real, q_ref, kv_ref, ck_ref, cv_ref, qw_ref, kw_ref, bc_ref, bn_ref, sink_ref,
                       o_ref, ko_ref, vo_ref):
    rows = N_Q_HEADS * DEC_TP
    tq = _iota((rows, WINDOW), 0) % DEC_TP
    valid_c = _iota((rows, WINDOW), 1) > tq
    jn = _iota((rows, DEC_TP), 1)
    valid_n = jnp.logical_and(jn <= _iota((rows, DEC_TP), 0) % DEC_TP, jn < t_real)
    lane = _iota((1, LANES), 1)
    masks = (jnp.where(lane < ATT_HD, 1.0, 0.0), jnp.where(lane < ATT_HD, 0.0, 1.0))
    row8 = _iota((DEC_TP, 1), 0)
    scale = ATT_HD ** -0.5
    bias_c = jnp.concatenate([bc_ref[kv] for kv in range(N_KV_HEADS)], axis=0)
    bias_n = jnp.concatenate([bn_ref[kv][:, 0:DEC_TP] for kv in range(N_KV_HEADS)], axis=0)
    sink = jnp.concatenate([sink_ref[kv][:, 0:1] for kv in range(N_KV_HEADS)], axis=0)
    bs = range(nbt)
    heads = range(N_Q_HEADS)
    qn_all = _head_norm(q_ref[...].reshape(nbt * DEC_TP, QW), qw_ref[...])
    kvn_all = kv_ref[...].reshape(nbt * DEC_TP, 2 * KW)
    knew_all = _head_norm(kvn_all[:, 0:KW], kw_ref[...])
    knew = [knew_all[b * DEC_TP:(b + 1) * DEC_TP] for b in bs]
    vnew = [kvn_all[b * DEC_TP:(b + 1) * DEC_TP, KW:2 * KW] for b in bs]
    kc = [ck_ref[b] for b in bs]
    vc = [cv_ref[b] for b in bs]

    def stack_q(b):
        pieces = []
        for h in heads:
            qt = qn_all[b * DEC_TP:(b + 1) * DEC_TP, (h // 2) * LANES:(h // 2 + 1) * LANES]
            if h % 2 != h // Q_PER_KV:
                qt = pltpu.roll(qt, ATT_HD, 1)
            pieces.append(qt * masks[h // Q_PER_KV])
        return jnp.concatenate(pieces, axis=0)

    qs = [stack_q(b) for b in bs]
    l_c = [jnp.where(valid_c, _bdot(qs[b], kc[b], _NT) * scale + bias_c, NEG) for b in bs]
    l_n = [jnp.where(valid_n, _bdot(qs[b], knew[b], _NT) * scale + bias_n, NEG) for b in bs]
    mx = [jnp.maximum(jnp.maximum(jnp.max(l_c[b], axis=-1, keepdims=True),
                                  jnp.max(l_n[b], axis=-1, keepdims=True)), sink) for b in bs]
    p_c = [jnp.exp(l_c[b] - mx[b]) for b in bs]
    p_n = [jnp.exp(l_n[b] - mx[b]) for b in bs]
    inv = [1.0 / (jnp.sum(p_c[b], axis=-1, keepdims=True) + jnp.sum(p_n[b], axis=-1, keepdims=True)
                  + jnp.exp(sink - mx[b])) for b in bs]
    o = [_bdot(p_c[b] * inv[b], vc[b]) + _bdot(p_n[b] * inv[b], vnew[b]) for b in bs]
    for b in bs:
        for jq in range(QW // LANES):
            parts = []
            for h in (2 * jq, 2 * jq + 1):
                piece = o[b][h * DEC_TP:(h + 1) * DEC_TP]
                if h % 2 != h // Q_PER_KV:
                    piece = pltpu.roll(piece, ATT_HD, 1)
                parts.append(piece * masks[h % 2])
            o_ref[b, :, jq * LANES:(jq + 1) * LANES] = parts[0] + parts[1]
    for b in bs:
        for cache, new, out in ((kc[b], knew[b], ko_ref), (vc[b], vnew[b], vo_ref)):
            shifted = pltpu.roll(cache, WINDOW - t_real, 0)
            new_r = pltpu.roll(new, DEC_TP - t_real, 0)
            out[b, 0:WINDOW - DEC_TP] = shifted[0:WINDOW - DEC_TP]
            out[b, WINDOW - DEC_TP:WINDOW] = jnp.where(row8 >= DEC_TP - t_real, new_r,
                                                       shifted[WINDOW - DEC_TP:WINDOW])


def swa_decode(q, kv, cache_k, cache_v, bsz, t, q_norm, k_norm, rel_table, sinks, nbt):
    pad = ((0, 0), (0, DEC_TP - t), (0, 0))
    q3 = jnp.pad(q.reshape(bsz, t, QW), pad)
    kv3 = jnp.pad(kv.reshape(bsz, t, 2 * KW), pad)
    kpos = np.concatenate([np.arange(WINDOW) - WINDOW, np.arange(DEC_TP)])
    dist = np.arange(DEC_TP)[:, None] - kpos[None, :]
    bias = rel_bias(rel_table, dist)
    rows = Q_PER_KV * DEC_TP
    bias = bias.reshape(N_KV_HEADS, rows, WINDOW + DEC_TP)
    bias_c = bias[:, :, :WINDOW]
    bias_n = jnp.pad(bias[:, :, WINDOW:], ((0, 0), (0, 0), (0, LANES - DEC_TP)))
    sink_rows = jnp.broadcast_to(sinks.reshape(N_KV_HEADS, Q_PER_KV, 1, 1),
                                 (N_KV_HEADS, Q_PER_KV, DEC_TP, LANES)).reshape(N_KV_HEADS, rows, LANES)
    full3 = lambda shape: pl.BlockSpec(shape, lambda i: (0, 0, 0))
    o, ko, vo = pl.pallas_call(
        functools.partial(_swa_decode_kernel, nbt, t),
        grid=(bsz // nbt,),
        in_specs=[pl.BlockSpec((nbt, DEC_TP, QW), lambda i: (i, 0, 0)),
                  pl.BlockSpec((nbt, DEC_TP, 2 * KW), lambda i: (i, 0, 0)),
                  pl.BlockSpec((nbt, WINDOW, KW), lambda i: (i, 0, 0)),
                  pl.BlockSpec((nbt, WINDOW, KW), lambda i: (i, 0, 0)),
                  pl.BlockSpec((1, QW), lambda i: (0, 0)),
                  pl.BlockSpec((1, KW), lambda i: (0, 0)),
                  full3((N_KV_HEADS, rows, WINDOW)),
                  full3((N_KV_HEADS, rows, LANES)),
                  full3((N_KV_HEADS, rows, LANES))],
        out_specs=[pl.BlockSpec((nbt, DEC_TP, QW), lambda i: (i, 0, 0)),
                   pl.BlockSpec((nbt, WINDOW, KW), lambda i: (i, 0, 0)),
                   pl.BlockSpec((nbt, WINDOW, KW), lambda i: (i, 0, 0))],
        out_shape=[jax.ShapeDtypeStruct((bsz, DEC_TP, QW), F32),
                   jax.ShapeDtypeStruct((bsz, WINDOW, KW), F32),
                   jax.ShapeDtypeStruct((bsz, WINDOW, KW), F32)],
        compiler_params=_cparams(("parallel",)), name="swa_decode",
    )(q3, kv3, cache_k, cache_v, jnp.tile(q_norm, N_Q_HEADS).reshape(1, QW),
      jnp.tile(k_norm, N_KV_HEADS).reshape(1, KW), bias_c, bias_n, sink_rows)
    return o[:, :t].reshape(bsz * t, QW), ko, vo


S5_G = 32
S5_W = S5_G * S5_P
S5_CP = S5_CHUNK * S5_P
S5_PK = 2 * S5_N
S5_HW = S5_G * S5_PK


S5_QT = S5_W // LANES
S5_GT = LANES // S5_P
S5_XW = S5_CHUNK * LANES
S5_HQ = S5_GT * S5_PK


def _s5_group_maps(t_effs, a2, ldt, b1, b2, c1, c2):
    L = S5_CHUNK
    ar2 = a2[0:1, :]
    ai2 = a2[1:2, :]
    step = jnp.exp(ldt)
    mi = _iota((3 * SUBLANES, S5_PK), 0).astype(F32)
    mag = jnp.exp(mi * (step * ar2))
    ang = mi * (step * ai2)
    pwa = mag * jnp.cos(ang)
    pwb = mag * jnp.sin(ang)
    abr = pwa[1:2]
    abi = pwb[1:2]
    den = ar2 * ar2 + ai2 * ai2
    fa = ((abr - 1.0) * ar2 + abi * ai2) / den
    fb = (abi * ar2 - (abr - 1.0) * ai2) / den
    bp1 = b1 * fa + b2 * fb
    bp2 = b2 * fa - b1 * fb
    cpow = [c1 * pwa[m:m + 1] + c2 * pwb[m:m + 1] for m in range(L + 1)]
    kern_t = _hdot(bp1, jnp.concatenate(cpow[0:L], axis=0), _NT)
    sgn = jnp.where(_iota((1, S5_PK), 1) < S5_N, -1.0, 1.0)
    kbs, als = [], []
    for t_eff in t_effs:
        kbs.append([bp1 * pwa[max(t_eff - 1 - i, 0):max(t_eff - 1 - i, 0) + 1]
                    + bp2 * pwb[max(t_eff - 1 - i, 0):max(t_eff - 1 - i, 0) + 1] for i in range(L)])
        als.append(jnp.concatenate([pwa[t_eff:t_eff + 1], sgn * pwb[t_eff:t_eff + 1]], axis=0))
    return kern_t, kbs, als, cpow[1:L + 1]


def _s5_prep_kernel(t_effs, a_ref, ldt_ref, b1_ref, b2_ref, c1_ref, c2_ref, bd_ref, kc_ref, *rest):
    L = S5_CHUNK
    n_t = len(t_effs)
    kb_refs = rest[:n_t]
    al_refs = rest[n_t:2 * n_t]
    lane = _iota((S5_P, LANES), 1)
    for kb_ref in kb_refs:
        kb_ref[0] = jnp.zeros(kb_ref.shape[1:], kb_ref.dtype)
    bd_rows = [[] for _ in range(L)]
    kct_rows = [[] for _ in range(L)]
    for g in range(S5_GT):
        kern_t, kbs, als, kct = _s5_group_maps(t_effs, a_ref[g], ldt_ref[g], b1_ref[g], b2_ref[g],
                                               c1_ref[g], c2_ref[g])
        in_group = jnp.logical_and(lane >= g * S5_P, lane < (g + 1) * S5_P)
        for tau in range(L):
            shift = (g * S5_P - tau * S5_P) % S5_CP
            moved = pltpu.roll(kern_t, shift, 1) if shift else kern_t
            bd_rows[tau].append(jnp.where(in_group, moved[:, 0:LANES], 0.0))
        for k in range(n_t):
            for i in range(L):
                kb_refs[k][0, i * LANES + g * S5_P:i * LANES + (g + 1) * S5_P, g * S5_PK:(g + 1) * S5_PK] = (
                    kbs[k][i].astype(kb_refs[k].dtype))
            al_refs[k][g] = als[k]
        zl = jnp.zeros((S5_P, g * S5_PK), F32)
        zr = jnp.zeros((S5_P, (S5_GT - 1 - g) * S5_PK), F32)
        for t in range(L):
            parts = ([zl] if g else []) + [kct[t]] + ([zr] if g < S5_GT - 1 else [])
            kct_rows[t].append(jnp.concatenate(parts, axis=1))
    for tau in range(L):
        bd_ref[0, tau] = jnp.concatenate(bd_rows[tau], axis=0).astype(bd_ref.dtype)
    for t in range(L):
        blk_t = jnp.concatenate(kct_rows[t], axis=0)
        kc_ref[0, :, t * LANES:(t + 1) * LANES] = blk_t.T.astype(kc_ref.dtype)


def s5_prep(w, i, t_effs):
    dup = lambda z: jnp.concatenate([z, z], axis=-1)
    a = jnp.stack([dup(w["o_a_re"][i]), dup(w["o_a_im"][i])], axis=1)
    ldt = jnp.broadcast_to(w["o_log_dt"][i][:, None, None], (S5_G, 1, S5_PK))
    bt_re = jnp.swapaxes(w["o_b_re"][i], 1, 2)
    bt_im = jnp.swapaxes(w["o_b_im"][i], 1, 2)
    b1 = jnp.concatenate([bt_re, bt_im], -1)
    b2 = jnp.concatenate([-bt_im, bt_re], -1)
    c_re, c_im = w["o_c_re"][i], w["o_c_im"][i]
    c1 = jnp.concatenate([c_re, -c_im], -1)
    c2 = jnp.concatenate([-c_im, -c_re], -1)
    n_t = len(t_effs)
    g3 = lambda r, c: pl.BlockSpec((S5_GT, r, c), lambda q: (q, 0, 0))
    outs = pl.pallas_call(
        functools.partial(_s5_prep_kernel, tuple(t_effs)),
        grid=(S5_QT,),
        in_specs=[g3(2, S5_PK), g3(1, S5_PK), g3(S5_P, S5_PK), g3(S5_P, S5_PK), g3(S5_P, S5_PK), g3(S5_P, S5_PK)],
        out_specs=[pl.BlockSpec((1, S5_CHUNK, LANES, LANES), lambda q: (q, 0, 0, 0)),
                   pl.BlockSpec((1, S5_HQ, S5_XW), lambda q: (q, 0, 0))]
                  + [pl.BlockSpec((1, S5_XW, S5_HQ), lambda q: (q, 0, 0))] * n_t
                  + [g3(2, S5_PK)] * n_t,
        out_shape=[jax.ShapeDtypeStruct((S5_QT, S5_CHUNK, LANES, LANES), BF16),
                   jax.ShapeDtypeStruct((S5_QT, S5_HQ, S5_XW), BF16)]
                  + [jax.ShapeDtypeStruct((S5_QT, S5_XW, S5_HQ), BF16)] * n_t
                  + [jax.ShapeDtypeStruct((S5_G, 2, S5_PK), F32)] * n_t,
        compiler_params=_cparams(("parallel",)), name="s5_prep",
    )(a, ldt, b1, b2, c1, c2)
    bd, kc = outs[0], outs[1]
    mats = []
    for k in range(n_t):
        al = outs[2 + n_t + k]
        mats.append((bd, outs[2 + k], kc, al[:, 0, :].reshape(1, S5_HW), al[:, 1, :].reshape(1, S5_HW)))
    return mats


def _s5_e_kernel(u_ref, kb_ref, e_ref):
    e_ref[...] = _bdot(u_ref[0], kb_ref[0])


def _s5_swap(h):
    n = h.shape[-1]
    lane = _iota(h.shape, 1)
    return jnp.where(lane % S5_PK < S5_N, pltpu.roll(h, n - S5_N, 1), pltpu.roll(h, S5_N, 1))


def _s5_scan_kernel(bsz, cg, e_ref, h0_ref, ala_ref, alb_ref, hp_ref, hf_ref, h_scr):
    @pl.when(pl.program_id(0) == 0)
    def _():
        h_scr[...] = h0_ref[...]

    ala = ala_ref[...]
    alb = alb_ref[...]

    def body(c, hs):
        out = []
        for b in range(bsz):
            hp_ref[b, pl.ds(c, 1), :] = hs[b]
            out.append(ala * hs[b] + alb * _s5_swap(hs[b]) + e_ref[b, pl.ds(c, 1), :])
        return tuple(out)

    hs = lax.fori_loop(0, cg, body, tuple(h_scr[b:b + 1, :] for b in range(bsz)))
    for b in range(bsz):
        h_scr[b:b + 1, :] = hs[b]
        hf_ref[b:b + 1, :] = hs[b]


def _s5_step_kernel(e_ref, h0_ref, ala_ref, alb_ref, hf_ref):
    h = h0_ref[...]
    hf_ref[...] = ala_ref[...] * h + alb_ref[...] * _s5_swap(h) + e_ref[...]


def _s5_y_kernel(u_ref, hp_ref, bd_ref, kc_ref, y_ref, k_scr):
    @pl.when(pl.program_id(1) == 0)
    def _():
        zero = jnp.zeros((LANES, LANES), k_scr.dtype)
        for i in range(S5_CHUNK):
            for t in range(S5_CHUNK):
                k_scr[i * LANES:(i + 1) * LANES, t * LANES:(t + 1) * LANES] = bd_ref[0, t - i] if t >= i else zero

    y_ref[0] = _bdot(u_ref[0], k_scr[...]) + _bdot(hp_ref[...], kc_ref[0])


def s5_core(x, h0, bsz, mats):
    bd, kbbig, kcbig, ala, alb = mats
    rows = x.shape[1]
    nc = rows // bsz
    tr = math.gcd(rows, 512)
    e = pl.pallas_call(
        _s5_e_kernel,
        grid=(S5_QT, rows // tr),
        in_specs=[pl.BlockSpec((1, tr, S5_XW), lambda q, r: (q, r, 0)),
                  pl.BlockSpec((1, S5_XW, S5_HQ), lambda q, r: (q, 0, 0))],
        out_specs=pl.BlockSpec((tr, S5_HQ), lambda q, r: (r, q)),
        out_shape=jax.ShapeDtypeStruct((rows, S5_HW), F32),
        compiler_params=_cparams(("parallel", "parallel")), name="s5_e",
    )(x, kbbig)
    row = pl.BlockSpec((1, S5_HW), lambda i: (0, 0))
    if nc == 1:
        hp = h0
        hf = pl.pallas_call(
            _s5_step_kernel,
            out_shape=jax.ShapeDtypeStruct((bsz, S5_HW), F32), name="s5_step",
        )(e, h0, ala, alb)
    else:
        cg = math.gcd(nc, 64)
        hp, hf = pl.pallas_call(
            functools.partial(_s5_scan_kernel, bsz, cg),
            grid=(nc // cg,),
            in_specs=[pl.BlockSpec((bsz, cg, S5_HW), lambda i: (0, i, 0)),
                      pl.BlockSpec((bsz, S5_HW), lambda i: (0, 0)), row, row],
            out_specs=[pl.BlockSpec((bsz, cg, S5_HW), lambda i: (0, i, 0)),
                       pl.BlockSpec((bsz, S5_HW), lambda i: (0, 0))],
            out_shape=[jax.ShapeDtypeStruct((bsz, nc, S5_HW), F32),
                       jax.ShapeDtypeStruct((bsz, S5_HW), F32)],
            scratch_shapes=[pltpu.VMEM((bsz, S5_HW), F32)],
            compiler_params=_cparams(("arbitrary",)), name="s5_scan",
        )(e.reshape(bsz, nc, S5_HW), h0, ala, alb)
    y = pl.pallas_call(
        _s5_y_kernel,
        grid=(S5_QT, rows // tr),
        in_specs=[pl.BlockSpec((1, tr, S5_XW), lambda q, r: (q, r, 0)),
                  pl.BlockSpec((tr, S5_HQ), lambda q, r: (r, q)),
                  pl.BlockSpec((1, S5_CHUNK, LANES, LANES), lambda q, r: (q, 0, 0, 0)),
                  pl.BlockSpec((1, S5_HQ, S5_XW), lambda q, r: (q, 0, 0))],
        out_specs=pl.BlockSpec((1, tr, S5_XW), lambda q, r: (q, r, 0)),
        out_shape=jax.ShapeDtypeStruct((S5_QT, rows, S5_XW), F32),
        scratch_shapes=[pltpu.VMEM((S5_XW, S5_XW), BF16)],
        compiler_params=_cparams(("parallel", "arbitrary")), name="s5_y",
    )(x, hp.reshape(rows, S5_HW), bd, kcbig)
    return y, hf


def _s5_post_kernel(tpr, y_ref, u_ref, d_ref, gw_ref, gb_ref, o_ref, y_scr, u_scr):
    _from_chunk_rows(y_ref, y_scr, tpr)
    _from_chunk_rows(u_ref, u_scr, tpr)
    y = jnp.concatenate([y_scr[q] for q in range(S5_QT)], axis=1)
    u = jnp.concatenate([u_scr[q] for q in range(S5_QT)], axis=1)
    x = y + d_ref[...] * u
    z = 0.5 * x * (1.0 + jnp.tanh(math.sqrt(2.0 / math.pi) * (x + 0.044715 * (x * x * x))))
    o_ref[...] = z * _sigmoid(_dg(z.astype(BF16), gw_ref[...], _NN) + gb_ref[...])


def s5_post(y4, u4, m, tpr, d, glu_w_bf16, glu_b, tm):
    spec4 = pl.BlockSpec((S5_QT, tm // tpr, S5_XW), lambda i: (0, i, 0))
    row = pl.BlockSpec((1, S5_W), lambda i: (0, 0))
    return pl.pallas_call(
        functools.partial(_s5_post_kernel, tpr),
        grid=(m // tm,),
        in_specs=[spec4, spec4, row, pl.BlockSpec((S5_W, S5_W), lambda i: (0, 0)), row],
        out_specs=pl.BlockSpec((tm, S5_W), lambda i: (i, 0)),
        out_shape=jax.ShapeDtypeStruct((m, S5_W), F32),
        scratch_shapes=[pltpu.VMEM((S5_QT, tm, LANES), F32), pltpu.VMEM((S5_QT, tm, LANES), F32)],
        compiler_params=_cparams(("parallel",)), name="s5_post",
    )(y4, u4, d.reshape(1, S5_W), glu_w_bf16, glu_b.reshape(1, S5_W))


def s5_mix(u4, h_re, h_im, bsz, t, mats, d, glu_w_bf16, glu_b, tm):
    tpr = min(t, CHUNK_ROW_TOKENS)
    h0 = jnp.concatenate([h_re, h_im], axis=-1).reshape(bsz, S5_HW)
    y4, hf = s5_core(u4, h0, bsz, mats)
    out = s5_post(y4, u4, bsz * t, tpr, d, glu_w_bf16, glu_b, tm)
    hf = hf.reshape(bsz, S5_G, 2, S5_N)
    return out, hf[:, :, 0], hf[:, :, 1]


GLA_DK = 64
GLA_DV = 128
GLA_HEADS = 4
GLA_KW = GLA_HEADS * GLA_DK
GLA_VW = GLA_HEADS * GLA_DV
GLA_PW = 2 * GLA_KW + 2 * GLA_VW + LANES


def _gla_kernel(L, sub, t_real, nb, p_ref, aup_ref, ab_ref, nw_ref, s0_ref, y_ref, sf_ref, s_scr):
    c = pl.program_id(1)
    nc = pl.num_programs(1)
    n_pairs = GLA_KW // LANES

    @pl.when(c == 0)
    def _():
        for bb in range(nb):
            for j in range(n_pairs):
                s_scr[bb * n_pairs + j] = s0_ref[bb, j]

    lane = _iota((1, LANES), 1)
    masks = (jnp.where(lane < GLA_DK, 1.0, 0.0), jnp.where(lane < GLA_DK, 0.0, 1.0))
    incl = _iota((L, L), 1) <= _iota((L, L), 0)
    diag128 = _iota((LANES, LANES), 0) == _iota((LANES, LANES), 1)
    ones128 = jnp.ones((LANES, LANES), BF16)
    tri = _tri_incl(L)
    rows = range(nb)
    for part in range(sub):
        _gla_chunk(L, t_real, n_pairs, c * sub + part, slice(part * L, (part + 1) * L), rows, masks, incl,
                   diag128, ones128, tri, p_ref, aup_ref, ab_ref, nw_ref, y_ref, s_scr)

    @pl.when(c == nc - 1)
    def _():
        for bb in range(nb):
            for j in range(n_pairs):
                sf_ref[bb, j] = s_scr[bb * n_pairs + j]


def _gla_chunk(L, t_real, n_pairs, chunk, tsl, rows, masks, incl, diag128, ones128, tri,
               p_ref, aup_ref, ab_ref, nw_ref, y_ref, s_scr):
    ps = [p_ref[bb, tsl, :].astype(F32) for bb in rows]
    zs = [_bdot(p[:, 2 * GLA_KW + 2 * GLA_VW:], aup_ref[...]) + ab_ref[...] for p in ps]
    gk = [-_softplus(-z) * (1.0 / GLA_GATE_NORM) for z in zs]
    if t_real % L != 0:
        tok = chunk * L + _iota((L, 1), 0)
        gk = [jnp.where(tok < t_real, x, 0.0) for x in gk]
    b = [_xdot_r(tri, x) for x in gk]
    bl = [x[L - 1:L, :] for x in b]
    qd = [ps[i][:, 0:GLA_KW] * (GLA_DK ** -0.5) * jnp.exp(b[i]) for i in rows]
    kh = [ps[i][:, GLA_KW:2 * GLA_KW] * jnp.exp(-b[i]) for i in rows]
    kt = [ps[i][:, GLA_KW:2 * GLA_KW] * jnp.exp(bl[i] - b[i]) for i in rows]
    heads = [(bb, h) for bb in rows for h in range(GLA_HEADS)]
    sl = lambda h: slice((h // 2) * LANES, (h // 2 + 1) * LANES)
    hs = lambda h: slice(2 * GLA_KW + h * GLA_DV, 2 * GLA_KW + (h + 1) * GLA_DV)
    gs = lambda h: slice(2 * GLA_KW + GLA_VW + h * GLA_DV, 2 * GLA_KW + GLA_VW + (h + 1) * GLA_DV)
    st = [s_scr[bb * n_pairs + j] for bb in rows for j in range(n_pairs)]
    qm = [qd[bb][:, sl(h)] * masks[h % 2] for bb, h in heads]
    vh = [ps[bb][:, hs(h)] for bb, h in heads]
    attn = [jnp.where(incl, _bdot(qm[i], kh[bb][:, sl(h)], _NT), 0.0) for i, (bb, h) in enumerate(heads)]
    o = [_bdot(attn[i], vh[i]) + _bdot(qm[i], st[bb * n_pairs + h // 2]) for i, (bb, h) in enumerate(heads)]
    kv = [_bdot(kt[bb][:, sl(h)], vh[i], _TN) for i, (bb, h) in enumerate(heads)]
    for i, (bb, h) in enumerate(heads):
        of = o[i] * lax.rsqrt(jnp.mean(o[i] * o[i], axis=-1, keepdims=True) + EPS) * nw_ref[...]
        y_ref[bb, tsl, h * GLA_DV:(h + 1) * GLA_DV] = of * _silu(ps[bb][:, gs(h)])
    for bb in rows:
        for j in range(n_pairs):
            i0 = bb * GLA_HEADS + 2 * j
            pcol = _xdot_l(jnp.where(diag128, jnp.exp(bl[bb][:, j * LANES:(j + 1) * LANES]), 0.0), ones128)
            s_scr[bb * n_pairs + j] = pcol * st[bb * n_pairs + j] + jnp.concatenate(
                [kv[i0][0:GLA_DK], kv[i0 + 1][GLA_DK:2 * GLA_DK]], axis=0)


GLA_SUB = 4


def gla_mix(p_gla, s0, bsz, t, L, aup_pad, a_b, norm_w, nb=1):
    sub = math.gcd(-(-t // L), GLA_SUB)
    step = sub * L
    tp = -(-t // L) * L
    p3 = p_gla.reshape(bsz, t, GLA_PW)
    if tp != t:
        p3 = jnp.pad(p3, ((0, 0), (0, tp - t), (0, 0)))
    n_pairs = GLA_KW // LANES
    st_spec = pl.BlockSpec((nb, n_pairs, LANES, LANES), lambda b, c: (b, 0, 0, 0))
    y, s_fin = pl.pallas_call(
        functools.partial(_gla_kernel, L, sub, t, nb),
        grid=(bsz // nb, tp // step),
        in_specs=[pl.BlockSpec((nb, step, GLA_PW), lambda b, c: (b, c, 0)),
                  pl.BlockSpec((LANES, GLA_KW), lambda b, c: (0, 0)),
                  pl.BlockSpec((1, GLA_KW), lambda b, c: (0, 0)),
                  pl.BlockSpec((1, GLA_DV), lambda b, c: (0, 0)),
                  st_spec],
        out_specs=[pl.BlockSpec((nb, step, GLA_VW), lambda b, c: (b, c, 0)), st_spec],
        out_shape=[jax.ShapeDtypeStruct((bsz, tp, GLA_VW), F32),
                   jax.ShapeDtypeStruct((bsz, n_pairs, LANES, LANES), F32)],
        scratch_shapes=[pltpu.VMEM((nb * n_pairs, LANES, LANES), F32)],
        compiler_params=_cparams(("parallel", "arbitrary")), name="gla",
    )(p3, aup_pad, a_b.reshape(1, GLA_KW), norm_w.reshape(1, GLA_DV),
      s0.reshape(bsz, n_pairs, LANES, LANES))
    return y[:, :t].reshape(bsz * t, GLA_VW), s_fin.reshape(bsz, GLA_HEADS, GLA_DK, GLA_DV)


GLA_LR = 16
D_FF_CHUNK = 2816
D_FF_EXPERT_CHUNK = 1792
TOKEN_TILE = 512
PROMPT_CHUNK = 64
PROMPT_ROWS_PER_STEP = 2
SAMPLE_ROWS_PER_STEP = 16
FFN_TOKENS = 512
ROUTER_TOKENS = 1024
MOE_TOKENS = 2048


def _prepare_weights(w):
    win = w["e_w_in"][0]
    w_gla = jnp.pad(win[:, RSHIFT:], ((0, 0), (0, LANES - GLA_LR)))
    wo = w["o_w_in"][0]
    return {
        "e_w_rwkv": win[:, :RSHIFT].astype(BF16),
        "e_w_gla": w_gla.astype(BF16),
        "rwkv": rwkv_params(w, 0),
        "gla_aup": jnp.pad(w["e_gla_a_up"][0], ((0, LANES - GLA_LR), (0, 0))).astype(BF16),
        "e_wo_a": w["e_w_out"][0][:RW].astype(BF16),
        "e_wo_b": w["e_w_out"][0][RW:].astype(BF16),
        "ff_w1": w["e_ff_w1"][0].astype(BF16),
        "ff_w3": w["e_ff_w3"][0].astype(BF16),
        "ff_w2": w["e_ff_w2"][0].astype(BF16),
        "o_w_q": wo[:, :QW].astype(BF16),
        "o_w_kv": wo[:, QW:QW + 2 * KW].astype(BF16),
        "o_w_u": wo[:, QW + 2 * KW:].astype(BF16),
        "glu_w": w["o_glu_w"][0].astype(BF16),
        "o_wo_a": w["o_w_out"][0][:QW].astype(BF16),
        "o_wo_b": w["o_w_out"][0][QW:].astype(BF16),
        "moe_w1": w["o_moe_w1"][0].astype(BF16),
        "moe_w3": w["o_moe_w3"][0].astype(BF16),
        "moe_w2": w["o_moe_w2"][0].astype(BF16),
    }


def _trunk(x3, st, w, pw, tm, chunk, nb, s5_prep_t, prompt_bias):
    bsz, t, d = x3.shape
    m = bsz * t
    x = x3.reshape(m, d)
    p_r, p_g = norm_proj(x, w["e_norm1"][0], [pw["e_w_rwkv"], pw["e_w_gla"]], tm, [F32, BF16])
    ya, s_rwkv = rwkv_mix(p_r, st["shift"], st["rwkv"], bsz, t, chunk, pw["rwkv"],
                          w["e_lnx_w"][0], w["e_lnx_b"][0], nb)
    s_shift = p_r.reshape(bsz, t, RSHIFT)[:, -1]
    yb, s_gla = gla_mix(p_g, st["gla"], bsz, t, chunk, pw["gla_aup"], w["e_gla_a_b"][0], w["e_gla_norm"][0], nb)
    x = ffn(x, ya, yb, pw["e_wo_a"], pw["e_wo_b"], w["e_norm2"][0], pw["ff_w1"], pw["ff_w3"], pw["ff_w2"],
            math.gcd(m, FFN_TOKENS), D_FF_CHUNK)
    q, kv, u = norm_proj_tiles(x, w["o_norm1"][0], [pw["o_w_q"], pw["o_w_kv"], pw["o_w_u"]], tm,
                               min(t, CHUNK_ROW_TOKENS))
    if st["win_k"] is None:
        yc, nk, nv = swa_prompt(q, kv, bsz, t, w["o_q_norm"][0], w["o_k_norm"][0], prompt_bias, w["o_sinks"][0])
    else:
        yc, nk, nv = swa_decode(q, kv, st["win_k"].reshape(bsz, WINDOW, KW), st["win_v"].reshape(bsz, WINDOW, KW),
                                bsz, t, w["o_q_norm"][0], w["o_k_norm"][0], w["rel_table"], w["o_sinks"][0], 8)
    yd, s5r, s5i = s5_mix(u, st["s5_re"], st["s5_im"], bsz, t, s5_prep_t, w["o_d"][0], pw["glu_w"],
                          w["o_glu_b"][0], tm)
    t_router = min(m, ROUTER_TOKENS)
    x, hn, gates, pos, counts = router(x, yc, yd, pw["o_wo_a"], pw["o_wo_b"], w["o_norm2"][0],
                                       w["o_router_w"][0], w["o_router_b"][0], t_router)
    x = moe(x, hn, gates, pos, counts, pw["moe_w1"], pw["moe_w3"], pw["moe_w2"], t_router, min(m, MOE_TOKENS),
            D_FF_EXPERT_CHUNK)
    kv_shape = (bsz, WINDOW, N_KV_HEADS, ATT_HD)
    return (x.reshape(bsz, t, d), s_rwkv[None], s_shift[None], s_gla[None], nk.reshape(kv_shape)[None],
            nv.reshape(kv_shape)[None], s5r[None], s5i[None])


def kernel(x_prompt, x_sample, state_rwkv, state_shift, state_gla, cache_win_k, cache_win_v, state_s5_re,
           state_s5_im, rel_table, e_norm1, e_w_in, e_mu, e_w0, e_w_up, e_a0, e_a_up, e_g_up, e_k_k, e_k_a, e_r_k,
           e_lnx_w, e_lnx_b, e_gla_a_up, e_gla_a_b, e_gla_norm, e_w_out, e_norm2, e_ff_w1, e_ff_w3, e_ff_w2,
           o_norm1, o_w_in, o_q_norm, o_k_norm, o_sinks, o_a_re, o_a_im, o_log_dt, o_b_re, o_b_im, o_c_re, o_c_im,
           o_d, o_glu_w, o_glu_b, o_w_out, o_norm2, o_router_w, o_router_b, o_moe_w1, o_moe_w3, o_moe_w2):
    w = dict(rel_table=rel_table, e_norm1=e_norm1, e_w_in=e_w_in, e_mu=e_mu, e_w0=e_w0, e_w_up=e_w_up, e_a0=e_a0,
             e_a_up=e_a_up, e_g_up=e_g_up, e_k_k=e_k_k, e_k_a=e_k_a, e_r_k=e_r_k, e_lnx_w=e_lnx_w, e_lnx_b=e_lnx_b,
             e_gla_a_up=e_gla_a_up, e_gla_a_b=e_gla_a_b, e_gla_norm=e_gla_norm, e_w_out=e_w_out, e_norm2=e_norm2,
             e_ff_w1=e_ff_w1, e_ff_w3=e_ff_w3, e_ff_w2=e_ff_w2, o_norm1=o_norm1, o_w_in=o_w_in, o_q_norm=o_q_norm,
             o_k_norm=o_k_norm, o_sinks=o_sinks, o_a_re=o_a_re, o_a_im=o_a_im, o_log_dt=o_log_dt, o_b_re=o_b_re,
             o_b_im=o_b_im, o_c_re=o_c_re, o_c_im=o_c_im, o_d=o_d, o_glu_w=o_glu_w, o_glu_b=o_glu_b,
             o_w_out=o_w_out, o_norm2=o_norm2, o_router_w=o_router_w, o_router_b=o_router_b, o_moe_w1=o_moe_w1,
             o_moe_w3=o_moe_w3, o_moe_w2=o_moe_w2)
    pw = _prepare_weights(w)
    bp, tp, _ = x_prompt.shape
    bs, ts, _ = x_sample.shape
    qi = np.arange(WINDOW)[:, None]
    kj = np.arange(2 * WINDOW)[None, :]
    prompt_bias = rel_bias(rel_table, qi + WINDOW - kj)
    zeros = lambda *shape: jnp.zeros(shape, F32)
    st_p = {"rwkv": zeros(bp, RW // RWKV_HD, RWKV_HD, RWKV_HD), "shift": zeros(bp, RSHIFT),
            "gla": zeros(bp, GLA_HEADS, GLA_DK, GLA_DV), "win_k": None, "win_v": None,
            "s5_re": zeros(bp, S5_G, S5_N), "s5_im": zeros(bp, S5_G, S5_N)}
    st_s = {"rwkv": state_rwkv[0], "shift": state_shift[0], "gla": state_gla[0], "win_k": cache_win_k[0],
            "win_v": cache_win_v[0], "s5_re": state_s5_re[0], "s5_im": state_s5_im[0]}
    s5_p, s5_s = s5_prep(w, 0, (S5_CHUNK, ts))
    out_p = _trunk(x_prompt, st_p, w, pw, math.gcd(bp * tp, TOKEN_TILE), PROMPT_CHUNK,
                   math.gcd(bp, PROMPT_ROWS_PER_STEP), s5_p, prompt_bias)
    out_s = _trunk(x_sample, st_s, w, pw, math.gcd(bs * ts, TOKEN_TILE), -(-ts // SUBLANES) * SUBLANES,
                   math.gcd(bs, SAMPLE_ROWS_PER_STEP), s5_s, None)
    res = [out_p[0], out_s[0]]
    for a, b in zip(out_p[1:], out_s[1:]):
        res += [a, b]
    return tuple(res)
```
